```python
import math
import jax, jax.numpy as jnp
from jax import lax
import numpy as np

D_MODEL = 1024
BATCH = 8
SEQ = 8192
DEPTH = 2

N_A = DEPTH // 2
N_B = DEPTH - N_A

H_A = 4
DK_A = D_MODEL // H_A
DV_A = 2 * DK_A
WIDTH_A = H_A * DV_A
CHUNK = 128
ROPE_BASE_A = 10000.0

H_B = 16
QK_NOPE = 128
QK_ROPE = 64
V_HEAD = 128
Q_LORA = 768
KV_LORA = 512
WIDTH_B = H_B * V_HEAD
Q_BLOCK = 128
ROPE_BASE_B = 10000.0

IN_A = H_A * DK_A * 2 + WIDTH_A + WIDTH_A
IN_B = Q_LORA + WIDTH_B

DEEPNORM_ALPHA = (2.0 * DEPTH) ** 0.25
DEEPNORM_BETA = (8.0 * DEPTH) ** -0.25

kernel_name = "yoco_retention_mla_gated_deepnorm"


def _layer_norm(x, g, b, eps=1e-5):
    xf = x.astype(jnp.float32)
    mu = jnp.mean(xf, axis=-1, keepdims=True)
    var = jnp.mean(jnp.square(xf - mu), axis=-1, keepdims=True)
    return ((xf - mu) * lax.rsqrt(var + eps)).astype(x.dtype) * g + b


def _rms_norm(x, g, eps=1e-6):
    xf = x.astype(jnp.float32)
    return (xf * lax.rsqrt(jnp.mean(jnp.square(xf), axis=-1, keepdims=True) + eps)).astype(x.dtype) * g


def _rope(x, base):
    s, d = x.shape[1], x.shape[-1]
    half = d // 2
    inv = base ** (-jnp.arange(half, dtype=jnp.float32) / half)
    ang = jnp.arange(s, dtype=jnp.float32)[:, None] * inv[None, :]
    shp = (s,) + (1,) * (x.ndim - 3) + (half,)
    cos = jnp.cos(ang).reshape(shp).astype(x.dtype)
    sin = jnp.sin(ang).reshape(shp).astype(x.dtype)
    x1, x2 = x[..., :half], x[..., half:]
    return jnp.concatenate([x1 * cos - x2 * sin, x2 * cos + x1 * sin], axis=-1)


def _chunkwise_retention(q, k, v):
    b, s, h, dk = q.shape
    dv = v.shape[-1]
    n = s // CHUNK

    def to_chunks(t):
        return t.astype(jnp.float32).reshape(b, n, CHUNK, h, t.shape[-1]).transpose(1, 0, 3, 2, 4)

    qc, kc, vc = to_chunks(q), to_chunks(k), to_chunks(v)
    lg = jnp.log1p(-jnp.exp2(-5.0 - jnp.arange(h, dtype=jnp.float32)))
    idx = jnp.arange(CHUNK, dtype=jnp.float32)
    diff = idx[:, None] - idx[None, :]
    causal = diff >= 0
    dmat = jnp.where(causal, jnp.exp(jnp.where(causal, diff, 0.0)[None] * lg[:, None, None]), 0.0)
    qdec = jnp.exp((idx + 1.0)[None, :] * lg[:, None])
    kdec = jnp.exp((CHUNK - 1.0 - idx)[None, :] * lg[:, None])
    cdec = jnp.exp(CHUNK * lg)

    def step(state, inp):
        qi, ki, vi = inp
        scores = jnp.einsum('bhcd,bhed->bhce', qi, ki) * dmat
        inner = jnp.einsum('bhce,bhev->bhcv', scores, vi)
        cross = jnp.einsum('bhcd,bhdv->bhcv', qi * qdec[None, :, :, None], state)
        state = state * cdec[None, :, None, None] + jnp.einsum(
            'bhcd,bhcv->bhdv', ki * kdec[None, :, :, None], vi)
        return state, inner + cross

    init = jnp.zeros((b, h, dk, dv), jnp.float32)
    _, out = lax.scan(step, init, (qc, kc, vc))
    return out.transpose(1, 0, 3, 2, 4).reshape(b, s, h, dv)


def _retention_layer(x, w_in, w_out):
    b, s, _ = x.shape
    hproj = x @ w_in
    qk = H_A * DK_A
    q = hproj[..., :qk].reshape(b, s, H_A, DK_A)
    k = hproj[..., qk:2 * qk].reshape(b, s, H_A, DK_A)
    v = hproj[..., 2 * qk:2 * qk + WIDTH_A].reshape(b, s, H_A, DV_A)
    gate = hproj[..., 2 * qk + WIDTH_A:]
    q = _rope(q, ROPE_BASE_A)
    k = _rope(k, ROPE_BASE_A) * (DK_A ** -0.5)
    o = _chunkwise_retention(q, k, v)
    mu = jnp.mean(o, axis=-1, keepdims=True)
    var = jnp.mean(jnp.square(o - mu), axis=-1, keepdims=True)
    o = ((o - mu) * lax.rsqrt(var + 1e-5)).astype(x.dtype).reshape(b, s, WIDTH_A)
    return (o * jax.nn.silu(gate)) @ w_out


def _shared_latent_kv(x, w_down, kv_norm, w_up):
    b, s, _ = x.shape
    c = x @ w_down
    lat = _rms_norm(c[..., :KV_LORA], kv_norm)
    k_rope = _rope(c[..., KV_LORA:], ROPE_BASE_B)
    kv = (lat @ w_up).reshape(b, s, H_B, QK_NOPE + V_HEAD)
    return kv[..., :QK_NOPE], k_rope, kv[..., QK_NOPE:]


def _causal_block_attention(q_nope, q_rope, k_nope, k_rope, v):
    b, s, h, _ = q_nope.shape
    nb = s // Q_BLOCK
    scale = (QK_NOPE + QK_ROPE) ** -0.5
    qn_b = q_nope.reshape(b, nb, Q_BLOCK, h, QK_NOPE).transpose(1, 0, 2, 3, 4)
    qr_b = q_rope.reshape(b, nb, Q_BLOCK, h, QK_ROPE).transpose(1, 0, 2, 3, 4)
    kpos = jnp.arange(s)

    def block(args):
        qn_i, qr_i, i = args
        sc = (jnp.einsum('bqhd,bkhd->bhqk', qn_i, k_nope)
              + jnp.einsum('bqhr,bkr->bhqk', qr_i, k_rope)).astype(jnp.float32) * scale
        qpos = i * Q_BLOCK + jnp.arange(Q_BLOCK)
        sc = jnp.where(kpos[None, :] <= qpos[:, None], sc, -jnp.inf)
        p = jax.nn.softmax(sc, axis=-1).astype(v.dtype)
        return jnp.einsum('bhqk,bkhd->bqhd', p, v)

    out = lax.map(block, (qn_b, qr_b, jnp.arange(nb)))
    return out.transpose(1, 0, 2, 3, 4).reshape(b, s, h, V_HEAD)


def _mla_layer(x, w_in, q_norm, w_uq, w_out, k_nope, k_rope, v):
    b, s, _ = x.shape
    hproj = x @ w_in
    q_lat, gate = hproj[..., :Q_LORA], hproj[..., Q_LORA:]
    q = (_rms_norm(q_lat, q_norm) @ w_uq).reshape(b, s, H_B, QK_NOPE + QK_ROPE)
    q_nope = q[..., :QK_NOPE]
    q_rope = _rope(q[..., QK_NOPE:], ROPE_BASE_B)
    o = _causal_block_attention(q_nope, q_rope, k_nope, k_rope, v).reshape(b, s, WIDTH_B)
    return (o * jax.nn.silu(gate)) @ w_out


def _fwd_setup_inputs(seed: int = 0) -> dict:
    key = jax.random.key(seed)
    ks = jax.random.split(key, 16)
    nrm = jax.random.normal
    f32 = jnp.float32
    return {
        "x": nrm(ks[0], (BATCH, SEQ, D_MODEL), f32),
        "a_w_in": nrm(ks[1], (N_A, D_MODEL, IN_A), f32) * D_MODEL ** -0.5,
        "a_w_out": nrm(ks[2], (N_A, WIDTH_A, D_MODEL), f32) * (WIDTH_A ** -0.5 * DEEPNORM_BETA),
        "b_w_in": nrm(ks[3], (N_B, D_MODEL, IN_B), f32) * D_MODEL ** -0.5,
        "b_q_norm": 1.0 + 0.02 * nrm(ks[4], (N_B, Q_LORA), f32),
        "b_w_uq": nrm(ks[5], (N_B, Q_LORA, H_B * (QK_NOPE + QK_ROPE)), f32) * Q_LORA ** -0.5,
        "b_w_out": nrm(ks[6], (N_B, WIDTH_B, D_MODEL), f32) * (WIDTH_B ** -0.5 * DEEPNORM_BETA),
        "kv_w_down": nrm(ks[7], (D_MODEL, KV_LORA + QK_ROPE), f32) * D_MODEL ** -0.5,
        "kv_norm": 1.0 + 0.02 * nrm(ks[8], (KV_LORA,), f32),
        "kv_w_up": nrm(ks[9], (KV_LORA, H_B * (QK_NOPE + V_HEAD)), f32) * KV_LORA ** -0.5,
        "ln_g": 1.0 + 0.02 * nrm(ks[10], (DEPTH, D_MODEL), f32),
        "ln_b": 0.02 * nrm(ks[11], (DEPTH, D_MODEL), f32),
    }


def _fwd_reference(x, a_w_in, a_w_out, b_w_in, b_q_norm, b_w_uq, b_w_out,
              kv_w_down, kv_norm, kv_w_up, ln_g, ln_b):
    shared = None
    for layer in range(DEPTH):
        if layer < N_A:
            y = _retention_layer(x, a_w_in[layer], a_w_out[layer])
        else:
            j = layer - N_A
            if j == 0:
                shared = _shared_latent_kv(x, kv_w_down, kv_norm, kv_w_up)
            y = _mla_layer(x, b_w_in[j], b_q_norm[j], b_w_uq[j], b_w_out[j], *shared)
        x = _layer_norm(DEEPNORM_ALPHA * x + y, ln_g[layer], ln_b[layer])
    return x


import jax as _jax
import jax.numpy as _jnp

TWIN_FORMAT = 'train_step'
FWD_PARAMS = ['x', 'a_w_in', 'a_w_out', 'b_w_in', 'b_q_norm', 'b_w_uq', 'b_w_out', 'kv_w_down', 'kv_norm', 'kv_w_up', 'ln_g', 'ln_b']
TWIN_WEIGHTS = ['a_w_in', 'a_w_out', 'b_w_in', 'b_q_norm', 'b_w_uq', 'b_w_out', 'kv_w_down', 'kv_norm', 'kv_w_up', 'ln_g', 'ln_b']
TWIN_DIFF_INPUT = 'x'
TWIN_INPUTS = ['x', 'a_w_in', 'a_w_out', 'b_w_in', 'b_q_norm', 'b_w_uq', 'b_w_out', 'kv_w_down', 'kv_norm', 'kv_w_up', 'ln_g', 'ln_b', 'loss_target', 'm_a_w_in', 'm_a_w_out', 'm_b_w_in', 'm_b_q_norm', 'm_b_w_uq', 'm_b_w_out', 'm_kv_w_down', 'm_kv_norm', 'm_kv_w_up', 'm_ln_g', 'm_ln_b', 'v_a_w_in', 'v_a_w_out', 'v_b_w_in', 'v_b_q_norm', 'v_b_w_uq', 'v_b_w_out', 'v_kv_w_down', 'v_kv_norm', 'v_kv_w_up', 'v_ln_g', 'v_ln_b']
TWIN_OUTPUTS = ['loss', 'grad_x', 'grad_a_w_in', 'grad_a_w_out', 'grad_b_w_in', 'grad_b_q_norm', 'grad_b_w_uq', 'grad_b_w_out', 'grad_kv_w_down', 'grad_kv_norm', 'grad_kv_w_up', 'grad_ln_g', 'grad_ln_b', 'delta_a_w_in', 'delta_a_w_out', 'delta_b_w_in', 'delta_b_q_norm', 'delta_b_w_uq', 'delta_b_w_out', 'delta_kv_w_down', 'delta_kv_norm', 'delta_kv_w_up', 'delta_ln_g', 'delta_ln_b', 'new_m_a_w_in', 'new_m_a_w_out', 'new_m_b_w_in', 'new_m_b_q_norm', 'new_m_b_w_uq', 'new_m_b_w_out', 'new_m_kv_w_down', 'new_m_kv_norm', 'new_m_kv_w_up', 'new_m_ln_g', 'new_m_ln_b', 'new_v_a_w_in', 'new_v_a_w_out', 'new_v_b_w_in', 'new_v_b_q_norm', 'new_v_b_w_uq', 'new_v_b_w_out', 'new_v_kv_w_down', 'new_v_kv_norm', 'new_v_kv_w_up', 'new_v_ln_g', 'new_v_ln_b']
TWIN_LEAF_KINDS = {'loss': 'loss', 'grad_x': 'grad_x', 'grad_a_w_in': 'grad_w', 'grad_a_w_out': 'grad_w', 'grad_b_w_in': 'grad_w', 'grad_b_q_norm': 'grad_w', 'grad_b_w_uq': 'grad_w', 'grad_b_w_out': 'grad_w', 'grad_kv_w_down': 'grad_w', 'grad_kv_norm': 'grad_w', 'grad_kv_w_up': 'grad_w', 'grad_ln_g': 'grad_w', 'grad_ln_b': 'grad_w', 'delta_a_w_in': 'delta_w', 'delta_a_w_out': 'delta_w', 'delta_b_w_in': 'delta_w', 'delta_b_q_norm': 'delta_w', 'delta_b_w_uq': 'delta_w', 'delta_b_w_out': 'delta_w', 'delta_kv_w_down': 'delta_w', 'delta_kv_norm': 'delta_w', 'delta_kv_w_up': 'delta_w', 'delta_ln_g': 'delta_w', 'delta_ln_b': 'delta_w', 'new_m_a_w_in': 'new_m', 'new_m_a_w_out': 'new_m', 'new_m_b_w_in': 'new_m', 'new_m_b_q_norm': 'new_m', 'new_m_b_w_uq': 'new_m', 'new_m_b_w_out': 'new_m', 'new_m_kv_w_down': 'new_m', 'new_m_kv_norm': 'new_m', 'new_m_kv_w_up': 'new_m', 'new_m_ln_g': 'new_m', 'new_m_ln_b': 'new_m', 'new_v_a_w_in': 'new_v', 'new_v_a_w_out': 'new_v', 'new_v_b_w_in': 'new_v', 'new_v_b_q_norm': 'new_v', 'new_v_b_w_uq': 'new_v', 'new_v_b_w_out': 'new_v', 'new_v_kv_w_down': 'new_v', 'new_v_kv_norm': 'new_v', 'new_v_kv_w_up': 'new_v', 'new_v_ln_g': 'new_v', 'new_v_ln_b': 'new_v'}


def _forward(args):
    return _fwd_reference(*[args[k] for k in FWD_PARAMS])


def _output_shape():
    def fwd():
        inp = _fwd_setup_inputs(0)
        return _fwd_reference(*[inp[k] for k in FWD_PARAMS])
    out = _jax.eval_shape(fwd)
    return out.shape, out.dtype

N_MICROBATCH = 1
ADAM_LR = 0.001
ADAM_B1 = 0.9
ADAM_B2 = 0.999
ADAM_EPS = 1e-08
ADAM_WD = 0.01
ADAM_STEP = 10
PER_EXAMPLE_BATCH_AXIS = {'x': 0, 'loss_target': 0}
SHARED_INPUTS = []
_WEIGHT_DTYPES = {'a_w_in': _jnp.float32, 'a_w_out': _jnp.float32, 'b_w_in': _jnp.float32, 'b_q_norm': _jnp.float32, 'b_w_uq': _jnp.float32, 'b_w_out': _jnp.float32, 'kv_w_down': _jnp.float32, 'kv_norm': _jnp.float32, 'kv_w_up': _jnp.float32, 'ln_g': _jnp.float32, 'ln_b': _jnp.float32}
MOMENT_SCALE = {'a_w_in': 4.297805e-02, 'a_w_out': 1.035009e-01, 'b_w_in': 9.146297e-03, 'b_q_norm': 1.072607e-02, 'b_w_uq': 5.432794e-03, 'b_w_out': 2.305716e-02, 'kv_w_down': 1.964666e-02, 'kv_norm': 2.010402e-02, 'kv_w_up': 6.967585e-03, 'ln_g': 4.530506e+01, 'ln_b': 1.160520e+00}


def _to_microbatches(a, axis):
    t = _jnp.moveaxis(a, axis, 0)
    t = t.reshape((N_MICROBATCH, t.shape[0] // N_MICROBATCH) + t.shape[1:])
    return _jnp.moveaxis(t, 1, axis + 1)


def setup_inputs(seed: int = 0) -> dict:
    inp = _fwd_setup_inputs(seed)
    key = _jax.random.fold_in(_jax.random.key(seed), 7919)
    shape, _ = _output_shape()
    out = dict(inp)
    out["loss_target"] = _jax.random.normal(_jax.random.fold_in(key, 0), shape, _jnp.float32)
    for i, name in enumerate(TWIN_WEIGHTS):
        w = inp[name].astype(_jnp.float32)
        if MOMENT_SCALE is None:
            s = _jnp.sqrt(_jnp.mean(_jnp.square(w)) + 1e-30)
        else:
            s = MOMENT_SCALE[name]
        km, kv = _jax.random.split(_jax.random.fold_in(key, i + 1))
        out[name] = w
        out["m_" + name] = s * _jax.random.normal(km, w.shape, _jnp.float32)
        out["v_" + name] = (s * s) * _jax.random.uniform(kv, w.shape, _jnp.float32, 0.5, 1.5)
    if N_MICROBATCH > 1:
        for name, axis in PER_EXAMPLE_BATCH_AXIS.items():
            out[name] = _to_microbatches(out[name], axis)
    return {'x': out['x'], 'a_w_in': out['a_w_in'], 'a_w_out': out['a_w_out'], 'b_w_in': out['b_w_in'], 'b_q_norm': out['b_q_norm'], 'b_w_uq': out['b_w_uq'], 'b_w_out': out['b_w_out'], 'kv_w_down': out['kv_w_down'], 'kv_norm': out['kv_norm'], 'kv_w_up': out['kv_w_up'], 'ln_g': out['ln_g'], 'ln_b': out['ln_b'], 'loss_target': out['loss_target'], 'm_a_w_in': out['m_a_w_in'], 'm_a_w_out': out['m_a_w_out'], 'm_b_w_in': out['m_b_w_in'], 'm_b_q_norm': out['m_b_q_norm'], 'm_b_w_uq': out['m_b_w_uq'], 'm_b_w_out': out['m_b_w_out'], 'm_kv_w_down': out['m_kv_w_down'], 'm_kv_norm': out['m_kv_norm'], 'm_kv_w_up': out['m_kv_w_up'], 'm_ln_g': out['m_ln_g'], 'm_ln_b': out['m_ln_b'], 'v_a_w_in': out['v_a_w_in'], 'v_a_w_out': out['v_a_w_out'], 'v_b_w_in': out['v_b_w_in'], 'v_b_q_norm': out['v_b_q_norm'], 'v_b_w_uq': out['v_b_w_uq'], 'v_b_w_out': out['v_b_w_out'], 'v_kv_w_down': out['v_kv_w_down'], 'v_kv_norm': out['v_kv_norm'], 'v_kv_w_up': out['v_kv_w_up'], 'v_ln_g': out['v_ln_g'], 'v_ln_b': out['v_ln_b']}


def _loss(weights, diff, rest, loss_target):
    with _jax.named_scope("forward"):
        args = {**rest, TWIN_DIFF_INPUT: diff, **{k: w.astype(_WEIGHT_DTYPES[k]) for k, w in weights.items()}}
        y = _forward(args)
    with _jax.named_scope("loss_head"):
        err = _jnp.square(y.astype(_jnp.float32) - loss_target)
        return 0.5 * _jnp.sum(_jnp.mean(err, axis=-1)) if err.ndim else 0.5 * err


def _adamw(w, g, m, v):
    m = ADAM_B1 * m + (1.0 - ADAM_B1) * g
    v = ADAM_B2 * v + (1.0 - ADAM_B2) * _jnp.square(g)
    m_hat = m / (1.0 - ADAM_B1 ** ADAM_STEP)
    v_hat = v / (1.0 - ADAM_B2 ** ADAM_STEP)
    delta = -ADAM_LR * (m_hat / (_jnp.sqrt(v_hat) + ADAM_EPS) + ADAM_WD * w)
    return delta, m, v


def reference(x, a_w_in, a_w_out, b_w_in, b_q_norm, b_w_uq, b_w_out, kv_w_down, kv_norm, kv_w_up, ln_g, ln_b, loss_target, m_a_w_in, m_a_w_out, m_b_w_in, m_b_q_norm, m_b_w_uq, m_b_w_out, m_kv_w_down, m_kv_norm, m_kv_w_up, m_ln_g, m_ln_b, v_a_w_in, v_a_w_out, v_b_w_in, v_b_q_norm, v_b_w_uq, v_b_w_out, v_kv_w_down, v_kv_norm, v_kv_w_up, v_ln_g, v_ln_b):
    given = dict(x=x, a_w_in=a_w_in, a_w_out=a_w_out, b_w_in=b_w_in, b_q_norm=b_q_norm, b_w_uq=b_w_uq, b_w_out=b_w_out, kv_w_down=kv_w_down, kv_norm=kv_norm, kv_w_up=kv_w_up, ln_g=ln_g, ln_b=ln_b, loss_target=loss_target, m_a_w_in=m_a_w_in, m_a_w_out=m_a_w_out, m_b_w_in=m_b_w_in, m_b_q_norm=m_b_q_norm, m_b_w_uq=m_b_w_uq, m_b_w_out=m_b_w_out, m_kv_w_down=m_kv_w_down, m_kv_norm=m_kv_norm, m_kv_w_up=m_kv_w_up, m_ln_g=m_ln_g, m_ln_b=m_ln_b, v_a_w_in=v_a_w_in, v_a_w_out=v_a_w_out, v_b_w_in=v_b_w_in, v_b_q_norm=v_b_q_norm, v_b_w_uq=v_b_w_uq, v_b_w_out=v_b_w_out, v_kv_w_down=v_kv_w_down, v_kv_norm=v_kv_norm, v_kv_w_up=v_kv_w_up, v_ln_g=v_ln_g, v_ln_b=v_ln_b)
    weights = {n: given[n] for n in TWIN_WEIGHTS}
    shared = {n: given[n] for n in SHARED_INPUTS}
    per_example = {n: given[n] for n in ['x']}
    grad_fn = _jax.value_and_grad(_loss, argnums=(0, 1))

    def one_microbatch(ex, loss_target):
        ex = dict(ex)
        diff = ex.pop(TWIN_DIFF_INPUT)
        return grad_fn(weights, diff, {**shared, **ex}, loss_target)

    if N_MICROBATCH == 1:
        loss, (grad_w, grad_x) = one_microbatch(per_example, given["loss_target"])
    else:
        def body(carry, xs):
            loss_sum, grad_sum = carry
            l_k, (gw_k, gx_k) = one_microbatch(xs[0], xs[1])
            with _jax.named_scope("update"):
                return (loss_sum + l_k, _jax.tree.map(_jnp.add, grad_sum, gw_k)), gx_k

        init = (_jnp.zeros((), _jnp.float32), _jax.tree.map(_jnp.zeros_like, weights))
        (loss, grad_w), grad_x = _jax.lax.scan(body, init, (per_example, given["loss_target"]))
    with _jax.named_scope("update"):
        delta_w, new_m, new_v = {}, {}, {}
        for n in TWIN_WEIGHTS:
            delta_w[n], new_m[n], new_v[n] = _adamw(weights[n], grad_w[n], given["m_" + n], given["v_" + n])
    return (loss, grad_x, *[grad_w[n] for n in TWIN_WEIGHTS], *[delta_w[n] for n in TWIN_WEIGHTS],
            *[new_m[n] for n in TWIN_WEIGHTS], *[new_v[n] for n in TWIN_WEIGHTS])
```

```python
import functools
import math

import jax
import jax.numpy as jnp
from jax import lax
from jax.experimental import pallas as pl
from jax.experimental.pallas import tpu as pltpu

F32 = jnp.float32
BF16 = jnp.bfloat16
MESH = pl.DeviceIdType.MESH

D_MODEL = 1024
DEPTH = 2
H_A, DK_A, DV_A = 4, 256, 512
WIDTH_A = H_A * DV_A
CHUNK = 128
H_B, QK_NOPE, QK_ROPE, V_HEAD = 16, 128, 64, 128
QK_PAD = 256
Q_LORA, KV_LORA = 768, 512
KV_DOWN_PAD = 640
WIDTH_B = H_B * V_HEAD
IN_A = 2 * H_A * DK_A + 2 * WIDTH_A
IN_B = Q_LORA + WIDTH_B
H1_B = KV_DOWN_PAD + IN_B
ROPE_BASE = 10000.0
ALPHA = (2.0 * DEPTH) ** 0.25
ATT_SCALE = (QK_NOPE + QK_ROPE) ** -0.5
NEG_BIG = -1e30

ADAM_LR, ADAM_B1, ADAM_B2, ADAM_EPS, ADAM_WD, ADAM_STEP = 0.001, 0.9, 0.999, 1e-08, 0.01, 10

VMEM_LIMIT_BYTES = 56 * 1024 * 1024
LANES = 128
FLAT_COLS = 1024
SHARD_ROWS = (("a_w_in", 1536), ("a_w_out", 512), ("b_w_in", 704), ("b_w_uq", 576), ("b_w_out", 512),
              ("kv_w_down", 144), ("kv_w_up", 512))
FLAT_ROWS = 4608
HALF_ROWS = FLAT_ROWS // 2
N_CHIPS = 4
N_DEV = 8


def _params(sem, vmem=VMEM_LIMIT_BYTES):
    return pltpu.CompilerParams(dimension_semantics=sem, vmem_limit_bytes=vmem)


def _row_spec(ts, w, col_block=0):
    return pl.BlockSpec((ts, w), lambda i: (i, col_block))


def _bc_spec(shape):
    nd = len(shape)
    return pl.BlockSpec(shape, lambda i: (0,) * nd)


def _sigmoid(x):
    return 1.0 / (1.0 + jnp.exp(-x))


def _fold8(v):
    ts, w = v.shape
    return jnp.sum(v.reshape(ts // 8, 8, w), axis=0)


def _mm(a, b, *, ta=False, tb=False, out_dtype=F32, tm=512, tn=512, tk=None, name):
    if ta:
        K, M = a.shape
    else:
        M, K = a.shape
    if tb:
        N, Kb = b.shape
    else:
        Kb, N = b.shape
    assert K == Kb, (a.shape, b.shape)
    tm, tn = min(tm, M), min(tn, N)
    tk = K if tk is None else min(tk, K)
    assert M % tm == 0 and N % tn == 0 and K % tk == 0, (name, M, N, K, tm, tn, tk)
    nk = K // tk
    dims = (((0,) if ta else (1,), (1,) if tb else (0,)), ((), ()))

    def body(a_ref, b_ref, o_ref, *scratch):
        prod = lax.dot_general(a_ref[...].astype(BF16), b_ref[...].astype(BF16), dims,
                               preferred_element_type=F32)
        if nk == 1:
            o_ref[...] = prod.astype(o_ref.dtype)
        else:
            acc, = scratch
            k = pl.program_id(2)

            @pl.when(k == 0)
            def _():
                acc[...] = prod

            @pl.when(k > 0)
            def _():
                acc[...] += prod

            @pl.when(k == nk - 1)
            def _():
                o_ref[...] = acc[...].astype(o_ref.dtype)

    a_spec = pl.BlockSpec((tk, tm), lambda i, j, k: (k, i)) if ta else pl.BlockSpec((tm, tk), lambda i, j, k: (i, k))
    b_spec = pl.BlockSpec((tn, tk), lambda i, j, k: (j, k)) if tb else pl.BlockSpec((tk, tn), lambda i, j, k: (k, j))
    return pl.pallas_call(
        body, name=name,
        grid=(M // tm, N // tn, nk),
        in_specs=[a_spec, b_spec],
        out_specs=pl.BlockSpec((tm, tn), lambda i, j, k: (i, j)),
        out_shape=jax.ShapeDtypeStruct((M, N), out_dtype),
        scratch_shapes=[] if nk == 1 else [pltpu.VMEM((tm, tn), F32)],
        compiler_params=_params(("parallel", "parallel", "arbitrary")),
    )(a, b)


def _rope_tables_a(s):
    half = DK_A // 2
    inv = ROPE_BASE ** (-jnp.arange(half, dtype=F32) / half)
    ang = jnp.arange(s, dtype=F32)[:, None] * inv[None, :]
    return jnp.cos(ang), jnp.sin(ang)


def _rope_tables_b(s):
    half = QK_ROPE // 2
    inv = ROPE_BASE ** (-jnp.arange(half, dtype=F32) / half)
    ang = jnp.arange(s, dtype=F32)[:, None] * inv[None, :]
    c, sn = jnp.cos(ang), jnp.sin(ang)
    z = jnp.zeros_like(c)
    cos = jnp.concatenate([c, c, z, z], axis=1)
    sa = jnp.concatenate([-sn, z, z, z], axis=1)
    sb = jnp.concatenate([z, sn, z, z], axis=1)
    return cos, sa, sb


def _rope_b(r, cos, sa, sb, sign):
    return r * cos + sign * (pltpu.roll(r, 96, 1) * sa + pltpu.roll(r, 32, 1) * sb)


def _retention_tables():
    lg = jnp.log1p(-jnp.exp2(-5.0 - jnp.arange(H_A, dtype=F32)))
    idx = jnp.arange(CHUNK, dtype=F32)
    diff = idx[:, None] - idx[None, :]
    causal = diff >= 0
    dmat = jnp.where(causal, jnp.exp(jnp.where(causal, diff, 0.0)[None] * lg[:, None, None]), 0.0)
    qdec = jnp.exp((idx + 1.0)[None, :] * lg[:, None])[:, :, None]
    kdec = jnp.exp((CHUNK - 1.0 - idx)[None, :] * lg[:, None])[:, :, None]
    cdec = jnp.broadcast_to(jnp.exp(CHUNK * lg)[:, None, None], (H_A, 1, DV_A))
    return dmat, qdec, kdec, cdec


def _rope_a_fwd(h_a, cos, sin, *, ts):
    s = h_a.shape[0]

    def body(h_ref, c_ref, s_ref, q_ref, k_ref, v_ref):
        c, sn = c_ref[...], s_ref[...]
        for o_ref, base, scale in ((q_ref, 0, 1.0), (k_ref, H_A * DK_A, DK_A ** -0.5)):
            for hd in range(H_A):
                lo = hd * DK_A
                x1 = h_ref[:, base + lo:base + lo + 128]
                x2 = h_ref[:, base + lo + 128:base + lo + 256]
                o_ref[:, lo:lo + 128] = ((x1 * c - x2 * sn) * scale).astype(o_ref.dtype)
                o_ref[:, lo + 128:lo + 256] = ((x2 * c + x1 * sn) * scale).astype(o_ref.dtype)
        v_ref[...] = h_ref[:, 2 * H_A * DK_A:2 * H_A * DK_A + WIDTH_A].astype(v_ref.dtype)

    return pl.pallas_call(
        body, name="rope_a_fwd", grid=(s // ts,),
        in_specs=[_row_spec(ts, 4096), _row_spec(ts, 128), _row_spec(ts, 128)],
        out_specs=[_row_spec(ts, 1024), _row_spec(ts, 1024), _row_spec(ts, 2048)],
        out_shape=[jax.ShapeDtypeStruct((s, 1024), BF16), jax.ShapeDtypeStruct((s, 1024), BF16),
                   jax.ShapeDtypeStruct((s, 2048), BF16)],
        compiler_params=_params(("parallel",)),
    )(h_a, cos, sin)


def _rope_a_bwd(dq, dk, dv16, dg16, cos, sin, *, ts):
    s = dq.shape[0]

    def body(dq_ref, dk_ref, dv_ref, dg_ref, c_ref, s_ref, o_ref):
        c, sn = c_ref[...], s_ref[...]
        for d_ref, base, scale in ((dq_ref, 0, 1.0), (dk_ref, H_A * DK_A, DK_A ** -0.5)):
            for hd in range(H_A):
                lo = hd * DK_A
                d1 = d_ref[:, lo:lo + 128] * scale
                d2 = d_ref[:, lo + 128:lo + 256] * scale
                o_ref[:, base + lo:base + lo + 128] = (d1 * c + d2 * sn).astype(o_ref.dtype)
                o_ref[:, base + lo + 128:base + lo + 256] = (d2 * c - d1 * sn).astype(o_ref.dtype)
        o_ref[:, 2048:4096] = dv_ref[...]
        o_ref[:, 4096:6144] = dg_ref[...]

    return pl.pallas_call(
        body, name="rope_a_bwd", grid=(s // ts,),
        in_specs=[_row_spec(ts, 1024), _row_spec(ts, 1024), _row_spec(ts, 2048), _row_spec(ts, 2048),
                  _row_spec(ts, 128), _row_spec(ts, 128)],
        out_specs=_row_spec(ts, IN_A),
        out_shape=jax.ShapeDtypeStruct((s, IN_A), BF16),
        compiler_params=_params(("parallel",)),
    )(dq, dk, dv16, dg16, cos, sin)


def _gn_gate_fwd(o, h_a, *, ts):
    s = o.shape[0]

    def body(o_ref, g_ref, u_ref):
        for hd in range(H_A):
            sl = slice(hd * DV_A, (hd + 1) * DV_A)
            ov = o_ref[:, sl]
            mu = jnp.mean(ov, axis=-1, keepdims=True)
            oc = ov - mu
            var = jnp.mean(oc * oc, axis=-1, keepdims=True)
            on = oc * lax.rsqrt(var + 1e-5)
            g = g_ref[:, sl]
            u_ref[:, sl] = (on * (g * _sigmoid(g))).astype(u_ref.dtype)

    return pl.pallas_call(
        body, name="gn_gate_fwd", grid=(s // ts,),
        in_specs=[_row_spec(ts, WIDTH_A), _row_spec(ts, WIDTH_A, 2)],
        out_specs=_row_spec(ts, WIDTH_A),
        out_shape=jax.ShapeDtypeStruct((s, WIDTH_A), BF16),
        compiler_params=_params(("parallel",)),
    )(o, h_a)


def _gn_gate_bwd(du, o, h_a, *, ts):
    s = o.shape[0]

    def body(du_ref, o_ref, g_ref, do_ref, dg_ref):
        for hd in range(H_A):
            sl = slice(hd * DV_A, (hd + 1) * DV_A)
            ov = o_ref[:, sl]
            mu = jnp.mean(ov, axis=-1, keepdims=True)
            oc = ov - mu
            var = jnp.mean(oc * oc, axis=-1, keepdims=True)
            rstd = lax.rsqrt(var + 1e-5)
            on = oc * rstd
            g = g_ref[:, sl]
            sg = _sigmoid(g)
            du = du_ref[:, sl]
            don = du * (g * sg)
            dg_ref[:, sl] = (du * on * (sg * (1.0 + g * (1.0 - sg)))).astype(dg_ref.dtype)
            m1 = jnp.mean(don, axis=-1, keepdims=True)
            m2 = jnp.mean(don * on, axis=-1, keepdims=True)
            do_ref[:, sl] = (rstd * (don - m1 - on * m2)).astype(do_ref.dtype)

    return pl.pallas_call(
        body, name="gn_gate_bwd", grid=(s // ts,),
        in_specs=[_row_spec(ts, WIDTH_A), _row_spec(ts, WIDTH_A), _row_spec(ts, WIDTH_A, 2)],
        out_specs=[_row_spec(ts, WIDTH_A), _row_spec(ts, WIDTH_A)],
        out_shape=[jax.ShapeDtypeStruct((s, WIDTH_A), BF16), jax.ShapeDtypeStruct((s, WIDTH_A), BF16)],
        compiler_params=_params(("parallel",)),
    )(du, o, h_a)


def _ln_stats(z):
    mu = jnp.mean(z, axis=-1, keepdims=True)
    zc = z - mu
    var = jnp.mean(zc * zc, axis=-1, keepdims=True)
    rstd = lax.rsqrt(var + 1e-5)
    return zc * rstd, rstd


def _ln_bwd(dy, xhat, rstd, g):
    dxh = dy * g
    m1 = jnp.mean(dxh, axis=-1, keepdims=True)
    m2 = jnp.mean(dxh * xhat, axis=-1, keepdims=True)
    return rstd * (dxh - m1 - xhat * m2)


def _ln_fwd(x, y, g, b, *, ts):
    s = x.shape[0]

    def body(x_ref, y_ref, g_ref, b_ref, o_ref, o16_ref):
        xhat, _ = _ln_stats(ALPHA * x_ref[...] + y_ref[...])
        out = xhat * g_ref[...] + b_ref[...]
        o_ref[...] = out
        o16_ref[...] = out.astype(o16_ref.dtype)

    return pl.pallas_call(
        body, name="ln_fwd", grid=(s // ts,),
        in_specs=[_row_spec(ts, D_MODEL), _row_spec(ts, D_MODEL), _bc_spec((1, D_MODEL)), _bc_spec((1, D_MODEL))],
        out_specs=[_row_spec(ts, D_MODEL), _row_spec(ts, D_MODEL)],
        out_shape=[jax.ShapeDtypeStruct((s, D_MODEL), F32), jax.ShapeDtypeStruct((s, D_MODEL), BF16)],
        compiler_params=_params(("parallel",)),
    )(x, y, g, b)


def _ln_loss_bwd(x1, y, target, g, b, *, ts):
    s = x1.shape[0]
    n = s // ts

    def body(x_ref, y_ref, t_ref, g_ref, b_ref, dz_ref, dz16_ref, dg_ref, db_ref, loss_ref, ag, ab, al):
        i = pl.program_id(0)

        @pl.when(i == 0)
        def _():
            ag[...] = jnp.zeros_like(ag)
            ab[...] = jnp.zeros_like(ab)
            al[...] = jnp.zeros_like(al)

        xhat, rstd = _ln_stats(ALPHA * x_ref[...] + y_ref[...])
        err = xhat * g_ref[...] + b_ref[...] - t_ref[...]
        al[...] += _fold8(err * err)
        dy = err * (1.0 / D_MODEL)
        ag[...] += _fold8(dy * xhat)
        ab[...] += _fold8(dy)
        dz = _ln_bwd(dy, xhat, rstd, g_ref[...])
        dz_ref[...] = dz
        dz16_ref[...] = dz.astype(dz16_ref.dtype)

        @pl.when(i == n - 1)
        def _():
            dg_ref[...] = jnp.sum(ag[...], axis=0, keepdims=True)
            db_ref[...] = jnp.sum(ab[...], axis=0, keepdims=True)
            loss_ref[...] = jnp.full((1, LANES), (0.5 / D_MODEL) * jnp.sum(al[...]), F32)

    return pl.pallas_call(
        body, name="ln_loss_bwd", grid=(n,),
        in_specs=[_row_spec(ts, D_MODEL)] * 3 + [_bc_spec((1, D_MODEL))] * 2,
        out_specs=[_row_spec(ts, D_MODEL), _row_spec(ts, D_MODEL), _bc_spec((1, D_MODEL)), _bc_spec((1, D_MODEL)),
                   _bc_spec((1, LANES))],
        out_shape=[jax.ShapeDtypeStruct((s, D_MODEL), F32), jax.ShapeDtypeStruct((s, D_MODEL), BF16),
                   jax.ShapeDtypeStruct((1, D_MODEL), F32), jax.ShapeDtypeStruct((1, D_MODEL), F32),
                   jax.ShapeDtypeStruct((1, LANES), F32)],
        scratch_shapes=[pltpu.VMEM((8, D_MODEL), F32)] * 3,
        compiler_params=_params(("arbitrary",)),
    )(x1, y, target, g, b)


def _ln_bwd_call(dz_next, dx_branch, x, y, g, *, ts):
    s = x.shape[0]
    n = s // ts

    def body(dzn_ref, dxb_ref, x_ref, y_ref, g_ref, dz_ref, dz16_ref, dg_ref, db_ref, ag, ab):
        i = pl.program_id(0)

        @pl.when(i == 0)
        def _():
            ag[...] = jnp.zeros_like(ag)
            ab[...] = jnp.zeros_like(ab)

        xhat, rstd = _ln_stats(ALPHA * x_ref[...] + y_ref[...])
        dy = ALPHA * dzn_ref[...] + dxb_ref[...]
        ag[...] += _fold8(dy * xhat)
        ab[...] += _fold8(dy)
        dz = _ln_bwd(dy, xhat, rstd, g_ref[...])
        dz_ref[...] = dz
        dz16_ref[...] = dz.astype(dz16_ref.dtype)

        @pl.when(i == n - 1)
        def _():
            dg_ref[...] = jnp.sum(ag[...], axis=0, keepdims=True)
            db_ref[...] = jnp.sum(ab[...], axis=0, keepdims=True)

    return pl.pallas_call(
        body, name="ln_bwd", grid=(n,),
        in_specs=[_row_spec(ts, D_MODEL)] * 4 + [_bc_spec((1, D_MODEL))],
        out_specs=[_row_spec(ts, D_MODEL), _row_spec(ts, D_MODEL), _bc_spec((1, D_MODEL)), _bc_spec((1, D_MODEL))],
        out_shape=[jax.ShapeDtypeStruct((s, D_MODEL), F32), jax.ShapeDtypeStruct((s, D_MODEL), BF16),
                   jax.ShapeDtypeStruct((1, D_MODEL), F32), jax.ShapeDtypeStruct((1, D_MODEL), F32)],
        scratch_shapes=[pltpu.VMEM((8, D_MODEL), F32)] * 2,
        compiler_params=_params(("arbitrary",)),
    )(dz_next, dx_branch, x, y, g)


def _residual_grad(dz, dx_branch, *, ts):
    s = dz.shape[0]

    def body(dz_ref, dxb_ref, o_ref):
        o_ref[...] = ALPHA * dz_ref[...] + dxb_ref[...]

    return pl.pallas_call(
        body, name="residual_grad", grid=(s // ts,),
        in_specs=[_row_spec(ts, D_MODEL)] * 2, out_specs=_row_spec(ts, D_MODEL),
        out_shape=jax.ShapeDtypeStruct((s, D_MODEL), F32),
        compiler_params=_params(("parallel",)),
    )(dz, dx_branch)


C_LAT = slice(0, KV_LORA)
C_ROPE = slice(KV_LORA, KV_DOWN_PAD)
C_QL = slice(KV_DOWN_PAD, KV_DOWN_PAD + Q_LORA)
C_GATE = slice(KV_DOWN_PAD + Q_LORA, H1_B)


def _rms(x, eps=1e-6):
    r = lax.rsqrt(jnp.mean(x * x, axis=-1, keepdims=True) + eps)
    return x * r, r


def _rms_bwd(dy, xhat, r, g):
    dxh = dy * g
    return r * (dxh - xhat * jnp.mean(dxh * xhat, axis=-1, keepdims=True))


def _kvq_prep(h1, kv_norm, q_norm, cos, sa, sb, *, ts):
    s = h1.shape[0]

    def body(h_ref, kn_ref, qn_ref, c_ref, sa_ref, sb_ref, lat_ref, kr_ref, ql_ref):
        lat, _ = _rms(h_ref[:, C_LAT])
        lat_ref[...] = (lat * kn_ref[...]).astype(lat_ref.dtype)
        kr_ref[...] = _rope_b(h_ref[:, C_ROPE], c_ref[...], sa_ref[...], sb_ref[...], 1.0).astype(kr_ref.dtype)
        ql, _ = _rms(h_ref[:, C_QL])
        ql_ref[...] = (ql * qn_ref[...]).astype(ql_ref.dtype)

    return pl.pallas_call(
        body, name="kvq_prep", grid=(s // ts,),
        in_specs=[_row_spec(ts, H1_B), _bc_spec((1, KV_LORA)), _bc_spec((1, Q_LORA))] + [_row_spec(ts, 128)] * 3,
        out_specs=[_row_spec(ts, KV_LORA), _row_spec(ts, 128), _row_spec(ts, Q_LORA)],
        out_shape=[jax.ShapeDtypeStruct((s, KV_LORA), BF16), jax.ShapeDtypeStruct((s, 128), BF16),
                   jax.ShapeDtypeStruct((s, Q_LORA), BF16)],
        compiler_params=_params(("parallel",)),
    )(h1, kv_norm, q_norm, cos, sa, sb)


def _rope_q(qf, cos, sa, sb, sign, out_dtype, *, ts, name):
    s = qf.shape[0]

    def body(q_ref, c_ref, sa_ref, sb_ref, o_ref):
        c, a, b = c_ref[...], sa_ref[...], sb_ref[...]
        for hd in range(H_B):
            lo = hd * QK_PAD
            o_ref[:, lo:lo + 128] = q_ref[:, lo:lo + 128].astype(o_ref.dtype)
            o_ref[:, lo + 128:lo + 256] = _rope_b(q_ref[:, lo + 128:lo + 256], c, a, b, sign).astype(o_ref.dtype)

    return pl.pallas_call(
        body, name=name, grid=(s // ts,),
        in_specs=[_row_spec(ts, H_B * QK_PAD)] + [_row_spec(ts, 128)] * 3,
        out_specs=_row_spec(ts, H_B * QK_PAD),
        out_shape=jax.ShapeDtypeStruct((s, H_B * QK_PAD), out_dtype),
        compiler_params=_params(("parallel",)),
    )(qf, cos, sa, sb)


def _gate_b_fwd(o, h1, *, ts):
    s = o.shape[0]

    def body(o_ref, h_ref, u_ref):
        g = h_ref[:, C_GATE]
        u_ref[...] = (o_ref[...] * (g * _sigmoid(g))).astype(u_ref.dtype)

    return pl.pallas_call(
        body, name="gate_b_fwd", grid=(s // ts,),
        in_specs=[_row_spec(ts, WIDTH_B), _row_spec(ts, H1_B)],
        out_specs=_row_spec(ts, WIDTH_B),
        out_shape=jax.ShapeDtypeStruct((s, WIDTH_B), BF16),
        compiler_params=_params(("parallel",)),
    )(o, h1)


def _gate_b_bwd(du, o, h1, *, ts):
    s = o.shape[0]

    def body(du_ref, o_ref, h_ref, do_ref, dg_ref, dl_ref):
        g = h_ref[:, C_GATE]
        sg = _sigmoid(g)
        du = du_ref[...]
        ov = o_ref[...]
        do = du * (g * sg)
        do_ref[...] = do.astype(do_ref.dtype)
        dg_ref[...] = (du * ov * (sg * (1.0 + g * (1.0 - sg)))).astype(dg_ref.dtype)
        prod = do * ov
        for hd in range(H_B):
            dl_ref[hd] = jnp.sum(prod[:, hd * V_HEAD:(hd + 1) * V_HEAD], axis=-1, keepdims=True)

    return pl.pallas_call(
        body, name="gate_b_bwd", grid=(s // ts,),
        in_specs=[_row_spec(ts, WIDTH_B), _row_spec(ts, WIDTH_B), _row_spec(ts, H1_B)],
        out_specs=[_row_spec(ts, WIDTH_B), _row_spec(ts, WIDTH_B), pl.BlockSpec((H_B, ts, 1), lambda i: (0, i, 0))],
        out_shape=[jax.ShapeDtypeStruct((s, WIDTH_B), BF16), jax.ShapeDtypeStruct((s, WIDTH_B), BF16),
                   jax.ShapeDtypeStruct((H_B, s, 1), F32)],
        compiler_params=_params(("parallel",)),
    )(du, o, h1)


def _h1_bwd(h1, dlat_k, dlat_v, dkr_heads, dqn, dg16, kv_norm, q_norm, cos, sa, sb, *, ts):
    s = h1.shape[0]
    n = s // ts

    def body(h_ref, dk_ref, dv_ref, dkr_ref, dqn_ref, dg_ref, kn_ref, qn_ref, c_ref, sa_ref, sb_ref,
             o_ref, dkn_ref, dqn_out_ref, akn, aqn):
        i = pl.program_id(0)

        @pl.when(i == 0)
        def _():
            akn[...] = jnp.zeros_like(akn)
            aqn[...] = jnp.zeros_like(aqn)

        lat, r = _rms(h_ref[:, C_LAT])
        dlat = dk_ref[...] + dv_ref[...]
        akn[...] += _fold8(dlat * lat)
        o_ref[:, C_LAT] = _rms_bwd(dlat, lat, r, kn_ref[...]).astype(o_ref.dtype)

        dkr = dkr_ref[:, 0:128]
        for hd in range(1, H_B):
            dkr = dkr + dkr_ref[:, hd * 128:(hd + 1) * 128]
        o_ref[:, C_ROPE] = _rope_b(dkr, c_ref[...], sa_ref[...], sb_ref[...], -1.0).astype(o_ref.dtype)

        ql, rq = _rms(h_ref[:, C_QL])
        dq = dqn_ref[...]
        aqn[...] += _fold8(dq * ql)
        o_ref[:, C_QL] = _rms_bwd(dq, ql, rq, qn_ref[...]).astype(o_ref.dtype)
        o_ref[:, C_GATE] = dg_ref[...]

        @pl.when(i == n - 1)
        def _():
            dkn_ref[...] = jnp.sum(akn[...], axis=0, keepdims=True)
            dqn_out_ref[...] = jnp.sum(aqn[...], axis=0, keepdims=True)

    return pl.pallas_call(
        body, name="h1_bwd", grid=(n,),
        in_specs=[_row_spec(ts, H1_B), _row_spec(ts, KV_LORA), _row_spec(ts, KV_LORA), _row_spec(ts, H_B * 128),
                  _row_spec(ts, Q_LORA), _row_spec(ts, WIDTH_B), _bc_spec((1, KV_LORA)), _bc_spec((1, Q_LORA))]
        + [_row_spec(ts, 128)] * 3,
        out_specs=[_row_spec(ts, H1_B), _bc_spec((1, KV_LORA)), _bc_spec((1, Q_LORA))],
        out_shape=[jax.ShapeDtypeStruct((s, H1_B), BF16), jax.ShapeDtypeStruct((1, KV_LORA), F32),
                   jax.ShapeDtypeStruct((1, Q_LORA), F32)],
        scratch_shapes=[pltpu.VMEM((8, KV_LORA), F32), pltpu.VMEM((8, Q_LORA), F32)],
        compiler_params=_params(("arbitrary",)),
    )(h1, dlat_k, dlat_v, dkr_heads, dqn, dg16, kv_norm, q_norm, cos, sa, sb)


def _dot(a, b, ca, cb):
    return lax.dot_general(a, b, (((ca,), (cb,)), ((), ())), preferred_element_type=F32)


def _retention_fwd(q, k, v, tables):
    s = q.shape[0]
    n = s // CHUNK
    dmat, qdec, kdec, cdec = tables

    def body(q_ref, k_ref, v_ref, dm_ref, qd_ref, kd_ref, cd_ref, o_ref, st_ref, state):
        i = pl.program_id(1)

        @pl.when(i == 0)
        def _():
            state[...] = jnp.zeros_like(state)

        qv, kv, vv = q_ref[...], k_ref[...], v_ref[...]
        st16 = state[...].astype(BF16)
        st_ref[...] = st16
        scores = _dot(qv, kv, 1, 1) * dm_ref[0]
        inner = _dot(scores.astype(BF16), vv, 1, 0)
        qd = (qv.astype(F32) * qd_ref[0]).astype(BF16)
        cross = _dot(qd, st16, 1, 0)
        o_ref[...] = inner + cross
        kd = (kv.astype(F32) * kd_ref[0]).astype(BF16)
        state[...] = state[...] * cd_ref[0] + _dot(kd, vv, 0, 0)

    return pl.pallas_call(
        body, name="retention_fwd", grid=(H_A, n),
        in_specs=[pl.BlockSpec((CHUNK, DK_A), lambda h, i: (i, h)), pl.BlockSpec((CHUNK, DK_A), lambda h, i: (i, h)),
                  pl.BlockSpec((CHUNK, DV_A), lambda h, i: (i, h)),
                  pl.BlockSpec((1, CHUNK, CHUNK), lambda h, i: (h, 0, 0)), pl.BlockSpec((1, CHUNK, 1), lambda h, i: (h, 0, 0)),
                  pl.BlockSpec((1, CHUNK, 1), lambda h, i: (h, 0, 0)), pl.BlockSpec((1, 1, DV_A), lambda h, i: (h, 0, 0))],
        out_specs=[pl.BlockSpec((CHUNK, DV_A), lambda h, i: (i, h)),
                   pl.BlockSpec((DK_A, DV_A), lambda h, i: (h * n + i, 0))],
        out_shape=[jax.ShapeDtypeStruct((s, WIDTH_A), F32), jax.ShapeDtypeStruct((H_A * n * DK_A, DV_A), BF16)],
        scratch_shapes=[pltpu.VMEM((DK_A, DV_A), F32)],
        compiler_params=_params(("parallel", "arbitrary")),
    )(q, k, v, dmat, qdec, kdec, cdec)


def _retention_bwd(q, k, v, states, do, tables):
    s = q.shape[0]
    n = s // CHUNK
    dmat, qdec, kdec, cdec = tables

    def body(q_ref, k_ref, v_ref, st_ref, do_ref, dm_ref, qd_ref, kd_ref, cd_ref, dq_ref, dk_ref, dv_ref, grad_state):
        i = pl.program_id(1)

        @pl.when(i == 0)
        def _():
            grad_state[...] = jnp.zeros_like(grad_state)

        qv, kv, vv, dov = q_ref[...], k_ref[...], v_ref[...], do_ref[...]
        dm = dm_ref[0]
        g16 = grad_state[...].astype(BF16)
        scores = (_dot(qv, kv, 1, 1) * dm).astype(BF16)
        dscores = (_dot(dov, vv, 1, 1) * dm).astype(BF16)
        qd = (qv.astype(F32) * qd_ref[0]).astype(BF16)
        kd = (kv.astype(F32) * kd_ref[0]).astype(BF16)
        dq_ref[...] = _dot(dscores, kv, 1, 0) + _dot(dov, st_ref[...], 1, 1) * qd_ref[0]
        dk_ref[...] = _dot(dscores, qv, 0, 0) + _dot(vv, g16, 1, 1) * kd_ref[0]
        dv_ref[...] = (_dot(scores, dov, 0, 0) + _dot(kd, g16, 1, 0)).astype(dv_ref.dtype)
        grad_state[...] = grad_state[...] * cd_ref[0] + _dot(qd, dov, 0, 0)

    rev = lambda h, i: (n - 1 - i, h)
    return pl.pallas_call(
        body, name="retention_bwd", grid=(H_A, n),
        in_specs=[pl.BlockSpec((CHUNK, DK_A), rev), pl.BlockSpec((CHUNK, DK_A), rev), pl.BlockSpec((CHUNK, DV_A), rev),
                  pl.BlockSpec((DK_A, DV_A), lambda h, i: (h * n + n - 1 - i, 0)), pl.BlockSpec((CHUNK, DV_A), rev),
                  pl.BlockSpec((1, CHUNK, CHUNK), lambda h, i: (h, 0, 0)), pl.BlockSpec((1, CHUNK, 1), lambda h, i: (h, 0, 0)),
                  pl.BlockSpec((1, CHUNK, 1), lambda h, i: (h, 0, 0)), pl.BlockSpec((1, 1, DV_A), lambda h, i: (h, 0, 0))],
        out_specs=[pl.BlockSpec((CHUNK, DK_A), rev), pl.BlockSpec((CHUNK, DK_A), rev), pl.BlockSpec((CHUNK, DV_A), rev)],
        out_shape=[jax.ShapeDtypeStruct((s, H_A * DK_A), F32), jax.ShapeDtypeStruct((s, H_A * DK_A), F32),
                   jax.ShapeDtypeStruct((s, WIDTH_A), BF16)],
        scratch_shapes=[pltpu.VMEM((DK_A, DV_A), F32)],
        compiler_params=_params(("parallel", "arbitrary")),
    )(q, k, v, states, do, dmat, qdec, kdec, cdec)


def _causal_mask(s, bq, bk):
    row = lax.broadcasted_iota(jnp.int32, (bq, bk), 0)
    col = lax.broadcasted_iota(jnp.int32, (bq, bk), 1)
    return jnp.where(col <= row, s, NEG_BIG)


def _attention_fwd(q, kn, kr, v, *, blk):
    s = q.shape[0]
    nb = s // blk

    def body(q_ref, kn_ref, kr_ref, v_ref, o_ref, lse_ref, m_s, l_s, acc_s):
        i, j = pl.program_id(1), pl.program_id(2)

        @pl.when(j == 0)
        def _():
            m_s[...] = jnp.full_like(m_s, NEG_BIG)
            l_s[...] = jnp.zeros_like(l_s)
            acc_s[...] = jnp.zeros_like(acc_s)

        def step(masked):
            kcat = jnp.concatenate([kn_ref[...], kr_ref[...]], axis=1)
            sc = _dot(q_ref[...], kcat, 1, 1) * ATT_SCALE
            if masked:
                sc = _causal_mask(sc, blk, blk)
            m_prev = m_s[...]
            m_new = jnp.maximum(m_prev, jnp.max(sc, axis=-1, keepdims=True))
            p = jnp.exp(sc - m_new)
            a = jnp.exp(m_prev - m_new)
            l_s[...] = a * l_s[...] + jnp.sum(p, axis=-1, keepdims=True)
            acc_s[...] = a * acc_s[...] + _dot(p.astype(BF16), v_ref[...], 1, 0)
            m_s[...] = m_new

        @pl.when(j < i)
        def _():
            step(False)

        @pl.when(j == i)
        def _():
            step(True)
            o_ref[...] = acc_s[...] / l_s[...]
            lse_ref[0] = m_s[...] + jnp.log(l_s[...])

    kv_idx = lambda h, i, j: (jnp.minimum(j, i), h)
    return pl.pallas_call(
        body, name="attention_fwd", grid=(H_B, nb, nb),
        in_specs=[pl.BlockSpec((blk, QK_PAD), lambda h, i, j: (i, h)), pl.BlockSpec((blk, 128), kv_idx),
                  pl.BlockSpec((blk, 128), lambda h, i, j: (jnp.minimum(j, i), 0)), pl.BlockSpec((blk, 128), kv_idx)],
        out_specs=[pl.BlockSpec((blk, V_HEAD), lambda h, i, j: (i, h)),
                   pl.BlockSpec((1, blk, 1), lambda h, i, j: (h, i, 0))],
        out_shape=[jax.ShapeDtypeStruct((s, WIDTH_B), F32), jax.ShapeDtypeStruct((H_B, s, 1), F32)],
        scratch_shapes=[pltpu.VMEM((blk, 1), F32), pltpu.VMEM((blk, 1), F32), pltpu.VMEM((blk, V_HEAD), F32)],
        compiler_params=_params(("parallel", "parallel", "arbitrary")),
    )(q, kn, kr, v)


def _attention_bwd(q, kn, kr, v, do, lse, delta, *, blk):
    s = q.shape[0]
    nb = s // blk

    def body(q_ref, kn_ref, kr_ref, v_ref, do_ref, lse_ref, dl_ref, dq_ref, dkn_ref, dkr_ref, dv_ref, dk_s, dv_s, kcat_s):
        j, i = pl.program_id(1), pl.program_id(2)

        @pl.when(i == 0)
        def _():
            dk_s[...] = jnp.zeros_like(dk_s)
            dv_s[...] = jnp.zeros_like(dv_s)
            kcat_s[...] = jnp.concatenate([kn_ref[...], kr_ref[...]], axis=1)

        def step(masked):
            qv, dov, kcat = q_ref[...], do_ref[...], kcat_s[...]
            sc = _dot(qv, kcat, 1, 1) * ATT_SCALE
            if masked:
                sc = _causal_mask(sc, blk, blk)
            p = jnp.exp(sc - lse_ref[0])
            dv_s[...] += _dot(p.astype(BF16), dov, 0, 0)
            dp = _dot(dov, v_ref[...], 1, 1)
            ds = (p * (dp - dl_ref[0]) * ATT_SCALE).astype(BF16)
            dk_s[...] += _dot(ds, qv, 0, 0)
            dq_blk = _dot(ds, kcat, 1, 0)
            rows = pl.ds(pl.multiple_of(i * blk, blk), blk)

            @pl.when(j == 0)
            def _():
                dq_ref[rows, :] = dq_blk

            @pl.when(j > 0)
            def _():
                dq_ref[rows, :] += dq_blk

        @pl.when(i > j)
        def _():
            step(False)

        @pl.when(i == j)
        def _():
            step(True)

        @pl.when(i == nb - 1)
        def _():
            dkn_ref[...] = dk_s[:, 0:128]
            dkr_ref[...] = dk_s[:, 128:256]
            dv_ref[...] = dv_s[...]

    q_idx = lambda h, j, i: (jnp.maximum(i, j), h)
    st_idx = lambda h, j, i: (h, jnp.maximum(i, j), 0)
    kv_idx = lambda h, j, i: (j, h)
    return pl.pallas_call(
        body, name="attention_bwd", grid=(H_B, nb, nb),
        in_specs=[pl.BlockSpec((blk, QK_PAD), q_idx), pl.BlockSpec((blk, 128), kv_idx),
                  pl.BlockSpec((blk, 128), lambda h, j, i: (j, 0)), pl.BlockSpec((blk, 128), kv_idx),
                  pl.BlockSpec((blk, V_HEAD), q_idx), pl.BlockSpec((1, blk, 1), st_idx), pl.BlockSpec((1, blk, 1), st_idx)],
        out_specs=[pl.BlockSpec((s, QK_PAD), lambda h, j, i: (0, h)), pl.BlockSpec((blk, 128), kv_idx),
                   pl.BlockSpec((blk, 128), kv_idx), pl.BlockSpec((blk, 128), kv_idx)],
        out_shape=[jax.ShapeDtypeStruct((s, H_B * QK_PAD), F32), jax.ShapeDtypeStruct((s, H_B * 128), F32),
                   jax.ShapeDtypeStruct((s, H_B * 128), F32), jax.ShapeDtypeStruct((s, H_B * 128), F32)],
        scratch_shapes=[pltpu.VMEM((blk, QK_PAD), F32), pltpu.VMEM((blk, V_HEAD), F32), pltpu.VMEM((blk, QK_PAD), BF16)],
        compiler_params=_params(("parallel", "arbitrary", "arbitrary")),
    )(q, kn, kr, v, do, lse, delta)


def _local_step(x, target, w, kv_norm, q_norm, ln_g, ln_b, *, ts=256, blk=512):
    s = x.shape[0]
    cos_a, sin_a = _rope_tables_a(s)
    cos_b, sa_b, sb_b = _rope_tables_b(s)
    tables = _retention_tables()
    g0, g1, b0, b1 = ln_g[0:1], ln_g[1:2], ln_b[0:1], ln_b[1:2]

    h_a = _mm(x, w["a_in"], tn=1536, name="a_in_fwd")
    q_a, k_a, v_a = _rope_a_fwd(h_a, cos_a, sin_a, ts=ts)
    o_a, states = _retention_fwd(q_a, k_a, v_a, tables)
    u_a = _gn_gate_fwd(o_a, h_a, ts=ts)
    y_a = _mm(u_a, w["a_out"], tn=1024, name="a_out_fwd")
    x1, x1_16 = _ln_fwd(x, y_a, g0, b0, ts=ts)

    h1 = _mm(x1_16, w["b_in1"], tn=1152, name="b_in_fwd")
    lat16, kr16, qn16 = _kvq_prep(h1, kv_norm, q_norm, cos_b, sa_b, sb_b, ts=ts)
    kn16 = _mm(lat16, w["up_k"], out_dtype=BF16, tn=1024, name="up_k_fwd")
    v16 = _mm(lat16, w["up_v"], out_dtype=BF16, tn=1024, name="up_v_fwd")
    qf = _mm(qn16, w["uq"], tn=1024, name="uq_fwd")
    q16 = _rope_q(qf, cos_b, sa_b, sb_b, 1.0, BF16, ts=ts, name="rope_q_fwd")
    o_b, lse = _attention_fwd(q16, kn16, kr16, v16, blk=blk)
    u_b = _gate_b_fwd(o_b, h1, ts=ts)
    y_b = _mm(u_b, w["b_out"], tn=1024, name="b_out_fwd")

    dz_b, dz_b16, dg1, db1, loss = _ln_loss_bwd(x1, y_b, target, g1, b1, ts=ts)
    d_b_out = _mm(u_b, dz_b16, ta=True, tn=1024, tk=1024, name="b_out_dw")
    du_b = _mm(dz_b16, w["b_out"], tb=True, tn=1024, name="b_out_dx")
    do16, dgate16, delta = _gate_b_bwd(du_b, o_b, h1, ts=ts)
    dq, dkn, dkr_heads, dv = _attention_bwd(q16, kn16, kr16, v16, do16, lse, delta, blk=blk)
    dqf16 = _rope_q(dq, cos_b, sa_b, sb_b, -1.0, BF16, ts=ts, name="rope_q_bwd")
    d_uq = _mm(qn16, dqf16, ta=True, tm=768, tn=1024, tk=1024, name="uq_dw")
    dqn = _mm(dqf16, w["uq"], tb=True, tn=768, tk=2048, name="uq_dx")
    d_up_k = _mm(lat16, dkn, ta=True, tn=1024, tk=1024, name="up_k_dw")
    d_up_v = _mm(lat16, dv, ta=True, tn=1024, tk=1024, name="up_v_dw")
    dlat_k = _mm(dkn, w["up_k"], tb=True, tn=512, name="up_k_dx")
    dlat_v = _mm(dv, w["up_v"], tb=True, tn=512, name="up_v_dx")
    dh1, dkvn, dqnorm = _h1_bwd(h1, dlat_k, dlat_v, dkr_heads, dqn, dgate16, kv_norm, q_norm, cos_b, sa_b, sb_b, ts=ts)
    d_b_in1 = _mm(x1_16, dh1, ta=True, tn=1152, tk=1024, name="b_in_dw")
    dx1 = _mm(dh1, w["b_in1"], tb=True, tn=1024, name="b_in_dx")

    dz_a, dz_a16, dg0, db0 = _ln_bwd_call(dz_b, dx1, x, y_a, g0, ts=ts)
    d_a_out = _mm(u_a, dz_a16, ta=True, tn=1024, tk=1024, name="a_out_dw")
    du_a = _mm(dz_a16, w["a_out"], tb=True, tn=1024, name="a_out_dx")
    do_a16, dgate_a16 = _gn_gate_bwd(du_a, o_a, h_a, ts=ts)
    dq_a, dk_a, dv_a16 = _retention_bwd(q_a, k_a, v_a, states, do_a16, tables)
    dh_a = _rope_a_bwd(dq_a, dk_a, dv_a16, dgate_a16, cos_a, sin_a, ts=ts)
    d_a_in = _mm(x, dh_a, ta=True, tn=1536, tk=1024, name="a_in_dw")
    dx_a = _mm(dh_a, w["a_in"], tb=True, tn=1024, tk=2048, name="a_in_dx")
    grad_x = _residual_grad(dz_a, dx_a, ts=ts)

    grads = dict(a_in=d_a_in, a_out=d_a_out, b_in1=d_b_in1, uq=d_uq, b_out=d_b_out, up_k=d_up_k, up_v=d_up_v)
    small = dict(ln_g=jnp.concatenate([dg0, dg1], axis=0), ln_b=jnp.concatenate([db0, db1], axis=0),
                 q_norm=dqnorm, kv_norm=dkvn)
    return loss, grad_x, grads, small


def _flat_shards(shards, dtype):
    parts = [shards[name].reshape(rows, FLAT_COLS) for name, rows in SHARD_ROWS]
    used = sum(rows for _, rows in SHARD_ROWS)
    parts.append(jnp.zeros((FLAT_ROWS - used, FLAT_COLS), shards["a_w_in"].dtype))
    return jnp.concatenate(parts, axis=0).astype(dtype)


def _unflat_shards(flat, shapes):
    out, off = {}, 0
    for name, rows in SHARD_ROWS:
        out[name] = flat[off:off + rows].reshape(shapes[name])
        off += rows
    return out


COL_SHARDED = {"a_w_in": (D_MODEL, IN_A), "b_w_in": (D_MODEL, IN_B), "b_w_uq": (Q_LORA, H_B * (QK_NOPE + QK_ROPE)),
               "kv_w_up": (KV_LORA, H_B * (QK_NOPE + V_HEAD))}
ROW_SHARDED = {"a_w_out": (WIDTH_A, D_MODEL), "b_w_out": (WIDTH_B, D_MODEL), "kv_w_down": (D_MODEL, KV_LORA + QK_ROPE)}


def _full_from_gathered(gathered):
    out, off = {}, 0
    for name, rows in SHARD_ROWS:
        part = gathered[:, off:off + rows]
        off += rows
        if name in COL_SHARDED:
            r, c = COL_SHARDED[name]
            out[name] = part.reshape(N_CHIPS, r, c // N_CHIPS).transpose(1, 0, 2).reshape(r, c)
        else:
            r, c = ROW_SHARDED[name]
            out[name] = part.reshape(r, c)
    return out


def _gathered_from_full(full):
    parts = []
    for name, rows in SHARD_ROWS:
        g = full[name]
        if name in COL_SHARDED:
            r, c = COL_SHARDED[name]
            g = g.reshape(r, N_CHIPS, c // N_CHIPS).transpose(1, 0, 2)
        parts.append(g.reshape(N_CHIPS, rows, FLAT_COLS))
    used = sum(rows for _, rows in SHARD_ROWS)
    parts.append(jnp.zeros((N_CHIPS, FLAT_ROWS - used, FLAT_COLS), F32))
    return jnp.concatenate(parts, axis=1)


def _kernel_layout(full):
    uq = full["b_w_uq"].reshape(Q_LORA, H_B, QK_NOPE + QK_ROPE)
    uq = jnp.pad(uq, ((0, 0), (0, 0), (0, QK_PAD - QK_NOPE - QK_ROPE))).reshape(Q_LORA, H_B * QK_PAD)
    up = full["kv_w_up"].reshape(KV_LORA, H_B, QK_NOPE + V_HEAD)
    down = jnp.pad(full["kv_w_down"], ((0, 0), (0, KV_DOWN_PAD - KV_LORA - QK_ROPE)))
    return dict(a_in=full["a_w_in"], a_out=full["a_w_out"], b_out=full["b_w_out"], uq=uq,
                up_k=up[:, :, :QK_NOPE].reshape(KV_LORA, H_B * QK_NOPE),
                up_v=up[:, :, QK_NOPE:].reshape(KV_LORA, H_B * V_HEAD),
                b_in1=jnp.concatenate([down, full["b_w_in"]], axis=1))


def _reference_layout_grads(g):
    uq = g["uq"].reshape(Q_LORA, H_B, QK_PAD)[:, :, :QK_NOPE + QK_ROPE].reshape(Q_LORA, H_B * (QK_NOPE + QK_ROPE))
    up = jnp.concatenate([g["up_k"].reshape(KV_LORA, H_B, QK_NOPE), g["up_v"].reshape(KV_LORA, H_B, V_HEAD)], axis=2)
    return dict(a_w_in=g["a_in"], a_w_out=g["a_out"], b_w_out=g["b_out"], b_w_uq=uq,
                kv_w_up=up.reshape(KV_LORA, H_B * (QK_NOPE + V_HEAD)),
                kv_w_down=g["b_in1"][:, :KV_LORA + QK_ROPE], b_w_in=g["b_in1"][:, KV_DOWN_PAD:])


HBM_SPEC = pl.BlockSpec(memory_space=pl.ANY)


def _me():
    return lax.axis_index("x"), lax.axis_index("y"), lax.axis_index("c")


def _chip_flips(x, y):
    return [(1 - x, y), (x, 1 - y), (1 - x, 1 - y)]


def _gather_weights(flat16):
    def body(src_ref, out_ref, send_sems, recv_sems, local_sem):
        x, y, c = _me()
        mine = 2 * x + y
        local = pltpu.make_async_copy(src_ref, out_ref.at[mine], local_sem)
        local.start()
        copies = []
        for k, (px, py) in enumerate(_chip_flips(x, y)):
            copies.append(pltpu.make_async_remote_copy(
                src_ref=src_ref, dst_ref=out_ref.at[mine], send_sem=send_sems.at[k], recv_sem=recv_sems.at[k],
                device_id=(px, py, c), device_id_type=MESH))
        for cp in copies:
            cp.start()
        for cp in copies:
            cp.wait_send()
        for k, (px, py) in enumerate(_chip_flips(x, y)):
            pltpu.make_async_remote_copy(
                src_ref=src_ref, dst_ref=out_ref.at[2 * px + py], send_sem=send_sems.at[k], recv_sem=recv_sems.at[k],
                device_id=(px, py, c), device_id_type=MESH).wait_recv()
        local.wait()

    return pl.pallas_call(
        body, name="gather_weights",
        in_specs=[HBM_SPEC], out_specs=HBM_SPEC,
        out_shape=jax.ShapeDtypeStruct((N_CHIPS,) + flat16.shape, flat16.dtype),
        scratch_shapes=[pltpu.SemaphoreType.DMA((3,)), pltpu.SemaphoreType.DMA((3,)), pltpu.SemaphoreType.DMA],
    )(flat16)


def _pair_exchange_halves(g):
    def body(g_ref, out_ref, send_sem, recv_sem):
        x, y, c = _me()
        theirs = g_ref.at[:, pl.ds(pl.multiple_of((1 - c) * HALF_ROWS, 8), HALF_ROWS), :]
        cp = pltpu.make_async_remote_copy(src_ref=theirs, dst_ref=out_ref, send_sem=send_sem, recv_sem=recv_sem,
                                          device_id=(x, y, 1 - c), device_id_type=MESH)
        cp.start()
        cp.wait_send()
        cp.wait_recv()

    return pl.pallas_call(
        body, name="pair_exchange_halves",
        in_specs=[HBM_SPEC], out_specs=HBM_SPEC,
        out_shape=jax.ShapeDtypeStruct((N_CHIPS, HALF_ROWS, FLAT_COLS), g.dtype),
        scratch_shapes=[pltpu.SemaphoreType.DMA, pltpu.SemaphoreType.DMA],
    )(g)


def _chip_exchange(p):
    def body(p_ref, out_ref, send_sems, recv_sems):
        x, y, c = _me()
        copies = []
        for k, (px, py) in enumerate(_chip_flips(x, y)):
            copies.append(pltpu.make_async_remote_copy(
                src_ref=p_ref.at[2 * px + py], dst_ref=out_ref.at[k], send_sem=send_sems.at[k], recv_sem=recv_sems.at[k],
                device_id=(px, py, c), device_id_type=MESH))
        for cp in copies:
            cp.start()
        for cp in copies:
            cp.wait_send()
        for cp in copies:
            cp.wait_recv()

    return pl.pallas_call(
        body, name="chip_exchange",
        in_specs=[HBM_SPEC], out_specs=HBM_SPEC,
        out_shape=jax.ShapeDtypeStruct((3, HALF_ROWS, FLAT_COLS), p.dtype),
        scratch_shapes=[pltpu.SemaphoreType.DMA((3,)), pltpu.SemaphoreType.DMA((3,))],
    )(p)


def _pair_share(r):
    def body(r_ref, out_ref, send_sem, recv_sem, local_sem):
        x, y, c = _me()
        my_rows = pl.ds(pl.multiple_of(c * HALF_ROWS, 8), HALF_ROWS)
        local = pltpu.make_async_copy(r_ref, out_ref.at[my_rows, :], local_sem)
        local.start()
        cp = pltpu.make_async_remote_copy(src_ref=r_ref, dst_ref=out_ref.at[my_rows, :], send_sem=send_sem,
                                          recv_sem=recv_sem, device_id=(x, y, 1 - c), device_id_type=MESH)
        cp.start()
        cp.wait_send()
        their_rows = pl.ds(pl.multiple_of((1 - c) * HALF_ROWS, 8), HALF_ROWS)
        pltpu.make_async_remote_copy(src_ref=r_ref, dst_ref=out_ref.at[their_rows, :], send_sem=send_sem,
                                     recv_sem=recv_sem, device_id=(x, y, 1 - c), device_id_type=MESH).wait_recv()
        local.wait()

    return pl.pallas_call(
        body, name="pair_share",
        in_specs=[HBM_SPEC], out_specs=HBM_SPEC,
        out_shape=jax.ShapeDtypeStruct((FLAT_ROWS, FLAT_COLS), r.dtype),
        scratch_shapes=[pltpu.SemaphoreType.DMA, pltpu.SemaphoreType.DMA, pltpu.SemaphoreType.DMA],
    )(r)


def _sum_small(vec):
    def body(v_ref, out_ref, slots, send_sems, recv_sems):
        x, y, c = _me()
        me = 4 * x + 2 * y + c
        slots[me] = v_ref[...]
        flips = [(fx, fy, fc) for fx in (0, 1) for fy in (0, 1) for fc in (0, 1)][1:]
        copies = []
        for k, (fx, fy, fc) in enumerate(flips):
            copies.append(pltpu.make_async_remote_copy(
                src_ref=v_ref, dst_ref=slots.at[me], send_sem=send_sems.at[k], recv_sem=recv_sems.at[k],
                device_id=(x ^ fx, y ^ fy, c ^ fc), device_id_type=MESH))
        for cp in copies:
            cp.start()
        for cp in copies:
            cp.wait_send()
        for k, (fx, fy, fc) in enumerate(flips):
            src = 4 * (x ^ fx) + 2 * (y ^ fy) + (c ^ fc)
            pltpu.make_async_remote_copy(
                src_ref=v_ref, dst_ref=slots.at[src], send_sem=send_sems.at[k], recv_sem=recv_sems.at[k],
                device_id=(x ^ fx, y ^ fy, c ^ fc), device_id_type=MESH).wait_recv()
        total = slots[0]
        for d in range(1, N_DEV):
            total = total + slots[d]
        out_ref[...] = total

    return pl.pallas_call(
        body, name="sum_small",
        in_specs=[pl.BlockSpec(memory_space=pltpu.VMEM)], out_specs=pl.BlockSpec(memory_space=pltpu.VMEM),
        out_shape=jax.ShapeDtypeStruct(vec.shape, vec.dtype),
        scratch_shapes=[pltpu.VMEM((N_DEV,) + vec.shape, vec.dtype), pltpu.SemaphoreType.DMA((7,)),
                        pltpu.SemaphoreType.DMA((7,))],
    )(vec)


UPD_ROWS = 256


def _pair_sum(g, theirs, c_arr):
    nb = HALF_ROWS // UPD_ROWS

    def body(c_ref, g_ref, t_ref, o_ref):
        o_ref[...] = g_ref[...] + t_ref[...]

    return pl.pallas_call(
        body, name="pair_sum",
        grid_spec=pltpu.PrefetchScalarGridSpec(
            num_scalar_prefetch=1, grid=(N_CHIPS, nb),
            in_specs=[pl.BlockSpec((1, UPD_ROWS, FLAT_COLS), lambda d, i, c_ref: (d, c_ref[0] * nb + i, 0)),
                      pl.BlockSpec((1, UPD_ROWS, FLAT_COLS), lambda d, i, c_ref: (d, i, 0))],
            out_specs=pl.BlockSpec((1, UPD_ROWS, FLAT_COLS), lambda d, i, c_ref: (d, i, 0))),
        out_shape=jax.ShapeDtypeStruct((N_CHIPS, HALF_ROWS, FLAT_COLS), F32),
        compiler_params=_params(("parallel", "parallel")),
    )(c_arr, g, theirs)


def _chip_sum(p, received, chip_arr):
    nb = HALF_ROWS // UPD_ROWS

    def body(chip_ref, p_ref, r_ref, o_ref):
        o_ref[...] = ((p_ref[0] + r_ref[0]) + r_ref[1]) + r_ref[2]

    return pl.pallas_call(
        body, name="chip_sum",
        grid_spec=pltpu.PrefetchScalarGridSpec(
            num_scalar_prefetch=1, grid=(nb,),
            in_specs=[pl.BlockSpec((1, UPD_ROWS, FLAT_COLS), lambda i, chip_ref: (chip_ref[0], i, 0)),
                      pl.BlockSpec((3, UPD_ROWS, FLAT_COLS), lambda i, chip_ref: (0, i, 0))],
            out_specs=pl.BlockSpec((UPD_ROWS, FLAT_COLS), lambda i, chip_ref: (i, 0))),
        out_shape=jax.ShapeDtypeStruct((HALF_ROWS, FLAT_COLS), F32),
        compiler_params=_params(("parallel",)),
    )(chip_arr, p, received)


def _adamw(w, g, m, v, *, rows, name):
    r, c = w.shape
    rows = min(rows, r)
    assert r % rows == 0

    def body(w_ref, g_ref, m_ref, v_ref, d_ref, nm_ref, nv_ref):
        gv = g_ref[...]
        nm = ADAM_B1 * m_ref[...] + (1.0 - ADAM_B1) * gv
        nv = ADAM_B2 * v_ref[...] + (1.0 - ADAM_B2) * (gv * gv)
        m_hat = nm / (1.0 - ADAM_B1 ** ADAM_STEP)
        v_hat = nv / (1.0 - ADAM_B2 ** ADAM_STEP)
        d_ref[...] = -ADAM_LR * (m_hat / (jnp.sqrt(v_hat) + ADAM_EPS) + ADAM_WD * w_ref[...])
        nm_ref[...] = nm
        nv_ref[...] = nv

    spec = pl.BlockSpec((rows, c), lambda i: (i, 0))
    return pl.pallas_call(
        body, name=name, grid=(r // rows,),
        in_specs=[spec] * 4, out_specs=[spec] * 3,
        out_shape=[jax.ShapeDtypeStruct((r, c), F32)] * 3,
        compiler_params=_params(("parallel",)),
    )(w, g, m, v)


W_NAMES = ("a_w_in", "a_w_out", "b_w_in", "b_q_norm", "b_w_uq", "b_w_out", "kv_w_down", "kv_norm", "kv_w_up", "ln_g", "ln_b")
BIG = tuple(name for name, _ in SHARD_ROWS)


def _pack_small(ln_g, ln_b, q_norm, kv_norm, extra=None):
    pad = lambda a: jnp.pad(a.reshape(1, -1), ((0, 0), (0, FLAT_COLS - a.size)))
    rows = [ln_g, ln_b, pad(q_norm), pad(kv_norm),
            jnp.zeros((1, FLAT_COLS), F32) if extra is None else pad(extra), jnp.zeros((1, FLAT_COLS), F32)]
    return jnp.concatenate(rows, axis=0)


def _unpack_small(p):
    return dict(ln_g=p[0:2], ln_b=p[2:4], b_q_norm=p[4:5, :Q_LORA], kv_norm=p[5, :KV_LORA])


def kernel(x, a_w_in, a_w_out, b_w_in, b_q_norm, b_w_uq, b_w_out, kv_w_down, kv_norm, kv_w_up, ln_g, ln_b, loss_target, m_a_w_in, m_a_w_out, m_b_w_in, m_b_q_norm, m_b_w_uq, m_b_w_out, m_kv_w_down, m_kv_norm, m_kv_w_up, m_ln_g, m_ln_b, v_a_w_in, v_a_w_out, v_b_w_in, v_b_q_norm, v_b_w_uq, v_b_w_out, v_kv_w_down, v_kv_norm, v_kv_w_up, v_ln_g, v_ln_b):
    w_in = dict(a_w_in=a_w_in[0], a_w_out=a_w_out[0], b_w_in=b_w_in[0], b_w_uq=b_w_uq[0], b_w_out=b_w_out[0],
                kv_w_down=kv_w_down, kv_w_up=kv_w_up)
    m_in = dict(a_w_in=m_a_w_in[0], a_w_out=m_a_w_out[0], b_w_in=m_b_w_in[0], b_w_uq=m_b_w_uq[0], b_w_out=m_b_w_out[0],
                kv_w_down=m_kv_w_down, kv_w_up=m_kv_w_up)
    v_in = dict(a_w_in=v_a_w_in[0], a_w_out=v_a_w_out[0], b_w_in=v_b_w_in[0], b_w_uq=v_b_w_uq[0], b_w_out=v_b_w_out[0],
                kv_w_down=v_kv_w_down, kv_w_up=v_kv_w_up)
    shard_shapes = {name: w_in[name].shape for name in BIG}

    gathered = _gather_weights(_flat_shards(w_in, BF16))
    weights = _kernel_layout(_full_from_gathered(gathered))

    loss, grad_x, grads, small = _local_step(x[0], loss_target[0], weights, kv_norm.reshape(1, -1), b_q_norm, ln_g, ln_b)

    cx, cy, cc = lax.axis_index("x"), lax.axis_index("y"), lax.axis_index("c")
    g_all = _gathered_from_full(_reference_layout_grads(grads))
    theirs = _pair_exchange_halves(g_all)
    pair = _pair_sum(g_all, theirs, cc.astype(jnp.int32).reshape(1))
    received = _chip_exchange(pair)
    half = _chip_sum(pair, received, (2 * cx + cy).astype(jnp.int32).reshape(1))
    g_flat = _pair_share(half)

    small_sum = _sum_small(_pack_small(small["ln_g"], small["ln_b"], small["q_norm"], small["kv_norm"], loss[:, :1]))
    loss_out = small_sum[6, 0]
    g_small = _unpack_small(small_sum)

    d_flat, nm_flat, nv_flat = _adamw(_flat_shards(w_in, F32), g_flat, _flat_shards(m_in, F32), _flat_shards(v_in, F32),
                                      rows=512, name="adamw_shards")
    ds, nms, nvs = _adamw(_pack_small(ln_g, ln_b, b_q_norm, kv_norm), small_sum.at[6].set(0.0),
                          _pack_small(m_ln_g, m_ln_b, m_b_q_norm, m_kv_norm),
                          _pack_small(v_ln_g, v_ln_b, v_b_q_norm, v_kv_norm), rows=8, name="adamw_small")

    def assemble(flat, small_packed):
        big = _unflat_shards(flat, shard_shapes)
        sm = _unpack_small(small_packed)
        out = {}
        for name in W_NAMES:
            if name in big:
                out[name] = big[name][None] if name in ("a_w_in", "a_w_out", "b_w_in", "b_w_uq", "b_w_out") else big[name]
            else:
                out[name] = sm[name]
        return [out[name] for name in W_NAMES]

    grad_list = assemble(g_flat, small_sum)
    return (loss_out, grad_x[None], *grad_list, *assemble(d_flat, ds), *assemble(nm_flat, nms), *assemble(nv_flat, nvs))
```

```python
import functools
import math

import jax
import jax.numpy as jnp
from jax import lax
from jax.experimental import pallas as pl
from jax.experimental.pallas import tpu as pltpu

F32 = jnp.float32
BF16 = jnp.bfloat16
MESH = pl.DeviceIdType.MESH

D_MODEL = 1024
DEPTH = 2
H_A, DK_A, DV_A = 4, 256, 512
WIDTH_A = H_A * DV_A
CHUNK = 128
H_B, QK_NOPE, QK_ROPE, V_HEAD = 16, 128, 64, 128
QK_PAD = 256
Q_LORA, KV_LORA = 768, 512
KV_DOWN_PAD = 640
WIDTH_B = H_B * V_HEAD
IN_A = 2 * H_A * DK_A + 2 * WIDTH_A
IN_B = Q_LORA + WIDTH_B
H1_B = KV_DOWN_PAD + IN_B
ROPE_BASE = 10000.0
ALPHA = (2.0 * DEPTH) ** 0.25
ATT_SCALE = (QK_NOPE + QK_ROPE) ** -0.5
NEG_BIG = -1e30

ADAM_LR, ADAM_B1, ADAM_B2, ADAM_EPS, ADAM_WD, ADAM_STEP = 0.001, 0.9, 0.999, 1e-08, 0.01, 10

VMEM_LIMIT_BYTES = 56 * 1024 * 1024
LANES = 128
FLAT_COLS = 1024
SHARD_ROWS = (("a_w_in", 1536), ("a_w_out", 512), ("b_w_in", 704), ("b_w_uq", 576), ("b_w_out", 512),
              ("kv_w_down", 144), ("kv_w_up", 512))
FLAT_ROWS = 4608
HALF_ROWS = FLAT_ROWS // 2
N_CHIPS = 4
N_DEV = 8


def _params(sem, vmem=VMEM_LIMIT_BYTES):
    return pltpu.CompilerParams(dimension_semantics=sem, vmem_limit_bytes=vmem)


def _row_spec(ts, w, col_block=0):
    return pl.BlockSpec((ts, w), lambda i: (i, col_block))


def _bc_spec(shape):
    nd = len(shape)
    return pl.BlockSpec(shape, lambda i: (0,) * nd)


def _sigmoid(x):
    return 1.0 / (1.0 + jnp.exp(-x))


def _fold8(v):
    ts, w = v.shape
    return jnp.sum(v.reshape(ts // 8, 8, w), axis=0)


def _mm(a, b, *, ta=False, tb=False, out_dtype=F32, tm=512, tn=512, tk=None, name):
    if ta:
        K, M = a.shape
    else:
        M, K = a.shape
    if tb:
        N, Kb = b.shape
    else:
        Kb, N = b.shape
    assert K == Kb, (a.shape, b.shape)
    tm, tn = min(tm, M), min(tn, N)
    tk = K if tk is None else min(tk, K)
    assert M % tm == 0 and N % tn == 0 and K % tk == 0, (name, M, N, K, tm, tn, tk)
    nk = K // tk
    dims = (((0,) if ta else (1,), (1,) if tb else (0,)), ((), ()))

    def body(a_ref, b_ref, o_ref, *scratch):
        prod = lax.dot_general(a_ref[...].astype(BF16), b_ref[...].astype(BF16), dims,
                               preferred_element_type=F32)
        if nk == 1:
            o_ref[...] = prod.astype(o_ref.dtype)
        else:
            acc, = scratch
            k = pl.program_id(2)

            @pl.when(k == 0)
            def _():
                acc[...] = prod

            @pl.when(k > 0)
            def _():
                acc[...] += prod

            @pl.when(k == nk - 1)
            def _():
                o_ref[...] = acc[...].astype(o_ref.dtype)

    a_spec = pl.BlockSpec((tk, tm), lambda i, j, k: (k, i)) if ta else pl.BlockSpec((tm, tk), lambda i, j, k: (i, k))
    b_spec = pl.BlockSpec((tn, tk), lambda i, j, k: (j, k)) if tb else pl.BlockSpec((tk, tn), lambda i, j, k: (k, j))
    return pl.pallas_call(
        body, name=name,
        grid=(M // tm, N // tn, nk),
        in_specs=[a_spec, b_spec],
        out_specs=pl.BlockSpec((tm, tn), lambda i, j, k: (i, j)),
        out_shape=jax.ShapeDtypeStruct((M, N), out_dtype),
        scratch_shapes=[] if nk == 1 else [pltpu.VMEM((tm, tn), F32)],
        compiler_params=_params(("parallel", "parallel", "arbitrary")),
    )(a, b)


def _rope_tables_a(s):
    half = DK_A // 2
    inv = ROPE_BASE ** (-jnp.arange(half, dtype=F32) / half)
    ang = jnp.arange(s, dtype=F32)[:, None] * inv[None, :]
    return jnp.cos(ang), jnp.sin(ang)


def _rope_tables_b(s):
    half = QK_ROPE // 2
    inv = ROPE_BASE ** (-jnp.arange(half, dtype=F32) / half)
    ang = jnp.arange(s, dtype=F32)[:, None] * inv[None, :]
    c, sn = jnp.cos(ang), jnp.sin(ang)
    z = jnp.zeros_like(c)
    cos = jnp.concatenate([c, c, z, z], axis=1)
    sa = jnp.concatenate([-sn, z, z, z], axis=1)
    sb = jnp.concatenate([z, sn, z, z], axis=1)
    return cos, sa, sb


def _rope_b(r, cos, sa, sb, sign):
    return r * cos + sign * (pltpu.roll(r, 96, 1) * sa + pltpu.roll(r, 32, 1) * sb)


def _retention_tables():
    lg = jnp.log1p(-jnp.exp2(-5.0 - jnp.arange(H_A, dtype=F32)))
    idx = jnp.arange(CHUNK, dtype=F32)
    diff = idx[:, None] - idx[None, :]
    causal = diff >= 0
    dmat = jnp.where(causal, jnp.exp(jnp.where(causal, diff, 0.0)[None] * lg[:, None, None]), 0.0)
    qdec = jnp.exp((idx + 1.0)[None, :] * lg[:, None])[:, :, None]
    kdec = jnp.exp((CHUNK - 1.0 - idx)[None, :] * lg[:, None])[:, :, None]
    cdec = jnp.broadcast_to(jnp.exp(CHUNK * lg)[:, None, None], (H_A, 1, DV_A))
    return dmat, qdec, kdec, cdec


def _rope_a_fwd(h_a, cos, sin, *, ts):
    s = h_a.shape[0]

    def body(h_ref, c_ref, s_ref, q_ref, k_ref, v_ref):
        c, sn = c_ref[...], s_ref[...]
        for o_ref, base, scale in ((q_ref, 0, 1.0), (k_ref, H_A * DK_A, DK_A ** -0.5)):
            for hd in range(H_A):
                lo = hd * DK_A
                x1 = h_ref[:, base + lo:base + lo + 128]
                x2 = h_ref[:, base + lo + 128:base + lo + 256]
                o_ref[:, lo:lo + 128] = ((x1 * c - x2 * sn) * scale).astype(o_ref.dtype)
                o_ref[:, lo + 128:lo + 256] = ((x2 * c + x1 * sn) * scale).astype(o_ref.dtype)
        v_ref[...] = h_ref[:, 2 * H_A * DK_A:2 * H_A * DK_A + WIDTH_A].astype(v_ref.dtype)

    return pl.pallas_call(
        body, name="rope_a_fwd", grid=(s // ts,),
        in_specs=[_row_spec(ts, 4096), _row_spec(ts, 128), _row_spec(ts, 128)],
        out_specs=[_row_spec(ts, 1024), _row_spec(ts, 1024), _row_spec(ts, 2048)],
        out_shape=[jax.ShapeDtypeStruct((s, 1024), BF16), jax.ShapeDtypeStruct((s, 1024), BF16),
                   jax.ShapeDtypeStruct((s, 2048), BF16)],
        compiler_params=_params(("parallel",)),
    )(h_a, cos, sin)


def _rope_a_bwd(dq, dk, dv16, dg16, cos, sin, *, ts):
    s = dq.shape[0]

    def body(dq_ref, dk_ref, dv_ref, dg_ref, c_ref, s_ref, o_ref):
        c, sn = c_ref[...], s_ref[...]
        for d_ref, base, scale in ((dq_ref, 0, 1.0), (dk_ref, H_A * DK_A, DK_A ** -0.5)):
            for hd in range(H_A):
                lo = hd * DK_A
                d1 = d_ref[:, lo:lo + 128] * scale
                d2 = d_ref[:, lo + 128:lo + 256] * scale
                o_ref[:, base + lo:base + lo + 128] = (d1 * c + d2 * sn).astype(o_ref.dtype)
                o_ref[:, base + lo + 128:base + lo + 256] = (d2 * c - d1 * sn).astype(o_ref.dtype)
        o_ref[:, 2048:4096] = dv_ref[...]
        o_ref[:, 4096:6144] = dg_ref[...]

    return pl.pallas_call(
        body, name="rope_a_bwd", grid=(s // ts,),
        in_specs=[_row_spec(ts, 1024), _row_spec(ts, 1024), _row_spec(ts, 2048), _row_spec(ts, 2048),
                  _row_spec(ts, 128), _row_spec(ts, 128)],
        out_specs=_row_spec(ts, IN_A),
        out_shape=jax.ShapeDtypeStruct((s, IN_A), BF16),
        compiler_params=_params(("parallel",)),
    )(dq, dk, dv16, dg16, cos, sin)


def _gn_gate_fwd(o, h_a, *, ts):
    s = o.shape[0]

    def body(o_ref, g_ref, u_ref):
        for hd in range(H_A):
            sl = slice(hd * DV_A, (hd + 1) * DV_A)
            ov = o_ref[:, sl]
            mu = jnp.mean(ov, axis=-1, keepdims=True)
            oc = ov - mu
            var = jnp.mean(oc * oc, axis=-1, keepdims=True)
            on = oc * lax.rsqrt(var + 1e-5)
            g = g_ref[:, sl]
            u_ref[:, sl] = (on * (g * _sigmoid(g))).astype(u_ref.dtype)

    return pl.pallas_call(
        body, name="gn_gate_fwd", grid=(s // ts,),
        in_specs=[_row_spec(ts, WIDTH_A), _row_spec(ts, WIDTH_A, 2)],
        out_specs=_row_spec(ts, WIDTH_A),
        out_shape=jax.ShapeDtypeStruct((s, WIDTH_A), BF16),
        compiler_params=_params(("parallel",)),
    )(o, h_a)


def _gn_gate_bwd(du, o, h_a, *, ts):
    s = o.shape[0]

    def body(du_ref, o_ref, g_ref, do_ref, dg_ref):
        for hd in range(H_A):
            sl = slice(hd * DV_A, (hd + 1) * DV_A)
            ov = o_ref[:, sl]
            mu = jnp.mean(ov, axis=-1, keepdims=True)
            oc = ov - mu
            var = jnp.mean(oc * oc, axis=-1, keepdims=True)
            rstd = lax.rsqrt(var + 1e-5)
            on = oc * rstd
            g = g_ref[:, sl]
            sg = _sigmoid(g)
            du = du_ref[:, sl]
            don = du * (g * sg)
            dg_ref[:, sl] = (du * on * (sg * (1.0 + g * (1.0 - sg)))).astype(dg_ref.dtype)
            m1 = jnp.mean(don, axis=-1, keepdims=True)
            m2 = jnp.mean(don * on, axis=-1, keepdims=True)
            do_ref[:, sl] = (rstd * (don - m1 - on * m2)).astype(do_ref.dtype)

    return pl.pallas_call(
        body, name="gn_gate_bwd", grid=(s // ts,),
        in_specs=[_row_spec(ts, WIDTH_A), _row_spec(ts, WIDTH_A), _row_spec(ts, WIDTH_A, 2)],
        out_specs=[_row_spec(ts, WIDTH_A), _row_spec(ts, WIDTH_A)],
        out_shape=[jax.ShapeDtypeStruct((s, WIDTH_A), BF16), jax.ShapeDtypeStruct((s, WIDTH_A), BF16)],
        compiler_params=_params(("parallel",)),
    )(du, o, h_a)


def _ln_stats(z):
    mu = jnp.mean(z, axis=-1, keepdims=True)
    zc = z - mu
    var = jnp.mean(zc * zc, axis=-1, keepdims=True)
    rstd = lax.rsqrt(var + 1e-5)
    return zc * rstd, rstd


def _ln_bwd(dy, xhat, rstd, g):
    dxh = dy * g
    m1 = jnp.mean(dxh, axis=-1, keepdims=True)
    m2 = jnp.mean(dxh * xhat, axis=-1, keepdims=True)
    return rstd * (dxh - m1 - xhat * m2)


def _ln_fwd(x, y, g, b, *, ts):
    s = x.shape[0]

    def body(x_ref, y_ref, g_ref, b_ref, o_ref, o16_ref):
        xhat, _ = _ln_stats(ALPHA * x_ref[...] + y_ref[...])
        out = xhat * g_ref[...] + b_ref[...]
        o_ref[...] = out
        o16_ref[...] = out.astype(o16_ref.dtype)

    return pl.pallas_call(
        body, name="ln_fwd", grid=(s // ts,),
        in_specs=[_row_spec(ts, D_MODEL), _row_spec(ts, D_MODEL), _bc_spec((1, D_MODEL)), _bc_spec((1, D_MODEL))],
        out_specs=[_row_spec(ts, D_MODEL), _row_spec(ts, D_MODEL)],
        out_shape=[jax.ShapeDtypeStruct((s, D_MODEL), F32), jax.ShapeDtypeStruct((s, D_MODEL), BF16)],
        compiler_params=_params(("parallel",)),
    )(x, y, g, b)


def _ln_loss_bwd(x1, y, target, g, b, *, ts):
    s = x1.shape[0]
    n = s // ts

    def body(x_ref, y_ref, t_ref, g_ref, b_ref, dz_ref, dz16_ref, dg_ref, db_ref, loss_ref, ag, ab, al):
        i = pl.program_id(0)

        @pl.when(i == 0)
        def _():
            ag[...] = jnp.zeros_like(ag)
            ab[...] = jnp.zeros_like(ab)
            al[...] = jnp.zeros_like(al)

        xhat, rstd = _ln_stats(ALPHA * x_ref[...] + y_ref[...])
        err = xhat * g_ref[...] + b_ref[...] - t_ref[...]
        al[...] += _fold8(err * err)
        dy = err * (1.0 / D_MODEL)
        ag[...] += _fold8(dy * xhat)
        ab[...] += _fold8(dy)
        dz = _ln_bwd(dy, xhat, rstd, g_ref[...])
        dz_ref[...] = dz
        dz16_ref[...] = dz.astype(dz16_ref.dtype)

        @pl.when(i == n - 1)
        def _():
            dg_ref[...] = jnp.sum(ag[...], axis=0, keepdims=True)
            db_ref[...] = jnp.sum(ab[...], axis=0, keepdims=True)
            loss_ref[...] = jnp.full((1, LANES), (0.5 / D_MODEL) * jnp.sum(al[...]), F32)

    return pl.pallas_call(
        body, name="ln_loss_bwd", grid=(n,),
        in_specs=[_row_spec(ts, D_MODEL)] * 3 + [_bc_spec((1, D_MODEL))] * 2,
        out_specs=[_row_spec(ts, D_MODEL), _row_spec(ts, D_MODEL), _bc_spec((1, D_MODEL)), _bc_spec((1, D_MODEL)),
                   _bc_spec((1, LANES))],
        out_shape=[jax.ShapeDtypeStruct((s, D_MODEL), F32), jax.ShapeDtypeStruct((s, D_MODEL), BF16),
                   jax.ShapeDtypeStruct((1, D_MODEL), F32), jax.ShapeDtypeStruct((1, D_MODEL), F32),
                   jax.ShapeDtypeStruct((1, LANES), F32)],
        scratch_shapes=[pltpu.VMEM((8, D_MODEL), F32)] * 3,
        compiler_params=_params(("arbitrary",)),
    )(x1, y, target, g, b)


def _ln_bwd_call(dz_next, dx_branch, x, y, g, *, ts):
    s = x.shape[0]
    n = s // ts

    def body(dzn_ref, dxb_ref, x_ref, y_ref, g_ref, dz_ref, dz16_ref, dg_ref, db_ref, ag, ab):
        i = pl.program_id(0)

        @pl.when(i == 0)
        def _():
            ag[...] = jnp.zeros_like(ag)
            ab[...] = jnp.zeros_like(ab)

        xhat, rstd = _ln_stats(ALPHA * x_ref[...] + y_ref[...])
        dy = ALPHA * dzn_ref[...] + dxb_ref[...]
        ag[...] += _fold8(dy * xhat)
        ab[...] += _fold8(dy)
        dz = _ln_bwd(dy, xhat, rstd, g_ref[...])
        dz_ref[...] = dz
        dz16_ref[...] = dz.astype(dz16_ref.dtype)

        @pl.when(i == n - 1)
        def _():
            dg_ref[...] = jnp.sum(ag[...], axis=0, keepdims=True)
            db_ref[...] = jnp.sum(ab[...], axis=0, keepdims=True)

    return pl.pallas_call(
        body, name="ln_bwd", grid=(n,),
        in_specs=[_row_spec(ts, D_MODEL)] * 4 + [_bc_spec((1, D_MODEL))],
        out_specs=[_row_spec(ts, D_MODEL), _row_spec(ts, D_MODEL), _bc_spec((1, D_MODEL)), _bc_spec((1, D_MODEL))],
        out_shape=[jax.ShapeDtypeStruct((s, D_MODEL), F32), jax.ShapeDtypeStruct((s, D_MODEL), BF16),
                   jax.ShapeDtypeStruct((1, D_MODEL), F32), jax.ShapeDtypeStruct((1, D_MODEL), F32)],
        scratch_shapes=[pltpu.VMEM((8, D_MODEL), F32)] * 2,
        compiler_params=_params(("arbitrary",)),
    )(dz_next, dx_branch, x, y, g)


def _residual_grad(dz, dx_branch, *, ts):
    s = dz.shape[0]

    def body(dz_ref, dxb_ref, o_ref):
        o_ref[...] = ALPHA * dz_ref[...] + dxb_ref[...]

    return pl.pallas_call(
        body, name="residual_grad", grid=(s // ts,),
        in_specs=[_row_spec(ts, D_MODEL)] * 2, out_specs=_row_spec(ts, D_MODEL),
        out_shape=jax.ShapeDtypeStruct((s, D_MODEL), F32),
        compiler_params=_params(("parallel",)),
    )(dz, dx_branch)


C_LAT = slice(0, KV_LORA)
C_ROPE = slice(KV_LORA, KV_DOWN_PAD)
C_QL = slice(KV_DOWN_PAD, KV_DOWN_PAD + Q_LORA)
C_GATE = slice(KV_DOWN_PAD + Q_LORA, H1_B)


def _rms(x, eps=1e-6):
    r = lax.rsqrt(jnp.mean(x * x, axis=-1, keepdims=True) + eps)
    return x * r, r


def _rms_bwd(dy, xhat, r, g):
    dxh = dy * g
    return r * (dxh - xhat * jnp.mean(dxh * xhat, axis=-1, keepdims=True))


def _kvq_prep(h1, kv_norm, q_norm, cos, sa, sb, *, ts):
    s = h1.shape[0]

    def body(h_ref, kn_ref, qn_ref, c_ref, sa_ref, sb_ref, lat_ref, kr_ref, ql_ref):
        lat, _ = _rms(h_ref[:, C_LAT])
        lat_ref[...] = (lat * kn_ref[...]).astype(lat_ref.dtype)
        kr_ref[...] = _rope_b(h_ref[:, C_ROPE], c_ref[...], sa_ref[...], sb_ref[...], 1.0).astype(kr_ref.dtype)
        ql, _ = _rms(h_ref[:, C_QL])
        ql_ref[...] = (ql * qn_ref[...]).astype(ql_ref.dtype)

    return pl.pallas_call(
        body, name="kvq_prep", grid=(s // ts,),
        in_specs=[_row_spec(ts, H1_B), _bc_spec((1, KV_LORA)), _bc_spec((1, Q_LORA))] + [_row_spec(ts, 128)] * 3,
        out_specs=[_row_spec(ts, KV_LORA), _row_spec(ts, 128), _row_spec(ts, Q_LORA)],
        out_shape=[jax.ShapeDtypeStruct((s, KV_LORA), BF16), jax.ShapeDtypeStruct((s, 128), BF16),
                   jax.ShapeDtypeStruct((s, Q_LORA), BF16)],
        compiler_params=_params(("parallel",)),
    )(h1, kv_norm, q_norm, cos, sa, sb)


def _rope_q(qf, cos, sa, sb, sign, scale, out_dtype, *, ts, name):
    s = qf.shape[0]

    def body(q_ref, c_ref, sa_ref, sb_ref, o_ref):
        c, a, b = c_ref[...], sa_ref[...], sb_ref[...]
        for hd in range(H_B):
            lo = hd * QK_PAD
            o_ref[:, lo:lo + 128] = (q_ref[:, lo:lo + 128] * scale).astype(o_ref.dtype)
            o_ref[:, lo + 128:lo + 256] = (_rope_b(q_ref[:, lo + 128:lo + 256], c, a, b, sign) * scale).astype(o_ref.dtype)

    return pl.pallas_call(
        body, name=name, grid=(s // ts,),
        in_specs=[_row_spec(ts, H_B * QK_PAD)] + [_row_spec(ts, 128)] * 3,
        out_specs=_row_spec(ts, H_B * QK_PAD),
        out_shape=jax.ShapeDtypeStruct((s, H_B * QK_PAD), out_dtype),
        compiler_params=_params(("parallel",)),
    )(qf, cos, sa, sb)


def _gate_b_fwd(o, h1, *, ts):
    s = o.shape[0]

    def body(o_ref, h_ref, u_ref):
        g = h_ref[:, C_GATE]
        u_ref[...] = (o_ref[...] * (g * _sigmoid(g))).astype(u_ref.dtype)

    return pl.pallas_call(
        body, name="gate_b_fwd", grid=(s // ts,),
        in_specs=[_row_spec(ts, WIDTH_B), _row_spec(ts, H1_B)],
        out_specs=_row_spec(ts, WIDTH_B),
        out_shape=jax.ShapeDtypeStruct((s, WIDTH_B), BF16),
        compiler_params=_params(("parallel",)),
    )(o, h1)


def _gate_b_bwd(du, o, h1, *, ts):
    s = o.shape[0]

    def body(du_ref, o_ref, h_ref, do_ref, dg_ref, dl_ref):
        g = h_ref[:, C_GATE]
        sg = _sigmoid(g)
        du = du_ref[...]
        ov = o_ref[...]
        do = du * (g * sg)
        do_ref[...] = do.astype(do_ref.dtype)
        dg_ref[...] = (du * ov * (sg * (1.0 + g * (1.0 - sg)))).astype(dg_ref.dtype)
        prod = do * ov
        for hd in range(H_B):
            dl_ref[hd] = jnp.sum(prod[:, hd * V_HEAD:(hd + 1) * V_HEAD], axis=-1, keepdims=True)

    return pl.pallas_call(
        body, name="gate_b_bwd", grid=(s // ts,),
        in_specs=[_row_spec(ts, WIDTH_B), _row_spec(ts, WIDTH_B), _row_spec(ts, H1_B)],
        out_specs=[_row_spec(ts, WIDTH_B), _row_spec(ts, WIDTH_B), pl.BlockSpec((H_B, ts, 1), lambda i: (0, i, 0))],
        out_shape=[jax.ShapeDtypeStruct((s, WIDTH_B), BF16), jax.ShapeDtypeStruct((s, WIDTH_B), BF16),
                   jax.ShapeDtypeStruct((H_B, s, 1), F32)],
        compiler_params=_params(("parallel",)),
    )(du, o, h1)


def _h1_bwd(h1, dlat_k, dlat_v, dkr_heads, dqn, dg16, kv_norm, q_norm, cos, sa, sb, *, ts):
    s = h1.shape[0]
    n = s // ts

    def body(h_ref, dk_ref, dv_ref, dkr_ref, dqn_ref, dg_ref, kn_ref, qn_ref, c_ref, sa_ref, sb_ref,
             o_ref, dkn_ref, dqn_out_ref, akn, aqn):
        i = pl.program_id(0)

        @pl.when(i == 0)
        def _():
            akn[...] = jnp.zeros_like(akn)
            aqn[...] = jnp.zeros_like(aqn)

        lat, r = _rms(h_ref[:, C_LAT])
        dlat = dk_ref[...] + dv_ref[...]
        akn[...] += _fold8(dlat * lat)
        o_ref[:, C_LAT] = _rms_bwd(dlat, lat, r, kn_ref[...]).astype(o_ref.dtype)

        dkr = dkr_ref[:, 0:128]
        for hd in range(1, H_B):
            dkr = dkr + dkr_ref[:, hd * 128:(hd + 1) * 128]
        o_ref[:, C_ROPE] = _rope_b(dkr, c_ref[...], sa_ref[...], sb_ref[...], -1.0).astype(o_ref.dtype)

        ql, rq = _rms(h_ref[:, C_QL])
        dq = dqn_ref[...]
        aqn[...] += _fold8(dq * ql)
        o_ref[:, C_QL] = _rms_bwd(dq, ql, rq, qn_ref[...]).astype(o_ref.dtype)
        o_ref[:, C_GATE] = dg_ref[...]

        @pl.when(i == n - 1)
        def _():
            dkn_ref[...] = jnp.sum(akn[...], axis=0, keepdims=True)
            dqn_out_ref[...] = jnp.sum(aqn[...], axis=0, keepdims=True)

    return pl.pallas_call(
        body, name="h1_bwd", grid=(n,),
        in_specs=[_row_spec(ts, H1_B), _row_spec(ts, KV_LORA), _row_spec(ts, KV_LORA), _row_spec(ts, H_B * 128),
                  _row_spec(ts, Q_LORA), _row_spec(ts, WIDTH_B), _bc_spec((1, KV_LORA)), _bc_spec((1, Q_LORA))]
        + [_row_spec(ts, 128)] * 3,
        out_specs=[_row_spec(ts, H1_B), _bc_spec((1, KV_LORA)), _bc_spec((1, Q_LORA))],
        out_shape=[jax.ShapeDtypeStruct((s, H1_B), BF16), jax.ShapeDtypeStruct((1, KV_LORA), F32),
                   jax.ShapeDtypeStruct((1, Q_LORA), F32)],
        scratch_shapes=[pltpu.VMEM((8, KV_LORA), F32), pltpu.VMEM((8, Q_LORA), F32)],
        compiler_params=_params(("arbitrary",)),
    )(h1, dlat_k, dlat_v, dkr_heads, dqn, dg16, kv_norm, q_norm, cos, sa, sb)


def _dot(a, b, ca, cb):
    return lax.dot_general(a, b, (((ca,), (cb,)), ((), ())), preferred_element_type=F32)


def _retention_fwd(q, k, v, tables):
    s = q.shape[0]
    n = s // CHUNK
    dmat, qdec, kdec, cdec = tables

    def body(q_ref, k_ref, v_ref, dm_ref, qd_ref, kd_ref, cd_ref, o_ref, st_ref, state):
        i = pl.program_id(1)

        @pl.when(i == 0)
        def _():
            state[...] = jnp.zeros_like(state)

        qv, kv, vv = q_ref[...], k_ref[...], v_ref[...]
        st16 = state[...].astype(BF16)
        st_ref[...] = st16
        scores = _dot(qv, kv, 1, 1) * dm_ref[0]
        inner = _dot(scores.astype(BF16), vv, 1, 0)
        qd = (qv.astype(F32) * qd_ref[0]).astype(BF16)
        cross = _dot(qd, st16, 1, 0)
        o_ref[...] = inner + cross
        kd = (kv.astype(F32) * kd_ref[0]).astype(BF16)
        state[...] = state[...] * cd_ref[0] + _dot(kd, vv, 0, 0)

    return pl.pallas_call(
        body, name="retention_fwd", grid=(H_A, n),
        in_specs=[pl.BlockSpec((CHUNK, DK_A), lambda h, i: (i, h)), pl.BlockSpec((CHUNK, DK_A), lambda h, i: (i, h)),
                  pl.BlockSpec((CHUNK, DV_A), lambda h, i: (i, h)),
                  pl.BlockSpec((1, CHUNK, CHUNK), lambda h, i: (h, 0, 0)), pl.BlockSpec((1, CHUNK, 1), lambda h, i: (h, 0, 0)),
                  pl.BlockSpec((1, CHUNK, 1), lambda h, i: (h, 0, 0)), pl.BlockSpec((1, 1, DV_A), lambda h, i: (h, 0, 0))],
        out_specs=[pl.BlockSpec((CHUNK, DV_A), lambda h, i: (i, h)),
                   pl.BlockSpec((DK_A, DV_A), lambda h, i: (h * n + i, 0))],
        out_shape=[jax.ShapeDtypeStruct((s, WIDTH_A), F32), jax.ShapeDtypeStruct((H_A * n * DK_A, DV_A), BF16)],
        scratch_shapes=[pltpu.VMEM((DK_A, DV_A), F32)],
        compiler_params=_params(("parallel", "arbitrary")),
    )(q, k, v, dmat, qdec, kdec, cdec)


def _retention_bwd(q, k, v, states, do, tables):
    s = q.shape[0]
    n = s // CHUNK
    dmat, qdec, kdec, cdec = tables

    def body(q_ref, k_ref, v_ref, st_ref, do_ref, dm_ref, qd_ref, kd_ref, cd_ref, dq_ref, dk_ref, dv_ref, grad_state):
        i = pl.program_id(1)

        @pl.when(i == 0)
        def _():
            grad_state[...] = jnp.zeros_like(grad_state)

        qv, kv, vv, dov = q_ref[...], k_ref[...], v_ref[...], do_ref[...]
        dm = dm_ref[0]
        g16 = grad_state[...].astype(BF16)
        scores = (_dot(qv, kv, 1, 1) * dm).astype(BF16)
        dscores = (_dot(dov, vv, 1, 1) * dm).astype(BF16)
        qd = (qv.astype(F32) * qd_ref[0]).astype(BF16)
        kd = (kv.astype(F32) * kd_ref[0]).astype(BF16)
        dq_ref[...] = _dot(dscores, kv, 1, 0) + _dot(dov, st_ref[...], 1, 1) * qd_ref[0]
        dk_ref[...] = _dot(dscores, qv, 0, 0) + _dot(vv, g16, 1, 1) * kd_ref[0]
        dv_ref[...] = (_dot(scores, dov, 0, 0) + _dot(kd, g16, 1, 0)).astype(dv_ref.dtype)
        grad_state[...] = grad_state[...] * cd_ref[0] + _dot(qd, dov, 0, 0)

    rev = lambda h, i: (n - 1 - i, h)
    return pl.pallas_call(
        body, name="retention_bwd", grid=(H_A, n),
        in_specs=[pl.BlockSpec((CHUNK, DK_A), rev), pl.BlockSpec((CHUNK, DK_A), rev), pl.BlockSpec((CHUNK, DV_A), rev),
                  pl.BlockSpec((DK_A, DV_A), lambda h, i: (h * n + n - 1 - i, 0)), pl.BlockSpec((CHUNK, DV_A), rev),
                  pl.BlockSpec((1, CHUNK, CHUNK), lambda h, i: (h, 0, 0)), pl.BlockSpec((1, CHUNK, 1), lambda h, i: (h, 0, 0)),
                  pl.BlockSpec((1, CHUNK, 1), lambda h, i: (h, 0, 0)), pl.BlockSpec((1, 1, DV_A), lambda h, i: (h, 0, 0))],
        out_specs=[pl.BlockSpec((CHUNK, DK_A), rev), pl.BlockSpec((CHUNK, DK_A), rev), pl.BlockSpec((CHUNK, DV_A), rev)],
        out_shape=[jax.ShapeDtypeStruct((s, H_A * DK_A), F32), jax.ShapeDtypeStruct((s, H_A * DK_A), F32),
                   jax.ShapeDtypeStruct((s, WIDTH_A), BF16)],
        scratch_shapes=[pltpu.VMEM((DK_A, DV_A), F32)],
        compiler_params=_params(("parallel", "arbitrary")),
    )(q, k, v, states, do, dmat, qdec, kdec, cdec)


LOG2E = 1.4426950408889634
LN2 = 0.6931471805599453


def _assemble_k(kn, kr, *, ts):
    s = kn.shape[0]

    def body(kn_ref, kr_ref, o_ref):
        r = kr_ref[...]
        for hd in range(H_B):
            o_ref[:, hd * QK_PAD:hd * QK_PAD + 128] = kn_ref[:, hd * 128:(hd + 1) * 128]
            o_ref[:, hd * QK_PAD + 128:(hd + 1) * QK_PAD] = r

    return pl.pallas_call(
        body, name="assemble_k", grid=(s // ts,),
        in_specs=[_row_spec(ts, H_B * 128), _row_spec(ts, 128)],
        out_specs=_row_spec(ts, H_B * QK_PAD),
        out_shape=jax.ShapeDtypeStruct((s, H_B * QK_PAD), kn.dtype),
        compiler_params=_params(("parallel",)),
    )(kn, kr)


def _causal_mask(sc, row0):
    row = lax.broadcasted_iota(jnp.int32, sc.shape, 0) + row0
    col = lax.broadcasted_iota(jnp.int32, sc.shape, 1)
    return jnp.where(col <= row, sc, NEG_BIG)


def _attention_fwd(q, k, v, *, blk, sub):
    s = q.shape[0]
    nb = s // blk
    reps = blk // LANES

    def body(q_ref, k_ref, v_ref, o_ref, lse_ref, vext_s, m_s, acc_s):
        i = pl.program_id(1)

        @pl.when(i == 0)
        def _():
            vext_s[:, 0:V_HEAD] = v_ref[...]
            vext_s[:, V_HEAD:2 * V_HEAD] = jnp.ones((s, V_HEAD), vext_s.dtype)

        m_s[...] = jnp.full_like(m_s, NEG_BIG)
        acc_s[...] = jnp.zeros_like(acc_s)

        def step(j, masked):
            kv_rows = pl.ds(pl.multiple_of(j * blk, blk), blk)
            kb = k_ref[kv_rows, :]
            vb = vext_s[kv_rows, :]
            for r in range(blk // sub):
                rows = slice(r * sub, (r + 1) * sub)
                sc = _dot(q_ref[rows, :], kb, 1, 1)
                if masked:
                    sc = _causal_mask(sc, r * sub)
                m_prev = m_s[rows, :]
                m_new = jnp.maximum(m_prev, jnp.max(sc, axis=-1, keepdims=True))
                p = jnp.exp2(sc - jnp.tile(m_new, (1, reps)))
                a = jnp.exp2(m_prev - m_new)
                acc_s[rows, :] = jnp.tile(a, (1, 2)) * acc_s[rows, :] + _dot(p.astype(BF16), vb, 1, 0)
                m_s[rows, :] = m_new

        def pair_body(jj, carry):
            step(2 * jj, False)
            step(2 * jj + 1, False)
            return carry

        lax.fori_loop(0, i // 2, pair_body, 0)

        @pl.when(i % 2 == 1)
        def _():
            step(i - 1, False)

        step(i, True)
        acc = acc_s[...]
        l = acc[:, V_HEAD:2 * V_HEAD]
        o_ref[...] = acc[:, 0:V_HEAD] / l
        lse_ref[0] = (m_s[...] + jnp.log2(l))[:, 0:1]

    return pl.pallas_call(
        body, name="attention_fwd", grid=(H_B, nb),
        in_specs=[pl.BlockSpec((blk, QK_PAD), lambda h, i: (i, h)), pl.BlockSpec((s, QK_PAD), lambda h, i: (0, h)),
                  pl.BlockSpec((s, V_HEAD), lambda h, i: (0, h))],
        out_specs=[pl.BlockSpec((blk, V_HEAD), lambda h, i: (i, h)), pl.BlockSpec((1, blk, 1), lambda h, i: (h, i, 0))],
        out_shape=[jax.ShapeDtypeStruct((s, WIDTH_B), F32), jax.ShapeDtypeStruct((H_B, s, 1), F32)],
        scratch_shapes=[pltpu.VMEM((s, 2 * V_HEAD), BF16), pltpu.VMEM((blk, LANES), F32), pltpu.VMEM((blk, 2 * V_HEAD), F32)],
        compiler_params=_params(("parallel", "arbitrary")),
    )(q, k, v)


def _attention_bwd(q, k, v, do, lse, delta, *, blk):
    s = q.shape[0]
    nb = s // blk
    reps = blk // LANES

    def body(q_ref, k_ref, v_ref, do_ref, lse_ref, dl_ref, dq_ref, dkn_ref, dkr_ref, dv_ref, lse_s, dl_s):
        i = pl.program_id(1)
        lse_s[...] = jnp.broadcast_to(lse_ref[0], (blk, LANES))
        dl_s[...] = jnp.broadcast_to(dl_ref[0], (blk, LANES))

        def step(j, diagonal):
            kv_rows = pl.ds(pl.multiple_of(j * blk, blk), blk)
            qv, dov, kb = q_ref[...], do_ref[...], k_ref[kv_rows, :]
            sc = _dot(qv, kb, 1, 1)
            if diagonal:
                sc = _causal_mask(sc, 0)
            p = jnp.exp2(sc - jnp.tile(lse_s[...], (1, reps)))
            dp = _dot(dov, v_ref[kv_rows, :], 1, 1)
            ds = (p * (dp - jnp.tile(dl_s[...], (1, reps)))).astype(BF16)
            dv_c = _dot(p.astype(BF16), dov, 0, 0)
            dk_c = _dot(ds, qv, 0, 0)
            dq_c = _dot(ds, kb, 1, 0)
            if diagonal:
                dkn_ref[kv_rows, :] = dk_c[:, 0:128]
                dkr_ref[kv_rows, :] = dk_c[:, 128:256]
                dv_ref[kv_rows, :] = dv_c
                dq_ref[...] = (dq_ref[...] + dq_c) * ATT_SCALE
            else:
                dkn_ref[kv_rows, :] += dk_c[:, 0:128]
                dkr_ref[kv_rows, :] += dk_c[:, 128:256]
                dv_ref[kv_rows, :] += dv_c
                dq_ref[...] += dq_c

        dq_ref[...] = jnp.zeros_like(dq_ref)

        def pair_body(jj, carry):
            step(2 * jj, False)
            step(2 * jj + 1, False)
            return carry

        lax.fori_loop(0, i // 2, pair_body, 0)

        @pl.when(i % 2 == 1)
        def _():
            step(i - 1, False)

        step(i, True)

        @pl.when(i == nb - 1)
        def _():
            dkn_ref[...] = dkn_ref[...] * LN2
            dkr_ref[...] = dkr_ref[...] * LN2

    head = lambda h, i: (0, h)
    blk_idx = lambda h, i: (i, h)
    st_idx = lambda h, i: (h, i, 0)
    return pl.pallas_call(
        body, name="attention_bwd", grid=(H_B, nb),
        in_specs=[pl.BlockSpec((blk, QK_PAD), blk_idx), pl.BlockSpec((s, QK_PAD), head), pl.BlockSpec((s, V_HEAD), head),
                  pl.BlockSpec((blk, V_HEAD), blk_idx), pl.BlockSpec((1, blk, 1), st_idx), pl.BlockSpec((1, blk, 1), st_idx)],
        out_specs=[pl.BlockSpec((blk, QK_PAD), blk_idx), pl.BlockSpec((s, 128), head), pl.BlockSpec((s, 128), head),
                   pl.BlockSpec((s, 128), head)],
        out_shape=[jax.ShapeDtypeStruct((s, H_B * QK_PAD), F32), jax.ShapeDtypeStruct((s, H_B * 128), F32),
                   jax.ShapeDtypeStruct((s, H_B * 128), F32), jax.ShapeDtypeStruct((s, H_B * 128), F32)],
        scratch_shapes=[pltpu.VMEM((blk, LANES), F32), pltpu.VMEM((blk, LANES), F32)],
        compiler_params=_params(("parallel", "arbitrary")),
    )(q, k, v, do, lse, delta)


def _local_step(x, target, w, kv_norm, q_norm, ln_g, ln_b, *, ts=256, blk=512):
    s = x.shape[0]
    cos_a, sin_a = _rope_tables_a(s)
    cos_b, sa_b, sb_b = _rope_tables_b(s)
    tables = _retention_tables()
    g0, g1, b0, b1 = ln_g[0:1], ln_g[1:2], ln_b[0:1], ln_b[1:2]

    h_a = _mm(x, w["a_in"], tn=1536, name="a_in_fwd")
    q_a, k_a, v_a = _rope_a_fwd(h_a, cos_a, sin_a, ts=ts)
    o_a, states = _retention_fwd(q_a, k_a, v_a, tables)
    u_a = _gn_gate_fwd(o_a, h_a, ts=ts)
    y_a = _mm(u_a, w["a_out"], tn=1024, name="a_out_fwd")
    x1, x1_16 = _ln_fwd(x, y_a, g0, b0, ts=ts)

    h1 = _mm(x1_16, w["b_in1"], tn=1152, name="b_in_fwd")
    lat16, kr16, qn16 = _kvq_prep(h1, kv_norm, q_norm, cos_b, sa_b, sb_b, ts=ts)
    kn16 = _mm(lat16, w["up_k"], out_dtype=BF16, tn=1024, name="up_k_fwd")
    v16 = _mm(lat16, w["up_v"], out_dtype=BF16, tn=1024, name="up_v_fwd")
    qf = _mm(qn16, w["uq"], tn=1024, name="uq_fwd")
    q16 = _rope_q(qf, cos_b, sa_b, sb_b, 1.0, ATT_SCALE * LOG2E, BF16, ts=ts, name="rope_q_fwd")
    k16 = _assemble_k(kn16, kr16, ts=ts)
    o_b, lse = _attention_fwd(q16, k16, v16, blk=blk, sub=blk // 2)
    u_b = _gate_b_fwd(o_b, h1, ts=ts)
    y_b = _mm(u_b, w["b_out"], tn=1024, name="b_out_fwd")

    dz_b, dz_b16, dg1, db1, loss = _ln_loss_bwd(x1, y_b, target, g1, b1, ts=ts)
    d_b_out = _mm(u_b, dz_b16, ta=True, tn=1024, tk=1024, name="b_out_dw")
    du_b = _mm(dz_b16, w["b_out"], tb=True, tn=1024, name="b_out_dx")
    do16, dgate16, delta = _gate_b_bwd(du_b, o_b, h1, ts=ts)
    dq, dkn, dkr_heads, dv = _attention_bwd(q16, k16, v16, do16, lse, delta, blk=blk)
    dqf16 = _rope_q(dq, cos_b, sa_b, sb_b, -1.0, 1.0, BF16, ts=ts, name="rope_q_bwd")
    d_uq = _mm(qn16, dqf16, ta=True, tm=768, tn=1024, tk=1024, name="uq_dw")
    dqn = _mm(dqf16, w["uq"], tb=True, tn=768, tk=2048, name="uq_dx")
    d_up_k = _mm(lat16, dkn, ta=True, tn=1024, tk=1024, name="up_k_dw")
    d_up_v = _mm(lat16, dv, ta=True, tn=1024, tk=1024, name="up_v_dw")
    dlat_k = _mm(dkn, w["up_k"], tb=True, tn=512, name="up_k_dx")
    dlat_v = _mm(dv, w["up_v"], tb=True, tn=512, name="up_v_dx")
    dh1, dkvn, dqnorm = _h1_bwd(h1, dlat_k, dlat_v, dkr_heads, dqn, dgate16, kv_norm, q_norm, cos_b, sa_b, sb_b, ts=ts)
    d_b_in1 = _mm(x1_16, dh1, ta=True, tn=1152, tk=1024, name="b_in_dw")
    dx1 = _mm(dh1, w["b_in1"], tb=True, tn=1024, name="b_in_dx")

    dz_a, dz_a16, dg0, db0 = _ln_bwd_call(dz_b, dx1, x, y_a, g0, ts=ts)
    d_a_out = _mm(u_a, dz_a16, ta=True, tn=1024, tk=1024, name="a_out_dw")
    du_a = _mm(dz_a16, w["a_out"], tb=True, tn=1024, name="a_out_dx")
    do_a16, dgate_a16 = _gn_gate_bwd(du_a, o_a, h_a, ts=ts)
    dq_a, dk_a, dv_a16 = _retention_bwd(q_a, k_a, v_a, states, do_a16, tables)
    dh_a = _rope_a_bwd(dq_a, dk_a, dv_a16, dgate_a16, cos_a, sin_a, ts=ts)
    d_a_in = _mm(x, dh_a, ta=True, tn=1536, tk=1024, name="a_in_dw")
    dx_a = _mm(dh_a, w["a_in"], tb=True, tn=1024, tk=2048, name="a_in_dx")
    grad_x = _residual_grad(dz_a, dx_a, ts=ts)

    grads = dict(a_in=d_a_in, a_out=d_a_out, b_in1=d_b_in1, uq=d_uq, b_out=d_b_out, up_k=d_up_k, up_v=d_up_v)
    small = dict(ln_g=jnp.concatenate([dg0, dg1], axis=0), ln_b=jnp.concatenate([db0, db1], axis=0),
                 q_norm=dqnorm, kv_norm=dkvn)
    return loss, grad_x, grads, small


def _flat_shards(shards, dtype):
    parts = [shards[name].reshape(rows, FLAT_COLS) for name, rows in SHARD_ROWS]
    used = sum(rows for _, rows in SHARD_ROWS)
    parts.append(jnp.zeros((FLAT_ROWS - used, FLAT_COLS), shards["a_w_in"].dtype))
    return jnp.concatenate(parts, axis=0).astype(dtype)


def _unflat_shards(flat, shapes):
    out, off = {}, 0
    for name, rows in SHARD_ROWS:
        out[name] = flat[off:off + rows].reshape(shapes[name])
        off += rows
    return out


COL_SHARDED = {"a_w_in": (D_MODEL, IN_A), "b_w_in": (D_MODEL, IN_B), "b_w_uq": (Q_LORA, H_B * (QK_NOPE + QK_ROPE)),
               "kv_w_up": (KV_LORA, H_B * (QK_NOPE + V_HEAD))}
ROW_SHARDED = {"a_w_out": (WIDTH_A, D_MODEL), "b_w_out": (WIDTH_B, D_MODEL), "kv_w_down": (D_MODEL, KV_LORA + QK_ROPE)}


def _full_from_gathered(gathered):
    out, off = {}, 0
    for name, rows in SHARD_ROWS:
        part = gathered[:, off:off + rows]
        off += rows
        if name in COL_SHARDED:
            r, c = COL_SHARDED[name]
            out[name] = part.reshape(N_CHIPS, r, c // N_CHIPS).transpose(1, 0, 2).reshape(r, c)
        else:
            r, c = ROW_SHARDED[name]
            out[name] = part.reshape(r, c)
    return out


def _gathered_from_full(full):
    parts = []
    for name, rows in SHARD_ROWS:
        g = full[name]
        if name in COL_SHARDED:
            r, c = COL_SHARDED[name]
            g = g.reshape(r, N_CHIPS, c // N_CHIPS).transpose(1, 0, 2)
        parts.append(g.reshape(N_CHIPS, rows, FLAT_COLS))
    used = sum(rows for _, rows in SHARD_ROWS)
    parts.append(jnp.zeros((N_CHIPS, FLAT_ROWS - used, FLAT_COLS), F32))
    return jnp.concatenate(parts, axis=1)


def _kernel_layout(full):
    uq = full["b_w_uq"].reshape(Q_LORA, H_B, QK_NOPE + QK_ROPE)
    uq = jnp.pad(uq, ((0, 0), (0, 0), (0, QK_PAD - QK_NOPE - QK_ROPE))).reshape(Q_LORA, H_B * QK_PAD)
    up = full["kv_w_up"].reshape(KV_LORA, H_B, QK_NOPE + V_HEAD)
    down = jnp.pad(full["kv_w_down"], ((0, 0), (0, KV_DOWN_PAD - KV_LORA - QK_ROPE)))
    return dict(a_in=full["a_w_in"], a_out=full["a_w_out"], b_out=full["b_w_out"], uq=uq,
                up_k=up[:, :, :QK_NOPE].reshape(KV_LORA, H_B * QK_NOPE),
                up_v=up[:, :, QK_NOPE:].reshape(KV_LORA, H_B * V_HEAD),
                b_in1=jnp.concatenate([down, full["b_w_in"]], axis=1))


def _reference_layout_grads(g):
    uq = g["uq"].reshape(Q_LORA, H_B, QK_PAD)[:, :, :QK_NOPE + QK_ROPE].reshape(Q_LORA, H_B * (QK_NOPE + QK_ROPE))
    up = jnp.concatenate([g["up_k"].reshape(KV_LORA, H_B, QK_NOPE), g["up_v"].reshape(KV_LORA, H_B, V_HEAD)], axis=2)
    return dict(a_w_in=g["a_in"], a_w_out=g["a_out"], b_w_out=g["b_out"], b_w_uq=uq,
                kv_w_up=up.reshape(KV_LORA, H_B * (QK_NOPE + V_HEAD)),
                kv_w_down=g["b_in1"][:, :KV_LORA + QK_ROPE], b_w_in=g["b_in1"][:, KV_DOWN_PAD:])


HBM_SPEC = pl.BlockSpec(memory_space=pl.ANY)


def _me():
    return lax.axis_index("x"), lax.axis_index("y"), lax.axis_index("c")


def _chip_flips(x, y):
    return [(1 - x, y), (x, 1 - y), (1 - x, 1 - y)]


def _gather_weights(flat16):
    def body(src_ref, out_ref, send_sems, recv_sems, local_sem):
        x, y, c = _me()
        mine = 2 * x + y
        local = pltpu.make_async_copy(src_ref, out_ref.at[mine], local_sem)
        local.start()
        copies = []
        for k, (px, py) in enumerate(_chip_flips(x, y)):
            copies.append(pltpu.make_async_remote_copy(
                src_ref=src_ref, dst_ref=out_ref.at[mine], send_sem=send_sems.at[k], recv_sem=recv_sems.at[k],
                device_id=(px, py, c), device_id_type=MESH))
        for cp in copies:
            cp.start()
        for cp in copies:
            cp.wait_send()
        for k, (px, py) in enumerate(_chip_flips(x, y)):
            pltpu.make_async_remote_copy(
                src_ref=src_ref, dst_ref=out_ref.at[2 * px + py], send_sem=send_sems.at[k], recv_sem=recv_sems.at[k],
                device_id=(px, py, c), device_id_type=MESH).wait_recv()
        local.wait()

    return pl.pallas_call(
        body, name="gather_weights",
        in_specs=[HBM_SPEC], out_specs=HBM_SPEC,
        out_shape=jax.ShapeDtypeStruct((N_CHIPS,) + flat16.shape, flat16.dtype),
        scratch_shapes=[pltpu.SemaphoreType.DMA((3,)), pltpu.SemaphoreType.DMA((3,)), pltpu.SemaphoreType.DMA],
    )(flat16)


def _pair_exchange_halves(g):
    def body(g_ref, out_ref, send_sem, recv_sem):
        x, y, c = _me()
        theirs = g_ref.at[:, pl.ds(pl.multiple_of((1 - c) * HALF_ROWS, 8), HALF_ROWS), :]
        cp = pltpu.make_async_remote_copy(src_ref=theirs, dst_ref=out_ref, send_sem=send_sem, recv_sem=recv_sem,
                                          device_id=(x, y, 1 - c), device_id_type=MESH)
        cp.start()
        cp.wait_send()
        cp.wait_recv()

    return pl.pallas_call(
        body, name="pair_exchange_halves",
        in_specs=[HBM_SPEC], out_specs=HBM_SPEC,
        out_shape=jax.ShapeDtypeStruct((N_CHIPS, HALF_ROWS, FLAT_COLS), g.dtype),
        scratch_shapes=[pltpu.SemaphoreType.DMA, pltpu.SemaphoreType.DMA],
    )(g)


def _chip_exchange(p):
    def body(p_ref, out_ref, send_sems, recv_sems):
        x, y, c = _me()
        copies = []
        for k, (px, py) in enumerate(_chip_flips(x, y)):
            copies.append(pltpu.make_async_remote_copy(
                src_ref=p_ref.at[2 * px + py], dst_ref=out_ref.at[k], send_sem=send_sems.at[k], recv_sem=recv_sems.at[k],
                device_id=(px, py, c), device_id_type=MESH))
        for cp in copies:
            cp.start()
        for cp in copies:
            cp.wait_send()
        for cp in copies:
            cp.wait_recv()

    return pl.pallas_call(
        body, name="chip_exchange",
        in_specs=[HBM_SPEC], out_specs=HBM_SPEC,
        out_shape=jax.ShapeDtypeStruct((3, HALF_ROWS, FLAT_COLS), p.dtype),
        scratch_shapes=[pltpu.SemaphoreType.DMA((3,)), pltpu.SemaphoreType.DMA((3,))],
    )(p)


def _pair_share(r):
    def body(r_ref, out_ref, send_sem, recv_sem, local_sem):
        x, y, c = _me()
        my_rows = pl.ds(pl.multiple_of(c * HALF_ROWS, 8), HALF_ROWS)
        local = pltpu.make_async_copy(r_ref, out_ref.at[my_rows, :], local_sem)
        local.start()
        cp = pltpu.make_async_remote_copy(src_ref=r_ref, dst_ref=out_ref.at[my_rows, :], send_sem=send_sem,
                                          recv_sem=recv_sem, device_id=(x, y, 1 - c), device_id_type=MESH)
        cp.start()
        cp.wait_send()
        their_rows = pl.ds(pl.multiple_of((1 - c) * HALF_ROWS, 8), HALF_ROWS)
        pltpu.make_async_remote_copy(src_ref=r_ref, dst_ref=out_ref.at[their_rows, :], send_sem=send_sem,
                                     recv_sem=recv_sem, device_id=(x, y, 1 - c), device_id_type=MESH).wait_recv()
        local.wait()

    return pl.pallas_call(
        body, name="pair_share",
        in_specs=[HBM_SPEC], out_specs=HBM_SPEC,
        out_shape=jax.ShapeDtypeStruct((FLAT_ROWS, FLAT_COLS), r.dtype),
        scratch_shapes=[pltpu.SemaphoreType.DMA, pltpu.SemaphoreType.DMA, pltpu.SemaphoreType.DMA],
    )(r)


def _sum_small(vec):
    def body(v_ref, out_ref, slots, send_sems, recv_sems):
        x, y, c = _me()
        me = 4 * x + 2 * y + c
        slots[me] = v_ref[...]
        flips = [(fx, fy, fc) for fx in (0, 1) for fy in (0, 1) for fc in (0, 1)][1:]
        copies = []
        for k, (fx, fy, fc) in enumerate(flips):
            copies.append(pltpu.make_async_remote_copy(
                src_ref=v_ref, dst_ref=slots.at[me], send_sem=send_sems.at[k], recv_sem=recv_sems.at[k],
                device_id=(x ^ fx, y ^ fy, c ^ fc), device_id_type=MESH))
        for cp in copies:
            cp.start()
        for cp in copies:
            cp.wait_send()
        for k, (fx, fy, fc) in enumerate(flips):
            src = 4 * (x ^ fx) + 2 * (y ^ fy) + (c ^ fc)
            pltpu.make_async_remote_copy(
                src_ref=v_ref, dst_ref=slots.at[src], send_sem=send_sems.at[k], recv_sem=recv_sems.at[k],
                device_id=(x ^ fx, y ^ fy, c ^ fc), device_id_type=MESH).wait_recv()
        total = slots[0]
        for d in range(1, N_DEV):
            total = total + slots[d]
        out_ref[...] = total

    return pl.pallas_call(
        body, name="sum_small",
        in_specs=[pl.BlockSpec(memory_space=pltpu.VMEM)], out_specs=pl.BlockSpec(memory_space=pltpu.VMEM),
        out_shape=jax.ShapeDtypeStruct(vec.shape, vec.dtype),
        scratch_shapes=[pltpu.VMEM((N_DEV,) + vec.shape, vec.dtype), pltpu.SemaphoreType.DMA((7,)),
                        pltpu.SemaphoreType.DMA((7,))],
    )(vec)


UPD_ROWS = 256


def _pair_sum(g, theirs, c_arr):
    nb = HALF_ROWS // UPD_ROWS

    def body(c_ref, g_ref, t_ref, o_ref):
        o_ref[...] = g_ref[...] + t_ref[...]

    return pl.pallas_call(
        body, name="pair_sum",
        grid_spec=pltpu.PrefetchScalarGridSpec(
            num_scalar_prefetch=1, grid=(N_CHIPS, nb),
            in_specs=[pl.BlockSpec((1, UPD_ROWS, FLAT_COLS), lambda d, i, c_ref: (d, c_ref[0] * nb + i, 0)),
                      pl.BlockSpec((1, UPD_ROWS, FLAT_COLS), lambda d, i, c_ref: (d, i, 0))],
            out_specs=pl.BlockSpec((1, UPD_ROWS, FLAT_COLS), lambda d, i, c_ref: (d, i, 0))),
        out_shape=jax.ShapeDtypeStruct((N_CHIPS, HALF_ROWS, FLAT_COLS), F32),
        compiler_params=_params(("parallel", "parallel")),
    )(c_arr, g, theirs)


def _chip_sum(p, received, chip_arr):
    nb = HALF_ROWS // UPD_ROWS

    def body(chip_ref, p_ref, r_ref, o_ref):
        o_ref[...] = ((p_ref[0] + r_ref[0]) + r_ref[1]) + r_ref[2]

    return pl.pallas_call(
        body, name="chip_sum",
        grid_spec=pltpu.PrefetchScalarGridSpec(
            num_scalar_prefetch=1, grid=(nb,),
            in_specs=[pl.BlockSpec((1, UPD_ROWS, FLAT_COLS), lambda i, chip_ref: (chip_ref[0], i, 0)),
                      pl.BlockSpec((3, UPD_ROWS, FLAT_COLS), lambda i, chip_ref: (0, i, 0))],
            out_specs=pl.BlockSpec((UPD_ROWS, FLAT_COLS), lambda i, chip_ref: (i, 0))),
        out_shape=jax.ShapeDtypeStruct((HALF_ROWS, FLAT_COLS), F32),
        compiler_params=_params(("parallel",)),
    )(chip_arr, p, received)


def _adamw(w, g, m, v, *, rows, name):
    r, c = w.shape
    rows = min(rows, r)
    assert r % rows == 0

    def body(w_ref, g_ref, m_ref, v_ref, d_ref, nm_ref, nv_ref):
        gv = g_ref[...]
        nm = ADAM_B1 * m_ref[...] + (1.0 - ADAM_B1) * gv
        nv = ADAM_B2 * v_ref[...] + (1.0 - ADAM_B2) * (gv * gv)
        m_hat = nm / (1.0 - ADAM_B1 ** ADAM_STEP)
        v_hat = nv / (1.0 - ADAM_B2 ** ADAM_STEP)
        d_ref[...] = -ADAM_LR * (m_hat / (jnp.sqrt(v_hat) + ADAM_EPS) + ADAM_WD * w_ref[...])
        nm_ref[...] = nm
        nv_ref[...] = nv

    spec = pl.BlockSpec((rows, c), lambda i: (i, 0))
    return pl.pallas_call(
        body, name=name, grid=(r // rows,),
        in_specs=[spec] * 4, out_specs=[spec] * 3,
        out_shape=[jax.ShapeDtypeStruct((r, c), F32)] * 3,
        compiler_params=_params(("parallel",)),
    )(w, g, m, v)


W_NAMES = ("a_w_in", "a_w_out", "b_w_in", "b_q_norm", "b_w_uq", "b_w_out", "kv_w_down", "kv_norm", "kv_w_up", "ln_g", "ln_b")
BIG = tuple(name for name, _ in SHARD_ROWS)


def _pack_small(ln_g, ln_b, q_norm, kv_norm, extra=None):
    pad = lambda a: jnp.pad(a.reshape(1, -1), ((0, 0), (0, FLAT_COLS - a.size)))
    rows = [ln_g, ln_b, pad(q_norm), pad(kv_norm),
            jnp.zeros((1, FLAT_COLS), F32) if extra is None else pad(extra), jnp.zeros((1, FLAT_COLS), F32)]
    return jnp.concatenate(rows, axis=0)


def _unpack_small(p):
    return dict(ln_g=p[0:2], ln_b=p[2:4], b_q_norm=p[4:5, :Q_LORA], kv_norm=p[5, :KV_LORA])


def kernel(x, a_w_in, a_w_out, b_w_in, b_q_norm, b_w_uq, b_w_out, kv_w_down, kv_norm, kv_w_up, ln_g, ln_b, loss_target, m_a_w_in, m_a_w_out, m_b_w_in, m_b_q_norm, m_b_w_uq, m_b_w_out, m_kv_w_down, m_kv_norm, m_kv_w_up, m_ln_g, m_ln_b, v_a_w_in, v_a_w_out, v_b_w_in, v_b_q_norm, v_b_w_uq, v_b_w_out, v_kv_w_down, v_kv_norm, v_kv_w_up, v_ln_g, v_ln_b):
    w_in = dict(a_w_in=a_w_in[0], a_w_out=a_w_out[0], b_w_in=b_w_in[0], b_w_uq=b_w_uq[0], b_w_out=b_w_out[0],
                kv_w_down=kv_w_down, kv_w_up=kv_w_up)
    m_in = dict(a_w_in=m_a_w_in[0], a_w_out=m_a_w_out[0], b_w_in=m_b_w_in[0], b_w_uq=m_b_w_uq[0], b_w_out=m_b_w_out[0],
                kv_w_down=m_kv_w_down, kv_w_up=m_kv_w_up)
    v_in = dict(a_w_in=v_a_w_in[0], a_w_out=v_a_w_out[0], b_w_in=v_b_w_in[0], b_w_uq=v_b_w_uq[0], b_w_out=v_b_w_out[0],
                kv_w_down=v_kv_w_down, kv_w_up=v_kv_w_up)
    shard_shapes = {name: w_in[name].shape for name in BIG}

    gathered = _gather_weights(_flat_shards(w_in, BF16))
    weights = _kernel_layout(_full_from_gathered(gathered))

    loss, grad_x, grads, small = _local_step(x[0], loss_target[0], weights, kv_norm.reshape(1, -1), b_q_norm, ln_g, ln_b)

    cx, cy, cc = lax.axis_index("x"), lax.axis_index("y"), lax.axis_index("c")
    g_all = _gathered_from_full(_reference_layout_grads(grads))
    theirs = _pair_exchange_halves(g_all)
    pair = _pair_sum(g_all, theirs, cc.astype(jnp.int32).reshape(1))
    received = _chip_exchange(pair)
    half = _chip_sum(pair, received, (2 * cx + cy).astype(jnp.int32).reshape(1))
    g_flat = _pair_share(half)

    small_sum = _sum_small(_pack_small(small["ln_g"], small["ln_b"], small["q_norm"], small["kv_norm"], loss[:, :1]))
    loss_out = small_sum[6, 0]
    g_small = _unpack_small(small_sum)

    d_flat, nm_flat, nv_flat = _adamw(_flat_shards(w_in, F32), g_flat, _flat_shards(m_in, F32), _flat_shards(v_in, F32),
                                      rows=512, name="adamw_shards")
    ds, nms, nvs = _adamw(_pack_small(ln_g, ln_b, b_q_norm, kv_norm), small_sum.at[6].set(0.0),
                          _pack_small(m_ln_g, m_ln_b, m_b_q_norm, m_kv_norm),
                          _pack_small(v_ln_g, v_ln_b, v_b_q_norm, v_kv_norm), rows=8, name="adamw_small")

    def assemble(flat, small_packed):
        big = _unflat_shards(flat, shard_shapes)
        sm = _unpack_small(small_packed)
        out = {}
        for name in W_NAMES:
            if name in big:
                out[name] = big[name][None] if name in ("a_w_in", "a_w_out", "b_w_in", "b_w_uq", "b_w_out") else big[name]
            else:
                out[name] = sm[name]
        return [out[name] for name in W_NAMES]

    grad_list = assemble(g_flat, small_sum)
    return (loss_out, grad_x[None], *grad_list, *assemble(d_flat, ds), *assemble(nm_flat, nms), *assemble(nv_flat, nvs))
```

```python
import functools
import math

import jax
import jax.numpy as jnp
from jax import lax
from jax.experimental import pallas as pl
from jax.experimental.pallas import tpu as pltpu

F32 = jnp.float32
BF16 = jnp.bfloat16
MESH = pl.DeviceIdType.MESH

D_MODEL = 1024
DEPTH = 2
H_A, DK_A, DV_A = 4, 256, 512
WIDTH_A = H_A * DV_A
CHUNK = 128
H_B, QK_NOPE, QK_ROPE, V_HEAD = 16, 128, 64, 128
QK_PAD = 256
Q_LORA, KV_LORA = 768, 512
KV_DOWN_PAD = 640
WIDTH_B = H_B * V_HEAD
IN_A = 2 * H_A * DK_A + 2 * WIDTH_A
IN_B = Q_LORA + WIDTH_B
H1_B = KV_DOWN_PAD + IN_B
ROPE_BASE = 10000.0
ALPHA = (2.0 * DEPTH) ** 0.25
ATT_SCALE = (QK_NOPE + QK_ROPE) ** -0.5
NEG_BIG = -1e30

ADAM_LR, ADAM_B1, ADAM_B2, ADAM_EPS, ADAM_WD, ADAM_STEP = 0.001, 0.9, 0.999, 1e-08, 0.01, 10

VMEM_LIMIT_BYTES = 56 * 1024 * 1024
LANES = 128
FLAT_COLS = 1024
SHARD_ROWS = (("a_w_in", 1536), ("a_w_out", 512), ("b_w_in", 704), ("b_w_uq", 576), ("b_w_out", 512),
              ("kv_w_down", 144), ("kv_w_up", 512))
FLAT_ROWS = 4608
HALF_ROWS = FLAT_ROWS // 2
N_CHIPS = 4
N_DEV = 8


def _params(sem, vmem=VMEM_LIMIT_BYTES):
    return pltpu.CompilerParams(dimension_semantics=sem, vmem_limit_bytes=vmem)


def _row_spec(ts, w, col_block=0):
    return pl.BlockSpec((ts, w), lambda i: (i, col_block))


def _bc_spec(shape):
    nd = len(shape)
    return pl.BlockSpec(shape, lambda i: (0,) * nd)


def _sigmoid(x):
    return 1.0 / (1.0 + jnp.exp(-x))


def _fold8(v):
    ts, w = v.shape
    return jnp.sum(v.reshape(ts // 8, 8, w), axis=0)


def _mm(a, b, *, ta=False, tb=False, out_dtype=F32, tm=1024, tn=512, tk=None, name):
    if ta:
        K, M = a.shape
    else:
        M, K = a.shape
    if tb:
        N, Kb = b.shape
    else:
        Kb, N = b.shape
    assert K == Kb, (a.shape, b.shape)
    tm, tn = min(tm, M), min(tn, N)
    tk = K if tk is None else min(tk, K)
    assert M % tm == 0 and N % tn == 0 and K % tk == 0, (name, M, N, K, tm, tn, tk)
    nk = K // tk
    dims = (((0,) if ta else (1,), (1,) if tb else (0,)), ((), ()))

    def body(a_ref, b_ref, o_ref, *scratch):
        prod = lax.dot_general(a_ref[...].astype(BF16), b_ref[...].astype(BF16), dims,
                               preferred_element_type=F32)
        if nk == 1:
            o_ref[...] = prod.astype(o_ref.dtype)
        else:
            acc, = scratch
            k = pl.program_id(2)

            @pl.when(k == 0)
            def _():
                acc[...] = prod

            @pl.when(k > 0)
            def _():
                acc[...] += prod

            @pl.when(k == nk - 1)
            def _():
                o_ref[...] = acc[...].astype(o_ref.dtype)

    a_spec = pl.BlockSpec((tk, tm), lambda i, j, k: (k, i)) if ta else pl.BlockSpec((tm, tk), lambda i, j, k: (i, k))
    b_spec = pl.BlockSpec((tn, tk), lambda i, j, k: (j, k)) if tb else pl.BlockSpec((tk, tn), lambda i, j, k: (k, j))
    return pl.pallas_call(
        body, name=name,
        grid=(M // tm, N // tn, nk),
        in_specs=[a_spec, b_spec],
        out_specs=pl.BlockSpec((tm, tn), lambda i, j, k: (i, j)),
        out_shape=jax.ShapeDtypeStruct((M, N), out_dtype),
        scratch_shapes=[] if nk == 1 else [pltpu.VMEM((tm, tn), F32)],
        compiler_params=_params(("parallel", "parallel", "arbitrary")),
    )(a, b)


def _rope_tables_a(s):
    half = DK_A // 2
    inv = ROPE_BASE ** (-jnp.arange(half, dtype=F32) / half)
    ang = jnp.arange(s, dtype=F32)[:, None] * inv[None, :]
    return jnp.cos(ang), jnp.sin(ang)


def _rope_tables_b(s):
    half = QK_ROPE // 2
    inv = ROPE_BASE ** (-jnp.arange(half, dtype=F32) / half)
    ang = jnp.arange(s, dtype=F32)[:, None] * inv[None, :]
    c, sn = jnp.cos(ang), jnp.sin(ang)
    z = jnp.zeros_like(c)
    cos = jnp.concatenate([c, c, z, z], axis=1)
    sa = jnp.concatenate([-sn, z, z, z], axis=1)
    sb = jnp.concatenate([z, sn, z, z], axis=1)
    return cos, sa, sb


def _rope_b(r, cos, sa, sb, sign):
    return r * cos + sign * (pltpu.roll(r, 96, 1) * sa + pltpu.roll(r, 32, 1) * sb)


def _retention_tables():
    lg = jnp.log1p(-jnp.exp2(-5.0 - jnp.arange(H_A, dtype=F32)))
    idx = jnp.arange(CHUNK, dtype=F32)
    diff = idx[:, None] - idx[None, :]
    causal = diff >= 0
    dmat = jnp.where(causal, jnp.exp(jnp.where(causal, diff, 0.0)[None] * lg[:, None, None]), 0.0)
    qdec = jnp.exp((idx + 1.0)[None, :] * lg[:, None])[:, :, None]
    kdec = jnp.exp((CHUNK - 1.0 - idx)[None, :] * lg[:, None])[:, :, None]
    cdec = jnp.broadcast_to(jnp.exp(CHUNK * lg)[:, None, None], (H_A, 1, DV_A))
    return dmat, qdec, kdec, cdec


def _rope_a_fwd(h_a, cos, sin, *, ts):
    s = h_a.shape[0]

    def body(h_ref, c_ref, s_ref, q_ref, k_ref, v_ref):
        c, sn = c_ref[...], s_ref[...]
        for o_ref, base, scale in ((q_ref, 0, 1.0), (k_ref, H_A * DK_A, DK_A ** -0.5)):
            for hd in range(H_A):
                lo = hd * DK_A
                x1 = h_ref[:, base + lo:base + lo + 128]
                x2 = h_ref[:, base + lo + 128:base + lo + 256]
                o_ref[:, lo:lo + 128] = ((x1 * c - x2 * sn) * scale).astype(o_ref.dtype)
                o_ref[:, lo + 128:lo + 256] = ((x2 * c + x1 * sn) * scale).astype(o_ref.dtype)
        v_ref[...] = h_ref[:, 2 * H_A * DK_A:2 * H_A * DK_A + WIDTH_A].astype(v_ref.dtype)

    return pl.pallas_call(
        body, name="rope_a_fwd", grid=(s // ts,),
        in_specs=[_row_spec(ts, 4096), _row_spec(ts, 128), _row_spec(ts, 128)],
        out_specs=[_row_spec(ts, 1024), _row_spec(ts, 1024), _row_spec(ts, 2048)],
        out_shape=[jax.ShapeDtypeStruct((s, 1024), BF16), jax.ShapeDtypeStruct((s, 1024), BF16),
                   jax.ShapeDtypeStruct((s, 2048), BF16)],
        compiler_params=_params(("parallel",)),
    )(h_a, cos, sin)


def _rope_a_bwd(dq, dk, dv16, dg16, cos, sin, *, ts):
    s = dq.shape[0]

    def body(dq_ref, dk_ref, dv_ref, dg_ref, c_ref, s_ref, o_ref):
        c, sn = c_ref[...], s_ref[...]
        for d_ref, base, scale in ((dq_ref, 0, 1.0), (dk_ref, H_A * DK_A, DK_A ** -0.5)):
            for hd in range(H_A):
                lo = hd * DK_A
                d1 = d_ref[:, lo:lo + 128] * scale
                d2 = d_ref[:, lo + 128:lo + 256] * scale
                o_ref[:, base + lo:base + lo + 128] = (d1 * c + d2 * sn).astype(o_ref.dtype)
                o_ref[:, base + lo + 128:base + lo + 256] = (d2 * c - d1 * sn).astype(o_ref.dtype)
        o_ref[:, 2048:4096] = dv_ref[...]
        o_ref[:, 4096:6144] = dg_ref[...]

    return pl.pallas_call(
        body, name="rope_a_bwd", grid=(s // ts,),
        in_specs=[_row_spec(ts, 1024), _row_spec(ts, 1024), _row_spec(ts, 2048), _row_spec(ts, 2048),
                  _row_spec(ts, 128), _row_spec(ts, 128)],
        out_specs=_row_spec(ts, IN_A),
        out_shape=jax.ShapeDtypeStruct((s, IN_A), BF16),
        compiler_params=_params(("parallel",)),
    )(dq, dk, dv16, dg16, cos, sin)


def _gn_gate_fwd(o, h_a, *, ts):
    s = o.shape[0]

    def body(o_ref, g_ref, u_ref):
        for hd in range(H_A):
            sl = slice(hd * DV_A, (hd + 1) * DV_A)
            ov = o_ref[:, sl]
            mu = jnp.mean(ov, axis=-1, keepdims=True)
            oc = ov - mu
            var = jnp.mean(oc * oc, axis=-1, keepdims=True)
            on = oc * lax.rsqrt(var + 1e-5)
            g = g_ref[:, sl]
            u_ref[:, sl] = (on * (g * _sigmoid(g))).astype(u_ref.dtype)

    return pl.pallas_call(
        body, name="gn_gate_fwd", grid=(s // ts,),
        in_specs=[_row_spec(ts, WIDTH_A), _row_spec(ts, WIDTH_A, 2)],
        out_specs=_row_spec(ts, WIDTH_A),
        out_shape=jax.ShapeDtypeStruct((s, WIDTH_A), BF16),
        compiler_params=_params(("parallel",)),
    )(o, h_a)


def _gn_gate_bwd(du, o, h_a, *, ts):
    s = o.shape[0]

    def body(du_ref, o_ref, g_ref, do_ref, dg_ref):
        for hd in range(H_A):
            sl = slice(hd * DV_A, (hd + 1) * DV_A)
            ov = o_ref[:, sl]
            mu = jnp.mean(ov, axis=-1, keepdims=True)
            oc = ov - mu
            var = jnp.mean(oc * oc, axis=-1, keepdims=True)
            rstd = lax.rsqrt(var + 1e-5)
            on = oc * rstd
            g = g_ref[:, sl]
            sg = _sigmoid(g)
            du = du_ref[:, sl]
            don = du * (g * sg)
            dg_ref[:, sl] = (du * on * (sg * (1.0 + g * (1.0 - sg)))).astype(dg_ref.dtype)
            m1 = jnp.mean(don, axis=-1, keepdims=True)
            m2 = jnp.mean(don * on, axis=-1, keepdims=True)
            do_ref[:, sl] = (rstd * (don - m1 - on * m2)).astype(do_ref.dtype)

    return pl.pallas_call(
        body, name="gn_gate_bwd", grid=(s // ts,),
        in_specs=[_row_spec(ts, WIDTH_A), _row_spec(ts, WIDTH_A), _row_spec(ts, WIDTH_A, 2)],
        out_specs=[_row_spec(ts, WIDTH_A), _row_spec(ts, WIDTH_A)],
        out_shape=[jax.ShapeDtypeStruct((s, WIDTH_A), BF16), jax.ShapeDtypeStruct((s, WIDTH_A), BF16)],
        compiler_params=_params(("parallel",)),
    )(du, o, h_a)


def _ln_stats(z):
    mu = jnp.mean(z, axis=-1, keepdims=True)
    zc = z - mu
    var = jnp.mean(zc * zc, axis=-1, keepdims=True)
    rstd = lax.rsqrt(var + 1e-5)
    return zc * rstd, rstd


def _ln_bwd(dy, xhat, rstd, g):
    dxh = dy * g
    m1 = jnp.mean(dxh, axis=-1, keepdims=True)
    m2 = jnp.mean(dxh * xhat, axis=-1, keepdims=True)
    return rstd * (dxh - m1 - xhat * m2)


def _ln_fwd(x, y, g, b, *, ts):
    s = x.shape[0]

    def body(x_ref, y_ref, g_ref, b_ref, o_ref, o16_ref):
        xhat, _ = _ln_stats(ALPHA * x_ref[...] + y_ref[...])
        out = xhat * g_ref[...] + b_ref[...]
        o_ref[...] = out
        o16_ref[...] = out.astype(o16_ref.dtype)

    return pl.pallas_call(
        body, name="ln_fwd", grid=(s // ts,),
        in_specs=[_row_spec(ts, D_MODEL), _row_spec(ts, D_MODEL), _bc_spec((1, D_MODEL)), _bc_spec((1, D_MODEL))],
        out_specs=[_row_spec(ts, D_MODEL), _row_spec(ts, D_MODEL)],
        out_shape=[jax.ShapeDtypeStruct((s, D_MODEL), F32), jax.ShapeDtypeStruct((s, D_MODEL), BF16)],
        compiler_params=_params(("parallel",)),
    )(x, y, g, b)


def _ln_loss_bwd(x1, y, target, g, b, *, ts):
    s = x1.shape[0]
    n = s // ts

    def body(x_ref, y_ref, t_ref, g_ref, b_ref, dz_ref, dz16_ref, dg_ref, db_ref, loss_ref, ag, ab, al):
        i = pl.program_id(0)

        @pl.when(i == 0)
        def _():
            ag[...] = jnp.zeros_like(ag)
            ab[...] = jnp.zeros_like(ab)
            al[...] = jnp.zeros_like(al)

        xhat, rstd = _ln_stats(ALPHA * x_ref[...] + y_ref[...])
        err = xhat * g_ref[...] + b_ref[...] - t_ref[...]
        al[...] += _fold8(err * err)
        dy = err * (1.0 / D_MODEL)
        ag[...] += _fold8(dy * xhat)
        ab[...] += _fold8(dy)
        dz = _ln_bwd(dy, xhat, rstd, g_ref[...])
        dz_ref[...] = dz
        dz16_ref[...] = dz.astype(dz16_ref.dtype)

        @pl.when(i == n - 1)
        def _():
            dg_ref[...] = jnp.sum(ag[...], axis=0, keepdims=True)
            db_ref[...] = jnp.sum(ab[...], axis=0, keepdims=True)
            loss_ref[...] = jnp.full((1, LANES), (0.5 / D_MODEL) * jnp.sum(al[...]), F32)

    return pl.pallas_call(
        body, name="ln_loss_bwd", grid=(n,),
        in_specs=[_row_spec(ts, D_MODEL)] * 3 + [_bc_spec((1, D_MODEL))] * 2,
        out_specs=[_row_spec(ts, D_MODEL), _row_spec(ts, D_MODEL), _bc_spec((1, D_MODEL)), _bc_spec((1, D_MODEL)),
                   _bc_spec((1, LANES))],
        out_shape=[jax.ShapeDtypeStruct((s, D_MODEL), F32), jax.ShapeDtypeStruct((s, D_MODEL), BF16),
                   jax.ShapeDtypeStruct((1, D_MODEL), F32), jax.ShapeDtypeStruct((1, D_MODEL), F32),
                   jax.ShapeDtypeStruct((1, LANES), F32)],
        scratch_shapes=[pltpu.VMEM((8, D_MODEL), F32)] * 3,
        compiler_params=_params(("arbitrary",)),
    )(x1, y, target, g, b)


def _ln_bwd_call(dz_next, dx_branch, x, y, g, *, ts):
    s = x.shape[0]
    n = s // ts

    def body(dzn_ref, dxb_ref, x_ref, y_ref, g_ref, dz_ref, dz16_ref, dg_ref, db_ref, ag, ab):
        i = pl.program_id(0)

        @pl.when(i == 0)
        def _():
            ag[...] = jnp.zeros_like(ag)
            ab[...] = jnp.zeros_like(ab)

        xhat, rstd = _ln_stats(ALPHA * x_ref[...] + y_ref[...])
        dy = ALPHA * dzn_ref[...] + dxb_ref[...]
        ag[...] += _fold8(dy * xhat)
        ab[...] += _fold8(dy)
        dz = _ln_bwd(dy, xhat, rstd, g_ref[...])
        dz_ref[...] = dz
        dz16_ref[...] = dz.astype(dz16_ref.dtype)

        @pl.when(i == n - 1)
        def _():
            dg_ref[...] = jnp.sum(ag[...], axis=0, keepdims=True)
            db_ref[...] = jnp.sum(ab[...], axis=0, keepdims=True)

    return pl.pallas_call(
        body, name="ln_bwd", grid=(n,),
        in_specs=[_row_spec(ts, D_MODEL)] * 4 + [_bc_spec((1, D_MODEL))],
        out_specs=[_row_spec(ts, D_MODEL), _row_spec(ts, D_MODEL), _bc_spec((1, D_MODEL)), _bc_spec((1, D_MODEL))],
        out_shape=[jax.ShapeDtypeStruct((s, D_MODEL), F32), jax.ShapeDtypeStruct((s, D_MODEL), BF16),
                   jax.ShapeDtypeStruct((1, D_MODEL), F32), jax.ShapeDtypeStruct((1, D_MODEL), F32)],
        scratch_shapes=[pltpu.VMEM((8, D_MODEL), F32)] * 2,
        compiler_params=_params(("arbitrary",)),
    )(dz_next, dx_branch, x, y, g)


def _residual_grad(dz, dx_branch, *, ts):
    s = dz.shape[0]

    def body(dz_ref, dxb_ref, o_ref):
        o_ref[...] = ALPHA * dz_ref[...] + dxb_ref[...]

    return pl.pallas_call(
        body, name="residual_grad", grid=(s // ts,),
        in_specs=[_row_spec(ts, D_MODEL)] * 2, out_specs=_row_spec(ts, D_MODEL),
        out_shape=jax.ShapeDtypeStruct((s, D_MODEL), F32),
        compiler_params=_params(("parallel",)),
    )(dz, dx_branch)


C_LAT = slice(0, KV_LORA)
C_ROPE = slice(KV_LORA, KV_DOWN_PAD)
C_QL = slice(KV_DOWN_PAD, KV_DOWN_PAD + Q_LORA)
C_GATE = slice(KV_DOWN_PAD + Q_LORA, H1_B)


def _rms(x, eps=1e-6):
    r = lax.rsqrt(jnp.mean(x * x, axis=-1, keepdims=True) + eps)
    return x * r, r


def _rms_bwd(dy, xhat, r, g):
    dxh = dy * g
    return r * (dxh - xhat * jnp.mean(dxh * xhat, axis=-1, keepdims=True))


def _kvq_prep(h1, kv_norm, q_norm, cos, sa, sb, *, ts):
    s = h1.shape[0]

    def body(h_ref, kn_ref, qn_ref, c_ref, sa_ref, sb_ref, lat_ref, kr_ref, ql_ref):
        lat, _ = _rms(h_ref[:, C_LAT])
        lat_ref[...] = (lat * kn_ref[...]).astype(lat_ref.dtype)
        kr_ref[...] = _rope_b(h_ref[:, C_ROPE], c_ref[...], sa_ref[...], sb_ref[...], 1.0).astype(kr_ref.dtype)
        ql, _ = _rms(h_ref[:, C_QL])
        ql_ref[...] = (ql * qn_ref[...]).astype(ql_ref.dtype)

    return pl.pallas_call(
        body, name="kvq_prep", grid=(s // ts,),
        in_specs=[_row_spec(ts, H1_B), _bc_spec((1, KV_LORA)), _bc_spec((1, Q_LORA))] + [_row_spec(ts, 128)] * 3,
        out_specs=[_row_spec(ts, KV_LORA), _row_spec(ts, 128), _row_spec(ts, Q_LORA)],
        out_shape=[jax.ShapeDtypeStruct((s, KV_LORA), BF16), jax.ShapeDtypeStruct((s, 128), BF16),
                   jax.ShapeDtypeStruct((s, Q_LORA), BF16)],
        compiler_params=_params(("parallel",)),
    )(h1, kv_norm, q_norm, cos, sa, sb)


def _rope_q(qf, cos, sa, sb, sign, scale, out_dtype, *, ts, name):
    s = qf.shape[0]

    def body(q_ref, c_ref, sa_ref, sb_ref, o_ref):
        c, a, b = c_ref[...], sa_ref[...], sb_ref[...]
        for hd in range(H_B):
            lo = hd * QK_PAD
            o_ref[:, lo:lo + 128] = (q_ref[:, lo:lo + 128] * scale).astype(o_ref.dtype)
            o_ref[:, lo + 128:lo + 256] = (_rope_b(q_ref[:, lo + 128:lo + 256], c, a, b, sign) * scale).astype(o_ref.dtype)

    return pl.pallas_call(
        body, name=name, grid=(s // ts,),
        in_specs=[_row_spec(ts, H_B * QK_PAD)] + [_row_spec(ts, 128)] * 3,
        out_specs=_row_spec(ts, H_B * QK_PAD),
        out_shape=jax.ShapeDtypeStruct((s, H_B * QK_PAD), out_dtype),
        compiler_params=_params(("parallel",)),
    )(qf, cos, sa, sb)


def _gate_b_fwd(o, h1, *, ts):
    s = o.shape[0]

    def body(o_ref, h_ref, u_ref):
        g = h_ref[:, C_GATE]
        u_ref[...] = (o_ref[...] * (g * _sigmoid(g))).astype(u_ref.dtype)

    return pl.pallas_call(
        body, name="gate_b_fwd", grid=(s // ts,),
        in_specs=[_row_spec(ts, WIDTH_B), _row_spec(ts, H1_B)],
        out_specs=_row_spec(ts, WIDTH_B),
        out_shape=jax.ShapeDtypeStruct((s, WIDTH_B), BF16),
        compiler_params=_params(("parallel",)),
    )(o, h1)


def _gate_b_bwd(du, o, h1, *, ts):
    s = o.shape[0]

    def body(du_ref, o_ref, h_ref, do_ref, dg_ref, dl_ref):
        g = h_ref[:, C_GATE]
        sg = _sigmoid(g)
        du = du_ref[...]
        ov = o_ref[...]
        do = du * (g * sg)
        do_ref[...] = do.astype(do_ref.dtype)
        dg_ref[...] = (du * ov * (sg * (1.0 + g * (1.0 - sg)))).astype(dg_ref.dtype)
        prod = do * ov
        for hd in range(H_B):
            dl_ref[hd] = jnp.sum(prod[:, hd * V_HEAD:(hd + 1) * V_HEAD], axis=-1, keepdims=True)

    return pl.pallas_call(
        body, name="gate_b_bwd", grid=(s // ts,),
        in_specs=[_row_spec(ts, WIDTH_B), _row_spec(ts, WIDTH_B), _row_spec(ts, H1_B)],
        out_specs=[_row_spec(ts, WIDTH_B), _row_spec(ts, WIDTH_B), pl.BlockSpec((H_B, ts, 1), lambda i: (0, i, 0))],
        out_shape=[jax.ShapeDtypeStruct((s, WIDTH_B), BF16), jax.ShapeDtypeStruct((s, WIDTH_B), BF16),
                   jax.ShapeDtypeStruct((H_B, s, 1), F32)],
        compiler_params=_params(("parallel",)),
    )(du, o, h1)


def _h1_bwd(h1, dlat_k, dlat_v, dkr_heads, dqn, dg16, kv_norm, q_norm, cos, sa, sb, *, ts):
    s = h1.shape[0]
    n = s // ts

    def body(h_ref, dk_ref, dv_ref, dkr_ref, dqn_ref, dg_ref, kn_ref, qn_ref, c_ref, sa_ref, sb_ref,
             o_ref, dkn_ref, dqn_out_ref, akn, aqn):
        i = pl.program_id(0)

        @pl.when(i == 0)
        def _():
            akn[...] = jnp.zeros_like(akn)
            aqn[...] = jnp.zeros_like(aqn)

        lat, r = _rms(h_ref[:, C_LAT])
        dlat = dk_ref[...] + dv_ref[...]
        akn[...] += _fold8(dlat * lat)
        o_ref[:, C_LAT] = _rms_bwd(dlat, lat, r, kn_ref[...]).astype(o_ref.dtype)

        dkr = dkr_ref[:, 0:128]
        for hd in range(1, H_B):
            dkr = dkr + dkr_ref[:, hd * 128:(hd + 1) * 128]
        o_ref[:, C_ROPE] = _rope_b(dkr, c_ref[...], sa_ref[...], sb_ref[...], -1.0).astype(o_ref.dtype)

        ql, rq = _rms(h_ref[:, C_QL])
        dq = dqn_ref[...]
        aqn[...] += _fold8(dq * ql)
        o_ref[:, C_QL] = _rms_bwd(dq, ql, rq, qn_ref[...]).astype(o_ref.dtype)
        o_ref[:, C_GATE] = dg_ref[...]

        @pl.when(i == n - 1)
        def _():
            dkn_ref[...] = jnp.sum(akn[...], axis=0, keepdims=True)
            dqn_out_ref[...] = jnp.sum(aqn[...], axis=0, keepdims=True)

    return pl.pallas_call(
        body, name="h1_bwd", grid=(n,),
        in_specs=[_row_spec(ts, H1_B), _row_spec(ts, KV_LORA), _row_spec(ts, KV_LORA), _row_spec(ts, H_B * 128),
                  _row_spec(ts, Q_LORA), _row_spec(ts, WIDTH_B), _bc_spec((1, KV_LORA)), _bc_spec((1, Q_LORA))]
        + [_row_spec(ts, 128)] * 3,
        out_specs=[_row_spec(ts, H1_B), _bc_spec((1, KV_LORA)), _bc_spec((1, Q_LORA))],
        out_shape=[jax.ShapeDtypeStruct((s, H1_B), BF16), jax.ShapeDtypeStruct((1, KV_LORA), F32),
                   jax.ShapeDtypeStruct((1, Q_LORA), F32)],
        scratch_shapes=[pltpu.VMEM((8, KV_LORA), F32), pltpu.VMEM((8, Q_LORA), F32)],
        compiler_params=_params(("arbitrary",)),
    )(h1, dlat_k, dlat_v, dkr_heads, dqn, dg16, kv_norm, q_norm, cos, sa, sb)


def _dot(a, b, ca, cb):
    return lax.dot_general(a, b, (((ca,), (cb,)), ((), ())), preferred_element_type=F32)


def _retention_specs(n, idx):
    full = lambda shape: pl.BlockSpec(shape, lambda i: (0,) * len(shape))
    return ([pl.BlockSpec((CHUNK, H_A * DK_A), lambda i: (idx(i), 0))] * 2 + [pl.BlockSpec((CHUNK, WIDTH_A), lambda i: (idx(i), 0))],
            [full((H_A, CHUNK, CHUNK)), full((H_A, CHUNK, 1)), full((H_A, CHUNK, 1)), full((H_A, 1, DV_A))])


def _retention_fwd(q, k, v, tables):
    s = q.shape[0]
    n = s // CHUNK

    def body(q_ref, k_ref, v_ref, dm_ref, qd_ref, kd_ref, cd_ref, o_ref, st_ref, state):
        @pl.when(pl.program_id(0) == 0)
        def _():
            state[...] = jnp.zeros_like(state)

        for hd in range(H_A):
            qs, vs = slice(hd * DK_A, (hd + 1) * DK_A), slice(hd * DV_A, (hd + 1) * DV_A)
            qv, kv, vv = q_ref[:, qs], k_ref[:, qs], v_ref[:, vs]
            st = state[hd]
            st16 = st.astype(BF16)
            st_ref[0, hd] = st16
            scores = _dot(qv, kv, 1, 1) * dm_ref[hd]
            qd = (qv.astype(F32) * qd_ref[hd]).astype(BF16)
            o_ref[:, vs] = _dot(scores.astype(BF16), vv, 1, 0) + _dot(qd, st16, 1, 0)
            kd = (kv.astype(F32) * kd_ref[hd]).astype(BF16)
            state[hd] = st * cd_ref[hd] + _dot(kd, vv, 0, 0)

    rows, tabs = _retention_specs(n, lambda i: i)
    return pl.pallas_call(
        body, name="retention_fwd", grid=(n,),
        in_specs=rows + tabs,
        out_specs=[pl.BlockSpec((CHUNK, WIDTH_A), lambda i: (i, 0)),
                   pl.BlockSpec((1, H_A, DK_A, DV_A), lambda i: (i, 0, 0, 0))],
        out_shape=[jax.ShapeDtypeStruct((s, WIDTH_A), F32), jax.ShapeDtypeStruct((n, H_A, DK_A, DV_A), BF16)],
        scratch_shapes=[pltpu.VMEM((H_A, DK_A, DV_A), F32)],
        compiler_params=_params(("arbitrary",)),
    )(q, k, v, *tables)


def _retention_bwd(q, k, v, states, do, tables):
    s = q.shape[0]
    n = s // CHUNK

    def body(q_ref, k_ref, v_ref, dm_ref, qd_ref, kd_ref, cd_ref, st_ref, do_ref, dq_ref, dk_ref, dv_ref, grad_state):
        @pl.when(pl.program_id(0) == 0)
        def _():
            grad_state[...] = jnp.zeros_like(grad_state)

        for hd in range(H_A):
            qs, vs = slice(hd * DK_A, (hd + 1) * DK_A), slice(hd * DV_A, (hd + 1) * DV_A)
            qv, kv, vv, dov = q_ref[:, qs], k_ref[:, qs], v_ref[:, vs], do_ref[:, vs]
            dm = dm_ref[hd]
            gs = grad_state[hd]
            g16 = gs.astype(BF16)
            scores = (_dot(qv, kv, 1, 1) * dm).astype(BF16)
            dscores = (_dot(dov, vv, 1, 1) * dm).astype(BF16)
            qd = (qv.astype(F32) * qd_ref[hd]).astype(BF16)
            kd = (kv.astype(F32) * kd_ref[hd]).astype(BF16)
            dq_ref[:, qs] = _dot(dscores, kv, 1, 0) + _dot(dov, st_ref[0, hd], 1, 1) * qd_ref[hd]
            dk_ref[:, qs] = _dot(dscores, qv, 0, 0) + _dot(vv, g16, 1, 1) * kd_ref[hd]
            dv_ref[:, vs] = (_dot(scores, dov, 0, 0) + _dot(kd, g16, 1, 0)).astype(dv_ref.dtype)
            grad_state[hd] = gs * cd_ref[hd] + _dot(qd, dov, 0, 0)

    rev = lambda i: n - 1 - i
    rows, tabs = _retention_specs(n, rev)
    return pl.pallas_call(
        body, name="retention_bwd", grid=(n,),
        in_specs=rows + tabs + [pl.BlockSpec((1, H_A, DK_A, DV_A), lambda i: (rev(i), 0, 0, 0)),
                                pl.BlockSpec((CHUNK, WIDTH_A), lambda i: (rev(i), 0))],
        out_specs=[pl.BlockSpec((CHUNK, H_A * DK_A), lambda i: (rev(i), 0))] * 2
        + [pl.BlockSpec((CHUNK, WIDTH_A), lambda i: (rev(i), 0))],
        out_shape=[jax.ShapeDtypeStruct((s, H_A * DK_A), F32), jax.ShapeDtypeStruct((s, H_A * DK_A), F32),
                   jax.ShapeDtypeStruct((s, WIDTH_A), BF16)],
        scratch_shapes=[pltpu.VMEM((H_A, DK_A, DV_A), F32)],
        compiler_params=_params(("arbitrary",)),
    )(q, k, v, *tables, states, do)


LOG2E = 1.4426950408889634
LN2 = 0.6931471805599453


def _assemble_k(kn, kr, *, ts):
    s = kn.shape[0]

    def body(kn_ref, kr_ref, o_ref):
        r = kr_ref[...]
        for hd in range(H_B):
            o_ref[:, hd * QK_PAD:hd * QK_PAD + 128] = kn_ref[:, hd * 128:(hd + 1) * 128]
            o_ref[:, hd * QK_PAD + 128:(hd + 1) * QK_PAD] = r

    return pl.pallas_call(
        body, name="assemble_k", grid=(s // ts,),
        in_specs=[_row_spec(ts, H_B * 128), _row_spec(ts, 128)],
        out_specs=_row_spec(ts, H_B * QK_PAD),
        out_shape=jax.ShapeDtypeStruct((s, H_B * QK_PAD), kn.dtype),
        compiler_params=_params(("parallel",)),
    )(kn, kr)


def _causal_mask(sc, row0):
    row = lax.broadcasted_iota(jnp.int32, sc.shape, 0) + row0
    col = lax.broadcasted_iota(jnp.int32, sc.shape, 1)
    return jnp.where(col <= row, sc, NEG_BIG)


def _attention_fwd(q, k, v, *, blk, sub):
    s = q.shape[0]
    nb = s // blk
    reps = blk // LANES

    def body(q_ref, k_ref, v_ref, o_ref, lse_ref, vext_s, m_s, acc_s):
        i = pl.program_id(1)

        @pl.when(i == 0)
        def _():
            vext_s[:, 0:V_HEAD] = v_ref[...]
            vext_s[:, V_HEAD:2 * V_HEAD] = jnp.ones((s, V_HEAD), vext_s.dtype)

        m_s[...] = jnp.full_like(m_s, NEG_BIG)
        acc_s[...] = jnp.zeros_like(acc_s)

        def step(j, masked):
            kv_rows = pl.ds(pl.multiple_of(j * blk, blk), blk)
            kb = k_ref[kv_rows, :]
            vb = vext_s[kv_rows, :]
            for r in range(blk // sub):
                rows = slice(r * sub, (r + 1) * sub)
                sc = _dot(q_ref[rows, :], kb, 1, 1)
                if masked:
                    sc = _causal_mask(sc, r * sub)
                m_prev = m_s[rows, :]
                m_new = jnp.maximum(m_prev, jnp.max(sc, axis=-1, keepdims=True))
                p = jnp.exp2(sc - jnp.tile(m_new, (1, reps)))
                a = jnp.exp2(m_prev - m_new)
                acc_s[rows, :] = jnp.tile(a, (1, 2)) * acc_s[rows, :] + _dot(p.astype(BF16), vb, 1, 0)
                m_s[rows, :] = m_new

        def pair_body(jj, carry):
            step(2 * jj, False)
            step(2 * jj + 1, False)
            return carry

        lax.fori_loop(0, i // 2, pair_body, 0)

        @pl.when(i % 2 == 1)
        def _():
            step(i - 1, False)

        step(i, True)
        acc = acc_s[...]
        l = acc[:, V_HEAD:2 * V_HEAD]
        o_ref[...] = acc[:, 0:V_HEAD] / l
        lse_ref[0] = (m_s[...] + jnp.log2(l))[:, 0:1]

    return pl.pallas_call(
        body, name="attention_fwd", grid=(H_B, nb),
        in_specs=[pl.BlockSpec((blk, QK_PAD), lambda h, i: (i, h)), pl.BlockSpec((s, QK_PAD), lambda h, i: (0, h)),
                  pl.BlockSpec((s, V_HEAD), lambda h, i: (0, h))],
        out_specs=[pl.BlockSpec((blk, V_HEAD), lambda h, i: (i, h)), pl.BlockSpec((1, blk, 1), lambda h, i: (h, i, 0))],
        out_shape=[jax.ShapeDtypeStruct((s, WIDTH_B), F32), jax.ShapeDtypeStruct((H_B, s, 1), F32)],
        scratch_shapes=[pltpu.VMEM((s, 2 * V_HEAD), BF16), pltpu.VMEM((blk, LANES), F32), pltpu.VMEM((blk, 2 * V_HEAD), F32)],
        compiler_params=_params(("parallel", "arbitrary")),
    )(q, k, v)


def _attention_bwd(q, k, v, do, lse, delta, *, blk):
    s = q.shape[0]
    nb = s // blk
    reps = blk // LANES

    def body(q_ref, k_ref, v_ref, do_ref, lse_ref, dl_ref, dq_ref, dkn_ref, dkr_ref, dv_ref, lse_s, dl_s):
        i = pl.program_id(1)
        lse_s[...] = jnp.broadcast_to(lse_ref[0], (blk, LANES))
        dl_s[...] = jnp.broadcast_to(dl_ref[0], (blk, LANES))

        def step(j, diagonal):
            kv_rows = pl.ds(pl.multiple_of(j * blk, blk), blk)
            qv, dov, kb = q_ref[...], do_ref[...], k_ref[kv_rows, :]
            sc = _dot(qv, kb, 1, 1)
            if diagonal:
                sc = _causal_mask(sc, 0)
            p = jnp.exp2(sc - jnp.tile(lse_s[...], (1, reps)))
            dp = _dot(dov, v_ref[kv_rows, :], 1, 1)
            ds = (p * (dp - jnp.tile(dl_s[...], (1, reps)))).astype(BF16)
            dv_c = _dot(p.astype(BF16), dov, 0, 0)
            dk_c = _dot(ds, qv, 0, 0)
            dq_c = _dot(ds, kb, 1, 0)
            if diagonal:
                dkn_ref[kv_rows, :] = dk_c[:, 0:128]
                dkr_ref[kv_rows, :] = dk_c[:, 128:256]
                dv_ref[kv_rows, :] = dv_c
                dq_ref[...] = (dq_ref[...] + dq_c) * ATT_SCALE
            else:
                dkn_ref[kv_rows, :] += dk_c[:, 0:128]
                dkr_ref[kv_rows, :] += dk_c[:, 128:256]
                dv_ref[kv_rows, :] += dv_c
                dq_ref[...] += dq_c

        dq_ref[...] = jnp.zeros_like(dq_ref)

        def pair_body(jj, carry):
            step(2 * jj, False)
            step(2 * jj + 1, False)
            return carry

        lax.fori_loop(0, i // 2, pair_body, 0)

        @pl.when(i % 2 == 1)
        def _():
            step(i - 1, False)

        step(i, True)

        @pl.when(i == nb - 1)
        def _():
            dkn_ref[...] = dkn_ref[...] * LN2
            dkr_ref[...] = dkr_ref[...] * LN2

    head = lambda h, i: (0, h)
    blk_idx = lambda h, i: (i, h)
    st_idx = lambda h, i: (h, i, 0)
    return pl.pallas_call(
        body, name="attention_bwd", grid=(H_B, nb),
        in_specs=[pl.BlockSpec((blk, QK_PAD), blk_idx), pl.BlockSpec((s, QK_PAD), head), pl.BlockSpec((s, V_HEAD), head),
                  pl.BlockSpec((blk, V_HEAD), blk_idx), pl.BlockSpec((1, blk, 1), st_idx), pl.BlockSpec((1, blk, 1), st_idx)],
        out_specs=[pl.BlockSpec((blk, QK_PAD), blk_idx), pl.BlockSpec((s, 128), head), pl.BlockSpec((s, 128), head),
                   pl.BlockSpec((s, 128), head)],
        out_shape=[jax.ShapeDtypeStruct((s, H_B * QK_PAD), F32), jax.ShapeDtypeStruct((s, H_B * 128), F32),
                   jax.ShapeDtypeStruct((s, H_B * 128), F32), jax.ShapeDtypeStruct((s, H_B * 128), F32)],
        scratch_shapes=[pltpu.VMEM((blk, LANES), F32), pltpu.VMEM((blk, LANES), F32)],
        compiler_params=_params(("parallel", "arbitrary")),
    )(q, k, v, do, lse, delta)


def _local_step(x, target, w, kv_norm, q_norm, ln_g, ln_b, *, ts=256, blk=512):
    s = x.shape[0]
    cos_a, sin_a = _rope_tables_a(s)
    cos_b, sa_b, sb_b = _rope_tables_b(s)
    tables = _retention_tables()
    g0, g1, b0, b1 = ln_g[0:1], ln_g[1:2], ln_b[0:1], ln_b[1:2]

    x16 = x.astype(BF16)
    h_a = _mm(x16, w["a_in"], tn=1536, name="a_in_fwd")
    q_a, k_a, v_a = _rope_a_fwd(h_a, cos_a, sin_a, ts=ts)
    o_a, states = _retention_fwd(q_a, k_a, v_a, tables)
    u_a = _gn_gate_fwd(o_a, h_a, ts=ts)
    y_a = _mm(u_a, w["a_out"], tn=1024, name="a_out_fwd")
    x1, x1_16 = _ln_fwd(x, y_a, g0, b0, ts=ts)

    h1 = _mm(x1_16, w["b_in1"], tn=1152, name="b_in_fwd")
    lat16, kr16, qn16 = _kvq_prep(h1, kv_norm, q_norm, cos_b, sa_b, sb_b, ts=ts)
    kn16 = _mm(lat16, w["up_k"], out_dtype=BF16, tn=2048, name="up_k_fwd")
    v16 = _mm(lat16, w["up_v"], out_dtype=BF16, tn=2048, name="up_v_fwd")
    qf = _mm(qn16, w["uq"], tn=2048, name="uq_fwd")
    q16 = _rope_q(qf, cos_b, sa_b, sb_b, 1.0, ATT_SCALE * LOG2E, BF16, ts=ts, name="rope_q_fwd")
    k16 = _assemble_k(kn16, kr16, ts=ts)
    o_b, lse = _attention_fwd(q16, k16, v16, blk=blk, sub=blk // 2)
    u_b = _gate_b_fwd(o_b, h1, ts=ts)
    y_b = _mm(u_b, w["b_out"], tn=1024, name="b_out_fwd")

    dz_b, dz_b16, dg1, db1, loss = _ln_loss_bwd(x1, y_b, target, g1, b1, ts=ts)
    d_b_out = _mm(u_b, dz_b16, ta=True, tn=1024, tk=1024, name="b_out_dw")
    du_b = _mm(dz_b16, w["b_out"], tb=True, tn=1024, name="b_out_dx")
    do16, dgate16, delta = _gate_b_bwd(du_b, o_b, h1, ts=ts)
    dq, dkn, dkr_heads, dv = _attention_bwd(q16, k16, v16, do16, lse, delta, blk=blk)
    dqf16 = _rope_q(dq, cos_b, sa_b, sb_b, -1.0, 1.0, BF16, ts=ts, name="rope_q_bwd")
    d_uq = _mm(qn16, dqf16, ta=True, tm=768, tn=2048, tk=1024, name="uq_dw")
    dqn = _mm(dqf16, w["uq"], tb=True, tn=768, tk=2048, name="uq_dx")
    d_up_k = _mm(lat16, dkn, ta=True, tn=2048, tk=1024, name="up_k_dw")
    d_up_v = _mm(lat16, dv, ta=True, tn=2048, tk=1024, name="up_v_dw")
    dlat_k = _mm(dkn, w["up_k"], tb=True, tn=512, name="up_k_dx")
    dlat_v = _mm(dv, w["up_v"], tb=True, tn=512, name="up_v_dx")
    dh1, dkvn, dqnorm = _h1_bwd(h1, dlat_k, dlat_v, dkr_heads, dqn, dgate16, kv_norm, q_norm, cos_b, sa_b, sb_b, ts=ts)
    d_b_in1 = _mm(x1_16, dh1, ta=True, tn=1152, tk=1024, name="b_in_dw")
    dx1 = _mm(dh1, w["b_in1"], tb=True, tn=1024, name="b_in_dx")

    dz_a, dz_a16, dg0, db0 = _ln_bwd_call(dz_b, dx1, x, y_a, g0, ts=ts)
    d_a_out = _mm(u_a, dz_a16, ta=True, tn=1024, tk=1024, name="a_out_dw")
    du_a = _mm(dz_a16, w["a_out"], tb=True, tn=1024, name="a_out_dx")
    do_a16, dgate_a16 = _gn_gate_bwd(du_a, o_a, h_a, ts=ts)
    dq_a, dk_a, dv_a16 = _retention_bwd(q_a, k_a, v_a, states, do_a16, tables)
    dh_a = _rope_a_bwd(dq_a, dk_a, dv_a16, dgate_a16, cos_a, sin_a, ts=ts)
    d_a_in = _mm(x16, dh_a, ta=True, tn=1536, tk=1024, name="a_in_dw")
    dx_a = _mm(dh_a, w["a_in"], tb=True, tn=1024, tk=2048, name="a_in_dx")
    grad_x = _residual_grad(dz_a, dx_a, ts=ts)

    grads = dict(a_in=d_a_in, a_out=d_a_out, b_in1=d_b_in1, uq=d_uq, b_out=d_b_out, up_k=d_up_k, up_v=d_up_v)
    small = dict(ln_g=jnp.concatenate([dg0, dg1], axis=0), ln_b=jnp.concatenate([db0, db1], axis=0),
                 q_norm=dqnorm, kv_norm=dkvn)
    return loss, grad_x, grads, small


def _flat_shards(shards, dtype):
    parts = [shards[name].reshape(rows, FLAT_COLS) for name, rows in SHARD_ROWS]
    used = sum(rows for _, rows in SHARD_ROWS)
    parts.append(jnp.zeros((FLAT_ROWS - used, FLAT_COLS), shards["a_w_in"].dtype))
    return jnp.concatenate(parts, axis=0).astype(dtype)


def _unflat_shards(flat, shapes):
    out, off = {}, 0
    for name, rows in SHARD_ROWS:
        out[name] = flat[off:off + rows].reshape(shapes[name])
        off += rows
    return out


COL_SHARDED = {"a_w_in": (D_MODEL, IN_A), "b_w_in": (D_MODEL, IN_B), "b_w_uq": (Q_LORA, H_B * (QK_NOPE + QK_ROPE)),
               "kv_w_up": (KV_LORA, H_B * (QK_NOPE + V_HEAD))}
ROW_SHARDED = {"a_w_out": (WIDTH_A, D_MODEL), "b_w_out": (WIDTH_B, D_MODEL), "kv_w_down": (D_MODEL, KV_LORA + QK_ROPE)}


def _full_from_gathered(gathered):
    out, off = {}, 0
    for name, rows in SHARD_ROWS:
        part = gathered[:, off:off + rows]
        off += rows
        if name in COL_SHARDED:
            r, c = COL_SHARDED[name]
            out[name] = part.reshape(N_CHIPS, r, c // N_CHIPS).transpose(1, 0, 2).reshape(r, c)
        else:
            r, c = ROW_SHARDED[name]
            out[name] = part.reshape(r, c)
    return out


def _gathered_from_full(full):
    parts = []
    for name, rows in SHARD_ROWS:
        g = full[name]
        if name in COL_SHARDED:
            r, c = COL_SHARDED[name]
            g = g.reshape(r, N_CHIPS, c // N_CHIPS).transpose(1, 0, 2)
        parts.append(g.reshape(N_CHIPS, rows, FLAT_COLS))
    used = sum(rows for _, rows in SHARD_ROWS)
    parts.append(jnp.zeros((N_CHIPS, FLAT_ROWS - used, FLAT_COLS), F32))
    return jnp.concatenate(parts, axis=1)


def _kernel_layout(full):
    uq = full["b_w_uq"].reshape(Q_LORA, H_B, QK_NOPE + QK_ROPE)
    uq = jnp.pad(uq, ((0, 0), (0, 0), (0, QK_PAD - QK_NOPE - QK_ROPE))).reshape(Q_LORA, H_B * QK_PAD)
    up = full["kv_w_up"].reshape(KV_LORA, H_B, QK_NOPE + V_HEAD)
    down = jnp.pad(full["kv_w_down"], ((0, 0), (0, KV_DOWN_PAD - KV_LORA - QK_ROPE)))
    return dict(a_in=full["a_w_in"], a_out=full["a_w_out"], b_out=full["b_w_out"], uq=uq,
                up_k=up[:, :, :QK_NOPE].reshape(KV_LORA, H_B * QK_NOPE),
                up_v=up[:, :, QK_NOPE:].reshape(KV_LORA, H_B * V_HEAD),
                b_in1=jnp.concatenate([down, full["b_w_in"]], axis=1))


def _reference_layout_grads(g):
    uq = g["uq"].reshape(Q_LORA, H_B, QK_PAD)[:, :, :QK_NOPE + QK_ROPE].reshape(Q_LORA, H_B * (QK_NOPE + QK_ROPE))
    up = jnp.concatenate([g["up_k"].reshape(KV_LORA, H_B, QK_NOPE), g["up_v"].reshape(KV_LORA, H_B, V_HEAD)], axis=2)
    return dict(a_w_in=g["a_in"], a_w_out=g["a_out"], b_w_out=g["b_out"], b_w_uq=uq,
                kv_w_up=up.reshape(KV_LORA, H_B * (QK_NOPE + V_HEAD)),
                kv_w_down=g["b_in1"][:, :KV_LORA + QK_ROPE], b_w_in=g["b_in1"][:, KV_DOWN_PAD:])


HBM_SPEC = pl.BlockSpec(memory_space=pl.ANY)


def _me():
    return lax.axis_index("x"), lax.axis_index("y"), lax.axis_index("c")


def _chip_flips(x, y):
    return [(1 - x, y), (x, 1 - y), (1 - x, 1 - y)]


def _gather_weights(flat16):
    def body(src_ref, out_ref, send_sems, recv_sems, local_sem):
        x, y, c = _me()
        mine = 2 * x + y
        local = pltpu.make_async_copy(src_ref, out_ref.at[mine], local_sem)
        local.start()
        copies = []
        for k, (px, py) in enumerate(_chip_flips(x, y)):
            copies.append(pltpu.make_async_remote_copy(
                src_ref=src_ref, dst_ref=out_ref.at[mine], send_sem=send_sems.at[k], recv_sem=recv_sems.at[k],
                device_id=(px, py, c), device_id_type=MESH))
        for cp in copies:
            cp.start()
        for cp in copies:
            cp.wait_send()
        for k, (px, py) in enumerate(_chip_flips(x, y)):
            pltpu.make_async_remote_copy(
                src_ref=src_ref, dst_ref=out_ref.at[2 * px + py], send_sem=send_sems.at[k], recv_sem=recv_sems.at[k],
                device_id=(px, py, c), device_id_type=MESH).wait_recv()
        local.wait()

    return pl.pallas_call(
        body, name="gather_weights",
        in_specs=[HBM_SPEC], out_specs=HBM_SPEC,
        out_shape=jax.ShapeDtypeStruct((N_CHIPS,) + flat16.shape, flat16.dtype),
        scratch_shapes=[pltpu.SemaphoreType.DMA((3,)), pltpu.SemaphoreType.DMA((3,)), pltpu.SemaphoreType.DMA],
    )(flat16)


def _pair_exchange_halves(g):
    def body(g_ref, out_ref, send_sem, recv_sem):
        x, y, c = _me()
        theirs = g_ref.at[:, pl.ds(pl.multiple_of((1 - c) * HALF_ROWS, 8), HALF_ROWS), :]
        cp = pltpu.make_async_remote_copy(src_ref=theirs, dst_ref=out_ref, send_sem=send_sem, recv_sem=recv_sem,
                                          device_id=(x, y, 1 - c), device_id_type=MESH)
        cp.start()
        cp.wait_send()
        cp.wait_recv()

    return pl.pallas_call(
        body, name="pair_exchange_halves",
        in_specs=[HBM_SPEC], out_specs=HBM_SPEC,
        out_shape=jax.ShapeDtypeStruct((N_CHIPS, HALF_ROWS, FLAT_COLS), g.dtype),
        scratch_shapes=[pltpu.SemaphoreType.DMA, pltpu.SemaphoreType.DMA],
    )(g)


def _chip_exchange(p):
    def body(p_ref, out_ref, send_sems, recv_sems):
        x, y, c = _me()
        copies = []
        for k, (px, py) in enumerate(_chip_flips(x, y)):
            copies.append(pltpu.make_async_remote_copy(
                src_ref=p_ref.at[2 * px + py], dst_ref=out_ref.at[k], send_sem=send_sems.at[k], recv_sem=recv_sems.at[k],
                device_id=(px, py, c), device_id_type=MESH))
        for cp in copies:
            cp.start()
        for cp in copies:
            cp.wait_send()
        for cp in copies:
            cp.wait_recv()

    return pl.pallas_call(
        body, name="chip_exchange",
        in_specs=[HBM_SPEC], out_specs=HBM_SPEC,
        out_shape=jax.ShapeDtypeStruct((3, HALF_ROWS, FLAT_COLS), p.dtype),
        scratch_shapes=[pltpu.SemaphoreType.DMA((3,)), pltpu.SemaphoreType.DMA((3,))],
    )(p)


def _pair_share(r):
    def body(r_ref, out_ref, send_sem, recv_sem):
        x, y, c = _me()
        cp = pltpu.make_async_remote_copy(src_ref=r_ref, dst_ref=out_ref, send_sem=send_sem, recv_sem=recv_sem,
                                          device_id=(x, y, 1 - c), device_id_type=MESH)
        cp.start()
        cp.wait_send()
        cp.wait_recv()

    return pl.pallas_call(
        body, name="pair_share",
        in_specs=[HBM_SPEC], out_specs=HBM_SPEC,
        out_shape=jax.ShapeDtypeStruct(r.shape, r.dtype),
        scratch_shapes=[pltpu.SemaphoreType.DMA, pltpu.SemaphoreType.DMA],
    )(r)


def _sum_small(vec):
    def body(v_ref, out_ref, slots, send_sems, recv_sems):
        x, y, c = _me()
        me = 4 * x + 2 * y + c
        slots[me] = v_ref[...]
        flips = [(fx, fy, fc) for fx in (0, 1) for fy in (0, 1) for fc in (0, 1)][1:]
        copies = []
        for k, (fx, fy, fc) in enumerate(flips):
            copies.append(pltpu.make_async_remote_copy(
                src_ref=v_ref, dst_ref=slots.at[me], send_sem=send_sems.at[k], recv_sem=recv_sems.at[k],
                device_id=(x ^ fx, y ^ fy, c ^ fc), device_id_type=MESH))
        for cp in copies:
            cp.start()
        for cp in copies:
            cp.wait_send()
        for k, (fx, fy, fc) in enumerate(flips):
            src = 4 * (x ^ fx) + 2 * (y ^ fy) + (c ^ fc)
            pltpu.make_async_remote_copy(
                src_ref=v_ref, dst_ref=slots.at[src], send_sem=send_sems.at[k], recv_sem=recv_sems.at[k],
                device_id=(x ^ fx, y ^ fy, c ^ fc), device_id_type=MESH).wait_recv()
        total = slots[0]
        for d in range(1, N_DEV):
            total = total + slots[d]
        out_ref[...] = total

    return pl.pallas_call(
        body, name="sum_small",
        in_specs=[pl.BlockSpec(memory_space=pltpu.VMEM)], out_specs=pl.BlockSpec(memory_space=pltpu.VMEM),
        out_shape=jax.ShapeDtypeStruct(vec.shape, vec.dtype),
        scratch_shapes=[pltpu.VMEM((N_DEV,) + vec.shape, vec.dtype), pltpu.SemaphoreType.DMA((7,)),
                        pltpu.SemaphoreType.DMA((7,))],
    )(vec)


UPD_ROWS = 256


def _pair_sum(g, theirs, place):
    nb = HALF_ROWS // UPD_ROWS

    def body(place_ref, g_ref, t_ref, own_ref, o16_ref):
        total = g_ref[0] + t_ref[0].astype(F32)
        o16_ref[0] = total.astype(o16_ref.dtype)

        @pl.when(pl.program_id(1) == place_ref[1])
        def _():
            own_ref[...] = total

    return pl.pallas_call(
        body, name="pair_sum",
        grid_spec=pltpu.PrefetchScalarGridSpec(
            num_scalar_prefetch=1, grid=(nb, N_CHIPS),
            in_specs=[pl.BlockSpec((1, UPD_ROWS, FLAT_COLS), lambda i, d, place_ref: (d, place_ref[0] * nb + i, 0)),
                      pl.BlockSpec((1, UPD_ROWS, FLAT_COLS), lambda i, d, place_ref: (d, i, 0))],
            out_specs=[pl.BlockSpec((UPD_ROWS, FLAT_COLS), lambda i, d, place_ref: (i, 0)),
                       pl.BlockSpec((1, UPD_ROWS, FLAT_COLS), lambda i, d, place_ref: (d, i, 0))]),
        out_shape=[jax.ShapeDtypeStruct((HALF_ROWS, FLAT_COLS), F32),
                   jax.ShapeDtypeStruct((N_CHIPS, HALF_ROWS, FLAT_COLS), BF16)],
        compiler_params=_params(("parallel", "arbitrary")),
    )(place, g, theirs)


def _chip_sum(own, received):
    nb = HALF_ROWS // UPD_ROWS

    def body(p_ref, r_ref, o_ref):
        o_ref[...] = ((p_ref[...] + r_ref[0].astype(F32)) + r_ref[1].astype(F32)) + r_ref[2].astype(F32)

    return pl.pallas_call(
        body, name="chip_sum", grid=(nb,),
        in_specs=[pl.BlockSpec((UPD_ROWS, FLAT_COLS), lambda i: (i, 0)),
                  pl.BlockSpec((3, UPD_ROWS, FLAT_COLS), lambda i: (0, i, 0))],
        out_specs=pl.BlockSpec((UPD_ROWS, FLAT_COLS), lambda i: (i, 0)),
        out_shape=jax.ShapeDtypeStruct((HALF_ROWS, FLAT_COLS), F32),
        compiler_params=_params(("parallel",)),
    )(own, received)


def _adamw(w, g, m, v, *, rows, name):
    r, c = w.shape
    rows = min(rows, r)
    assert r % rows == 0

    def body(w_ref, g_ref, m_ref, v_ref, d_ref, nm_ref, nv_ref):
        gv = g_ref[...]
        nm = ADAM_B1 * m_ref[...] + (1.0 - ADAM_B1) * gv
        nv = ADAM_B2 * v_ref[...] + (1.0 - ADAM_B2) * (gv * gv)
        m_hat = nm / (1.0 - ADAM_B1 ** ADAM_STEP)
        v_hat = nv / (1.0 - ADAM_B2 ** ADAM_STEP)
        d_ref[...] = -ADAM_LR * (m_hat / (jnp.sqrt(v_hat) + ADAM_EPS) + ADAM_WD * w_ref[...])
        nm_ref[...] = nm
        nv_ref[...] = nv

    spec = pl.BlockSpec((rows, c), lambda i: (i, 0))
    return pl.pallas_call(
        body, name=name, grid=(r // rows,),
        in_specs=[spec] * 4, out_specs=[spec] * 3,
        out_shape=[jax.ShapeDtypeStruct((r, c), F32)] * 3,
        compiler_params=_params(("parallel",)),
    )(w, g, m, v)


W_NAMES = ("a_w_in", "a_w_out", "b_w_in", "b_q_norm", "b_w_uq", "b_w_out", "kv_w_down", "kv_norm", "kv_w_up", "ln_g", "ln_b")
BIG = tuple(name for name, _ in SHARD_ROWS)


def _pack_small(ln_g, ln_b, q_norm, kv_norm, extra=None):
    pad = lambda a: jnp.pad(a.reshape(1, -1), ((0, 0), (0, FLAT_COLS - a.size)))
    rows = [ln_g, ln_b, pad(q_norm), pad(kv_norm),
            jnp.zeros((1, FLAT_COLS), F32) if extra is None else pad(extra), jnp.zeros((1, FLAT_COLS), F32)]
    return jnp.concatenate(rows, axis=0)


def _unpack_small(p):
    return dict(ln_g=p[0:2], ln_b=p[2:4], b_q_norm=p[4:5, :Q_LORA], kv_norm=p[5, :KV_LORA])


def kernel(x, a_w_in, a_w_out, b_w_in, b_q_norm, b_w_uq, b_w_out, kv_w_down, kv_norm, kv_w_up, ln_g, ln_b, loss_target, m_a_w_in, m_a_w_out, m_b_w_in, m_b_q_norm, m_b_w_uq, m_b_w_out, m_kv_w_down, m_kv_norm, m_kv_w_up, m_ln_g, m_ln_b, v_a_w_in, v_a_w_out, v_b_w_in, v_b_q_norm, v_b_w_uq, v_b_w_out, v_kv_w_down, v_kv_norm, v_kv_w_up, v_ln_g, v_ln_b):
    w_in = dict(a_w_in=a_w_in[0], a_w_out=a_w_out[0], b_w_in=b_w_in[0], b_w_uq=b_w_uq[0], b_w_out=b_w_out[0],
                kv_w_down=kv_w_down, kv_w_up=kv_w_up)
    m_in = dict(a_w_in=m_a_w_in[0], a_w_out=m_a_w_out[0], b_w_in=m_b_w_in[0], b_w_uq=m_b_w_uq[0], b_w_out=m_b_w_out[0],
                kv_w_down=m_kv_w_down, kv_w_up=m_kv_w_up)
    v_in = dict(a_w_in=v_a_w_in[0], a_w_out=v_a_w_out[0], b_w_in=v_b_w_in[0], b_w_uq=v_b_w_uq[0], b_w_out=v_b_w_out[0],
                kv_w_down=v_kv_w_down, kv_w_up=v_kv_w_up)
    shard_shapes = {name: w_in[name].shape for name in BIG}

    gathered = _gather_weights(_flat_shards(w_in, BF16))
    weights = _kernel_layout(_full_from_gathered(gathered))

    loss, grad_x, grads, small = _local_step(x[0], loss_target[0], weights, kv_norm.reshape(1, -1), b_q_norm, ln_g, ln_b)

    cx, cy, cc = lax.axis_index("x"), lax.axis_index("y"), lax.axis_index("c")
    g_all = _gathered_from_full(_reference_layout_grads(grads))
    theirs = _pair_exchange_halves(g_all.astype(BF16))
    own, pair16 = _pair_sum(g_all, theirs, jnp.stack([cc, 2 * cx + cy]).astype(jnp.int32))
    mine = _chip_sum(own, _chip_exchange(pair16))
    sibling = _pair_share(mine)
    g_flat = jnp.concatenate([jnp.where(cc == 0, mine, sibling), jnp.where(cc == 0, sibling, mine)], axis=0)

    small_sum = _sum_small(_pack_small(small["ln_g"], small["ln_b"], small["q_norm"], small["kv_norm"], loss[:, :1]))
    loss_out = small_sum[6, 0]
    g_small = _unpack_small(small_sum)

    d_flat, nm_flat, nv_flat = _adamw(_flat_shards(w_in, F32), g_flat, _flat_shards(m_in, F32), _flat_shards(v_in, F32),
                                      rows=512, name="adamw_shards")
    ds, nms, nvs = _adamw(_pack_small(ln_g, ln_b, b_q_norm, kv_norm), small_sum.at[6].set(0.0),
                          _pack_small(m_ln_g, m_ln_b, m_b_q_norm, m_kv_norm),
                          _pack_small(v_ln_g, v_ln_b, v_b_q_norm, v_kv_norm), rows=8, name="adamw_small")

    def assemble(flat, small_packed):
        big = _unflat_shards(flat, shard_shapes)
        sm = _unpack_small(small_packed)
        out = {}
        for name in W_NAMES:
            if name in big:
                out[name] = big[name][None] if name in ("a_w_in", "a_w_out", "b_w_in", "b_w_uq", "b_w_out") else big[name]
            else:
                out[name] = sm[name]
        return [out[name] for name in W_NAMES]

    grad_list = assemble(g_flat, small_sum)
    return (loss_out, grad_x[None], *grad_list, *assemble(d_flat, ds), *assemble(nm_flat, nms), *assemble(nv_flat, nvs))
```

```python
import functools
import math

import jax
import jax.numpy as jnp
from jax import lax
from jax.experimental import pallas as pl
from jax.experimental.pallas import tpu as pltpu

F32 = jnp.float32
BF16 = jnp.bfloat16
MESH = pl.DeviceIdType.MESH

D_MODEL = 1024
DEPTH = 2
H_A, DK_A, DV_A = 4, 256, 512
WIDTH_A = H_A * DV_A
CHUNK = 128
H_B, QK_NOPE, QK_ROPE, V_HEAD = 16, 128, 64, 128
QK_PAD = 256
Q_LORA, KV_LORA = 768, 512
KV_DOWN_PAD = 640
WIDTH_B = H_B * V_HEAD
IN_A = 2 * H_A * DK_A + 2 * WIDTH_A
IN_B = Q_LORA + WIDTH_B
H1_B = KV_DOWN_PAD + IN_B
ROPE_BASE = 10000.0
ALPHA = (2.0 * DEPTH) ** 0.25
ATT_SCALE = (QK_NOPE + QK_ROPE) ** -0.5
NEG_BIG = -1e30

ADAM_LR, ADAM_B1, ADAM_B2, ADAM_EPS, ADAM_WD, ADAM_STEP = 0.001, 0.9, 0.999, 1e-08, 0.01, 10

VMEM_LIMIT_BYTES = 56 * 1024 * 1024
LANES = 128
FLAT_COLS = 1024
SHARD_ROWS = (("a_w_in", 1536), ("a_w_out", 512), ("b_w_in", 704), ("b_w_uq", 576), ("b_w_out", 512),
              ("kv_w_down", 144), ("kv_w_up", 512))
FLAT_ROWS = 4608
HALF_ROWS = FLAT_ROWS // 2
N_CHIPS = 4
N_DEV = 8


def _params(sem, vmem=VMEM_LIMIT_BYTES):
    return pltpu.CompilerParams(dimension_semantics=sem, vmem_limit_bytes=vmem)


def _row_spec(ts, w, col_block=0):
    return pl.BlockSpec((ts, w), lambda i: (i, col_block))


def _bc_spec(shape):
    nd = len(shape)
    return pl.BlockSpec(shape, lambda i: (0,) * nd)


def _sigmoid(x):
    return 1.0 / (1.0 + jnp.exp(-x))


def _fold8(v):
    ts, w = v.shape
    return jnp.sum(v.reshape(ts // 8, 8, w), axis=0)


def _mm(a, b, *, ta=False, tb=False, out_dtype=F32, tm=1024, tn=512, tk=None, name, extras=(), epilogue=None, out_tn=None):
    if ta:
        K, M = a.shape
    else:
        M, K = a.shape
    if tb:
        N, Kb = b.shape
    else:
        Kb, N = b.shape
    assert K == Kb, (a.shape, b.shape)
    tm, tn = min(tm, M), min(tn, N)
    tk = K if tk is None else min(tk, K)
    assert M % tm == 0 and N % tn == 0 and K % tk == 0, (name, M, N, K, tm, tn, tk)
    nk = K // tk
    out_tn = tn if out_tn is None else out_tn
    n_extra = len(extras)
    dims = (((0,) if ta else (1,), (1,) if tb else (0,)), ((), ()))

    def body(a_ref, b_ref, *rest):
        extra_refs, o_ref, scratch = rest[:n_extra], rest[n_extra], rest[n_extra + 1:]
        prod = lax.dot_general(a_ref[...].astype(BF16), b_ref[...].astype(BF16), dims,
                               preferred_element_type=F32)

        def store(tile):
            if epilogue is None:
                o_ref[...] = tile.astype(o_ref.dtype)
            else:
                epilogue(tile, o_ref, *extra_refs)

        if nk == 1:
            store(prod)
        else:
            acc, = scratch
            k = pl.program_id(2)

            @pl.when(k == 0)
            def _():
                acc[...] = prod

            @pl.when(k > 0)
            def _():
                acc[...] += prod

            @pl.when(k == nk - 1)
            def _():
                store(acc[...])

    a_spec = pl.BlockSpec((tk, tm), lambda i, j, k: (k, i)) if ta else pl.BlockSpec((tm, tk), lambda i, j, k: (i, k))
    b_spec = pl.BlockSpec((tn, tk), lambda i, j, k: (j, k)) if tb else pl.BlockSpec((tk, tn), lambda i, j, k: (k, j))
    extra_specs = [pl.BlockSpec((tm, e.shape[1]), lambda i, j, k: (i, 0)) for e in extras]
    return pl.pallas_call(
        body, name=name,
        grid=(M // tm, N // tn, nk),
        in_specs=[a_spec, b_spec] + extra_specs,
        out_specs=pl.BlockSpec((tm, out_tn), lambda i, j, k: (i, j)),
        out_shape=jax.ShapeDtypeStruct((M, (N // tn) * out_tn), out_dtype),
        scratch_shapes=[] if nk == 1 else [pltpu.VMEM((tm, tn), F32)],
        compiler_params=_params(("parallel", "parallel", "arbitrary")),
    )(a, b, *extras)


def _rope_tables_a(s):
    half = DK_A // 2
    inv = ROPE_BASE ** (-jnp.arange(half, dtype=F32) / half)
    ang = jnp.arange(s, dtype=F32)[:, None] * inv[None, :]
    return jnp.cos(ang), jnp.sin(ang)


def _rope_tables_b(s):
    half = QK_ROPE // 2
    inv = ROPE_BASE ** (-jnp.arange(half, dtype=F32) / half)
    ang = jnp.arange(s, dtype=F32)[:, None] * inv[None, :]
    c, sn = jnp.cos(ang), jnp.sin(ang)
    z = jnp.zeros_like(c)
    cos = jnp.concatenate([c, c, z, z], axis=1)
    sa = jnp.concatenate([-sn, z, z, z], axis=1)
    sb = jnp.concatenate([z, sn, z, z], axis=1)
    return cos, sa, sb


def _rope_b(r, cos, sa, sb, sign):
    return r * cos + sign * (pltpu.roll(r, 96, 1) * sa + pltpu.roll(r, 32, 1) * sb)


def _retention_tables():
    lg = jnp.log1p(-jnp.exp2(-5.0 - jnp.arange(H_A, dtype=F32)))
    idx = jnp.arange(CHUNK, dtype=F32)
    diff = idx[:, None] - idx[None, :]
    causal = diff >= 0
    dmat = jnp.where(causal, jnp.exp(jnp.where(causal, diff, 0.0)[None] * lg[:, None, None]), 0.0)
    qdec = jnp.exp((idx + 1.0)[None, :] * lg[:, None])[:, :, None]
    kdec = jnp.exp((CHUNK - 1.0 - idx)[None, :] * lg[:, None])[:, :, None]
    cdec = jnp.broadcast_to(jnp.exp(CHUNK * lg)[:, None, None], (H_A, 1, DV_A))
    return dmat, qdec, kdec, cdec


def _rope_a_fwd(h_a, cos, sin, *, ts):
    s = h_a.shape[0]

    def body(h_ref, c_ref, s_ref, q_ref, k_ref, v_ref):
        c, sn = c_ref[...], s_ref[...]
        for o_ref, base, scale in ((q_ref, 0, 1.0), (k_ref, H_A * DK_A, DK_A ** -0.5)):
            for hd in range(H_A):
                lo = hd * DK_A
                x1 = h_ref[:, base + lo:base + lo + 128]
                x2 = h_ref[:, base + lo + 128:base + lo + 256]
                o_ref[:, lo:lo + 128] = ((x1 * c - x2 * sn) * scale).astype(o_ref.dtype)
                o_ref[:, lo + 128:lo + 256] = ((x2 * c + x1 * sn) * scale).astype(o_ref.dtype)
        v_ref[...] = h_ref[:, 2 * H_A * DK_A:2 * H_A * DK_A + WIDTH_A].astype(v_ref.dtype)

    return pl.pallas_call(
        body, name="rope_a_fwd", grid=(s // ts,),
        in_specs=[_row_spec(ts, 4096), _row_spec(ts, 128), _row_spec(ts, 128)],
        out_specs=[_row_spec(ts, 1024), _row_spec(ts, 1024), _row_spec(ts, 2048)],
        out_shape=[jax.ShapeDtypeStruct((s, 1024), BF16), jax.ShapeDtypeStruct((s, 1024), BF16),
                   jax.ShapeDtypeStruct((s, 2048), BF16)],
        compiler_params=_params(("parallel",)),
    )(h_a, cos, sin)


def _rope_a_bwd(dq, dk, dv16, dg16, cos, sin, *, ts):
    s = dq.shape[0]

    def body(dq_ref, dk_ref, dv_ref, dg_ref, c_ref, s_ref, o_ref):
        c, sn = c_ref[...], s_ref[...]
        for d_ref, base, scale in ((dq_ref, 0, 1.0), (dk_ref, H_A * DK_A, DK_A ** -0.5)):
            for hd in range(H_A):
                lo = hd * DK_A
                d1 = d_ref[:, lo:lo + 128] * scale
                d2 = d_ref[:, lo + 128:lo + 256] * scale
                o_ref[:, base + lo:base + lo + 128] = (d1 * c + d2 * sn).astype(o_ref.dtype)
                o_ref[:, base + lo + 128:base + lo + 256] = (d2 * c - d1 * sn).astype(o_ref.dtype)
        o_ref[:, 2048:4096] = dv_ref[...]
        o_ref[:, 4096:6144] = dg_ref[...]

    return pl.pallas_call(
        body, name="rope_a_bwd", grid=(s // ts,),
        in_specs=[_row_spec(ts, 1024), _row_spec(ts, 1024), _row_spec(ts, 2048), _row_spec(ts, 2048),
                  _row_spec(ts, 128), _row_spec(ts, 128)],
        out_specs=_row_spec(ts, IN_A),
        out_shape=jax.ShapeDtypeStruct((s, IN_A), BF16),
        compiler_params=_params(("parallel",)),
    )(dq, dk, dv16, dg16, cos, sin)


def _gn_gate_fwd(o, h_a, *, ts):
    s = o.shape[0]

    def body(o_ref, g_ref, u_ref):
        for hd in range(H_A):
            sl = slice(hd * DV_A, (hd + 1) * DV_A)
            ov = o_ref[:, sl]
            mu = jnp.mean(ov, axis=-1, keepdims=True)
            oc = ov - mu
            var = jnp.mean(oc * oc, axis=-1, keepdims=True)
            on = oc * lax.rsqrt(var + 1e-5)
            g = g_ref[:, sl]
            u_ref[:, sl] = (on * (g * _sigmoid(g))).astype(u_ref.dtype)

    return pl.pallas_call(
        body, name="gn_gate_fwd", grid=(s // ts,),
        in_specs=[_row_spec(ts, WIDTH_A), _row_spec(ts, WIDTH_A, 2)],
        out_specs=_row_spec(ts, WIDTH_A),
        out_shape=jax.ShapeDtypeStruct((s, WIDTH_A), BF16),
        compiler_params=_params(("parallel",)),
    )(o, h_a)


def _gn_gate_bwd(du, o, h_a, *, ts):
    s = o.shape[0]

    def body(du_ref, o_ref, g_ref, do_ref, dg_ref):
        for hd in range(H_A):
            sl = slice(hd * DV_A, (hd + 1) * DV_A)
            ov = o_ref[:, sl]
            mu = jnp.mean(ov, axis=-1, keepdims=True)
            oc = ov - mu
            var = jnp.mean(oc * oc, axis=-1, keepdims=True)
            rstd = lax.rsqrt(var + 1e-5)
            on = oc * rstd
            g = g_ref[:, sl]
            sg = _sigmoid(g)
            du = du_ref[:, sl]
            don = du * (g * sg)
            dg_ref[:, sl] = (du * on * (sg * (1.0 + g * (1.0 - sg)))).astype(dg_ref.dtype)
            m1 = jnp.mean(don, axis=-1, keepdims=True)
            m2 = jnp.mean(don * on, axis=-1, keepdims=True)
            do_ref[:, sl] = (rstd * (don - m1 - on * m2)).astype(do_ref.dtype)

    return pl.pallas_call(
        body, name="gn_gate_bwd", grid=(s // ts,),
        in_specs=[_row_spec(ts, WIDTH_A), _row_spec(ts, WIDTH_A), _row_spec(ts, WIDTH_A, 2)],
        out_specs=[_row_spec(ts, WIDTH_A), _row_spec(ts, WIDTH_A)],
        out_shape=[jax.ShapeDtypeStruct((s, WIDTH_A), BF16), jax.ShapeDtypeStruct((s, WIDTH_A), BF16)],
        compiler_params=_params(("parallel",)),
    )(du, o, h_a)


def _ln_stats(z):
    mu = jnp.mean(z, axis=-1, keepdims=True)
    zc = z - mu
    var = jnp.mean(zc * zc, axis=-1, keepdims=True)
    rstd = lax.rsqrt(var + 1e-5)
    return zc * rstd, rstd


def _ln_bwd(dy, xhat, rstd, g):
    dxh = dy * g
    m1 = jnp.mean(dxh, axis=-1, keepdims=True)
    m2 = jnp.mean(dxh * xhat, axis=-1, keepdims=True)
    return rstd * (dxh - m1 - xhat * m2)


def _ln_fwd(x, y, g, b, *, ts):
    s = x.shape[0]

    def body(x_ref, y_ref, g_ref, b_ref, o_ref, o16_ref):
        xhat, _ = _ln_stats(ALPHA * x_ref[...] + y_ref[...])
        out = xhat * g_ref[...] + b_ref[...]
        o_ref[...] = out
        o16_ref[...] = out.astype(o16_ref.dtype)

    return pl.pallas_call(
        body, name="ln_fwd", grid=(s // ts,),
        in_specs=[_row_spec(ts, D_MODEL), _row_spec(ts, D_MODEL), _bc_spec((1, D_MODEL)), _bc_spec((1, D_MODEL))],
        out_specs=[_row_spec(ts, D_MODEL), _row_spec(ts, D_MODEL)],
        out_shape=[jax.ShapeDtypeStruct((s, D_MODEL), F32), jax.ShapeDtypeStruct((s, D_MODEL), BF16)],
        compiler_params=_params(("parallel",)),
    )(x, y, g, b)


def _ln_loss_bwd(x1, y, target, g, b, *, ts):
    s = x1.shape[0]
    n = s // ts

    def body(x_ref, y_ref, t_ref, g_ref, b_ref, dz_ref, dz16_ref, dg_ref, db_ref, loss_ref, ag, ab, al):
        i = pl.program_id(0)

        @pl.when(i == 0)
        def _():
            ag[...] = jnp.zeros_like(ag)
            ab[...] = jnp.zeros_like(ab)
            al[...] = jnp.zeros_like(al)

        xhat, rstd = _ln_stats(ALPHA * x_ref[...] + y_ref[...])
        err = xhat * g_ref[...] + b_ref[...] - t_ref[...]
        al[...] += _fold8(err * err)
        dy = err * (1.0 / D_MODEL)
        ag[...] += _fold8(dy * xhat)
        ab[...] += _fold8(dy)
        dz = _ln_bwd(dy, xhat, rstd, g_ref[...])
        dz_ref[...] = dz
        dz16_ref[...] = dz.astype(dz16_ref.dtype)

        @pl.when(i == n - 1)
        def _():
            dg_ref[...] = jnp.sum(ag[...], axis=0, keepdims=True)
            db_ref[...] = jnp.sum(ab[...], axis=0, keepdims=True)
            loss_ref[...] = jnp.full((1, LANES), (0.5 / D_MODEL) * jnp.sum(al[...]), F32)

    return pl.pallas_call(
        body, name="ln_loss_bwd", grid=(n,),
        in_specs=[_row_spec(ts, D_MODEL)] * 3 + [_bc_spec((1, D_MODEL))] * 2,
        out_specs=[_row_spec(ts, D_MODEL), _row_spec(ts, D_MODEL), _bc_spec((1, D_MODEL)), _bc_spec((1, D_MODEL)),
                   _bc_spec((1, LANES))],
        out_shape=[jax.ShapeDtypeStruct((s, D_MODEL), F32), jax.ShapeDtypeStruct((s, D_MODEL), BF16),
                   jax.ShapeDtypeStruct((1, D_MODEL), F32), jax.ShapeDtypeStruct((1, D_MODEL), F32),
                   jax.ShapeDtypeStruct((1, LANES), F32)],
        scratch_shapes=[pltpu.VMEM((8, D_MODEL), F32)] * 3,
        compiler_params=_params(("arbitrary",)),
    )(x1, y, target, g, b)


def _ln_bwd_call(dz_next, dx_branch, x, y, g, *, ts):
    s = x.shape[0]
    n = s // ts

    def body(dzn_ref, dxb_ref, x_ref, y_ref, g_ref, dz_ref, dz16_ref, dg_ref, db_ref, ag, ab):
        i = pl.program_id(0)

        @pl.when(i == 0)
        def _():
            ag[...] = jnp.zeros_like(ag)
            ab[...] = jnp.zeros_like(ab)

        xhat, rstd = _ln_stats(ALPHA * x_ref[...] + y_ref[...])
        dy = ALPHA * dzn_ref[...] + dxb_ref[...]
        ag[...] += _fold8(dy * xhat)
        ab[...] += _fold8(dy)
        dz = _ln_bwd(dy, xhat, rstd, g_ref[...])
        dz_ref[...] = dz
        dz16_ref[...] = dz.astype(dz16_ref.dtype)

        @pl.when(i == n - 1)
        def _():
            dg_ref[...] = jnp.sum(ag[...], axis=0, keepdims=True)
            db_ref[...] = jnp.sum(ab[...], axis=0, keepdims=True)

    return pl.pallas_call(
        body, name="ln_bwd", grid=(n,),
        in_specs=[_row_spec(ts, D_MODEL)] * 4 + [_bc_spec((1, D_MODEL))],
        out_specs=[_row_spec(ts, D_MODEL), _row_spec(ts, D_MODEL), _bc_spec((1, D_MODEL)), _bc_spec((1, D_MODEL))],
        out_shape=[jax.ShapeDtypeStruct((s, D_MODEL), F32), jax.ShapeDtypeStruct((s, D_MODEL), BF16),
                   jax.ShapeDtypeStruct((1, D_MODEL), F32), jax.ShapeDtypeStruct((1, D_MODEL), F32)],
        scratch_shapes=[pltpu.VMEM((8, D_MODEL), F32)] * 2,
        compiler_params=_params(("arbitrary",)),
    )(dz_next, dx_branch, x, y, g)


def _residual_grad(dz, dx_branch, *, ts):
    s = dz.shape[0]

    def body(dz_ref, dxb_ref, o_ref):
        o_ref[...] = ALPHA * dz_ref[...] + dxb_ref[...]

    return pl.pallas_call(
        body, name="residual_grad", grid=(s // ts,),
        in_specs=[_row_spec(ts, D_MODEL)] * 2, out_specs=_row_spec(ts, D_MODEL),
        out_shape=jax.ShapeDtypeStruct((s, D_MODEL), F32),
        compiler_params=_params(("parallel",)),
    )(dz, dx_branch)


C_LAT = slice(0, KV_LORA)
C_ROPE = slice(KV_LORA, KV_DOWN_PAD)
C_QL = slice(KV_DOWN_PAD, KV_DOWN_PAD + Q_LORA)
C_GATE = slice(KV_DOWN_PAD + Q_LORA, H1_B)


def _rms(x, eps=1e-6):
    r = lax.rsqrt(jnp.mean(x * x, axis=-1, keepdims=True) + eps)
    return x * r, r


def _rms_bwd(dy, xhat, r, g):
    dxh = dy * g
    return r * (dxh - xhat * jnp.mean(dxh * xhat, axis=-1, keepdims=True))


def _kvq_prep(h1, kv_norm, q_norm, cos, sa, sb, *, ts):
    s = h1.shape[0]

    def body(h_ref, kn_ref, qn_ref, c_ref, sa_ref, sb_ref, lat_ref, kr_ref, ql_ref):
        lat, _ = _rms(h_ref[:, C_LAT])
        lat_ref[...] = (lat * kn_ref[...]).astype(lat_ref.dtype)
        kr_ref[...] = _rope_b(h_ref[:, C_ROPE], c_ref[...], sa_ref[...], sb_ref[...], 1.0).astype(kr_ref.dtype)
        ql, _ = _rms(h_ref[:, C_QL])
        ql_ref[...] = (ql * qn_ref[...]).astype(ql_ref.dtype)

    return pl.pallas_call(
        body, name="kvq_prep", grid=(s // ts,),
        in_specs=[_row_spec(ts, H1_B), _bc_spec((1, KV_LORA)), _bc_spec((1, Q_LORA))] + [_row_spec(ts, 128)] * 3,
        out_specs=[_row_spec(ts, KV_LORA), _row_spec(ts, 128), _row_spec(ts, Q_LORA)],
        out_shape=[jax.ShapeDtypeStruct((s, KV_LORA), BF16), jax.ShapeDtypeStruct((s, 128), BF16),
                   jax.ShapeDtypeStruct((s, Q_LORA), BF16)],
        compiler_params=_params(("parallel",)),
    )(h1, kv_norm, q_norm, cos, sa, sb)


LOG2E = 1.4426950408889634
LN2 = 0.6931471805599453
Q_SCALE = ATT_SCALE * LOG2E


def _rope_q_store(tile, o_ref, c_ref, sa_ref, sb_ref):
    c, a, b = c_ref[...], sa_ref[...], sb_ref[...]
    for hd in range(tile.shape[1] // QK_PAD):
        lo = hd * QK_PAD
        o_ref[:, lo:lo + 128] = (tile[:, lo:lo + 128] * Q_SCALE).astype(o_ref.dtype)
        o_ref[:, lo + 128:lo + 256] = (_rope_b(tile[:, lo + 128:lo + 256], c, a, b, 1.0) * Q_SCALE).astype(o_ref.dtype)


def _assemble_k_store(tile, o_ref, kr_ref):
    r = kr_ref[...]
    for hd in range(tile.shape[1] // QK_NOPE):
        o_ref[:, hd * QK_PAD:hd * QK_PAD + 128] = tile[:, hd * 128:(hd + 1) * 128].astype(o_ref.dtype)
        o_ref[:, hd * QK_PAD + 128:(hd + 1) * QK_PAD] = r


def _h1_bwd(h1, dlat_k, dlat_v, dkr_heads, dqn, dg16, kv_norm, q_norm, cos, sa, sb, *, ts):
    s = h1.shape[0]
    n = s // ts

    def body(h_ref, dk_ref, dv_ref, dkr_ref, dqn_ref, dg_ref, kn_ref, qn_ref, c_ref, sa_ref, sb_ref,
             o_ref, dkn_ref, dqn_out_ref, akn, aqn):
        i = pl.program_id(0)

        @pl.when(i == 0)
        def _():
            akn[...] = jnp.zeros_like(akn)
            aqn[...] = jnp.zeros_like(aqn)

        lat, r = _rms(h_ref[:, C_LAT])
        dlat = dk_ref[...] + dv_ref[...]
        akn[...] += _fold8(dlat * lat)
        o_ref[:, C_LAT] = _rms_bwd(dlat, lat, r, kn_ref[...]).astype(o_ref.dtype)

        dkr = dkr_ref[:, 0:128]
        for hd in range(1, H_B):
            dkr = dkr + dkr_ref[:, hd * 128:(hd + 1) * 128]
        o_ref[:, C_ROPE] = _rope_b(dkr, c_ref[...], sa_ref[...], sb_ref[...], -1.0).astype(o_ref.dtype)

        ql, rq = _rms(h_ref[:, C_QL])
        dq = dqn_ref[...]
        aqn[...] += _fold8(dq * ql)
        o_ref[:, C_QL] = _rms_bwd(dq, ql, rq, qn_ref[...]).astype(o_ref.dtype)
        o_ref[:, C_GATE] = dg_ref[...]

        @pl.when(i == n - 1)
        def _():
            dkn_ref[...] = jnp.sum(akn[...], axis=0, keepdims=True)
            dqn_out_ref[...] = jnp.sum(aqn[...], axis=0, keepdims=True)

    return pl.pallas_call(
        body, name="h1_bwd", grid=(n,),
        in_specs=[_row_spec(ts, H1_B), _row_spec(ts, KV_LORA), _row_spec(ts, KV_LORA), _row_spec(ts, H_B * 128),
                  _row_spec(ts, Q_LORA), _row_spec(ts, WIDTH_B), _bc_spec((1, KV_LORA)), _bc_spec((1, Q_LORA))]
        + [_row_spec(ts, 128)] * 3,
        out_specs=[_row_spec(ts, H1_B), _bc_spec((1, KV_LORA)), _bc_spec((1, Q_LORA))],
        out_shape=[jax.ShapeDtypeStruct((s, H1_B), BF16), jax.ShapeDtypeStruct((1, KV_LORA), F32),
                   jax.ShapeDtypeStruct((1, Q_LORA), F32)],
        scratch_shapes=[pltpu.VMEM((8, KV_LORA), F32), pltpu.VMEM((8, Q_LORA), F32)],
        compiler_params=_params(("arbitrary",)),
    )(h1, dlat_k, dlat_v, dkr_heads, dqn, dg16, kv_norm, q_norm, cos, sa, sb)


def _dot(a, b, ca, cb):
    return lax.dot_general(a, b, (((ca,), (cb,)), ((), ())), preferred_element_type=F32)


def _retention_specs(n, idx):
    full = lambda shape: pl.BlockSpec(shape, lambda i: (0,) * len(shape))
    return ([pl.BlockSpec((CHUNK, H_A * DK_A), lambda i: (idx(i), 0))] * 2 + [pl.BlockSpec((CHUNK, WIDTH_A), lambda i: (idx(i), 0))],
            [full((H_A, CHUNK, CHUNK)), full((H_A, CHUNK, 1)), full((H_A, CHUNK, 1)), full((H_A, 1, DV_A))])


def _retention_fwd(q, k, v, tables):
    s = q.shape[0]
    n = s // CHUNK

    def body(q_ref, k_ref, v_ref, dm_ref, qd_ref, kd_ref, cd_ref, o_ref, st_ref, state):
        @pl.when(pl.program_id(0) == 0)
        def _():
            state[...] = jnp.zeros_like(state)

        for hd in range(H_A):
            qs, vs = slice(hd * DK_A, (hd + 1) * DK_A), slice(hd * DV_A, (hd + 1) * DV_A)
            qv, kv, vv = q_ref[:, qs], k_ref[:, qs], v_ref[:, vs]
            st = state[hd]
            st16 = st.astype(BF16)
            st_ref[0, hd] = st16
            scores = _dot(qv, kv, 1, 1) * dm_ref[hd]
            qd = (qv.astype(F32) * qd_ref[hd]).astype(BF16)
            o_ref[:, vs] = _dot(scores.astype(BF16), vv, 1, 0) + _dot(qd, st16, 1, 0)
            kd = (kv.astype(F32) * kd_ref[hd]).astype(BF16)
            state[hd] = st * cd_ref[hd] + _dot(kd, vv, 0, 0)

    rows, tabs = _retention_specs(n, lambda i: i)
    return pl.pallas_call(
        body, name="retention_fwd", grid=(n,),
        in_specs=rows + tabs,
        out_specs=[pl.BlockSpec((CHUNK, WIDTH_A), lambda i: (i, 0)),
                   pl.BlockSpec((1, H_A, DK_A, DV_A), lambda i: (i, 0, 0, 0))],
        out_shape=[jax.ShapeDtypeStruct((s, WIDTH_A), F32), jax.ShapeDtypeStruct((n, H_A, DK_A, DV_A), BF16)],
        scratch_shapes=[pltpu.VMEM((H_A, DK_A, DV_A), F32)],
        compiler_params=_params(("arbitrary",)),
    )(q, k, v, *tables)


def _retention_bwd(q, k, v, states, do, tables):
    s = q.shape[0]
    n = s // CHUNK

    def body(q_ref, k_ref, v_ref, dm_ref, qd_ref, kd_ref, cd_ref, st_ref, do_ref, dq_ref, dk_ref, dv_ref, grad_state):
        @pl.when(pl.program_id(0) == 0)
        def _():
            grad_state[...] = jnp.zeros_like(grad_state)

        for hd in range(H_A):
            qs, vs = slice(hd * DK_A, (hd + 1) * DK_A), slice(hd * DV_A, (hd + 1) * DV_A)
            qv, kv, vv, dov = q_ref[:, qs], k_ref[:, qs], v_ref[:, vs], do_ref[:, vs]
            dm = dm_ref[hd]
            gs = grad_state[hd]
            g16 = gs.astype(BF16)
            scores = (_dot(qv, kv, 1, 1) * dm).astype(BF16)
            dscores = (_dot(dov, vv, 1, 1) * dm).astype(BF16)
            qd = (qv.astype(F32) * qd_ref[hd]).astype(BF16)
            kd = (kv.astype(F32) * kd_ref[hd]).astype(BF16)
            dq_ref[:, qs] = _dot(dscores, kv, 1, 0) + _dot(dov, st_ref[0, hd], 1, 1) * qd_ref[hd]
            dk_ref[:, qs] = _dot(dscores, qv, 0, 0) + _dot(vv, g16, 1, 1) * kd_ref[hd]
            dv_ref[:, vs] = (_dot(scores, dov, 0, 0) + _dot(kd, g16, 1, 0)).astype(dv_ref.dtype)
            grad_state[hd] = gs * cd_ref[hd] + _dot(qd, dov, 0, 0)

    rev = lambda i: n - 1 - i
    rows, tabs = _retention_specs(n, rev)
    return pl.pallas_call(
        body, name="retention_bwd", grid=(n,),
        in_specs=rows + tabs + [pl.BlockSpec((1, H_A, DK_A, DV_A), lambda i: (rev(i), 0, 0, 0)),
                                pl.BlockSpec((CHUNK, WIDTH_A), lambda i: (rev(i), 0))],
        out_specs=[pl.BlockSpec((CHUNK, H_A * DK_A), lambda i: (rev(i), 0))] * 2
        + [pl.BlockSpec((CHUNK, WIDTH_A), lambda i: (rev(i), 0))],
        out_shape=[jax.ShapeDtypeStruct((s, H_A * DK_A), F32), jax.ShapeDtypeStruct((s, H_A * DK_A), F32),
                   jax.ShapeDtypeStruct((s, WIDTH_A), BF16)],
        scratch_shapes=[pltpu.VMEM((H_A, DK_A, DV_A), F32)],
        compiler_params=_params(("arbitrary",)),
    )(q, k, v, *tables, states, do)


GATE_BLOCK0 = (KV_DOWN_PAD + Q_LORA) // LANES


def _causal_mask(sc, row0):
    row = lax.broadcasted_iota(jnp.int32, sc.shape, 0) + row0
    col = lax.broadcasted_iota(jnp.int32, sc.shape, 1)
    return jnp.where(col <= row, sc, NEG_BIG)


def _key_block_loop(step, i, per_trip=2):
    def trip_body(jj, carry):
        for t in range(per_trip):
            step(per_trip * jj + t, False)
        return carry

    lax.fori_loop(0, i // per_trip, trip_body, 0)
    group = per_trip // 2
    while group >= 1:
        def tail(group=group):
            first = (i // (2 * group)) * (2 * group)
            for t in range(group):
                step(first + t, False)

        pl.when((i // group) % 2 == 1)(tail)
        group //= 2
    step(i, True)


def _attention_fwd(q, k, v, h1, *, blk, sub):
    s = q.shape[0]
    nb = s // blk
    reps = blk // LANES

    def body(q_ref, k_ref, v_ref, g_ref, o_ref, u_ref, lse_ref, vext_s, m_s, acc_s):
        i = pl.program_id(1)

        @pl.when(i == 0)
        def _():
            vext_s[:, 0:V_HEAD] = v_ref[...]
            vext_s[:, V_HEAD:2 * V_HEAD] = jnp.ones((s, V_HEAD), vext_s.dtype)

        m_s[...] = jnp.full_like(m_s, NEG_BIG)
        acc_s[...] = jnp.zeros_like(acc_s)

        def step(j, diagonal):
            kv_rows = pl.ds(pl.multiple_of(j * blk, blk), blk)
            kb = k_ref[kv_rows, :]
            vb = vext_s[kv_rows, :]
            for r in range(blk // sub):
                rows = slice(r * sub, (r + 1) * sub)
                sc = _dot(q_ref[rows, :], kb, 1, 1)
                if diagonal:
                    sc = _causal_mask(sc, r * sub)
                m_prev = m_s[rows, :]
                m_new = jnp.maximum(m_prev, jnp.max(sc, axis=-1, keepdims=True))
                p = jnp.exp2(sc - jnp.tile(m_new, (1, reps)))
                a = jnp.exp2(m_prev - m_new)
                acc_s[rows, :] = jnp.tile(a, (1, 2)) * acc_s[rows, :] + _dot(p.astype(BF16), vb, 1, 0)
                m_s[rows, :] = m_new

        _key_block_loop(step, i, per_trip=4)
        acc = acc_s[...]
        l = acc[:, V_HEAD:2 * V_HEAD]
        o = acc[:, 0:V_HEAD] / l
        g = g_ref[...]
        o_ref[...] = o
        u_ref[...] = (o * (g * _sigmoid(g))).astype(u_ref.dtype)
        lse_ref[0] = (m_s[...] + jnp.log2(l))[:, 0:1]

    blk_idx = lambda h, i: (i, h)
    return pl.pallas_call(
        body, name="attention_fwd", grid=(H_B, nb),
        in_specs=[pl.BlockSpec((blk, QK_PAD), blk_idx), pl.BlockSpec((s, QK_PAD), lambda h, i: (0, h)),
                  pl.BlockSpec((s, V_HEAD), lambda h, i: (0, h)), pl.BlockSpec((blk, LANES), lambda h, i: (i, GATE_BLOCK0 + h))],
        out_specs=[pl.BlockSpec((blk, V_HEAD), blk_idx), pl.BlockSpec((blk, V_HEAD), blk_idx),
                   pl.BlockSpec((1, blk, 1), lambda h, i: (h, i, 0))],
        out_shape=[jax.ShapeDtypeStruct((s, WIDTH_B), F32), jax.ShapeDtypeStruct((s, WIDTH_B), BF16),
                   jax.ShapeDtypeStruct((H_B, s, 1), F32)],
        scratch_shapes=[pltpu.VMEM((s, 2 * V_HEAD), BF16), pltpu.VMEM((blk, LANES), F32), pltpu.VMEM((blk, 2 * V_HEAD), F32)],
        compiler_params=_params(("parallel", "arbitrary")),
    )(q, k, v, h1)


def _attention_bwd(q, k, v, du, o, h1, lse, cos, sa, sb, *, blk):
    s = q.shape[0]
    nb = s // blk
    reps = blk // LANES

    def body(q_ref, k_ref, v_ref, du_ref, o_ref, g_ref, lse_ref, c_ref, sa_ref, sb_ref,
             dq_ref, dkn_ref, dkr_ref, dv_ref, dg_ref, lse_s, dl_s, do_s, dq_s):
        i = pl.program_id(1)
        g = g_ref[...]
        sg = _sigmoid(g)
        du_v, ov = du_ref[...], o_ref[...]
        do = du_v * (g * sg)
        dg_ref[...] = (du_v * ov * (sg * (1.0 + g * (1.0 - sg)))).astype(dg_ref.dtype)
        do_s[...] = do.astype(do_s.dtype)
        dl_s[...] = jnp.broadcast_to(jnp.sum(do * ov, axis=-1, keepdims=True), (blk, LANES))
        lse_s[...] = jnp.broadcast_to(lse_ref[0], (blk, LANES))
        dq_s[...] = jnp.zeros_like(dq_s)

        def step(j, diagonal):
            kv_rows = pl.ds(pl.multiple_of(j * blk, blk), blk)
            qv, dov, kb = q_ref[...], do_s[...], k_ref[kv_rows, :]
            sc = _dot(qv, kb, 1, 1)
            if diagonal:
                sc = _causal_mask(sc, 0)
            p = jnp.exp2(sc - jnp.tile(lse_s[...], (1, reps)))
            dp = _dot(dov, v_ref[kv_rows, :], 1, 1)
            ds = (p * (dp - jnp.tile(dl_s[...], (1, reps)))).astype(BF16)
            dv_c = _dot(p.astype(BF16), dov, 0, 0)
            dk_c = _dot(ds, qv, 0, 0)
            dq_s[...] += _dot(ds, kb, 1, 0)
            if diagonal:
                dkn_ref[kv_rows, :] = dk_c[:, 0:128]
                dkr_ref[kv_rows, :] = dk_c[:, 128:256]
                dv_ref[kv_rows, :] = dv_c
            else:
                dkn_ref[kv_rows, :] += dk_c[:, 0:128]
                dkr_ref[kv_rows, :] += dk_c[:, 128:256]
                dv_ref[kv_rows, :] += dv_c

        _key_block_loop(step, i)
        dq = dq_s[...] * ATT_SCALE
        dq_ref[:, 0:128] = dq[:, 0:128].astype(dq_ref.dtype)
        dq_ref[:, 128:256] = _rope_b(dq[:, 128:256], c_ref[...], sa_ref[...], sb_ref[...], -1.0).astype(dq_ref.dtype)

        @pl.when(i == nb - 1)
        def _():
            dkn_ref[...] = dkn_ref[...] * LN2
            dkr_ref[...] = dkr_ref[...] * LN2

    head = lambda h, i: (0, h)
    blk_idx = lambda h, i: (i, h)
    row_idx = lambda h, i: (i, 0)
    return pl.pallas_call(
        body, name="attention_bwd", grid=(H_B, nb),
        in_specs=[pl.BlockSpec((blk, QK_PAD), blk_idx), pl.BlockSpec((s, QK_PAD), head), pl.BlockSpec((s, V_HEAD), head),
                  pl.BlockSpec((blk, V_HEAD), blk_idx), pl.BlockSpec((blk, V_HEAD), blk_idx),
                  pl.BlockSpec((blk, LANES), lambda h, i: (i, GATE_BLOCK0 + h)), pl.BlockSpec((1, blk, 1), lambda h, i: (h, i, 0)),
                  pl.BlockSpec((blk, LANES), row_idx), pl.BlockSpec((blk, LANES), row_idx), pl.BlockSpec((blk, LANES), row_idx)],
        out_specs=[pl.BlockSpec((blk, QK_PAD), blk_idx), pl.BlockSpec((s, 128), head), pl.BlockSpec((s, 128), head),
                   pl.BlockSpec((s, 128), head), pl.BlockSpec((blk, V_HEAD), blk_idx)],
        out_shape=[jax.ShapeDtypeStruct((s, H_B * QK_PAD), BF16), jax.ShapeDtypeStruct((s, H_B * 128), F32),
                   jax.ShapeDtypeStruct((s, H_B * 128), F32), jax.ShapeDtypeStruct((s, H_B * 128), F32),
                   jax.ShapeDtypeStruct((s, WIDTH_B), BF16)],
        scratch_shapes=[pltpu.VMEM((blk, LANES), F32), pltpu.VMEM((blk, LANES), F32), pltpu.VMEM((blk, V_HEAD), BF16),
                        pltpu.VMEM((blk, QK_PAD), F32)],
        compiler_params=_params(("parallel", "arbitrary")),
    )(q, k, v, du, o, h1, lse, cos, sa, sb)


def _local_step(x, target, w, kv_norm, q_norm, ln_g, ln_b, *, ts=256, blk=512):
    s = x.shape[0]
    cos_a, sin_a = _rope_tables_a(s)
    cos_b, sa_b, sb_b = _rope_tables_b(s)
    tables = _retention_tables()
    g0, g1, b0, b1 = ln_g[0:1], ln_g[1:2], ln_b[0:1], ln_b[1:2]

    x16 = x.astype(BF16)
    h_a = _mm(x16, w["a_in"], tn=1536, name="a_in_fwd")
    q_a, k_a, v_a = _rope_a_fwd(h_a, cos_a, sin_a, ts=ts)
    o_a, states = _retention_fwd(q_a, k_a, v_a, tables)
    u_a = _gn_gate_fwd(o_a, h_a, ts=ts)
    y_a = _mm(u_a, w["a_out"], tn=1024, name="a_out_fwd")
    x1, x1_16 = _ln_fwd(x, y_a, g0, b0, ts=ts)

    h1 = _mm(x1_16, w["b_in1"], tn=1152, name="b_in_fwd")
    lat16, kr16, qn16 = _kvq_prep(h1, kv_norm, q_norm, cos_b, sa_b, sb_b, ts=ts)
    k16 = _mm(lat16, w["up_k"], out_dtype=BF16, tn=2048, out_tn=H_B * QK_PAD, extras=(kr16,), epilogue=_assemble_k_store, name="up_k_fwd")
    v16 = _mm(lat16, w["up_v"], out_dtype=BF16, tn=2048, name="up_v_fwd")
    q16 = _mm(qn16, w["uq"], out_dtype=BF16, tn=2048, extras=(cos_b, sa_b, sb_b), epilogue=_rope_q_store, name="uq_fwd")
    o_b, u_b, lse = _attention_fwd(q16, k16, v16, h1, blk=blk, sub=blk // 2)
    y_b = _mm(u_b, w["b_out"], tn=1024, name="b_out_fwd")

    dz_b, dz_b16, dg1, db1, loss = _ln_loss_bwd(x1, y_b, target, g1, b1, ts=ts)
    d_b_out = _mm(u_b, dz_b16, ta=True, tn=1024, tk=1024, name="b_out_dw")
    du_b = _mm(dz_b16, w["b_out"], tb=True, tn=1024, name="b_out_dx")
    dqf16, dkn, dkr_heads, dv, dgate16 = _attention_bwd(q16, k16, v16, du_b, o_b, h1, lse, cos_b, sa_b, sb_b, blk=blk)
    d_uq = _mm(qn16, dqf16, ta=True, tm=768, tn=2048, tk=1024, name="uq_dw")
    dqn = _mm(dqf16, w["uq"], tb=True, tn=768, tk=2048, name="uq_dx")
    d_up_k = _mm(lat16, dkn, ta=True, tn=2048, tk=1024, name="up_k_dw")
    d_up_v = _mm(lat16, dv, ta=True, tn=2048, tk=1024, name="up_v_dw")
    dlat_k = _mm(dkn, w["up_k"], tb=True, tn=512, name="up_k_dx")
    dlat_v = _mm(dv, w["up_v"], tb=True, tn=512, name="up_v_dx")
    dh1, dkvn, dqnorm = _h1_bwd(h1, dlat_k, dlat_v, dkr_heads, dqn, dgate16, kv_norm, q_norm, cos_b, sa_b, sb_b, ts=ts)
    d_b_in1 = _mm(x1_16, dh1, ta=True, tn=1152, tk=1024, name="b_in_dw")
    dx1 = _mm(dh1, w["b_in1"], tb=True, tn=1024, name="b_in_dx")

    dz_a, dz_a16, dg0, db0 = _ln_bwd_call(dz_b, dx1, x, y_a, g0, ts=ts)
    d_a_out = _mm(u_a, dz_a16, ta=True, tn=1024, tk=1024, name="a_out_dw")
    du_a = _mm(dz_a16, w["a_out"], tb=True, tn=1024, name="a_out_dx")
    do_a16, dgate_a16 = _gn_gate_bwd(du_a, o_a, h_a, ts=ts)
    dq_a, dk_a, dv_a16 = _retention_bwd(q_a, k_a, v_a, states, do_a16, tables)
    dh_a = _rope_a_bwd(dq_a, dk_a, dv_a16, dgate_a16, cos_a, sin_a, ts=ts)
    d_a_in = _mm(x16, dh_a, ta=True, tn=1536, tk=1024, name="a_in_dw")
    dx_a = _mm(dh_a, w["a_in"], tb=True, tn=1024, tk=2048, name="a_in_dx")
    grad_x = _residual_grad(dz_a, dx_a, ts=ts)

    grads = dict(a_in=d_a_in, a_out=d_a_out, b_in1=d_b_in1, uq=d_uq, b_out=d_b_out, up_k=d_up_k, up_v=d_up_v)
    small = dict(ln_g=jnp.concatenate([dg0, dg1], axis=0), ln_b=jnp.concatenate([db0, db1], axis=0),
                 q_norm=dqnorm, kv_norm=dkvn)
    return loss, grad_x, grads, small


def _flat_shards(shards, dtype):
    parts = [shards[name].reshape(rows, FLAT_COLS) for name, rows in SHARD_ROWS]
    used = sum(rows for _, rows in SHARD_ROWS)
    parts.append(jnp.zeros((FLAT_ROWS - used, FLAT_COLS), shards["a_w_in"].dtype))
    return jnp.concatenate(parts, axis=0).astype(dtype)


def _unflat_shards(flat, shapes):
    out, off = {}, 0
    for name, rows in SHARD_ROWS:
        out[name] = flat[off:off + rows].reshape(shapes[name])
        off += rows
    return out


COL_SHARDED = {"a_w_in": (D_MODEL, IN_A), "b_w_in": (D_MODEL, IN_B), "b_w_uq": (Q_LORA, H_B * (QK_NOPE + QK_ROPE)),
               "kv_w_up": (KV_LORA, H_B * (QK_NOPE + V_HEAD))}
ROW_SHARDED = {"a_w_out": (WIDTH_A, D_MODEL), "b_w_out": (WIDTH_B, D_MODEL), "kv_w_down": (D_MODEL, KV_LORA + QK_ROPE)}


def _full_from_gathered(gathered):
    out, off = {}, 0
    for name, rows in SHARD_ROWS:
        part = gathered[:, off:off + rows]
        off += rows
        if name in COL_SHARDED:
            r, c = COL_SHARDED[name]
            out[name] = part.reshape(N_CHIPS, r, c // N_CHIPS).transpose(1, 0, 2).reshape(r, c)
        else:
            r, c = ROW_SHARDED[name]
            out[name] = part.reshape(r, c)
    return out


def _gathered_from_full(full):
    parts = []
    for name, rows in SHARD_ROWS:
        g = full[name]
        if name in COL_SHARDED:
            r, c = COL_SHARDED[name]
            g = g.reshape(r, N_CHIPS, c // N_CHIPS).transpose(1, 0, 2)
        parts.append(g.reshape(N_CHIPS, rows, FLAT_COLS))
    used = sum(rows for _, rows in SHARD_ROWS)
    parts.append(jnp.zeros((N_CHIPS, FLAT_ROWS - used, FLAT_COLS), F32))
    return jnp.concatenate(parts, axis=1)


def _kernel_layout(full):
    uq = full["b_w_uq"].reshape(Q_LORA, H_B, QK_NOPE + QK_ROPE)
    uq = jnp.pad(uq, ((0, 0), (0, 0), (0, QK_PAD - QK_NOPE - QK_ROPE))).reshape(Q_LORA, H_B * QK_PAD)
    up = full["kv_w_up"].reshape(KV_LORA, H_B, QK_NOPE + V_HEAD)
    down = jnp.pad(full["kv_w_down"], ((0, 0), (0, KV_DOWN_PAD - KV_LORA - QK_ROPE)))
    return dict(a_in=full["a_w_in"], a_out=full["a_w_out"], b_out=full["b_w_out"], uq=uq,
                up_k=up[:, :, :QK_NOPE].reshape(KV_LORA, H_B * QK_NOPE),
                up_v=up[:, :, QK_NOPE:].reshape(KV_LORA, H_B * V_HEAD),
                b_in1=jnp.concatenate([down, full["b_w_in"]], axis=1))


def _reference_layout_grads(g):
    uq = g["uq"].reshape(Q_LORA, H_B, QK_PAD)[:, :, :QK_NOPE + QK_ROPE].reshape(Q_LORA, H_B * (QK_NOPE + QK_ROPE))
    up = jnp.concatenate([g["up_k"].reshape(KV_LORA, H_B, QK_NOPE), g["up_v"].reshape(KV_LORA, H_B, V_HEAD)], axis=2)
    return dict(a_w_in=g["a_in"], a_w_out=g["a_out"], b_w_out=g["b_out"], b_w_uq=uq,
                kv_w_up=up.reshape(KV_LORA, H_B * (QK_NOPE + V_HEAD)),
                kv_w_down=g["b_in1"][:, :KV_LORA + QK_ROPE], b_w_in=g["b_in1"][:, KV_DOWN_PAD:])


HBM_SPEC = pl.BlockSpec(memory_space=pl.ANY)


def _me():
    return lax.axis_index("x"), lax.axis_index("y"), lax.axis_index("c")


def _chip_flips(x, y):
    return [(1 - x, y), (x, 1 - y), (1 - x, 1 - y)]


def _gather_weights(flat16):
    def body(src_ref, out_ref, send_sems, recv_sems):
        x, y, c = _me()
        mine = 2 * x + y
        my_rows = pl.ds(pl.multiple_of(c * HALF_ROWS, 8), HALF_ROWS)
        their_rows = pl.ds(pl.multiple_of((1 - c) * HALF_ROWS, 8), HALF_ROWS)
        chips = _chip_flips(x, y)

        def copy(k, src, dst, to):
            return pltpu.make_async_remote_copy(src_ref=src, dst_ref=dst, send_sem=send_sems.at[k], recv_sem=recv_sems.at[k],
                                                device_id=to, device_id_type=MESH)

        first = [copy(k, src_ref.at[my_rows, :], out_ref.at[mine, my_rows, :], (px, py, c)) for k, (px, py) in enumerate(chips)]
        for cp in first:
            cp.start()
        passed = []
        for k, (px, py) in enumerate(chips):
            landed = out_ref.at[2 * px + py, my_rows, :]
            copy(k, landed, landed, (px, py, c)).wait_recv()
            passed.append(copy(3 + k, landed, landed, (x, y, 1 - c)))
            passed[-1].start()
        for k, (px, py) in enumerate(chips):
            theirs = out_ref.at[2 * px + py, their_rows, :]
            copy(3 + k, theirs, theirs, (x, y, 1 - c)).wait_recv()
        for cp in first + passed:
            cp.wait_send()

    return pl.pallas_call(
        body, name="gather_weights",
        in_specs=[HBM_SPEC], out_specs=HBM_SPEC,
        out_shape=jax.ShapeDtypeStruct((N_CHIPS,) + flat16.shape, flat16.dtype),
        scratch_shapes=[pltpu.SemaphoreType.DMA((6,)), pltpu.SemaphoreType.DMA((6,))],
    )(flat16)


def _pair_exchange_halves(g):
    def body(g_ref, out_ref, send_sem, recv_sem):
        x, y, c = _me()
        theirs = g_ref.at[:, pl.ds(pl.multiple_of((1 - c) * HALF_ROWS, 8), HALF_ROWS), :]
        cp = pltpu.make_async_remote_copy(src_ref=theirs, dst_ref=out_ref, send_sem=send_sem, recv_sem=recv_sem,
                                          device_id=(x, y, 1 - c), device_id_type=MESH)
        cp.start()
        cp.wait_send()
        cp.wait_recv()

    return pl.pallas_call(
        body, name="pair_exchange_halves",
        in_specs=[HBM_SPEC], out_specs=HBM_SPEC,
        out_shape=jax.ShapeDtypeStruct((N_CHIPS, HALF_ROWS, FLAT_COLS), g.dtype),
        scratch_shapes=[pltpu.SemaphoreType.DMA, pltpu.SemaphoreType.DMA],
    )(g)


def _chip_exchange(p):
    def body(p_ref, out_ref, send_sems, recv_sems):
        x, y, c = _me()
        copies = []
        for k, (px, py) in enumerate(_chip_flips(x, y)):
            copies.append(pltpu.make_async_remote_copy(
                src_ref=p_ref.at[2 * px + py], dst_ref=out_ref.at[k], send_sem=send_sems.at[k], recv_sem=recv_sems.at[k],
                device_id=(px, py, c), device_id_type=MESH))
        for cp in copies:
            cp.start()
        for cp in copies:
            cp.wait_send()
        for cp in copies:
            cp.wait_recv()

    return pl.pallas_call(
        body, name="chip_exchange",
        in_specs=[HBM_SPEC], out_specs=HBM_SPEC,
        out_shape=jax.ShapeDtypeStruct((3, HALF_ROWS, FLAT_COLS), p.dtype),
        scratch_shapes=[pltpu.SemaphoreType.DMA((3,)), pltpu.SemaphoreType.DMA((3,))],
    )(p)


def _pair_share(r):
    def body(r_ref, out_ref, send_sem, recv_sem):
        x, y, c = _me()
        cp = pltpu.make_async_remote_copy(src_ref=r_ref, dst_ref=out_ref, send_sem=send_sem, recv_sem=recv_sem,
                                          device_id=(x, y, 1 - c), device_id_type=MESH)
        cp.start()
        cp.wait_send()
        cp.wait_recv()

    return pl.pallas_call(
        body, name="pair_share",
        in_specs=[HBM_SPEC], out_specs=HBM_SPEC,
        out_shape=jax.ShapeDtypeStruct(r.shape, r.dtype),
        scratch_shapes=[pltpu.SemaphoreType.DMA, pltpu.SemaphoreType.DMA],
    )(r)


def _sum_small(vec):
    def body(v_ref, out_ref, slots, send_sems, recv_sems):
        x, y, c = _me()
        me = 4 * x + 2 * y + c
        slots[me] = v_ref[...]
        flips = [(fx, fy, fc) for fx in (0, 1) for fy in (0, 1) for fc in (0, 1)][1:]
        copies = []
        for k, (fx, fy, fc) in enumerate(flips):
            copies.append(pltpu.make_async_remote_copy(
                src_ref=v_ref, dst_ref=slots.at[me], send_sem=send_sems.at[k], recv_sem=recv_sems.at[k],
                device_id=(x ^ fx, y ^ fy, c ^ fc), device_id_type=MESH))
        for cp in copies:
            cp.start()
        for cp in copies:
            cp.wait_send()
        for k, (fx, fy, fc) in enumerate(flips):
            src = 4 * (x ^ fx) + 2 * (y ^ fy) + (c ^ fc)
            pltpu.make_async_remote_copy(
                src_ref=v_ref, dst_ref=slots.at[src], send_sem=send_sems.at[k], recv_sem=recv_sems.at[k],
                device_id=(x ^ fx, y ^ fy, c ^ fc), device_id_type=MESH).wait_recv()
        total = slots[0]
        for d in range(1, N_DEV):
            total = total + slots[d]
        out_ref[...] = total

    return pl.pallas_call(
        body, name="sum_small",
        in_specs=[pl.BlockSpec(memory_space=pltpu.VMEM)], out_specs=pl.BlockSpec(memory_space=pltpu.VMEM),
        out_shape=jax.ShapeDtypeStruct(vec.shape, vec.dtype),
        scratch_shapes=[pltpu.VMEM((N_DEV,) + vec.shape, vec.dtype), pltpu.SemaphoreType.DMA((7,)),
                        pltpu.SemaphoreType.DMA((7,))],
    )(vec)


UPD_ROWS = 256


def _pair_sum(g, theirs, core, chip):
    nb = HALF_ROWS // UPD_ROWS

    def body(core_ref, chip_ref, g_ref, t_ref, own_ref, o16_ref):
        total = g_ref[0] + t_ref[0].astype(F32)
        o16_ref[0] = total.astype(o16_ref.dtype)

        @pl.when(pl.program_id(1) == chip_ref[0])
        def _():
            own_ref[...] = total

    return pl.pallas_call(
        body, name="pair_sum",
        grid_spec=pltpu.PrefetchScalarGridSpec(
            num_scalar_prefetch=2, grid=(nb, N_CHIPS),
            in_specs=[pl.BlockSpec((1, UPD_ROWS, FLAT_COLS), lambda i, d, core_ref, chip_ref: (d, core_ref[0] * nb + i, 0)),
                      pl.BlockSpec((1, UPD_ROWS, FLAT_COLS), lambda i, d, core_ref, chip_ref: (d, i, 0))],
            out_specs=[pl.BlockSpec((UPD_ROWS, FLAT_COLS), lambda i, d, core_ref, chip_ref: (i, 0)),
                       pl.BlockSpec((1, UPD_ROWS, FLAT_COLS), lambda i, d, core_ref, chip_ref: (d, i, 0))]),
        out_shape=[jax.ShapeDtypeStruct((HALF_ROWS, FLAT_COLS), F32),
                   jax.ShapeDtypeStruct((N_CHIPS, HALF_ROWS, FLAT_COLS), BF16)],
        compiler_params=_params(("parallel", "arbitrary")),
    )(core, chip, g, theirs)


def _chip_sum(own, received):
    nb = HALF_ROWS // UPD_ROWS

    def body(p_ref, r_ref, o_ref):
        o_ref[...] = ((p_ref[...] + r_ref[0].astype(F32)) + r_ref[1].astype(F32)) + r_ref[2].astype(F32)

    return pl.pallas_call(
        body, name="chip_sum", grid=(nb,),
        in_specs=[pl.BlockSpec((UPD_ROWS, FLAT_COLS), lambda i: (i, 0)),
                  pl.BlockSpec((3, UPD_ROWS, FLAT_COLS), lambda i: (0, i, 0))],
        out_specs=pl.BlockSpec((UPD_ROWS, FLAT_COLS), lambda i: (i, 0)),
        out_shape=jax.ShapeDtypeStruct((HALF_ROWS, FLAT_COLS), F32),
        compiler_params=_params(("parallel",)),
    )(own, received)


def _adamw(w, g, m, v, *, rows, name):
    r, c = w.shape
    rows = min(rows, r)
    assert r % rows == 0

    def body(w_ref, g_ref, m_ref, v_ref, d_ref, nm_ref, nv_ref):
        gv = g_ref[...]
        nm = ADAM_B1 * m_ref[...] + (1.0 - ADAM_B1) * gv
        nv = ADAM_B2 * v_ref[...] + (1.0 - ADAM_B2) * (gv * gv)
        m_hat = nm / (1.0 - ADAM_B1 ** ADAM_STEP)
        v_hat = nv / (1.0 - ADAM_B2 ** ADAM_STEP)
        d_ref[...] = -ADAM_LR * (m_hat / (jnp.sqrt(v_hat) + ADAM_EPS) + ADAM_WD * w_ref[...])
        nm_ref[...] = nm
        nv_ref[...] = nv

    spec = pl.BlockSpec((rows, c), lambda i: (i, 0))
    return pl.pallas_call(
        body, name=name, grid=(r // rows,),
        in_specs=[spec] * 4, out_specs=[spec] * 3,
        out_shape=[jax.ShapeDtypeStruct((r, c), F32)] * 3,
        compiler_params=_params(("parallel",)),
    )(w, g, m, v)


W_NAMES = ("a_w_in", "a_w_out", "b_w_in", "b_q_norm", "b_w_uq", "b_w_out", "kv_w_down", "kv_norm", "kv_w_up", "ln_g", "ln_b")
BIG = tuple(name for name, _ in SHARD_ROWS)


def _pack_small(ln_g, ln_b, q_norm, kv_norm, extra=None):
    pad = lambda a: jnp.pad(a.reshape(1, -1), ((0, 0), (0, FLAT_COLS - a.size)))
    rows = [ln_g, ln_b, pad(q_norm), pad(kv_norm),
            jnp.zeros((1, FLAT_COLS), F32) if extra is None else pad(extra), jnp.zeros((1, FLAT_COLS), F32)]
    return jnp.concatenate(rows, axis=0)


def _unpack_small(p):
    return dict(ln_g=p[0:2], ln_b=p[2:4], b_q_norm=p[4:5, :Q_LORA], kv_norm=p[5, :KV_LORA])


def kernel(x, a_w_in, a_w_out, b_w_in, b_q_norm, b_w_uq, b_w_out, kv_w_down, kv_norm, kv_w_up, ln_g, ln_b, loss_target, m_a_w_in, m_a_w_out, m_b_w_in, m_b_q_norm, m_b_w_uq, m_b_w_out, m_kv_w_down, m_kv_norm, m_kv_w_up, m_ln_g, m_ln_b, v_a_w_in, v_a_w_out, v_b_w_in, v_b_q_norm, v_b_w_uq, v_b_w_out, v_kv_w_down, v_kv_norm, v_kv_w_up, v_ln_g, v_ln_b):
    w_in = dict(a_w_in=a_w_in[0], a_w_out=a_w_out[0], b_w_in=b_w_in[0], b_w_uq=b_w_uq[0], b_w_out=b_w_out[0],
                kv_w_down=kv_w_down, kv_w_up=kv_w_up)
    m_in = dict(a_w_in=m_a_w_in[0], a_w_out=m_a_w_out[0], b_w_in=m_b_w_in[0], b_w_uq=m_b_w_uq[0], b_w_out=m_b_w_out[0],
                kv_w_down=m_kv_w_down, kv_w_up=m_kv_w_up)
    v_in = dict(a_w_in=v_a_w_in[0], a_w_out=v_a_w_out[0], b_w_in=v_b_w_in[0], b_w_uq=v_b_w_uq[0], b_w_out=v_b_w_out[0],
                kv_w_down=v_kv_w_down, kv_w_up=v_kv_w_up)
    shard_shapes = {name: w_in[name].shape for name in BIG}

    cx, cy, cc = lax.axis_index("x"), lax.axis_index("y"), lax.axis_index("c")
    flat16 = _flat_shards(w_in, BF16)
    gathered = lax.dynamic_update_slice(_gather_weights(flat16), flat16[None], (2 * cx + cy, 0, 0))
    weights = _kernel_layout(_full_from_gathered(gathered))

    loss, grad_x, grads, small = _local_step(x[0], loss_target[0], weights, kv_norm.reshape(1, -1), b_q_norm, ln_g, ln_b)

    g_all = _gathered_from_full(_reference_layout_grads(grads))
    theirs = _pair_exchange_halves(g_all.astype(BF16))
    own, pair16 = _pair_sum(g_all, theirs, cc.astype(jnp.int32).reshape(1), (2 * cx + cy).astype(jnp.int32).reshape(1))
    mine = _chip_sum(own, _chip_exchange(pair16))
    sibling = _pair_share(mine)
    g_flat = jnp.concatenate([jnp.where(cc == 0, mine, sibling), jnp.where(cc == 0, sibling, mine)], axis=0)

    small_sum = _sum_small(_pack_small(small["ln_g"], small["ln_b"], small["q_norm"], small["kv_norm"], loss[:, :1]))
    loss_out = small_sum[6, 0]
    g_small = _unpack_small(small_sum)

    d_flat, nm_flat, nv_flat = _adamw(_flat_shards(w_in, F32), g_flat, _flat_shards(m_in, F32), _flat_shards(v_in, F32),
                                      rows=512, name="adamw_shards")
    ds, nms, nvs = _adamw(_pack_small(ln_g, ln_b, b_q_norm, kv_norm), small_sum.at[6].set(0.0),
                          _pack_small(m_ln_g, m_ln_b, m_b_q_norm, m_kv_norm),
                          _pack_small(v_ln_g, v_ln_b, v_b_q_norm, v_kv_norm), rows=8, name="adamw_small")

    def assemble(flat, small_packed):
        big = _unflat_shards(flat, shard_shapes)
        sm = _unpack_small(small_packed)
        out = {}
        for name in W_NAMES:
            if name in big:
                out[name] = big[name][None] if name in ("a_w_in", "a_w_out", "b_w_in", "b_w_uq", "b_w_out") else big[name]
            else:
                out[name] = sm[name]
        return [out[name] for name in W_NAMES]

    grad_list = assemble(g_flat, small_sum)
    return (loss_out, grad_x[None], *grad_list, *assemble(d_flat, ds), *assemble(nm_flat, nms), *assemble(nv_flat, nvs))
```

```python
import functools
import math

import jax
import jax.numpy as jnp
from jax import lax
from jax.experimental import pallas as pl
from jax.experimental.pallas import tpu as pltpu

F32 = jnp.float32
BF16 = jnp.bfloat16
MESH = pl.DeviceIdType.MESH

D_MODEL = 1024
DEPTH = 2
H_A, DK_A, DV_A = 4, 256, 512
WIDTH_A = H_A * DV_A
CHUNK = 128
H_B, QK_NOPE, QK_ROPE, V_HEAD = 16, 128, 64, 128
QK_PAD = 256
Q_LORA, KV_LORA = 768, 512
KV_DOWN_PAD = 640
WIDTH_B = H_B * V_HEAD
IN_A = 2 * H_A * DK_A + 2 * WIDTH_A
IN_B = Q_LORA + WIDTH_B
H1_B = KV_DOWN_PAD + IN_B
ROPE_BASE = 10000.0
ALPHA = (2.0 * DEPTH) ** 0.25
ATT_SCALE = (QK_NOPE + QK_ROPE) ** -0.5
NEG_BIG = -1e30

ADAM_LR, ADAM_B1, ADAM_B2, ADAM_EPS, ADAM_WD, ADAM_STEP = 0.001, 0.9, 0.999, 1e-08, 0.01, 10

VMEM_LIMIT_BYTES = 56 * 1024 * 1024
LANES = 128
FLAT_COLS = 1024
SHARD_ROWS = (("a_w_in", 1536), ("a_w_out", 512), ("b_w_in", 704), ("b_w_uq", 576), ("b_w_out", 512),
              ("kv_w_down", 144), ("kv_w_up", 512))
FLAT_ROWS = 4608
HALF_ROWS = FLAT_ROWS // 2
N_CHIPS = 4
N_DEV = 8


def _params(sem, vmem=VMEM_LIMIT_BYTES):
    return pltpu.CompilerParams(dimension_semantics=sem, vmem_limit_bytes=vmem)


def _row_spec(ts, w, col_block=0):
    return pl.BlockSpec((ts, w), lambda i: (i, col_block))


def _bc_spec(shape):
    nd = len(shape)
    return pl.BlockSpec(shape, lambda i: (0,) * nd)


def _sigmoid(x):
    return 1.0 / (1.0 + jnp.exp(-x))


def _fold8(v):
    ts, w = v.shape
    return jnp.sum(v.reshape(ts // 8, 8, w), axis=0)


def _mm(a, b, *, ta=False, tb=False, out_dtype=F32, tm=1024, tn=512, tk=None, name, extras=(), epilogue=None, out_tn=None):
    if ta:
        K, M = a.shape
    else:
        M, K = a.shape
    if tb:
        N, Kb = b.shape
    else:
        Kb, N = b.shape
    assert K == Kb, (a.shape, b.shape)
    tm, tn = min(tm, M), min(tn, N)
    tk = K if tk is None else min(tk, K)
    assert M % tm == 0 and N % tn == 0 and K % tk == 0, (name, M, N, K, tm, tn, tk)
    nk = K // tk
    out_tn = tn if out_tn is None else out_tn
    n_extra = len(extras)
    dims = (((0,) if ta else (1,), (1,) if tb else (0,)), ((), ()))

    def body(a_ref, b_ref, *rest):
        extra_refs, o_ref, scratch = rest[:n_extra], rest[n_extra], rest[n_extra + 1:]
        prod = lax.dot_general(a_ref[...].astype(BF16), b_ref[...].astype(BF16), dims,
                               preferred_element_type=F32)

        def store(tile):
            if epilogue is None:
                o_ref[...] = tile.astype(o_ref.dtype)
            else:
                epilogue(tile, o_ref, *extra_refs)

        if nk == 1:
            store(prod)
        else:
            acc, = scratch
            k = pl.program_id(2)

            @pl.when(k == 0)
            def _():
                acc[...] = prod

            @pl.when(k > 0)
            def _():
                acc[...] += prod

            @pl.when(k == nk - 1)
            def _():
                store(acc[...])

    a_spec = pl.BlockSpec((tk, tm), lambda i, j, k: (k, i)) if ta else pl.BlockSpec((tm, tk), lambda i, j, k: (i, k))
    b_spec = pl.BlockSpec((tn, tk), lambda i, j, k: (j, k)) if tb else pl.BlockSpec((tk, tn), lambda i, j, k: (k, j))
    extra_specs = [pl.BlockSpec((tm, e.shape[1]), lambda i, j, k: (i, 0)) for e in extras]
    return pl.pallas_call(
        body, name=name,
        grid=(M // tm, N // tn, nk),
        in_specs=[a_spec, b_spec] + extra_specs,
        out_specs=pl.BlockSpec((tm, out_tn), lambda i, j, k: (i, j)),
        out_shape=jax.ShapeDtypeStruct((M, (N // tn) * out_tn), out_dtype),
        scratch_shapes=[] if nk == 1 else [pltpu.VMEM((tm, tn), F32)],
        compiler_params=_params(("parallel", "parallel", "arbitrary")),
    )(a, b, *extras)


def _rope_tables_a(s):
    half = DK_A // 2
    inv = ROPE_BASE ** (-jnp.arange(half, dtype=F32) / half)
    ang = jnp.arange(s, dtype=F32)[:, None] * inv[None, :]
    return jnp.cos(ang), jnp.sin(ang)


def _rope_tables_b(s):
    half = QK_ROPE // 2
    inv = ROPE_BASE ** (-jnp.arange(half, dtype=F32) / half)
    ang = jnp.arange(s, dtype=F32)[:, None] * inv[None, :]
    c, sn = jnp.cos(ang), jnp.sin(ang)
    z = jnp.zeros_like(c)
    cos = jnp.concatenate([c, c, z, z], axis=1)
    sa = jnp.concatenate([-sn, z, z, z], axis=1)
    sb = jnp.concatenate([z, sn, z, z], axis=1)
    return cos, sa, sb


def _rope_b(r, cos, sa, sb, sign):
    return r * cos + sign * (pltpu.roll(r, 96, 1) * sa + pltpu.roll(r, 32, 1) * sb)


def _retention_tables():
    lg = jnp.log1p(-jnp.exp2(-5.0 - jnp.arange(H_A, dtype=F32)))
    idx = jnp.arange(CHUNK, dtype=F32)
    diff = idx[:, None] - idx[None, :]
    causal = diff >= 0
    dmat = jnp.where(causal, jnp.exp(jnp.where(causal, diff, 0.0)[None] * lg[:, None, None]), 0.0)
    qdec = jnp.exp((idx + 1.0)[None, :] * lg[:, None])[:, :, None]
    kdec = jnp.exp((CHUNK - 1.0 - idx)[None, :] * lg[:, None])[:, :, None]
    cdec = jnp.broadcast_to(jnp.exp(CHUNK * lg)[:, None, None], (H_A, 1, DV_A))
    return dmat, qdec, kdec, cdec


def _group_norm(o):
    mu = jnp.mean(o, axis=-1, keepdims=True)
    oc = o - mu
    var = jnp.mean(oc * oc, axis=-1, keepdims=True)
    rstd = lax.rsqrt(var + 1e-5)
    return oc * rstd, rstd


Q_COL, K_COL, V_COL, GATE_COL = 0, H_A * DK_A, 2 * H_A * DK_A, 2 * H_A * DK_A + WIDTH_A


def _ln_stats(z):
    mu = jnp.mean(z, axis=-1, keepdims=True)
    zc = z - mu
    var = jnp.mean(zc * zc, axis=-1, keepdims=True)
    rstd = lax.rsqrt(var + 1e-5)
    return zc * rstd, rstd


def _ln_bwd(dy, xhat, rstd, g):
    dxh = dy * g
    m1 = jnp.mean(dxh, axis=-1, keepdims=True)
    m2 = jnp.mean(dxh * xhat, axis=-1, keepdims=True)
    return rstd * (dxh - m1 - xhat * m2)


def _ln_fwd(x, y, g, b, *, ts):
    s = x.shape[0]

    def body(x_ref, y_ref, g_ref, b_ref, o_ref, o16_ref):
        xhat, _ = _ln_stats(ALPHA * x_ref[...] + y_ref[...])
        out = xhat * g_ref[...] + b_ref[...]
        o_ref[...] = out
        o16_ref[...] = out.astype(o16_ref.dtype)

    return pl.pallas_call(
        body, name="ln_fwd", grid=(s // ts,),
        in_specs=[_row_spec(ts, D_MODEL), _row_spec(ts, D_MODEL), _bc_spec((1, D_MODEL)), _bc_spec((1, D_MODEL))],
        out_specs=[_row_spec(ts, D_MODEL), _row_spec(ts, D_MODEL)],
        out_shape=[jax.ShapeDtypeStruct((s, D_MODEL), F32), jax.ShapeDtypeStruct((s, D_MODEL), BF16)],
        compiler_params=_params(("parallel",)),
    )(x, y, g, b)


def _ln_loss_bwd(x1, y, target, g, b, *, ts):
    s = x1.shape[0]
    n = s // ts

    def body(x_ref, y_ref, t_ref, g_ref, b_ref, dz_ref, dz16_ref, dg_ref, db_ref, loss_ref, ag, ab, al):
        i = pl.program_id(0)

        @pl.when(i == 0)
        def _():
            ag[...] = jnp.zeros_like(ag)
            ab[...] = jnp.zeros_like(ab)
            al[...] = jnp.zeros_like(al)

        xhat, rstd = _ln_stats(ALPHA * x_ref[...] + y_ref[...])
        err = xhat * g_ref[...] + b_ref[...] - t_ref[...]
        al[...] += _fold8(err * err)
        dy = err * (1.0 / D_MODEL)
        ag[...] += _fold8(dy * xhat)
        ab[...] += _fold8(dy)
        dz = _ln_bwd(dy, xhat, rstd, g_ref[...])
        dz_ref[...] = dz
        dz16_ref[...] = dz.astype(dz16_ref.dtype)

        @pl.when(i == n - 1)
        def _():
            dg_ref[...] = jnp.sum(ag[...], axis=0, keepdims=True)
            db_ref[...] = jnp.sum(ab[...], axis=0, keepdims=True)
            loss_ref[...] = jnp.full((1, LANES), (0.5 / D_MODEL) * jnp.sum(al[...]), F32)

    return pl.pallas_call(
        body, name="ln_loss_bwd", grid=(n,),
        in_specs=[_row_spec(ts, D_MODEL)] * 3 + [_bc_spec((1, D_MODEL))] * 2,
        out_specs=[_row_spec(ts, D_MODEL), _row_spec(ts, D_MODEL), _bc_spec((1, D_MODEL)), _bc_spec((1, D_MODEL)),
                   _bc_spec((1, LANES))],
        out_shape=[jax.ShapeDtypeStruct((s, D_MODEL), F32), jax.ShapeDtypeStruct((s, D_MODEL), BF16),
                   jax.ShapeDtypeStruct((1, D_MODEL), F32), jax.ShapeDtypeStruct((1, D_MODEL), F32),
                   jax.ShapeDtypeStruct((1, LANES), F32)],
        scratch_shapes=[pltpu.VMEM((8, D_MODEL), F32)] * 3,
        compiler_params=_params(("arbitrary",)),
    )(x1, y, target, g, b)


def _ln_bwd_call(dy, x, y, g, *, ts):
    s = x.shape[0]
    n = s // ts

    def body(dy_ref, x_ref, y_ref, g_ref, dz_ref, dz16_ref, dg_ref, db_ref, ag, ab):
        i = pl.program_id(0)

        @pl.when(i == 0)
        def _():
            ag[...] = jnp.zeros_like(ag)
            ab[...] = jnp.zeros_like(ab)

        xhat, rstd = _ln_stats(ALPHA * x_ref[...] + y_ref[...])
        dy = dy_ref[...]
        ag[...] += _fold8(dy * xhat)
        ab[...] += _fold8(dy)
        dz = _ln_bwd(dy, xhat, rstd, g_ref[...])
        dz_ref[...] = dz
        dz16_ref[...] = dz.astype(dz16_ref.dtype)

        @pl.when(i == n - 1)
        def _():
            dg_ref[...] = jnp.sum(ag[...], axis=0, keepdims=True)
            db_ref[...] = jnp.sum(ab[...], axis=0, keepdims=True)

    return pl.pallas_call(
        body, name="ln_bwd", grid=(n,),
        in_specs=[_row_spec(ts, D_MODEL)] * 3 + [_bc_spec((1, D_MODEL))],
        out_specs=[_row_spec(ts, D_MODEL), _row_spec(ts, D_MODEL), _bc_spec((1, D_MODEL)), _bc_spec((1, D_MODEL))],
        out_shape=[jax.ShapeDtypeStruct((s, D_MODEL), F32), jax.ShapeDtypeStruct((s, D_MODEL), BF16),
                   jax.ShapeDtypeStruct((1, D_MODEL), F32), jax.ShapeDtypeStruct((1, D_MODEL), F32)],
        scratch_shapes=[pltpu.VMEM((8, D_MODEL), F32)] * 2,
        compiler_params=_params(("arbitrary",)),
    )(dy, x, y, g)


def _residual_store(tile, o_ref, dz_ref):
    o_ref[...] = ALPHA * dz_ref[...] + tile


C_LAT = slice(0, KV_LORA)
C_ROPE = slice(KV_LORA, KV_DOWN_PAD)
C_QL = slice(KV_DOWN_PAD, KV_DOWN_PAD + Q_LORA)
C_GATE = slice(KV_DOWN_PAD + Q_LORA, H1_B)


def _rms(x, eps=1e-6):
    r = lax.rsqrt(jnp.mean(x * x, axis=-1, keepdims=True) + eps)
    return x * r, r


def _rms_bwd(dy, xhat, r, g):
    dxh = dy * g
    return r * (dxh - xhat * jnp.mean(dxh * xhat, axis=-1, keepdims=True))


def _kvq_prep(h1, kv_norm, q_norm, cos, sa, sb, *, ts):
    s = h1.shape[0]

    def body(h_ref, kn_ref, qn_ref, c_ref, sa_ref, sb_ref, lat_ref, kr_ref, ql_ref):
        lat, _ = _rms(h_ref[:, C_LAT])
        lat_ref[...] = (lat * kn_ref[...]).astype(lat_ref.dtype)
        kr_ref[...] = _rope_b(h_ref[:, C_ROPE], c_ref[...], sa_ref[...], sb_ref[...], 1.0).astype(kr_ref.dtype)
        ql, _ = _rms(h_ref[:, C_QL])
        ql_ref[...] = (ql * qn_ref[...]).astype(ql_ref.dtype)

    return pl.pallas_call(
        body, name="kvq_prep", grid=(s // ts,),
        in_specs=[_row_spec(ts, H1_B), _bc_spec((1, KV_LORA)), _bc_spec((1, Q_LORA))] + [_row_spec(ts, 128)] * 3,
        out_specs=[_row_spec(ts, KV_LORA), _row_spec(ts, 128), _row_spec(ts, Q_LORA)],
        out_shape=[jax.ShapeDtypeStruct((s, KV_LORA), BF16), jax.ShapeDtypeStruct((s, 128), BF16),
                   jax.ShapeDtypeStruct((s, Q_LORA), BF16)],
        compiler_params=_params(("parallel",)),
    )(h1, kv_norm, q_norm, cos, sa, sb)


LOG2E = 1.4426950408889634
LN2 = 0.6931471805599453
Q_SCALE = ATT_SCALE * LOG2E


def _rope_q_store(tile, o_ref, c_ref, sa_ref, sb_ref):
    c, a, b = c_ref[...], sa_ref[...], sb_ref[...]
    for hd in range(tile.shape[1] // QK_PAD):
        lo = hd * QK_PAD
        o_ref[:, lo:lo + 128] = (tile[:, lo:lo + 128] * Q_SCALE).astype(o_ref.dtype)
        o_ref[:, lo + 128:lo + 256] = (_rope_b(tile[:, lo + 128:lo + 256], c, a, b, 1.0) * Q_SCALE).astype(o_ref.dtype)


def _assemble_k_store(tile, o_ref, kr_ref):
    r = kr_ref[...]
    for hd in range(tile.shape[1] // QK_NOPE):
        o_ref[:, hd * QK_PAD:hd * QK_PAD + 128] = tile[:, hd * 128:(hd + 1) * 128].astype(o_ref.dtype)
        o_ref[:, hd * QK_PAD + 128:(hd + 1) * QK_PAD] = r


def _h1_bwd(h1, dlat_k, dlat_v, dkr_heads, dqn, dg16, kv_norm, q_norm, cos, sa, sb, *, ts):
    s = h1.shape[0]
    n = s // ts

    def body(h_ref, dk_ref, dv_ref, dkr_ref, dqn_ref, dg_ref, kn_ref, qn_ref, c_ref, sa_ref, sb_ref,
             o_ref, dkn_ref, dqn_out_ref, akn, aqn):
        i = pl.program_id(0)

        @pl.when(i == 0)
        def _():
            akn[...] = jnp.zeros_like(akn)
            aqn[...] = jnp.zeros_like(aqn)

        lat, r = _rms(h_ref[:, C_LAT])
        dlat = dk_ref[...] + dv_ref[...]
        akn[...] += _fold8(dlat * lat)
        o_ref[:, C_LAT] = _rms_bwd(dlat, lat, r, kn_ref[...]).astype(o_ref.dtype)

        dkr = dkr_ref[:, 0:128]
        for hd in range(1, H_B):
            dkr = dkr + dkr_ref[:, hd * 128:(hd + 1) * 128]
        o_ref[:, C_ROPE] = _rope_b(dkr, c_ref[...], sa_ref[...], sb_ref[...], -1.0).astype(o_ref.dtype)

        ql, rq = _rms(h_ref[:, C_QL])
        dq = dqn_ref[...]
        aqn[...] += _fold8(dq * ql)
        o_ref[:, C_QL] = _rms_bwd(dq, ql, rq, qn_ref[...]).astype(o_ref.dtype)
        o_ref[:, C_GATE] = dg_ref[...]

        @pl.when(i == n - 1)
        def _():
            dkn_ref[...] = jnp.sum(akn[...], axis=0, keepdims=True)
            dqn_out_ref[...] = jnp.sum(aqn[...], axis=0, keepdims=True)

    return pl.pallas_call(
        body, name="h1_bwd", grid=(n,),
        in_specs=[_row_spec(ts, H1_B), _row_spec(ts, KV_LORA), _row_spec(ts, KV_LORA), _row_spec(ts, H_B * 128),
                  _row_spec(ts, Q_LORA), _row_spec(ts, WIDTH_B), _bc_spec((1, KV_LORA)), _bc_spec((1, Q_LORA))]
        + [_row_spec(ts, 128)] * 3,
        out_specs=[_row_spec(ts, H1_B), _bc_spec((1, KV_LORA)), _bc_spec((1, Q_LORA))],
        out_shape=[jax.ShapeDtypeStruct((s, H1_B), BF16), jax.ShapeDtypeStruct((1, KV_LORA), F32),
                   jax.ShapeDtypeStruct((1, Q_LORA), F32)],
        scratch_shapes=[pltpu.VMEM((8, KV_LORA), F32), pltpu.VMEM((8, Q_LORA), F32)],
        compiler_params=_params(("arbitrary",)),
    )(h1, dlat_k, dlat_v, dkr_heads, dqn, dg16, kv_norm, q_norm, cos, sa, sb)


def _dot(a, b, ca, cb):
    return lax.dot_general(a, b, (((ca,), (cb,)), ((), ())), preferred_element_type=F32)


def _table_specs():
    full = lambda shape: pl.BlockSpec(shape, lambda i: (0,) * len(shape))
    return [full((H_A, CHUNK, CHUNK)), full((H_A, CHUNK, 1)), full((H_A, CHUNK, 1)), full((H_A, 1, DV_A))]


def _retention_fwd(h_a, cos, sin, tables):
    s = h_a.shape[0]
    n = s // CHUNK

    def body(h_ref, c_ref, s_ref, dm_ref, qd_ref, kd_ref, cd_ref, q_ref, k_ref, v_ref, o_ref, u_ref, st_ref, state):
        @pl.when(pl.program_id(0) == 0)
        def _():
            state[...] = jnp.zeros_like(state)

        c, sn = c_ref[...], s_ref[...]
        for hd in range(H_A):
            qs, vs = slice(hd * DK_A, (hd + 1) * DK_A), slice(hd * DV_A, (hd + 1) * DV_A)
            for r_ref, base, scale in ((q_ref, Q_COL, 1.0), (k_ref, K_COL, DK_A ** -0.5)):
                lo = base + hd * DK_A
                x1, x2 = h_ref[:, lo:lo + 128], h_ref[:, lo + 128:lo + 256]
                r_ref[:, hd * DK_A:hd * DK_A + 128] = ((x1 * c - x2 * sn) * scale).astype(r_ref.dtype)
                r_ref[:, hd * DK_A + 128:(hd + 1) * DK_A] = ((x2 * c + x1 * sn) * scale).astype(r_ref.dtype)
            v_ref[:, vs] = h_ref[:, V_COL + hd * DV_A:V_COL + (hd + 1) * DV_A].astype(v_ref.dtype)
            qv, kv, vv = q_ref[:, qs], k_ref[:, qs], v_ref[:, vs]
            st = state[hd]
            st16 = st.astype(BF16)
            st_ref[0, hd] = st16
            scores = _dot(qv, kv, 1, 1) * dm_ref[hd]
            qd = (qv.astype(F32) * qd_ref[hd]).astype(BF16)
            o = _dot(scores.astype(BF16), vv, 1, 0) + _dot(qd, st16, 1, 0)
            o_ref[:, vs] = o
            kd = (kv.astype(F32) * kd_ref[hd]).astype(BF16)
            state[hd] = st * cd_ref[hd] + _dot(kd, vv, 0, 0)
            on, _ = _group_norm(o)
            g = h_ref[:, GATE_COL + hd * DV_A:GATE_COL + (hd + 1) * DV_A]
            u_ref[:, vs] = (on * (g * _sigmoid(g))).astype(u_ref.dtype)

    row = lambda w: pl.BlockSpec((CHUNK, w), lambda i: (i, 0))
    return pl.pallas_call(
        body, name="retention_fwd", grid=(n,),
        in_specs=[row(IN_A), row(128), row(128)] + _table_specs(),
        out_specs=[row(H_A * DK_A), row(H_A * DK_A), row(WIDTH_A), row(WIDTH_A), row(WIDTH_A),
                   pl.BlockSpec((1, H_A, DK_A, DV_A), lambda i: (i, 0, 0, 0))],
        out_shape=[jax.ShapeDtypeStruct((s, H_A * DK_A), BF16), jax.ShapeDtypeStruct((s, H_A * DK_A), BF16),
                   jax.ShapeDtypeStruct((s, WIDTH_A), BF16), jax.ShapeDtypeStruct((s, WIDTH_A), F32),
                   jax.ShapeDtypeStruct((s, WIDTH_A), BF16), jax.ShapeDtypeStruct((n, H_A, DK_A, DV_A), BF16)],
        scratch_shapes=[pltpu.VMEM((H_A, DK_A, DV_A), F32)],
        compiler_params=_params(("arbitrary",)),
    )(h_a, cos, sin, *tables)


def _retention_bwd(q, k, v, states, du, o, h_a, cos, sin, tables):
    s = q.shape[0]
    n = s // CHUNK

    def body(q_ref, k_ref, v_ref, st_ref, du_ref, o_ref, g_ref, c_ref, s_ref, dm_ref, qd_ref, kd_ref, cd_ref, dh_ref, grad_state):
        @pl.when(pl.program_id(0) == 0)
        def _():
            grad_state[...] = jnp.zeros_like(grad_state)

        c, sn = c_ref[...], s_ref[...]
        for hd in range(H_A):
            qs, vs = slice(hd * DK_A, (hd + 1) * DK_A), slice(hd * DV_A, (hd + 1) * DV_A)
            on, rstd = _group_norm(o_ref[:, vs])
            g = g_ref[:, vs]
            sg = _sigmoid(g)
            du_v = du_ref[:, vs]
            don = du_v * (g * sg)
            dh_ref[:, GATE_COL + hd * DV_A:GATE_COL + (hd + 1) * DV_A] = (du_v * on * (sg * (1.0 + g * (1.0 - sg)))).astype(dh_ref.dtype)
            m1 = jnp.mean(don, axis=-1, keepdims=True)
            m2 = jnp.mean(don * on, axis=-1, keepdims=True)
            dov = (rstd * (don - m1 - on * m2)).astype(BF16)

            qv, kv, vv = q_ref[:, qs], k_ref[:, qs], v_ref[:, vs]
            dm = dm_ref[hd]
            gs = grad_state[hd]
            g16 = gs.astype(BF16)
            scores = (_dot(qv, kv, 1, 1) * dm).astype(BF16)
            dscores = (_dot(dov, vv, 1, 1) * dm).astype(BF16)
            qd = (qv.astype(F32) * qd_ref[hd]).astype(BF16)
            kd = (kv.astype(F32) * kd_ref[hd]).astype(BF16)
            dq = _dot(dscores, kv, 1, 0) + _dot(dov, st_ref[0, hd], 1, 1) * qd_ref[hd]
            dk = (_dot(dscores, qv, 0, 0) + _dot(vv, g16, 1, 1) * kd_ref[hd]) * (DK_A ** -0.5)
            dh_ref[:, V_COL + hd * DV_A:V_COL + (hd + 1) * DV_A] = (_dot(scores, dov, 0, 0) + _dot(kd, g16, 1, 0)).astype(dh_ref.dtype)
            grad_state[hd] = gs * cd_ref[hd] + _dot(qd, dov, 0, 0)
            for d, base in ((dq, Q_COL), (dk, K_COL)):
                lo = base + hd * DK_A
                d1, d2 = d[:, 0:128], d[:, 128:256]
                dh_ref[:, lo:lo + 128] = (d1 * c + d2 * sn).astype(dh_ref.dtype)
                dh_ref[:, lo + 128:lo + 256] = (d2 * c - d1 * sn).astype(dh_ref.dtype)

    rev = lambda i: n - 1 - i
    row = lambda w, col=0: pl.BlockSpec((CHUNK, w), lambda i: (rev(i), col))
    return pl.pallas_call(
        body, name="retention_bwd", grid=(n,),
        in_specs=[row(H_A * DK_A), row(H_A * DK_A), row(WIDTH_A), pl.BlockSpec((1, H_A, DK_A, DV_A), lambda i: (rev(i), 0, 0, 0)),
                  row(WIDTH_A), row(WIDTH_A), row(WIDTH_A, GATE_COL // WIDTH_A), row(128), row(128)] + _table_specs(),
        out_specs=row(IN_A),
        out_shape=jax.ShapeDtypeStruct((s, IN_A), BF16),
        scratch_shapes=[pltpu.VMEM((H_A, DK_A, DV_A), F32)],
        compiler_params=_params(("arbitrary",)),
    )(q, k, v, states, du, o, h_a, cos, sin, *tables)


GATE_BLOCK0 = (KV_DOWN_PAD + Q_LORA) // LANES


def _causal_mask(sc, row0):
    row = lax.broadcasted_iota(jnp.int32, sc.shape, 0) + row0
    col = lax.broadcasted_iota(jnp.int32, sc.shape, 1)
    return jnp.where(col <= row, sc, NEG_BIG)


def _key_block_loop(step, i, per_trip=2):
    def trip_body(jj, carry):
        for t in range(per_trip):
            step(per_trip * jj + t, False)
        return carry

    lax.fori_loop(0, i // per_trip, trip_body, 0)
    group = per_trip // 2
    while group >= 1:
        def tail(group=group):
            first = (i // (2 * group)) * (2 * group)
            for t in range(group):
                step(first + t, False)

        pl.when((i // group) % 2 == 1)(tail)
        group //= 2
    step(i, True)


def _attention_fwd(q, k, v, h1, *, blk, sub):
    s = q.shape[0]
    nb = s // blk
    reps = blk // LANES

    def body(q_ref, k_ref, v_ref, g_ref, o_ref, u_ref, lse_ref, vext_s, m_s, acc_s):
        i = pl.program_id(1)

        @pl.when(i == 0)
        def _():
            vext_s[:, 0:V_HEAD] = v_ref[...]
            vext_s[:, V_HEAD:2 * V_HEAD] = jnp.ones((s, V_HEAD), vext_s.dtype)

        m_s[...] = jnp.full_like(m_s, NEG_BIG)
        acc_s[...] = jnp.zeros_like(acc_s)

        def step(j, diagonal):
            kv_rows = pl.ds(pl.multiple_of(j * blk, blk), blk)
            kb = k_ref[kv_rows, :]
            vb = vext_s[kv_rows, :]
            for r in range(blk // sub):
                rows = slice(r * sub, (r + 1) * sub)
                sc = _dot(q_ref[rows, :], kb, 1, 1)
                if diagonal:
                    sc = _causal_mask(sc, r * sub)
                m_prev = m_s[rows, :]
                m_new = jnp.maximum(m_prev, jnp.max(sc, axis=-1, keepdims=True))
                p = jnp.exp2(sc - jnp.tile(m_new, (1, reps)))
                a = jnp.exp2(m_prev - m_new)
                acc_s[rows, :] = jnp.tile(a, (1, 2)) * acc_s[rows, :] + _dot(p.astype(BF16), vb, 1, 0)
                m_s[rows, :] = m_new

        _key_block_loop(step, i, per_trip=4)
        acc = acc_s[...]
        l = acc[:, V_HEAD:2 * V_HEAD]
        o = acc[:, 0:V_HEAD] / l
        g = g_ref[...]
        o_ref[...] = o
        u_ref[...] = (o * (g * _sigmoid(g))).astype(u_ref.dtype)
        lse_ref[0] = (m_s[...] + jnp.log2(l))[:, 0:1]

    blk_idx = lambda h, i: (i, h)
    return pl.pallas_call(
        body, name="attention_fwd", grid=(H_B, nb),
        in_specs=[pl.BlockSpec((blk, QK_PAD), blk_idx), pl.BlockSpec((s, QK_PAD), lambda h, i: (0, h)),
                  pl.BlockSpec((s, V_HEAD), lambda h, i: (0, h)), pl.BlockSpec((blk, LANES), lambda h, i: (i, GATE_BLOCK0 + h))],
        out_specs=[pl.BlockSpec((blk, V_HEAD), blk_idx), pl.BlockSpec((blk, V_HEAD), blk_idx),
                   pl.BlockSpec((1, blk, 1), lambda h, i: (h, i, 0))],
        out_shape=[jax.ShapeDtypeStruct((s, WIDTH_B), F32), jax.ShapeDtypeStruct((s, WIDTH_B), BF16),
                   jax.ShapeDtypeStruct((H_B, s, 1), F32)],
        scratch_shapes=[pltpu.VMEM((s, 2 * V_HEAD), BF16), pltpu.VMEM((blk, LANES), F32), pltpu.VMEM((blk, 2 * V_HEAD), F32)],
        compiler_params=_params(("parallel", "arbitrary")),
    )(q, k, v, h1)


def _attention_bwd(q, k, v, du, o, h1, lse, cos, sa, sb, *, blk):
    s = q.shape[0]
    nb = s // blk
    reps = blk // LANES

    def body(q_ref, k_ref, v_ref, du_ref, o_ref, g_ref, lse_ref, c_ref, sa_ref, sb_ref,
             dq_ref, dkn_ref, dkr_ref, dv_ref, dg_ref, lse_s, dl_s, do_s, dq_s):
        i = pl.program_id(1)
        g = g_ref[...]
        sg = _sigmoid(g)
        du_v, ov = du_ref[...], o_ref[...]
        do = du_v * (g * sg)
        dg_ref[...] = (du_v * ov * (sg * (1.0 + g * (1.0 - sg)))).astype(dg_ref.dtype)
        do_s[...] = do.astype(do_s.dtype)
        dl_s[...] = jnp.broadcast_to(jnp.sum(do * ov, axis=-1, keepdims=True), (blk, LANES))
        lse_s[...] = jnp.broadcast_to(lse_ref[0], (blk, LANES))
        dq_s[...] = jnp.zeros_like(dq_s)

        def step(j, diagonal):
            kv_rows = pl.ds(pl.multiple_of(j * blk, blk), blk)
            qv, dov, kb = q_ref[...], do_s[...], k_ref[kv_rows, :]
            sc = _dot(qv, kb, 1, 1)
            if diagonal:
                sc = _causal_mask(sc, 0)
            p = jnp.exp2(sc - jnp.tile(lse_s[...], (1, reps)))
            dp = _dot(dov, v_ref[kv_rows, :], 1, 1)
            ds = (p * (dp - jnp.tile(dl_s[...], (1, reps)))).astype(BF16)
            dv_c = _dot(p.astype(BF16), dov, 0, 0)
            dk_c = _dot(ds, qv, 0, 0)
            dq_s[...] += _dot(ds, kb, 1, 0)
            if diagonal:
                dkn_ref[kv_rows, :] = dk_c[:, 0:128]
                dkr_ref[kv_rows, :] = dk_c[:, 128:256]
                dv_ref[kv_rows, :] = dv_c
            else:
                dkn_ref[kv_rows, :] += dk_c[:, 0:128]
                dkr_ref[kv_rows, :] += dk_c[:, 128:256]
                dv_ref[kv_rows, :] += dv_c

        _key_block_loop(step, i, per_trip=4)
        dq = dq_s[...] * ATT_SCALE
        dq_ref[:, 0:128] = dq[:, 0:128].astype(dq_ref.dtype)
        dq_ref[:, 128:256] = _rope_b(dq[:, 128:256], c_ref[...], sa_ref[...], sb_ref[...], -1.0).astype(dq_ref.dtype)

        @pl.when(i == nb - 1)
        def _():
            dkn_ref[...] = dkn_ref[...] * LN2
            dkr_ref[...] = dkr_ref[...] * LN2

    head = lambda h, i: (0, h)
    blk_idx = lambda h, i: (i, h)
    row_idx = lambda h, i: (i, 0)
    return pl.pallas_call(
        body, name="attention_bwd", grid=(H_B, nb),
        in_specs=[pl.BlockSpec((blk, QK_PAD), blk_idx), pl.BlockSpec((s, QK_PAD), head), pl.BlockSpec((s, V_HEAD), head),
                  pl.BlockSpec((blk, V_HEAD), blk_idx), pl.BlockSpec((blk, V_HEAD), blk_idx),
                  pl.BlockSpec((blk, LANES), lambda h, i: (i, GATE_BLOCK0 + h)), pl.BlockSpec((1, blk, 1), lambda h, i: (h, i, 0)),
                  pl.BlockSpec((blk, LANES), row_idx), pl.BlockSpec((blk, LANES), row_idx), pl.BlockSpec((blk, LANES), row_idx)],
        out_specs=[pl.BlockSpec((blk, QK_PAD), blk_idx), pl.BlockSpec((s, 128), head), pl.BlockSpec((s, 128), head),
                   pl.BlockSpec((s, 128), head), pl.BlockSpec((blk, V_HEAD), blk_idx)],
        out_shape=[jax.ShapeDtypeStruct((s, H_B * QK_PAD), BF16), jax.ShapeDtypeStruct((s, H_B * 128), F32),
                   jax.ShapeDtypeStruct((s, H_B * 128), F32), jax.ShapeDtypeStruct((s, H_B * 128), F32),
                   jax.ShapeDtypeStruct((s, WIDTH_B), BF16)],
        scratch_shapes=[pltpu.VMEM((blk, LANES), F32), pltpu.VMEM((blk, LANES), F32), pltpu.VMEM((blk, V_HEAD), BF16),
                        pltpu.VMEM((blk, QK_PAD), F32)],
        compiler_params=_params(("parallel", "arbitrary")),
    )(q, k, v, du, o, h1, lse, cos, sa, sb)


def _local_step(x, target, w, kv_norm, q_norm, ln_g, ln_b, *, ts=256, blk=512):
    s = x.shape[0]
    cos_a, sin_a = _rope_tables_a(s)
    cos_b, sa_b, sb_b = _rope_tables_b(s)
    tables = _retention_tables()
    g0, g1, b0, b1 = ln_g[0:1], ln_g[1:2], ln_b[0:1], ln_b[1:2]

    x16 = x.astype(BF16)
    h_a = _mm(x16, w["a_in"], tn=1536, name="a_in_fwd")
    q_a, k_a, v_a, o_a, u_a, states = _retention_fwd(h_a, cos_a, sin_a, tables)
    y_a = _mm(u_a, w["a_out"], tn=1024, name="a_out_fwd")
    x1, x1_16 = _ln_fwd(x, y_a, g0, b0, ts=ts)

    h1 = _mm(x1_16, w["b_in1"], tn=1152, name="b_in_fwd")
    lat16, kr16, qn16 = _kvq_prep(h1, kv_norm, q_norm, cos_b, sa_b, sb_b, ts=ts)
    k16 = _mm(lat16, w["up_k"], out_dtype=BF16, tn=2048, out_tn=H_B * QK_PAD, extras=(kr16,), epilogue=_assemble_k_store, name="up_k_fwd")
    v16 = _mm(lat16, w["up_v"], out_dtype=BF16, tn=2048, name="up_v_fwd")
    q16 = _mm(qn16, w["uq"], out_dtype=BF16, tn=2048, extras=(cos_b, sa_b, sb_b), epilogue=_rope_q_store, name="uq_fwd")
    o_b, u_b, lse = _attention_fwd(q16, k16, v16, h1, blk=blk, sub=blk // 2)
    y_b = _mm(u_b, w["b_out"], tn=1024, name="b_out_fwd")

    dz_b, dz_b16, dg1, db1, loss = _ln_loss_bwd(x1, y_b, target, g1, b1, ts=ts)
    d_b_out = _mm(u_b, dz_b16, ta=True, tn=1024, tk=1024, name="b_out_dw")
    du_b = _mm(dz_b16, w["b_out"], tb=True, tn=1024, name="b_out_dx")
    dqf16, dkn, dkr_heads, dv, dgate16 = _attention_bwd(q16, k16, v16, du_b, o_b, h1, lse, cos_b, sa_b, sb_b, blk=blk)
    d_uq = _mm(qn16, dqf16, ta=True, tm=768, tn=2048, tk=1024, name="uq_dw")
    dqn = _mm(dqf16, w["uq"], tb=True, tn=768, tk=2048, name="uq_dx")
    d_up_k = _mm(lat16, dkn, ta=True, tn=2048, tk=1024, name="up_k_dw")
    d_up_v = _mm(lat16, dv, ta=True, tn=2048, tk=1024, name="up_v_dw")
    dlat_k = _mm(dkn, w["up_k"], tb=True, tn=512, name="up_k_dx")
    dlat_v = _mm(dv, w["up_v"], tb=True, tn=512, name="up_v_dx")
    dh1, dkvn, dqnorm = _h1_bwd(h1, dlat_k, dlat_v, dkr_heads, dqn, dgate16, kv_norm, q_norm, cos_b, sa_b, sb_b, ts=ts)
    d_b_in1 = _mm(x1_16, dh1, ta=True, tn=1152, tk=1024, name="b_in_dw")
    dx1 = _mm(dh1, w["b_in1"], tb=True, tn=1024, extras=(dz_b,), epilogue=_residual_store, name="b_in_dx")

    dz_a, dz_a16, dg0, db0 = _ln_bwd_call(dx1, x, y_a, g0, ts=ts)
    d_a_out = _mm(u_a, dz_a16, ta=True, tn=1024, tk=1024, name="a_out_dw")
    du_a = _mm(dz_a16, w["a_out"], tb=True, tn=1024, name="a_out_dx")
    dh_a = _retention_bwd(q_a, k_a, v_a, states, du_a, o_a, h_a, cos_a, sin_a, tables)
    d_a_in = _mm(x16, dh_a, ta=True, tn=1536, tk=1024, name="a_in_dw")
    grad_x = _mm(dh_a, w["a_in"], tb=True, tn=1024, tk=2048, extras=(dz_a,), epilogue=_residual_store, name="a_in_dx")

    grads = dict(a_in=d_a_in, a_out=d_a_out, b_in1=d_b_in1, uq=d_uq, b_out=d_b_out, up_k=d_up_k, up_v=d_up_v)
    small = dict(ln_g=jnp.concatenate([dg0, dg1], axis=0), ln_b=jnp.concatenate([db0, db1], axis=0),
                 q_norm=dqnorm, kv_norm=dkvn)
    return loss, grad_x, grads, small


def _flat_shards(shards, dtype):
    parts = [shards[name].reshape(rows, FLAT_COLS) for name, rows in SHARD_ROWS]
    used = sum(rows for _, rows in SHARD_ROWS)
    parts.append(jnp.zeros((FLAT_ROWS - used, FLAT_COLS), shards["a_w_in"].dtype))
    return jnp.concatenate(parts, axis=0).astype(dtype)


def _unflat_shards(flat, shapes):
    out, off = {}, 0
    for name, rows in SHARD_ROWS:
        out[name] = flat[off:off + rows].reshape(shapes[name])
        off += rows
    return out


COL_SHARDED = {"a_w_in": (D_MODEL, IN_A), "b_w_in": (D_MODEL, IN_B), "b_w_uq": (Q_LORA, H_B * (QK_NOPE + QK_ROPE)),
               "kv_w_up": (KV_LORA, H_B * (QK_NOPE + V_HEAD))}
ROW_SHARDED = {"a_w_out": (WIDTH_A, D_MODEL), "b_w_out": (WIDTH_B, D_MODEL), "kv_w_down": (D_MODEL, KV_LORA + QK_ROPE)}


def _full_from_gathered(gathered):
    out, off = {}, 0
    for name, rows in SHARD_ROWS:
        part = gathered[:, off:off + rows]
        off += rows
        if name in COL_SHARDED:
            r, c = COL_SHARDED[name]
            out[name] = part.reshape(N_CHIPS, r, c // N_CHIPS).transpose(1, 0, 2).reshape(r, c)
        else:
            r, c = ROW_SHARDED[name]
            out[name] = part.reshape(r, c)
    return out


def _gathered_from_full(full):
    parts = []
    for name, rows in SHARD_ROWS:
        g = full[name]
        if name in COL_SHARDED:
            r, c = COL_SHARDED[name]
            g = g.reshape(r, N_CHIPS, c // N_CHIPS).transpose(1, 0, 2)
        parts.append(g.reshape(N_CHIPS, rows, FLAT_COLS))
    used = sum(rows for _, rows in SHARD_ROWS)
    parts.append(jnp.zeros((N_CHIPS, FLAT_ROWS - used, FLAT_COLS), F32))
    return jnp.concatenate(parts, axis=1)


def _kernel_layout(full):
    uq = full["b_w_uq"].reshape(Q_LORA, H_B, QK_NOPE + QK_ROPE)
    uq = jnp.pad(uq, ((0, 0), (0, 0), (0, QK_PAD - QK_NOPE - QK_ROPE))).reshape(Q_LORA, H_B * QK_PAD)
    up = full["kv_w_up"].reshape(KV_LORA, H_B, QK_NOPE + V_HEAD)
    down = jnp.pad(full["kv_w_down"], ((0, 0), (0, KV_DOWN_PAD - KV_LORA - QK_ROPE)))
    return dict(a_in=full["a_w_in"], a_out=full["a_w_out"], b_out=full["b_w_out"], uq=uq,
                up_k=up[:, :, :QK_NOPE].reshape(KV_LORA, H_B * QK_NOPE),
                up_v=up[:, :, QK_NOPE:].reshape(KV_LORA, H_B * V_HEAD),
                b_in1=jnp.concatenate([down, full["b_w_in"]], axis=1))


def _reference_layout_grads(g):
    uq = g["uq"].reshape(Q_LORA, H_B, QK_PAD)[:, :, :QK_NOPE + QK_ROPE].reshape(Q_LORA, H_B * (QK_NOPE + QK_ROPE))
    up = jnp.concatenate([g["up_k"].reshape(KV_LORA, H_B, QK_NOPE), g["up_v"].reshape(KV_LORA, H_B, V_HEAD)], axis=2)
    return dict(a_w_in=g["a_in"], a_w_out=g["a_out"], b_w_out=g["b_out"], b_w_uq=uq,
                kv_w_up=up.reshape(KV_LORA, H_B * (QK_NOPE + V_HEAD)),
                kv_w_down=g["b_in1"][:, :KV_LORA + QK_ROPE], b_w_in=g["b_in1"][:, KV_DOWN_PAD:])


HBM_SPEC = pl.BlockSpec(memory_space=pl.ANY)


def _me():
    return lax.axis_index("x"), lax.axis_index("y"), lax.axis_index("c")


def _chip_flips(x, y):
    return [(1 - x, y), (x, 1 - y), (1 - x, 1 - y)]


def _gather_weights(flat16):
    def body(src_ref, out_ref, send_sems, recv_sems):
        x, y, c = _me()
        mine = 2 * x + y
        my_rows = pl.ds(pl.multiple_of(c * HALF_ROWS, 8), HALF_ROWS)
        their_rows = pl.ds(pl.multiple_of((1 - c) * HALF_ROWS, 8), HALF_ROWS)
        chips = _chip_flips(x, y)

        def copy(k, src, dst, to):
            return pltpu.make_async_remote_copy(src_ref=src, dst_ref=dst, send_sem=send_sems.at[k], recv_sem=recv_sems.at[k],
                                                device_id=to, device_id_type=MESH)

        first = [copy(k, src_ref.at[my_rows, :], out_ref.at[mine, my_rows, :], (px, py, c)) for k, (px, py) in enumerate(chips)]
        for cp in first:
            cp.start()
        passed = []
        for k, (px, py) in enumerate(chips):
            landed = out_ref.at[2 * px + py, my_rows, :]
            copy(k, landed, landed, (px, py, c)).wait_recv()
            passed.append(copy(3 + k, landed, landed, (x, y, 1 - c)))
            passed[-1].start()
        for k, (px, py) in enumerate(chips):
            theirs = out_ref.at[2 * px + py, their_rows, :]
            copy(3 + k, theirs, theirs, (x, y, 1 - c)).wait_recv()
        for cp in first + passed:
            cp.wait_send()

    return pl.pallas_call(
        body, name="gather_weights",
        in_specs=[HBM_SPEC], out_specs=HBM_SPEC,
        out_shape=jax.ShapeDtypeStruct((N_CHIPS,) + flat16.shape, flat16.dtype),
        scratch_shapes=[pltpu.SemaphoreType.DMA((6,)), pltpu.SemaphoreType.DMA((6,))],
    )(flat16)


def _pair_exchange_halves(g):
    def body(g_ref, out_ref, send_sem, recv_sem):
        x, y, c = _me()
        theirs = g_ref.at[:, pl.ds(pl.multiple_of((1 - c) * HALF_ROWS, 8), HALF_ROWS), :]
        cp = pltpu.make_async_remote_copy(src_ref=theirs, dst_ref=out_ref, send_sem=send_sem, recv_sem=recv_sem,
                                          device_id=(x, y, 1 - c), device_id_type=MESH)
        cp.start()
        cp.wait_send()
        cp.wait_recv()

    return pl.pallas_call(
        body, name="pair_exchange_halves",
        in_specs=[HBM_SPEC], out_specs=HBM_SPEC,
        out_shape=jax.ShapeDtypeStruct((N_CHIPS, HALF_ROWS, FLAT_COLS), g.dtype),
        scratch_shapes=[pltpu.SemaphoreType.DMA, pltpu.SemaphoreType.DMA],
    )(g)


def _chip_exchange(p):
    def body(p_ref, out_ref, send_sems, recv_sems):
        x, y, c = _me()
        copies = []
        for k, (px, py) in enumerate(_chip_flips(x, y)):
            copies.append(pltpu.make_async_remote_copy(
                src_ref=p_ref.at[2 * px + py], dst_ref=out_ref.at[k], send_sem=send_sems.at[k], recv_sem=recv_sems.at[k],
                device_id=(px, py, c), device_id_type=MESH))
        for cp in copies:
            cp.start()
        for cp in copies:
            cp.wait_send()
        for cp in copies:
            cp.wait_recv()

    return pl.pallas_call(
        body, name="chip_exchange",
        in_specs=[HBM_SPEC], out_specs=HBM_SPEC,
        out_shape=jax.ShapeDtypeStruct((3, HALF_ROWS, FLAT_COLS), p.dtype),
        scratch_shapes=[pltpu.SemaphoreType.DMA((3,)), pltpu.SemaphoreType.DMA((3,))],
    )(p)


def _pair_share(r):
    def body(r_ref, out_ref, send_sem, recv_sem):
        x, y, c = _me()
        cp = pltpu.make_async_remote_copy(src_ref=r_ref, dst_ref=out_ref, send_sem=send_sem, recv_sem=recv_sem,
                                          device_id=(x, y, 1 - c), device_id_type=MESH)
        cp.start()
        cp.wait_send()
        cp.wait_recv()

    return pl.pallas_call(
        body, name="pair_share",
        in_specs=[HBM_SPEC], out_specs=HBM_SPEC,
        out_shape=jax.ShapeDtypeStruct(r.shape, r.dtype),
        scratch_shapes=[pltpu.SemaphoreType.DMA, pltpu.SemaphoreType.DMA],
    )(r)


def _sum_small(vec):
    def body(v_ref, out_ref, slots, send_sems, recv_sems):
        x, y, c = _me()
        me = 4 * x + 2 * y + c
        slots[me] = v_ref[...]
        flips = [(fx, fy, fc) for fx in (0, 1) for fy in (0, 1) for fc in (0, 1)][1:]
        copies = []
        for k, (fx, fy, fc) in enumerate(flips):
            copies.append(pltpu.make_async_remote_copy(
                src_ref=v_ref, dst_ref=slots.at[me], send_sem=send_sems.at[k], recv_sem=recv_sems.at[k],
                device_id=(x ^ fx, y ^ fy, c ^ fc), device_id_type=MESH))
        for cp in copies:
            cp.start()
        for cp in copies:
            cp.wait_send()
        for k, (fx, fy, fc) in enumerate(flips):
            src = 4 * (x ^ fx) + 2 * (y ^ fy) + (c ^ fc)
            pltpu.make_async_remote_copy(
                src_ref=v_ref, dst_ref=slots.at[src], send_sem=send_sems.at[k], recv_sem=recv_sems.at[k],
                device_id=(x ^ fx, y ^ fy, c ^ fc), device_id_type=MESH).wait_recv()
        total = slots[0]
        for d in range(1, N_DEV):
            total = total + slots[d]
        out_ref[...] = total

    return pl.pallas_call(
        body, name="sum_small",
        in_specs=[pl.BlockSpec(memory_space=pltpu.VMEM)], out_specs=pl.BlockSpec(memory_space=pltpu.VMEM),
        out_shape=jax.ShapeDtypeStruct(vec.shape, vec.dtype),
        scratch_shapes=[pltpu.VMEM((N_DEV,) + vec.shape, vec.dtype), pltpu.SemaphoreType.DMA((7,)),
                        pltpu.SemaphoreType.DMA((7,))],
    )(vec)


UPD_ROWS = 256


def _pair_sum(g, theirs, core, chip):
    nb = HALF_ROWS // UPD_ROWS

    def body(core_ref, chip_ref, g_ref, t_ref, own_ref, o16_ref):
        total = g_ref[0] + t_ref[0].astype(F32)
        o16_ref[0] = total.astype(o16_ref.dtype)

        @pl.when(pl.program_id(1) == chip_ref[0])
        def _():
            own_ref[...] = total

    return pl.pallas_call(
        body, name="pair_sum",
        grid_spec=pltpu.PrefetchScalarGridSpec(
            num_scalar_prefetch=2, grid=(nb, N_CHIPS),
            in_specs=[pl.BlockSpec((1, UPD_ROWS, FLAT_COLS), lambda i, d, core_ref, chip_ref: (d, core_ref[0] * nb + i, 0)),
                      pl.BlockSpec((1, UPD_ROWS, FLAT_COLS), lambda i, d, core_ref, chip_ref: (d, i, 0))],
            out_specs=[pl.BlockSpec((UPD_ROWS, FLAT_COLS), lambda i, d, core_ref, chip_ref: (i, 0)),
                       pl.BlockSpec((1, UPD_ROWS, FLAT_COLS), lambda i, d, core_ref, chip_ref: (d, i, 0))]),
        out_shape=[jax.ShapeDtypeStruct((HALF_ROWS, FLAT_COLS), F32),
                   jax.ShapeDtypeStruct((N_CHIPS, HALF_ROWS, FLAT_COLS), BF16)],
        compiler_params=_params(("parallel", "arbitrary")),
    )(core, chip, g, theirs)


def _chip_sum(own, received):
    nb = HALF_ROWS // UPD_ROWS

    def body(p_ref, r_ref, o_ref):
        o_ref[...] = ((p_ref[...] + r_ref[0].astype(F32)) + r_ref[1].astype(F32)) + r_ref[2].astype(F32)

    return pl.pallas_call(
        body, name="chip_sum", grid=(nb,),
        in_specs=[pl.BlockSpec((UPD_ROWS, FLAT_COLS), lambda i: (i, 0)),
                  pl.BlockSpec((3, UPD_ROWS, FLAT_COLS), lambda i: (0, i, 0))],
        out_specs=pl.BlockSpec((UPD_ROWS, FLAT_COLS), lambda i: (i, 0)),
        out_shape=jax.ShapeDtypeStruct((HALF_ROWS, FLAT_COLS), F32),
        compiler_params=_params(("parallel",)),
    )(own, received)


def _adamw(w, g, m, v, *, rows, name):
    r, c = w.shape
    rows = min(rows, r)
    assert r % rows == 0

    def body(w_ref, g_ref, m_ref, v_ref, d_ref, nm_ref, nv_ref):
        gv = g_ref[...]
        nm = ADAM_B1 * m_ref[...] + (1.0 - ADAM_B1) * gv
        nv = ADAM_B2 * v_ref[...] + (1.0 - ADAM_B2) * (gv * gv)
        m_hat = nm / (1.0 - ADAM_B1 ** ADAM_STEP)
        v_hat = nv / (1.0 - ADAM_B2 ** ADAM_STEP)
        d_ref[...] = -ADAM_LR * (m_hat / (jnp.sqrt(v_hat) + ADAM_EPS) + ADAM_WD * w_ref[...])
        nm_ref[...] = nm
        nv_ref[...] = nv

    spec = pl.BlockSpec((rows, c), lambda i: (i, 0))
    return pl.pallas_call(
        body, name=name, grid=(r // rows,),
        in_specs=[spec] * 4, out_specs=[spec] * 3,
        out_shape=[jax.ShapeDtypeStruct((r, c), F32)] * 3,
        compiler_params=_params(("parallel",)),
    )(w, g, m, v)


W_NAMES = ("a_w_in", "a_w_out", "b_w_in", "b_q_norm", "b_w_uq", "b_w_out", "kv_w_down", "kv_norm", "kv_w_up", "ln_g", "ln_b")
BIG = tuple(name for name, _ in SHARD_ROWS)


def _pack_small(ln_g, ln_b, q_norm, kv_norm, extra=None):
    pad = lambda a: jnp.pad(a.reshape(1, -1), ((0, 0), (0, FLAT_COLS - a.size)))
    rows = [ln_g, ln_b, pad(q_norm), pad(kv_norm),
            jnp.zeros((1, FLAT_COLS), F32) if extra is None else pad(extra), jnp.zeros((1, FLAT_COLS), F32)]
    return jnp.concatenate(rows, axis=0)


def _unpack_small(p):
    return dict(ln_g=p[0:2], ln_b=p[2:4], b_q_norm=p[4:5, :Q_LORA], kv_norm=p[5, :KV_LORA])


def kernel(x, a_w_in, a_w_out, b_w_in, b_q_norm, b_w_uq, b_w_out, kv_w_down, kv_norm, kv_w_up, ln_g, ln_b, loss_target, m_a_w_in, m_a_w_out, m_b_w_in, m_b_q_norm, m_b_w_uq, m_b_w_out, m_kv_w_down, m_kv_norm, m_kv_w_up, m_ln_g, m_ln_b, v_a_w_in, v_a_w_out, v_b_w_in, v_b_q_norm, v_b_w_uq, v_b_w_out, v_kv_w_down, v_kv_norm, v_kv_w_up, v_ln_g, v_ln_b):
    w_in = dict(a_w_in=a_w_in[0], a_w_out=a_w_out[0], b_w_in=b_w_in[0], b_w_uq=b_w_uq[0], b_w_out=b_w_out[0],
                kv_w_down=kv_w_down, kv_w_up=kv_w_up)
    m_in = dict(a_w_in=m_a_w_in[0], a_w_out=m_a_w_out[0], b_w_in=m_b_w_in[0], b_w_uq=m_b_w_uq[0], b_w_out=m_b_w_out[0],
                kv_w_down=m_kv_w_down, kv_w_up=m_kv_w_up)
    v_in = dict(a_w_in=v_a_w_in[0], a_w_out=v_a_w_out[0], b_w_in=v_b_w_in[0], b_w_uq=v_b_w_uq[0], b_w_out=v_b_w_out[0],
                kv_w_down=v_kv_w_down, kv_w_up=v_kv_w_up)
    shard_shapes = {name: w_in[name].shape for name in BIG}

    cx, cy, cc = lax.axis_index("x"), lax.axis_index("y"), lax.axis_index("c")
    flat16 = _flat_shards(w_in, BF16)
    gathered = lax.dynamic_update_slice(_gather_weights(flat16), flat16[None], (2 * cx + cy, 0, 0))
    weights = _kernel_layout(_full_from_gathered(gathered))

    loss, grad_x, grads, small = _local_step(x[0], loss_target[0], weights, kv_norm.reshape(1, -1), b_q_norm, ln_g, ln_b)

    g_all = _gathered_from_full(_reference_layout_grads(grads))
    theirs = _pair_exchange_halves(g_all.astype(BF16))
    own, pair16 = _pair_sum(g_all, theirs, cc.astype(jnp.int32).reshape(1), (2 * cx + cy).astype(jnp.int32).reshape(1))
    mine = _chip_sum(own, _chip_exchange(pair16))
    sibling = _pair_share(mine)
    g_flat = jnp.concatenate([jnp.where(cc == 0, mine, sibling), jnp.where(cc == 0, sibling, mine)], axis=0)

    small_sum = _sum_small(_pack_small(small["ln_g"], small["ln_b"], small["q_norm"], small["kv_norm"], loss[:, :1]))
    loss_out = small_sum[6, 0]

    g_big = _unflat_shards(g_flat, shard_shapes)
    upd = {name: _adamw(w_in[name], g_big[name], m_in[name], v_in[name], rows=256, name="adamw_" + name) for name in BIG}
    ds, nms, nvs = _adamw(_pack_small(ln_g, ln_b, b_q_norm, kv_norm), small_sum.at[6].set(0.0),
                          _pack_small(m_ln_g, m_ln_b, m_b_q_norm, m_kv_norm),
                          _pack_small(v_ln_g, v_ln_b, v_b_q_norm, v_kv_norm), rows=8, name="adamw_small")

    def assemble(big, small_packed):
        sm = _unpack_small(small_packed)
        out = {}
        for name in W_NAMES:
            if name in big:
                out[name] = big[name][None] if name in ("a_w_in", "a_w_out", "b_w_in", "b_w_uq", "b_w_out") else big[name]
            else:
                out[name] = sm[name]
        return [out[name] for name in W_NAMES]

    part = lambda k: {name: upd[name][k] for name in BIG}
    return (loss_out, grad_x[None], *assemble(g_big, small_sum), *assemble(part(0), ds), *assemble(part(1), nms),
            *assemble(part(2), nvs))
```

```python
import functools
import math

import jax
import jax.numpy as jnp
from jax import lax
from jax.experimental import pallas as pl
from jax.experimental.pallas import tpu as pltpu

F32 = jnp.float32
BF16 = jnp.bfloat16
MESH = pl.DeviceIdType.MESH

D_MODEL = 1024
DEPTH = 2
H_A, DK_A, DV_A = 4, 256, 512
WIDTH_A = H_A * DV_A
CHUNK = 128
H_B, QK_NOPE, QK_ROPE, V_HEAD = 16, 128, 64, 128
QK_PAD = 256
Q_LORA, KV_LORA = 768, 512
KV_DOWN_PAD = 640
WIDTH_B = H_B * V_HEAD
IN_A = 2 * H_A * DK_A + 2 * WIDTH_A
IN_B = Q_LORA + WIDTH_B
H1_B = KV_DOWN_PAD + IN_B
ROPE_BASE = 10000.0
ALPHA = (2.0 * DEPTH) ** 0.25
ATT_SCALE = (QK_NOPE + QK_ROPE) ** -0.5
NEG_BIG = -1e30

ADAM_LR, ADAM_B1, ADAM_B2, ADAM_EPS, ADAM_WD, ADAM_STEP = 0.001, 0.9, 0.999, 1e-08, 0.01, 10

VMEM_LIMIT_BYTES = 56 * 1024 * 1024
LANES = 128
FLAT_COLS = 1024
SHARD_ROWS = (("a_w_in", 1536), ("a_w_out", 512), ("b_w_in", 704), ("b_w_uq", 576), ("b_w_out", 512),
              ("kv_w_down", 144), ("kv_w_up", 512))
FLAT_ROWS = 4608
HALF_ROWS = FLAT_ROWS // 2
N_CHIPS = 4
N_DEV = 8


def _params(sem, vmem=VMEM_LIMIT_BYTES):
    return pltpu.CompilerParams(dimension_semantics=sem, vmem_limit_bytes=vmem)


def _row_spec(ts, w, col_block=0):
    return pl.BlockSpec((ts, w), lambda i: (i, col_block))


def _bc_spec(shape):
    nd = len(shape)
    return pl.BlockSpec(shape, lambda i: (0,) * nd)


def _sigmoid(x):
    return 1.0 / (1.0 + jnp.exp(-x))


def _fold8(v):
    ts, w = v.shape
    return jnp.sum(v.reshape(ts // 8, 8, w), axis=0)


def _mm(a, b, *, ta=False, tb=False, out_dtype=F32, tm=1024, tn=512, tk=None, name, extras=(), epilogue=None, out_tn=None,
        side=None):
    if ta:
        K, M = a.shape
    else:
        M, K = a.shape
    if tb:
        N, Kb = b.shape
    else:
        Kb, N = b.shape
    assert K == Kb, (a.shape, b.shape)
    tm, tn = min(tm, M), min(tn, N)
    tk = K if tk is None else min(tk, K)
    assert M % tm == 0 and N % tn == 0 and K % tk == 0, (name, M, N, K, tm, tn, tk)
    grid = (M // tm, N // tn, K // tk)
    nk = grid[2]
    out_tn = tn if out_tn is None else out_tn
    n_extra = len(extras)
    side_inputs = [] if side is None else list(side["inputs"])
    n_side = len(side_inputs)
    n_acc = 0 if nk == 1 else 1
    dims = (((0,) if ta else (1,), (1,) if tb else (0,)), ((), ()))

    def body(a_ref, b_ref, *rest):
        extra_refs, rest = rest[:n_extra], rest[n_extra:]
        side_in, o_ref, rest = rest[:n_side], rest[n_side], rest[n_side + 1:]
        if side is not None:
            side_refs, rest = side_in + rest[:1] + rest[1 + n_acc:], rest[1:]
            ids = [pl.program_id(d) for d in range(3)]

            @pl.when((ids[0] == 0) & (ids[1] == 0) & (ids[2] == 0))
            def _():
                side["start"](*side_refs)

        prod = lax.dot_general(a_ref[...].astype(BF16), b_ref[...].astype(BF16), dims,
                               preferred_element_type=F32)

        def store(tile):
            if epilogue is None:
                o_ref[...] = tile.astype(o_ref.dtype)
            else:
                epilogue(tile, o_ref, *extra_refs)

        if nk == 1:
            store(prod)
        else:
            acc = rest[0]
            k = pl.program_id(2)

            @pl.when(k == 0)
            def _():
                acc[...] = prod

            @pl.when(k > 0)
            def _():
                acc[...] += prod

            @pl.when(k == nk - 1)
            def _():
                store(acc[...])

        if side is not None:
            @pl.when((ids[0] == grid[0] - 1) & (ids[1] == grid[1] - 1) & (ids[2] == grid[2] - 1))
            def _():
                side["finish"](*side_refs)

    a_spec = pl.BlockSpec((tk, tm), lambda i, j, k: (k, i)) if ta else pl.BlockSpec((tm, tk), lambda i, j, k: (i, k))
    b_spec = pl.BlockSpec((tn, tk), lambda i, j, k: (j, k)) if tb else pl.BlockSpec((tk, tn), lambda i, j, k: (k, j))
    extra_specs = [pl.BlockSpec((tm, e.shape[1]), lambda i, j, k: (i, 0)) for e in extras]
    out_specs = pl.BlockSpec((tm, out_tn), lambda i, j, k: (i, j))
    out_shape = jax.ShapeDtypeStruct((M, (N // tn) * out_tn), out_dtype)
    scratch = [] if nk == 1 else [pltpu.VMEM((tm, tn), F32)]
    if side is not None:
        out_specs, out_shape, scratch = [out_specs, HBM_SPEC], [out_shape, side["out_shape"]], scratch + list(side["scratch"])
    return pl.pallas_call(
        body, name=name, grid=grid,
        in_specs=[a_spec, b_spec] + extra_specs + [HBM_SPEC] * n_side,
        out_specs=out_specs, out_shape=out_shape, scratch_shapes=scratch,
        compiler_params=_params(("parallel", "parallel", "arbitrary") if side is None else ("arbitrary",) * 3),
    )(a, b, *extras, *side_inputs)


def _rope_tables_a(s):
    half = DK_A // 2
    inv = ROPE_BASE ** (-jnp.arange(half, dtype=F32) / half)
    ang = jnp.arange(s, dtype=F32)[:, None] * inv[None, :]
    return jnp.cos(ang), jnp.sin(ang)


def _rope_tables_b(s):
    half = QK_ROPE // 2
    inv = ROPE_BASE ** (-jnp.arange(half, dtype=F32) / half)
    ang = jnp.arange(s, dtype=F32)[:, None] * inv[None, :]
    c, sn = jnp.cos(ang), jnp.sin(ang)
    z = jnp.zeros_like(c)
    cos = jnp.concatenate([c, c, z, z], axis=1)
    sa = jnp.concatenate([-sn, z, z, z], axis=1)
    sb = jnp.concatenate([z, sn, z, z], axis=1)
    return cos, sa, sb


def _rope_b(r, cos, sa, sb, sign):
    return r * cos + sign * (pltpu.roll(r, 96, 1) * sa + pltpu.roll(r, 32, 1) * sb)


def _retention_tables():
    lg = jnp.log1p(-jnp.exp2(-5.0 - jnp.arange(H_A, dtype=F32)))
    idx = jnp.arange(CHUNK, dtype=F32)
    diff = idx[:, None] - idx[None, :]
    causal = diff >= 0
    dmat = jnp.where(causal, jnp.exp(jnp.where(causal, diff, 0.0)[None] * lg[:, None, None]), 0.0)
    qdec = jnp.exp((idx + 1.0)[None, :] * lg[:, None])[:, :, None]
    kdec = jnp.exp((CHUNK - 1.0 - idx)[None, :] * lg[:, None])[:, :, None]
    cdec = jnp.broadcast_to(jnp.exp(CHUNK * lg)[:, None, None], (H_A, 1, DV_A))
    return dmat, qdec, kdec, cdec


def _group_norm(o):
    mu = jnp.mean(o, axis=-1, keepdims=True)
    oc = o - mu
    var = jnp.mean(oc * oc, axis=-1, keepdims=True)
    rstd = lax.rsqrt(var + 1e-5)
    return oc * rstd, rstd


Q_COL, K_COL, V_COL, GATE_COL = 0, H_A * DK_A, 2 * H_A * DK_A, 2 * H_A * DK_A + WIDTH_A


def _ln_stats(z):
    mu = jnp.mean(z, axis=-1, keepdims=True)
    zc = z - mu
    var = jnp.mean(zc * zc, axis=-1, keepdims=True)
    rstd = lax.rsqrt(var + 1e-5)
    return zc * rstd, rstd


def _ln_bwd(dy, xhat, rstd, g):
    dxh = dy * g
    m1 = jnp.mean(dxh, axis=-1, keepdims=True)
    m2 = jnp.mean(dxh * xhat, axis=-1, keepdims=True)
    return rstd * (dxh - m1 - xhat * m2)


def _ln_fwd(x, y, g, b, *, ts):
    s = x.shape[0]

    def body(x_ref, y_ref, g_ref, b_ref, o_ref, o16_ref):
        xhat, _ = _ln_stats(ALPHA * x_ref[...] + y_ref[...])
        out = xhat * g_ref[...] + b_ref[...]
        o_ref[...] = out
        o16_ref[...] = out.astype(o16_ref.dtype)

    return pl.pallas_call(
        body, name="ln_fwd", grid=(s // ts,),
        in_specs=[_row_spec(ts, D_MODEL), _row_spec(ts, D_MODEL), _bc_spec((1, D_MODEL)), _bc_spec((1, D_MODEL))],
        out_specs=[_row_spec(ts, D_MODEL), _row_spec(ts, D_MODEL)],
        out_shape=[jax.ShapeDtypeStruct((s, D_MODEL), F32), jax.ShapeDtypeStruct((s, D_MODEL), BF16)],
        compiler_params=_params(("parallel",)),
    )(x, y, g, b)


def _ln_loss_bwd(x1, y, target, g, b, *, ts):
    s = x1.shape[0]
    n = s // ts

    def body(x_ref, y_ref, t_ref, g_ref, b_ref, dz_ref, dz16_ref, dg_ref, db_ref, loss_ref, ag, ab, al):
        i = pl.program_id(0)

        @pl.when(i == 0)
        def _():
            ag[...] = jnp.zeros_like(ag)
            ab[...] = jnp.zeros_like(ab)
            al[...] = jnp.zeros_like(al)

        xhat, rstd = _ln_stats(ALPHA * x_ref[...] + y_ref[...])
        err = xhat * g_ref[...] + b_ref[...] - t_ref[...]
        al[...] += _fold8(err * err)
        dy = err * (1.0 / D_MODEL)
        ag[...] += _fold8(dy * xhat)
        ab[...] += _fold8(dy)
        dz = _ln_bwd(dy, xhat, rstd, g_ref[...])
        dz_ref[...] = dz
        dz16_ref[...] = dz.astype(dz16_ref.dtype)

        @pl.when(i == n - 1)
        def _():
            dg_ref[...] = jnp.sum(ag[...], axis=0, keepdims=True)
            db_ref[...] = jnp.sum(ab[...], axis=0, keepdims=True)
            loss_ref[...] = jnp.full((1, LANES), (0.5 / D_MODEL) * jnp.sum(al[...]), F32)

    return pl.pallas_call(
        body, name="ln_loss_bwd", grid=(n,),
        in_specs=[_row_spec(ts, D_MODEL)] * 3 + [_bc_spec((1, D_MODEL))] * 2,
        out_specs=[_row_spec(ts, D_MODEL), _row_spec(ts, D_MODEL), _bc_spec((1, D_MODEL)), _bc_spec((1, D_MODEL)),
                   _bc_spec((1, LANES))],
        out_shape=[jax.ShapeDtypeStruct((s, D_MODEL), F32), jax.ShapeDtypeStruct((s, D_MODEL), BF16),
                   jax.ShapeDtypeStruct((1, D_MODEL), F32), jax.ShapeDtypeStruct((1, D_MODEL), F32),
                   jax.ShapeDtypeStruct((1, LANES), F32)],
        scratch_shapes=[pltpu.VMEM((8, D_MODEL), F32)] * 3,
        compiler_params=_params(("arbitrary",)),
    )(x1, y, target, g, b)


def _ln_bwd_call(dy, x, y, g, *, ts):
    s = x.shape[0]
    n = s // ts

    def body(dy_ref, x_ref, y_ref, g_ref, dz_ref, dz16_ref, dg_ref, db_ref, ag, ab):
        i = pl.program_id(0)

        @pl.when(i == 0)
        def _():
            ag[...] = jnp.zeros_like(ag)
            ab[...] = jnp.zeros_like(ab)

        xhat, rstd = _ln_stats(ALPHA * x_ref[...] + y_ref[...])
        dy = dy_ref[...]
        ag[...] += _fold8(dy * xhat)
        ab[...] += _fold8(dy)
        dz = _ln_bwd(dy, xhat, rstd, g_ref[...])
        dz_ref[...] = dz
        dz16_ref[...] = dz.astype(dz16_ref.dtype)

        @pl.when(i == n - 1)
        def _():
            dg_ref[...] = jnp.sum(ag[...], axis=0, keepdims=True)
            db_ref[...] = jnp.sum(ab[...], axis=0, keepdims=True)

    return pl.pallas_call(
        body, name="ln_bwd", grid=(n,),
        in_specs=[_row_spec(ts, D_MODEL)] * 3 + [_bc_spec((1, D_MODEL))],
        out_specs=[_row_spec(ts, D_MODEL), _row_spec(ts, D_MODEL), _bc_spec((1, D_MODEL)), _bc_spec((1, D_MODEL))],
        out_shape=[jax.ShapeDtypeStruct((s, D_MODEL), F32), jax.ShapeDtypeStruct((s, D_MODEL), BF16),
                   jax.ShapeDtypeStruct((1, D_MODEL), F32), jax.ShapeDtypeStruct((1, D_MODEL), F32)],
        scratch_shapes=[pltpu.VMEM((8, D_MODEL), F32)] * 2,
        compiler_params=_params(("arbitrary",)),
    )(dy, x, y, g)


def _residual_store(tile, o_ref, dz_ref):
    o_ref[...] = ALPHA * dz_ref[...] + tile


C_LAT = slice(0, KV_LORA)
C_ROPE = slice(KV_LORA, KV_DOWN_PAD)
C_QL = slice(KV_DOWN_PAD, KV_DOWN_PAD + Q_LORA)
C_GATE = slice(KV_DOWN_PAD + Q_LORA, H1_B)


def _rms(x, eps=1e-6):
    r = lax.rsqrt(jnp.mean(x * x, axis=-1, keepdims=True) + eps)
    return x * r, r


def _rms_bwd(dy, xhat, r, g):
    dxh = dy * g
    return r * (dxh - xhat * jnp.mean(dxh * xhat, axis=-1, keepdims=True))


def _kvq_prep(h1, kv_norm, q_norm, cos, sa, sb, *, ts):
    s = h1.shape[0]

    def body(h_ref, kn_ref, qn_ref, c_ref, sa_ref, sb_ref, lat_ref, kr_ref, ql_ref):
        lat, _ = _rms(h_ref[:, C_LAT])
        lat_ref[...] = (lat * kn_ref[...]).astype(lat_ref.dtype)
        kr_ref[...] = _rope_b(h_ref[:, C_ROPE], c_ref[...], sa_ref[...], sb_ref[...], 1.0).astype(kr_ref.dtype)
        ql, _ = _rms(h_ref[:, C_QL])
        ql_ref[...] = (ql * qn_ref[...]).astype(ql_ref.dtype)

    return pl.pallas_call(
        body, name="kvq_prep", grid=(s // ts,),
        in_specs=[_row_spec(ts, H1_B), _bc_spec((1, KV_LORA)), _bc_spec((1, Q_LORA))] + [_row_spec(ts, 128)] * 3,
        out_specs=[_row_spec(ts, KV_LORA), _row_spec(ts, 128), _row_spec(ts, Q_LORA)],
        out_shape=[jax.ShapeDtypeStruct((s, KV_LORA), BF16), jax.ShapeDtypeStruct((s, 128), BF16),
                   jax.ShapeDtypeStruct((s, Q_LORA), BF16)],
        compiler_params=_params(("parallel",)),
    )(h1, kv_norm, q_norm, cos, sa, sb)


LOG2E = 1.4426950408889634
LN2 = 0.6931471805599453
Q_SCALE = ATT_SCALE * LOG2E


def _rope_q_store(tile, o_ref, c_ref, sa_ref, sb_ref):
    c, a, b = c_ref[...], sa_ref[...], sb_ref[...]
    for hd in range(tile.shape[1] // QK_PAD):
        lo = hd * QK_PAD
        o_ref[:, lo:lo + 128] = (tile[:, lo:lo + 128] * Q_SCALE).astype(o_ref.dtype)
        o_ref[:, lo + 128:lo + 256] = (_rope_b(tile[:, lo + 128:lo + 256], c, a, b, 1.0) * Q_SCALE).astype(o_ref.dtype)


def _assemble_k_store(tile, o_ref, kr_ref):
    r = kr_ref[...]
    for hd in range(tile.shape[1] // QK_NOPE):
        o_ref[:, hd * QK_PAD:hd * QK_PAD + 128] = tile[:, hd * 128:(hd + 1) * 128].astype(o_ref.dtype)
        o_ref[:, hd * QK_PAD + 128:(hd + 1) * QK_PAD] = r


def _h1_bwd(h1, dlat_k, dlat_v, dkr_heads, dqn, dg16, kv_norm, q_norm, cos, sa, sb, *, ts):
    s = h1.shape[0]
    n = s // ts

    def body(h_ref, dk_ref, dv_ref, dkr_ref, dqn_ref, dg_ref, kn_ref, qn_ref, c_ref, sa_ref, sb_ref,
             o_ref, dkn_ref, dqn_out_ref, akn, aqn):
        i = pl.program_id(0)

        @pl.when(i == 0)
        def _():
            akn[...] = jnp.zeros_like(akn)
            aqn[...] = jnp.zeros_like(aqn)

        lat, r = _rms(h_ref[:, C_LAT])
        dlat = dk_ref[...] + dv_ref[...]
        akn[...] += _fold8(dlat * lat)
        o_ref[:, C_LAT] = _rms_bwd(dlat, lat, r, kn_ref[...]).astype(o_ref.dtype)

        dkr = dkr_ref[:, 0:128]
        for hd in range(1, H_B):
            dkr = dkr + dkr_ref[:, hd * 128:(hd + 1) * 128]
        o_ref[:, C_ROPE] = _rope_b(dkr, c_ref[...], sa_ref[...], sb_ref[...], -1.0).astype(o_ref.dtype)

        ql, rq = _rms(h_ref[:, C_QL])
        dq = dqn_ref[...]
        aqn[...] += _fold8(dq * ql)
        o_ref[:, C_QL] = _rms_bwd(dq, ql, rq, qn_ref[...]).astype(o_ref.dtype)
        o_ref[:, C_GATE] = dg_ref[...]

        @pl.when(i == n - 1)
        def _():
            dkn_ref[...] = jnp.sum(akn[...], axis=0, keepdims=True)
            dqn_out_ref[...] = jnp.sum(aqn[...], axis=0, keepdims=True)

    return pl.pallas_call(
        body, name="h1_bwd", grid=(n,),
        in_specs=[_row_spec(ts, H1_B), _row_spec(ts, KV_LORA), _row_spec(ts, KV_LORA), _row_spec(ts, H_B * 128),
                  _row_spec(ts, Q_LORA), _row_spec(ts, WIDTH_B), _bc_spec((1, KV_LORA)), _bc_spec((1, Q_LORA))]
        + [_row_spec(ts, 128)] * 3,
        out_specs=[_row_spec(ts, H1_B), _bc_spec((1, KV_LORA)), _bc_spec((1, Q_LORA))],
        out_shape=[jax.ShapeDtypeStruct((s, H1_B), BF16), jax.ShapeDtypeStruct((1, KV_LORA), F32),
                   jax.ShapeDtypeStruct((1, Q_LORA), F32)],
        scratch_shapes=[pltpu.VMEM((8, KV_LORA), F32), pltpu.VMEM((8, Q_LORA), F32)],
        compiler_params=_params(("arbitrary",)),
    )(h1, dlat_k, dlat_v, dkr_heads, dqn, dg16, kv_norm, q_norm, cos, sa, sb)


def _dot(a, b, ca, cb):
    return lax.dot_general(a, b, (((ca,), (cb,)), ((), ())), preferred_element_type=F32)


def _table_specs():
    full = lambda shape: pl.BlockSpec(shape, lambda i: (0,) * len(shape))
    return [full((H_A, CHUNK, CHUNK)), full((H_A, CHUNK, 1)), full((H_A, CHUNK, 1)), full((H_A, 1, DV_A))]


def _retention_fwd(h_a, cos, sin, tables):
    s = h_a.shape[0]
    n = s // CHUNK

    def body(h_ref, c_ref, s_ref, dm_ref, qd_ref, kd_ref, cd_ref, q_ref, k_ref, v_ref, o_ref, u_ref, st_ref, state):
        @pl.when(pl.program_id(0) == 0)
        def _():
            state[...] = jnp.zeros_like(state)

        c, sn = c_ref[...], s_ref[...]
        for hd in range(H_A):
            qs, vs = slice(hd * DK_A, (hd + 1) * DK_A), slice(hd * DV_A, (hd + 1) * DV_A)
            for r_ref, base, scale in ((q_ref, Q_COL, 1.0), (k_ref, K_COL, DK_A ** -0.5)):
                lo = base + hd * DK_A
                x1, x2 = h_ref[:, lo:lo + 128], h_ref[:, lo + 128:lo + 256]
                r_ref[:, hd * DK_A:hd * DK_A + 128] = ((x1 * c - x2 * sn) * scale).astype(r_ref.dtype)
                r_ref[:, hd * DK_A + 128:(hd + 1) * DK_A] = ((x2 * c + x1 * sn) * scale).astype(r_ref.dtype)
            v_ref[:, vs] = h_ref[:, V_COL + hd * DV_A:V_COL + (hd + 1) * DV_A].astype(v_ref.dtype)
            qv, kv, vv = q_ref[:, qs], k_ref[:, qs], v_ref[:, vs]
            st = state[hd]
            st16 = st.astype(BF16)
            st_ref[0, hd] = st16
            scores = _dot(qv, kv, 1, 1) * dm_ref[hd]
            qd = (qv.astype(F32) * qd_ref[hd]).astype(BF16)
            o = _dot(scores.astype(BF16), vv, 1, 0) + _dot(qd, st16, 1, 0)
            o_ref[:, vs] = o
            kd = (kv.astype(F32) * kd_ref[hd]).astype(BF16)
            state[hd] = st * cd_ref[hd] + _dot(kd, vv, 0, 0)
            on, _ = _group_norm(o)
            g = h_ref[:, GATE_COL + hd * DV_A:GATE_COL + (hd + 1) * DV_A]
            u_ref[:, vs] = (on * (g * _sigmoid(g))).astype(u_ref.dtype)

    row = lambda w: pl.BlockSpec((CHUNK, w), lambda i: (i, 0))
    return pl.pallas_call(
        body, name="retention_fwd", grid=(n,),
        in_specs=[row(IN_A), row(128), row(128)] + _table_specs(),
        out_specs=[row(H_A * DK_A), row(H_A * DK_A), row(WIDTH_A), row(WIDTH_A), row(WIDTH_A),
                   pl.BlockSpec((1, H_A, DK_A, DV_A), lambda i: (i, 0, 0, 0))],
        out_shape=[jax.ShapeDtypeStruct((s, H_A * DK_A), BF16), jax.ShapeDtypeStruct((s, H_A * DK_A), BF16),
                   jax.ShapeDtypeStruct((s, WIDTH_A), BF16), jax.ShapeDtypeStruct((s, WIDTH_A), F32),
                   jax.ShapeDtypeStruct((s, WIDTH_A), BF16), jax.ShapeDtypeStruct((n, H_A, DK_A, DV_A), BF16)],
        scratch_shapes=[pltpu.VMEM((H_A, DK_A, DV_A), F32)],
        compiler_params=_params(("arbitrary",)),
    )(h_a, cos, sin, *tables)


def _retention_bwd(q, k, v, states, du, o, h_a, cos, sin, tables):
    s = q.shape[0]
    n = s // CHUNK

    def body(q_ref, k_ref, v_ref, st_ref, du_ref, o_ref, g_ref, c_ref, s_ref, dm_ref, qd_ref, kd_ref, cd_ref, dh_ref, grad_state):
        @pl.when(pl.program_id(0) == 0)
        def _():
            grad_state[...] = jnp.zeros_like(grad_state)

        c, sn = c_ref[...], s_ref[...]
        for hd in range(H_A):
            qs, vs = slice(hd * DK_A, (hd + 1) * DK_A), slice(hd * DV_A, (hd + 1) * DV_A)
            on, rstd = _group_norm(o_ref[:, vs])
            g = g_ref[:, vs]
            sg = _sigmoid(g)
            du_v = du_ref[:, vs]
            don = du_v * (g * sg)
            dh_ref[:, GATE_COL + hd * DV_A:GATE_COL + (hd + 1) * DV_A] = (du_v * on * (sg * (1.0 + g * (1.0 - sg)))).astype(dh_ref.dtype)
            m1 = jnp.mean(don, axis=-1, keepdims=True)
            m2 = jnp.mean(don * on, axis=-1, keepdims=True)
            dov = (rstd * (don - m1 - on * m2)).astype(BF16)

            qv, kv, vv = q_ref[:, qs], k_ref[:, qs], v_ref[:, vs]
            dm = dm_ref[hd]
            gs = grad_state[hd]
            g16 = gs.astype(BF16)
            scores = (_dot(qv, kv, 1, 1) * dm).astype(BF16)
            dscores = (_dot(dov, vv, 1, 1) * dm).astype(BF16)
            qd = (qv.astype(F32) * qd_ref[hd]).astype(BF16)
            kd = (kv.astype(F32) * kd_ref[hd]).astype(BF16)
            dq = _dot(dscores, kv, 1, 0) + _dot(dov, st_ref[0, hd], 1, 1) * qd_ref[hd]
            dk = (_dot(dscores, qv, 0, 0) + _dot(vv, g16, 1, 1) * kd_ref[hd]) * (DK_A ** -0.5)
            dh_ref[:, V_COL + hd * DV_A:V_COL + (hd + 1) * DV_A] = (_dot(scores, dov, 0, 0) + _dot(kd, g16, 1, 0)).astype(dh_ref.dtype)
            grad_state[hd] = gs * cd_ref[hd] + _dot(qd, dov, 0, 0)
            for d, base in ((dq, Q_COL), (dk, K_COL)):
                lo = base + hd * DK_A
                d1, d2 = d[:, 0:128], d[:, 128:256]
                dh_ref[:, lo:lo + 128] = (d1 * c + d2 * sn).astype(dh_ref.dtype)
                dh_ref[:, lo + 128:lo + 256] = (d2 * c - d1 * sn).astype(dh_ref.dtype)

    rev = lambda i: n - 1 - i
    row = lambda w, col=0: pl.BlockSpec((CHUNK, w), lambda i: (rev(i), col))
    return pl.pallas_call(
        body, name="retention_bwd", grid=(n,),
        in_specs=[row(H_A * DK_A), row(H_A * DK_A), row(WIDTH_A), pl.BlockSpec((1, H_A, DK_A, DV_A), lambda i: (rev(i), 0, 0, 0)),
                  row(WIDTH_A), row(WIDTH_A), row(WIDTH_A, GATE_COL // WIDTH_A), row(128), row(128)] + _table_specs(),
        out_specs=row(IN_A),
        out_shape=jax.ShapeDtypeStruct((s, IN_A), BF16),
        scratch_shapes=[pltpu.VMEM((H_A, DK_A, DV_A), F32)],
        compiler_params=_params(("arbitrary",)),
    )(q, k, v, states, du, o, h_a, cos, sin, *tables)


GATE_BLOCK0 = (KV_DOWN_PAD + Q_LORA) // LANES


def _causal_mask(sc, row0):
    row = lax.broadcasted_iota(jnp.int32, sc.shape, 0) + row0
    col = lax.broadcasted_iota(jnp.int32, sc.shape, 1)
    return jnp.where(col <= row, sc, NEG_BIG)


def _key_block_loop(step, i, per_trip=2):
    def trip_body(jj, carry):
        for t in range(per_trip):
            step(per_trip * jj + t, False)
        return carry

    lax.fori_loop(0, i // per_trip, trip_body, 0)
    group = per_trip // 2
    while group >= 1:
        def tail(group=group):
            first = (i // (2 * group)) * (2 * group)
            for t in range(group):
                step(first + t, False)

        pl.when((i // group) % 2 == 1)(tail)
        group //= 2
    step(i, True)


def _attention_fwd(q, k, v, h1, *, blk, sub):
    s = q.shape[0]
    nb = s // blk
    reps = blk // LANES

    def body(q_ref, k_ref, v_ref, g_ref, o_ref, u_ref, lse_ref, vext_s, m_s, acc_s):
        i = pl.program_id(1)

        @pl.when(i == 0)
        def _():
            vext_s[:, 0:V_HEAD] = v_ref[...]
            vext_s[:, V_HEAD:2 * V_HEAD] = jnp.ones((s, V_HEAD), vext_s.dtype)

        m_s[...] = jnp.full_like(m_s, NEG_BIG)
        acc_s[...] = jnp.zeros_like(acc_s)

        def step(j, diagonal):
            kv_rows = pl.ds(pl.multiple_of(j * blk, blk), blk)
            kb = k_ref[kv_rows, :]
            vb = vext_s[kv_rows, :]
            for r in range(blk // sub):
                rows = slice(r * sub, (r + 1) * sub)
                sc = _dot(q_ref[rows, :], kb, 1, 1)
                if diagonal:
                    sc = _causal_mask(sc, r * sub)
                m_prev = m_s[rows, :]
                m_new = jnp.maximum(m_prev, jnp.max(sc, axis=-1, keepdims=True))
                p = jnp.exp2(sc - jnp.tile(m_new, (1, reps)))
                a = jnp.exp2(m_prev - m_new)
                acc_s[rows, :] = jnp.tile(a, (1, 2)) * acc_s[rows, :] + _dot(p.astype(BF16), vb, 1, 0)
                m_s[rows, :] = m_new

        _key_block_loop(step, i, per_trip=8)
        acc = acc_s[...]
        l = acc[:, V_HEAD:2 * V_HEAD]
        o = acc[:, 0:V_HEAD] / l
        g = g_ref[...]
        o_ref[...] = o
        u_ref[...] = (o * (g * _sigmoid(g))).astype(u_ref.dtype)
        lse_ref[0] = (m_s[...] + jnp.log2(l))[:, 0:1]

    blk_idx = lambda h, i: (i, h)
    return pl.pallas_call(
        body, name="attention_fwd", grid=(H_B, nb),
        in_specs=[pl.BlockSpec((blk, QK_PAD), blk_idx), pl.BlockSpec((s, QK_PAD), lambda h, i: (0, h)),
                  pl.BlockSpec((s, V_HEAD), lambda h, i: (0, h)), pl.BlockSpec((blk, LANES), lambda h, i: (i, GATE_BLOCK0 + h))],
        out_specs=[pl.BlockSpec((blk, V_HEAD), blk_idx), pl.BlockSpec((blk, V_HEAD), blk_idx),
                   pl.BlockSpec((1, blk, 1), lambda h, i: (h, i, 0))],
        out_shape=[jax.ShapeDtypeStruct((s, WIDTH_B), F32), jax.ShapeDtypeStruct((s, WIDTH_B), BF16),
                   jax.ShapeDtypeStruct((H_B, s, 1), F32)],
        scratch_shapes=[pltpu.VMEM((s, 2 * V_HEAD), BF16), pltpu.VMEM((blk, LANES), F32), pltpu.VMEM((blk, 2 * V_HEAD), F32)],
        compiler_params=_params(("parallel", "arbitrary")),
    )(q, k, v, h1)


def _attention_bwd(q, k, v, du, o, h1, lse, cos, sa, sb, *, blk):
    s = q.shape[0]
    nb = s // blk
    reps = blk // LANES

    def body(q_ref, k_ref, v_ref, du_ref, o_ref, g_ref, lse_ref, c_ref, sa_ref, sb_ref,
             dq_ref, dkn_ref, dkr_ref, dv_ref, dg_ref, lse_s, dl_s, do_s, dq_s):
        i = pl.program_id(1)
        g = g_ref[...]
        sg = _sigmoid(g)
        du_v, ov = du_ref[...], o_ref[...]
        do = du_v * (g * sg)
        dg_ref[...] = (du_v * ov * (sg * (1.0 + g * (1.0 - sg)))).astype(dg_ref.dtype)
        do_s[...] = do.astype(do_s.dtype)
        dl_s[...] = jnp.broadcast_to(jnp.sum(do * ov, axis=-1, keepdims=True), (blk, LANES))
        lse_s[...] = jnp.broadcast_to(lse_ref[0], (blk, LANES))
        dq_s[...] = jnp.zeros_like(dq_s)

        def step(j, diagonal):
            kv_rows = pl.ds(pl.multiple_of(j * blk, blk), blk)
            qv, dov, kb = q_ref[...], do_s[...], k_ref[kv_rows, :]
            sc = _dot(qv, kb, 1, 1)
            if diagonal:
                sc = _causal_mask(sc, 0)
            p = jnp.exp2(sc - jnp.tile(lse_s[...], (1, reps)))
            dp = _dot(dov, v_ref[kv_rows, :], 1, 1)
            ds = (p * (dp - jnp.tile(dl_s[...], (1, reps)))).astype(BF16)
            dv_c = _dot(p.astype(BF16), dov, 0, 0)
            dk_c = _dot(ds, qv, 0, 0)
            dq_s[...] += _dot(ds, kb, 1, 0)
            if diagonal:
                dkn_ref[kv_rows, :] = dk_c[:, 0:128]
                dkr_ref[kv_rows, :] = dk_c[:, 128:256]
                dv_ref[kv_rows, :] = dv_c
            else:
                dkn_ref[kv_rows, :] += dk_c[:, 0:128]
                dkr_ref[kv_rows, :] += dk_c[:, 128:256]
                dv_ref[kv_rows, :] += dv_c

        _key_block_loop(step, i, per_trip=4)
        dq = dq_s[...] * ATT_SCALE
        dq_ref[:, 0:128] = dq[:, 0:128].astype(dq_ref.dtype)
        dq_ref[:, 128:256] = _rope_b(dq[:, 128:256], c_ref[...], sa_ref[...], sb_ref[...], -1.0).astype(dq_ref.dtype)

        @pl.when(i == nb - 1)
        def _():
            dkn_ref[...] = dkn_ref[...] * LN2
            dkr_ref[...] = dkr_ref[...] * LN2

    head = lambda h, i: (0, h)
    blk_idx = lambda h, i: (i, h)
    row_idx = lambda h, i: (i, 0)
    return pl.pallas_call(
        body, name="attention_bwd", grid=(H_B, nb),
        in_specs=[pl.BlockSpec((blk, QK_PAD), blk_idx), pl.BlockSpec((s, QK_PAD), head), pl.BlockSpec((s, V_HEAD), head),
                  pl.BlockSpec((blk, V_HEAD), blk_idx), pl.BlockSpec((blk, V_HEAD), blk_idx),
                  pl.BlockSpec((blk, LANES), lambda h, i: (i, GATE_BLOCK0 + h)), pl.BlockSpec((1, blk, 1), lambda h, i: (h, i, 0)),
                  pl.BlockSpec((blk, LANES), row_idx), pl.BlockSpec((blk, LANES), row_idx), pl.BlockSpec((blk, LANES), row_idx)],
        out_specs=[pl.BlockSpec((blk, QK_PAD), blk_idx), pl.BlockSpec((s, 128), head), pl.BlockSpec((s, 128), head),
                   pl.BlockSpec((s, 128), head), pl.BlockSpec((blk, V_HEAD), blk_idx)],
        out_shape=[jax.ShapeDtypeStruct((s, H_B * QK_PAD), BF16), jax.ShapeDtypeStruct((s, H_B * 128), F32),
                   jax.ShapeDtypeStruct((s, H_B * 128), F32), jax.ShapeDtypeStruct((s, H_B * 128), F32),
                   jax.ShapeDtypeStruct((s, WIDTH_B), BF16)],
        scratch_shapes=[pltpu.VMEM((blk, LANES), F32), pltpu.VMEM((blk, LANES), F32), pltpu.VMEM((blk, V_HEAD), BF16),
                        pltpu.VMEM((blk, QK_PAD), F32)],
        compiler_params=_params(("parallel", "arbitrary")),
    )(q, k, v, du, o, h1, lse, cos, sa, sb)


def _local_step(x, target, w, kv_norm, q_norm, ln_g, ln_b, *, ts=256, blk=512, late=None):
    s = x.shape[0]
    cos_a, sin_a = _rope_tables_a(s)
    cos_b, sa_b, sb_b = _rope_tables_b(s)
    tables = _retention_tables()
    g0, g1, b0, b1 = ln_g[0:1], ln_g[1:2], ln_b[0:1], ln_b[1:2]

    x16 = x.astype(BF16)
    if late is None:
        h_a = _mm(x16, w["a_in"], tn=1536, name="a_in_fwd")
    else:
        flat_b, chip = late
        h_a, got = _mm(x16, w["a_in"], tn=1536, name="a_in_fwd", side=_gather_side(flat_b))
        got = lax.dynamic_update_slice(got, flat_b[None], (chip, 0, 0))
        w = {**w, **_kernel_layout_b(_full_from_gathered(got, B_SHARDS))}
    q_a, k_a, v_a, o_a, u_a, states = _retention_fwd(h_a, cos_a, sin_a, tables)
    y_a = _mm(u_a, w["a_out"], tn=1024, name="a_out_fwd")
    x1, x1_16 = _ln_fwd(x, y_a, g0, b0, ts=ts)

    h1 = _mm(x1_16, w["b_in1"], tn=1152, name="b_in_fwd")
    lat16, kr16, qn16 = _kvq_prep(h1, kv_norm, q_norm, cos_b, sa_b, sb_b, ts=ts)
    k16 = _mm(lat16, w["up_k"], out_dtype=BF16, tn=2048, out_tn=H_B * QK_PAD, extras=(kr16,), epilogue=_assemble_k_store, name="up_k_fwd")
    v16 = _mm(lat16, w["up_v"], out_dtype=BF16, tn=2048, name="up_v_fwd")
    q16 = _mm(qn16, w["uq"], out_dtype=BF16, tn=2048, extras=(cos_b, sa_b, sb_b), epilogue=_rope_q_store, name="uq_fwd")
    o_b, u_b, lse = _attention_fwd(q16, k16, v16, h1, blk=blk, sub=blk // 2)
    y_b = _mm(u_b, w["b_out"], tn=1024, name="b_out_fwd")

    dz_b, dz_b16, dg1, db1, loss = _ln_loss_bwd(x1, y_b, target, g1, b1, ts=ts)
    d_b_out = _mm(u_b, dz_b16, ta=True, tn=1024, tk=1024, name="b_out_dw")
    du_b = _mm(dz_b16, w["b_out"], tb=True, tn=1024, name="b_out_dx")
    dqf16, dkn, dkr_heads, dv, dgate16 = _attention_bwd(q16, k16, v16, du_b, o_b, h1, lse, cos_b, sa_b, sb_b, blk=blk)
    d_uq = _mm(qn16, dqf16, ta=True, tm=768, tn=2048, tk=1024, name="uq_dw")
    dqn = _mm(dqf16, w["uq"], tb=True, tn=768, tk=2048, name="uq_dx")
    d_up_k = _mm(lat16, dkn, ta=True, tn=2048, tk=1024, name="up_k_dw")
    d_up_v = _mm(lat16, dv, ta=True, tn=2048, tk=1024, name="up_v_dw")
    dlat_k = _mm(dkn, w["up_k"], tb=True, tn=512, name="up_k_dx")
    dlat_v = _mm(dv, w["up_v"], tb=True, tn=512, name="up_v_dx")
    dh1, dkvn, dqnorm = _h1_bwd(h1, dlat_k, dlat_v, dkr_heads, dqn, dgate16, kv_norm, q_norm, cos_b, sa_b, sb_b, ts=ts)
    d_b_in1 = _mm(x1_16, dh1, ta=True, tn=1152, tk=1024, name="b_in_dw")
    dx1 = _mm(dh1, w["b_in1"], tb=True, tn=1024, extras=(dz_b,), epilogue=_residual_store, name="b_in_dx")

    dz_a, dz_a16, dg0, db0 = _ln_bwd_call(dx1, x, y_a, g0, ts=ts)
    d_a_out = _mm(u_a, dz_a16, ta=True, tn=1024, tk=1024, name="a_out_dw")
    du_a = _mm(dz_a16, w["a_out"], tb=True, tn=1024, name="a_out_dx")
    dh_a = _retention_bwd(q_a, k_a, v_a, states, du_a, o_a, h_a, cos_a, sin_a, tables)
    d_a_in = _mm(x16, dh_a, ta=True, tn=1536, tk=1024, name="a_in_dw")
    grad_x = _mm(dh_a, w["a_in"], tb=True, tn=1024, tk=2048, extras=(dz_a,), epilogue=_residual_store, name="a_in_dx")

    grads = dict(a_in=d_a_in, a_out=d_a_out, b_in1=d_b_in1, uq=d_uq, b_out=d_b_out, up_k=d_up_k, up_v=d_up_v)
    small = dict(ln_g=jnp.concatenate([dg0, dg1], axis=0), ln_b=jnp.concatenate([db0, db1], axis=0),
                 q_norm=dqnorm, kv_norm=dkvn)
    return loss, grad_x, grads, small


def _flat_shards(shards, dtype):
    parts = [shards[name].reshape(rows, FLAT_COLS) for name, rows in SHARD_ROWS]
    used = sum(rows for _, rows in SHARD_ROWS)
    parts.append(jnp.zeros((FLAT_ROWS - used, FLAT_COLS), shards["a_w_in"].dtype))
    return jnp.concatenate(parts, axis=0).astype(dtype)


def _unflat_shards(flat, shapes):
    out, off = {}, 0
    for name, rows in SHARD_ROWS:
        out[name] = flat[off:off + rows].reshape(shapes[name])
        off += rows
    return out


COL_SHARDED = {"a_w_in": (D_MODEL, IN_A), "b_w_in": (D_MODEL, IN_B), "b_w_uq": (Q_LORA, H_B * (QK_NOPE + QK_ROPE)),
               "kv_w_up": (KV_LORA, H_B * (QK_NOPE + V_HEAD))}
ROW_SHARDED = {"a_w_out": (WIDTH_A, D_MODEL), "b_w_out": (WIDTH_B, D_MODEL), "kv_w_down": (D_MODEL, KV_LORA + QK_ROPE)}


def _full_from_gathered(gathered, shards=SHARD_ROWS):
    out, off = {}, 0
    for name, rows in shards:
        part = gathered[:, off:off + rows]
        off += rows
        if name in COL_SHARDED:
            r, c = COL_SHARDED[name]
            out[name] = part.reshape(N_CHIPS, r, c // N_CHIPS).transpose(1, 0, 2).reshape(r, c)
        else:
            r, c = ROW_SHARDED[name]
            out[name] = part.reshape(r, c)
    return out


def _gathered_from_full(full):
    parts = []
    for name, rows in SHARD_ROWS:
        g = full[name]
        if name in COL_SHARDED:
            r, c = COL_SHARDED[name]
            g = g.reshape(r, N_CHIPS, c // N_CHIPS).transpose(1, 0, 2)
        parts.append(g.reshape(N_CHIPS, rows, FLAT_COLS))
    used = sum(rows for _, rows in SHARD_ROWS)
    parts.append(jnp.zeros((N_CHIPS, FLAT_ROWS - used, FLAT_COLS), F32))
    return jnp.concatenate(parts, axis=1)


A_SHARDS, B_SHARDS = SHARD_ROWS[:2], SHARD_ROWS[2:]
A_ROWS = sum(rows for _, rows in A_SHARDS)


def _kernel_layout_a(full):
    return dict(a_in=full["a_w_in"], a_out=full["a_w_out"])


def _kernel_layout_b(full):
    uq = full["b_w_uq"].reshape(Q_LORA, H_B, QK_NOPE + QK_ROPE)
    uq = jnp.pad(uq, ((0, 0), (0, 0), (0, QK_PAD - QK_NOPE - QK_ROPE))).reshape(Q_LORA, H_B * QK_PAD)
    up = full["kv_w_up"].reshape(KV_LORA, H_B, QK_NOPE + V_HEAD)
    down = jnp.pad(full["kv_w_down"], ((0, 0), (0, KV_DOWN_PAD - KV_LORA - QK_ROPE)))
    return dict(b_out=full["b_w_out"], uq=uq,
                up_k=up[:, :, :QK_NOPE].reshape(KV_LORA, H_B * QK_NOPE),
                up_v=up[:, :, QK_NOPE:].reshape(KV_LORA, H_B * V_HEAD),
                b_in1=jnp.concatenate([down, full["b_w_in"]], axis=1))


def _kernel_layout(full):
    return {**_kernel_layout_a(full), **_kernel_layout_b(full)}


def _reference_layout_grads(g):
    uq = g["uq"].reshape(Q_LORA, H_B, QK_PAD)[:, :, :QK_NOPE + QK_ROPE].reshape(Q_LORA, H_B * (QK_NOPE + QK_ROPE))
    up = jnp.concatenate([g["up_k"].reshape(KV_LORA, H_B, QK_NOPE), g["up_v"].reshape(KV_LORA, H_B, V_HEAD)], axis=2)
    return dict(a_w_in=g["a_in"], a_w_out=g["a_out"], b_w_out=g["b_out"], b_w_uq=uq,
                kv_w_up=up.reshape(KV_LORA, H_B * (QK_NOPE + V_HEAD)),
                kv_w_down=g["b_in1"][:, :KV_LORA + QK_ROPE], b_w_in=g["b_in1"][:, KV_DOWN_PAD:])


HBM_SPEC = pl.BlockSpec(memory_space=pl.ANY)


def _me():
    return lax.axis_index("x"), lax.axis_index("y"), lax.axis_index("c")


def _chip_flips(x, y):
    return [(1 - x, y), (x, 1 - y), (1 - x, 1 - y)]


def _gather_copies(src_ref, out_ref, send_sems, recv_sems):
    x, y, c = _me()
    half = src_ref.shape[0] // 2
    my_rows = pl.ds(pl.multiple_of(c * half, 16), half)
    their_rows = pl.ds(pl.multiple_of((1 - c) * half, 16), half)
    chips = _chip_flips(x, y)
    sibling = (x, y, 1 - c)

    def copy(k, src, dst, to):
        return pltpu.make_async_remote_copy(src_ref=src, dst_ref=dst, send_sem=send_sems.at[k], recv_sem=recv_sems.at[k],
                                            device_id=to, device_id_type=MESH)

    sends = [copy(k, src_ref.at[my_rows, :], out_ref.at[2 * x + y, my_rows, :], (px, py, c)) for k, (px, py) in enumerate(chips)]
    landed = [out_ref.at[2 * px + py, my_rows, :] for px, py in chips]
    lands = [copy(k, landed[k], landed[k], (px, py, c)) for k, (px, py) in enumerate(chips)]
    forwards = [copy(3 + k, landed[k], landed[k], sibling) for k in range(3)]
    theirs = [out_ref.at[2 * px + py, their_rows, :] for px, py in chips]
    arrivals = [copy(3 + k, theirs[k], theirs[k], sibling) for k in range(3)]
    return sends, lands, forwards, arrivals


def _gather_start(src_ref, out_ref, send_sems, recv_sems):
    sends, _, _, _ = _gather_copies(src_ref, out_ref, send_sems, recv_sems)
    for cp in sends:
        cp.start()


def _gather_finish(src_ref, out_ref, send_sems, recv_sems):
    sends, lands, forwards, arrivals = _gather_copies(src_ref, out_ref, send_sems, recv_sems)
    for k in range(3):
        lands[k].wait_recv()
        forwards[k].start()
    for cp in arrivals:
        cp.wait_recv()
    for cp in sends + forwards:
        cp.wait_send()


def _gather_scratch():
    return [pltpu.SemaphoreType.DMA((6,)), pltpu.SemaphoreType.DMA((6,))]


def _gather_weights(flat16):
    def body(src_ref, out_ref, send_sems, recv_sems):
        _gather_start(src_ref, out_ref, send_sems, recv_sems)
        _gather_finish(src_ref, out_ref, send_sems, recv_sems)

    return pl.pallas_call(
        body, name="gather_weights",
        in_specs=[HBM_SPEC], out_specs=HBM_SPEC,
        out_shape=jax.ShapeDtypeStruct((N_CHIPS,) + flat16.shape, flat16.dtype),
        scratch_shapes=_gather_scratch(),
    )(flat16)


def _gather_side(flat16):
    return dict(inputs=[flat16], out_shape=jax.ShapeDtypeStruct((N_CHIPS,) + flat16.shape, flat16.dtype),
                scratch=_gather_scratch(), start=_gather_start, finish=_gather_finish)


def _chip_exchange(p):
    def body(p_ref, out_ref, send_sems, recv_sems):
        x, y, c = _me()
        copies = []
        for k, (px, py) in enumerate(_chip_flips(x, y)):
            copies.append(pltpu.make_async_remote_copy(
                src_ref=p_ref.at[2 * px + py], dst_ref=out_ref.at[k], send_sem=send_sems.at[k], recv_sem=recv_sems.at[k],
                device_id=(px, py, c), device_id_type=MESH))
        for cp in copies:
            cp.start()
        for cp in copies:
            cp.wait_send()
        for cp in copies:
            cp.wait_recv()

    return pl.pallas_call(
        body, name="chip_exchange",
        in_specs=[HBM_SPEC], out_specs=HBM_SPEC,
        out_shape=jax.ShapeDtypeStruct((3, HALF_ROWS, FLAT_COLS), p.dtype),
        scratch_shapes=[pltpu.SemaphoreType.DMA((3,)), pltpu.SemaphoreType.DMA((3,))],
    )(p)


def _pair_swap(r, name):
    def body(r_ref, out_ref, send_sem, recv_sem):
        x, y, c = _me()
        cp = pltpu.make_async_remote_copy(src_ref=r_ref, dst_ref=out_ref, send_sem=send_sem, recv_sem=recv_sem,
                                          device_id=(x, y, 1 - c), device_id_type=MESH)
        cp.start()
        cp.wait_send()
        cp.wait_recv()

    return pl.pallas_call(
        body, name=name,
        in_specs=[HBM_SPEC], out_specs=HBM_SPEC,
        out_shape=jax.ShapeDtypeStruct(r.shape, r.dtype),
        scratch_shapes=[pltpu.SemaphoreType.DMA, pltpu.SemaphoreType.DMA],
    )(r)


def _sum_small(vec):
    def body(v_ref, out_ref, slots, send_sems, recv_sems):
        x, y, c = _me()
        me = 4 * x + 2 * y + c
        slots[me] = v_ref[...]
        flips = [(fx, fy, fc) for fx in (0, 1) for fy in (0, 1) for fc in (0, 1)][1:]
        copies = []
        for k, (fx, fy, fc) in enumerate(flips):
            copies.append(pltpu.make_async_remote_copy(
                src_ref=v_ref, dst_ref=slots.at[me], send_sem=send_sems.at[k], recv_sem=recv_sems.at[k],
                device_id=(x ^ fx, y ^ fy, c ^ fc), device_id_type=MESH))
        for cp in copies:
            cp.start()
        for cp in copies:
            cp.wait_send()
        for k, (fx, fy, fc) in enumerate(flips):
            src = 4 * (x ^ fx) + 2 * (y ^ fy) + (c ^ fc)
            pltpu.make_async_remote_copy(
                src_ref=v_ref, dst_ref=slots.at[src], send_sem=send_sems.at[k], recv_sem=recv_sems.at[k],
                device_id=(x ^ fx, y ^ fy, c ^ fc), device_id_type=MESH).wait_recv()
        total = slots[0]
        for d in range(1, N_DEV):
            total = total + slots[d]
        out_ref[...] = total

    return pl.pallas_call(
        body, name="sum_small",
        in_specs=[pl.BlockSpec(memory_space=pltpu.VMEM)], out_specs=pl.BlockSpec(memory_space=pltpu.VMEM),
        out_shape=jax.ShapeDtypeStruct(vec.shape, vec.dtype),
        scratch_shapes=[pltpu.VMEM((N_DEV,) + vec.shape, vec.dtype), pltpu.SemaphoreType.DMA((7,)),
                        pltpu.SemaphoreType.DMA((7,))],
    )(vec)


UPD_ROWS = 256


def _pair_sum(g, theirs, core, chip):
    nb = HALF_ROWS // UPD_ROWS

    def body(core_ref, chip_ref, g_ref, t_ref, own_ref, o16_ref):
        total = g_ref[0] + t_ref[0].astype(F32)
        o16_ref[0] = total.astype(o16_ref.dtype)

        @pl.when(pl.program_id(1) == chip_ref[0])
        def _():
            own_ref[...] = total

    return pl.pallas_call(
        body, name="pair_sum",
        grid_spec=pltpu.PrefetchScalarGridSpec(
            num_scalar_prefetch=2, grid=(nb, N_CHIPS),
            in_specs=[pl.BlockSpec((1, UPD_ROWS, FLAT_COLS), lambda i, d, core_ref, chip_ref: (d, core_ref[0] * nb + i, 0)),
                      pl.BlockSpec((1, UPD_ROWS, FLAT_COLS), lambda i, d, core_ref, chip_ref: (d, i, 0))],
            out_specs=[pl.BlockSpec((UPD_ROWS, FLAT_COLS), lambda i, d, core_ref, chip_ref: (i, 0)),
                       pl.BlockSpec((1, UPD_ROWS, FLAT_COLS), lambda i, d, core_ref, chip_ref: (d, i, 0))]),
        out_shape=[jax.ShapeDtypeStruct((HALF_ROWS, FLAT_COLS), F32),
                   jax.ShapeDtypeStruct((N_CHIPS, HALF_ROWS, FLAT_COLS), BF16)],
        compiler_params=_params(("parallel", "arbitrary")),
    )(core, chip, g, theirs)


def _chip_sum(own, received):
    nb = HALF_ROWS // UPD_ROWS

    def body(p_ref, r_ref, o_ref):
        o_ref[...] = ((p_ref[...] + r_ref[0].astype(F32)) + r_ref[1].astype(F32)) + r_ref[2].astype(F32)

    return pl.pallas_call(
        body, name="chip_sum", grid=(nb,),
        in_specs=[pl.BlockSpec((UPD_ROWS, FLAT_COLS), lambda i: (i, 0)),
                  pl.BlockSpec((3, UPD_ROWS, FLAT_COLS), lambda i: (0, i, 0))],
        out_specs=pl.BlockSpec((UPD_ROWS, FLAT_COLS), lambda i: (i, 0)),
        out_shape=jax.ShapeDtypeStruct((HALF_ROWS, FLAT_COLS), F32),
        compiler_params=_params(("parallel",)),
    )(own, received)


def _adamw(w, g, m, v, *, rows, name):
    r, c = w.shape
    rows = min(rows, r)
    assert r % rows == 0

    def body(w_ref, g_ref, m_ref, v_ref, d_ref, nm_ref, nv_ref):
        gv = g_ref[...]
        nm = ADAM_B1 * m_ref[...] + (1.0 - ADAM_B1) * gv
        nv = ADAM_B2 * v_ref[...] + (1.0 - ADAM_B2) * (gv * gv)
        m_hat = nm / (1.0 - ADAM_B1 ** ADAM_STEP)
        v_hat = nv / (1.0 - ADAM_B2 ** ADAM_STEP)
        d_ref[...] = -ADAM_LR * (m_hat / (jnp.sqrt(v_hat) + ADAM_EPS) + ADAM_WD * w_ref[...])
        nm_ref[...] = nm
        nv_ref[...] = nv

    spec = pl.BlockSpec((rows, c), lambda i: (i, 0))
    return pl.pallas_call(
        body, name=name, grid=(r // rows,),
        in_specs=[spec] * 4, out_specs=[spec] * 3,
        out_shape=[jax.ShapeDtypeStruct((r, c), F32)] * 3,
        compiler_params=_params(("parallel",)),
    )(w, g, m, v)


W_NAMES = ("a_w_in", "a_w_out", "b_w_in", "b_q_norm", "b_w_uq", "b_w_out", "kv_w_down", "kv_norm", "kv_w_up", "ln_g", "ln_b")
BIG = tuple(name for name, _ in SHARD_ROWS)


def _pack_small(ln_g, ln_b, q_norm, kv_norm, extra=None):
    pad = lambda a: jnp.pad(a.reshape(1, -1), ((0, 0), (0, FLAT_COLS - a.size)))
    rows = [ln_g, ln_b, pad(q_norm), pad(kv_norm),
            jnp.zeros((1, FLAT_COLS), F32) if extra is None else pad(extra), jnp.zeros((1, FLAT_COLS), F32)]
    return jnp.concatenate(rows, axis=0)


def _unpack_small(p):
    return dict(ln_g=p[0:2], ln_b=p[2:4], b_q_norm=p[4:5, :Q_LORA], kv_norm=p[5, :KV_LORA])


def kernel(x, a_w_in, a_w_out, b_w_in, b_q_norm, b_w_uq, b_w_out, kv_w_down, kv_norm, kv_w_up, ln_g, ln_b, loss_target, m_a_w_in, m_a_w_out, m_b_w_in, m_b_q_norm, m_b_w_uq, m_b_w_out, m_kv_w_down, m_kv_norm, m_kv_w_up, m_ln_g, m_ln_b, v_a_w_in, v_a_w_out, v_b_w_in, v_b_q_norm, v_b_w_uq, v_b_w_out, v_kv_w_down, v_kv_norm, v_kv_w_up, v_ln_g, v_ln_b):
    w_in = dict(a_w_in=a_w_in[0], a_w_out=a_w_out[0], b_w_in=b_w_in[0], b_w_uq=b_w_uq[0], b_w_out=b_w_out[0],
                kv_w_down=kv_w_down, kv_w_up=kv_w_up)
    m_in = dict(a_w_in=m_a_w_in[0], a_w_out=m_a_w_out[0], b_w_in=m_b_w_in[0], b_w_uq=m_b_w_uq[0], b_w_out=m_b_w_out[0],
                kv_w_down=m_kv_w_down, kv_w_up=m_kv_w_up)
    v_in = dict(a_w_in=v_a_w_in[0], a_w_out=v_a_w_out[0], b_w_in=v_b_w_in[0], b_w_uq=v_b_w_uq[0], b_w_out=v_b_w_out[0],
                kv_w_down=v_kv_w_down, kv_w_up=v_kv_w_up)
    shard_shapes = {name: w_in[name].shape for name in BIG}

    cx, cy, cc = lax.axis_index("x"), lax.axis_index("y"), lax.axis_index("c")
    flat16 = _flat_shards(w_in, BF16)
    chip = 2 * cx + cy
    flat_a, flat_b = flat16[:A_ROWS], flat16[A_ROWS:]
    got_a = lax.dynamic_update_slice(_gather_weights(flat_a), flat_a[None], (chip, 0, 0))
    weights_a = _kernel_layout_a(_full_from_gathered(got_a, A_SHARDS))

    loss, grad_x, grads, small = _local_step(x[0], loss_target[0], weights_a, kv_norm.reshape(1, -1), b_q_norm, ln_g, ln_b,
                                             late=(flat_b, chip))

    g_all = _gathered_from_full(_reference_layout_grads(grads))
    other_half = lax.dynamic_slice(g_all, (0, (1 - cc) * HALF_ROWS, 0), (N_CHIPS, HALF_ROWS, FLAT_COLS)).astype(BF16)
    theirs = _pair_swap(other_half, "pair_exchange_halves")
    own, pair16 = _pair_sum(g_all, theirs, cc.astype(jnp.int32).reshape(1), (2 * cx + cy).astype(jnp.int32).reshape(1))
    mine = _chip_sum(own, _chip_exchange(pair16))
    sibling = _pair_swap(mine, "pair_share")
    g_flat = jnp.concatenate([jnp.where(cc == 0, mine, sibling), jnp.where(cc == 0, sibling, mine)], axis=0)

    small_sum = _sum_small(_pack_small(small["ln_g"], small["ln_b"], small["q_norm"], small["kv_norm"], loss[:, :1]))
    loss_out = small_sum[6, 0]

    g_big = _unflat_shards(g_flat, shard_shapes)
    upd = {name: _adamw(w_in[name], g_big[name], m_in[name], v_in[name], rows=256, name="adamw_" + name) for name in BIG}
    ds, nms, nvs = _adamw(_pack_small(ln_g, ln_b, b_q_norm, kv_norm), small_sum.at[6].set(0.0),
                          _pack_small(m_ln_g, m_ln_b, m_b_q_norm, m_kv_norm),
                          _pack_small(v_ln_g, v_ln_b, v_b_q_norm, v_kv_norm), rows=8, name="adamw_small")

    def assemble(big, small_packed):
        sm = _unpack_small(small_packed)
        out = {}
        for name in W_NAMES:
            if name in big:
                out[name] = big[name][None] if name in ("a_w_in", "a_w_out", "b_w_in", "b_w_uq", "b_w_out") else big[name]
            else:
                out[name] = sm[name]
        return [out[name] for name in W_NAMES]

    part = lambda k: {name: upd[name][k] for name in BIG}
    return (loss_out, grad_x[None], *assemble(g_big, small_sum), *assemble(part(0), ds), *assemble(part(1), nms),
            *assemble(part(2), nvs))
```

```python
import functools
import math

import jax
import jax.numpy as jnp
from jax import lax
from jax.experimental import pallas as pl
from jax.experimental.pallas import tpu as pltpu

F32 = jnp.float32
BF16 = jnp.bfloat16
MESH = pl.DeviceIdType.MESH

D_MODEL = 1024
DEPTH = 2
H_A, DK_A, DV_A = 4, 256, 512
WIDTH_A = H_A * DV_A
CHUNK = 128
H_B, QK_NOPE, QK_ROPE, V_HEAD = 16, 128, 64, 128
QK_PAD = 256
Q_LORA, KV_LORA = 768, 512
KV_DOWN_PAD = 640
WIDTH_B = H_B * V_HEAD
IN_A = 2 * H_A * DK_A + 2 * WIDTH_A
IN_B = Q_LORA + WIDTH_B
H1_B = KV_DOWN_PAD + IN_B
ROPE_BASE = 10000.0
ALPHA = (2.0 * DEPTH) ** 0.25
ATT_SCALE = (QK_NOPE + QK_ROPE) ** -0.5
NEG_BIG = -1e30

ADAM_LR, ADAM_B1, ADAM_B2, ADAM_EPS, ADAM_WD, ADAM_STEP = 0.001, 0.9, 0.999, 1e-08, 0.01, 10

VMEM_LIMIT_BYTES = 56 * 1024 * 1024
LANES = 128
FLAT_COLS = 1024
SHARD_ROWS = (("a_w_in", 1536), ("a_w_out", 512), ("b_w_in", 704), ("b_w_uq", 576), ("b_w_out", 512),
              ("kv_w_down", 144), ("kv_w_up", 512))
FLAT_ROWS = 4608
HALF_ROWS = FLAT_ROWS // 2
N_CHIPS = 4
N_DEV = 8


def _params(sem, vmem=VMEM_LIMIT_BYTES):
    return pltpu.CompilerParams(dimension_semantics=sem, vmem_limit_bytes=vmem)


def _row_spec(ts, w, col_block=0):
    return pl.BlockSpec((ts, w), lambda i: (i, col_block))


def _bc_spec(shape):
    nd = len(shape)
    return pl.BlockSpec(shape, lambda i: (0,) * nd)


def _sigmoid(x):
    return 1.0 / (1.0 + jnp.exp(-x))


def _fold8(v):
    ts, w = v.shape
    return jnp.sum(v.reshape(ts // 8, 8, w), axis=0)


def _mm(a, b, *, ta=False, tb=False, out_dtype=F32, tm=1024, tn=512, tk=None, name, extras=(), epilogue=None, out_tn=None,
        side=None):
    if ta:
        K, M = a.shape
    else:
        M, K = a.shape
    if tb:
        N, Kb = b.shape
    else:
        Kb, N = b.shape
    assert K == Kb, (a.shape, b.shape)
    tm, tn = min(tm, M), min(tn, N)
    tk = K if tk is None else min(tk, K)
    assert M % tm == 0 and N % tn == 0 and K % tk == 0, (name, M, N, K, tm, tn, tk)
    grid = (M // tm, N // tn, K // tk)
    nk = grid[2]
    out_tn = tn if out_tn is None else out_tn
    n_extra = len(extras)
    side_inputs = [] if side is None else list(side["inputs"])
    n_side = len(side_inputs)
    n_acc = 0 if nk == 1 else 1
    dims = (((0,) if ta else (1,), (1,) if tb else (0,)), ((), ()))

    def body(a_ref, b_ref, *rest):
        extra_refs, rest = rest[:n_extra], rest[n_extra:]
        side_in, o_ref, rest = rest[:n_side], rest[n_side], rest[n_side + 1:]
        if side is not None:
            side_refs, rest = side_in + rest[:1] + rest[1 + n_acc:], rest[1:]
            ids = [pl.program_id(d) for d in range(3)]

            @pl.when((ids[0] == 0) & (ids[1] == 0) & (ids[2] == 0))
            def _():
                side["start"](*side_refs)

        prod = lax.dot_general(a_ref[...].astype(BF16), b_ref[...].astype(BF16), dims,
                               preferred_element_type=F32)

        def store(tile):
            if epilogue is None:
                o_ref[...] = tile.astype(o_ref.dtype)
            else:
                epilogue(tile, o_ref, *extra_refs)

        if nk == 1:
            store(prod)
        else:
            acc = rest[0]
            k = pl.program_id(2)

            @pl.when(k == 0)
            def _():
                acc[...] = prod

            @pl.when(k > 0)
            def _():
                acc[...] += prod

            @pl.when(k == nk - 1)
            def _():
                store(acc[...])

        if side is not None:
            @pl.when((ids[0] == grid[0] - 1) & (ids[1] == grid[1] - 1) & (ids[2] == grid[2] - 1))
            def _():
                side["finish"](*side_refs)

    a_spec = pl.BlockSpec((tk, tm), lambda i, j, k: (k, i)) if ta else pl.BlockSpec((tm, tk), lambda i, j, k: (i, k))
    b_spec = pl.BlockSpec((tn, tk), lambda i, j, k: (j, k)) if tb else pl.BlockSpec((tk, tn), lambda i, j, k: (k, j))
    extra_specs = [pl.BlockSpec((tm, e.shape[1]), lambda i, j, k: (i, 0)) for e in extras]
    out_specs = pl.BlockSpec((tm, out_tn), lambda i, j, k: (i, j))
    out_shape = jax.ShapeDtypeStruct((M, (N // tn) * out_tn), out_dtype)
    scratch = [] if nk == 1 else [pltpu.VMEM((tm, tn), F32)]
    if side is not None:
        out_specs, out_shape, scratch = [out_specs, HBM_SPEC], [out_shape, side["out_shape"]], scratch + list(side["scratch"])
    return pl.pallas_call(
        body, name=name, grid=grid,
        in_specs=[a_spec, b_spec] + extra_specs + [HBM_SPEC] * n_side,
        out_specs=out_specs, out_shape=out_shape, scratch_shapes=scratch,
        compiler_params=_params(("parallel", "parallel", "arbitrary") if side is None else ("arbitrary",) * 3),
    )(a, b, *extras, *side_inputs)


def _rope_tables_a(s):
    half = DK_A // 2
    inv = ROPE_BASE ** (-jnp.arange(half, dtype=F32) / half)
    ang = jnp.arange(s, dtype=F32)[:, None] * inv[None, :]
    return jnp.cos(ang), jnp.sin(ang)


def _rope_tables_b(s):
    half = QK_ROPE // 2
    inv = ROPE_BASE ** (-jnp.arange(half, dtype=F32) / half)
    ang = jnp.arange(s, dtype=F32)[:, None] * inv[None, :]
    c, sn = jnp.cos(ang), jnp.sin(ang)
    z = jnp.zeros_like(c)
    cos = jnp.concatenate([c, c, z, z], axis=1)
    sa = jnp.concatenate([-sn, z, z, z], axis=1)
    sb = jnp.concatenate([z, sn, z, z], axis=1)
    return cos, sa, sb


def _rope_b(r, cos, sa, sb, sign):
    return r * cos + sign * (pltpu.roll(r, 96, 1) * sa + pltpu.roll(r, 32, 1) * sb)


def _retention_tables():
    lg = jnp.log1p(-jnp.exp2(-5.0 - jnp.arange(H_A, dtype=F32)))
    idx = jnp.arange(CHUNK, dtype=F32)
    diff = idx[:, None] - idx[None, :]
    causal = diff >= 0
    dmat = jnp.where(causal, jnp.exp(jnp.where(causal, diff, 0.0)[None] * lg[:, None, None]), 0.0)
    qdec = jnp.exp((idx + 1.0)[None, :] * lg[:, None])[:, :, None]
    kdec = jnp.exp((CHUNK - 1.0 - idx)[None, :] * lg[:, None])[:, :, None]
    cdec = jnp.broadcast_to(jnp.exp(CHUNK * lg)[:, None, None], (H_A, 1, DV_A))
    return dmat, qdec, kdec, cdec


def _group_norm(o):
    mu = jnp.mean(o, axis=-1, keepdims=True)
    oc = o - mu
    var = jnp.mean(oc * oc, axis=-1, keepdims=True)
    rstd = lax.rsqrt(var + 1e-5)
    return oc * rstd, rstd


Q_COL, K_COL, V_COL, GATE_COL = 0, H_A * DK_A, 2 * H_A * DK_A, 2 * H_A * DK_A + WIDTH_A


def _ln_stats(z):
    mu = jnp.mean(z, axis=-1, keepdims=True)
    zc = z - mu
    var = jnp.mean(zc * zc, axis=-1, keepdims=True)
    rstd = lax.rsqrt(var + 1e-5)
    return zc * rstd, rstd


def _ln_bwd(dy, xhat, rstd, g):
    dxh = dy * g
    m1 = jnp.mean(dxh, axis=-1, keepdims=True)
    m2 = jnp.mean(dxh * xhat, axis=-1, keepdims=True)
    return rstd * (dxh - m1 - xhat * m2)


def _ln_fwd(x, y, g, b, *, ts):
    s = x.shape[0]

    def body(x_ref, y_ref, g_ref, b_ref, o_ref, o16_ref):
        xhat, _ = _ln_stats(ALPHA * x_ref[...] + y_ref[...])
        out = xhat * g_ref[...] + b_ref[...]
        o_ref[...] = out
        o16_ref[...] = out.astype(o16_ref.dtype)

    return pl.pallas_call(
        body, name="ln_fwd", grid=(s // ts,),
        in_specs=[_row_spec(ts, D_MODEL), _row_spec(ts, D_MODEL), _bc_spec((1, D_MODEL)), _bc_spec((1, D_MODEL))],
        out_specs=[_row_spec(ts, D_MODEL), _row_spec(ts, D_MODEL)],
        out_shape=[jax.ShapeDtypeStruct((s, D_MODEL), F32), jax.ShapeDtypeStruct((s, D_MODEL), BF16)],
        compiler_params=_params(("parallel",)),
    )(x, y, g, b)


def _ln_loss_bwd(x1, y, target, g, b, *, ts):
    s = x1.shape[0]
    n = s // ts

    def body(x_ref, y_ref, t_ref, g_ref, b_ref, dz_ref, dz16_ref, dg_ref, db_ref, loss_ref, ag, ab, al):
        i = pl.program_id(0)

        @pl.when(i == 0)
        def _():
            ag[...] = jnp.zeros_like(ag)
            ab[...] = jnp.zeros_like(ab)
            al[...] = jnp.zeros_like(al)

        xhat, rstd = _ln_stats(ALPHA * x_ref[...] + y_ref[...])
        err = xhat * g_ref[...] + b_ref[...] - t_ref[...]
        al[...] += _fold8(err * err)
        dy = err * (1.0 / D_MODEL)
        ag[...] += _fold8(dy * xhat)
        ab[...] += _fold8(dy)
        dz = _ln_bwd(dy, xhat, rstd, g_ref[...])
        dz_ref[...] = dz
        dz16_ref[...] = dz.astype(dz16_ref.dtype)

        @pl.when(i == n - 1)
        def _():
            dg_ref[...] = jnp.sum(ag[...], axis=0, keepdims=True)
            db_ref[...] = jnp.sum(ab[...], axis=0, keepdims=True)
            loss_ref[...] = jnp.full((1, LANES), (0.5 / D_MODEL) * jnp.sum(al[...]), F32)

    return pl.pallas_call(
        body, name="ln_loss_bwd", grid=(n,),
        in_specs=[_row_spec(ts, D_MODEL)] * 3 + [_bc_spec((1, D_MODEL))] * 2,
        out_specs=[_row_spec(ts, D_MODEL), _row_spec(ts, D_MODEL), _bc_spec((1, D_MODEL)), _bc_spec((1, D_MODEL)),
                   _bc_spec((1, LANES))],
        out_shape=[jax.ShapeDtypeStruct((s, D_MODEL), F32), jax.ShapeDtypeStruct((s, D_MODEL), BF16),
                   jax.ShapeDtypeStruct((1, D_MODEL), F32), jax.ShapeDtypeStruct((1, D_MODEL), F32),
                   jax.ShapeDtypeStruct((1, LANES), F32)],
        scratch_shapes=[pltpu.VMEM((8, D_MODEL), F32)] * 3,
        compiler_params=_params(("arbitrary",)),
    )(x1, y, target, g, b)


def _ln_bwd_call(dy, x, y, g, *, ts):
    s = x.shape[0]
    n = s // ts

    def body(dy_ref, x_ref, y_ref, g_ref, dz_ref, dz16_ref, dg_ref, db_ref, ag, ab):
        i = pl.program_id(0)

        @pl.when(i == 0)
        def _():
            ag[...] = jnp.zeros_like(ag)
            ab[...] = jnp.zeros_like(ab)

        xhat, rstd = _ln_stats(ALPHA * x_ref[...] + y_ref[...])
        dy = dy_ref[...]
        ag[...] += _fold8(dy * xhat)
        ab[...] += _fold8(dy)
        dz = _ln_bwd(dy, xhat, rstd, g_ref[...])
        dz_ref[...] = dz
        dz16_ref[...] = dz.astype(dz16_ref.dtype)

        @pl.when(i == n - 1)
        def _():
            dg_ref[...] = jnp.sum(ag[...], axis=0, keepdims=True)
            db_ref[...] = jnp.sum(ab[...], axis=0, keepdims=True)

    return pl.pallas_call(
        body, name="ln_bwd", grid=(n,),
        in_specs=[_row_spec(ts, D_MODEL)] * 3 + [_bc_spec((1, D_MODEL))],
        out_specs=[_row_spec(ts, D_MODEL), _row_spec(ts, D_MODEL), _bc_spec((1, D_MODEL)), _bc_spec((1, D_MODEL))],
        out_shape=[jax.ShapeDtypeStruct((s, D_MODEL), F32), jax.ShapeDtypeStruct((s, D_MODEL), BF16),
                   jax.ShapeDtypeStruct((1, D_MODEL), F32), jax.ShapeDtypeStruct((1, D_MODEL), F32)],
        scratch_shapes=[pltpu.VMEM((8, D_MODEL), F32)] * 2,
        compiler_params=_params(("arbitrary",)),
    )(dy, x, y, g)


def _residual_store(tile, o_ref, dz_ref):
    o_ref[...] = ALPHA * dz_ref[...] + tile


C_LAT = slice(0, KV_LORA)
C_ROPE = slice(KV_LORA, KV_DOWN_PAD)
C_QL = slice(KV_DOWN_PAD, KV_DOWN_PAD + Q_LORA)
C_GATE = slice(KV_DOWN_PAD + Q_LORA, H1_B)


def _rms(x, eps=1e-6):
    r = lax.rsqrt(jnp.mean(x * x, axis=-1, keepdims=True) + eps)
    return x * r, r


def _rms_bwd(dy, xhat, r, g):
    dxh = dy * g
    return r * (dxh - xhat * jnp.mean(dxh * xhat, axis=-1, keepdims=True))


def _kvq_prep(h1, kv_norm, q_norm, cos, sa, sb, *, ts):
    s = h1.shape[0]

    def body(h_ref, kn_ref, qn_ref, c_ref, sa_ref, sb_ref, lat_ref, kr_ref, ql_ref):
        lat, _ = _rms(h_ref[:, C_LAT])
        lat_ref[...] = (lat * kn_ref[...]).astype(lat_ref.dtype)
        kr_ref[...] = _rope_b(h_ref[:, C_ROPE], c_ref[...], sa_ref[...], sb_ref[...], 1.0).astype(kr_ref.dtype)
        ql, _ = _rms(h_ref[:, C_QL])
        ql_ref[...] = (ql * qn_ref[...]).astype(ql_ref.dtype)

    return pl.pallas_call(
        body, name="kvq_prep", grid=(s // ts,),
        in_specs=[_row_spec(ts, H1_B), _bc_spec((1, KV_LORA)), _bc_spec((1, Q_LORA))] + [_row_spec(ts, 128)] * 3,
        out_specs=[_row_spec(ts, KV_LORA), _row_spec(ts, 128), _row_spec(ts, Q_LORA)],
        out_shape=[jax.ShapeDtypeStruct((s, KV_LORA), BF16), jax.ShapeDtypeStruct((s, 128), BF16),
                   jax.ShapeDtypeStruct((s, Q_LORA), BF16)],
        compiler_params=_params(("parallel",)),
    )(h1, kv_norm, q_norm, cos, sa, sb)


LOG2E = 1.4426950408889634
LN2 = 0.6931471805599453
Q_SCALE = ATT_SCALE * LOG2E


def _rope_q_store(tile, o_ref, c_ref, sa_ref, sb_ref):
    c, a, b = c_ref[...], sa_ref[...], sb_ref[...]
    for hd in range(tile.shape[1] // QK_PAD):
        lo = hd * QK_PAD
        o_ref[:, lo:lo + 128] = (tile[:, lo:lo + 128] * Q_SCALE).astype(o_ref.dtype)
        o_ref[:, lo + 128:lo + 256] = (_rope_b(tile[:, lo + 128:lo + 256], c, a, b, 1.0) * Q_SCALE).astype(o_ref.dtype)


def _assemble_k_store(tile, o_ref, kr_ref):
    r = kr_ref[...]
    for hd in range(tile.shape[1] // QK_NOPE):
        o_ref[:, hd * QK_PAD:hd * QK_PAD + 128] = tile[:, hd * 128:(hd + 1) * 128].astype(o_ref.dtype)
        o_ref[:, hd * QK_PAD + 128:(hd + 1) * QK_PAD] = r


def _h1_bwd(h1, dlat_k, dlat_v, dkr_heads, dqn, dg16, kv_norm, q_norm, cos, sa, sb, *, ts):
    s = h1.shape[0]
    n = s // ts

    def body(h_ref, dk_ref, dv_ref, dkr_ref, dqn_ref, dg_ref, kn_ref, qn_ref, c_ref, sa_ref, sb_ref,
             o_ref, dkn_ref, dqn_out_ref, akn, aqn):
        i = pl.program_id(0)

        @pl.when(i == 0)
        def _():
            akn[...] = jnp.zeros_like(akn)
            aqn[...] = jnp.zeros_like(aqn)

        lat, r = _rms(h_ref[:, C_LAT])
        dlat = dk_ref[...] + dv_ref[...]
        akn[...] += _fold8(dlat * lat)
        o_ref[:, C_LAT] = _rms_bwd(dlat, lat, r, kn_ref[...]).astype(o_ref.dtype)

        dkr = dkr_ref[:, 0:128]
        for hd in range(1, H_B):
            dkr = dkr + dkr_ref[:, hd * 128:(hd + 1) * 128]
        o_ref[:, C_ROPE] = _rope_b(dkr, c_ref[...], sa_ref[...], sb_ref[...], -1.0).astype(o_ref.dtype)

        ql, rq = _rms(h_ref[:, C_QL])
        dq = dqn_ref[...]
        aqn[...] += _fold8(dq * ql)
        o_ref[:, C_QL] = _rms_bwd(dq, ql, rq, qn_ref[...]).astype(o_ref.dtype)
        o_ref[:, C_GATE] = dg_ref[...]

        @pl.when(i == n - 1)
        def _():
            dkn_ref[...] = jnp.sum(akn[...], axis=0, keepdims=True)
            dqn_out_ref[...] = jnp.sum(aqn[...], axis=0, keepdims=True)

    return pl.pallas_call(
        body, name="h1_bwd", grid=(n,),
        in_specs=[_row_spec(ts, H1_B), _row_spec(ts, KV_LORA), _row_spec(ts, KV_LORA), _row_spec(ts, H_B * 128),
                  _row_spec(ts, Q_LORA), _row_spec(ts, WIDTH_B), _bc_spec((1, KV_LORA)), _bc_spec((1, Q_LORA))]
        + [_row_spec(ts, 128)] * 3,
        out_specs=[_row_spec(ts, H1_B), _bc_spec((1, KV_LORA)), _bc_spec((1, Q_LORA))],
        out_shape=[jax.ShapeDtypeStruct((s, H1_B), BF16), jax.ShapeDtypeStruct((1, KV_LORA), F32),
                   jax.ShapeDtypeStruct((1, Q_LORA), F32)],
        scratch_shapes=[pltpu.VMEM((8, KV_LORA), F32), pltpu.VMEM((8, Q_LORA), F32)],
        compiler_params=_params(("arbitrary",)),
    )(h1, dlat_k, dlat_v, dkr_heads, dqn, dg16, kv_norm, q_norm, cos, sa, sb)


def _dot(a, b, ca, cb):
    return lax.dot_general(a, b, (((ca,), (cb,)), ((), ())), preferred_element_type=F32)


def _table_specs():
    full = lambda shape: pl.BlockSpec(shape, lambda i: (0,) * len(shape))
    return [full((H_A, CHUNK, CHUNK)), full((H_A, CHUNK, 1)), full((H_A, CHUNK, 1)), full((H_A, 1, DV_A))]


def _retention_fwd(h_a, cos, sin, tables):
    s = h_a.shape[0]
    n = s // CHUNK

    def body(h_ref, c_ref, s_ref, dm_ref, qd_ref, kd_ref, cd_ref, q_ref, k_ref, v_ref, o_ref, u_ref, st_ref, state):
        @pl.when(pl.program_id(0) == 0)
        def _():
            state[...] = jnp.zeros_like(state)

        c, sn = c_ref[...], s_ref[...]
        for hd in range(H_A):
            qs, vs = slice(hd * DK_A, (hd + 1) * DK_A), slice(hd * DV_A, (hd + 1) * DV_A)
            for r_ref, base, scale in ((q_ref, Q_COL, 1.0), (k_ref, K_COL, DK_A ** -0.5)):
                lo = base + hd * DK_A
                x1, x2 = h_ref[:, lo:lo + 128], h_ref[:, lo + 128:lo + 256]
                r_ref[:, hd * DK_A:hd * DK_A + 128] = ((x1 * c - x2 * sn) * scale).astype(r_ref.dtype)
                r_ref[:, hd * DK_A + 128:(hd + 1) * DK_A] = ((x2 * c + x1 * sn) * scale).astype(r_ref.dtype)
            v_ref[:, vs] = h_ref[:, V_COL + hd * DV_A:V_COL + (hd + 1) * DV_A].astype(v_ref.dtype)
            qv, kv, vv = q_ref[:, qs], k_ref[:, qs], v_ref[:, vs]
            st = state[hd]
            st16 = st.astype(BF16)
            st_ref[0, hd] = st16
            scores = _dot(qv, kv, 1, 1) * dm_ref[hd]
            qd = (qv.astype(F32) * qd_ref[hd]).astype(BF16)
            o = _dot(scores.astype(BF16), vv, 1, 0) + _dot(qd, st16, 1, 0)
            o_ref[:, vs] = o
            kd = (kv.astype(F32) * kd_ref[hd]).astype(BF16)
            state[hd] = st * cd_ref[hd] + _dot(kd, vv, 0, 0)
            on, _ = _group_norm(o)
            g = h_ref[:, GATE_COL + hd * DV_A:GATE_COL + (hd + 1) * DV_A]
            u_ref[:, vs] = (on * (g * _sigmoid(g))).astype(u_ref.dtype)

    row = lambda w: pl.BlockSpec((CHUNK, w), lambda i: (i, 0))
    return pl.pallas_call(
        body, name="retention_fwd", grid=(n,),
        in_specs=[row(IN_A), row(128), row(128)] + _table_specs(),
        out_specs=[row(H_A * DK_A), row(H_A * DK_A), row(WIDTH_A), row(WIDTH_A), row(WIDTH_A),
                   pl.BlockSpec((1, H_A, DK_A, DV_A), lambda i: (i, 0, 0, 0))],
        out_shape=[jax.ShapeDtypeStruct((s, H_A * DK_A), BF16), jax.ShapeDtypeStruct((s, H_A * DK_A), BF16),
                   jax.ShapeDtypeStruct((s, WIDTH_A), BF16), jax.ShapeDtypeStruct((s, WIDTH_A), F32),
                   jax.ShapeDtypeStruct((s, WIDTH_A), BF16), jax.ShapeDtypeStruct((n, H_A, DK_A, DV_A), BF16)],
        scratch_shapes=[pltpu.VMEM((H_A, DK_A, DV_A), F32)],
        compiler_params=_params(("arbitrary",)),
    )(h_a, cos, sin, *tables)


def _retention_bwd(q, k, v, states, du, o, h_a, cos, sin, tables):
    s = q.shape[0]
    n = s // CHUNK

    def body(q_ref, k_ref, v_ref, st_ref, du_ref, o_ref, g_ref, c_ref, s_ref, dm_ref, qd_ref, kd_ref, cd_ref, dh_ref, grad_state):
        @pl.when(pl.program_id(0) == 0)
        def _():
            grad_state[...] = jnp.zeros_like(grad_state)

        c, sn = c_ref[...], s_ref[...]
        for hd in range(H_A):
            qs, vs = slice(hd * DK_A, (hd + 1) * DK_A), slice(hd * DV_A, (hd + 1) * DV_A)
            on, rstd = _group_norm(o_ref[:, vs])
            g = g_ref[:, vs]
            sg = _sigmoid(g)
            du_v = du_ref[:, vs]
            don = du_v * (g * sg)
            dh_ref[:, GATE_COL + hd * DV_A:GATE_COL + (hd + 1) * DV_A] = (du_v * on * (sg * (1.0 + g * (1.0 - sg)))).astype(dh_ref.dtype)
            m1 = jnp.mean(don, axis=-1, keepdims=True)
            m2 = jnp.mean(don * on, axis=-1, keepdims=True)
            dov = (rstd * (don - m1 - on * m2)).astype(BF16)

            qv, kv, vv = q_ref[:, qs], k_ref[:, qs], v_ref[:, vs]
            dm = dm_ref[hd]
            gs = grad_state[hd]
            g16 = gs.astype(BF16)
            scores = (_dot(qv, kv, 1, 1) * dm).astype(BF16)
            dscores = (_dot(dov, vv, 1, 1) * dm).astype(BF16)
            qd = (qv.astype(F32) * qd_ref[hd]).astype(BF16)
            kd = (kv.astype(F32) * kd_ref[hd]).astype(BF16)
            dq = _dot(dscores, kv, 1, 0) + _dot(dov, st_ref[0, hd], 1, 1) * qd_ref[hd]
            dk = (_dot(dscores, qv, 0, 0) + _dot(vv, g16, 1, 1) * kd_ref[hd]) * (DK_A ** -0.5)
            dh_ref[:, V_COL + hd * DV_A:V_COL + (hd + 1) * DV_A] = (_dot(scores, dov, 0, 0) + _dot(kd, g16, 1, 0)).astype(dh_ref.dtype)
            grad_state[hd] = gs * cd_ref[hd] + _dot(qd, dov, 0, 0)
            for d, base in ((dq, Q_COL), (dk, K_COL)):
                lo = base + hd * DK_A
                d1, d2 = d[:, 0:128], d[:, 128:256]
                dh_ref[:, lo:lo + 128] = (d1 * c + d2 * sn).astype(dh_ref.dtype)
                dh_ref[:, lo + 128:lo + 256] = (d2 * c - d1 * sn).astype(dh_ref.dtype)

    rev = lambda i: n - 1 - i
    row = lambda w, col=0: pl.BlockSpec((CHUNK, w), lambda i: (rev(i), col))
    return pl.pallas_call(
        body, name="retention_bwd", grid=(n,),
        in_specs=[row(H_A * DK_A), row(H_A * DK_A), row(WIDTH_A), pl.BlockSpec((1, H_A, DK_A, DV_A), lambda i: (rev(i), 0, 0, 0)),
                  row(WIDTH_A), row(WIDTH_A), row(WIDTH_A, GATE_COL // WIDTH_A), row(128), row(128)] + _table_specs(),
        out_specs=row(IN_A),
        out_shape=jax.ShapeDtypeStruct((s, IN_A), BF16),
        scratch_shapes=[pltpu.VMEM((H_A, DK_A, DV_A), F32)],
        compiler_params=_params(("arbitrary",)),
    )(q, k, v, states, du, o, h_a, cos, sin, *tables)


GATE_BLOCK0 = (KV_DOWN_PAD + Q_LORA) // LANES


def _causal_mask(sc, row0):
    row = lax.broadcasted_iota(jnp.int32, sc.shape, 0) + row0
    col = lax.broadcasted_iota(jnp.int32, sc.shape, 1)
    return jnp.where(col <= row, sc, NEG_BIG)


def _key_block_loop(step, i, per_trip=2):
    def trip_body(jj, carry):
        for t in range(per_trip):
            step(per_trip * jj + t, False)
        return carry

    lax.fori_loop(0, i // per_trip, trip_body, 0)
    group = per_trip // 2
    while group >= 1:
        def tail(group=group):
            first = (i // (2 * group)) * (2 * group)
            for t in range(group):
                step(first + t, False)

        pl.when((i // group) % 2 == 1)(tail)
        group //= 2
    step(i, True)


def _attention_fwd(q, k, v, h1, *, blk, sub):
    s = q.shape[0]
    nb = s // blk
    reps = blk // LANES

    def body(q_ref, k_ref, v_ref, g_ref, o_ref, u_ref, lse_ref, vext_s, m_s, acc_s):
        i = pl.program_id(1)

        @pl.when(i == 0)
        def _():
            vext_s[:, 0:V_HEAD] = v_ref[...]
            vext_s[:, V_HEAD:2 * V_HEAD] = jnp.ones((s, V_HEAD), vext_s.dtype)

        m_s[...] = jnp.full_like(m_s, NEG_BIG)
        acc_s[...] = jnp.zeros_like(acc_s)

        def step(j, diagonal):
            kv_rows = pl.ds(pl.multiple_of(j * blk, blk), blk)
            kb = k_ref[kv_rows, :]
            vb = vext_s[kv_rows, :]
            for r in range(blk // sub):
                rows = slice(r * sub, (r + 1) * sub)
                ncols = (r + 1) * sub if diagonal else blk
                sc = _dot(q_ref[rows, :], kb[0:ncols], 1, 1)
                if diagonal:
                    sc = _causal_mask(sc, r * sub)
                m_prev = m_s[rows, :]
                m_new = jnp.maximum(m_prev, jnp.max(sc, axis=-1, keepdims=True))
                p = jnp.exp2(sc - jnp.tile(m_new, (1, ncols // LANES)))
                a = jnp.exp2(m_prev - m_new)
                acc_s[rows, :] = jnp.tile(a, (1, 2)) * acc_s[rows, :] + _dot(p.astype(BF16), vb[0:ncols], 1, 0)
                m_s[rows, :] = m_new

        _key_block_loop(step, i, per_trip=8)
        acc = acc_s[...]
        l = acc[:, V_HEAD:2 * V_HEAD]
        o = acc[:, 0:V_HEAD] / l
        g = g_ref[...]
        o_ref[...] = o
        u_ref[...] = (o * (g * _sigmoid(g))).astype(u_ref.dtype)
        lse_ref[0] = (m_s[...] + jnp.log2(l))[:, 0:1]

    blk_idx = lambda h, i: (i, h)
    return pl.pallas_call(
        body, name="attention_fwd", grid=(H_B, nb),
        in_specs=[pl.BlockSpec((blk, QK_PAD), blk_idx), pl.BlockSpec((s, QK_PAD), lambda h, i: (0, h)),
                  pl.BlockSpec((s, V_HEAD), lambda h, i: (0, h)), pl.BlockSpec((blk, LANES), lambda h, i: (i, GATE_BLOCK0 + h))],
        out_specs=[pl.BlockSpec((blk, V_HEAD), blk_idx), pl.BlockSpec((blk, V_HEAD), blk_idx),
                   pl.BlockSpec((1, blk, 1), lambda h, i: (h, i, 0))],
        out_shape=[jax.ShapeDtypeStruct((s, WIDTH_B), F32), jax.ShapeDtypeStruct((s, WIDTH_B), BF16),
                   jax.ShapeDtypeStruct((H_B, s, 1), F32)],
        scratch_shapes=[pltpu.VMEM((s, 2 * V_HEAD), BF16), pltpu.VMEM((blk, LANES), F32), pltpu.VMEM((blk, 2 * V_HEAD), F32)],
        compiler_params=_params(("parallel", "arbitrary")),
    )(q, k, v, h1)


def _attention_bwd(q, k, v, du, o, h1, lse, cos, sa, sb, *, blk):
    s = q.shape[0]
    nb = s // blk
    reps = blk // LANES

    def body(q_ref, k_ref, v_ref, du_ref, o_ref, g_ref, lse_ref, c_ref, sa_ref, sb_ref,
             dq_ref, dkn_ref, dkr_ref, dv_ref, dg_ref, lse_s, dl_s, do_s, dq_s):
        i = pl.program_id(1)
        g = g_ref[...]
        sg = _sigmoid(g)
        du_v, ov = du_ref[...], o_ref[...]
        do = du_v * (g * sg)
        dg_ref[...] = (du_v * ov * (sg * (1.0 + g * (1.0 - sg)))).astype(dg_ref.dtype)
        do_s[...] = do.astype(do_s.dtype)
        dl_s[...] = jnp.broadcast_to(jnp.sum(do * ov, axis=-1, keepdims=True), (blk, LANES))
        lse_s[...] = jnp.broadcast_to(lse_ref[0], (blk, LANES))
        dq_s[...] = jnp.zeros_like(dq_s)

        def step(j, diagonal):
            kv_rows = pl.ds(pl.multiple_of(j * blk, blk), blk)
            qv, dov, kb = q_ref[...], do_s[...], k_ref[kv_rows, :]
            sc = _dot(qv, kb, 1, 1)
            if diagonal:
                sc = _causal_mask(sc, 0)
            p = jnp.exp2(sc - jnp.tile(lse_s[...], (1, reps)))
            dp = _dot(dov, v_ref[kv_rows, :], 1, 1)
            ds = (p * (dp - jnp.tile(dl_s[...], (1, reps)))).astype(BF16)
            dv_c = _dot(p.astype(BF16), dov, 0, 0)
            dk_c = _dot(ds, qv, 0, 0)
            dq_s[...] += _dot(ds, kb, 1, 0)
            if diagonal:
                dkn_ref[kv_rows, :] = dk_c[:, 0:128]
                dkr_ref[kv_rows, :] = dk_c[:, 128:256]
                dv_ref[kv_rows, :] = dv_c
            else:
                dkn_ref[kv_rows, :] += dk_c[:, 0:128]
                dkr_ref[kv_rows, :] += dk_c[:, 128:256]
                dv_ref[kv_rows, :] += dv_c

        _key_block_loop(step, i, per_trip=4)
        dq = dq_s[...] * ATT_SCALE
        dq_ref[:, 0:128] = dq[:, 0:128].astype(dq_ref.dtype)
        dq_ref[:, 128:256] = _rope_b(dq[:, 128:256], c_ref[...], sa_ref[...], sb_ref[...], -1.0).astype(dq_ref.dtype)

        @pl.when(i == nb - 1)
        def _():
            dkn_ref[...] = dkn_ref[...] * LN2
            dkr_ref[...] = dkr_ref[...] * LN2

    head = lambda h, i: (0, h)
    blk_idx = lambda h, i: (i, h)
    row_idx = lambda h, i: (i, 0)
    return pl.pallas_call(
        body, name="attention_bwd", grid=(H_B, nb),
        in_specs=[pl.BlockSpec((blk, QK_PAD), blk_idx), pl.BlockSpec((s, QK_PAD), head), pl.BlockSpec((s, V_HEAD), head),
                  pl.BlockSpec((blk, V_HEAD), blk_idx), pl.BlockSpec((blk, V_HEAD), blk_idx),
                  pl.BlockSpec((blk, LANES), lambda h, i: (i, GATE_BLOCK0 + h)), pl.BlockSpec((1, blk, 1), lambda h, i: (h, i, 0)),
                  pl.BlockSpec((blk, LANES), row_idx), pl.BlockSpec((blk, LANES), row_idx), pl.BlockSpec((blk, LANES), row_idx)],
        out_specs=[pl.BlockSpec((blk, QK_PAD), blk_idx), pl.BlockSpec((s, 128), head), pl.BlockSpec((s, 128), head),
                   pl.BlockSpec((s, 128), head), pl.BlockSpec((blk, V_HEAD), blk_idx)],
        out_shape=[jax.ShapeDtypeStruct((s, H_B * QK_PAD), BF16), jax.ShapeDtypeStruct((s, H_B * 128), F32),
                   jax.ShapeDtypeStruct((s, H_B * 128), F32), jax.ShapeDtypeStruct((s, H_B * 128), F32),
                   jax.ShapeDtypeStruct((s, WIDTH_B), BF16)],
        scratch_shapes=[pltpu.VMEM((blk, LANES), F32), pltpu.VMEM((blk, LANES), F32), pltpu.VMEM((blk, V_HEAD), BF16),
                        pltpu.VMEM((blk, QK_PAD), F32)],
        compiler_params=_params(("parallel", "arbitrary")),
    )(q, k, v, du, o, h1, lse, cos, sa, sb)


def _local_step(x, target, w, kv_norm, q_norm, ln_g, ln_b, *, ts=256, blk=512, late=None, reduce=None):
    s = x.shape[0]
    cos_a, sin_a = _rope_tables_a(s)
    cos_b, sa_b, sb_b = _rope_tables_b(s)
    tables = _retention_tables()
    g0, g1, b0, b1 = ln_g[0:1], ln_g[1:2], ln_b[0:1], ln_b[1:2]

    x16 = x.astype(BF16)
    if late is None:
        h_a = _mm(x16, w["a_in"], tn=1536, name="a_in_fwd")
    else:
        flat_b, chip = late
        h_a, got = _mm(x16, w["a_in"], tn=1536, name="a_in_fwd", side=_gather_side(flat_b))
        got = lax.dynamic_update_slice(got, flat_b[None], (chip, 0, 0))
        w = {**w, **_kernel_layout_b(_full_from_gathered(got, B_SHARDS))}
    q_a, k_a, v_a, o_a, u_a, states = _retention_fwd(h_a, cos_a, sin_a, tables)
    y_a = _mm(u_a, w["a_out"], tn=1024, name="a_out_fwd")
    x1, x1_16 = _ln_fwd(x, y_a, g0, b0, ts=ts)

    h1 = _mm(x1_16, w["b_in1"], tn=1152, name="b_in_fwd")
    lat16, kr16, qn16 = _kvq_prep(h1, kv_norm, q_norm, cos_b, sa_b, sb_b, ts=ts)
    k16 = _mm(lat16, w["up_k"], out_dtype=BF16, tn=2048, out_tn=H_B * QK_PAD, extras=(kr16,), epilogue=_assemble_k_store, name="up_k_fwd")
    v16 = _mm(lat16, w["up_v"], out_dtype=BF16, tn=2048, name="up_v_fwd")
    q16 = _mm(qn16, w["uq"], out_dtype=BF16, tn=2048, extras=(cos_b, sa_b, sb_b), epilogue=_rope_q_store, name="uq_fwd")
    o_b, u_b, lse = _attention_fwd(q16, k16, v16, h1, blk=blk, sub=blk // 2)
    y_b = _mm(u_b, w["b_out"], tn=1024, name="b_out_fwd")

    dz_b, dz_b16, dg1, db1, loss = _ln_loss_bwd(x1, y_b, target, g1, b1, ts=ts)
    d_b_out = _mm(u_b, dz_b16, ta=True, tn=1024, tk=1024, name="b_out_dw")
    du_b = _mm(dz_b16, w["b_out"], tb=True, tn=1024, name="b_out_dx")
    dqf16, dkn, dkr_heads, dv, dgate16 = _attention_bwd(q16, k16, v16, du_b, o_b, h1, lse, cos_b, sa_b, sb_b, blk=blk)
    d_uq = _mm(qn16, dqf16, ta=True, tm=768, tn=2048, tk=1024, name="uq_dw")
    dqn = _mm(dqf16, w["uq"], tb=True, tn=768, tk=2048, name="uq_dx")
    d_up_k = _mm(lat16, dkn, ta=True, tn=2048, tk=1024, name="up_k_dw")
    d_up_v = _mm(lat16, dv, ta=True, tn=2048, tk=1024, name="up_v_dw")
    dlat_k = _mm(dkn, w["up_k"], tb=True, tn=512, name="up_k_dx")
    dlat_v = _mm(dv, w["up_v"], tb=True, tn=512, name="up_v_dx")
    dh1, dkvn, dqnorm = _h1_bwd(h1, dlat_k, dlat_v, dkr_heads, dqn, dgate16, kv_norm, q_norm, cos_b, sa_b, sb_b, ts=ts)
    d_b_in1 = _mm(x1_16, dh1, ta=True, tn=1152, tk=1024, name="b_in_dw")
    dx1 = _mm(dh1, w["b_in1"], tb=True, tn=1024, extras=(dz_b,), epilogue=_residual_store, name="b_in_dx")

    dz_a, dz_a16, dg0, db0 = _ln_bwd_call(dx1, x, y_a, g0, ts=ts)
    d_a_out = _mm(u_a, dz_a16, ta=True, tn=1024, tk=1024, name="a_out_dw")
    du_a = _mm(dz_a16, w["a_out"], tb=True, tn=1024, name="a_out_dx")
    dh_a = _retention_bwd(q_a, k_a, v_a, states, du_a, o_a, h_a, cos_a, sin_a, tables)
    grads = dict(a_out=d_a_out, b_in1=d_b_in1, uq=d_uq, b_out=d_b_out, up_k=d_up_k, up_v=d_up_v)
    small = dict(ln_g=jnp.concatenate([dg0, dg1], axis=0), ln_b=jnp.concatenate([db0, db1], axis=0),
                 q_norm=dqnorm, kv_norm=dkvn)
    if reduce is None:
        grads["a_in"] = _mm(x16, dh_a, ta=True, tn=1536, tk=1024, name="a_in_dw")
        grad_x = _mm(dh_a, w["a_in"], tb=True, tn=1024, tk=2048, extras=(dz_a,), epilogue=_residual_store, name="a_in_dx")
        return loss, grad_x, grads, small
    own_early, travel_early = reduce(_reference_layout_grads(grads), EARLY_SHARDS, EARLY_ROWS, "early")
    d_a_in, got_early = _mm(x16, dh_a, ta=True, tn=1536, tk=1024, name="a_in_dw", side=_chip_exchange_side(travel_early))
    own_late, travel_late = reduce(dict(a_w_in=d_a_in), LATE_SHARDS, LATE_ROWS, "late")
    grad_x, got_late = _mm(dh_a, w["a_in"], tb=True, tn=1024, tk=2048, extras=(dz_a,), epilogue=_residual_store, name="a_in_dx",
                           side=_chip_exchange_side(travel_late))
    return loss, grad_x, ((own_early, got_early), (own_late, got_late)), small


def _flat_shards(shards, dtype):
    parts = [shards[name].reshape(rows, FLAT_COLS) for name, rows in SHARD_ROWS]
    used = sum(rows for _, rows in SHARD_ROWS)
    parts.append(jnp.zeros((FLAT_ROWS - used, FLAT_COLS), shards["a_w_in"].dtype))
    return jnp.concatenate(parts, axis=0).astype(dtype)


def _unflat_shards(flat, shapes, shards):
    out, off = {}, 0
    for name, rows in shards:
        out[name] = flat[off:off + rows].reshape(shapes[name])
        off += rows
    return out


COL_SHARDED = {"a_w_in": (D_MODEL, IN_A), "b_w_in": (D_MODEL, IN_B), "b_w_uq": (Q_LORA, H_B * (QK_NOPE + QK_ROPE)),
               "kv_w_up": (KV_LORA, H_B * (QK_NOPE + V_HEAD))}
ROW_SHARDED = {"a_w_out": (WIDTH_A, D_MODEL), "b_w_out": (WIDTH_B, D_MODEL), "kv_w_down": (D_MODEL, KV_LORA + QK_ROPE)}


def _full_from_gathered(gathered, shards=SHARD_ROWS):
    out, off = {}, 0
    for name, rows in shards:
        part = gathered[:, off:off + rows]
        off += rows
        if name in COL_SHARDED:
            r, c = COL_SHARDED[name]
            out[name] = part.reshape(N_CHIPS, r, c // N_CHIPS).transpose(1, 0, 2).reshape(r, c)
        else:
            r, c = ROW_SHARDED[name]
            out[name] = part.reshape(r, c)
    return out


def _gathered_from_full(full, shards, total_rows):
    parts = []
    for name, rows in shards:
        g = full[name]
        if name in COL_SHARDED:
            r, c = COL_SHARDED[name]
            g = g.reshape(r, N_CHIPS, c // N_CHIPS).transpose(1, 0, 2)
        parts.append(g.reshape(N_CHIPS, rows, FLAT_COLS))
    used = sum(rows for _, rows in shards)
    if total_rows > used:
        parts.append(jnp.zeros((N_CHIPS, total_rows - used, FLAT_COLS), F32))
    return jnp.concatenate(parts, axis=1)


A_SHARDS, B_SHARDS = SHARD_ROWS[:2], SHARD_ROWS[2:]
A_ROWS = sum(rows for _, rows in A_SHARDS)


def _kernel_layout_a(full):
    return dict(a_in=full["a_w_in"], a_out=full["a_w_out"])


def _kernel_layout_b(full):
    uq = full["b_w_uq"].reshape(Q_LORA, H_B, QK_NOPE + QK_ROPE)
    uq = jnp.pad(uq, ((0, 0), (0, 0), (0, QK_PAD - QK_NOPE - QK_ROPE))).reshape(Q_LORA, H_B * QK_PAD)
    up = full["kv_w_up"].reshape(KV_LORA, H_B, QK_NOPE + V_HEAD)
    down = jnp.pad(full["kv_w_down"], ((0, 0), (0, KV_DOWN_PAD - KV_LORA - QK_ROPE)))
    return dict(b_out=full["b_w_out"], uq=uq,
                up_k=up[:, :, :QK_NOPE].reshape(KV_LORA, H_B * QK_NOPE),
                up_v=up[:, :, QK_NOPE:].reshape(KV_LORA, H_B * V_HEAD),
                b_in1=jnp.concatenate([down, full["b_w_in"]], axis=1))


def _kernel_layout(full):
    return {**_kernel_layout_a(full), **_kernel_layout_b(full)}


EARLY_SHARDS = tuple(sh for sh in SHARD_ROWS if sh[0] != "a_w_in")
LATE_SHARDS = tuple(sh for sh in SHARD_ROWS if sh[0] == "a_w_in")
EARLY_ROWS, LATE_ROWS = 3072, 1536


def _reference_layout_grads(g):
    uq = g["uq"].reshape(Q_LORA, H_B, QK_PAD)[:, :, :QK_NOPE + QK_ROPE].reshape(Q_LORA, H_B * (QK_NOPE + QK_ROPE))
    up = jnp.concatenate([g["up_k"].reshape(KV_LORA, H_B, QK_NOPE), g["up_v"].reshape(KV_LORA, H_B, V_HEAD)], axis=2)
    return dict(a_w_out=g["a_out"], b_w_out=g["b_out"], b_w_uq=uq,
                kv_w_up=up.reshape(KV_LORA, H_B * (QK_NOPE + V_HEAD)),
                kv_w_down=g["b_in1"][:, :KV_LORA + QK_ROPE], b_w_in=g["b_in1"][:, KV_DOWN_PAD:])


HBM_SPEC = pl.BlockSpec(memory_space=pl.ANY)


def _me():
    return lax.axis_index("x"), lax.axis_index("y"), lax.axis_index("c")


def _chip_flips(x, y):
    return [(1 - x, y), (x, 1 - y), (1 - x, 1 - y)]


def _gather_copies(src_ref, out_ref, send_sems, recv_sems):
    x, y, c = _me()
    half = src_ref.shape[0] // 2
    my_rows = pl.ds(pl.multiple_of(c * half, 16), half)
    their_rows = pl.ds(pl.multiple_of((1 - c) * half, 16), half)
    chips = _chip_flips(x, y)
    sibling = (x, y, 1 - c)

    def copy(k, src, dst, to):
        return pltpu.make_async_remote_copy(src_ref=src, dst_ref=dst, send_sem=send_sems.at[k], recv_sem=recv_sems.at[k],
                                            device_id=to, device_id_type=MESH)

    sends = [copy(k, src_ref.at[my_rows, :], out_ref.at[2 * x + y, my_rows, :], (px, py, c)) for k, (px, py) in enumerate(chips)]
    landed = [out_ref.at[2 * px + py, my_rows, :] for px, py in chips]
    lands = [copy(k, landed[k], landed[k], (px, py, c)) for k, (px, py) in enumerate(chips)]
    forwards = [copy(3 + k, landed[k], landed[k], sibling) for k in range(3)]
    theirs = [out_ref.at[2 * px + py, their_rows, :] for px, py in chips]
    arrivals = [copy(3 + k, theirs[k], theirs[k], sibling) for k in range(3)]
    return sends, lands, forwards, arrivals


def _gather_start(src_ref, out_ref, send_sems, recv_sems):
    sends, _, _, _ = _gather_copies(src_ref, out_ref, send_sems, recv_sems)
    for cp in sends:
        cp.start()


def _gather_finish(src_ref, out_ref, send_sems, recv_sems):
    sends, lands, forwards, arrivals = _gather_copies(src_ref, out_ref, send_sems, recv_sems)
    for k in range(3):
        lands[k].wait_recv()
        forwards[k].start()
    for cp in arrivals:
        cp.wait_recv()
    for cp in sends + forwards:
        cp.wait_send()


def _gather_scratch():
    return [pltpu.SemaphoreType.DMA((6,)), pltpu.SemaphoreType.DMA((6,))]


def _gather_weights(flat16):
    def body(src_ref, out_ref, send_sems, recv_sems):
        _gather_start(src_ref, out_ref, send_sems, recv_sems)
        _gather_finish(src_ref, out_ref, send_sems, recv_sems)

    return pl.pallas_call(
        body, name="gather_weights",
        in_specs=[HBM_SPEC], out_specs=HBM_SPEC,
        out_shape=jax.ShapeDtypeStruct((N_CHIPS,) + flat16.shape, flat16.dtype),
        scratch_shapes=_gather_scratch(),
    )(flat16)


def _gather_side(flat16):
    return dict(inputs=[flat16], out_shape=jax.ShapeDtypeStruct((N_CHIPS,) + flat16.shape, flat16.dtype),
                scratch=_gather_scratch(), start=_gather_start, finish=_gather_finish)


def _chip_exchange_copies(p_ref, out_ref, send_sems, recv_sems):
    x, y, c = _me()
    return [pltpu.make_async_remote_copy(
        src_ref=p_ref.at[2 * px + py], dst_ref=out_ref.at[k], send_sem=send_sems.at[k], recv_sem=recv_sems.at[k],
        device_id=(px, py, c), device_id_type=MESH) for k, (px, py) in enumerate(_chip_flips(x, y))]


def _chip_exchange_start(p_ref, out_ref, send_sems, recv_sems):
    for cp in _chip_exchange_copies(p_ref, out_ref, send_sems, recv_sems):
        cp.start()


def _chip_exchange_finish(p_ref, out_ref, send_sems, recv_sems):
    copies = _chip_exchange_copies(p_ref, out_ref, send_sems, recv_sems)
    for cp in copies:
        cp.wait_send()
    for cp in copies:
        cp.wait_recv()


def _chip_exchange_side(p):
    return dict(inputs=[p], out_shape=jax.ShapeDtypeStruct((3,) + p.shape[1:], p.dtype),
                scratch=[pltpu.SemaphoreType.DMA((3,)), pltpu.SemaphoreType.DMA((3,))],
                start=_chip_exchange_start, finish=_chip_exchange_finish)


def _pair_swap(r, name):
    def body(r_ref, out_ref, send_sem, recv_sem):
        x, y, c = _me()
        cp = pltpu.make_async_remote_copy(src_ref=r_ref, dst_ref=out_ref, send_sem=send_sem, recv_sem=recv_sem,
                                          device_id=(x, y, 1 - c), device_id_type=MESH)
        cp.start()
        cp.wait_send()
        cp.wait_recv()

    return pl.pallas_call(
        body, name=name,
        in_specs=[HBM_SPEC], out_specs=HBM_SPEC,
        out_shape=jax.ShapeDtypeStruct(r.shape, r.dtype),
        scratch_shapes=[pltpu.SemaphoreType.DMA, pltpu.SemaphoreType.DMA],
    )(r)


def _sum_small(vec):
    def body(v_ref, out_ref, slots, send_sems, recv_sems):
        x, y, c = _me()
        me = 4 * x + 2 * y + c
        slots[me] = v_ref[...]
        flips = [(fx, fy, fc) for fx in (0, 1) for fy in (0, 1) for fc in (0, 1)][1:]
        copies = []
        for k, (fx, fy, fc) in enumerate(flips):
            copies.append(pltpu.make_async_remote_copy(
                src_ref=v_ref, dst_ref=slots.at[me], send_sem=send_sems.at[k], recv_sem=recv_sems.at[k],
                device_id=(x ^ fx, y ^ fy, c ^ fc), device_id_type=MESH))
        for cp in copies:
            cp.start()
        for cp in copies:
            cp.wait_send()
        for k, (fx, fy, fc) in enumerate(flips):
            src = 4 * (x ^ fx) + 2 * (y ^ fy) + (c ^ fc)
            pltpu.make_async_remote_copy(
                src_ref=v_ref, dst_ref=slots.at[src], send_sem=send_sems.at[k], recv_sem=recv_sems.at[k],
                device_id=(x ^ fx, y ^ fy, c ^ fc), device_id_type=MESH).wait_recv()
        total = slots[0]
        for d in range(1, N_DEV):
            total = total + slots[d]
        out_ref[...] = total

    return pl.pallas_call(
        body, name="sum_small",
        in_specs=[pl.BlockSpec(memory_space=pltpu.VMEM)], out_specs=pl.BlockSpec(memory_space=pltpu.VMEM),
        out_shape=jax.ShapeDtypeStruct(vec.shape, vec.dtype),
        scratch_shapes=[pltpu.VMEM((N_DEV,) + vec.shape, vec.dtype), pltpu.SemaphoreType.DMA((7,)),
                        pltpu.SemaphoreType.DMA((7,))],
    )(vec)


UPD_ROWS = 256


def _pair_sum(g, theirs, core, chip, name):
    half = theirs.shape[1]
    nb = half // UPD_ROWS

    def body(core_ref, chip_ref, g_ref, t_ref, own_ref, o16_ref):
        total = g_ref[0] + t_ref[0].astype(F32)
        o16_ref[0] = total.astype(o16_ref.dtype)

        @pl.when(pl.program_id(1) == chip_ref[0])
        def _():
            own_ref[...] = total

    return pl.pallas_call(
        body, name=name,
        grid_spec=pltpu.PrefetchScalarGridSpec(
            num_scalar_prefetch=2, grid=(nb, N_CHIPS),
            in_specs=[pl.BlockSpec((1, UPD_ROWS, FLAT_COLS), lambda i, d, core_ref, chip_ref: (d, core_ref[0] * nb + i, 0)),
                      pl.BlockSpec((1, UPD_ROWS, FLAT_COLS), lambda i, d, core_ref, chip_ref: (d, i, 0))],
            out_specs=[pl.BlockSpec((UPD_ROWS, FLAT_COLS), lambda i, d, core_ref, chip_ref: (i, 0)),
                       pl.BlockSpec((1, UPD_ROWS, FLAT_COLS), lambda i, d, core_ref, chip_ref: (d, i, 0))]),
        out_shape=[jax.ShapeDtypeStruct((half, FLAT_COLS), F32),
                   jax.ShapeDtypeStruct((N_CHIPS, half, FLAT_COLS), BF16)],
        compiler_params=_params(("parallel", "arbitrary")),
    )(core, chip, g, theirs)


def _chip_sum(own, received, name):
    half = own.shape[0]
    nb = half // UPD_ROWS

    def body(p_ref, r_ref, o_ref):
        o_ref[...] = ((p_ref[...] + r_ref[0].astype(F32)) + r_ref[1].astype(F32)) + r_ref[2].astype(F32)

    return pl.pallas_call(
        body, name=name, grid=(nb,),
        in_specs=[pl.BlockSpec((UPD_ROWS, FLAT_COLS), lambda i: (i, 0)),
                  pl.BlockSpec((3, UPD_ROWS, FLAT_COLS), lambda i: (0, i, 0))],
        out_specs=pl.BlockSpec((UPD_ROWS, FLAT_COLS), lambda i: (i, 0)),
        out_shape=jax.ShapeDtypeStruct((half, FLAT_COLS), F32),
        compiler_params=_params(("parallel",)),
    )(own, received)


def _adamw(w, g, m, v, *, rows, name):
    r, c = w.shape
    rows = min(rows, r)
    assert r % rows == 0

    def body(w_ref, g_ref, m_ref, v_ref, d_ref, nm_ref, nv_ref):
        gv = g_ref[...]
        nm = ADAM_B1 * m_ref[...] + (1.0 - ADAM_B1) * gv
        nv = ADAM_B2 * v_ref[...] + (1.0 - ADAM_B2) * (gv * gv)
        m_hat = nm / (1.0 - ADAM_B1 ** ADAM_STEP)
        v_hat = nv / (1.0 - ADAM_B2 ** ADAM_STEP)
        d_ref[...] = -ADAM_LR * (m_hat / (jnp.sqrt(v_hat) + ADAM_EPS) + ADAM_WD * w_ref[...])
        nm_ref[...] = nm
        nv_ref[...] = nv

    spec = pl.BlockSpec((rows, c), lambda i: (i, 0))
    return pl.pallas_call(
        body, name=name, grid=(r // rows,),
        in_specs=[spec] * 4, out_specs=[spec] * 3,
        out_shape=[jax.ShapeDtypeStruct((r, c), F32)] * 3,
        compiler_params=_params(("parallel",)),
    )(w, g, m, v)


W_NAMES = ("a_w_in", "a_w_out", "b_w_in", "b_q_norm", "b_w_uq", "b_w_out", "kv_w_down", "kv_norm", "kv_w_up", "ln_g", "ln_b")
BIG = tuple(name for name, _ in SHARD_ROWS)


def _pack_small(ln_g, ln_b, q_norm, kv_norm, extra=None):
    pad = lambda a: jnp.pad(a.reshape(1, -1), ((0, 0), (0, FLAT_COLS - a.size)))
    rows = [ln_g, ln_b, pad(q_norm), pad(kv_norm),
            jnp.zeros((1, FLAT_COLS), F32) if extra is None else pad(extra), jnp.zeros((1, FLAT_COLS), F32)]
    return jnp.concatenate(rows, axis=0)


def _unpack_small(p):
    return dict(ln_g=p[0:2], ln_b=p[2:4], b_q_norm=p[4:5, :Q_LORA], kv_norm=p[5, :KV_LORA])


def kernel(x, a_w_in, a_w_out, b_w_in, b_q_norm, b_w_uq, b_w_out, kv_w_down, kv_norm, kv_w_up, ln_g, ln_b, loss_target, m_a_w_in, m_a_w_out, m_b_w_in, m_b_q_norm, m_b_w_uq, m_b_w_out, m_kv_w_down, m_kv_norm, m_kv_w_up, m_ln_g, m_ln_b, v_a_w_in, v_a_w_out, v_b_w_in, v_b_q_norm, v_b_w_uq, v_b_w_out, v_kv_w_down, v_kv_norm, v_kv_w_up, v_ln_g, v_ln_b):
    w_in = dict(a_w_in=a_w_in[0], a_w_out=a_w_out[0], b_w_in=b_w_in[0], b_w_uq=b_w_uq[0], b_w_out=b_w_out[0],
                kv_w_down=kv_w_down, kv_w_up=kv_w_up)
    m_in = dict(a_w_in=m_a_w_in[0], a_w_out=m_a_w_out[0], b_w_in=m_b_w_in[0], b_w_uq=m_b_w_uq[0], b_w_out=m_b_w_out[0],
                kv_w_down=m_kv_w_down, kv_w_up=m_kv_w_up)
    v_in = dict(a_w_in=v_a_w_in[0], a_w_out=v_a_w_out[0], b_w_in=v_b_w_in[0], b_w_uq=v_b_w_uq[0], b_w_out=v_b_w_out[0],
                kv_w_down=v_kv_w_down, kv_w_up=v_kv_w_up)
    shard_shapes = {name: w_in[name].shape for name in BIG}

    cx, cy, cc = lax.axis_index("x"), lax.axis_index("y"), lax.axis_index("c")
    flat16 = _flat_shards(w_in, BF16)
    chip = 2 * cx + cy
    flat_a, flat_b = flat16[:A_ROWS], flat16[A_ROWS:]
    got_a = lax.dynamic_update_slice(_gather_weights(flat_a), flat_a[None], (chip, 0, 0))
    weights_a = _kernel_layout_a(_full_from_gathered(got_a, A_SHARDS))

    core_arr, chip_arr = cc.astype(jnp.int32).reshape(1), chip.astype(jnp.int32).reshape(1)

    def reduce_pair(full, shards, rows, tag):
        g_all = _gathered_from_full(full, shards, rows)
        half = rows // 2
        other_half = lax.dynamic_slice(g_all, (0, (1 - cc) * half, 0), (N_CHIPS, half, FLAT_COLS)).astype(BF16)
        theirs = _pair_swap(other_half, "pair_exchange_" + tag)
        return _pair_sum(g_all, theirs, core_arr, chip_arr, "pair_sum_" + tag)

    loss, grad_x, reduced, small = _local_step(x[0], loss_target[0], weights_a, kv_norm.reshape(1, -1), b_q_norm, ln_g, ln_b,
                                               late=(flat_b, chip), reduce=reduce_pair)

    g_big = {}
    for (own, received), shards, tag in zip(reduced, (EARLY_SHARDS, LATE_SHARDS), ("early", "late")):
        mine = _chip_sum(own, received, "chip_sum_" + tag)
        sibling = _pair_swap(mine, "pair_share_" + tag)
        g_flat = jnp.concatenate([jnp.where(cc == 0, mine, sibling), jnp.where(cc == 0, sibling, mine)], axis=0)
        g_big.update(_unflat_shards(g_flat, shard_shapes, shards))

    small_sum = _sum_small(_pack_small(small["ln_g"], small["ln_b"], small["q_norm"], small["kv_norm"], loss[:, :1]))
    loss_out = small_sum[6, 0]

    upd = {name: _adamw(w_in[name], g_big[name], m_in[name], v_in[name], rows=256, name="adamw_" + name) for name in BIG}
    ds, nms, nvs = _adamw(_pack_small(ln_g, ln_b, b_q_norm, kv_norm), small_sum.at[6].set(0.0),
                          _pack_small(m_ln_g, m_ln_b, m_b_q_norm, m_kv_norm),
                          _pack_small(v_ln_g, v_ln_b, v_b_q_norm, v_kv_norm), rows=8, name="adamw_small")

    def assemble(big, small_packed):
        sm = _unpack_small(small_packed)
        out = {}
        for name in W_NAMES:
            if name in big:
                out[name] = big[name][None] if name in ("a_w_in", "a_w_out", "b_w_in", "b_w_uq", "b_w_out") else big[name]
            else:
                out[name] = sm[name]
        return [out[name] for name in W_NAMES]

    part = lambda k: {name: upd[name][k] for name in BIG}
    return (loss_out, grad_x[None], *assemble(g_big, small_sum), *assemble(part(0), ds), *assemble(part(1), nms),
            *assemble(part(2), nvs))
```

```python
import functools
import math

import jax
import jax.numpy as jnp
from jax import lax
from jax.experimental import pallas as pl
from jax.experimental.pallas import tpu as pltpu

F32 = jnp.float32
BF16 = jnp.bfloat16
MESH = pl.DeviceIdType.MESH

D_MODEL = 1024
DEPTH = 2
H_A, DK_A, DV_A = 4, 256, 512
WIDTH_A = H_A * DV_A
CHUNK = 128
H_B, QK_NOPE, QK_ROPE, V_HEAD = 16, 128, 64, 128
QK_PAD = 256
Q_LORA, KV_LORA = 768, 512
KV_DOWN_PAD = 640
WIDTH_B = H_B * V_HEAD
IN_A = 2 * H_A * DK_A + 2 * WIDTH_A
IN_B = Q_LORA + WIDTH_B
H1_B = KV_DOWN_PAD + IN_B
ROPE_BASE = 10000.0
ALPHA = (2.0 * DEPTH) ** 0.25
ATT_SCALE = (QK_NOPE + QK_ROPE) ** -0.5
NEG_BIG = -1e30

ADAM_LR, ADAM_B1, ADAM_B2, ADAM_EPS, ADAM_WD, ADAM_STEP = 0.001, 0.9, 0.999, 1e-08, 0.01, 10

VMEM_LIMIT_BYTES = 56 * 1024 * 1024
LANES = 128
FLAT_COLS = 1024
SHARD_ROWS = (("a_w_in", 1536), ("a_w_out", 512), ("b_w_in", 704), ("b_w_uq", 576), ("b_w_out", 512),
              ("kv_w_down", 144), ("kv_w_up", 512))
FLAT_ROWS = 4608
HALF_ROWS = FLAT_ROWS // 2
N_CHIPS = 4
N_DEV = 8


def _params(sem, vmem=VMEM_LIMIT_BYTES):
    return pltpu.CompilerParams(dimension_semantics=sem, vmem_limit_bytes=vmem)


def _row_spec(ts, w, col_block=0):
    return pl.BlockSpec((ts, w), lambda i: (i, col_block))


def _bc_spec(shape):
    nd = len(shape)
    return pl.BlockSpec(shape, lambda i: (0,) * nd)


def _sigmoid(x):
    return 1.0 / (1.0 + jnp.exp(-x))


def _fold8(v):
    ts, w = v.shape
    return jnp.sum(v.reshape(ts // 8, 8, w), axis=0)


def _mm(a, b, *, ta=False, tb=False, out_dtype=F32, tm=1024, tn=512, tk=None, name, extras=(), epilogue=None, out_tn=None,
        side=None):
    if ta:
        K, M = a.shape
    else:
        M, K = a.shape
    if tb:
        N, Kb = b.shape
    else:
        Kb, N = b.shape
    assert K == Kb, (a.shape, b.shape)
    tm, tn = min(tm, M), min(tn, N)
    tk = K if tk is None else min(tk, K)
    assert M % tm == 0 and N % tn == 0 and K % tk == 0, (name, M, N, K, tm, tn, tk)
    grid = (M // tm, N // tn, K // tk)
    nk = grid[2]
    out_tn = tn if out_tn is None else out_tn
    n_extra = len(extras)
    side_inputs = [] if side is None else list(side["inputs"])
    n_side = len(side_inputs)
    n_acc = 0 if nk == 1 else 1
    dims = (((0,) if ta else (1,), (1,) if tb else (0,)), ((), ()))

    def body(a_ref, b_ref, *rest):
        extra_refs, rest = rest[:n_extra], rest[n_extra:]
        side_in, o_ref, rest = rest[:n_side], rest[n_side], rest[n_side + 1:]
        if side is not None:
            side_refs, rest = side_in + rest[:1] + rest[1 + n_acc:], rest[1:]
            ids = [pl.program_id(d) for d in range(3)]

            @pl.when((ids[0] == 0) & (ids[1] == 0) & (ids[2] == 0))
            def _():
                side["start"](*side_refs)

        prod = lax.dot_general(a_ref[...].astype(BF16), b_ref[...].astype(BF16), dims,
                               preferred_element_type=F32)

        def store(tile):
            if epilogue is None:
                o_ref[...] = tile.astype(o_ref.dtype)
            else:
                epilogue(tile, o_ref, *extra_refs)

        if nk == 1:
            store(prod)
        else:
            acc = rest[0]
            k = pl.program_id(2)

            @pl.when(k == 0)
            def _():
                acc[...] = prod

            @pl.when(k > 0)
            def _():
                acc[...] += prod

            @pl.when(k == nk - 1)
            def _():
                store(acc[...])

        if side is not None:
            @pl.when((ids[0] == grid[0] - 1) & (ids[1] == grid[1] - 1) & (ids[2] == grid[2] - 1))
            def _():
                side["finish"](*side_refs)

    a_spec = pl.BlockSpec((tk, tm), lambda i, j, k: (k, i)) if ta else pl.BlockSpec((tm, tk), lambda i, j, k: (i, k))
    b_spec = pl.BlockSpec((tn, tk), lambda i, j, k: (j, k)) if tb else pl.BlockSpec((tk, tn), lambda i, j, k: (k, j))
    extra_specs = [pl.BlockSpec((tm, e.shape[1]), lambda i, j, k: (i, 0)) for e in extras]
    out_specs = pl.BlockSpec((tm, out_tn), lambda i, j, k: (i, j))
    out_shape = jax.ShapeDtypeStruct((M, (N // tn) * out_tn), out_dtype)
    scratch = [] if nk == 1 else [pltpu.VMEM((tm, tn), F32)]
    if side is not None:
        out_specs, out_shape, scratch = [out_specs, HBM_SPEC], [out_shape, side["out_shape"]], scratch + list(side["scratch"])
    return pl.pallas_call(
        body, name=name, grid=grid,
        in_specs=[a_spec, b_spec] + extra_specs + [HBM_SPEC] * n_side,
        out_specs=out_specs, out_shape=out_shape, scratch_shapes=scratch,
        compiler_params=_params(("parallel", "parallel", "arbitrary") if side is None else ("arbitrary",) * 3),
    )(a, b, *extras, *side_inputs)


def _rope_tables_a(s):
    half = DK_A // 2
    inv = ROPE_BASE ** (-jnp.arange(half, dtype=F32) / half)
    ang = jnp.arange(s, dtype=F32)[:, None] * inv[None, :]
    return jnp.cos(ang), jnp.sin(ang)


def _rope_tables_b(s):
    half = QK_ROPE // 2
    inv = ROPE_BASE ** (-jnp.arange(half, dtype=F32) / half)
    ang = jnp.arange(s, dtype=F32)[:, None] * inv[None, :]
    c, sn = jnp.cos(ang), jnp.sin(ang)
    z = jnp.zeros_like(c)
    cos = jnp.concatenate([c, c, z, z], axis=1)
    sa = jnp.concatenate([-sn, z, z, z], axis=1)
    sb = jnp.concatenate([z, sn, z, z], axis=1)
    return cos, sa, sb


def _rope_b(r, cos, sa, sb, sign):
    return r * cos + sign * (pltpu.roll(r, 96, 1) * sa + pltpu.roll(r, 32, 1) * sb)


def _retention_tables():
    lg = jnp.log1p(-jnp.exp2(-5.0 - jnp.arange(H_A, dtype=F32)))
    idx = jnp.arange(CHUNK, dtype=F32)
    diff = idx[:, None] - idx[None, :]
    causal = diff >= 0
    dmat = jnp.where(causal, jnp.exp(jnp.where(causal, diff, 0.0)[None] * lg[:, None, None]), 0.0)
    qdec = jnp.exp((idx + 1.0)[None, :] * lg[:, None])[:, :, None]
    kdec = jnp.exp((CHUNK - 1.0 - idx)[None, :] * lg[:, None])[:, :, None]
    cdec = jnp.broadcast_to(jnp.exp(CHUNK * lg)[:, None, None], (H_A, 1, DV_A))
    return dmat, qdec, kdec, cdec


def _group_norm(o):
    mu = jnp.mean(o, axis=-1, keepdims=True)
    oc = o - mu
    var = jnp.mean(oc * oc, axis=-1, keepdims=True)
    rstd = lax.rsqrt(var + 1e-5)
    return oc * rstd, rstd


Q_COL, K_COL, V_COL, GATE_COL = 0, H_A * DK_A, 2 * H_A * DK_A, 2 * H_A * DK_A + WIDTH_A


def _ln_stats(z):
    mu = jnp.mean(z, axis=-1, keepdims=True)
    zc = z - mu
    var = jnp.mean(zc * zc, axis=-1, keepdims=True)
    rstd = lax.rsqrt(var + 1e-5)
    return zc * rstd, rstd


def _ln_bwd(dy, xhat, rstd, g):
    dxh = dy * g
    m1 = jnp.mean(dxh, axis=-1, keepdims=True)
    m2 = jnp.mean(dxh * xhat, axis=-1, keepdims=True)
    return rstd * (dxh - m1 - xhat * m2)


def _ln_fwd(x, y, g, b, *, ts):
    s = x.shape[0]

    def body(x_ref, y_ref, g_ref, b_ref, o_ref, o16_ref):
        xhat, _ = _ln_stats(ALPHA * x_ref[...] + y_ref[...])
        out = xhat * g_ref[...] + b_ref[...]
        o_ref[...] = out
        o16_ref[...] = out.astype(o16_ref.dtype)

    return pl.pallas_call(
        body, name="ln_fwd", grid=(s // ts,),
        in_specs=[_row_spec(ts, D_MODEL), _row_spec(ts, D_MODEL), _bc_spec((1, D_MODEL)), _bc_spec((1, D_MODEL))],
        out_specs=[_row_spec(ts, D_MODEL), _row_spec(ts, D_MODEL)],
        out_shape=[jax.ShapeDtypeStruct((s, D_MODEL), F32), jax.ShapeDtypeStruct((s, D_MODEL), BF16)],
        compiler_params=_params(("parallel",)),
    )(x, y, g, b)


def _ln_loss_bwd(x1, y, target, g, b, *, ts):
    s = x1.shape[0]
    n = s // ts

    def body(x_ref, y_ref, t_ref, g_ref, b_ref, dz_ref, dz16_ref, dg_ref, db_ref, loss_ref, ag, ab, al):
        i = pl.program_id(0)

        @pl.when(i == 0)
        def _():
            ag[...] = jnp.zeros_like(ag)
            ab[...] = jnp.zeros_like(ab)
            al[...] = jnp.zeros_like(al)

        xhat, rstd = _ln_stats(ALPHA * x_ref[...] + y_ref[...])
        err = xhat * g_ref[...] + b_ref[...] - t_ref[...]
        al[...] += _fold8(err * err)
        dy = err * (1.0 / D_MODEL)
        ag[...] += _fold8(dy * xhat)
        ab[...] += _fold8(dy)
        dz = _ln_bwd(dy, xhat, rstd, g_ref[...])
        dz_ref[...] = dz
        dz16_ref[...] = dz.astype(dz16_ref.dtype)

        @pl.when(i == n - 1)
        def _():
            dg_ref[...] = jnp.sum(ag[...], axis=0, keepdims=True)
            db_ref[...] = jnp.sum(ab[...], axis=0, keepdims=True)
            loss_ref[...] = jnp.full((1, LANES), (0.5 / D_MODEL) * jnp.sum(al[...]), F32)

    return pl.pallas_call(
        body, name="ln_loss_bwd", grid=(n,),
        in_specs=[_row_spec(ts, D_MODEL)] * 3 + [_bc_spec((1, D_MODEL))] * 2,
        out_specs=[_row_spec(ts, D_MODEL), _row_spec(ts, D_MODEL), _bc_spec((1, D_MODEL)), _bc_spec((1, D_MODEL)),
                   _bc_spec((1, LANES))],
        out_shape=[jax.ShapeDtypeStruct((s, D_MODEL), F32), jax.ShapeDtypeStruct((s, D_MODEL), BF16),
                   jax.ShapeDtypeStruct((1, D_MODEL), F32), jax.ShapeDtypeStruct((1, D_MODEL), F32),
                   jax.ShapeDtypeStruct((1, LANES), F32)],
        scratch_shapes=[pltpu.VMEM((8, D_MODEL), F32)] * 3,
        compiler_params=_params(("arbitrary",)),
    )(x1, y, target, g, b)


def _ln_bwd_call(dy, x, y, g, *, ts):
    s = x.shape[0]
    n = s // ts

    def body(dy_ref, x_ref, y_ref, g_ref, dz_ref, dz16_ref, dg_ref, db_ref, ag, ab):
        i = pl.program_id(0)

        @pl.when(i == 0)
        def _():
            ag[...] = jnp.zeros_like(ag)
            ab[...] = jnp.zeros_like(ab)

        xhat, rstd = _ln_stats(ALPHA * x_ref[...] + y_ref[...])
        dy = dy_ref[...]
        ag[...] += _fold8(dy * xhat)
        ab[...] += _fold8(dy)
        dz = _ln_bwd(dy, xhat, rstd, g_ref[...])
        dz_ref[...] = dz
        dz16_ref[...] = dz.astype(dz16_ref.dtype)

        @pl.when(i == n - 1)
        def _():
            dg_ref[...] = jnp.sum(ag[...], axis=0, keepdims=True)
            db_ref[...] = jnp.sum(ab[...], axis=0, keepdims=True)

    return pl.pallas_call(
        body, name="ln_bwd", grid=(n,),
        in_specs=[_row_spec(ts, D_MODEL)] * 3 + [_bc_spec((1, D_MODEL))],
        out_specs=[_row_spec(ts, D_MODEL), _row_spec(ts, D_MODEL), _bc_spec((1, D_MODEL)), _bc_spec((1, D_MODEL))],
        out_shape=[jax.ShapeDtypeStruct((s, D_MODEL), F32), jax.ShapeDtypeStruct((s, D_MODEL), BF16),
                   jax.ShapeDtypeStruct((1, D_MODEL), F32), jax.ShapeDtypeStruct((1, D_MODEL), F32)],
        scratch_shapes=[pltpu.VMEM((8, D_MODEL), F32)] * 2,
        compiler_params=_params(("arbitrary",)),
    )(dy, x, y, g)


def _residual_store(tile, o_ref, dz_ref):
    o_ref[...] = ALPHA * dz_ref[...] + tile


C_LAT = slice(0, KV_LORA)
C_ROPE = slice(KV_LORA, KV_DOWN_PAD)
C_QL = slice(KV_DOWN_PAD, KV_DOWN_PAD + Q_LORA)
C_GATE = slice(KV_DOWN_PAD + Q_LORA, H1_B)


def _rms(x, eps=1e-6):
    r = lax.rsqrt(jnp.mean(x * x, axis=-1, keepdims=True) + eps)
    return x * r, r


def _rms_bwd(dy, xhat, r, g):
    dxh = dy * g
    return r * (dxh - xhat * jnp.mean(dxh * xhat, axis=-1, keepdims=True))


def _kvq_prep(h1, kv_norm, q_norm, cos, sa, sb, *, ts):
    s = h1.shape[0]

    def body(h_ref, kn_ref, qn_ref, c_ref, sa_ref, sb_ref, lat_ref, kr_ref, ql_ref):
        lat, _ = _rms(h_ref[:, C_LAT])
        lat_ref[...] = (lat * kn_ref[...]).astype(lat_ref.dtype)
        kr_ref[...] = _rope_b(h_ref[:, C_ROPE], c_ref[...], sa_ref[...], sb_ref[...], 1.0).astype(kr_ref.dtype)
        ql, _ = _rms(h_ref[:, C_QL])
        ql_ref[...] = (ql * qn_ref[...]).astype(ql_ref.dtype)

    return pl.pallas_call(
        body, name="kvq_prep", grid=(s // ts,),
        in_specs=[_row_spec(ts, H1_B), _bc_spec((1, KV_LORA)), _bc_spec((1, Q_LORA))] + [_row_spec(ts, 128)] * 3,
        out_specs=[_row_spec(ts, KV_LORA), _row_spec(ts, 128), _row_spec(ts, Q_LORA)],
        out_shape=[jax.ShapeDtypeStruct((s, KV_LORA), BF16), jax.ShapeDtypeStruct((s, 128), BF16),
                   jax.ShapeDtypeStruct((s, Q_LORA), BF16)],
        compiler_params=_params(("parallel",)),
    )(h1, kv_norm, q_norm, cos, sa, sb)


LOG2E = 1.4426950408889634
LN2 = 0.6931471805599453
Q_SCALE = ATT_SCALE * LOG2E


def _rope_q_store(tile, o_ref, c_ref, sa_ref, sb_ref):
    c, a, b = c_ref[...], sa_ref[...], sb_ref[...]
    for hd in range(tile.shape[1] // QK_PAD):
        lo = hd * QK_PAD
        o_ref[:, lo:lo + 128] = (tile[:, lo:lo + 128] * Q_SCALE).astype(o_ref.dtype)
        o_ref[:, lo + 128:lo + 256] = (_rope_b(tile[:, lo + 128:lo + 256], c, a, b, 1.0) * Q_SCALE).astype(o_ref.dtype)


def _assemble_k_store(tile, o_ref, kr_ref):
    r = kr_ref[...]
    for hd in range(tile.shape[1] // QK_NOPE):
        o_ref[:, hd * QK_PAD:hd * QK_PAD + 128] = tile[:, hd * 128:(hd + 1) * 128].astype(o_ref.dtype)
        o_ref[:, hd * QK_PAD + 128:(hd + 1) * QK_PAD] = r


def _h1_bwd(h1, dlat_k, dlat_v, dkr_heads, dqn, dg16, kv_norm, q_norm, cos, sa, sb, *, ts):
    s = h1.shape[0]
    n = s // ts

    def body(h_ref, dk_ref, dv_ref, dkr_ref, dqn_ref, dg_ref, kn_ref, qn_ref, c_ref, sa_ref, sb_ref,
             o_ref, dkn_ref, dqn_out_ref, akn, aqn):
        i = pl.program_id(0)

        @pl.when(i == 0)
        def _():
            akn[...] = jnp.zeros_like(akn)
            aqn[...] = jnp.zeros_like(aqn)

        lat, r = _rms(h_ref[:, C_LAT])
        dlat = dk_ref[...] + dv_ref[...]
        akn[...] += _fold8(dlat * lat)
        o_ref[:, C_LAT] = _rms_bwd(dlat, lat, r, kn_ref[...]).astype(o_ref.dtype)

        dkr = dkr_ref[:, 0:128]
        for hd in range(1, H_B):
            dkr = dkr + dkr_ref[:, hd * 128:(hd + 1) * 128]
        o_ref[:, C_ROPE] = _rope_b(dkr, c_ref[...], sa_ref[...], sb_ref[...], -1.0).astype(o_ref.dtype)

        ql, rq = _rms(h_ref[:, C_QL])
        dq = dqn_ref[...]
        aqn[...] += _fold8(dq * ql)
        o_ref[:, C_QL] = _rms_bwd(dq, ql, rq, qn_ref[...]).astype(o_ref.dtype)
        o_ref[:, C_GATE] = dg_ref[...]

        @pl.when(i == n - 1)
        def _():
            dkn_ref[...] = jnp.sum(akn[...], axis=0, keepdims=True)
            dqn_out_ref[...] = jnp.sum(aqn[...], axis=0, keepdims=True)

    return pl.pallas_call(
        body, name="h1_bwd", grid=(n,),
        in_specs=[_row_spec(ts, H1_B), _row_spec(ts, KV_LORA), _row_spec(ts, KV_LORA), _row_spec(ts, H_B * 128),
                  _row_spec(ts, Q_LORA), _row_spec(ts, WIDTH_B), _bc_spec((1, KV_LORA)), _bc_spec((1, Q_LORA))]
        + [_row_spec(ts, 128)] * 3,
        out_specs=[_row_spec(ts, H1_B), _bc_spec((1, KV_LORA)), _bc_spec((1, Q_LORA))],
        out_shape=[jax.ShapeDtypeStruct((s, H1_B), BF16), jax.ShapeDtypeStruct((1, KV_LORA), F32),
                   jax.ShapeDtypeStruct((1, Q_LORA), F32)],
        scratch_shapes=[pltpu.VMEM((8, KV_LORA), F32), pltpu.VMEM((8, Q_LORA), F32)],
        compiler_params=_params(("arbitrary",)),
    )(h1, dlat_k, dlat_v, dkr_heads, dqn, dg16, kv_norm, q_norm, cos, sa, sb)


def _dot(a, b, ca, cb):
    return lax.dot_general(a, b, (((ca,), (cb,)), ((), ())), preferred_element_type=F32)


def _table_specs():
    full = lambda shape: pl.BlockSpec(shape, lambda i: (0,) * len(shape))
    return [full((H_A, CHUNK, CHUNK)), full((H_A, CHUNK, 1)), full((H_A, CHUNK, 1)), full((H_A, 1, DV_A))]


def _retention_fwd(h_a, cos, sin, tables):
    s = h_a.shape[0]
    n = s // CHUNK

    def body(h_ref, c_ref, s_ref, dm_ref, qd_ref, kd_ref, cd_ref, q_ref, k_ref, v_ref, o_ref, u_ref, st_ref, state):
        @pl.when(pl.program_id(0) == 0)
        def _():
            state[...] = jnp.zeros_like(state)

        c, sn = c_ref[...], s_ref[...]
        for hd in range(H_A):
            qs, vs = slice(hd * DK_A, (hd + 1) * DK_A), slice(hd * DV_A, (hd + 1) * DV_A)
            for r_ref, base, scale in ((q_ref, Q_COL, 1.0), (k_ref, K_COL, DK_A ** -0.5)):
                lo = base + hd * DK_A
                x1, x2 = h_ref[:, lo:lo + 128], h_ref[:, lo + 128:lo + 256]
                r_ref[:, hd * DK_A:hd * DK_A + 128] = ((x1 * c - x2 * sn) * scale).astype(r_ref.dtype)
                r_ref[:, hd * DK_A + 128:(hd + 1) * DK_A] = ((x2 * c + x1 * sn) * scale).astype(r_ref.dtype)
            v_ref[:, vs] = h_ref[:, V_COL + hd * DV_A:V_COL + (hd + 1) * DV_A].astype(v_ref.dtype)
            qv, kv, vv = q_ref[:, qs], k_ref[:, qs], v_ref[:, vs]
            st = state[hd]
            st16 = st.astype(BF16)
            st_ref[0, hd] = st16
            scores = _dot(qv, kv, 1, 1) * dm_ref[hd]
            qd = (qv.astype(F32) * qd_ref[hd]).astype(BF16)
            o = _dot(scores.astype(BF16), vv, 1, 0) + _dot(qd, st16, 1, 0)
            o_ref[:, vs] = o
            kd = (kv.astype(F32) * kd_ref[hd]).astype(BF16)
            state[hd] = st * cd_ref[hd] + _dot(kd, vv, 0, 0)
            on, _ = _group_norm(o)
            g = h_ref[:, GATE_COL + hd * DV_A:GATE_COL + (hd + 1) * DV_A]
            u_ref[:, vs] = (on * (g * _sigmoid(g))).astype(u_ref.dtype)

    row = lambda w: pl.BlockSpec((CHUNK, w), lambda i: (i, 0))
    return pl.pallas_call(
        body, name="retention_fwd", grid=(n,),
        in_specs=[row(IN_A), row(128), row(128)] + _table_specs(),
        out_specs=[row(H_A * DK_A), row(H_A * DK_A), row(WIDTH_A), row(WIDTH_A), row(WIDTH_A),
                   pl.BlockSpec((1, H_A, DK_A, DV_A), lambda i: (i, 0, 0, 0))],
        out_shape=[jax.ShapeDtypeStruct((s, H_A * DK_A), BF16), jax.ShapeDtypeStruct((s, H_A * DK_A), BF16),
                   jax.ShapeDtypeStruct((s, WIDTH_A), BF16), jax.ShapeDtypeStruct((s, WIDTH_A), F32),
                   jax.ShapeDtypeStruct((s, WIDTH_A), BF16), jax.ShapeDtypeStruct((n, H_A, DK_A, DV_A), BF16)],
        scratch_shapes=[pltpu.VMEM((H_A, DK_A, DV_A), F32)],
        compiler_params=_params(("arbitrary",)),
    )(h_a, cos, sin, *tables)


def _retention_bwd(q, k, v, states, du, o, h_a, cos, sin, tables):
    s = q.shape[0]
    n = s // CHUNK

    def body(q_ref, k_ref, v_ref, st_ref, du_ref, o_ref, g_ref, c_ref, s_ref, dm_ref, qd_ref, kd_ref, cd_ref, dh_ref, grad_state):
        @pl.when(pl.program_id(0) == 0)
        def _():
            grad_state[...] = jnp.zeros_like(grad_state)

        c, sn = c_ref[...], s_ref[...]
        for hd in range(H_A):
            qs, vs = slice(hd * DK_A, (hd + 1) * DK_A), slice(hd * DV_A, (hd + 1) * DV_A)
            on, rstd = _group_norm(o_ref[:, vs])
            g = g_ref[:, vs]
            sg = _sigmoid(g)
            du_v = du_ref[:, vs]
            don = du_v * (g * sg)
            dh_ref[:, GATE_COL + hd * DV_A:GATE_COL + (hd + 1) * DV_A] = (du_v * on * (sg * (1.0 + g * (1.0 - sg)))).astype(dh_ref.dtype)
            m1 = jnp.mean(don, axis=-1, keepdims=True)
            m2 = jnp.mean(don * on, axis=-1, keepdims=True)
            dov = (rstd * (don - m1 - on * m2)).astype(BF16)

            qv, kv, vv = q_ref[:, qs], k_ref[:, qs], v_ref[:, vs]
            dm = dm_ref[hd]
            gs = grad_state[hd]
            g16 = gs.astype(BF16)
            scores = (_dot(qv, kv, 1, 1) * dm).astype(BF16)
            dscores = (_dot(dov, vv, 1, 1) * dm).astype(BF16)
            qd = (qv.astype(F32) * qd_ref[hd]).astype(BF16)
            kd = (kv.astype(F32) * kd_ref[hd]).astype(BF16)
            dq = _dot(dscores, kv, 1, 0) + _dot(dov, st_ref[0, hd], 1, 1) * qd_ref[hd]
            dk = (_dot(dscores, qv, 0, 0) + _dot(vv, g16, 1, 1) * kd_ref[hd]) * (DK_A ** -0.5)
            dh_ref[:, V_COL + hd * DV_A:V_COL + (hd + 1) * DV_A] = (_dot(scores, dov, 0, 0) + _dot(kd, g16, 1, 0)).astype(dh_ref.dtype)
            grad_state[hd] = gs * cd_ref[hd] + _dot(qd, dov, 0, 0)
            for d, base in ((dq, Q_COL), (dk, K_COL)):
                lo = base + hd * DK_A
                d1, d2 = d[:, 0:128], d[:, 128:256]
                dh_ref[:, lo:lo + 128] = (d1 * c + d2 * sn).astype(dh_ref.dtype)
                dh_ref[:, lo + 128:lo + 256] = (d2 * c - d1 * sn).astype(dh_ref.dtype)

    rev = lambda i: n - 1 - i
    row = lambda w, col=0: pl.BlockSpec((CHUNK, w), lambda i: (rev(i), col))
    return pl.pallas_call(
        body, name="retention_bwd", grid=(n,),
        in_specs=[row(H_A * DK_A), row(H_A * DK_A), row(WIDTH_A), pl.BlockSpec((1, H_A, DK_A, DV_A), lambda i: (rev(i), 0, 0, 0)),
                  row(WIDTH_A), row(WIDTH_A), row(WIDTH_A, GATE_COL // WIDTH_A), row(128), row(128)] + _table_specs(),
        out_specs=row(IN_A),
        out_shape=jax.ShapeDtypeStruct((s, IN_A), BF16),
        scratch_shapes=[pltpu.VMEM((H_A, DK_A, DV_A), F32)],
        compiler_params=_params(("arbitrary",)),
    )(q, k, v, states, du, o, h_a, cos, sin, *tables)


GATE_BLOCK0 = (KV_DOWN_PAD + Q_LORA) // LANES


def _causal_mask(sc, row0):
    row = lax.broadcasted_iota(jnp.int32, sc.shape, 0) + row0
    col = lax.broadcasted_iota(jnp.int32, sc.shape, 1)
    return jnp.where(col <= row, sc, NEG_BIG)


def _key_block_loop(step, n, per_trip, smallest=1):
    def trip_body(jj, carry):
        for t in range(per_trip):
            step(per_trip * jj + t)
        return carry

    lax.fori_loop(0, n // per_trip, trip_body, 0)
    group = per_trip // 2
    while group >= smallest:
        def tail(group=group):
            first = (n // (2 * group)) * (2 * group)
            for t in range(group):
                step(first + t)

        pl.when((n // group) % 2 == 1)(tail)
        group //= 2


def _attention_fwd(q, k, v, h1, *, blk, bk, sub, per_trip):
    s = q.shape[0]
    nb = s // blk

    def body(q_ref, k_ref, v_ref, g_ref, o_ref, u_ref, lse_ref, vext_s, m_s, acc_s):
        i = pl.program_id(1)

        @pl.when(i == 0)
        def _():
            vext_s[:, 0:V_HEAD] = v_ref[...]
            vext_s[:, V_HEAD:2 * V_HEAD] = jnp.ones((s, V_HEAD), vext_s.dtype)

        m_s[...] = jnp.full_like(m_s, NEG_BIG)
        acc_s[...] = jnp.zeros_like(acc_s)

        def update(rows, kb, vb, row0):
            sc = _dot(q_ref[rows, :], kb, 1, 1)
            if row0 is not None:
                sc = _causal_mask(sc, row0)
            m_prev = m_s[rows, :]
            m_new = jnp.maximum(m_prev, jnp.max(sc, axis=-1, keepdims=True))
            p = jnp.exp2(sc - jnp.tile(m_new, (1, kb.shape[0] // LANES)))
            a = jnp.exp2(m_prev - m_new)
            acc_s[rows, :] = jnp.tile(a, (1, 2)) * acc_s[rows, :] + _dot(p.astype(BF16), vb, 1, 0)
            m_s[rows, :] = m_new

        def step(j):
            kv_rows = pl.ds(pl.multiple_of(j * bk, bk), bk)
            kb, vb = k_ref[kv_rows, :], vext_s[kv_rows, :]
            for r in range(blk // sub):
                update(slice(r * sub, (r + 1) * sub), kb, vb, None)

        _key_block_loop(step, i * (blk // bk), per_trip, smallest=blk // bk)
        for r in range(blk // sub):
            ncols = (r + 1) * sub
            kv_rows = pl.ds(pl.multiple_of(i * blk, blk), ncols)
            update(slice(r * sub, (r + 1) * sub), k_ref[kv_rows, :], vext_s[kv_rows, :], r * sub)
        acc = acc_s[...]
        l = acc[:, V_HEAD:2 * V_HEAD]
        o = acc[:, 0:V_HEAD] / l
        g = g_ref[...]
        o_ref[...] = o
        u_ref[...] = (o * (g * _sigmoid(g))).astype(u_ref.dtype)
        lse_ref[0] = (m_s[...] + jnp.log2(l))[:, 0:1]

    blk_idx = lambda h, i: (i, h)
    return pl.pallas_call(
        body, name="attention_fwd", grid=(H_B, nb),
        in_specs=[pl.BlockSpec((blk, QK_PAD), blk_idx), pl.BlockSpec((s, QK_PAD), lambda h, i: (0, h)),
                  pl.BlockSpec((s, V_HEAD), lambda h, i: (0, h)), pl.BlockSpec((blk, LANES), lambda h, i: (i, GATE_BLOCK0 + h))],
        out_specs=[pl.BlockSpec((blk, V_HEAD), blk_idx), pl.BlockSpec((blk, V_HEAD), blk_idx),
                   pl.BlockSpec((1, blk, 1), lambda h, i: (h, i, 0))],
        out_shape=[jax.ShapeDtypeStruct((s, WIDTH_B), F32), jax.ShapeDtypeStruct((s, WIDTH_B), BF16),
                   jax.ShapeDtypeStruct((H_B, s, 1), F32)],
        scratch_shapes=[pltpu.VMEM((s, 2 * V_HEAD), BF16), pltpu.VMEM((blk, LANES), F32), pltpu.VMEM((blk, 2 * V_HEAD), F32)],
        compiler_params=_params(("parallel", "arbitrary")),
    )(q, k, v, h1)


def _attention_bwd(q, k, v, du, o, h1, lse, cos, sa, sb, *, blk):
    s = q.shape[0]
    nb = s // blk
    reps = blk // LANES

    def body(q_ref, k_ref, v_ref, du_ref, o_ref, g_ref, lse_ref, c_ref, sa_ref, sb_ref,
             dq_ref, dkn_ref, dkr_ref, dv_ref, dg_ref, lse_s, dl_s, do_s, dq_s):
        i = pl.program_id(1)
        g = g_ref[...]
        sg = _sigmoid(g)
        du_v, ov = du_ref[...], o_ref[...]
        do = du_v * (g * sg)
        dg_ref[...] = (du_v * ov * (sg * (1.0 + g * (1.0 - sg)))).astype(dg_ref.dtype)
        do_s[...] = do.astype(do_s.dtype)
        dl_s[...] = jnp.broadcast_to(jnp.sum(do * ov, axis=-1, keepdims=True), (blk, LANES))
        lse_s[...] = jnp.broadcast_to(lse_ref[0], (blk, LANES))
        dq_s[...] = jnp.zeros_like(dq_s)

        def products(rows, kv_rows, row0):
            qv, dov, kb = q_ref[rows, :], do_s[rows, :], k_ref[kv_rows, :]
            tile = (1, kb.shape[0] // LANES)
            sc = _dot(qv, kb, 1, 1)
            if row0 is not None:
                sc = _causal_mask(sc, row0)
            p = jnp.exp2(sc - jnp.tile(lse_s[rows, :], tile))
            dp = _dot(dov, v_ref[kv_rows, :], 1, 1)
            ds = (p * (dp - jnp.tile(dl_s[rows, :], tile))).astype(BF16)
            dq_s[rows, :] += _dot(ds, kb, 1, 0)
            return _dot(ds, qv, 0, 0), _dot(p.astype(BF16), dov, 0, 0)

        def put(kv_rows, dk_c, dv_c, first):
            if first:
                dkn_ref[kv_rows, :] = dk_c[:, 0:128]
                dkr_ref[kv_rows, :] = dk_c[:, 128:256]
                dv_ref[kv_rows, :] = dv_c
            else:
                dkn_ref[kv_rows, :] += dk_c[:, 0:128]
                dkr_ref[kv_rows, :] += dk_c[:, 128:256]
                dv_ref[kv_rows, :] += dv_c

        def step(j):
            kv_rows = pl.ds(pl.multiple_of(j * blk, blk), blk)
            dk_c, dv_c = products(slice(0, blk), kv_rows, None)
            put(kv_rows, dk_c, dv_c, False)

        _key_block_loop(step, i, per_trip=4)
        half = blk // 2
        start = pl.multiple_of(i * blk, blk)
        dk_c, dv_c = products(slice(0, half), pl.ds(start, half), 0)
        put(pl.ds(start, half), dk_c, dv_c, True)
        dk_c, dv_c = products(slice(half, blk), pl.ds(start, blk), half)
        put(pl.ds(start, half), dk_c[0:half], dv_c[0:half], False)
        put(pl.ds(pl.multiple_of(start + half, half), half), dk_c[half:blk], dv_c[half:blk], True)
        dq = dq_s[...] * ATT_SCALE
        dq_ref[:, 0:128] = dq[:, 0:128].astype(dq_ref.dtype)
        dq_ref[:, 128:256] = _rope_b(dq[:, 128:256], c_ref[...], sa_ref[...], sb_ref[...], -1.0).astype(dq_ref.dtype)

        @pl.when(i == nb - 1)
        def _():
            dkn_ref[...] = dkn_ref[...] * LN2
            dkr_ref[...] = dkr_ref[...] * LN2

    head = lambda h, i: (0, h)
    blk_idx = lambda h, i: (i, h)
    row_idx = lambda h, i: (i, 0)
    return pl.pallas_call(
        body, name="attention_bwd", grid=(H_B, nb),
        in_specs=[pl.BlockSpec((blk, QK_PAD), blk_idx), pl.BlockSpec((s, QK_PAD), head), pl.BlockSpec((s, V_HEAD), head),
                  pl.BlockSpec((blk, V_HEAD), blk_idx), pl.BlockSpec((blk, V_HEAD), blk_idx),
                  pl.BlockSpec((blk, LANES), lambda h, i: (i, GATE_BLOCK0 + h)), pl.BlockSpec((1, blk, 1), lambda h, i: (h, i, 0)),
                  pl.BlockSpec((blk, LANES), row_idx), pl.BlockSpec((blk, LANES), row_idx), pl.BlockSpec((blk, LANES), row_idx)],
        out_specs=[pl.BlockSpec((blk, QK_PAD), blk_idx), pl.BlockSpec((s, 128), head), pl.BlockSpec((s, 128), head),
                   pl.BlockSpec((s, 128), head), pl.BlockSpec((blk, V_HEAD), blk_idx)],
        out_shape=[jax.ShapeDtypeStruct((s, H_B * QK_PAD), BF16), jax.ShapeDtypeStruct((s, H_B * 128), F32),
                   jax.ShapeDtypeStruct((s, H_B * 128), F32), jax.ShapeDtypeStruct((s, H_B * 128), F32),
                   jax.ShapeDtypeStruct((s, WIDTH_B), BF16)],
        scratch_shapes=[pltpu.VMEM((blk, LANES), F32), pltpu.VMEM((blk, LANES), F32), pltpu.VMEM((blk, V_HEAD), BF16),
                        pltpu.VMEM((blk, QK_PAD), F32)],
        compiler_params=_params(("parallel", "arbitrary")),
    )(q, k, v, du, o, h1, lse, cos, sa, sb)


def _local_step(x, target, w, kv_norm, q_norm, ln_g, ln_b, *, ts=256, blk=512, late=None, reduce=None):
    s = x.shape[0]
    cos_a, sin_a = _rope_tables_a(s)
    cos_b, sa_b, sb_b = _rope_tables_b(s)
    tables = _retention_tables()
    g0, g1, b0, b1 = ln_g[0:1], ln_g[1:2], ln_b[0:1], ln_b[1:2]

    x16 = x.astype(BF16)
    if late is None:
        h_a = _mm(x16, w["a_in"], tn=1536, name="a_in_fwd")
    else:
        flat_b, chip = late
        h_a, got = _mm(x16, w["a_in"], tn=1536, name="a_in_fwd", side=_gather_side(flat_b))
        got = lax.dynamic_update_slice(got, flat_b[None], (chip, 0, 0))
        w = {**w, **_kernel_layout_b(_full_from_gathered(got, B_SHARDS))}
    q_a, k_a, v_a, o_a, u_a, states = _retention_fwd(h_a, cos_a, sin_a, tables)
    y_a = _mm(u_a, w["a_out"], tn=1024, name="a_out_fwd")
    x1, x1_16 = _ln_fwd(x, y_a, g0, b0, ts=ts)

    h1 = _mm(x1_16, w["b_in1"], tn=1152, name="b_in_fwd")
    lat16, kr16, qn16 = _kvq_prep(h1, kv_norm, q_norm, cos_b, sa_b, sb_b, ts=ts)
    k16 = _mm(lat16, w["up_k"], out_dtype=BF16, tn=2048, out_tn=H_B * QK_PAD, extras=(kr16,), epilogue=_assemble_k_store, name="up_k_fwd")
    v16 = _mm(lat16, w["up_v"], out_dtype=BF16, tn=2048, name="up_v_fwd")
    q16 = _mm(qn16, w["uq"], out_dtype=BF16, tn=2048, extras=(cos_b, sa_b, sb_b), epilogue=_rope_q_store, name="uq_fwd")
    o_b, u_b, lse = _attention_fwd(q16, k16, v16, h1, blk=2 * blk, bk=blk, sub=blk // 2, per_trip=4)
    y_b = _mm(u_b, w["b_out"], tn=1024, name="b_out_fwd")

    dz_b, dz_b16, dg1, db1, loss = _ln_loss_bwd(x1, y_b, target, g1, b1, ts=ts)
    d_b_out = _mm(u_b, dz_b16, ta=True, tn=1024, tk=1024, name="b_out_dw")
    du_b = _mm(dz_b16, w["b_out"], tb=True, tn=1024, name="b_out_dx")
    dqf16, dkn, dkr_heads, dv, dgate16 = _attention_bwd(q16, k16, v16, du_b, o_b, h1, lse, cos_b, sa_b, sb_b, blk=blk)
    d_uq = _mm(qn16, dqf16, ta=True, tm=768, tn=2048, tk=1024, name="uq_dw")
    dqn = _mm(dqf16, w["uq"], tb=True, tn=768, tk=2048, name="uq_dx")
    d_up_k = _mm(lat16, dkn, ta=True, tn=2048, tk=1024, name="up_k_dw")
    d_up_v = _mm(lat16, dv, ta=True, tn=2048, tk=1024, name="up_v_dw")
    dlat_k = _mm(dkn, w["up_k"], tb=True, tn=512, name="up_k_dx")
    dlat_v = _mm(dv, w["up_v"], tb=True, tn=512, name="up_v_dx")
    dh1, dkvn, dqnorm = _h1_bwd(h1, dlat_k, dlat_v, dkr_heads, dqn, dgate16, kv_norm, q_norm, cos_b, sa_b, sb_b, ts=ts)
    d_b_in1 = _mm(x1_16, dh1, ta=True, tn=1152, tk=1024, name="b_in_dw")
    dx1 = _mm(dh1, w["b_in1"], tb=True, tn=1024, extras=(dz_b,), epilogue=_residual_store, name="b_in_dx")

    dz_a, dz_a16, dg0, db0 = _ln_bwd_call(dx1, x, y_a, g0, ts=ts)
    d_a_out = _mm(u_a, dz_a16, ta=True, tn=1024, tk=1024, name="a_out_dw")
    du_a = _mm(dz_a16, w["a_out"], tb=True, tn=1024, name="a_out_dx")
    dh_a = _retention_bwd(q_a, k_a, v_a, states, du_a, o_a, h_a, cos_a, sin_a, tables)
    grads = dict(a_out=d_a_out, b_in1=d_b_in1, uq=d_uq, b_out=d_b_out, up_k=d_up_k, up_v=d_up_v)
    small = dict(ln_g=jnp.concatenate([dg0, dg1], axis=0), ln_b=jnp.concatenate([db0, db1], axis=0),
                 q_norm=dqnorm, kv_norm=dkvn)
    if reduce is None:
        grads["a_in"] = _mm(x16, dh_a, ta=True, tn=1536, tk=1024, name="a_in_dw")
        grad_x = _mm(dh_a, w["a_in"], tb=True, tn=1024, tk=2048, extras=(dz_a,), epilogue=_residual_store, name="a_in_dx")
        return loss, grad_x, grads, small
    own_early, travel_early = reduce(_reference_layout_grads(grads), EARLY_SHARDS, EARLY_ROWS, "early")
    d_a_in, got_early = _mm(x16, dh_a, ta=True, tn=1536, tk=1024, name="a_in_dw", side=_chip_exchange_side(travel_early))
    own_late, travel_late = reduce(dict(a_w_in=d_a_in), LATE_SHARDS, LATE_ROWS, "late")
    grad_x, got_late = _mm(dh_a, w["a_in"], tb=True, tn=1024, tk=2048, extras=(dz_a,), epilogue=_residual_store, name="a_in_dx",
                           side=_chip_exchange_side(travel_late))
    return loss, grad_x, ((own_early, got_early), (own_late, got_late)), small


def _flat_shards(shards, dtype):
    parts = [shards[name].reshape(rows, FLAT_COLS) for name, rows in SHARD_ROWS]
    used = sum(rows for _, rows in SHARD_ROWS)
    parts.append(jnp.zeros((FLAT_ROWS - used, FLAT_COLS), shards["a_w_in"].dtype))
    return jnp.concatenate(parts, axis=0).astype(dtype)


def _unflat_shards(flat, shapes, shards):
    out, off = {}, 0
    for name, rows in shards:
        out[name] = flat[off:off + rows].reshape(shapes[name])
        off += rows
    return out


COL_SHARDED = {"a_w_in": (D_MODEL, IN_A), "b_w_in": (D_MODEL, IN_B), "b_w_uq": (Q_LORA, H_B * (QK_NOPE + QK_ROPE)),
               "kv_w_up": (KV_LORA, H_B * (QK_NOPE + V_HEAD))}
ROW_SHARDED = {"a_w_out": (WIDTH_A, D_MODEL), "b_w_out": (WIDTH_B, D_MODEL), "kv_w_down": (D_MODEL, KV_LORA + QK_ROPE)}


def _full_from_gathered(gathered, shards=SHARD_ROWS):
    out, off = {}, 0
    for name, rows in shards:
        part = gathered[:, off:off + rows]
        off += rows
        if name in COL_SHARDED:
            r, c = COL_SHARDED[name]
            out[name] = part.reshape(N_CHIPS, r, c // N_CHIPS).transpose(1, 0, 2).reshape(r, c)
        else:
            r, c = ROW_SHARDED[name]
            out[name] = part.reshape(r, c)
    return out


def _gathered_from_full(full, shards, total_rows):
    parts = []
    for name, rows in shards:
        g = full[name]
        if name in COL_SHARDED:
            r, c = COL_SHARDED[name]
            g = g.reshape(r, N_CHIPS, c // N_CHIPS).transpose(1, 0, 2)
        parts.append(g.reshape(N_CHIPS, rows, FLAT_COLS))
    used = sum(rows for _, rows in shards)
    if total_rows > used:
        parts.append(jnp.zeros((N_CHIPS, total_rows - used, FLAT_COLS), F32))
    return jnp.concatenate(parts, axis=1)


A_SHARDS, B_SHARDS = SHARD_ROWS[:1], SHARD_ROWS[1:]
A_ROWS = sum(rows for _, rows in A_SHARDS)


def _kernel_layout_a(full):
    return dict(a_in=full["a_w_in"])


def _kernel_layout_b(full):
    uq = full["b_w_uq"].reshape(Q_LORA, H_B, QK_NOPE + QK_ROPE)
    uq = jnp.pad(uq, ((0, 0), (0, 0), (0, QK_PAD - QK_NOPE - QK_ROPE))).reshape(Q_LORA, H_B * QK_PAD)
    up = full["kv_w_up"].reshape(KV_LORA, H_B, QK_NOPE + V_HEAD)
    down = jnp.pad(full["kv_w_down"], ((0, 0), (0, KV_DOWN_PAD - KV_LORA - QK_ROPE)))
    return dict(a_out=full["a_w_out"], b_out=full["b_w_out"], uq=uq,
                up_k=up[:, :, :QK_NOPE].reshape(KV_LORA, H_B * QK_NOPE),
                up_v=up[:, :, QK_NOPE:].reshape(KV_LORA, H_B * V_HEAD),
                b_in1=jnp.concatenate([down, full["b_w_in"]], axis=1))


def _kernel_layout(full):
    return {**_kernel_layout_a(full), **_kernel_layout_b(full)}


EARLY_SHARDS = tuple(sh for sh in SHARD_ROWS if sh[0] != "a_w_in")
LATE_SHARDS = tuple(sh for sh in SHARD_ROWS if sh[0] == "a_w_in")
EARLY_ROWS, LATE_ROWS = 3072, 1536


def _reference_layout_grads(g):
    uq = g["uq"].reshape(Q_LORA, H_B, QK_PAD)[:, :, :QK_NOPE + QK_ROPE].reshape(Q_LORA, H_B * (QK_NOPE + QK_ROPE))
    up = jnp.concatenate([g["up_k"].reshape(KV_LORA, H_B, QK_NOPE), g["up_v"].reshape(KV_LORA, H_B, V_HEAD)], axis=2)
    return dict(a_w_out=g["a_out"], b_w_out=g["b_out"], b_w_uq=uq,
                kv_w_up=up.reshape(KV_LORA, H_B * (QK_NOPE + V_HEAD)),
                kv_w_down=g["b_in1"][:, :KV_LORA + QK_ROPE], b_w_in=g["b_in1"][:, KV_DOWN_PAD:])


HBM_SPEC = pl.BlockSpec(memory_space=pl.ANY)


def _me():
    return lax.axis_index("x"), lax.axis_index("y"), lax.axis_index("c")


def _chip_flips(x, y):
    return [(1 - x, y), (x, 1 - y), (1 - x, 1 - y)]


def _gather_copies(src_ref, out_ref, send_sems, recv_sems):
    x, y, c = _me()
    half = src_ref.shape[0] // 2
    my_rows = pl.ds(pl.multiple_of(c * half, 16), half)
    their_rows = pl.ds(pl.multiple_of((1 - c) * half, 16), half)
    chips = _chip_flips(x, y)
    sibling = (x, y, 1 - c)

    def copy(k, src, dst, to):
        return pltpu.make_async_remote_copy(src_ref=src, dst_ref=dst, send_sem=send_sems.at[k], recv_sem=recv_sems.at[k],
                                            device_id=to, device_id_type=MESH)

    sends = [copy(k, src_ref.at[my_rows, :], out_ref.at[2 * x + y, my_rows, :], (px, py, c)) for k, (px, py) in enumerate(chips)]
    landed = [out_ref.at[2 * px + py, my_rows, :] for px, py in chips]
    lands = [copy(k, landed[k], landed[k], (px, py, c)) for k, (px, py) in enumerate(chips)]
    forwards = [copy(3 + k, landed[k], landed[k], sibling) for k in range(3)]
    theirs = [out_ref.at[2 * px + py, their_rows, :] for px, py in chips]
    arrivals = [copy(3 + k, theirs[k], theirs[k], sibling) for k in range(3)]
    return sends, lands, forwards, arrivals


def _gather_start(src_ref, out_ref, send_sems, recv_sems):
    sends, _, _, _ = _gather_copies(src_ref, out_ref, send_sems, recv_sems)
    for cp in sends:
        cp.start()


def _gather_finish(src_ref, out_ref, send_sems, recv_sems):
    sends, lands, forwards, arrivals = _gather_copies(src_ref, out_ref, send_sems, recv_sems)
    for k in range(3):
        lands[k].wait_recv()
        forwards[k].start()
    for cp in arrivals:
        cp.wait_recv()
    for cp in sends + forwards:
        cp.wait_send()


def _gather_scratch():
    return [pltpu.SemaphoreType.DMA((6,)), pltpu.SemaphoreType.DMA((6,))]


def _gather_weights(flat16):
    def body(src_ref, out_ref, send_sems, recv_sems):
        _gather_start(src_ref, out_ref, send_sems, recv_sems)
        _gather_finish(src_ref, out_ref, send_sems, recv_sems)

    return pl.pallas_call(
        body, name="gather_weights",
        in_specs=[HBM_SPEC], out_specs=HBM_SPEC,
        out_shape=jax.ShapeDtypeStruct((N_CHIPS,) + flat16.shape, flat16.dtype),
        scratch_shapes=_gather_scratch(),
    )(flat16)


def _gather_side(flat16):
    return dict(inputs=[flat16], out_shape=jax.ShapeDtypeStruct((N_CHIPS,) + flat16.shape, flat16.dtype),
                scratch=_gather_scratch(), start=_gather_start, finish=_gather_finish)


def _chip_exchange_copies(p_ref, out_ref, send_sems, recv_sems):
    x, y, c = _me()
    return [pltpu.make_async_remote_copy(
        src_ref=p_ref.at[2 * px + py], dst_ref=out_ref.at[k], send_sem=send_sems.at[k], recv_sem=recv_sems.at[k],
        device_id=(px, py, c), device_id_type=MESH) for k, (px, py) in enumerate(_chip_flips(x, y))]


def _chip_exchange_start(p_ref, out_ref, send_sems, recv_sems):
    for cp in _chip_exchange_copies(p_ref, out_ref, send_sems, recv_sems):
        cp.start()


def _chip_exchange_finish(p_ref, out_ref, send_sems, recv_sems):
    copies = _chip_exchange_copies(p_ref, out_ref, send_sems, recv_sems)
    for cp in copies:
        cp.wait_send()
    for cp in copies:
        cp.wait_recv()


def _chip_exchange_side(p):
    return dict(inputs=[p], out_shape=jax.ShapeDtypeStruct((3,) + p.shape[1:], p.dtype),
                scratch=[pltpu.SemaphoreType.DMA((3,)), pltpu.SemaphoreType.DMA((3,))],
                start=_chip_exchange_start, finish=_chip_exchange_finish)


def _pair_swap(r, name):
    def body(r_ref, out_ref, send_sem, recv_sem):
        x, y, c = _me()
        cp = pltpu.make_async_remote_copy(src_ref=r_ref, dst_ref=out_ref, send_sem=send_sem, recv_sem=recv_sem,
                                          device_id=(x, y, 1 - c), device_id_type=MESH)
        cp.start()
        cp.wait_send()
        cp.wait_recv()

    return pl.pallas_call(
        body, name=name,
        in_specs=[HBM_SPEC], out_specs=HBM_SPEC,
        out_shape=jax.ShapeDtypeStruct(r.shape, r.dtype),
        scratch_shapes=[pltpu.SemaphoreType.DMA, pltpu.SemaphoreType.DMA],
    )(r)


def _sum_small(vec):
    def body(v_ref, out_ref, slots, send_sems, recv_sems):
        x, y, c = _me()
        me = 4 * x + 2 * y + c
        slots[me] = v_ref[...]
        flips = [(fx, fy, fc) for fx in (0, 1) for fy in (0, 1) for fc in (0, 1)][1:]
        copies = []
        for k, (fx, fy, fc) in enumerate(flips):
            copies.append(pltpu.make_async_remote_copy(
                src_ref=v_ref, dst_ref=slots.at[me], send_sem=send_sems.at[k], recv_sem=recv_sems.at[k],
                device_id=(x ^ fx, y ^ fy, c ^ fc), device_id_type=MESH))
        for cp in copies:
            cp.start()
        for cp in copies:
            cp.wait_send()
        for k, (fx, fy, fc) in enumerate(flips):
            src = 4 * (x ^ fx) + 2 * (y ^ fy) + (c ^ fc)
            pltpu.make_async_remote_copy(
                src_ref=v_ref, dst_ref=slots.at[src], send_sem=send_sems.at[k], recv_sem=recv_sems.at[k],
                device_id=(x ^ fx, y ^ fy, c ^ fc), device_id_type=MESH).wait_recv()
        total = slots[0]
        for d in range(1, N_DEV):
            total = total + slots[d]
        out_ref[...] = total

    return pl.pallas_call(
        body, name="sum_small",
        in_specs=[pl.BlockSpec(memory_space=pltpu.VMEM)], out_specs=pl.BlockSpec(memory_space=pltpu.VMEM),
        out_shape=jax.ShapeDtypeStruct(vec.shape, vec.dtype),
        scratch_shapes=[pltpu.VMEM((N_DEV,) + vec.shape, vec.dtype), pltpu.SemaphoreType.DMA((7,)),
                        pltpu.SemaphoreType.DMA((7,))],
    )(vec)


UPD_ROWS = 256


def _pair_sum(g, theirs, core, chip, name):
    half = theirs.shape[1]
    nb = half // UPD_ROWS

    def body(core_ref, chip_ref, g_ref, t_ref, own_ref, o16_ref):
        total = g_ref[0] + t_ref[0].astype(F32)
        o16_ref[0] = total.astype(o16_ref.dtype)

        @pl.when(pl.program_id(1) == chip_ref[0])
        def _():
            own_ref[...] = total

    return pl.pallas_call(
        body, name=name,
        grid_spec=pltpu.PrefetchScalarGridSpec(
            num_scalar_prefetch=2, grid=(nb, N_CHIPS),
            in_specs=[pl.BlockSpec((1, UPD_ROWS, FLAT_COLS), lambda i, d, core_ref, chip_ref: (d, core_ref[0] * nb + i, 0)),
                      pl.BlockSpec((1, UPD_ROWS, FLAT_COLS), lambda i, d, core_ref, chip_ref: (d, i, 0))],
            out_specs=[pl.BlockSpec((UPD_ROWS, FLAT_COLS), lambda i, d, core_ref, chip_ref: (i, 0)),
                       pl.BlockSpec((1, UPD_ROWS, FLAT_COLS), lambda i, d, core_ref, chip_ref: (d, i, 0))]),
        out_shape=[jax.ShapeDtypeStruct((half, FLAT_COLS), F32),
                   jax.ShapeDtypeStruct((N_CHIPS, half, FLAT_COLS), BF16)],
        compiler_params=_params(("parallel", "arbitrary")),
    )(core, chip, g, theirs)


def _chip_sum(own, received, name):
    half = own.shape[0]
    nb = half // UPD_ROWS

    def body(p_ref, r_ref, o_ref):
        o_ref[...] = ((p_ref[...] + r_ref[0].astype(F32)) + r_ref[1].astype(F32)) + r_ref[2].astype(F32)

    return pl.pallas_call(
        body, name=name, grid=(nb,),
        in_specs=[pl.BlockSpec((UPD_ROWS, FLAT_COLS), lambda i: (i, 0)),
                  pl.BlockSpec((3, UPD_ROWS, FLAT_COLS), lambda i: (0, i, 0))],
        out_specs=pl.BlockSpec((UPD_ROWS, FLAT_COLS), lambda i: (i, 0)),
        out_shape=jax.ShapeDtypeStruct((half, FLAT_COLS), F32),
        compiler_params=_params(("parallel",)),
    )(own, received)


def _adamw(w, g, m, v, *, rows, name):
    r, c = w.shape
    rows = min(rows, r)
    assert r % rows == 0

    def body(w_ref, g_ref, m_ref, v_ref, d_ref, nm_ref, nv_ref):
        gv = g_ref[...]
        nm = ADAM_B1 * m_ref[...] + (1.0 - ADAM_B1) * gv
        nv = ADAM_B2 * v_ref[...] + (1.0 - ADAM_B2) * (gv * gv)
        m_hat = nm / (1.0 - ADAM_B1 ** ADAM_STEP)
        v_hat = nv / (1.0 - ADAM_B2 ** ADAM_STEP)
        d_ref[...] = -ADAM_LR * (m_hat / (jnp.sqrt(v_hat) + ADAM_EPS) + ADAM_WD * w_ref[...])
        nm_ref[...] = nm
        nv_ref[...] = nv

    spec = pl.BlockSpec((rows, c), lambda i: (i, 0))
    return pl.pallas_call(
        body, name=name, grid=(r // rows,),
        in_specs=[spec] * 4, out_specs=[spec] * 3,
        out_shape=[jax.ShapeDtypeStruct((r, c), F32)] * 3,
        compiler_params=_params(("parallel",)),
    )(w, g, m, v)


W_NAMES = ("a_w_in", "a_w_out", "b_w_in", "b_q_norm", "b_w_uq", "b_w_out", "kv_w_down", "kv_norm", "kv_w_up", "ln_g", "ln_b")
BIG = tuple(name for name, _ in SHARD_ROWS)


def _pack_small(ln_g, ln_b, q_norm, kv_norm, extra=None):
    pad = lambda a: jnp.pad(a.reshape(1, -1), ((0, 0), (0, FLAT_COLS - a.size)))
    rows = [ln_g, ln_b, pad(q_norm), pad(kv_norm),
            jnp.zeros((1, FLAT_COLS), F32) if extra is None else pad(extra), jnp.zeros((1, FLAT_COLS), F32)]
    return jnp.concatenate(rows, axis=0)


def _unpack_small(p):
    return dict(ln_g=p[0:2], ln_b=p[2:4], b_q_norm=p[4:5, :Q_LORA], kv_norm=p[5, :KV_LORA])


def kernel(x, a_w_in, a_w_out, b_w_in, b_q_norm, b_w_uq, b_w_out, kv_w_down, kv_norm, kv_w_up, ln_g, ln_b, loss_target, m_a_w_in, m_a_w_out, m_b_w_in, m_b_q_norm, m_b_w_uq, m_b_w_out, m_kv_w_down, m_kv_norm, m_kv_w_up, m_ln_g, m_ln_b, v_a_w_in, v_a_w_out, v_b_w_in, v_b_q_norm, v_b_w_uq, v_b_w_out, v_kv_w_down, v_kv_norm, v_kv_w_up, v_ln_g, v_ln_b):
    w_in = dict(a_w_in=a_w_in[0], a_w_out=a_w_out[0], b_w_in=b_w_in[0], b_w_uq=b_w_uq[0], b_w_out=b_w_out[0],
                kv_w_down=kv_w_down, kv_w_up=kv_w_up)
    m_in = dict(a_w_in=m_a_w_in[0], a_w_out=m_a_w_out[0], b_w_in=m_b_w_in[0], b_w_uq=m_b_w_uq[0], b_w_out=m_b_w_out[0],
                kv_w_down=m_kv_w_down, kv_w_up=m_kv_w_up)
    v_in = dict(a_w_in=v_a_w_in[0], a_w_out=v_a_w_out[0], b_w_in=v_b_w_in[0], b_w_uq=v_b_w_uq[0], b_w_out=v_b_w_out[0],
                kv_w_down=v_kv_w_down, kv_w_up=v_kv_w_up)
    shard_shapes = {name: w_in[name].shape for name in BIG}

    cx, cy, cc = lax.axis_index("x"), lax.axis_index("y"), lax.axis_index("c")
    flat16 = _flat_shards(w_in, BF16)
    chip = 2 * cx + cy
    flat_a, flat_b = flat16[:A_ROWS], flat16[A_ROWS:]
    got_a = lax.dynamic_update_slice(_gather_weights(flat_a), flat_a[None], (chip, 0, 0))
    weights_a = _kernel_layout_a(_full_from_gathered(got_a, A_SHARDS))

    core_arr, chip_arr = cc.astype(jnp.int32).reshape(1), chip.astype(jnp.int32).reshape(1)

    def reduce_pair(full, shards, rows, tag):
        g_all = _gathered_from_full(full, shards, rows)
        half = rows // 2
        other_half = lax.dynamic_slice(g_all, (0, (1 - cc) * half, 0), (N_CHIPS, half, FLAT_COLS)).astype(BF16)
        theirs = _pair_swap(other_half, "pair_exchange_" + tag)
        return _pair_sum(g_all, theirs, core_arr, chip_arr, "pair_sum_" + tag)

    loss, grad_x, reduced, small = _local_step(x[0], loss_target[0], weights_a, kv_norm.reshape(1, -1), b_q_norm, ln_g, ln_b,
                                               late=(flat_b, chip), reduce=reduce_pair)

    g_big = {}
    for (own, received), shards, tag in zip(reduced, (EARLY_SHARDS, LATE_SHARDS), ("early", "late")):
        mine = _chip_sum(own, received, "chip_sum_" + tag)
        sibling = _pair_swap(mine, "pair_share_" + tag)
        g_flat = jnp.concatenate([jnp.where(cc == 0, mine, sibling), jnp.where(cc == 0, sibling, mine)], axis=0)
        g_big.update(_unflat_shards(g_flat, shard_shapes, shards))

    small_sum = _sum_small(_pack_small(small["ln_g"], small["ln_b"], small["q_norm"], small["kv_norm"], loss[:, :1]))
    loss_out = small_sum[6, 0]

    upd = {name: _adamw(w_in[name], g_big[name], m_in[name], v_in[name], rows=256, name="adamw_" + name) for name in BIG}
    ds, nms, nvs = _adamw(_pack_small(ln_g, ln_b, b_q_norm, kv_norm), small_sum.at[6].set(0.0),
                          _pack_small(m_ln_g, m_ln_b, m_b_q_norm, m_kv_norm),
                          _pack_small(v_ln_g, v_ln_b, v_b_q_norm, v_kv_norm), rows=8, name="adamw_small")

    def assemble(big, small_packed):
        sm = _unpack_small(small_packed)
        out = {}
        for name in W_NAMES:
            if name in big:
                out[name] = big[name][None] if name in ("a_w_in", "a_w_out", "b_w_in", "b_w_uq", "b_w_out") else big[name]
            else:
                out[name] = sm[name]
        return [out[name] for name in W_NAMES]

    part = lambda k: {name: upd[name][k] for name in BIG}
    return (loss_out, grad_x[None], *assemble(g_big, small_sum), *assemble(part(0), ds), *assemble(part(1), nms),
            *assemble(part(2), nvs))
```

```python
import functools
import math

import jax
import jax.numpy as jnp
from jax import lax
from jax.experimental import pallas as pl
from jax.experimental.pallas import tpu as pltpu

F32 = jnp.float32
BF16 = jnp.bfloat16
MESH = pl.DeviceIdType.MESH

D_MODEL = 1024
DEPTH = 2
H_A, DK_A, DV_A = 4, 256, 512
WIDTH_A = H_A * DV_A
CHUNK = 128
H_B, QK_NOPE, QK_ROPE, V_HEAD = 16, 128, 64, 128
QK_PAD = 256
Q_LORA, KV_LORA = 768, 512
KV_DOWN_PAD = 640
WIDTH_B = H_B * V_HEAD
IN_A = 2 * H_A * DK_A + 2 * WIDTH_A
IN_B = Q_LORA + WIDTH_B
H1_B = KV_DOWN_PAD + IN_B
ROPE_BASE = 10000.0
ALPHA = (2.0 * DEPTH) ** 0.25
ATT_SCALE = (QK_NOPE + QK_ROPE) ** -0.5
NEG_BIG = -1e30

ADAM_LR, ADAM_B1, ADAM_B2, ADAM_EPS, ADAM_WD, ADAM_STEP = 0.001, 0.9, 0.999, 1e-08, 0.01, 10

VMEM_LIMIT_BYTES = 56 * 1024 * 1024
LANES = 128
FLAT_COLS = 1024
SHARD_ROWS = (("a_w_in", 1536), ("a_w_out", 512), ("b_w_in", 704), ("b_w_uq", 576), ("b_w_out", 512),
              ("kv_w_down", 144), ("kv_w_up", 512))
FLAT_ROWS = 4608
HALF_ROWS = FLAT_ROWS // 2
N_CHIPS = 4
N_DEV = 8


def _params(sem, vmem=VMEM_LIMIT_BYTES):
    return pltpu.CompilerParams(dimension_semantics=sem, vmem_limit_bytes=vmem)


def _row_spec(ts, w, col_block=0):
    return pl.BlockSpec((ts, w), lambda i: (i, col_block))


def _bc_spec(shape):
    nd = len(shape)
    return pl.BlockSpec(shape, lambda i: (0,) * nd)


def _sigmoid(x):
    return 1.0 / (1.0 + jnp.exp(-x))


def _fold8(v):
    ts, w = v.shape
    return jnp.sum(v.reshape(ts // 8, 8, w), axis=0)


def _mm(a, b, *, ta=False, tb=False, out_dtype=F32, tm=1024, tn=512, tk=None, name, extras=(), epilogue=None, out_tn=None,
        side=None):
    if ta:
        K, M = a.shape
    else:
        M, K = a.shape
    if tb:
        N, Kb = b.shape
    else:
        Kb, N = b.shape
    assert K == Kb, (a.shape, b.shape)
    tm, tn = min(tm, M), min(tn, N)
    tk = K if tk is None else min(tk, K)
    assert M % tm == 0 and N % tn == 0 and K % tk == 0, (name, M, N, K, tm, tn, tk)
    grid = (M // tm, N // tn, K // tk)
    nk = grid[2]
    out_tn = tn if out_tn is None else out_tn
    n_extra = len(extras)
    side_inputs = [] if side is None else list(side["inputs"])
    n_side = len(side_inputs)
    n_acc = 0 if nk == 1 else 1
    dims = (((0,) if ta else (1,), (1,) if tb else (0,)), ((), ()))

    def body(a_ref, b_ref, *rest):
        extra_refs, rest = rest[:n_extra], rest[n_extra:]
        side_in, o_ref, rest = rest[:n_side], rest[n_side], rest[n_side + 1:]
        if side is not None:
            side_refs, rest = side_in + rest[:1] + rest[1 + n_acc:], rest[1:]
            ids = [pl.program_id(d) for d in range(3)]

            @pl.when((ids[0] == 0) & (ids[1] == 0) & (ids[2] == 0))
            def _():
                side["start"](*side_refs)

        prod = lax.dot_general(a_ref[...].astype(BF16), b_ref[...].astype(BF16), dims,
                               preferred_element_type=F32)

        def store(tile):
            if epilogue is None:
                o_ref[...] = tile.astype(o_ref.dtype)
            else:
                epilogue(tile, o_ref, *extra_refs)

        if nk == 1:
            store(prod)
        else:
            acc = rest[0]
            k = pl.program_id(2)

            @pl.when(k == 0)
            def _():
                acc[...] = prod

            @pl.when(k > 0)
            def _():
                acc[...] += prod

            @pl.when(k == nk - 1)
            def _():
                store(acc[...])

        if side is not None:
            @pl.when((ids[0] == grid[0] - 1) & (ids[1] == grid[1] - 1) & (ids[2] == grid[2] - 1))
            def _():
                side["finish"](*side_refs)

    a_spec = pl.BlockSpec((tk, tm), lambda i, j, k: (k, i)) if ta else pl.BlockSpec((tm, tk), lambda i, j, k: (i, k))
    b_spec = pl.BlockSpec((tn, tk), lambda i, j, k: (j, k)) if tb else pl.BlockSpec((tk, tn), lambda i, j, k: (k, j))
    extra_specs = [pl.BlockSpec((tm, e.shape[1]), lambda i, j, k: (i, 0)) for e in extras]
    out_specs = pl.BlockSpec((tm, out_tn), lambda i, j, k: (i, j))
    out_shape = jax.ShapeDtypeStruct((M, (N // tn) * out_tn), out_dtype)
    scratch = [] if nk == 1 else [pltpu.VMEM((tm, tn), F32)]
    if side is not None:
        out_specs, out_shape, scratch = [out_specs, HBM_SPEC], [out_shape, side["out_shape"]], scratch + list(side["scratch"])
    return pl.pallas_call(
        body, name=name, grid=grid,
        in_specs=[a_spec, b_spec] + extra_specs + [HBM_SPEC] * n_side,
        out_specs=out_specs, out_shape=out_shape, scratch_shapes=scratch,
        compiler_params=_params(("parallel", "parallel", "arbitrary") if side is None else ("arbitrary",) * 3),
    )(a, b, *extras, *side_inputs)


def _rope_tables_a(s):
    half = DK_A // 2
    inv = ROPE_BASE ** (-jnp.arange(half, dtype=F32) / half)
    ang = jnp.arange(s, dtype=F32)[:, None] * inv[None, :]
    return jnp.cos(ang), jnp.sin(ang)


def _rope_tables_b(s):
    half = QK_ROPE // 2
    inv = ROPE_BASE ** (-jnp.arange(half, dtype=F32) / half)
    ang = jnp.arange(s, dtype=F32)[:, None] * inv[None, :]
    c, sn = jnp.cos(ang), jnp.sin(ang)
    z = jnp.zeros_like(c)
    cos = jnp.concatenate([c, c, z, z], axis=1)
    sa = jnp.concatenate([-sn, z, z, z], axis=1)
    sb = jnp.concatenate([z, sn, z, z], axis=1)
    return cos, sa, sb


def _rope_b(r, cos, sa, sb, sign):
    return r * cos + sign * (pltpu.roll(r, 96, 1) * sa + pltpu.roll(r, 32, 1) * sb)


def _retention_tables():
    lg = jnp.log1p(-jnp.exp2(-5.0 - jnp.arange(H_A, dtype=F32)))
    idx = jnp.arange(CHUNK, dtype=F32)
    diff = idx[:, None] - idx[None, :]
    causal = diff >= 0
    dmat = jnp.where(causal, jnp.exp(jnp.where(causal, diff, 0.0)[None] * lg[:, None, None]), 0.0)
    qdec = jnp.exp((idx + 1.0)[None, :] * lg[:, None])[:, :, None]
    kdec = jnp.exp((CHUNK - 1.0 - idx)[None, :] * lg[:, None])[:, :, None]
    cdec = jnp.broadcast_to(jnp.exp(CHUNK * lg)[:, None, None], (H_A, 1, DV_A))
    return dmat, qdec, kdec, cdec


def _group_norm(o):
    mu = jnp.mean(o, axis=-1, keepdims=True)
    oc = o - mu
    var = jnp.mean(oc * oc, axis=-1, keepdims=True)
    rstd = lax.rsqrt(var + 1e-5)
    return oc * rstd, rstd


Q_COL, K_COL, V_COL, GATE_COL = 0, H_A * DK_A, 2 * H_A * DK_A, 2 * H_A * DK_A + WIDTH_A


def _ln_stats(z):
    mu = jnp.mean(z, axis=-1, keepdims=True)
    zc = z - mu
    var = jnp.mean(zc * zc, axis=-1, keepdims=True)
    rstd = lax.rsqrt(var + 1e-5)
    return zc * rstd, rstd


def _ln_bwd(dy, xhat, rstd, g):
    dxh = dy * g
    m1 = jnp.mean(dxh, axis=-1, keepdims=True)
    m2 = jnp.mean(dxh * xhat, axis=-1, keepdims=True)
    return rstd * (dxh - m1 - xhat * m2)


def _ln_fwd(x, y, g, b, *, ts):
    s = x.shape[0]

    def body(x_ref, y_ref, g_ref, b_ref, o_ref, o16_ref):
        xhat, _ = _ln_stats(ALPHA * x_ref[...] + y_ref[...])
        out = xhat * g_ref[...] + b_ref[...]
        o_ref[...] = out
        o16_ref[...] = out.astype(o16_ref.dtype)

    return pl.pallas_call(
        body, name="ln_fwd", grid=(s // ts,),
        in_specs=[_row_spec(ts, D_MODEL), _row_spec(ts, D_MODEL), _bc_spec((1, D_MODEL)), _bc_spec((1, D_MODEL))],
        out_specs=[_row_spec(ts, D_MODEL), _row_spec(ts, D_MODEL)],
        out_shape=[jax.ShapeDtypeStruct((s, D_MODEL), F32), jax.ShapeDtypeStruct((s, D_MODEL), BF16)],
        compiler_params=_params(("parallel",)),
    )(x, y, g, b)


def _ln_loss_bwd(x1, y, target, g, b, *, ts):
    s = x1.shape[0]
    n = s // ts

    def body(x_ref, y_ref, t_ref, g_ref, b_ref, dz_ref, dz16_ref, dg_ref, db_ref, loss_ref, ag, ab, al):
        i = pl.program_id(0)

        @pl.when(i == 0)
        def _():
            ag[...] = jnp.zeros_like(ag)
            ab[...] = jnp.zeros_like(ab)
            al[...] = jnp.zeros_like(al)

        xhat, rstd = _ln_stats(ALPHA * x_ref[...] + y_ref[...])
        err = xhat * g_ref[...] + b_ref[...] - t_ref[...]
        al[...] += _fold8(err * err)
        dy = err * (1.0 / D_MODEL)
        ag[...] += _fold8(dy * xhat)
        ab[...] += _fold8(dy)
        dz = _ln_bwd(dy, xhat, rstd, g_ref[...])
        dz_ref[...] = dz
        dz16_ref[...] = dz.astype(dz16_ref.dtype)

        @pl.when(i == n - 1)
        def _():
            dg_ref[...] = jnp.sum(ag[...], axis=0, keepdims=True)
            db_ref[...] = jnp.sum(ab[...], axis=0, keepdims=True)
            loss_ref[...] = jnp.full((1, LANES), (0.5 / D_MODEL) * jnp.sum(al[...]), F32)

    return pl.pallas_call(
        body, name="ln_loss_bwd", grid=(n,),
        in_specs=[_row_spec(ts, D_MODEL)] * 3 + [_bc_spec((1, D_MODEL))] * 2,
        out_specs=[_row_spec(ts, D_MODEL), _row_spec(ts, D_MODEL), _bc_spec((1, D_MODEL)), _bc_spec((1, D_MODEL)),
                   _bc_spec((1, LANES))],
        out_shape=[jax.ShapeDtypeStruct((s, D_MODEL), F32), jax.ShapeDtypeStruct((s, D_MODEL), BF16),
                   jax.ShapeDtypeStruct((1, D_MODEL), F32), jax.ShapeDtypeStruct((1, D_MODEL), F32),
                   jax.ShapeDtypeStruct((1, LANES), F32)],
        scratch_shapes=[pltpu.VMEM((8, D_MODEL), F32)] * 3,
        compiler_params=_params(("arbitrary",)),
    )(x1, y, target, g, b)


def _ln_bwd_call(dy, x, y, g, *, ts):
    s = x.shape[0]
    n = s // ts

    def body(dy_ref, x_ref, y_ref, g_ref, dz_ref, dz16_ref, dg_ref, db_ref, ag, ab):
        i = pl.program_id(0)

        @pl.when(i == 0)
        def _():
            ag[...] = jnp.zeros_like(ag)
            ab[...] = jnp.zeros_like(ab)

        xhat, rstd = _ln_stats(ALPHA * x_ref[...] + y_ref[...])
        dy = dy_ref[...]
        ag[...] += _fold8(dy * xhat)
        ab[...] += _fold8(dy)
        dz = _ln_bwd(dy, xhat, rstd, g_ref[...])
        dz_ref[...] = dz
        dz16_ref[...] = dz.astype(dz16_ref.dtype)

        @pl.when(i == n - 1)
        def _():
            dg_ref[...] = jnp.sum(ag[...], axis=0, keepdims=True)
            db_ref[...] = jnp.sum(ab[...], axis=0, keepdims=True)

    return pl.pallas_call(
        body, name="ln_bwd", grid=(n,),
        in_specs=[_row_spec(ts, D_MODEL)] * 3 + [_bc_spec((1, D_MODEL))],
        out_specs=[_row_spec(ts, D_MODEL), _row_spec(ts, D_MODEL), _bc_spec((1, D_MODEL)), _bc_spec((1, D_MODEL))],
        out_shape=[jax.ShapeDtypeStruct((s, D_MODEL), F32), jax.ShapeDtypeStruct((s, D_MODEL), BF16),
                   jax.ShapeDtypeStruct((1, D_MODEL), F32), jax.ShapeDtypeStruct((1, D_MODEL), F32)],
        scratch_shapes=[pltpu.VMEM((8, D_MODEL), F32)] * 2,
        compiler_params=_params(("arbitrary",)),
    )(dy, x, y, g)


def _residual_store(tile, o_ref, dz_ref):
    o_ref[...] = ALPHA * dz_ref[...] + tile


C_LAT = slice(0, KV_LORA)
C_ROPE = slice(KV_LORA, KV_DOWN_PAD)
C_QL = slice(KV_DOWN_PAD, KV_DOWN_PAD + Q_LORA)
C_GATE = slice(KV_DOWN_PAD + Q_LORA, H1_B)


def _rms(x, eps=1e-6):
    r = lax.rsqrt(jnp.mean(x * x, axis=-1, keepdims=True) + eps)
    return x * r, r


def _rms_bwd(dy, xhat, r, g):
    dxh = dy * g
    return r * (dxh - xhat * jnp.mean(dxh * xhat, axis=-1, keepdims=True))


def _kvq_prep(h1, kv_norm, q_norm, cos, sa, sb, *, ts):
    s = h1.shape[0]

    def body(h_ref, kn_ref, qn_ref, c_ref, sa_ref, sb_ref, lat_ref, kr_ref, ql_ref):
        lat, _ = _rms(h_ref[:, C_LAT])
        lat_ref[...] = (lat * kn_ref[...]).astype(lat_ref.dtype)
        kr_ref[...] = _rope_b(h_ref[:, C_ROPE], c_ref[...], sa_ref[...], sb_ref[...], 1.0).astype(kr_ref.dtype)
        ql, _ = _rms(h_ref[:, C_QL])
        ql_ref[...] = (ql * qn_ref[...]).astype(ql_ref.dtype)

    return pl.pallas_call(
        body, name="kvq_prep", grid=(s // ts,),
        in_specs=[_row_spec(ts, H1_B), _bc_spec((1, KV_LORA)), _bc_spec((1, Q_LORA))] + [_row_spec(ts, 128)] * 3,
        out_specs=[_row_spec(ts, KV_LORA), _row_spec(ts, 128), _row_spec(ts, Q_LORA)],
        out_shape=[jax.ShapeDtypeStruct((s, KV_LORA), BF16), jax.ShapeDtypeStruct((s, 128), BF16),
                   jax.ShapeDtypeStruct((s, Q_LORA), BF16)],
        compiler_params=_params(("parallel",)),
    )(h1, kv_norm, q_norm, cos, sa, sb)


LOG2E = 1.4426950408889634
LN2 = 0.6931471805599453
Q_SCALE = ATT_SCALE * LOG2E


def _rope_q_store(tile, o_ref, c_ref, sa_ref, sb_ref):
    c, a, b = c_ref[...], sa_ref[...], sb_ref[...]
    for hd in range(tile.shape[1] // QK_PAD):
        lo = hd * QK_PAD
        o_ref[:, lo:lo + 128] = (tile[:, lo:lo + 128] * Q_SCALE).astype(o_ref.dtype)
        o_ref[:, lo + 128:lo + 256] = (_rope_b(tile[:, lo + 128:lo + 256], c, a, b, 1.0) * Q_SCALE).astype(o_ref.dtype)


def _assemble_k_store(tile, o_ref, kr_ref):
    r = kr_ref[...]
    for hd in range(tile.shape[1] // QK_NOPE):
        o_ref[:, hd * QK_PAD:hd * QK_PAD + 128] = tile[:, hd * 128:(hd + 1) * 128].astype(o_ref.dtype)
        o_ref[:, hd * QK_PAD + 128:(hd + 1) * QK_PAD] = r


def _h1_bwd(h1, dlat_k, dlat_v, dkr_heads, dqn, dg16, kv_norm, q_norm, cos, sa, sb, *, ts):
    s = h1.shape[0]
    n = s // ts

    def body(h_ref, dk_ref, dv_ref, dkr_ref, dqn_ref, dg_ref, kn_ref, qn_ref, c_ref, sa_ref, sb_ref,
             o_ref, dkn_ref, dqn_out_ref, akn, aqn):
        i = pl.program_id(0)

        @pl.when(i == 0)
        def _():
            akn[...] = jnp.zeros_like(akn)
            aqn[...] = jnp.zeros_like(aqn)

        lat, r = _rms(h_ref[:, C_LAT])
        dlat = dk_ref[...] + dv_ref[...]
        akn[...] += _fold8(dlat * lat)
        o_ref[:, C_LAT] = _rms_bwd(dlat, lat, r, kn_ref[...]).astype(o_ref.dtype)

        dkr = dkr_ref[:, 0:128]
        for hd in range(1, H_B):
            dkr = dkr + dkr_ref[:, hd * 128:(hd + 1) * 128]
        o_ref[:, C_ROPE] = _rope_b(dkr, c_ref[...], sa_ref[...], sb_ref[...], -1.0).astype(o_ref.dtype)

        ql, rq = _rms(h_ref[:, C_QL])
        dq = dqn_ref[...]
        aqn[...] += _fold8(dq * ql)
        o_ref[:, C_QL] = _rms_bwd(dq, ql, rq, qn_ref[...]).astype(o_ref.dtype)
        o_ref[:, C_GATE] = dg_ref[...]

        @pl.when(i == n - 1)
        def _():
            dkn_ref[...] = jnp.sum(akn[...], axis=0, keepdims=True)
            dqn_out_ref[...] = jnp.sum(aqn[...], axis=0, keepdims=True)

    return pl.pallas_call(
        body, name="h1_bwd", grid=(n,),
        in_specs=[_row_spec(ts, H1_B), _row_spec(ts, KV_LORA), _row_spec(ts, KV_LORA), _row_spec(ts, H_B * 128),
                  _row_spec(ts, Q_LORA), _row_spec(ts, WIDTH_B), _bc_spec((1, KV_LORA)), _bc_spec((1, Q_LORA))]
        + [_row_spec(ts, 128)] * 3,
        out_specs=[_row_spec(ts, H1_B), _bc_spec((1, KV_LORA)), _bc_spec((1, Q_LORA))],
        out_shape=[jax.ShapeDtypeStruct((s, H1_B), BF16), jax.ShapeDtypeStruct((1, KV_LORA), F32),
                   jax.ShapeDtypeStruct((1, Q_LORA), F32)],
        scratch_shapes=[pltpu.VMEM((8, KV_LORA), F32), pltpu.VMEM((8, Q_LORA), F32)],
        compiler_params=_params(("arbitrary",)),
    )(h1, dlat_k, dlat_v, dkr_heads, dqn, dg16, kv_norm, q_norm, cos, sa, sb)


def _dot(a, b, ca, cb):
    return lax.dot_general(a, b, (((ca,), (cb,)), ((), ())), preferred_element_type=F32)


def _table_specs():
    full = lambda shape: pl.BlockSpec(shape, lambda i: (0,) * len(shape))
    return [full((H_A, CHUNK, CHUNK)), full((H_A, CHUNK, 1)), full((H_A, CHUNK, 1)), full((H_A, 1, DV_A))]


def _retention_fwd(h_a, cos, sin, tables):
    s = h_a.shape[0]
    n = s // CHUNK

    def body(h_ref, c_ref, s_ref, dm_ref, qd_ref, kd_ref, cd_ref, q_ref, k_ref, v_ref, o_ref, u_ref, st_ref, state):
        @pl.when(pl.program_id(0) == 0)
        def _():
            state[...] = jnp.zeros_like(state)

        c, sn = c_ref[...], s_ref[...]
        for hd in range(H_A):
            qs, vs = slice(hd * DK_A, (hd + 1) * DK_A), slice(hd * DV_A, (hd + 1) * DV_A)
            for r_ref, base, scale in ((q_ref, Q_COL, 1.0), (k_ref, K_COL, DK_A ** -0.5)):
                lo = base + hd * DK_A
                x1, x2 = h_ref[:, lo:lo + 128], h_ref[:, lo + 128:lo + 256]
                r_ref[:, hd * DK_A:hd * DK_A + 128] = ((x1 * c - x2 * sn) * scale).astype(r_ref.dtype)
                r_ref[:, hd * DK_A + 128:(hd + 1) * DK_A] = ((x2 * c + x1 * sn) * scale).astype(r_ref.dtype)
            v_ref[:, vs] = h_ref[:, V_COL + hd * DV_A:V_COL + (hd + 1) * DV_A].astype(v_ref.dtype)
            qv, kv, vv = q_ref[:, qs], k_ref[:, qs], v_ref[:, vs]
            st = state[hd]
            st16 = st.astype(BF16)
            st_ref[0, hd] = st16
            scores = _dot(qv, kv, 1, 1) * dm_ref[hd]
            qd = (qv.astype(F32) * qd_ref[hd]).astype(BF16)
            o = _dot(scores.astype(BF16), vv, 1, 0) + _dot(qd, st16, 1, 0)
            o_ref[:, vs] = o
            kd = (kv.astype(F32) * kd_ref[hd]).astype(BF16)
            state[hd] = st * cd_ref[hd] + _dot(kd, vv, 0, 0)
            on, _ = _group_norm(o)
            g = h_ref[:, GATE_COL + hd * DV_A:GATE_COL + (hd + 1) * DV_A]
            u_ref[:, vs] = (on * (g * _sigmoid(g))).astype(u_ref.dtype)

    row = lambda w: pl.BlockSpec((CHUNK, w), lambda i: (i, 0))
    return pl.pallas_call(
        body, name="retention_fwd", grid=(n,),
        in_specs=[row(IN_A), row(128), row(128)] + _table_specs(),
        out_specs=[row(H_A * DK_A), row(H_A * DK_A), row(WIDTH_A), row(WIDTH_A), row(WIDTH_A),
                   pl.BlockSpec((1, H_A, DK_A, DV_A), lambda i: (i, 0, 0, 0))],
        out_shape=[jax.ShapeDtypeStruct((s, H_A * DK_A), BF16), jax.ShapeDtypeStruct((s, H_A * DK_A), BF16),
                   jax.ShapeDtypeStruct((s, WIDTH_A), BF16), jax.ShapeDtypeStruct((s, WIDTH_A), F32),
                   jax.ShapeDtypeStruct((s, WIDTH_A), BF16), jax.ShapeDtypeStruct((n, H_A, DK_A, DV_A), BF16)],
        scratch_shapes=[pltpu.VMEM((H_A, DK_A, DV_A), F32)],
        compiler_params=_params(("arbitrary",)),
    )(h_a, cos, sin, *tables)


def _retention_bwd(q, k, v, states, du, o, h_a, cos, sin, tables):
    s = q.shape[0]
    n = s // CHUNK

    def body(q_ref, k_ref, v_ref, st_ref, du_ref, o_ref, g_ref, c_ref, s_ref, dm_ref, qd_ref, kd_ref, cd_ref, dh_ref, grad_state):
        @pl.when(pl.program_id(0) == 0)
        def _():
            grad_state[...] = jnp.zeros_like(grad_state)

        c, sn = c_ref[...], s_ref[...]
        for hd in range(H_A):
            qs, vs = slice(hd * DK_A, (hd + 1) * DK_A), slice(hd * DV_A, (hd + 1) * DV_A)
            on, rstd = _group_norm(o_ref[:, vs])
            g = g_ref[:, vs]
            sg = _sigmoid(g)
            du_v = du_ref[:, vs]
            don = du_v * (g * sg)
            dh_ref[:, GATE_COL + hd * DV_A:GATE_COL + (hd + 1) * DV_A] = (du_v * on * (sg * (1.0 + g * (1.0 - sg)))).astype(dh_ref.dtype)
            m1 = jnp.mean(don, axis=-1, keepdims=True)
            m2 = jnp.mean(don * on, axis=-1, keepdims=True)
            dov = (rstd * (don - m1 - on * m2)).astype(BF16)

            qv, kv, vv = q_ref[:, qs], k_ref[:, qs], v_ref[:, vs]
            dm = dm_ref[hd]
            gs = grad_state[hd]
            g16 = gs.astype(BF16)
            scores = (_dot(qv, kv, 1, 1) * dm).astype(BF16)
            dscores = (_dot(dov, vv, 1, 1) * dm).astype(BF16)
            qd = (qv.astype(F32) * qd_ref[hd]).astype(BF16)
            kd = (kv.astype(F32) * kd_ref[hd]).astype(BF16)
            dq = _dot(dscores, kv, 1, 0) + _dot(dov, st_ref[0, hd], 1, 1) * qd_ref[hd]
            dk = (_dot(dscores, qv, 0, 0) + _dot(vv, g16, 1, 1) * kd_ref[hd]) * (DK_A ** -0.5)
            dh_ref[:, V_COL + hd * DV_A:V_COL + (hd + 1) * DV_A] = (_dot(scores, dov, 0, 0) + _dot(kd, g16, 1, 0)).astype(dh_ref.dtype)
            grad_state[hd] = gs * cd_ref[hd] + _dot(qd, dov, 0, 0)
            for d, base in ((dq, Q_COL), (dk, K_COL)):
                lo = base + hd * DK_A
                d1, d2 = d[:, 0:128], d[:, 128:256]
                dh_ref[:, lo:lo + 128] = (d1 * c + d2 * sn).astype(dh_ref.dtype)
                dh_ref[:, lo + 128:lo + 256] = (d2 * c - d1 * sn).astype(dh_ref.dtype)

    rev = lambda i: n - 1 - i
    row = lambda w, col=0: pl.BlockSpec((CHUNK, w), lambda i: (rev(i), col))
    return pl.pallas_call(
        body, name="retention_bwd", grid=(n,),
        in_specs=[row(H_A * DK_A), row(H_A * DK_A), row(WIDTH_A), pl.BlockSpec((1, H_A, DK_A, DV_A), lambda i: (rev(i), 0, 0, 0)),
                  row(WIDTH_A), row(WIDTH_A), row(WIDTH_A, GATE_COL // WIDTH_A), row(128), row(128)] + _table_specs(),
        out_specs=row(IN_A),
        out_shape=jax.ShapeDtypeStruct((s, IN_A), BF16),
        scratch_shapes=[pltpu.VMEM((H_A, DK_A, DV_A), F32)],
        compiler_params=_params(("arbitrary",)),
    )(q, k, v, states, du, o, h_a, cos, sin, *tables)


GATE_BLOCK0 = (KV_DOWN_PAD + Q_LORA) // LANES


def _causal_mask(sc, row0):
    row = lax.broadcasted_iota(jnp.int32, sc.shape, 0) + row0
    col = lax.broadcasted_iota(jnp.int32, sc.shape, 1)
    return jnp.where(col <= row, sc, NEG_BIG)


def _key_block_loop(step, n, per_trip, smallest=1):
    def trip_body(jj, carry):
        for t in range(per_trip):
            step(per_trip * jj + t)
        return carry

    lax.fori_loop(0, n // per_trip, trip_body, 0)
    group = per_trip // 2
    while group >= smallest:
        def tail(group=group):
            first = (n // (2 * group)) * (2 * group)
            for t in range(group):
                step(first + t)

        pl.when((n // group) % 2 == 1)(tail)
        group //= 2


def _attention_fwd(q, k, v, h1, *, blk, bk, sub, per_trip):
    s = q.shape[0]
    nb = s // blk

    def body(q_ref, k_ref, v_ref, g_ref, o_ref, u_ref, lse_ref, vext_s, m_s, acc_s):
        i = pl.program_id(1)

        @pl.when(i == 0)
        def _():
            vext_s[:, 0:V_HEAD] = v_ref[...]
            vext_s[:, V_HEAD:2 * V_HEAD] = jnp.ones((s, V_HEAD), vext_s.dtype)

        m_s[...] = jnp.full_like(m_s, NEG_BIG)
        acc_s[...] = jnp.zeros_like(acc_s)

        def update(rows, kb, vb, row0):
            sc = _dot(q_ref[rows, :], kb, 1, 1)
            if row0 is not None:
                sc = _causal_mask(sc, row0)
            m_prev = m_s[rows, :]
            m_new = jnp.maximum(m_prev, jnp.max(sc, axis=-1, keepdims=True))
            p = jnp.exp2(sc - jnp.tile(m_new, (1, kb.shape[0] // LANES)))
            a = jnp.exp2(m_prev - m_new)
            acc_s[rows, :] = jnp.tile(a, (1, 2)) * acc_s[rows, :] + _dot(p.astype(BF16), vb, 1, 0)
            m_s[rows, :] = m_new

        def step(j):
            kv_rows = pl.ds(pl.multiple_of(j * bk, bk), bk)
            kb, vb = k_ref[kv_rows, :], vext_s[kv_rows, :]
            for r in range(blk // sub):
                update(slice(r * sub, (r + 1) * sub), kb, vb, None)

        _key_block_loop(step, i * (blk // bk), per_trip, smallest=blk // bk)
        for r in range(blk // sub):
            ncols = (r + 1) * sub
            kv_rows = pl.ds(pl.multiple_of(i * blk, blk), ncols)
            update(slice(r * sub, (r + 1) * sub), k_ref[kv_rows, :], vext_s[kv_rows, :], r * sub)
        acc = acc_s[...]
        l = acc[:, V_HEAD:2 * V_HEAD]
        o = acc[:, 0:V_HEAD] / l
        g = g_ref[...]
        o_ref[...] = o
        u_ref[...] = (o * (g * _sigmoid(g))).astype(u_ref.dtype)
        lse_ref[0] = (m_s[...] + jnp.log2(l))[:, 0:1]

    blk_idx = lambda h, i: (i, h)
    return pl.pallas_call(
        body, name="attention_fwd", grid=(H_B, nb),
        in_specs=[pl.BlockSpec((blk, QK_PAD), blk_idx), pl.BlockSpec((s, QK_PAD), lambda h, i: (0, h)),
                  pl.BlockSpec((s, V_HEAD), lambda h, i: (0, h)), pl.BlockSpec((blk, LANES), lambda h, i: (i, GATE_BLOCK0 + h))],
        out_specs=[pl.BlockSpec((blk, V_HEAD), blk_idx), pl.BlockSpec((blk, V_HEAD), blk_idx),
                   pl.BlockSpec((1, blk, 1), lambda h, i: (h, i, 0))],
        out_shape=[jax.ShapeDtypeStruct((s, WIDTH_B), F32), jax.ShapeDtypeStruct((s, WIDTH_B), BF16),
                   jax.ShapeDtypeStruct((H_B, s, 1), F32)],
        scratch_shapes=[pltpu.VMEM((s, 2 * V_HEAD), BF16), pltpu.VMEM((blk, LANES), F32), pltpu.VMEM((blk, 2 * V_HEAD), F32)],
        compiler_params=_params(("parallel", "arbitrary")),
    )(q, k, v, h1)


def _attention_bwd(q, k, v, du, o, h1, lse, cos, sa, sb, *, blk, bk, per_trip):
    s = q.shape[0]
    nb = s // blk

    def body(q_ref, k_ref, v_ref, du_ref, o_ref, g_ref, lse_ref, c_ref, sa_ref, sb_ref,
             dq_ref, dkn_ref, dkr_ref, dv_ref, dg_ref, lse_s, dl_s, do_s, dq_s):
        i = pl.program_id(1)
        g = g_ref[...]
        sg = _sigmoid(g)
        du_v, ov = du_ref[...], o_ref[...]
        do = du_v * (g * sg)
        dg_ref[...] = (du_v * ov * (sg * (1.0 + g * (1.0 - sg)))).astype(dg_ref.dtype)
        do_s[...] = do.astype(do_s.dtype)
        dl_s[...] = jnp.broadcast_to(jnp.sum(do * ov, axis=-1, keepdims=True), (blk, LANES))
        lse_s[...] = jnp.broadcast_to(lse_ref[0], (blk, LANES))
        dq_s[...] = jnp.zeros_like(dq_s)

        def products(rows, kv_rows, row0):
            qv, dov, kb = q_ref[rows, :], do_s[rows, :], k_ref[kv_rows, :]
            tile = (1, kb.shape[0] // LANES)
            sc = _dot(qv, kb, 1, 1)
            if row0 is not None:
                sc = _causal_mask(sc, row0)
            p = jnp.exp2(sc - jnp.tile(lse_s[rows, :], tile))
            dp = _dot(dov, v_ref[kv_rows, :], 1, 1)
            ds = (p * (dp - jnp.tile(dl_s[rows, :], tile))).astype(BF16)
            dq_s[rows, :] += _dot(ds, kb, 1, 0)
            return _dot(ds, qv, 0, 0), _dot(p.astype(BF16), dov, 0, 0)

        def put(kv_rows, dk_c, dv_c, first):
            if first:
                dkn_ref[kv_rows, :] = dk_c[:, 0:128]
                dkr_ref[kv_rows, :] = dk_c[:, 128:256]
                dv_ref[kv_rows, :] = dv_c
            else:
                dkn_ref[kv_rows, :] += dk_c[:, 0:128]
                dkr_ref[kv_rows, :] += dk_c[:, 128:256]
                dv_ref[kv_rows, :] += dv_c

        n_sub = blk // bk
        sub_rows = [slice(r * bk, (r + 1) * bk) for r in range(n_sub)]

        def step(j):
            kv_rows = pl.ds(pl.multiple_of(j * bk, bk), bk)
            for rows in sub_rows:
                dk_c, dv_c = products(rows, kv_rows, None)
                put(kv_rows, dk_c, dv_c, False)

        _key_block_loop(step, i * n_sub, per_trip, smallest=n_sub)
        for c in range(n_sub):
            kv_rows = pl.ds(pl.multiple_of(i * blk + c * bk, bk), bk)
            for r in range(c, n_sub):
                dk_c, dv_c = products(sub_rows[r], kv_rows, 0 if r == c else None)
                put(kv_rows, dk_c, dv_c, r == c)
        dq = dq_s[...] * ATT_SCALE
        dq_ref[:, 0:128] = dq[:, 0:128].astype(dq_ref.dtype)
        dq_ref[:, 128:256] = _rope_b(dq[:, 128:256], c_ref[...], sa_ref[...], sb_ref[...], -1.0).astype(dq_ref.dtype)

        @pl.when(i == nb - 1)
        def _():
            dkn_ref[...] = dkn_ref[...] * LN2
            dkr_ref[...] = dkr_ref[...] * LN2

    head = lambda h, i: (0, h)
    blk_idx = lambda h, i: (i, h)
    row_idx = lambda h, i: (i, 0)
    return pl.pallas_call(
        body, name="attention_bwd", grid=(H_B, nb),
        in_specs=[pl.BlockSpec((blk, QK_PAD), blk_idx), pl.BlockSpec((s, QK_PAD), head), pl.BlockSpec((s, V_HEAD), head),
                  pl.BlockSpec((blk, V_HEAD), blk_idx), pl.BlockSpec((blk, V_HEAD), blk_idx),
                  pl.BlockSpec((blk, LANES), lambda h, i: (i, GATE_BLOCK0 + h)), pl.BlockSpec((1, blk, 1), lambda h, i: (h, i, 0)),
                  pl.BlockSpec((blk, LANES), row_idx), pl.BlockSpec((blk, LANES), row_idx), pl.BlockSpec((blk, LANES), row_idx)],
        out_specs=[pl.BlockSpec((blk, QK_PAD), blk_idx), pl.BlockSpec((s, 128), head), pl.BlockSpec((s, 128), head),
                   pl.BlockSpec((s, 128), head), pl.BlockSpec((blk, V_HEAD), blk_idx)],
        out_shape=[jax.ShapeDtypeStruct((s, H_B * QK_PAD), BF16), jax.ShapeDtypeStruct((s, H_B * 128), F32),
                   jax.ShapeDtypeStruct((s, H_B * 128), F32), jax.ShapeDtypeStruct((s, H_B * 128), F32),
                   jax.ShapeDtypeStruct((s, WIDTH_B), BF16)],
        scratch_shapes=[pltpu.VMEM((blk, LANES), F32), pltpu.VMEM((blk, LANES), F32), pltpu.VMEM((blk, V_HEAD), BF16),
                        pltpu.VMEM((blk, QK_PAD), F32)],
        compiler_params=_params(("parallel", "arbitrary")),
    )(q, k, v, du, o, h1, lse, cos, sa, sb)


def _local_step(x, target, w, kv_norm, q_norm, ln_g, ln_b, *, ts=256, blk=512, late=None, reduce=None):
    s = x.shape[0]
    cos_a, sin_a = _rope_tables_a(s)
    cos_b, sa_b, sb_b = _rope_tables_b(s)
    tables = _retention_tables()
    g0, g1, b0, b1 = ln_g[0:1], ln_g[1:2], ln_b[0:1], ln_b[1:2]

    x16 = x.astype(BF16)
    if late is None:
        h_a = _mm(x16, w["a_in"], tn=1536, name="a_in_fwd")
    else:
        flat_b, chip = late
        h_a, got = _mm(x16, w["a_in"], tn=1536, name="a_in_fwd", side=_gather_side(flat_b))
        got = lax.dynamic_update_slice(got, flat_b[None], (chip, 0, 0))
        w = {**w, **_kernel_layout_b(_full_from_gathered(got, B_SHARDS))}
    q_a, k_a, v_a, o_a, u_a, states = _retention_fwd(h_a, cos_a, sin_a, tables)
    y_a = _mm(u_a, w["a_out"], tn=1024, name="a_out_fwd")
    x1, x1_16 = _ln_fwd(x, y_a, g0, b0, ts=ts)

    h1 = _mm(x1_16, w["b_in1"], tn=1152, name="b_in_fwd")
    lat16, kr16, qn16 = _kvq_prep(h1, kv_norm, q_norm, cos_b, sa_b, sb_b, ts=ts)
    k16 = _mm(lat16, w["up_k"], out_dtype=BF16, tn=2048, out_tn=H_B * QK_PAD, extras=(kr16,), epilogue=_assemble_k_store, name="up_k_fwd")
    v16 = _mm(lat16, w["up_v"], out_dtype=BF16, tn=2048, name="up_v_fwd")
    q16 = _mm(qn16, w["uq"], out_dtype=BF16, tn=2048, extras=(cos_b, sa_b, sb_b), epilogue=_rope_q_store, name="uq_fwd")
    o_b, u_b, lse = _attention_fwd(q16, k16, v16, h1, blk=2 * blk, bk=blk, sub=blk // 2, per_trip=4)
    y_b = _mm(u_b, w["b_out"], tn=1024, name="b_out_fwd")

    dz_b, dz_b16, dg1, db1, loss = _ln_loss_bwd(x1, y_b, target, g1, b1, ts=ts)
    d_b_out = _mm(u_b, dz_b16, ta=True, tn=1024, tk=1024, name="b_out_dw")
    du_b = _mm(dz_b16, w["b_out"], tb=True, tn=1024, name="b_out_dx")
    dqf16, dkn, dkr_heads, dv, dgate16 = _attention_bwd(q16, k16, v16, du_b, o_b, h1, lse, cos_b, sa_b, sb_b, blk=2 * blk, bk=blk, per_trip=2)
    d_uq = _mm(qn16, dqf16, ta=True, tm=768, tn=2048, tk=1024, name="uq_dw")
    dqn = _mm(dqf16, w["uq"], tb=True, tn=768, tk=2048, name="uq_dx")
    d_up_k = _mm(lat16, dkn, ta=True, tn=2048, tk=1024, name="up_k_dw")
    d_up_v = _mm(lat16, dv, ta=True, tn=2048, tk=1024, name="up_v_dw")
    dlat_k = _mm(dkn, w["up_k"], tb=True, tn=512, name="up_k_dx")
    dlat_v = _mm(dv, w["up_v"], tb=True, tn=512, name="up_v_dx")
    dh1, dkvn, dqnorm = _h1_bwd(h1, dlat_k, dlat_v, dkr_heads, dqn, dgate16, kv_norm, q_norm, cos_b, sa_b, sb_b, ts=ts)
    d_b_in1 = _mm(x1_16, dh1, ta=True, tn=1152, tk=1024, name="b_in_dw")
    dx1 = _mm(dh1, w["b_in1"], tb=True, tn=1024, extras=(dz_b,), epilogue=_residual_store, name="b_in_dx")

    dz_a, dz_a16, dg0, db0 = _ln_bwd_call(dx1, x, y_a, g0, ts=ts)
    d_a_out = _mm(u_a, dz_a16, ta=True, tn=1024, tk=1024, name="a_out_dw")
    du_a = _mm(dz_a16, w["a_out"], tb=True, tn=1024, name="a_out_dx")
    dh_a = _retention_bwd(q_a, k_a, v_a, states, du_a, o_a, h_a, cos_a, sin_a, tables)
    grads = dict(a_out=d_a_out, b_in1=d_b_in1, uq=d_uq, b_out=d_b_out, up_k=d_up_k, up_v=d_up_v)
    small = dict(ln_g=jnp.concatenate([dg0, dg1], axis=0), ln_b=jnp.concatenate([db0, db1], axis=0),
                 q_norm=dqnorm, kv_norm=dkvn)
    if reduce is None:
        grads["a_in"] = _mm(x16, dh_a, ta=True, tn=1536, tk=1024, name="a_in_dw")
        grad_x = _mm(dh_a, w["a_in"], tb=True, tn=1024, tk=2048, extras=(dz_a,), epilogue=_residual_store, name="a_in_dx")
        return loss, grad_x, grads, small
    own_early, travel_early = reduce(_reference_layout_grads(grads), EARLY_SHARDS, EARLY_ROWS, "early")
    d_a_in, got_early = _mm(x16, dh_a, ta=True, tn=1536, tk=1024, name="a_in_dw", side=_chip_exchange_side(travel_early))
    own_late, travel_late = reduce(dict(a_w_in=d_a_in), LATE_SHARDS, LATE_ROWS, "late")
    grad_x, got_late = _mm(dh_a, w["a_in"], tb=True, tn=1024, tk=2048, extras=(dz_a,), epilogue=_residual_store, name="a_in_dx",
                           side=_chip_exchange_side(travel_late))
    return loss, grad_x, ((own_early, got_early), (own_late, got_late)), small


def _flat_shards(shards, dtype):
    parts = [shards[name].reshape(rows, FLAT_COLS) for name, rows in SHARD_ROWS]
    used = sum(rows for _, rows in SHARD_ROWS)
    parts.append(jnp.zeros((FLAT_ROWS - used, FLAT_COLS), shards["a_w_in"].dtype))
    return jnp.concatenate(parts, axis=0).astype(dtype)


def _unflat_shards(flat, shapes, shards):
    out, off = {}, 0
    for name, rows in shards:
        out[name] = flat[off:off + rows].reshape(shapes[name])
        off += rows
    return out


COL_SHARDED = {"a_w_in": (D_MODEL, IN_A), "b_w_in": (D_MODEL, IN_B), "b_w_uq": (Q_LORA, H_B * (QK_NOPE + QK_ROPE)),
               "kv_w_up": (KV_LORA, H_B * (QK_NOPE + V_HEAD))}
ROW_SHARDED = {"a_w_out": (WIDTH_A, D_MODEL), "b_w_out": (WIDTH_B, D_MODEL), "kv_w_down": (D_MODEL, KV_LORA + QK_ROPE)}


def _full_from_gathered(gathered, shards=SHARD_ROWS):
    out, off = {}, 0
    for name, rows in shards:
        part = gathered[:, off:off + rows]
        off += rows
        if name in COL_SHARDED:
            r, c = COL_SHARDED[name]
            out[name] = part.reshape(N_CHIPS, r, c // N_CHIPS).transpose(1, 0, 2).reshape(r, c)
        else:
            r, c = ROW_SHARDED[name]
            out[name] = part.reshape(r, c)
    return out


def _chip_major(g):
    r, c = g.shape
    return g.reshape(r, N_CHIPS, c // N_CHIPS).transpose(1, 0, 2)


def _gathered_from_full(full, shards, total_rows):
    if len(shards) == 1 and shards[0][0] in COL_SHARDED:
        return _chip_major(full[shards[0][0]])
    parts = []
    for name, rows in shards:
        g = full[name]
        if name in COL_SHARDED:
            r, c = COL_SHARDED[name]
            g = g.reshape(r, N_CHIPS, c // N_CHIPS).transpose(1, 0, 2)
        parts.append(g.reshape(N_CHIPS, rows, FLAT_COLS))
    used = sum(rows for _, rows in shards)
    if total_rows > used:
        parts.append(jnp.zeros((N_CHIPS, total_rows - used, FLAT_COLS), F32))
    return jnp.concatenate(parts, axis=1)


A_SHARDS, B_SHARDS = SHARD_ROWS[:1], SHARD_ROWS[1:]
A_ROWS = sum(rows for _, rows in A_SHARDS)


def _kernel_layout_a(full):
    return dict(a_in=full["a_w_in"])


def _kernel_layout_b(full):
    uq = full["b_w_uq"].reshape(Q_LORA, H_B, QK_NOPE + QK_ROPE)
    uq = jnp.pad(uq, ((0, 0), (0, 0), (0, QK_PAD - QK_NOPE - QK_ROPE))).reshape(Q_LORA, H_B * QK_PAD)
    up = full["kv_w_up"].reshape(KV_LORA, H_B, QK_NOPE + V_HEAD)
    down = jnp.pad(full["kv_w_down"], ((0, 0), (0, KV_DOWN_PAD - KV_LORA - QK_ROPE)))
    return dict(a_out=full["a_w_out"], b_out=full["b_w_out"], uq=uq,
                up_k=up[:, :, :QK_NOPE].reshape(KV_LORA, H_B * QK_NOPE),
                up_v=up[:, :, QK_NOPE:].reshape(KV_LORA, H_B * V_HEAD),
                b_in1=jnp.concatenate([down, full["b_w_in"]], axis=1))


def _kernel_layout(full):
    return {**_kernel_layout_a(full), **_kernel_layout_b(full)}


EARLY_SHARDS = tuple(sh for sh in SHARD_ROWS if sh[0] != "a_w_in")
LATE_SHARDS = tuple(sh for sh in SHARD_ROWS if sh[0] == "a_w_in")
EARLY_ROWS, LATE_ROWS = 3072, 1536


def _reference_layout_grads(g):
    uq = g["uq"].reshape(Q_LORA, H_B, QK_PAD)[:, :, :QK_NOPE + QK_ROPE].reshape(Q_LORA, H_B * (QK_NOPE + QK_ROPE))
    up = jnp.concatenate([g["up_k"].reshape(KV_LORA, H_B, QK_NOPE), g["up_v"].reshape(KV_LORA, H_B, V_HEAD)], axis=2)
    return dict(a_w_out=g["a_out"], b_w_out=g["b_out"], b_w_uq=uq,
                kv_w_up=up.reshape(KV_LORA, H_B * (QK_NOPE + V_HEAD)),
                kv_w_down=g["b_in1"][:, :KV_LORA + QK_ROPE], b_w_in=g["b_in1"][:, KV_DOWN_PAD:])


HBM_SPEC = pl.BlockSpec(memory_space=pl.ANY)


def _me():
    return lax.axis_index("x"), lax.axis_index("y"), lax.axis_index("c")


def _chip_flips(x, y):
    return [(1 - x, y), (x, 1 - y), (1 - x, 1 - y)]


def _gather_copies(src_ref, out_ref, send_sems, recv_sems):
    x, y, c = _me()
    half = src_ref.shape[0] // 2
    my_rows = pl.ds(pl.multiple_of(c * half, 16), half)
    their_rows = pl.ds(pl.multiple_of((1 - c) * half, 16), half)
    chips = _chip_flips(x, y)
    sibling = (x, y, 1 - c)

    def copy(k, src, dst, to):
        return pltpu.make_async_remote_copy(src_ref=src, dst_ref=dst, send_sem=send_sems.at[k], recv_sem=recv_sems.at[k],
                                            device_id=to, device_id_type=MESH)

    sends = [copy(k, src_ref.at[my_rows, :], out_ref.at[2 * x + y, my_rows, :], (px, py, c)) for k, (px, py) in enumerate(chips)]
    landed = [out_ref.at[2 * px + py, my_rows, :] for px, py in chips]
    lands = [copy(k, landed[k], landed[k], (px, py, c)) for k, (px, py) in enumerate(chips)]
    forwards = [copy(3 + k, landed[k], landed[k], sibling) for k in range(3)]
    theirs = [out_ref.at[2 * px + py, their_rows, :] for px, py in chips]
    arrivals = [copy(3 + k, theirs[k], theirs[k], sibling) for k in range(3)]
    return sends, lands, forwards, arrivals


def _gather_start(src_ref, out_ref, send_sems, recv_sems):
    sends, _, _, _ = _gather_copies(src_ref, out_ref, send_sems, recv_sems)
    for cp in sends:
        cp.start()


def _gather_finish(src_ref, out_ref, send_sems, recv_sems):
    sends, lands, forwards, arrivals = _gather_copies(src_ref, out_ref, send_sems, recv_sems)
    for k in range(3):
        lands[k].wait_recv()
        forwards[k].start()
    for cp in arrivals:
        cp.wait_recv()
    for cp in sends + forwards:
        cp.wait_send()


def _gather_scratch():
    return [pltpu.SemaphoreType.DMA((6,)), pltpu.SemaphoreType.DMA((6,))]


def _gather_weights(flat16):
    def body(src_ref, out_ref, send_sems, recv_sems):
        _gather_start(src_ref, out_ref, send_sems, recv_sems)
        _gather_finish(src_ref, out_ref, send_sems, recv_sems)

    return pl.pallas_call(
        body, name="gather_weights",
        in_specs=[HBM_SPEC], out_specs=HBM_SPEC,
        out_shape=jax.ShapeDtypeStruct((N_CHIPS,) + flat16.shape, flat16.dtype),
        scratch_shapes=_gather_scratch(),
    )(flat16)


def _gather_side(flat16):
    return dict(inputs=[flat16], out_shape=jax.ShapeDtypeStruct((N_CHIPS,) + flat16.shape, flat16.dtype),
                scratch=_gather_scratch(), start=_gather_start, finish=_gather_finish)


def _chip_exchange_copies(p_ref, out_ref, send_sems, recv_sems):
    x, y, c = _me()
    return [pltpu.make_async_remote_copy(
        src_ref=p_ref.at[2 * px + py], dst_ref=out_ref.at[k], send_sem=send_sems.at[k], recv_sem=recv_sems.at[k],
        device_id=(px, py, c), device_id_type=MESH) for k, (px, py) in enumerate(_chip_flips(x, y))]


def _chip_exchange_start(p_ref, out_ref, send_sems, recv_sems):
    for cp in _chip_exchange_copies(p_ref, out_ref, send_sems, recv_sems):
        cp.start()


def _chip_exchange_finish(p_ref, out_ref, send_sems, recv_sems):
    copies = _chip_exchange_copies(p_ref, out_ref, send_sems, recv_sems)
    for cp in copies:
        cp.wait_send()
    for cp in copies:
        cp.wait_recv()


def _chip_exchange_side(p):
    return dict(inputs=[p], out_shape=jax.ShapeDtypeStruct((3,) + p.shape[1:], p.dtype),
                scratch=[pltpu.SemaphoreType.DMA((3,)), pltpu.SemaphoreType.DMA((3,))],
                start=_chip_exchange_start, finish=_chip_exchange_finish)


def _pair_swap(r, name):
    def body(r_ref, out_ref, send_sem, recv_sem):
        x, y, c = _me()
        cp = pltpu.make_async_remote_copy(src_ref=r_ref, dst_ref=out_ref, send_sem=send_sem, recv_sem=recv_sem,
                                          device_id=(x, y, 1 - c), device_id_type=MESH)
        cp.start()
        cp.wait_send()
        cp.wait_recv()

    return pl.pallas_call(
        body, name=name,
        in_specs=[HBM_SPEC], out_specs=HBM_SPEC,
        out_shape=jax.ShapeDtypeStruct(r.shape, r.dtype),
        scratch_shapes=[pltpu.SemaphoreType.DMA, pltpu.SemaphoreType.DMA],
    )(r)


def _sum_small(vec):
    def body(v_ref, out_ref, slots, send_sems, recv_sems):
        x, y, c = _me()
        me = 4 * x + 2 * y + c
        slots[me] = v_ref[...]
        flips = [(fx, fy, fc) for fx in (0, 1) for fy in (0, 1) for fc in (0, 1)][1:]
        copies = []
        for k, (fx, fy, fc) in enumerate(flips):
            copies.append(pltpu.make_async_remote_copy(
                src_ref=v_ref, dst_ref=slots.at[me], send_sem=send_sems.at[k], recv_sem=recv_sems.at[k],
                device_id=(x ^ fx, y ^ fy, c ^ fc), device_id_type=MESH))
        for cp in copies:
            cp.start()
        for cp in copies:
            cp.wait_send()
        for k, (fx, fy, fc) in enumerate(flips):
            src = 4 * (x ^ fx) + 2 * (y ^ fy) + (c ^ fc)
            pltpu.make_async_remote_copy(
                src_ref=v_ref, dst_ref=slots.at[src], send_sem=send_sems.at[k], recv_sem=recv_sems.at[k],
                device_id=(x ^ fx, y ^ fy, c ^ fc), device_id_type=MESH).wait_recv()
        total = slots[0]
        for d in range(1, N_DEV):
            total = total + slots[d]
        out_ref[...] = total

    return pl.pallas_call(
        body, name="sum_small",
        in_specs=[pl.BlockSpec(memory_space=pltpu.VMEM)], out_specs=pl.BlockSpec(memory_space=pltpu.VMEM),
        out_shape=jax.ShapeDtypeStruct(vec.shape, vec.dtype),
        scratch_shapes=[pltpu.VMEM((N_DEV,) + vec.shape, vec.dtype), pltpu.SemaphoreType.DMA((7,)),
                        pltpu.SemaphoreType.DMA((7,))],
    )(vec)


UPD_ROWS = 256


def _pair_sum(g, theirs, core, chip, name):
    half, cols = theirs.shape[1:]
    nb = half // UPD_ROWS

    def body(core_ref, chip_ref, g_ref, t_ref, own_ref, o16_ref):
        total = g_ref[0] + t_ref[0].astype(F32)
        o16_ref[0] = total.astype(o16_ref.dtype)

        @pl.when(pl.program_id(1) == chip_ref[0])
        def _():
            own_ref[...] = total

    return pl.pallas_call(
        body, name=name,
        grid_spec=pltpu.PrefetchScalarGridSpec(
            num_scalar_prefetch=2, grid=(nb, N_CHIPS),
            in_specs=[pl.BlockSpec((1, UPD_ROWS, cols), lambda i, d, core_ref, chip_ref: (d, core_ref[0] * nb + i, 0)),
                      pl.BlockSpec((1, UPD_ROWS, cols), lambda i, d, core_ref, chip_ref: (d, i, 0))],
            out_specs=[pl.BlockSpec((UPD_ROWS, cols), lambda i, d, core_ref, chip_ref: (i, 0)),
                       pl.BlockSpec((1, UPD_ROWS, cols), lambda i, d, core_ref, chip_ref: (d, i, 0))]),
        out_shape=[jax.ShapeDtypeStruct((half, cols), F32),
                   jax.ShapeDtypeStruct((N_CHIPS, half, cols), BF16)],
        compiler_params=_params(("parallel", "arbitrary")),
    )(core, chip, g, theirs)


def _chip_sum(own, received, name):
    half, cols = own.shape
    nb = half // UPD_ROWS

    def body(p_ref, r_ref, o_ref):
        o_ref[...] = ((p_ref[...] + r_ref[0].astype(F32)) + r_ref[1].astype(F32)) + r_ref[2].astype(F32)

    return pl.pallas_call(
        body, name=name, grid=(nb,),
        in_specs=[pl.BlockSpec((UPD_ROWS, cols), lambda i: (i, 0)),
                  pl.BlockSpec((3, UPD_ROWS, cols), lambda i: (0, i, 0))],
        out_specs=pl.BlockSpec((UPD_ROWS, cols), lambda i: (i, 0)),
        out_shape=jax.ShapeDtypeStruct((half, cols), F32),
        compiler_params=_params(("parallel",)),
    )(own, received)


def _adamw(w, g, m, v, *, rows, name):
    r, c = w.shape
    rows = min(rows, r)
    assert r % rows == 0

    def body(w_ref, g_ref, m_ref, v_ref, d_ref, nm_ref, nv_ref):
        gv = g_ref[...]
        nm = ADAM_B1 * m_ref[...] + (1.0 - ADAM_B1) * gv
        nv = ADAM_B2 * v_ref[...] + (1.0 - ADAM_B2) * (gv * gv)
        m_hat = nm / (1.0 - ADAM_B1 ** ADAM_STEP)
        v_hat = nv / (1.0 - ADAM_B2 ** ADAM_STEP)
        d_ref[...] = -ADAM_LR * (m_hat / (jnp.sqrt(v_hat) + ADAM_EPS) + ADAM_WD * w_ref[...])
        nm_ref[...] = nm
        nv_ref[...] = nv

    spec = pl.BlockSpec((rows, c), lambda i: (i, 0))
    return pl.pallas_call(
        body, name=name, grid=(r // rows,),
        in_specs=[spec] * 4, out_specs=[spec] * 3,
        out_shape=[jax.ShapeDtypeStruct((r, c), F32)] * 3,
        compiler_params=_params(("parallel",)),
    )(w, g, m, v)


W_NAMES = ("a_w_in", "a_w_out", "b_w_in", "b_q_norm", "b_w_uq", "b_w_out", "kv_w_down", "kv_norm", "kv_w_up", "ln_g", "ln_b")
BIG = tuple(name for name, _ in SHARD_ROWS)


def _pack_small(ln_g, ln_b, q_norm, kv_norm, extra=None):
    pad = lambda a: jnp.pad(a.reshape(1, -1), ((0, 0), (0, FLAT_COLS - a.size)))
    rows = [ln_g, ln_b, pad(q_norm), pad(kv_norm),
            jnp.zeros((1, FLAT_COLS), F32) if extra is None else pad(extra), jnp.zeros((1, FLAT_COLS), F32)]
    return jnp.concatenate(rows, axis=0)


def _unpack_small(p):
    return dict(ln_g=p[0:2], ln_b=p[2:4], b_q_norm=p[4:5, :Q_LORA], kv_norm=p[5, :KV_LORA])


def kernel(x, a_w_in, a_w_out, b_w_in, b_q_norm, b_w_uq, b_w_out, kv_w_down, kv_norm, kv_w_up, ln_g, ln_b, loss_target, m_a_w_in, m_a_w_out, m_b_w_in, m_b_q_norm, m_b_w_uq, m_b_w_out, m_kv_w_down, m_kv_norm, m_kv_w_up, m_ln_g, m_ln_b, v_a_w_in, v_a_w_out, v_b_w_in, v_b_q_norm, v_b_w_uq, v_b_w_out, v_kv_w_down, v_kv_norm, v_kv_w_up, v_ln_g, v_ln_b):
    w_in = dict(a_w_in=a_w_in[0], a_w_out=a_w_out[0], b_w_in=b_w_in[0], b_w_uq=b_w_uq[0], b_w_out=b_w_out[0],
                kv_w_down=kv_w_down, kv_w_up=kv_w_up)
    m_in = dict(a_w_in=m_a_w_in[0], a_w_out=m_a_w_out[0], b_w_in=m_b_w_in[0], b_w_uq=m_b_w_uq[0], b_w_out=m_b_w_out[0],
                kv_w_down=m_kv_w_down, kv_w_up=m_kv_w_up)
    v_in = dict(a_w_in=v_a_w_in[0], a_w_out=v_a_w_out[0], b_w_in=v_b_w_in[0], b_w_uq=v_b_w_uq[0], b_w_out=v_b_w_out[0],
                kv_w_down=v_kv_w_down, kv_w_up=v_kv_w_up)
    shard_shapes = {name: w_in[name].shape for name in BIG}

    cx, cy, cc = lax.axis_index("x"), lax.axis_index("y"), lax.axis_index("c")
    chip = 2 * cx + cy
    flat_b = _flat_shards(w_in, BF16)[A_ROWS:]
    a16 = w_in["a_w_in"].astype(BF16)
    got_a = lax.dynamic_update_slice(_gather_weights(a16), a16[None], (chip, 0, 0))
    weights_a = dict(a_in=got_a.transpose(1, 0, 2).reshape(D_MODEL, IN_A))

    core_arr, chip_arr = cc.astype(jnp.int32).reshape(1), chip.astype(jnp.int32).reshape(1)

    def reduce_pair(full, shards, rows, tag):
        g_all = _gathered_from_full(full, shards, rows)
        half, cols = g_all.shape[1] // 2, g_all.shape[2]
        other_half = lax.dynamic_slice(g_all, (0, (1 - cc) * half, 0), (N_CHIPS, half, cols)).astype(BF16)
        theirs = _pair_swap(other_half, "pair_exchange_" + tag)
        return _pair_sum(g_all, theirs, core_arr, chip_arr, "pair_sum_" + tag)

    loss, grad_x, reduced, small = _local_step(x[0], loss_target[0], weights_a, kv_norm.reshape(1, -1), b_q_norm, ln_g, ln_b,
                                               late=(flat_b, chip), reduce=reduce_pair)

    g_big = {}
    for (own, received), shards, tag in zip(reduced, (EARLY_SHARDS, LATE_SHARDS), ("early", "late")):
        mine = _chip_sum(own, received, "chip_sum_" + tag)
        sibling = _pair_swap(mine, "pair_share_" + tag)
        g_flat = jnp.concatenate([jnp.where(cc == 0, mine, sibling), jnp.where(cc == 0, sibling, mine)], axis=0)
        if g_flat.shape == shard_shapes[shards[0][0]]:
            g_big[shards[0][0]] = g_flat
        else:
            g_big.update(_unflat_shards(g_flat, shard_shapes, shards))

    small_sum = _sum_small(_pack_small(small["ln_g"], small["ln_b"], small["q_norm"], small["kv_norm"], loss[:, :1]))
    loss_out = small_sum[6, 0]

    upd = {name: _adamw(w_in[name], g_big[name], m_in[name], v_in[name], rows=256, name="adamw_" + name) for name in BIG}
    ds, nms, nvs = _adamw(_pack_small(ln_g, ln_b, b_q_norm, kv_norm), small_sum.at[6].set(0.0),
                          _pack_small(m_ln_g, m_ln_b, m_b_q_norm, m_kv_norm),
                          _pack_small(v_ln_g, v_ln_b, v_b_q_norm, v_kv_norm), rows=8, name="adamw_small")

    def assemble(big, small_packed):
        sm = _unpack_small(small_packed)
        out = {}
        for name in W_NAMES:
            if name in big:
                out[name] = big[name][None] if name in ("a_w_in", "a_w_out", "b_w_in", "b_w_uq", "b_w_out") else big[name]
            else:
                out[name] = sm[name]
        return [out[name] for name in W_NAMES]

    part = lambda k: {name: upd[name][k] for name in BIG}
    return (loss_out, grad_x[None], *assemble(g_big, small_sum), *assemble(part(0), ds), *assemble(part(1), nms),
            *assemble(part(2), nvs))
```

```python
import functools
import math

import jax
import jax.numpy as jnp
from jax import lax
from jax.experimental import pallas as pl
from jax.experimental.pallas import tpu as pltpu

F32 = jnp.float32
BF16 = jnp.bfloat16
MESH = pl.DeviceIdType.MESH

D_MODEL = 1024
DEPTH = 2
H_A, DK_A, DV_A = 4, 256, 512
WIDTH_A = H_A * DV_A
CHUNK = 128
H_B, QK_NOPE, QK_ROPE, V_HEAD = 16, 128, 64, 128
QK_PAD = 256
Q_LORA, KV_LORA = 768, 512
KV_DOWN_PAD = 640
WIDTH_B = H_B * V_HEAD
IN_A = 2 * H_A * DK_A + 2 * WIDTH_A
IN_B = Q_LORA + WIDTH_B
H1_B = KV_DOWN_PAD + IN_B
ROPE_BASE = 10000.0
ALPHA = (2.0 * DEPTH) ** 0.25
ATT_SCALE = (QK_NOPE + QK_ROPE) ** -0.5
NEG_BIG = -1e30

ADAM_LR, ADAM_B1, ADAM_B2, ADAM_EPS, ADAM_WD, ADAM_STEP = 0.001, 0.9, 0.999, 1e-08, 0.01, 10

VMEM_LIMIT_BYTES = 56 * 1024 * 1024
LANES = 128
FLAT_COLS = 1024
SHARD_ROWS = (("a_w_in", 1536), ("a_w_out", 512), ("b_w_in", 704), ("b_w_uq", 576), ("b_w_out", 512),
              ("kv_w_down", 144), ("kv_w_up", 512))
FLAT_ROWS = 4608
N_CHIPS = 4
N_DEV = 8


def _params(sem, vmem=VMEM_LIMIT_BYTES):
    return pltpu.CompilerParams(dimension_semantics=sem, vmem_limit_bytes=vmem)


def _row_spec(ts, w, col_block=0):
    return pl.BlockSpec((ts, w), lambda i: (i, col_block))


def _bc_spec(shape):
    nd = len(shape)
    return pl.BlockSpec(shape, lambda i: (0,) * nd)


def _sigmoid(x):
    return 1.0 / (1.0 + jnp.exp(-x))


def _fold8(v):
    ts, w = v.shape
    return jnp.sum(v.reshape(ts // 8, 8, w), axis=0)


def _mm(a, b, *, ta=False, tb=False, out_dtype=F32, tm=1024, tn=512, tk=None, name, extras=(), epilogue=None, out_tn=None,
        side=None):
    if ta:
        K, M = a.shape
    else:
        M, K = a.shape
    if tb:
        N, Kb = b.shape
    else:
        Kb, N = b.shape
    assert K == Kb, (a.shape, b.shape)
    tm, tn = min(tm, M), min(tn, N)
    tk = K if tk is None else min(tk, K)
    assert M % tm == 0 and N % tn == 0 and K % tk == 0, (name, M, N, K, tm, tn, tk)
    grid = (M // tm, N // tn, K // tk)
    nk = grid[2]
    out_tn = tn if out_tn is None else out_tn
    n_extra = len(extras)
    side_inputs = [] if side is None else list(side["inputs"])
    n_side = len(side_inputs)
    n_acc = 0 if nk == 1 else 1
    dims = (((0,) if ta else (1,), (1,) if tb else (0,)), ((), ()))

    def body(a_ref, b_ref, *rest):
        extra_refs, rest = rest[:n_extra], rest[n_extra:]
        side_in, o_ref, rest = rest[:n_side], rest[n_side], rest[n_side + 1:]
        if side is not None:
            side_refs, rest = side_in + rest[:1] + rest[1 + n_acc:], rest[1:]
            ids = [pl.program_id(d) for d in range(3)]

            @pl.when((ids[0] == 0) & (ids[1] == 0) & (ids[2] == 0))
            def _():
                side["start"](*side_refs)

        prod = lax.dot_general(a_ref[...].astype(BF16), b_ref[...].astype(BF16), dims,
                               preferred_element_type=F32)

        def store(tile):
            if epilogue is None:
                o_ref[...] = tile.astype(o_ref.dtype)
            else:
                epilogue(tile, o_ref, *extra_refs)

        if nk == 1:
            store(prod)
        else:
            acc = rest[0]
            k = pl.program_id(2)

            @pl.when(k == 0)
            def _():
                acc[...] = prod

            @pl.when(k > 0)
            def _():
                acc[...] += prod

            @pl.when(k == nk - 1)
            def _():
                store(acc[...])

        if side is not None:
            @pl.when((ids[0] == grid[0] - 1) & (ids[1] == grid[1] - 1) & (ids[2] == grid[2] - 1))
            def _():
                side["finish"](*side_refs)

    a_spec = pl.BlockSpec((tk, tm), lambda i, j, k: (k, i)) if ta else pl.BlockSpec((tm, tk), lambda i, j, k: (i, k))
    b_spec = pl.BlockSpec((tn, tk), lambda i, j, k: (j, k)) if tb else pl.BlockSpec((tk, tn), lambda i, j, k: (k, j))
    extra_specs = [pl.BlockSpec((tm, e.shape[1]), lambda i, j, k: (i, 0)) for e in extras]
    out_specs = pl.BlockSpec((tm, out_tn), lambda i, j, k: (i, j))
    out_shape = jax.ShapeDtypeStruct((M, (N // tn) * out_tn), out_dtype)
    scratch = [] if nk == 1 else [pltpu.VMEM((tm, tn), F32)]
    if side is not None:
        out_specs, out_shape, scratch = [out_specs, HBM_SPEC], [out_shape, side["out_shape"]], scratch + list(side["scratch"])
    return pl.pallas_call(
        body, name=name, grid=grid,
        in_specs=[a_spec, b_spec] + extra_specs + [HBM_SPEC] * n_side,
        out_specs=out_specs, out_shape=out_shape, scratch_shapes=scratch,
        compiler_params=_params(("parallel", "parallel", "arbitrary") if side is None else ("arbitrary",) * 3),
    )(a, b, *extras, *side_inputs)


def _rope_tables_a(s):
    half = DK_A // 2
    inv = ROPE_BASE ** (-jnp.arange(half, dtype=F32) / half)
    ang = jnp.arange(s, dtype=F32)[:, None] * inv[None, :]
    return jnp.cos(ang), jnp.sin(ang)


def _rope_tables_b(s):
    half = QK_ROPE // 2
    inv = ROPE_BASE ** (-jnp.arange(half, dtype=F32) / half)
    ang = jnp.arange(s, dtype=F32)[:, None] * inv[None, :]
    c, sn = jnp.cos(ang), jnp.sin(ang)
    z = jnp.zeros_like(c)
    cos = jnp.concatenate([c, c, z, z], axis=1)
    sa = jnp.concatenate([-sn, z, z, z], axis=1)
    sb = jnp.concatenate([z, sn, z, z], axis=1)
    return cos, sa, sb


def _rope_b(r, cos, sa, sb, sign):
    return r * cos + sign * (pltpu.roll(r, 96, 1) * sa + pltpu.roll(r, 32, 1) * sb)


def _retention_tables():
    lg = jnp.log1p(-jnp.exp2(-5.0 - jnp.arange(H_A, dtype=F32)))
    idx = jnp.arange(CHUNK, dtype=F32)
    diff = idx[:, None] - idx[None, :]
    causal = diff >= 0
    dmat = jnp.where(causal, jnp.exp(jnp.where(causal, diff, 0.0)[None] * lg[:, None, None]), 0.0)
    qdec = jnp.exp((idx + 1.0)[None, :] * lg[:, None])[:, :, None]
    kdec = jnp.exp((CHUNK - 1.0 - idx)[None, :] * lg[:, None])[:, :, None]
    cdec = jnp.broadcast_to(jnp.exp(CHUNK * lg)[:, None, None], (H_A, 1, DV_A))
    return dmat, qdec, kdec, cdec


def _group_norm(o):
    mu = jnp.mean(o, axis=-1, keepdims=True)
    oc = o - mu
    var = jnp.mean(oc * oc, axis=-1, keepdims=True)
    rstd = lax.rsqrt(var + 1e-5)
    return oc * rstd, rstd


Q_COL, K_COL, V_COL, GATE_COL = 0, H_A * DK_A, 2 * H_A * DK_A, 2 * H_A * DK_A + WIDTH_A


def _ln_stats(z):
    mu = jnp.mean(z, axis=-1, keepdims=True)
    zc = z - mu
    var = jnp.mean(zc * zc, axis=-1, keepdims=True)
    rstd = lax.rsqrt(var + 1e-5)
    return zc * rstd, rstd


def _ln_bwd(dy, xhat, rstd, g):
    dxh = dy * g
    m1 = jnp.mean(dxh, axis=-1, keepdims=True)
    m2 = jnp.mean(dxh * xhat, axis=-1, keepdims=True)
    return rstd * (dxh - m1 - xhat * m2)


def _ln_fwd(x, y, g, b, *, ts):
    s = x.shape[0]

    def body(x_ref, y_ref, g_ref, b_ref, o_ref, o16_ref):
        xhat, _ = _ln_stats(ALPHA * x_ref[...] + y_ref[...])
        out = xhat * g_ref[...] + b_ref[...]
        o_ref[...] = out
        o16_ref[...] = out.astype(o16_ref.dtype)

    return pl.pallas_call(
        body, name="ln_fwd", grid=(s // ts,),
        in_specs=[_row_spec(ts, D_MODEL), _row_spec(ts, D_MODEL), _bc_spec((1, D_MODEL)), _bc_spec((1, D_MODEL))],
        out_specs=[_row_spec(ts, D_MODEL), _row_spec(ts, D_MODEL)],
        out_shape=[jax.ShapeDtypeStruct((s, D_MODEL), F32), jax.ShapeDtypeStruct((s, D_MODEL), BF16)],
        compiler_params=_params(("parallel",)),
    )(x, y, g, b)


def _ln_loss_bwd(x1, y, target, g, b, *, ts):
    s = x1.shape[0]
    n = s // ts

    def body(x_ref, y_ref, t_ref, g_ref, b_ref, dz_ref, dz16_ref, dg_ref, db_ref, loss_ref, ag, ab, al):
        i = pl.program_id(0)

        @pl.when(i == 0)
        def _():
            ag[...] = jnp.zeros_like(ag)
            ab[...] = jnp.zeros_like(ab)
            al[...] = jnp.zeros_like(al)

        xhat, rstd = _ln_stats(ALPHA * x_ref[...] + y_ref[...])
        err = xhat * g_ref[...] + b_ref[...] - t_ref[...]
        al[...] += _fold8(err * err)
        dy = err * (1.0 / D_MODEL)
        ag[...] += _fold8(dy * xhat)
        ab[...] += _fold8(dy)
        dz = _ln_bwd(dy, xhat, rstd, g_ref[...])
        dz_ref[...] = dz
        dz16_ref[...] = dz.astype(dz16_ref.dtype)

        @pl.when(i == n - 1)
        def _():
            dg_ref[...] = jnp.sum(ag[...], axis=0, keepdims=True)
            db_ref[...] = jnp.sum(ab[...], axis=0, keepdims=True)
            loss_ref[...] = jnp.full((1, LANES), (0.5 / D_MODEL) * jnp.sum(al[...]), F32)

    return pl.pallas_call(
        body, name="ln_loss_bwd", grid=(n,),
        in_specs=[_row_spec(ts, D_MODEL)] * 3 + [_bc_spec((1, D_MODEL))] * 2,
        out_specs=[_row_spec(ts, D_MODEL), _row_spec(ts, D_MODEL), _bc_spec((1, D_MODEL)), _bc_spec((1, D_MODEL)),
                   _bc_spec((1, LANES))],
        out_shape=[jax.ShapeDtypeStruct((s, D_MODEL), F32), jax.ShapeDtypeStruct((s, D_MODEL), BF16),
                   jax.ShapeDtypeStruct((1, D_MODEL), F32), jax.ShapeDtypeStruct((1, D_MODEL), F32),
                   jax.ShapeDtypeStruct((1, LANES), F32)],
        scratch_shapes=[pltpu.VMEM((8, D_MODEL), F32)] * 3,
        compiler_params=_params(("arbitrary",)),
    )(x1, y, target, g, b)


def _ln_bwd_call(dy, x, y, g, *, ts):
    s = x.shape[0]
    n = s // ts

    def body(dy_ref, x_ref, y_ref, g_ref, dz_ref, dz16_ref, dg_ref, db_ref, ag, ab):
        i = pl.program_id(0)

        @pl.when(i == 0)
        def _():
            ag[...] = jnp.zeros_like(ag)
            ab[...] = jnp.zeros_like(ab)

        xhat, rstd = _ln_stats(ALPHA * x_ref[...] + y_ref[...])
        dy = dy_ref[...]
        ag[...] += _fold8(dy * xhat)
        ab[...] += _fold8(dy)
        dz = _ln_bwd(dy, xhat, rstd, g_ref[...])
        dz_ref[...] = dz
        dz16_ref[...] = dz.astype(dz16_ref.dtype)

        @pl.when(i == n - 1)
        def _():
            dg_ref[...] = jnp.sum(ag[...], axis=0, keepdims=True)
            db_ref[...] = jnp.sum(ab[...], axis=0, keepdims=True)

    return pl.pallas_call(
        body, name="ln_bwd", grid=(n,),
        in_specs=[_row_spec(ts, D_MODEL)] * 3 + [_bc_spec((1, D_MODEL))],
        out_specs=[_row_spec(ts, D_MODEL), _row_spec(ts, D_MODEL), _bc_spec((1, D_MODEL)), _bc_spec((1, D_MODEL))],
        out_shape=[jax.ShapeDtypeStruct((s, D_MODEL), F32), jax.ShapeDtypeStruct((s, D_MODEL), BF16),
                   jax.ShapeDtypeStruct((1, D_MODEL), F32), jax.ShapeDtypeStruct((1, D_MODEL), F32)],
        scratch_shapes=[pltpu.VMEM((8, D_MODEL), F32)] * 2,
        compiler_params=_params(("arbitrary",)),
    )(dy, x, y, g)


def _residual_store(tile, o_ref, dz_ref):
    o_ref[...] = ALPHA * dz_ref[...] + tile


C_LAT = slice(0, KV_LORA)
C_ROPE = slice(KV_LORA, KV_DOWN_PAD)
C_QL = slice(KV_DOWN_PAD, KV_DOWN_PAD + Q_LORA)
C_GATE = slice(KV_DOWN_PAD + Q_LORA, H1_B)


def _rms(x, eps=1e-6):
    r = lax.rsqrt(jnp.mean(x * x, axis=-1, keepdims=True) + eps)
    return x * r, r


def _rms_bwd(dy, xhat, r, g):
    dxh = dy * g
    return r * (dxh - xhat * jnp.mean(dxh * xhat, axis=-1, keepdims=True))


def _kvq_prep(h1, kv_norm, q_norm, cos, sa, sb, *, ts):
    s = h1.shape[0]

    def body(h_ref, kn_ref, qn_ref, c_ref, sa_ref, sb_ref, lat_ref, kr_ref, ql_ref):
        lat, _ = _rms(h_ref[:, C_LAT])
        lat_ref[...] = (lat * kn_ref[...]).astype(lat_ref.dtype)
        kr_ref[...] = _rope_b(h_ref[:, C_ROPE], c_ref[...], sa_ref[...], sb_ref[...], 1.0).astype(kr_ref.dtype)
        ql, _ = _rms(h_ref[:, C_QL])
        ql_ref[...] = (ql * qn_ref[...]).astype(ql_ref.dtype)

    return pl.pallas_call(
        body, name="kvq_prep", grid=(s // ts,),
        in_specs=[_row_spec(ts, H1_B), _bc_spec((1, KV_LORA)), _bc_spec((1, Q_LORA))] + [_row_spec(ts, 128)] * 3,
        out_specs=[_row_spec(ts, KV_LORA), _row_spec(ts, 128), _row_spec(ts, Q_LORA)],
        out_shape=[jax.ShapeDtypeStruct((s, KV_LORA), BF16), jax.ShapeDtypeStruct((s, 128), BF16),
                   jax.ShapeDtypeStruct((s, Q_LORA), BF16)],
        compiler_params=_params(("parallel",)),
    )(h1, kv_norm, q_norm, cos, sa, sb)


LOG2E = 1.4426950408889634
LN2 = 0.6931471805599453
Q_SCALE = ATT_SCALE * LOG2E


def _rope_q_store(tile, o_ref, c_ref, sa_ref, sb_ref):
    c, a, b = c_ref[...], sa_ref[...], sb_ref[...]
    for hd in range(tile.shape[1] // QK_PAD):
        lo = hd * QK_PAD
        o_ref[:, lo:lo + 128] = (tile[:, lo:lo + 128] * Q_SCALE).astype(o_ref.dtype)
        o_ref[:, lo + 128:lo + 256] = (_rope_b(tile[:, lo + 128:lo + 256], c, a, b, 1.0) * Q_SCALE).astype(o_ref.dtype)


def _assemble_k_store(tile, o_ref, kr_ref):
    r = kr_ref[...]
    for hd in range(tile.shape[1] // QK_NOPE):
        o_ref[:, hd * QK_PAD:hd * QK_PAD + 128] = tile[:, hd * 128:(hd + 1) * 128].astype(o_ref.dtype)
        o_ref[:, hd * QK_PAD + 128:(hd + 1) * QK_PAD] = r


def _h1_bwd(h1, dlat_k, dlat_v, dkr_heads, dqn, dg16, kv_norm, q_norm, cos, sa, sb, *, ts):
    s = h1.shape[0]
    n = s // ts

    def body(h_ref, dk_ref, dv_ref, dkr_ref, dqn_ref, dg_ref, kn_ref, qn_ref, c_ref, sa_ref, sb_ref,
             o_ref, dkn_ref, dqn_out_ref, akn, aqn):
        i = pl.program_id(0)

        @pl.when(i == 0)
        def _():
            akn[...] = jnp.zeros_like(akn)
            aqn[...] = jnp.zeros_like(aqn)

        lat, r = _rms(h_ref[:, C_LAT])
        dlat = dk_ref[...] + dv_ref[...]
        akn[...] += _fold8(dlat * lat)
        o_ref[:, C_LAT] = _rms_bwd(dlat, lat, r, kn_ref[...]).astype(o_ref.dtype)

        dkr = dkr_ref[:, 0:128]
        for hd in range(1, H_B):
            dkr = dkr + dkr_ref[:, hd * 128:(hd + 1) * 128]
        o_ref[:, C_ROPE] = _rope_b(dkr, c_ref[...], sa_ref[...], sb_ref[...], -1.0).astype(o_ref.dtype)

        ql, rq = _rms(h_ref[:, C_QL])
        dq = dqn_ref[...]
        aqn[...] += _fold8(dq * ql)
        o_ref[:, C_QL] = _rms_bwd(dq, ql, rq, qn_ref[...]).astype(o_ref.dtype)
        o_ref[:, C_GATE] = dg_ref[...]

        @pl.when(i == n - 1)
        def _():
            dkn_ref[...] = jnp.sum(akn[...], axis=0, keepdims=True)
            dqn_out_ref[...] = jnp.sum(aqn[...], axis=0, keepdims=True)

    return pl.pallas_call(
        body, name="h1_bwd", grid=(n,),
        in_specs=[_row_spec(ts, H1_B), _row_spec(ts, KV_LORA), _row_spec(ts, KV_LORA), _row_spec(ts, H_B * 128),
                  _row_spec(ts, Q_LORA), _row_spec(ts, WIDTH_B), _bc_spec((1, KV_LORA)), _bc_spec((1, Q_LORA))]
        + [_row_spec(ts, 128)] * 3,
        out_specs=[_row_spec(ts, H1_B), _bc_spec((1, KV_LORA)), _bc_spec((1, Q_LORA))],
        out_shape=[jax.ShapeDtypeStruct((s, H1_B), BF16), jax.ShapeDtypeStruct((1, KV_LORA), F32),
                   jax.ShapeDtypeStruct((1, Q_LORA), F32)],
        scratch_shapes=[pltpu.VMEM((8, KV_LORA), F32), pltpu.VMEM((8, Q_LORA), F32)],
        compiler_params=_params(("arbitrary",)),
    )(h1, dlat_k, dlat_v, dkr_heads, dqn, dg16, kv_norm, q_norm, cos, sa, sb)


def _dot(a, b, ca, cb):
    return lax.dot_general(a, b, (((ca,), (cb,)), ((), ())), preferred_element_type=F32)


def _table_specs():
    full = lambda shape: pl.BlockSpec(shape, lambda i: (0,) * len(shape))
    return [full((H_A, CHUNK, CHUNK)), full((H_A, CHUNK, 1)), full((H_A, CHUNK, 1)), full((H_A, 1, DV_A))]


def _retention_fwd(h_a, cos, sin, tables):
    s = h_a.shape[0]
    n = s // CHUNK

    def body(h_ref, c_ref, s_ref, dm_ref, qd_ref, kd_ref, cd_ref, q_ref, k_ref, v_ref, o_ref, u_ref, st_ref, state):
        @pl.when(pl.program_id(0) == 0)
        def _():
            state[...] = jnp.zeros_like(state)

        c, sn = c_ref[...], s_ref[...]
        for hd in range(H_A):
            qs, vs = slice(hd * DK_A, (hd + 1) * DK_A), slice(hd * DV_A, (hd + 1) * DV_A)
            for r_ref, base, scale in ((q_ref, Q_COL, 1.0), (k_ref, K_COL, DK_A ** -0.5)):
                lo = base + hd * DK_A
                x1, x2 = h_ref[:, lo:lo + 128], h_ref[:, lo + 128:lo + 256]
                r_ref[:, hd * DK_A:hd * DK_A + 128] = ((x1 * c - x2 * sn) * scale).astype(r_ref.dtype)
                r_ref[:, hd * DK_A + 128:(hd + 1) * DK_A] = ((x2 * c + x1 * sn) * scale).astype(r_ref.dtype)
            v_ref[:, vs] = h_ref[:, V_COL + hd * DV_A:V_COL + (hd + 1) * DV_A].astype(v_ref.dtype)
            qv, kv, vv = q_ref[:, qs], k_ref[:, qs], v_ref[:, vs]
            st = state[hd]
            st16 = st.astype(BF16)
            st_ref[0, hd] = st16
            scores = _dot(qv, kv, 1, 1) * dm_ref[hd]
            qd = (qv.astype(F32) * qd_ref[hd]).astype(BF16)
            o = _dot(scores.astype(BF16), vv, 1, 0) + _dot(qd, st16, 1, 0)
            o_ref[:, vs] = o
            kd = (kv.astype(F32) * kd_ref[hd]).astype(BF16)
            state[hd] = st * cd_ref[hd] + _dot(kd, vv, 0, 0)
            on, _ = _group_norm(o)
            g = h_ref[:, GATE_COL + hd * DV_A:GATE_COL + (hd + 1) * DV_A]
            u_ref[:, vs] = (on * (g * _sigmoid(g))).astype(u_ref.dtype)

    row = lambda w: pl.BlockSpec((CHUNK, w), lambda i: (i, 0))
    return pl.pallas_call(
        body, name="retention_fwd", grid=(n,),
        in_specs=[row(IN_A), row(128), row(128)] + _table_specs(),
        out_specs=[row(H_A * DK_A), row(H_A * DK_A), row(WIDTH_A), row(WIDTH_A), row(WIDTH_A),
                   pl.BlockSpec((1, H_A, DK_A, DV_A), lambda i: (i, 0, 0, 0))],
        out_shape=[jax.ShapeDtypeStruct((s, H_A * DK_A), BF16), jax.ShapeDtypeStruct((s, H_A * DK_A), BF16),
                   jax.ShapeDtypeStruct((s, WIDTH_A), BF16), jax.ShapeDtypeStruct((s, WIDTH_A), F32),
                   jax.ShapeDtypeStruct((s, WIDTH_A), BF16), jax.ShapeDtypeStruct((n, H_A, DK_A, DV_A), BF16)],
        scratch_shapes=[pltpu.VMEM((H_A, DK_A, DV_A), F32)],
        compiler_params=_params(("arbitrary",)),
    )(h_a, cos, sin, *tables)


def _retention_bwd(q, k, v, states, du, o, h_a, cos, sin, tables):
    s = q.shape[0]
    n = s // CHUNK

    def body(q_ref, k_ref, v_ref, st_ref, du_ref, o_ref, g_ref, c_ref, s_ref, dm_ref, qd_ref, kd_ref, cd_ref, dh_ref, grad_state):
        @pl.when(pl.program_id(0) == 0)
        def _():
            grad_state[...] = jnp.zeros_like(grad_state)

        c, sn = c_ref[...], s_ref[...]
        for hd in range(H_A):
            qs, vs = slice(hd * DK_A, (hd + 1) * DK_A), slice(hd * DV_A, (hd + 1) * DV_A)
            on, rstd = _group_norm(o_ref[:, vs])
            g = g_ref[:, vs]
            sg = _sigmoid(g)
            du_v = du_ref[:, vs]
            don = du_v * (g * sg)
            dh_ref[:, GATE_COL + hd * DV_A:GATE_COL + (hd + 1) * DV_A] = (du_v * on * (sg * (1.0 + g * (1.0 - sg)))).astype(dh_ref.dtype)
            m1 = jnp.mean(don, axis=-1, keepdims=True)
            m2 = jnp.mean(don * on, axis=-1, keepdims=True)
            dov = (rstd * (don - m1 - on * m2)).astype(BF16)

            qv, kv, vv = q_ref[:, qs], k_ref[:, qs], v_ref[:, vs]
            dm = dm_ref[hd]
            gs = grad_state[hd]
            g16 = gs.astype(BF16)
            scores = (_dot(qv, kv, 1, 1) * dm).astype(BF16)
            dscores = (_dot(dov, vv, 1, 1) * dm).astype(BF16)
            qd = (qv.astype(F32) * qd_ref[hd]).astype(BF16)
            kd = (kv.astype(F32) * kd_ref[hd]).astype(BF16)
            dq = _dot(dscores, kv, 1, 0) + _dot(dov, st_ref[0, hd], 1, 1) * qd_ref[hd]
            dk = (_dot(dscores, qv, 0, 0) + _dot(vv, g16, 1, 1) * kd_ref[hd]) * (DK_A ** -0.5)
            dh_ref[:, V_COL + hd * DV_A:V_COL + (hd + 1) * DV_A] = (_dot(scores, dov, 0, 0) + _dot(kd, g16, 1, 0)).astype(dh_ref.dtype)
            grad_state[hd] = gs * cd_ref[hd] + _dot(qd, dov, 0, 0)
            for d, base in ((dq, Q_COL), (dk, K_COL)):
                lo = base + hd * DK_A
                d1, d2 = d[:, 0:128], d[:, 128:256]
                dh_ref[:, lo:lo + 128] = (d1 * c + d2 * sn).astype(dh_ref.dtype)
                dh_ref[:, lo + 128:lo + 256] = (d2 * c - d1 * sn).astype(dh_ref.dtype)

    rev = lambda i: n - 1 - i
    row = lambda w, col=0: pl.BlockSpec((CHUNK, w), lambda i: (rev(i), col))
    return pl.pallas_call(
        body, name="retention_bwd", grid=(n,),
        in_specs=[row(H_A * DK_A), row(H_A * DK_A), row(WIDTH_A), pl.BlockSpec((1, H_A, DK_A, DV_A), lambda i: (rev(i), 0, 0, 0)),
                  row(WIDTH_A), row(WIDTH_A), row(WIDTH_A, GATE_COL // WIDTH_A), row(128), row(128)] + _table_specs(),
        out_specs=row(IN_A),
        out_shape=jax.ShapeDtypeStruct((s, IN_A), BF16),
        scratch_shapes=[pltpu.VMEM((H_A, DK_A, DV_A), F32)],
        compiler_params=_params(("arbitrary",)),
    )(q, k, v, states, du, o, h_a, cos, sin, *tables)


GATE_BLOCK0 = (KV_DOWN_PAD + Q_LORA) // LANES


def _causal_mask(sc, row0):
    row = lax.broadcasted_iota(jnp.int32, sc.shape, 0) + row0
    col = lax.broadcasted_iota(jnp.int32, sc.shape, 1)
    return jnp.where(col <= row, sc, NEG_BIG)


def _key_block_loop(step, n, per_trip, smallest=1):
    def trip_body(jj, carry):
        for t in range(per_trip):
            step(per_trip * jj + t)
        return carry

    lax.fori_loop(0, n // per_trip, trip_body, 0)
    group = per_trip // 2
    while group >= smallest:
        def tail(group=group):
            first = (n // (2 * group)) * (2 * group)
            for t in range(group):
                step(first + t)

        pl.when((n // group) % 2 == 1)(tail)
        group //= 2


def _attention_fwd(q, k, v, h1, *, blk, bk, sub, per_trip):
    s = q.shape[0]
    nb = s // blk

    def body(q_ref, k_ref, v_ref, g_ref, o_ref, u_ref, lse_ref, vext_s, m_s, acc_s):
        i = pl.program_id(1)

        @pl.when(i == 0)
        def _():
            vext_s[:, 0:V_HEAD] = v_ref[...]
            vext_s[:, V_HEAD:2 * V_HEAD] = jnp.ones((s, V_HEAD), vext_s.dtype)

        m_s[...] = jnp.full_like(m_s, NEG_BIG)
        acc_s[...] = jnp.zeros_like(acc_s)

        def update(rows, kb, vb, row0):
            sc = _dot(q_ref[rows, :], kb, 1, 1)
            if row0 is not None:
                sc = _causal_mask(sc, row0)
            m_prev = m_s[rows, :]
            m_new = jnp.maximum(m_prev, jnp.max(sc, axis=-1, keepdims=True))
            p = jnp.exp2(sc - jnp.tile(m_new, (1, kb.shape[0] // LANES)))
            a = jnp.exp2(m_prev - m_new)
            acc_s[rows, :] = jnp.tile(a, (1, 2)) * acc_s[rows, :] + _dot(p.astype(BF16), vb, 1, 0)
            m_s[rows, :] = m_new

        def step(j):
            kv_rows = pl.ds(pl.multiple_of(j * bk, bk), bk)
            kb, vb = k_ref[kv_rows, :], vext_s[kv_rows, :]
            for r in range(blk // sub):
                update(slice(r * sub, (r + 1) * sub), kb, vb, None)

        _key_block_loop(step, i * (blk // bk), per_trip, smallest=blk // bk)
        for r in range(blk // sub):
            ncols = (r + 1) * sub
            kv_rows = pl.ds(pl.multiple_of(i * blk, blk), ncols)
            update(slice(r * sub, (r + 1) * sub), k_ref[kv_rows, :], vext_s[kv_rows, :], r * sub)
        acc = acc_s[...]
        l = acc[:, V_HEAD:2 * V_HEAD]
        o = acc[:, 0:V_HEAD] / l
        g = g_ref[...]
        o_ref[...] = o
        u_ref[...] = (o * (g * _sigmoid(g))).astype(u_ref.dtype)
        lse_ref[0] = (m_s[...] + jnp.log2(l))[:, 0:1]

    blk_idx = lambda h, i: (i, h)
    return pl.pallas_call(
        body, name="attention_fwd", grid=(H_B, nb),
        in_specs=[pl.BlockSpec((blk, QK_PAD), blk_idx), pl.BlockSpec((s, QK_PAD), lambda h, i: (0, h)),
                  pl.BlockSpec((s, V_HEAD), lambda h, i: (0, h)), pl.BlockSpec((blk, LANES), lambda h, i: (i, GATE_BLOCK0 + h))],
        out_specs=[pl.BlockSpec((blk, V_HEAD), blk_idx), pl.BlockSpec((blk, V_HEAD), blk_idx),
                   pl.BlockSpec((1, blk, 1), lambda h, i: (h, i, 0))],
        out_shape=[jax.ShapeDtypeStruct((s, WIDTH_B), F32), jax.ShapeDtypeStruct((s, WIDTH_B), BF16),
                   jax.ShapeDtypeStruct((H_B, s, 1), F32)],
        scratch_shapes=[pltpu.VMEM((s, 2 * V_HEAD), BF16), pltpu.VMEM((blk, LANES), F32), pltpu.VMEM((blk, 2 * V_HEAD), F32)],
        compiler_params=_params(("parallel", "arbitrary")),
    )(q, k, v, h1)


def _attention_bwd(q, k, v, du, o, h1, lse, cos, sa, sb, *, blk, bk, per_trip):
    s = q.shape[0]
    nb = s // blk

    def body(q_ref, k_ref, v_ref, du_ref, o_ref, g_ref, lse_ref, c_ref, sa_ref, sb_ref,
             dq_ref, dkn_ref, dkr_ref, dv_ref, dg_ref, lse_s, dl_s, do_s, dq_s):
        i = pl.program_id(1)
        g = g_ref[...]
        sg = _sigmoid(g)
        du_v, ov = du_ref[...], o_ref[...]
        do = du_v * (g * sg)
        dg_ref[...] = (du_v * ov * (sg * (1.0 + g * (1.0 - sg)))).astype(dg_ref.dtype)
        do_s[...] = do.astype(do_s.dtype)
        dl_s[...] = jnp.broadcast_to(jnp.sum(do * ov, axis=-1, keepdims=True), (blk, LANES))
        lse_s[...] = jnp.broadcast_to(lse_ref[0], (blk, LANES))
        dq_s[...] = jnp.zeros_like(dq_s)

        def products(rows, kv_rows, row0):
            qv, dov, kb = q_ref[rows, :], do_s[rows, :], k_ref[kv_rows, :]
            tile = (1, kb.shape[0] // LANES)
            sc = _dot(qv, kb, 1, 1)
            if row0 is not None:
                sc = _causal_mask(sc, row0)
            p = jnp.exp2(sc - jnp.tile(lse_s[rows, :], tile))
            dp = _dot(dov, v_ref[kv_rows, :], 1, 1)
            ds = (p * (dp - jnp.tile(dl_s[rows, :], tile))).astype(BF16)
            dq_s[rows, :] += _dot(ds, kb, 1, 0)
            return _dot(ds, qv, 0, 0), _dot(p.astype(BF16), dov, 0, 0)

        def put(kv_rows, dk_c, dv_c, first):
            if first:
                dkn_ref[kv_rows, :] = dk_c[:, 0:128]
                dkr_ref[kv_rows, :] = dk_c[:, 128:256]
                dv_ref[kv_rows, :] = dv_c
            else:
                dkn_ref[kv_rows, :] += dk_c[:, 0:128]
                dkr_ref[kv_rows, :] += dk_c[:, 128:256]
                dv_ref[kv_rows, :] += dv_c

        n_sub = blk // bk
        sub_rows = [slice(r * bk, (r + 1) * bk) for r in range(n_sub)]

        def step(j):
            kv_rows = pl.ds(pl.multiple_of(j * bk, bk), bk)
            for rows in sub_rows:
                dk_c, dv_c = products(rows, kv_rows, None)
                put(kv_rows, dk_c, dv_c, False)

        _key_block_loop(step, i * n_sub, per_trip, smallest=n_sub)
        for c in range(n_sub):
            kv_rows = pl.ds(pl.multiple_of(i * blk + c * bk, bk), bk)
            for r in range(c, n_sub):
                dk_c, dv_c = products(sub_rows[r], kv_rows, 0 if r == c else None)
                put(kv_rows, dk_c, dv_c, r == c)
        dq = dq_s[...] * ATT_SCALE
        dq_ref[:, 0:128] = dq[:, 0:128].astype(dq_ref.dtype)
        dq_ref[:, 128:256] = _rope_b(dq[:, 128:256], c_ref[...], sa_ref[...], sb_ref[...], -1.0).astype(dq_ref.dtype)

        @pl.when(i == nb - 1)
        def _():
            dkn_ref[...] = dkn_ref[...] * LN2
            dkr_ref[...] = dkr_ref[...] * LN2

    head = lambda h, i: (0, h)
    blk_idx = lambda h, i: (i, h)
    row_idx = lambda h, i: (i, 0)
    return pl.pallas_call(
        body, name="attention_bwd", grid=(H_B, nb),
        in_specs=[pl.BlockSpec((blk, QK_PAD), blk_idx), pl.BlockSpec((s, QK_PAD), head), pl.BlockSpec((s, V_HEAD), head),
                  pl.BlockSpec((blk, V_HEAD), blk_idx), pl.BlockSpec((blk, V_HEAD), blk_idx),
                  pl.BlockSpec((blk, LANES), lambda h, i: (i, GATE_BLOCK0 + h)), pl.BlockSpec((1, blk, 1), lambda h, i: (h, i, 0)),
                  pl.BlockSpec((blk, LANES), row_idx), pl.BlockSpec((blk, LANES), row_idx), pl.BlockSpec((blk, LANES), row_idx)],
        out_specs=[pl.BlockSpec((blk, QK_PAD), blk_idx), pl.BlockSpec((s, 128), head), pl.BlockSpec((s, 128), head),
                   pl.BlockSpec((s, 128), head), pl.BlockSpec((blk, V_HEAD), blk_idx)],
        out_shape=[jax.ShapeDtypeStruct((s, H_B * QK_PAD), BF16), jax.ShapeDtypeStruct((s, H_B * 128), F32),
                   jax.ShapeDtypeStruct((s, H_B * 128), F32), jax.ShapeDtypeStruct((s, H_B * 128), F32),
                   jax.ShapeDtypeStruct((s, WIDTH_B), BF16)],
        scratch_shapes=[pltpu.VMEM((blk, LANES), F32), pltpu.VMEM((blk, LANES), F32), pltpu.VMEM((blk, V_HEAD), BF16),
                        pltpu.VMEM((blk, QK_PAD), F32)],
        compiler_params=_params(("parallel", "arbitrary")),
    )(q, k, v, du, o, h1, lse, cos, sa, sb)


def _local_step(x, target, w, kv_norm, q_norm, ln_g, ln_b, *, ts=256, blk=512, late=None, reduce=None):
    s = x.shape[0]
    cos_a, sin_a = _rope_tables_a(s)
    cos_b, sa_b, sb_b = _rope_tables_b(s)
    tables = _retention_tables()
    g0, g1, b0, b1 = ln_g[0:1], ln_g[1:2], ln_b[0:1], ln_b[1:2]

    x16 = x.astype(BF16)
    if late is None:
        h_a = _mm(x16, w["a_in"], tn=1536, name="a_in_fwd")
    else:
        flat_b, chip = late
        h_a, got = _mm(x16, w["a_in"], tn=1536, name="a_in_fwd", side=_gather_side(flat_b))
        got = lax.dynamic_update_slice(got, flat_b[None], (chip, 0, 0))
        w = {**w, **_kernel_layout_b(_full_from_gathered(got, B_SHARDS))}
    q_a, k_a, v_a, o_a, u_a, states = _retention_fwd(h_a, cos_a, sin_a, tables)
    y_a = _mm(u_a, w["a_out"], tn=1024, name="a_out_fwd")
    x1, x1_16 = _ln_fwd(x, y_a, g0, b0, ts=ts)

    h1 = _mm(x1_16, w["b_in1"], tn=1152, name="b_in_fwd")
    lat16, kr16, qn16 = _kvq_prep(h1, kv_norm, q_norm, cos_b, sa_b, sb_b, ts=ts)
    k16 = _mm(lat16, w["up_k"], out_dtype=BF16, tn=2048, out_tn=H_B * QK_PAD, extras=(kr16,), epilogue=_assemble_k_store, name="up_k_fwd")
    v16 = _mm(lat16, w["up_v"], out_dtype=BF16, tn=2048, name="up_v_fwd")
    q16 = _mm(qn16, w["uq"], out_dtype=BF16, tn=2048, extras=(cos_b, sa_b, sb_b), epilogue=_rope_q_store, name="uq_fwd")
    o_b, u_b, lse = _attention_fwd(q16, k16, v16, h1, blk=2 * blk, bk=blk, sub=blk // 2, per_trip=4)
    y_b = _mm(u_b, w["b_out"], tn=1024, name="b_out_fwd")

    dz_b, dz_b16, dg1, db1, loss = _ln_loss_bwd(x1, y_b, target, g1, b1, ts=ts)
    d_b_out = _mm(u_b, dz_b16, ta=True, tn=1024, tk=1024, name="b_out_dw")
    du_b = _mm(dz_b16, w["b_out"], tb=True, tn=1024, name="b_out_dx")
    dqf16, dkn, dkr_heads, dv, dgate16 = _attention_bwd(q16, k16, v16, du_b, o_b, h1, lse, cos_b, sa_b, sb_b, blk=2 * blk, bk=blk, per_trip=4)
    d_uq = _mm(qn16, dqf16, ta=True, tm=768, tn=2048, tk=1024, name="uq_dw")
    dqn = _mm(dqf16, w["uq"], tb=True, tn=768, tk=2048, name="uq_dx")
    d_up_k = _mm(lat16, dkn, ta=True, tn=2048, tk=1024, name="up_k_dw")
    d_up_v = _mm(lat16, dv, ta=True, tn=2048, tk=1024, name="up_v_dw")
    dlat_k = _mm(dkn, w["up_k"], tb=True, tn=512, name="up_k_dx")
    dlat_v = _mm(dv, w["up_v"], tb=True, tn=512, name="up_v_dx")
    dh1, dkvn, dqnorm = _h1_bwd(h1, dlat_k, dlat_v, dkr_heads, dqn, dgate16, kv_norm, q_norm, cos_b, sa_b, sb_b, ts=ts)
    d_b_in1 = _mm(x1_16, dh1, ta=True, tn=1152, tk=1024, name="b_in_dw")
    dx1 = _mm(dh1, w["b_in1"], tb=True, tn=1024, extras=(dz_b,), epilogue=_residual_store, name="b_in_dx")

    dz_a, dz_a16, dg0, db0 = _ln_bwd_call(dx1, x, y_a, g0, ts=ts)
    d_a_out = _mm(u_a, dz_a16, ta=True, tn=1024, tk=1024, name="a_out_dw")
    du_a = _mm(dz_a16, w["a_out"], tb=True, tn=1024, name="a_out_dx")
    dh_a = _retention_bwd(q_a, k_a, v_a, states, du_a, o_a, h_a, cos_a, sin_a, tables)
    grads = dict(a_out=d_a_out, b_in1=d_b_in1, uq=d_uq, b_out=d_b_out, up_k=d_up_k, up_v=d_up_v)
    small = dict(ln_g=jnp.concatenate([dg0, dg1], axis=0), ln_b=jnp.concatenate([db0, db1], axis=0),
                 q_norm=dqnorm, kv_norm=dkvn)
    if reduce is None:
        grads["a_in"] = _mm(x16, dh_a, ta=True, tn=1536, tk=1024, name="a_in_dw")
        grad_x = _mm(dh_a, w["a_in"], tb=True, tn=1024, tk=2048, extras=(dz_a,), epilogue=_residual_store, name="a_in_dx")
        return loss, grad_x, grads, small
    own_early, travel_early = reduce(_reference_layout_grads(grads), EARLY_SHARDS, EARLY_ROWS, "early")
    d_a_in, got_early = _mm(x16, dh_a, ta=True, tn=1536, tk=1024, name="a_in_dw", side=_chip_exchange_side(travel_early))
    own_late, travel_late = reduce(dict(a_w_in=d_a_in), LATE_SHARDS, LATE_ROWS, "late")
    grad_x, got_late = _mm(dh_a, w["a_in"], tb=True, tn=1024, tk=2048, extras=(dz_a,), epilogue=_residual_store, name="a_in_dx",
                           side=_chip_exchange_side(travel_late))
    return loss, grad_x, ((own_early, got_early), (own_late, got_late)), small


def _flat_shards(shards, dtype, which, total_rows):
    parts = [shards[name].reshape(rows, FLAT_COLS) for name, rows in which]
    used = sum(rows for _, rows in which)
    parts.append(jnp.zeros((total_rows - used, FLAT_COLS), parts[0].dtype))
    return jnp.concatenate(parts, axis=0).astype(dtype)


def _unflat_shards(flat, shapes, shards):
    out, off = {}, 0
    for name, rows in shards:
        out[name] = flat[off:off + rows].reshape(shapes[name])
        off += rows
    return out


COL_SHARDED = {"a_w_in": (D_MODEL, IN_A), "b_w_in": (D_MODEL, IN_B), "b_w_uq": (Q_LORA, H_B * (QK_NOPE + QK_ROPE)),
               "kv_w_up": (KV_LORA, H_B * (QK_NOPE + V_HEAD))}
ROW_SHARDED = {"a_w_out": (WIDTH_A, D_MODEL), "b_w_out": (WIDTH_B, D_MODEL), "kv_w_down": (D_MODEL, KV_LORA + QK_ROPE)}


def _full_from_gathered(gathered, shards=SHARD_ROWS):
    out, off = {}, 0
    for name, rows in shards:
        part = gathered[:, off:off + rows]
        off += rows
        if name in COL_SHARDED:
            r, c = COL_SHARDED[name]
            out[name] = part.reshape(N_CHIPS, r, c // N_CHIPS).transpose(1, 0, 2).reshape(r, c)
        else:
            r, c = ROW_SHARDED[name]
            out[name] = part.reshape(r, c)
    return out


def _chip_major(g):
    r, c = g.shape
    return g.reshape(r, N_CHIPS, c // N_CHIPS).transpose(1, 0, 2)


def _gathered_from_full(full, shards, total_rows):
    if len(shards) == 1 and shards[0][0] in COL_SHARDED:
        return _chip_major(full[shards[0][0]])
    parts = []
    for name, rows in shards:
        g = full[name]
        if name in COL_SHARDED:
            r, c = COL_SHARDED[name]
            g = g.reshape(r, N_CHIPS, c // N_CHIPS).transpose(1, 0, 2)
        parts.append(g.reshape(N_CHIPS, rows, FLAT_COLS))
    used = sum(rows for _, rows in shards)
    if total_rows > used:
        parts.append(jnp.zeros((N_CHIPS, total_rows - used, FLAT_COLS), F32))
    return jnp.concatenate(parts, axis=1)


A_SHARDS, B_SHARDS = SHARD_ROWS[:1], SHARD_ROWS[1:]
A_ROWS = sum(rows for _, rows in A_SHARDS)


def _kernel_layout_a(full):
    return dict(a_in=full["a_w_in"])


def _kernel_layout_b(full):
    uq = full["b_w_uq"].reshape(Q_LORA, H_B, QK_NOPE + QK_ROPE)
    uq = jnp.pad(uq, ((0, 0), (0, 0), (0, QK_PAD - QK_NOPE - QK_ROPE))).reshape(Q_LORA, H_B * QK_PAD)
    up = full["kv_w_up"].reshape(KV_LORA, H_B, QK_NOPE + V_HEAD)
    down = jnp.pad(full["kv_w_down"], ((0, 0), (0, KV_DOWN_PAD - KV_LORA - QK_ROPE)))
    return dict(a_out=full["a_w_out"], b_out=full["b_w_out"], uq=uq,
                up_k=up[:, :, :QK_NOPE].reshape(KV_LORA, H_B * QK_NOPE),
                up_v=up[:, :, QK_NOPE:].reshape(KV_LORA, H_B * V_HEAD),
                b_in1=jnp.concatenate([down, full["b_w_in"]], axis=1))


def _kernel_layout(full):
    return {**_kernel_layout_a(full), **_kernel_layout_b(full)}


EARLY_SHARDS = tuple(sh for sh in SHARD_ROWS if sh[0] != "a_w_in")
LATE_SHARDS = tuple(sh for sh in SHARD_ROWS if sh[0] == "a_w_in")
EARLY_ROWS, LATE_ROWS = 3072, 1536


def _reference_layout_grads(g):
    uq = g["uq"].reshape(Q_LORA, H_B, QK_PAD)[:, :, :QK_NOPE + QK_ROPE].reshape(Q_LORA, H_B * (QK_NOPE + QK_ROPE))
    up = jnp.concatenate([g["up_k"].reshape(KV_LORA, H_B, QK_NOPE), g["up_v"].reshape(KV_LORA, H_B, V_HEAD)], axis=2)
    return dict(a_w_out=g["a_out"], b_w_out=g["b_out"], b_w_uq=uq,
                kv_w_up=up.reshape(KV_LORA, H_B * (QK_NOPE + V_HEAD)),
                kv_w_down=g["b_in1"][:, :KV_LORA + QK_ROPE], b_w_in=g["b_in1"][:, KV_DOWN_PAD:])


HBM_SPEC = pl.BlockSpec(memory_space=pl.ANY)


def _me():
    return lax.axis_index("x"), lax.axis_index("y"), lax.axis_index("c")


def _chip_flips(x, y):
    return [(1 - x, y), (x, 1 - y), (1 - x, 1 - y)]


def _gather_copies(src_ref, out_ref, send_sems, recv_sems):
    x, y, c = _me()
    half = src_ref.shape[0] // 2
    my_rows = pl.ds(pl.multiple_of(c * half, 16), half)
    their_rows = pl.ds(pl.multiple_of((1 - c) * half, 16), half)
    chips = _chip_flips(x, y)
    sibling = (x, y, 1 - c)

    def copy(k, src, dst, to):
        return pltpu.make_async_remote_copy(src_ref=src, dst_ref=dst, send_sem=send_sems.at[k], recv_sem=recv_sems.at[k],
                                            device_id=to, device_id_type=MESH)

    sends = [copy(k, src_ref.at[my_rows, :], out_ref.at[2 * x + y, my_rows, :], (px, py, c)) for k, (px, py) in enumerate(chips)]
    landed = [out_ref.at[2 * px + py, my_rows, :] for px, py in chips]
    lands = [copy(k, landed[k], landed[k], (px, py, c)) for k, (px, py) in enumerate(chips)]
    forwards = [copy(3 + k, landed[k], landed[k], sibling) for k in range(3)]
    theirs = [out_ref.at[2 * px + py, their_rows, :] for px, py in chips]
    arrivals = [copy(3 + k, theirs[k], theirs[k], sibling) for k in range(3)]
    return sends, lands, forwards, arrivals


def _gather_start(src_ref, out_ref, send_sems, recv_sems):
    sends, _, _, _ = _gather_copies(src_ref, out_ref, send_sems, recv_sems)
    for cp in sends:
        cp.start()


def _gather_finish(src_ref, out_ref, send_sems, recv_sems):
    sends, lands, forwards, arrivals = _gather_copies(src_ref, out_ref, send_sems, recv_sems)
    for k in range(3):
        lands[k].wait_recv()
        forwards[k].start()
    for cp in arrivals:
        cp.wait_recv()
    for cp in sends + forwards:
        cp.wait_send()


def _gather_scratch():
    return [pltpu.SemaphoreType.DMA((6,)), pltpu.SemaphoreType.DMA((6,))]


def _gather_weights(flat16):
    def body(src_ref, out_ref, send_sems, recv_sems):
        _gather_start(src_ref, out_ref, send_sems, recv_sems)
        _gather_finish(src_ref, out_ref, send_sems, recv_sems)

    return pl.pallas_call(
        body, name="gather_weights",
        in_specs=[HBM_SPEC], out_specs=HBM_SPEC,
        out_shape=jax.ShapeDtypeStruct((N_CHIPS,) + flat16.shape, flat16.dtype),
        scratch_shapes=_gather_scratch(),
    )(flat16)


def _gather_side(flat16):
    return dict(inputs=[flat16], out_shape=jax.ShapeDtypeStruct((N_CHIPS,) + flat16.shape, flat16.dtype),
                scratch=_gather_scratch(), start=_gather_start, finish=_gather_finish)


def _chip_exchange_copies(p_ref, out_ref, send_sems, recv_sems):
    x, y, c = _me()
    return [pltpu.make_async_remote_copy(
        src_ref=p_ref.at[2 * px + py], dst_ref=out_ref.at[k], send_sem=send_sems.at[k], recv_sem=recv_sems.at[k],
        device_id=(px, py, c), device_id_type=MESH) for k, (px, py) in enumerate(_chip_flips(x, y))]


def _chip_exchange_start(p_ref, out_ref, send_sems, recv_sems):
    for cp in _chip_exchange_copies(p_ref, out_ref, send_sems, recv_sems):
        cp.start()


def _chip_exchange_finish(p_ref, out_ref, send_sems, recv_sems):
    copies = _chip_exchange_copies(p_ref, out_ref, send_sems, recv_sems)
    for cp in copies:
        cp.wait_send()
    for cp in copies:
        cp.wait_recv()


def _chip_exchange_side(p):
    return dict(inputs=[p], out_shape=jax.ShapeDtypeStruct((3,) + p.shape[1:], p.dtype),
                scratch=[pltpu.SemaphoreType.DMA((3,)), pltpu.SemaphoreType.DMA((3,))],
                start=_chip_exchange_start, finish=_chip_exchange_finish)


def _pair_swap(r, name):
    def body(r_ref, out_ref, send_sem, recv_sem):
        x, y, c = _me()
        cp = pltpu.make_async_remote_copy(src_ref=r_ref, dst_ref=out_ref, send_sem=send_sem, recv_sem=recv_sem,
                                          device_id=(x, y, 1 - c), device_id_type=MESH)
        cp.start()
        cp.wait_send()
        cp.wait_recv()

    return pl.pallas_call(
        body, name=name,
        in_specs=[HBM_SPEC], out_specs=HBM_SPEC,
        out_shape=jax.ShapeDtypeStruct(r.shape, r.dtype),
        scratch_shapes=[pltpu.SemaphoreType.DMA, pltpu.SemaphoreType.DMA],
    )(r)


def _sum_small(vec):
    def body(v_ref, out_ref, slots, send_sems, recv_sems):
        x, y, c = _me()
        me = 4 * x + 2 * y + c
        slots[me] = v_ref[...]
        flips = [(fx, fy, fc) for fx in (0, 1) for fy in (0, 1) for fc in (0, 1)][1:]
        copies = []
        for k, (fx, fy, fc) in enumerate(flips):
            copies.append(pltpu.make_async_remote_copy(
                src_ref=v_ref, dst_ref=slots.at[me], send_sem=send_sems.at[k], recv_sem=recv_sems.at[k],
                device_id=(x ^ fx, y ^ fy, c ^ fc), device_id_type=MESH))
        for cp in copies:
            cp.start()
        for cp in copies:
            cp.wait_send()
        for k, (fx, fy, fc) in enumerate(flips):
            src = 4 * (x ^ fx) + 2 * (y ^ fy) + (c ^ fc)
            pltpu.make_async_remote_copy(
                src_ref=v_ref, dst_ref=slots.at[src], send_sem=send_sems.at[k], recv_sem=recv_sems.at[k],
                device_id=(x ^ fx, y ^ fy, c ^ fc), device_id_type=MESH).wait_recv()
        total = slots[0]
        for d in range(1, N_DEV):
            total = total + slots[d]
        out_ref[...] = total

    return pl.pallas_call(
        body, name="sum_small",
        in_specs=[pl.BlockSpec(memory_space=pltpu.VMEM)], out_specs=pl.BlockSpec(memory_space=pltpu.VMEM),
        out_shape=jax.ShapeDtypeStruct(vec.shape, vec.dtype),
        scratch_shapes=[pltpu.VMEM((N_DEV,) + vec.shape, vec.dtype), pltpu.SemaphoreType.DMA((7,)),
                        pltpu.SemaphoreType.DMA((7,))],
    )(vec)


UPD_ROWS = 256


def _pair_sum(g, theirs, core, chip, name):
    half, cols = theirs.shape[1:]
    nb = half // UPD_ROWS

    def body(core_ref, chip_ref, g_ref, t_ref, own_ref, o16_ref):
        total = g_ref[0] + t_ref[0].astype(F32)
        o16_ref[0] = total.astype(o16_ref.dtype)

        @pl.when(pl.program_id(1) == chip_ref[0])
        def _():
            own_ref[...] = total

    return pl.pallas_call(
        body, name=name,
        grid_spec=pltpu.PrefetchScalarGridSpec(
            num_scalar_prefetch=2, grid=(nb, N_CHIPS),
            in_specs=[pl.BlockSpec((1, UPD_ROWS, cols), lambda i, d, core_ref, chip_ref: (d, core_ref[0] * nb + i, 0)),
                      pl.BlockSpec((1, UPD_ROWS, cols), lambda i, d, core_ref, chip_ref: (d, i, 0))],
            out_specs=[pl.BlockSpec((UPD_ROWS, cols), lambda i, d, core_ref, chip_ref: (i, 0)),
                       pl.BlockSpec((1, UPD_ROWS, cols), lambda i, d, core_ref, chip_ref: (d, i, 0))]),
        out_shape=[jax.ShapeDtypeStruct((half, cols), F32),
                   jax.ShapeDtypeStruct((N_CHIPS, half, cols), BF16)],
        compiler_params=_params(("parallel", "arbitrary")),
    )(core, chip, g, theirs)


def _chip_sum(own, received, name):
    half, cols = own.shape
    nb = half // UPD_ROWS

    def body(p_ref, r_ref, o_ref):
        o_ref[...] = ((p_ref[...] + r_ref[0].astype(F32)) + r_ref[1].astype(F32)) + r_ref[2].astype(F32)

    return pl.pallas_call(
        body, name=name, grid=(nb,),
        in_specs=[pl.BlockSpec((UPD_ROWS, cols), lambda i: (i, 0)),
                  pl.BlockSpec((3, UPD_ROWS, cols), lambda i: (0, i, 0))],
        out_specs=pl.BlockSpec((UPD_ROWS, cols), lambda i: (i, 0)),
        out_shape=jax.ShapeDtypeStruct((half, cols), F32),
        compiler_params=_params(("parallel",)),
    )(own, received)


def _adamw(w, g, m, v, *, rows, name):
    r, c = w.shape
    rows = min(rows, r)
    assert r % rows == 0

    def body(w_ref, g_ref, m_ref, v_ref, d_ref, nm_ref, nv_ref):
        gv = g_ref[...]
        nm = ADAM_B1 * m_ref[...] + (1.0 - ADAM_B1) * gv
        nv = ADAM_B2 * v_ref[...] + (1.0 - ADAM_B2) * (gv * gv)
        m_hat = nm / (1.0 - ADAM_B1 ** ADAM_STEP)
        v_hat = nv / (1.0 - ADAM_B2 ** ADAM_STEP)
        d_ref[...] = -ADAM_LR * (m_hat / (jnp.sqrt(v_hat) + ADAM_EPS) + ADAM_WD * w_ref[...])
        nm_ref[...] = nm
        nv_ref[...] = nv

    spec = pl.BlockSpec((rows, c), lambda i: (i, 0))
    return pl.pallas_call(
        body, name=name, grid=(r // rows,),
        in_specs=[spec] * 4, out_specs=[spec] * 3,
        out_shape=[jax.ShapeDtypeStruct((r, c), F32)] * 3,
        compiler_params=_params(("parallel",)),
    )(w, g, m, v)


W_NAMES = ("a_w_in", "a_w_out", "b_w_in", "b_q_norm", "b_w_uq", "b_w_out", "kv_w_down", "kv_norm", "kv_w_up", "ln_g", "ln_b")
BIG = tuple(name for name, _ in SHARD_ROWS)


def _pack_small(ln_g, ln_b, q_norm, kv_norm, extra=None):
    pad = lambda a: jnp.pad(a.reshape(1, -1), ((0, 0), (0, FLAT_COLS - a.size)))
    rows = [ln_g, ln_b, pad(q_norm), pad(kv_norm),
            jnp.zeros((1, FLAT_COLS), F32) if extra is None else pad(extra), jnp.zeros((1, FLAT_COLS), F32)]
    return jnp.concatenate(rows, axis=0)


def _unpack_small(p):
    return dict(ln_g=p[0:2], ln_b=p[2:4], b_q_norm=p[4:5, :Q_LORA], kv_norm=p[5, :KV_LORA])


def kernel(x, a_w_in, a_w_out, b_w_in, b_q_norm, b_w_uq, b_w_out, kv_w_down, kv_norm, kv_w_up, ln_g, ln_b, loss_target, m_a_w_in, m_a_w_out, m_b_w_in, m_b_q_norm, m_b_w_uq, m_b_w_out, m_kv_w_down, m_kv_norm, m_kv_w_up, m_ln_g, m_ln_b, v_a_w_in, v_a_w_out, v_b_w_in, v_b_q_norm, v_b_w_uq, v_b_w_out, v_kv_w_down, v_kv_norm, v_kv_w_up, v_ln_g, v_ln_b):
    w_in = dict(a_w_in=a_w_in[0], a_w_out=a_w_out[0], b_w_in=b_w_in[0], b_w_uq=b_w_uq[0], b_w_out=b_w_out[0],
                kv_w_down=kv_w_down, kv_w_up=kv_w_up)
    m_in = dict(a_w_in=m_a_w_in[0], a_w_out=m_a_w_out[0], b_w_in=m_b_w_in[0], b_w_uq=m_b_w_uq[0], b_w_out=m_b_w_out[0],
                kv_w_down=m_kv_w_down, kv_w_up=m_kv_w_up)
    v_in = dict(a_w_in=v_a_w_in[0], a_w_out=v_a_w_out[0], b_w_in=v_b_w_in[0], b_w_uq=v_b_w_uq[0], b_w_out=v_b_w_out[0],
                kv_w_down=v_kv_w_down, kv_w_up=v_kv_w_up)
    shard_shapes = {name: w_in[name].shape for name in BIG}

    cx, cy, cc = lax.axis_index("x"), lax.axis_index("y"), lax.axis_index("c")
    chip = 2 * cx + cy
    flat_b = _flat_shards(w_in, BF16, B_SHARDS, FLAT_ROWS - A_ROWS)
    a16 = w_in["a_w_in"].astype(BF16)
    got_a = lax.dynamic_update_slice(_gather_weights(a16), a16[None], (chip, 0, 0))
    weights_a = dict(a_in=got_a.transpose(1, 0, 2).reshape(D_MODEL, IN_A))

    core_arr, chip_arr = cc.astype(jnp.int32).reshape(1), chip.astype(jnp.int32).reshape(1)

    def reduce_pair(full, shards, rows, tag):
        g_all = _gathered_from_full(full, shards, rows)
        half, cols = g_all.shape[1] // 2, g_all.shape[2]
        other_half = lax.dynamic_slice(g_all, (0, (1 - cc) * half, 0), (N_CHIPS, half, cols)).astype(BF16)
        theirs = _pair_swap(other_half, "pair_exchange_" + tag)
        return _pair_sum(g_all, theirs, core_arr, chip_arr, "pair_sum_" + tag)

    loss, grad_x, reduced, small = _local_step(x[0], loss_target[0], weights_a, kv_norm.reshape(1, -1), b_q_norm, ln_g, ln_b,
                                               late=(flat_b, chip), reduce=reduce_pair)

    g_big = {}
    for (own, received), shards, tag in zip(reduced, (EARLY_SHARDS, LATE_SHARDS), ("early", "late")):
        mine = _chip_sum(own, received, "chip_sum_" + tag)
        sibling = _pair_swap(mine, "pair_share_" + tag)
        g_flat = jnp.concatenate([jnp.where(cc == 0, mine, sibling), jnp.where(cc == 0, sibling, mine)], axis=0)
        if g_flat.shape == shard_shapes[shards[0][0]]:
            g_big[shards[0][0]] = g_flat
        else:
            g_big.update(_unflat_shards(g_flat, shard_shapes, shards))

    small_sum = _sum_small(_pack_small(small["ln_g"], small["ln_b"], small["q_norm"], small["kv_norm"], loss[:, :1]))
    loss_out = small_sum[6, 0]

    row = lambda a: a.reshape(1, -1)
    g_all = {**g_big, **_unpack_small(small_sum)}
    g_all["kv_norm"] = row(g_all["kv_norm"])
    state = {**{name: (w_in[name], m_in[name], v_in[name]) for name in BIG},
             "ln_g": (ln_g, m_ln_g, v_ln_g), "ln_b": (ln_b, m_ln_b, v_ln_b), "b_q_norm": (b_q_norm, m_b_q_norm, v_b_q_norm),
             "kv_norm": (row(kv_norm), row(m_kv_norm), row(v_kv_norm))}
    upd = {name: _adamw(state[name][0], g_all[name], state[name][1], state[name][2], rows=256, name="adamw_" + name)
           for name in W_NAMES}

    def shaped(name, a):
        if name == "kv_norm":
            return a.reshape(-1)
        return a[None] if name in ("a_w_in", "a_w_out", "b_w_in", "b_w_uq", "b_w_out") else a

    outputs = [[shaped(name, g_all[name]) for name in W_NAMES]]
    outputs += [[shaped(name, upd[name][k]) for name in W_NAMES] for k in range(3)]
    return (loss_out, grad_x[None], *outputs[0], *outputs[1], *outputs[2], *outputs[3])
```

```python
import functools
import math

import jax
import jax.numpy as jnp
from jax import lax
from jax.experimental import pallas as pl
from jax.experimental.pallas import tpu as pltpu

F32 = jnp.float32
BF16 = jnp.bfloat16
MESH = pl.DeviceIdType.MESH

D_MODEL = 1024
DEPTH = 2
H_A, DK_A, DV_A = 4, 256, 512
WIDTH_A = H_A * DV_A
CHUNK = 128
H_B, QK_NOPE, QK_ROPE, V_HEAD = 16, 128, 64, 128
QK_PAD = 256
Q_LORA, KV_LORA = 768, 512
KV_DOWN_PAD = 640
WIDTH_B = H_B * V_HEAD
IN_A = 2 * H_A * DK_A + 2 * WIDTH_A
IN_B = Q_LORA + WIDTH_B
H1_B = KV_DOWN_PAD + IN_B
ROPE_BASE = 10000.0
ALPHA = (2.0 * DEPTH) ** 0.25
ATT_SCALE = (QK_NOPE + QK_ROPE) ** -0.5
NEG_BIG = -1e30

ADAM_LR, ADAM_B1, ADAM_B2, ADAM_EPS, ADAM_WD, ADAM_STEP = 0.001, 0.9, 0.999, 1e-08, 0.01, 10

VMEM_LIMIT_BYTES = 56 * 1024 * 1024
LANES = 128
FLAT_COLS = 1024
SHARD_ROWS = (("a_w_in", 1536), ("a_w_out", 512), ("b_w_in", 704), ("b_w_uq", 576), ("b_w_out", 512),
              ("kv_w_down", 144), ("kv_w_up", 512))
FLAT_ROWS = 4608
N_CHIPS = 4
N_DEV = 8


def _params(sem, vmem=VMEM_LIMIT_BYTES):
    return pltpu.CompilerParams(dimension_semantics=sem, vmem_limit_bytes=vmem)


def _row_spec(ts, w, col_block=0):
    return pl.BlockSpec((ts, w), lambda i: (i, col_block))


def _bc_spec(shape):
    nd = len(shape)
    return pl.BlockSpec(shape, lambda i: (0,) * nd)


def _sigmoid(x):
    return 1.0 / (1.0 + jnp.exp(-x))


def _fold8(v):
    ts, w = v.shape
    return jnp.sum(v.reshape(ts // 8, 8, w), axis=0)


def _mm(a, b, *, ta=False, tb=False, out_dtype=F32, tm=1024, tn=512, tk=None, name, extras=(), epilogue=None, out_tn=None,
        side=None):
    if ta:
        K, M = a.shape
    else:
        M, K = a.shape
    if tb:
        N, Kb = b.shape
    else:
        Kb, N = b.shape
    assert K == Kb, (a.shape, b.shape)
    tm, tn = min(tm, M), min(tn, N)
    tk = K if tk is None else min(tk, K)
    assert M % tm == 0 and N % tn == 0 and K % tk == 0, (name, M, N, K, tm, tn, tk)
    grid = (M // tm, N // tn, K // tk)
    nk = grid[2]
    out_tn = tn if out_tn is None else out_tn
    n_extra = len(extras)
    side_inputs = [] if side is None else list(side["inputs"])
    n_side = len(side_inputs)
    n_acc = 0 if nk == 1 else 1
    dims = (((0,) if ta else (1,), (1,) if tb else (0,)), ((), ()))

    def body(a_ref, b_ref, *rest):
        extra_refs, rest = rest[:n_extra], rest[n_extra:]
        side_in, o_ref, rest = rest[:n_side], rest[n_side], rest[n_side + 1:]
        if side is not None:
            side_refs, rest = side_in + rest[:1] + rest[1 + n_acc:], rest[1:]
            ids = [pl.program_id(d) for d in range(3)]

            @pl.when((ids[0] == 0) & (ids[1] == 0) & (ids[2] == 0))
            def _():
                side["start"](*side_refs)

        prod = lax.dot_general(a_ref[...].astype(BF16), b_ref[...].astype(BF16), dims,
                               preferred_element_type=F32)

        def store(tile):
            if epilogue is None:
                o_ref[...] = tile.astype(o_ref.dtype)
            else:
                epilogue(tile, o_ref, *extra_refs)

        if nk == 1:
            store(prod)
        else:
            acc = rest[0]
            k = pl.program_id(2)

            @pl.when(k == 0)
            def _():
                acc[...] = prod

            @pl.when(k > 0)
            def _():
                acc[...] += prod

            @pl.when(k == nk - 1)
            def _():
                store(acc[...])

        if side is not None:
            @pl.when((ids[0] == grid[0] - 1) & (ids[1] == grid[1] - 1) & (ids[2] == grid[2] - 1))
            def _():
                side["finish"](*side_refs)

    a_spec = pl.BlockSpec((tk, tm), lambda i, j, k: (k, i)) if ta else pl.BlockSpec((tm, tk), lambda i, j, k: (i, k))
    b_spec = pl.BlockSpec((tn, tk), lambda i, j, k: (j, k)) if tb else pl.BlockSpec((tk, tn), lambda i, j, k: (k, j))
    extra_specs = [pl.BlockSpec((tm, e.shape[1]), lambda i, j, k: (i, 0)) for e in extras]
    out_specs = pl.BlockSpec((tm, out_tn), lambda i, j, k: (i, j))
    out_shape = jax.ShapeDtypeStruct((M, (N // tn) * out_tn), out_dtype)
    scratch = [] if nk == 1 else [pltpu.VMEM((tm, tn), F32)]
    if side is not None:
        out_specs, out_shape, scratch = [out_specs, HBM_SPEC], [out_shape, side["out_shape"]], scratch + list(side["scratch"])
    return pl.pallas_call(
        body, name=name, grid=grid,
        in_specs=[a_spec, b_spec] + extra_specs + [HBM_SPEC] * n_side,
        out_specs=out_specs, out_shape=out_shape, scratch_shapes=scratch,
        compiler_params=_params(("parallel", "parallel", "arbitrary") if side is None else ("arbitrary",) * 3),
    )(a, b, *extras, *side_inputs)


def _rope_tables_a(s):
    half = DK_A // 2
    inv = ROPE_BASE ** (-jnp.arange(half, dtype=F32) / half)
    ang = jnp.arange(s, dtype=F32)[:, None] * inv[None, :]
    return jnp.cos(ang), jnp.sin(ang)


def _rope_tables_b(s):
    half = QK_ROPE // 2
    inv = ROPE_BASE ** (-jnp.arange(half, dtype=F32) / half)
    ang = jnp.arange(s, dtype=F32)[:, None] * inv[None, :]
    c, sn = jnp.cos(ang), jnp.sin(ang)
    z = jnp.zeros_like(c)
    cos = jnp.concatenate([c, c, z, z], axis=1)
    sa = jnp.concatenate([-sn, z, z, z], axis=1)
    sb = jnp.concatenate([z, sn, z, z], axis=1)
    return cos, sa, sb


def _rope_b(r, cos, sa, sb, sign):
    return r * cos + sign * (pltpu.roll(r, 96, 1) * sa + pltpu.roll(r, 32, 1) * sb)


def _retention_tables():
    lg = jnp.log1p(-jnp.exp2(-5.0 - jnp.arange(H_A, dtype=F32)))
    idx = jnp.arange(CHUNK, dtype=F32)
    diff = idx[:, None] - idx[None, :]
    causal = diff >= 0
    dmat = jnp.where(causal, jnp.exp(jnp.where(causal, diff, 0.0)[None] * lg[:, None, None]), 0.0)
    qdec = jnp.exp((idx + 1.0)[None, :] * lg[:, None])[:, :, None]
    kdec = jnp.exp((CHUNK - 1.0 - idx)[None, :] * lg[:, None])[:, :, None]
    cdec = jnp.broadcast_to(jnp.exp(CHUNK * lg)[:, None, None], (H_A, 1, DV_A))
    return dmat, qdec, kdec, cdec


def _group_norm(o):
    mu = jnp.mean(o, axis=-1, keepdims=True)
    oc = o - mu
    var = jnp.mean(oc * oc, axis=-1, keepdims=True)
    rstd = lax.rsqrt(var + 1e-5)
    return oc * rstd, rstd


Q_COL, K_COL, V_COL, GATE_COL = 0, H_A * DK_A, 2 * H_A * DK_A, 2 * H_A * DK_A + WIDTH_A


def _ln_stats(z):
    mu = jnp.mean(z, axis=-1, keepdims=True)
    zc = z - mu
    var = jnp.mean(zc * zc, axis=-1, keepdims=True)
    rstd = lax.rsqrt(var + 1e-5)
    return zc * rstd, rstd


def _ln_bwd(dy, xhat, rstd, g):
    dxh = dy * g
    m1 = jnp.mean(dxh, axis=-1, keepdims=True)
    m2 = jnp.mean(dxh * xhat, axis=-1, keepdims=True)
    return rstd * (dxh - m1 - xhat * m2)


def _ln_fwd(x, y, g, b, *, ts):
    s = x.shape[0]

    def body(x_ref, y_ref, g_ref, b_ref, o_ref, o16_ref):
        xhat, _ = _ln_stats(ALPHA * x_ref[...] + y_ref[...])
        out = xhat * g_ref[...] + b_ref[...]
        o_ref[...] = out
        o16_ref[...] = out.astype(o16_ref.dtype)

    return pl.pallas_call(
        body, name="ln_fwd", grid=(s // ts,),
        in_specs=[_row_spec(ts, D_MODEL), _row_spec(ts, D_MODEL), _bc_spec((1, D_MODEL)), _bc_spec((1, D_MODEL))],
        out_specs=[_row_spec(ts, D_MODEL), _row_spec(ts, D_MODEL)],
        out_shape=[jax.ShapeDtypeStruct((s, D_MODEL), F32), jax.ShapeDtypeStruct((s, D_MODEL), BF16)],
        compiler_params=_params(("parallel",)),
    )(x, y, g, b)


def _ln_loss_bwd(x1, y, target, g, b, *, ts):
    s = x1.shape[0]
    n = s // ts

    def body(x_ref, y_ref, t_ref, g_ref, b_ref, dz_ref, dz16_ref, dg_ref, db_ref, loss_ref, ag, ab, al):
        i = pl.program_id(0)

        @pl.when(i == 0)
        def _():
            ag[...] = jnp.zeros_like(ag)
            ab[...] = jnp.zeros_like(ab)
            al[...] = jnp.zeros_like(al)

        xhat, rstd = _ln_stats(ALPHA * x_ref[...] + y_ref[...])
        err = xhat * g_ref[...] + b_ref[...] - t_ref[...]
        al[...] += _fold8(err * err)
        dy = err * (1.0 / D_MODEL)
        ag[...] += _fold8(dy * xhat)
        ab[...] += _fold8(dy)
        dz = _ln_bwd(dy, xhat, rstd, g_ref[...])
        dz_ref[...] = dz
        dz16_ref[...] = dz.astype(dz16_ref.dtype)

        @pl.when(i == n - 1)
        def _():
            dg_ref[...] = jnp.sum(ag[...], axis=0, keepdims=True)
            db_ref[...] = jnp.sum(ab[...], axis=0, keepdims=True)
            loss_ref[...] = jnp.full((1, LANES), (0.5 / D_MODEL) * jnp.sum(al[...]), F32)

    return pl.pallas_call(
        body, name="ln_loss_bwd", grid=(n,),
        in_specs=[_row_spec(ts, D_MODEL)] * 3 + [_bc_spec((1, D_MODEL))] * 2,
        out_specs=[_row_spec(ts, D_MODEL), _row_spec(ts, D_MODEL), _bc_spec((1, D_MODEL)), _bc_spec((1, D_MODEL)),
                   _bc_spec((1, LANES))],
        out_shape=[jax.ShapeDtypeStruct((s, D_MODEL), F32), jax.ShapeDtypeStruct((s, D_MODEL), BF16),
                   jax.ShapeDtypeStruct((1, D_MODEL), F32), jax.ShapeDtypeStruct((1, D_MODEL), F32),
                   jax.ShapeDtypeStruct((1, LANES), F32)],
        scratch_shapes=[pltpu.VMEM((8, D_MODEL), F32)] * 3,
        compiler_params=_params(("arbitrary",)),
    )(x1, y, target, g, b)


def _ln_bwd_call(dy, x, y, g, *, ts):
    s = x.shape[0]
    n = s // ts

    def body(dy_ref, x_ref, y_ref, g_ref, dz_ref, dz16_ref, dg_ref, db_ref, ag, ab):
        i = pl.program_id(0)

        @pl.when(i == 0)
        def _():
            ag[...] = jnp.zeros_like(ag)
            ab[...] = jnp.zeros_like(ab)

        xhat, rstd = _ln_stats(ALPHA * x_ref[...] + y_ref[...])
        dy = dy_ref[...]
        ag[...] += _fold8(dy * xhat)
        ab[...] += _fold8(dy)
        dz = _ln_bwd(dy, xhat, rstd, g_ref[...])
        dz_ref[...] = dz
        dz16_ref[...] = dz.astype(dz16_ref.dtype)

        @pl.when(i == n - 1)
        def _():
            dg_ref[...] = jnp.sum(ag[...], axis=0, keepdims=True)
            db_ref[...] = jnp.sum(ab[...], axis=0, keepdims=True)

    return pl.pallas_call(
        body, name="ln_bwd", grid=(n,),
        in_specs=[_row_spec(ts, D_MODEL)] * 3 + [_bc_spec((1, D_MODEL))],
        out_specs=[_row_spec(ts, D_MODEL), _row_spec(ts, D_MODEL), _bc_spec((1, D_MODEL)), _bc_spec((1, D_MODEL))],
        out_shape=[jax.ShapeDtypeStruct((s, D_MODEL), F32), jax.ShapeDtypeStruct((s, D_MODEL), BF16),
                   jax.ShapeDtypeStruct((1, D_MODEL), F32), jax.ShapeDtypeStruct((1, D_MODEL), F32)],
        scratch_shapes=[pltpu.VMEM((8, D_MODEL), F32)] * 2,
        compiler_params=_params(("arbitrary",)),
    )(dy, x, y, g)


def _residual_store(tile, o_ref, dz_ref):
    o_ref[...] = ALPHA * dz_ref[...] + tile


C_LAT = slice(0, KV_LORA)
C_ROPE = slice(KV_LORA, KV_DOWN_PAD)
C_QL = slice(KV_DOWN_PAD, KV_DOWN_PAD + Q_LORA)
C_GATE = slice(KV_DOWN_PAD + Q_LORA, H1_B)


def _rms(x, eps=1e-6):
    r = lax.rsqrt(jnp.mean(x * x, axis=-1, keepdims=True) + eps)
    return x * r, r


def _rms_bwd(dy, xhat, r, g):
    dxh = dy * g
    return r * (dxh - xhat * jnp.mean(dxh * xhat, axis=-1, keepdims=True))


def _kvq_prep(h1, kv_norm, q_norm, cos, sa, sb, *, ts):
    s = h1.shape[0]

    def body(h_ref, kn_ref, qn_ref, c_ref, sa_ref, sb_ref, lat_ref, kr_ref, ql_ref):
        lat, _ = _rms(h_ref[:, C_LAT])
        lat_ref[...] = (lat * kn_ref[...]).astype(lat_ref.dtype)
        kr_ref[...] = _rope_b(h_ref[:, C_ROPE], c_ref[...], sa_ref[...], sb_ref[...], 1.0).astype(kr_ref.dtype)
        ql, _ = _rms(h_ref[:, C_QL])
        ql_ref[...] = (ql * qn_ref[...]).astype(ql_ref.dtype)

    return pl.pallas_call(
        body, name="kvq_prep", grid=(s // ts,),
        in_specs=[_row_spec(ts, H1_B), _bc_spec((1, KV_LORA)), _bc_spec((1, Q_LORA))] + [_row_spec(ts, 128)] * 3,
        out_specs=[_row_spec(ts, KV_LORA), _row_spec(ts, 128), _row_spec(ts, Q_LORA)],
        out_shape=[jax.ShapeDtypeStruct((s, KV_LORA), BF16), jax.ShapeDtypeStruct((s, 128), BF16),
                   jax.ShapeDtypeStruct((s, Q_LORA), BF16)],
        compiler_params=_params(("parallel",)),
    )(h1, kv_norm, q_norm, cos, sa, sb)


LOG2E = 1.4426950408889634
LN2 = 0.6931471805599453
Q_SCALE = ATT_SCALE * LOG2E


def _rope_q_store(tile, o_ref, c_ref, sa_ref, sb_ref):
    c, a, b = c_ref[...], sa_ref[...], sb_ref[...]
    for hd in range(tile.shape[1] // QK_PAD):
        lo = hd * QK_PAD
        o_ref[:, lo:lo + 128] = (tile[:, lo:lo + 128] * Q_SCALE).astype(o_ref.dtype)
        o_ref[:, lo + 128:lo + 256] = (_rope_b(tile[:, lo + 128:lo + 256], c, a, b, 1.0) * Q_SCALE).astype(o_ref.dtype)


def _assemble_k_store(tile, o_ref, kr_ref):
    r = kr_ref[...]
    for hd in range(tile.shape[1] // QK_NOPE):
        o_ref[:, hd * QK_PAD:hd * QK_PAD + 128] = tile[:, hd * 128:(hd + 1) * 128].astype(o_ref.dtype)
        o_ref[:, hd * QK_PAD + 128:(hd + 1) * QK_PAD] = r


def _h1_bwd(h1, dlat_k, dlat_v, dkr, dqn, dg16, kv_norm, q_norm, cos, sa, sb, *, ts):
    s = h1.shape[0]
    n = s // ts

    def body(h_ref, dk_ref, dv_ref, dkr_ref, dqn_ref, dg_ref, kn_ref, qn_ref, c_ref, sa_ref, sb_ref,
             o_ref, dkn_ref, dqn_out_ref, akn, aqn):
        i = pl.program_id(0)

        @pl.when(i == 0)
        def _():
            akn[...] = jnp.zeros_like(akn)
            aqn[...] = jnp.zeros_like(aqn)

        lat, r = _rms(h_ref[:, C_LAT])
        dlat = dk_ref[...] + dv_ref[...]
        akn[...] += _fold8(dlat * lat)
        o_ref[:, C_LAT] = _rms_bwd(dlat, lat, r, kn_ref[...]).astype(o_ref.dtype)

        o_ref[:, C_ROPE] = _rope_b(dkr_ref[...], c_ref[...], sa_ref[...], sb_ref[...], -1.0).astype(o_ref.dtype)

        ql, rq = _rms(h_ref[:, C_QL])
        dq = dqn_ref[...]
        aqn[...] += _fold8(dq * ql)
        o_ref[:, C_QL] = _rms_bwd(dq, ql, rq, qn_ref[...]).astype(o_ref.dtype)
        o_ref[:, C_GATE] = dg_ref[...]

        @pl.when(i == n - 1)
        def _():
            dkn_ref[...] = jnp.sum(akn[...], axis=0, keepdims=True)
            dqn_out_ref[...] = jnp.sum(aqn[...], axis=0, keepdims=True)

    return pl.pallas_call(
        body, name="h1_bwd", grid=(n,),
        in_specs=[_row_spec(ts, H1_B), _row_spec(ts, KV_LORA), _row_spec(ts, KV_LORA), _row_spec(ts, 128),
                  _row_spec(ts, Q_LORA), _row_spec(ts, WIDTH_B), _bc_spec((1, KV_LORA)), _bc_spec((1, Q_LORA))]
        + [_row_spec(ts, 128)] * 3,
        out_specs=[_row_spec(ts, H1_B), _bc_spec((1, KV_LORA)), _bc_spec((1, Q_LORA))],
        out_shape=[jax.ShapeDtypeStruct((s, H1_B), BF16), jax.ShapeDtypeStruct((1, KV_LORA), F32),
                   jax.ShapeDtypeStruct((1, Q_LORA), F32)],
        scratch_shapes=[pltpu.VMEM((8, KV_LORA), F32), pltpu.VMEM((8, Q_LORA), F32)],
        compiler_params=_params(("arbitrary",)),
    )(h1, dlat_k, dlat_v, dkr, dqn, dg16, kv_norm, q_norm, cos, sa, sb)


def _dot(a, b, ca, cb):
    return lax.dot_general(a, b, (((ca,), (cb,)), ((), ())), preferred_element_type=F32)


def _table_specs():
    full = lambda shape: pl.BlockSpec(shape, lambda i: (0,) * len(shape))
    return [full((H_A, CHUNK, CHUNK)), full((H_A, CHUNK, 1)), full((H_A, CHUNK, 1)), full((H_A, 1, DV_A))]


def _retention_fwd(h_a, cos, sin, tables):
    s = h_a.shape[0]
    n = s // CHUNK

    def body(h_ref, c_ref, s_ref, dm_ref, qd_ref, kd_ref, cd_ref, q_ref, k_ref, v_ref, o_ref, u_ref, st_ref, state):
        @pl.when(pl.program_id(0) == 0)
        def _():
            state[...] = jnp.zeros_like(state)

        c, sn = c_ref[...], s_ref[...]
        for hd in range(H_A):
            qs, vs = slice(hd * DK_A, (hd + 1) * DK_A), slice(hd * DV_A, (hd + 1) * DV_A)
            for r_ref, base, scale in ((q_ref, Q_COL, 1.0), (k_ref, K_COL, DK_A ** -0.5)):
                lo = base + hd * DK_A
                x1, x2 = h_ref[:, lo:lo + 128], h_ref[:, lo + 128:lo + 256]
                r_ref[:, hd * DK_A:hd * DK_A + 128] = ((x1 * c - x2 * sn) * scale).astype(r_ref.dtype)
                r_ref[:, hd * DK_A + 128:(hd + 1) * DK_A] = ((x2 * c + x1 * sn) * scale).astype(r_ref.dtype)
            v_ref[:, vs] = h_ref[:, V_COL + hd * DV_A:V_COL + (hd + 1) * DV_A].astype(v_ref.dtype)
            qv, kv, vv = q_ref[:, qs], k_ref[:, qs], v_ref[:, vs]
            st = state[hd]
            st16 = st.astype(BF16)
            st_ref[0, hd] = st16
            scores = _dot(qv, kv, 1, 1) * dm_ref[hd]
            qd = (qv.astype(F32) * qd_ref[hd]).astype(BF16)
            o = _dot(scores.astype(BF16), vv, 1, 0) + _dot(qd, st16, 1, 0)
            o_ref[:, vs] = o
            kd = (kv.astype(F32) * kd_ref[hd]).astype(BF16)
            state[hd] = st * cd_ref[hd] + _dot(kd, vv, 0, 0)
            on, _ = _group_norm(o)
            g = h_ref[:, GATE_COL + hd * DV_A:GATE_COL + (hd + 1) * DV_A]
            u_ref[:, vs] = (on * (g * _sigmoid(g))).astype(u_ref.dtype)

    row = lambda w: pl.BlockSpec((CHUNK, w), lambda i: (i, 0))
    return pl.pallas_call(
        body, name="retention_fwd", grid=(n,),
        in_specs=[row(IN_A), row(128), row(128)] + _table_specs(),
        out_specs=[row(H_A * DK_A), row(H_A * DK_A), row(WIDTH_A), row(WIDTH_A), row(WIDTH_A),
                   pl.BlockSpec((1, H_A, DK_A, DV_A), lambda i: (i, 0, 0, 0))],
        out_shape=[jax.ShapeDtypeStruct((s, H_A * DK_A), BF16), jax.ShapeDtypeStruct((s, H_A * DK_A), BF16),
                   jax.ShapeDtypeStruct((s, WIDTH_A), BF16), jax.ShapeDtypeStruct((s, WIDTH_A), F32),
                   jax.ShapeDtypeStruct((s, WIDTH_A), BF16), jax.ShapeDtypeStruct((n, H_A, DK_A, DV_A), BF16)],
        scratch_shapes=[pltpu.VMEM((H_A, DK_A, DV_A), F32)],
        compiler_params=_params(("arbitrary",)),
    )(h_a, cos, sin, *tables)


def _retention_bwd(q, k, v, states, du, o, h_a, cos, sin, tables):
    s = q.shape[0]
    n = s // CHUNK

    def body(q_ref, k_ref, v_ref, st_ref, du_ref, o_ref, g_ref, c_ref, s_ref, dm_ref, qd_ref, kd_ref, cd_ref, dh_ref, grad_state):
        @pl.when(pl.program_id(0) == 0)
        def _():
            grad_state[...] = jnp.zeros_like(grad_state)

        c, sn = c_ref[...], s_ref[...]
        for hd in range(H_A):
            qs, vs = slice(hd * DK_A, (hd + 1) * DK_A), slice(hd * DV_A, (hd + 1) * DV_A)
            on, rstd = _group_norm(o_ref[:, vs])
            g = g_ref[:, vs]
            sg = _sigmoid(g)
            du_v = du_ref[:, vs]
            don = du_v * (g * sg)
            dh_ref[:, GATE_COL + hd * DV_A:GATE_COL + (hd + 1) * DV_A] = (du_v * on * (sg * (1.0 + g * (1.0 - sg)))).astype(dh_ref.dtype)
            m1 = jnp.mean(don, axis=-1, keepdims=True)
            m2 = jnp.mean(don * on, axis=-1, keepdims=True)
            dov = (rstd * (don - m1 - on * m2)).astype(BF16)

            qv, kv, vv = q_ref[:, qs], k_ref[:, qs], v_ref[:, vs]
            dm = dm_ref[hd]
            gs = grad_state[hd]
            g16 = gs.astype(BF16)
            scores = (_dot(qv, kv, 1, 1) * dm).astype(BF16)
            dscores = (_dot(dov, vv, 1, 1) * dm).astype(BF16)
            qd = (qv.astype(F32) * qd_ref[hd]).astype(BF16)
            kd = (kv.astype(F32) * kd_ref[hd]).astype(BF16)
            dq = _dot(dscores, kv, 1, 0) + _dot(dov, st_ref[0, hd], 1, 1) * qd_ref[hd]
            dk = (_dot(dscores, qv, 0, 0) + _dot(vv, g16, 1, 1) * kd_ref[hd]) * (DK_A ** -0.5)
            dh_ref[:, V_COL + hd * DV_A:V_COL + (hd + 1) * DV_A] = (_dot(scores, dov, 0, 0) + _dot(kd, g16, 1, 0)).astype(dh_ref.dtype)
            grad_state[hd] = gs * cd_ref[hd] + _dot(qd, dov, 0, 0)
            for d, base in ((dq, Q_COL), (dk, K_COL)):
                lo = base + hd * DK_A
                d1, d2 = d[:, 0:128], d[:, 128:256]
                dh_ref[:, lo:lo + 128] = (d1 * c + d2 * sn).astype(dh_ref.dtype)
                dh_ref[:, lo + 128:lo + 256] = (d2 * c - d1 * sn).astype(dh_ref.dtype)

    rev = lambda i: n - 1 - i
    row = lambda w, col=0: pl.BlockSpec((CHUNK, w), lambda i: (rev(i), col))
    return pl.pallas_call(
        body, name="retention_bwd", grid=(n,),
        in_specs=[row(H_A * DK_A), row(H_A * DK_A), row(WIDTH_A), pl.BlockSpec((1, H_A, DK_A, DV_A), lambda i: (rev(i), 0, 0, 0)),
                  row(WIDTH_A), row(WIDTH_A), row(WIDTH_A, GATE_COL // WIDTH_A), row(128), row(128)] + _table_specs(),
        out_specs=row(IN_A),
        out_shape=jax.ShapeDtypeStruct((s, IN_A), BF16),
        scratch_shapes=[pltpu.VMEM((H_A, DK_A, DV_A), F32)],
        compiler_params=_params(("arbitrary",)),
    )(q, k, v, states, du, o, h_a, cos, sin, *tables)


GATE_BLOCK0 = (KV_DOWN_PAD + Q_LORA) // LANES


def _causal_mask(sc, row0):
    row = lax.broadcasted_iota(jnp.int32, sc.shape, 0) + row0
    col = lax.broadcasted_iota(jnp.int32, sc.shape, 1)
    return jnp.where(col <= row, sc, NEG_BIG)


def _key_block_loop(step, n, per_trip, smallest=1):
    def trip_body(jj, carry):
        for t in range(per_trip):
            step(per_trip * jj + t)
        return carry

    lax.fori_loop(0, n // per_trip, trip_body, 0)
    group = per_trip // 2
    while group >= smallest:
        def tail(group=group):
            first = (n // (2 * group)) * (2 * group)
            for t in range(group):
                step(first + t)

        pl.when((n // group) % 2 == 1)(tail)
        group //= 2


def _attention_fwd(q, k, v, h1, *, blk, bk, sub, per_trip):
    s = q.shape[0]
    nb = s // blk

    def body(q_ref, k_ref, v_ref, g_ref, o_ref, u_ref, lse_ref, vext_s, m_s, acc_s):
        i = pl.program_id(1)

        @pl.when(i == 0)
        def _():
            vext_s[:, 0:V_HEAD] = v_ref[...]
            vext_s[:, V_HEAD:2 * V_HEAD] = jnp.ones((s, V_HEAD), vext_s.dtype)

        m_s[...] = jnp.full_like(m_s, NEG_BIG)
        acc_s[...] = jnp.zeros_like(acc_s)

        def update(rows, kb, vb, row0):
            sc = _dot(q_ref[rows, :], kb, 1, 1)
            if row0 is not None:
                sc = _causal_mask(sc, row0)
            m_prev = m_s[rows, :]
            m_new = jnp.maximum(m_prev, jnp.max(sc, axis=-1, keepdims=True))
            p = jnp.exp2(sc - jnp.tile(m_new, (1, kb.shape[0] // LANES)))
            a = jnp.exp2(m_prev - m_new)
            acc_s[rows, :] = jnp.tile(a, (1, 2)) * acc_s[rows, :] + _dot(p.astype(BF16), vb, 1, 0)
            m_s[rows, :] = m_new

        def step(j):
            kv_rows = pl.ds(pl.multiple_of(j * bk, bk), bk)
            kb, vb = k_ref[kv_rows, :], vext_s[kv_rows, :]
            for r in range(blk // sub):
                update(slice(r * sub, (r + 1) * sub), kb, vb, None)

        _key_block_loop(step, i * (blk // bk), per_trip, smallest=blk // bk)
        for r in range(blk // sub):
            ncols = (r + 1) * sub
            kv_rows = pl.ds(pl.multiple_of(i * blk, blk), ncols)
            update(slice(r * sub, (r + 1) * sub), k_ref[kv_rows, :], vext_s[kv_rows, :], r * sub)
        acc = acc_s[...]
        l = acc[:, V_HEAD:2 * V_HEAD]
        o = acc[:, 0:V_HEAD] / l
        g = g_ref[...]
        o_ref[...] = o
        u_ref[...] = (o * (g * _sigmoid(g))).astype(u_ref.dtype)
        lse_ref[0] = (m_s[...] + jnp.log2(l))[:, 0:1]

    blk_idx = lambda h, i: (i, h)
    return pl.pallas_call(
        body, name="attention_fwd", grid=(H_B, nb),
        in_specs=[pl.BlockSpec((blk, QK_PAD), blk_idx), pl.BlockSpec((s, QK_PAD), lambda h, i: (0, h)),
                  pl.BlockSpec((s, V_HEAD), lambda h, i: (0, h)), pl.BlockSpec((blk, LANES), lambda h, i: (i, GATE_BLOCK0 + h))],
        out_specs=[pl.BlockSpec((blk, V_HEAD), blk_idx), pl.BlockSpec((blk, V_HEAD), blk_idx),
                   pl.BlockSpec((1, blk, 1), lambda h, i: (h, i, 0))],
        out_shape=[jax.ShapeDtypeStruct((s, WIDTH_B), F32), jax.ShapeDtypeStruct((s, WIDTH_B), BF16),
                   jax.ShapeDtypeStruct((H_B, s, 1), F32)],
        scratch_shapes=[pltpu.VMEM((s, 2 * V_HEAD), BF16), pltpu.VMEM((blk, LANES), F32), pltpu.VMEM((blk, 2 * V_HEAD), F32)],
        compiler_params=_params(("parallel", "arbitrary")),
    )(q, k, v, h1)


def _attention_bwd(q, k, v, du, o, h1, lse, cos, sa, sb, *, blk, bk, per_trip):
    s = q.shape[0]
    nb = s // blk

    def body(q_ref, k_ref, v_ref, du_ref, o_ref, g_ref, lse_ref, c_ref, sa_ref, sb_ref,
             dq_ref, dkn_ref, dkr_ref, dv_ref, dg_ref, lse_s, dl_s, do_s, dq_s):
        head, i = pl.program_id(0), pl.program_id(1)
        g = g_ref[...]
        sg = _sigmoid(g)
        du_v, ov = du_ref[...], o_ref[...]
        do = du_v * (g * sg)
        dg_ref[...] = (du_v * ov * (sg * (1.0 + g * (1.0 - sg)))).astype(dg_ref.dtype)
        do_s[...] = do.astype(do_s.dtype)
        dl_s[...] = jnp.broadcast_to(jnp.sum(do * ov, axis=-1, keepdims=True), (blk, LANES))
        lse_s[...] = jnp.broadcast_to(lse_ref[0], (blk, LANES))
        dq_s[...] = jnp.zeros_like(dq_s)

        def products(rows, kv_rows, row0):
            qv, dov, kb = q_ref[rows, :], do_s[rows, :], k_ref[kv_rows, :]
            tile = (1, kb.shape[0] // LANES)
            sc = _dot(qv, kb, 1, 1)
            if row0 is not None:
                sc = _causal_mask(sc, row0)
            p = jnp.exp2(sc - jnp.tile(lse_s[rows, :], tile))
            dp = _dot(dov, v_ref[kv_rows, :], 1, 1)
            ds = (p * (dp - jnp.tile(dl_s[rows, :], tile))).astype(BF16)
            dq_s[rows, :] += _dot(ds, kb, 1, 0)
            return _dot(ds, qv, 0, 0), _dot(p.astype(BF16), dov, 0, 0)

        def put(kv_rows, dk_c, dv_c, first):
            if first:
                dkn_ref[kv_rows, :] = dk_c[:, 0:128]
                dv_ref[kv_rows, :] = dv_c

                @pl.when(head == 0)
                def _():
                    dkr_ref[kv_rows, :] = dk_c[:, 128:256]

                @pl.when(head > 0)
                def _():
                    dkr_ref[kv_rows, :] += dk_c[:, 128:256]
            else:
                dkn_ref[kv_rows, :] += dk_c[:, 0:128]
                dkr_ref[kv_rows, :] += dk_c[:, 128:256]
                dv_ref[kv_rows, :] += dv_c

        n_sub = blk // bk
        sub_rows = [slice(r * bk, (r + 1) * bk) for r in range(n_sub)]

        def step(j):
            kv_rows = pl.ds(pl.multiple_of(j * bk, bk), bk)
            for rows in sub_rows:
                dk_c, dv_c = products(rows, kv_rows, None)
                put(kv_rows, dk_c, dv_c, False)

        _key_block_loop(step, i * n_sub, per_trip, smallest=n_sub)
        for c in range(n_sub):
            kv_rows = pl.ds(pl.multiple_of(i * blk + c * bk, bk), bk)
            for r in range(c, n_sub):
                dk_c, dv_c = products(sub_rows[r], kv_rows, 0 if r == c else None)
                put(kv_rows, dk_c, dv_c, r == c)
        dq = dq_s[...] * ATT_SCALE
        dq_ref[:, 0:128] = dq[:, 0:128].astype(dq_ref.dtype)
        dq_ref[:, 128:256] = _rope_b(dq[:, 128:256], c_ref[...], sa_ref[...], sb_ref[...], -1.0).astype(dq_ref.dtype)

        @pl.when(i == nb - 1)
        def _():
            dkn_ref[...] = dkn_ref[...] * LN2

        @pl.when((i == nb - 1) & (head == H_B - 1))
        def _():
            dkr_ref[...] = dkr_ref[...] * LN2

    per_head = lambda h, i: (0, h)
    blk_idx = lambda h, i: (i, h)
    row_idx = lambda h, i: (i, 0)
    return pl.pallas_call(
        body, name="attention_bwd", grid=(H_B, nb),
        in_specs=[pl.BlockSpec((blk, QK_PAD), blk_idx), pl.BlockSpec((s, QK_PAD), per_head), pl.BlockSpec((s, V_HEAD), per_head),
                  pl.BlockSpec((blk, V_HEAD), blk_idx), pl.BlockSpec((blk, V_HEAD), blk_idx),
                  pl.BlockSpec((blk, LANES), lambda h, i: (i, GATE_BLOCK0 + h)), pl.BlockSpec((1, blk, 1), lambda h, i: (h, i, 0)),
                  pl.BlockSpec((blk, LANES), row_idx), pl.BlockSpec((blk, LANES), row_idx), pl.BlockSpec((blk, LANES), row_idx)],
        out_specs=[pl.BlockSpec((blk, QK_PAD), blk_idx), pl.BlockSpec((s, 128), per_head),
                   pl.BlockSpec((s, 128), lambda h, i: (0, 0)),
                   pl.BlockSpec((s, 128), per_head), pl.BlockSpec((blk, V_HEAD), blk_idx)],
        out_shape=[jax.ShapeDtypeStruct((s, H_B * QK_PAD), BF16), jax.ShapeDtypeStruct((s, H_B * 128), F32),
                   jax.ShapeDtypeStruct((s, 128), F32), jax.ShapeDtypeStruct((s, H_B * 128), F32),
                   jax.ShapeDtypeStruct((s, WIDTH_B), BF16)],
        scratch_shapes=[pltpu.VMEM((blk, LANES), F32), pltpu.VMEM((blk, LANES), F32), pltpu.VMEM((blk, V_HEAD), BF16),
                        pltpu.VMEM((blk, QK_PAD), F32)],
        compiler_params=_params(("arbitrary", "arbitrary")),
    )(q, k, v, du, o, h1, lse, cos, sa, sb)


def _local_step(x, target, w, kv_norm, q_norm, ln_g, ln_b, *, ts=256, blk=512, late=None, reduce=None):
    s = x.shape[0]
    cos_a, sin_a = _rope_tables_a(s)
    cos_b, sa_b, sb_b = _rope_tables_b(s)
    tables = _retention_tables()
    g0, g1, b0, b1 = ln_g[0:1], ln_g[1:2], ln_b[0:1], ln_b[1:2]

    x16 = x.astype(BF16)
    if late is None:
        h_a = _mm(x16, w["a_in"], tn=1536, name="a_in_fwd")
    else:
        flat_b, chip = late
        h_a, got = _mm(x16, w["a_in"], tn=1536, name="a_in_fwd", side=_gather_side(flat_b))
        got = lax.dynamic_update_slice(got, flat_b[None], (chip, 0, 0))
        w = {**w, **_kernel_layout_b(_full_from_gathered(got, B_SHARDS))}
    q_a, k_a, v_a, o_a, u_a, states = _retention_fwd(h_a, cos_a, sin_a, tables)
    y_a = _mm(u_a, w["a_out"], tn=1024, name="a_out_fwd")
    x1, x1_16 = _ln_fwd(x, y_a, g0, b0, ts=ts)

    h1 = _mm(x1_16, w["b_in1"], tn=1152, name="b_in_fwd")
    lat16, kr16, qn16 = _kvq_prep(h1, kv_norm, q_norm, cos_b, sa_b, sb_b, ts=ts)
    k16 = _mm(lat16, w["up_k"], out_dtype=BF16, tn=2048, out_tn=H_B * QK_PAD, extras=(kr16,), epilogue=_assemble_k_store, name="up_k_fwd")
    v16 = _mm(lat16, w["up_v"], out_dtype=BF16, tn=2048, name="up_v_fwd")
    q16 = _mm(qn16, w["uq"], out_dtype=BF16, tn=2048, extras=(cos_b, sa_b, sb_b), epilogue=_rope_q_store, name="uq_fwd")
    o_b, u_b, lse = _attention_fwd(q16, k16, v16, h1, blk=2 * blk, bk=blk, sub=blk // 2, per_trip=4)
    y_b = _mm(u_b, w["b_out"], tn=1024, name="b_out_fwd")

    dz_b, dz_b16, dg1, db1, loss = _ln_loss_bwd(x1, y_b, target, g1, b1, ts=ts)
    d_b_out = _mm(u_b, dz_b16, ta=True, tn=1024, tk=1024, name="b_out_dw")
    du_b = _mm(dz_b16, w["b_out"], tb=True, tn=1024, name="b_out_dx")
    dqf16, dkn, dkr, dv, dgate16 = _attention_bwd(q16, k16, v16, du_b, o_b, h1, lse, cos_b, sa_b, sb_b, blk=2 * blk, bk=blk, per_trip=4)
    d_uq = _mm(qn16, dqf16, ta=True, tm=768, tn=2048, tk=1024, name="uq_dw")
    dqn = _mm(dqf16, w["uq"], tb=True, tn=768, tk=2048, name="uq_dx")
    d_up_k = _mm(lat16, dkn, ta=True, tn=2048, tk=1024, name="up_k_dw")
    d_up_v = _mm(lat16, dv, ta=True, tn=2048, tk=1024, name="up_v_dw")
    dlat_k = _mm(dkn, w["up_k"], tb=True, tn=512, name="up_k_dx")
    dlat_v = _mm(dv, w["up_v"], tb=True, tn=512, name="up_v_dx")
    dh1, dkvn, dqnorm = _h1_bwd(h1, dlat_k, dlat_v, dkr, dqn, dgate16, kv_norm, q_norm, cos_b, sa_b, sb_b, ts=ts)
    d_b_in1 = _mm(x1_16, dh1, ta=True, tn=1152, tk=1024, name="b_in_dw")
    dx1 = _mm(dh1, w["b_in1"], tb=True, tn=1024, extras=(dz_b,), epilogue=_residual_store, name="b_in_dx")

    dz_a, dz_a16, dg0, db0 = _ln_bwd_call(dx1, x, y_a, g0, ts=ts)
    d_a_out = _mm(u_a, dz_a16, ta=True, tn=1024, tk=1024, name="a_out_dw")
    du_a = _mm(dz_a16, w["a_out"], tb=True, tn=1024, name="a_out_dx")
    dh_a = _retention_bwd(q_a, k_a, v_a, states, du_a, o_a, h_a, cos_a, sin_a, tables)
    grads = dict(a_out=d_a_out, b_in1=d_b_in1, uq=d_uq, b_out=d_b_out, up_k=d_up_k, up_v=d_up_v)
    small = dict(ln_g=jnp.concatenate([dg0, dg1], axis=0), ln_b=jnp.concatenate([db0, db1], axis=0),
                 q_norm=dqnorm, kv_norm=dkvn)
    if reduce is None:
        grads["a_in"] = _mm(x16, dh_a, ta=True, tn=1536, tk=1024, name="a_in_dw")
        grad_x = _mm(dh_a, w["a_in"], tb=True, tn=1024, tk=2048, extras=(dz_a,), epilogue=_residual_store, name="a_in_dx")
        return loss, grad_x, grads, small
    own_early, travel_early = reduce(_reference_layout_grads(grads), EARLY_SHARDS, EARLY_ROWS, "early")
    d_a_in, got_early = _mm(x16, dh_a, ta=True, tn=1536, tk=1024, name="a_in_dw", side=_chip_exchange_side(travel_early))
    own_late, travel_late = reduce(dict(a_w_in=d_a_in), LATE_SHARDS, LATE_ROWS, "late")
    grad_x, got_late = _mm(dh_a, w["a_in"], tb=True, tn=1024, tk=2048, extras=(dz_a,), epilogue=_residual_store, name="a_in_dx",
                           side=_chip_exchange_side(travel_late))
    return loss, grad_x, ((own_early, got_early), (own_late, got_late)), small


def _flat_shards(shards, dtype, which, total_rows):
    parts = [shards[name].reshape(rows, FLAT_COLS) for name, rows in which]
    used = sum(rows for _, rows in which)
    parts.append(jnp.zeros((total_rows - used, FLAT_COLS), parts[0].dtype))
    return jnp.concatenate(parts, axis=0).astype(dtype)


def _unflat_shards(flat, shapes, shards):
    out, off = {}, 0
    for name, rows in shards:
        out[name] = flat[off:off + rows].reshape(shapes[name])
        off += rows
    return out


COL_SHARDED = {"a_w_in": (D_MODEL, IN_A), "b_w_in": (D_MODEL, IN_B), "b_w_uq": (Q_LORA, H_B * (QK_NOPE + QK_ROPE)),
               "kv_w_up": (KV_LORA, H_B * (QK_NOPE + V_HEAD))}
ROW_SHARDED = {"a_w_out": (WIDTH_A, D_MODEL), "b_w_out": (WIDTH_B, D_MODEL), "kv_w_down": (D_MODEL, KV_LORA + QK_ROPE)}


def _full_from_gathered(gathered, shards=SHARD_ROWS):
    out, off = {}, 0
    for name, rows in shards:
        part = gathered[:, off:off + rows]
        off += rows
        if name in COL_SHARDED:
            r, c = COL_SHARDED[name]
            out[name] = part.reshape(N_CHIPS, r, c // N_CHIPS).transpose(1, 0, 2).reshape(r, c)
        else:
            r, c = ROW_SHARDED[name]
            out[name] = part.reshape(r, c)
    return out


def _chip_major(g):
    r, c = g.shape
    return g.reshape(r, N_CHIPS, c // N_CHIPS).transpose(1, 0, 2)


def _gathered_from_full(full, shards, total_rows):
    if len(shards) == 1 and shards[0][0] in COL_SHARDED:
        return _chip_major(full[shards[0][0]])
    parts = []
    for name, rows in shards:
        g = full[name]
        if name in COL_SHARDED:
            r, c = COL_SHARDED[name]
            g = g.reshape(r, N_CHIPS, c // N_CHIPS).transpose(1, 0, 2)
        parts.append(g.reshape(N_CHIPS, rows, FLAT_COLS))
    used = sum(rows for _, rows in shards)
    if total_rows > used:
        parts.append(jnp.zeros((N_CHIPS, total_rows - used, FLAT_COLS), F32))
    return jnp.concatenate(parts, axis=1)


A_SHARDS, B_SHARDS = SHARD_ROWS[:1], SHARD_ROWS[1:]
A_ROWS = sum(rows for _, rows in A_SHARDS)


def _kernel_layout_a(full):
    return dict(a_in=full["a_w_in"])


def _kernel_layout_b(full):
    uq = full["b_w_uq"].reshape(Q_LORA, H_B, QK_NOPE + QK_ROPE)
    uq = jnp.pad(uq, ((0, 0), (0, 0), (0, QK_PAD - QK_NOPE - QK_ROPE))).reshape(Q_LORA, H_B * QK_PAD)
    up = full["kv_w_up"].reshape(KV_LORA, H_B, QK_NOPE + V_HEAD)
    down = jnp.pad(full["kv_w_down"], ((0, 0), (0, KV_DOWN_PAD - KV_LORA - QK_ROPE)))
    return dict(a_out=full["a_w_out"], b_out=full["b_w_out"], uq=uq,
                up_k=up[:, :, :QK_NOPE].reshape(KV_LORA, H_B * QK_NOPE),
                up_v=up[:, :, QK_NOPE:].reshape(KV_LORA, H_B * V_HEAD),
                b_in1=jnp.concatenate([down, full["b_w_in"]], axis=1))


def _kernel_layout(full):
    return {**_kernel_layout_a(full), **_kernel_layout_b(full)}


EARLY_SHARDS = tuple(sh for sh in SHARD_ROWS if sh[0] != "a_w_in")
LATE_SHARDS = tuple(sh for sh in SHARD_ROWS if sh[0] == "a_w_in")
EARLY_ROWS, LATE_ROWS = 3072, 1536


def _reference_layout_grads(g):
    uq = g["uq"].reshape(Q_LORA, H_B, QK_PAD)[:, :, :QK_NOPE + QK_ROPE].reshape(Q_LORA, H_B * (QK_NOPE + QK_ROPE))
    up = jnp.concatenate([g["up_k"].reshape(KV_LORA, H_B, QK_NOPE), g["up_v"].reshape(KV_LORA, H_B, V_HEAD)], axis=2)
    return dict(a_w_out=g["a_out"], b_w_out=g["b_out"], b_w_uq=uq,
                kv_w_up=up.reshape(KV_LORA, H_B * (QK_NOPE + V_HEAD)),
                kv_w_down=g["b_in1"][:, :KV_LORA + QK_ROPE], b_w_in=g["b_in1"][:, KV_DOWN_PAD:])


HBM_SPEC = pl.BlockSpec(memory_space=pl.ANY)


def _me():
    return lax.axis_index("x"), lax.axis_index("y"), lax.axis_index("c")


def _chip_flips(x, y):
    return [(1 - x, y), (x, 1 - y), (1 - x, 1 - y)]


def _gather_copies(src_ref, out_ref, send_sems, recv_sems):
    x, y, c = _me()
    half = src_ref.shape[0] // 2
    my_rows = pl.ds(pl.multiple_of(c * half, 16), half)
    their_rows = pl.ds(pl.multiple_of((1 - c) * half, 16), half)
    chips = _chip_flips(x, y)
    sibling = (x, y, 1 - c)

    def copy(k, src, dst, to):
        return pltpu.make_async_remote_copy(src_ref=src, dst_ref=dst, send_sem=send_sems.at[k], recv_sem=recv_sems.at[k],
                                            device_id=to, device_id_type=MESH)

    sends = [copy(k, src_ref.at[my_rows, :], out_ref.at[2 * x + y, my_rows, :], (px, py, c)) for k, (px, py) in enumerate(chips)]
    landed = [out_ref.at[2 * px + py, my_rows, :] for px, py in chips]
    lands = [copy(k, landed[k], landed[k], (px, py, c)) for k, (px, py) in enumerate(chips)]
    forwards = [copy(3 + k, landed[k], landed[k], sibling) for k in range(3)]
    theirs = [out_ref.at[2 * px + py, their_rows, :] for px, py in chips]
    arrivals = [copy(3 + k, theirs[k], theirs[k], sibling) for k in range(3)]
    return sends, lands, forwards, arrivals


def _gather_start(src_ref, out_ref, send_sems, recv_sems):
    sends, _, _, _ = _gather_copies(src_ref, out_ref, send_sems, recv_sems)
    for cp in sends:
        cp.start()


def _gather_finish(src_ref, out_ref, send_sems, recv_sems):
    sends, lands, forwards, arrivals = _gather_copies(src_ref, out_ref, send_sems, recv_sems)
    for k in range(3):
        lands[k].wait_recv()
        forwards[k].start()
    for cp in arrivals:
        cp.wait_recv()
    for cp in sends + forwards:
        cp.wait_send()


def _gather_scratch():
    return [pltpu.SemaphoreType.DMA((6,)), pltpu.SemaphoreType.DMA((6,))]


def _gather_weights(flat16):
    def body(src_ref, out_ref, send_sems, recv_sems):
        _gather_start(src_ref, out_ref, send_sems, recv_sems)
        _gather_finish(src_ref, out_ref, send_sems, recv_sems)

    return pl.pallas_call(
        body, name="gather_weights",
        in_specs=[HBM_SPEC], out_specs=HBM_SPEC,
        out_shape=jax.ShapeDtypeStruct((N_CHIPS,) + flat16.shape, flat16.dtype),
        scratch_shapes=_gather_scratch(),
    )(flat16)


def _gather_side(flat16):
    return dict(inputs=[flat16], out_shape=jax.ShapeDtypeStruct((N_CHIPS,) + flat16.shape, flat16.dtype),
                scratch=_gather_scratch(), start=_gather_start, finish=_gather_finish)


def _chip_exchange_copies(p_ref, out_ref, send_sems, recv_sems):
    x, y, c = _me()
    return [pltpu.make_async_remote_copy(
        src_ref=p_ref.at[2 * px + py], dst_ref=out_ref.at[k], send_sem=send_sems.at[k], recv_sem=recv_sems.at[k],
        device_id=(px, py, c), device_id_type=MESH) for k, (px, py) in enumerate(_chip_flips(x, y))]


def _chip_exchange_start(p_ref, out_ref, send_sems, recv_sems):
    for cp in _chip_exchange_copies(p_ref, out_ref, send_sems, recv_sems):
        cp.start()


def _chip_exchange_finish(p_ref, out_ref, send_sems, recv_sems):
    copies = _chip_exchange_copies(p_ref, out_ref, send_sems, recv_sems)
    for cp in copies:
        cp.wait_send()
    for cp in copies:
        cp.wait_recv()


def _chip_exchange_side(p):
    return dict(inputs=[p], out_shape=jax.ShapeDtypeStruct((3,) + p.shape[1:], p.dtype),
                scratch=[pltpu.SemaphoreType.DMA((3,)), pltpu.SemaphoreType.DMA((3,))],
                start=_chip_exchange_start, finish=_chip_exchange_finish)


def _pair_swap(r, name):
    def body(r_ref, out_ref, send_sem, recv_sem):
        x, y, c = _me()
        cp = pltpu.make_async_remote_copy(src_ref=r_ref, dst_ref=out_ref, send_sem=send_sem, recv_sem=recv_sem,
                                          device_id=(x, y, 1 - c), device_id_type=MESH)
        cp.start()
        cp.wait_send()
        cp.wait_recv()

    return pl.pallas_call(
        body, name=name,
        in_specs=[HBM_SPEC], out_specs=HBM_SPEC,
        out_shape=jax.ShapeDtypeStruct(r.shape, r.dtype),
        scratch_shapes=[pltpu.SemaphoreType.DMA, pltpu.SemaphoreType.DMA],
    )(r)


def _sum_small(vec):
    def body(v_ref, out_ref, slots, send_sems, recv_sems):
        x, y, c = _me()
        me = 4 * x + 2 * y + c
        slots[me] = v_ref[...]
        flips = [(fx, fy, fc) for fx in (0, 1) for fy in (0, 1) for fc in (0, 1)][1:]
        copies = []
        for k, (fx, fy, fc) in enumerate(flips):
            copies.append(pltpu.make_async_remote_copy(
                src_ref=v_ref, dst_ref=slots.at[me], send_sem=send_sems.at[k], recv_sem=recv_sems.at[k],
                device_id=(x ^ fx, y ^ fy, c ^ fc), device_id_type=MESH))
        for cp in copies:
            cp.start()
        for cp in copies:
            cp.wait_send()
        for k, (fx, fy, fc) in enumerate(flips):
            src = 4 * (x ^ fx) + 2 * (y ^ fy) + (c ^ fc)
            pltpu.make_async_remote_copy(
                src_ref=v_ref, dst_ref=slots.at[src], send_sem=send_sems.at[k], recv_sem=recv_sems.at[k],
                device_id=(x ^ fx, y ^ fy, c ^ fc), device_id_type=MESH).wait_recv()
        total = slots[0]
        for d in range(1, N_DEV):
            total = total + slots[d]
        out_ref[...] = total

    return pl.pallas_call(
        body, name="sum_small",
        in_specs=[pl.BlockSpec(memory_space=pltpu.VMEM)], out_specs=pl.BlockSpec(memory_space=pltpu.VMEM),
        out_shape=jax.ShapeDtypeStruct(vec.shape, vec.dtype),
        scratch_shapes=[pltpu.VMEM((N_DEV,) + vec.shape, vec.dtype), pltpu.SemaphoreType.DMA((7,)),
                        pltpu.SemaphoreType.DMA((7,))],
    )(vec)


UPD_ROWS = 256


def _pair_sum(g, theirs, core, chip, name):
    half, cols = theirs.shape[1:]
    nb = half // UPD_ROWS

    def body(core_ref, chip_ref, g_ref, t_ref, own_ref, o16_ref):
        total = g_ref[0] + t_ref[0].astype(F32)
        o16_ref[0] = total.astype(o16_ref.dtype)

        @pl.when(pl.program_id(1) == chip_ref[0])
        def _():
            own_ref[...] = total

    return pl.pallas_call(
        body, name=name,
        grid_spec=pltpu.PrefetchScalarGridSpec(
            num_scalar_prefetch=2, grid=(nb, N_CHIPS),
            in_specs=[pl.BlockSpec((1, UPD_ROWS, cols), lambda i, d, core_ref, chip_ref: (d, core_ref[0] * nb + i, 0)),
                      pl.BlockSpec((1, UPD_ROWS, cols), lambda i, d, core_ref, chip_ref: (d, i, 0))],
            out_specs=[pl.BlockSpec((UPD_ROWS, cols), lambda i, d, core_ref, chip_ref: (i, 0)),
                       pl.BlockSpec((1, UPD_ROWS, cols), lambda i, d, core_ref, chip_ref: (d, i, 0))]),
        out_shape=[jax.ShapeDtypeStruct((half, cols), F32),
                   jax.ShapeDtypeStruct((N_CHIPS, half, cols), BF16)],
        compiler_params=_params(("parallel", "arbitrary")),
    )(core, chip, g, theirs)


def _chip_sum(own, received, name):
    half, cols = own.shape
    nb = half // UPD_ROWS

    def body(p_ref, r_ref, o_ref):
        o_ref[...] = ((p_ref[...] + r_ref[0].astype(F32)) + r_ref[1].astype(F32)) + r_ref[2].astype(F32)

    return pl.pallas_call(
        body, name=name, grid=(nb,),
        in_specs=[pl.BlockSpec((UPD_ROWS, cols), lambda i: (i, 0)),
                  pl.BlockSpec((3, UPD_ROWS, cols), lambda i: (0, i, 0))],
        out_specs=pl.BlockSpec((UPD_ROWS, cols), lambda i: (i, 0)),
        out_shape=jax.ShapeDtypeStruct((half, cols), F32),
        compiler_params=_params(("parallel",)),
    )(own, received)


def _adamw(w, g, m, v, *, rows, name):
    r, c = w.shape
    rows = min(rows, r)
    assert r % rows == 0

    def body(w_ref, g_ref, m_ref, v_ref, d_ref, nm_ref, nv_ref):
        gv = g_ref[...]
        nm = ADAM_B1 * m_ref[...] + (1.0 - ADAM_B1) * gv
        nv = ADAM_B2 * v_ref[...] + (1.0 - ADAM_B2) * (gv * gv)
        m_hat = nm / (1.0 - ADAM_B1 ** ADAM_STEP)
        v_hat = nv / (1.0 - ADAM_B2 ** ADAM_STEP)
        d_ref[...] = -ADAM_LR * (m_hat / (jnp.sqrt(v_hat) + ADAM_EPS) + ADAM_WD * w_ref[...])
        nm_ref[...] = nm
        nv_ref[...] = nv

    spec = pl.BlockSpec((rows, c), lambda i: (i, 0))
    return pl.pallas_call(
        body, name=name, grid=(r // rows,),
        in_specs=[spec] * 4, out_specs=[spec] * 3,
        out_shape=[jax.ShapeDtypeStruct((r, c), F32)] * 3,
        compiler_params=_params(("parallel",)),
    )(w, g, m, v)


W_NAMES = ("a_w_in", "a_w_out", "b_w_in", "b_q_norm", "b_w_uq", "b_w_out", "kv_w_down", "kv_norm", "kv_w_up", "ln_g", "ln_b")
BIG = tuple(name for name, _ in SHARD_ROWS)


def _pack_small(ln_g, ln_b, q_norm, kv_norm, extra=None):
    pad = lambda a: jnp.pad(a.reshape(1, -1), ((0, 0), (0, FLAT_COLS - a.size)))
    rows = [ln_g, ln_b, pad(q_norm), pad(kv_norm),
            jnp.zeros((1, FLAT_COLS), F32) if extra is None else pad(extra), jnp.zeros((1, FLAT_COLS), F32)]
    return jnp.concatenate(rows, axis=0)


def _unpack_small(p):
    return dict(ln_g=p[0:2], ln_b=p[2:4], b_q_norm=p[4:5, :Q_LORA], kv_norm=p[5, :KV_LORA])


def kernel(x, a_w_in, a_w_out, b_w_in, b_q_norm, b_w_uq, b_w_out, kv_w_down, kv_norm, kv_w_up, ln_g, ln_b, loss_target, m_a_w_in, m_a_w_out, m_b_w_in, m_b_q_norm, m_b_w_uq, m_b_w_out, m_kv_w_down, m_kv_norm, m_kv_w_up, m_ln_g, m_ln_b, v_a_w_in, v_a_w_out, v_b_w_in, v_b_q_norm, v_b_w_uq, v_b_w_out, v_kv_w_down, v_kv_norm, v_kv_w_up, v_ln_g, v_ln_b):
    w_in = dict(a_w_in=a_w_in[0], a_w_out=a_w_out[0], b_w_in=b_w_in[0], b_w_uq=b_w_uq[0], b_w_out=b_w_out[0],
                kv_w_down=kv_w_down, kv_w_up=kv_w_up)
    m_in = dict(a_w_in=m_a_w_in[0], a_w_out=m_a_w_out[0], b_w_in=m_b_w_in[0], b_w_uq=m_b_w_uq[0], b_w_out=m_b_w_out[0],
                kv_w_down=m_kv_w_down, kv_w_up=m_kv_w_up)
    v_in = dict(a_w_in=v_a_w_in[0], a_w_out=v_a_w_out[0], b_w_in=v_b_w_in[0], b_w_uq=v_b_w_uq[0], b_w_out=v_b_w_out[0],
                kv_w_down=v_kv_w_down, kv_w_up=v_kv_w_up)
    shard_shapes = {name: w_in[name].shape for name in BIG}

    cx, cy, cc = lax.axis_index("x"), lax.axis_index("y"), lax.axis_index("c")
    chip = 2 * cx + cy
    flat_b = _flat_shards(w_in, BF16, B_SHARDS, FLAT_ROWS - A_ROWS)
    a16 = w_in["a_w_in"].astype(BF16)
    got_a = lax.dynamic_update_slice(_gather_weights(a16), a16[None], (chip, 0, 0))
    weights_a = dict(a_in=got_a.transpose(1, 0, 2).reshape(D_MODEL, IN_A))

    core_arr, chip_arr = cc.astype(jnp.int32).reshape(1), chip.astype(jnp.int32).reshape(1)

    def reduce_pair(full, shards, rows, tag):
        g_all = _gathered_from_full(full, shards, rows)
        half, cols = g_all.shape[1] // 2, g_all.shape[2]
        other_half = lax.dynamic_slice(g_all, (0, (1 - cc) * half, 0), (N_CHIPS, half, cols)).astype(BF16)
        theirs = _pair_swap(other_half, "pair_exchange_" + tag)
        return _pair_sum(g_all, theirs, core_arr, chip_arr, "pair_sum_" + tag)

    loss, grad_x, reduced, small = _local_step(x[0], loss_target[0], weights_a, kv_norm.reshape(1, -1), b_q_norm, ln_g, ln_b,
                                               late=(flat_b, chip), reduce=reduce_pair)

    g_big = {}
    for (own, received), shards, tag in zip(reduced, (EARLY_SHARDS, LATE_SHARDS), ("early", "late")):
        mine = _chip_sum(own, received, "chip_sum_" + tag)
        sibling = _pair_swap(mine, "pair_share_" + tag)
        g_flat = jnp.concatenate([jnp.where(cc == 0, mine, sibling), jnp.where(cc == 0, sibling, mine)], axis=0)
        if g_flat.shape == shard_shapes[shards[0][0]]:
            g_big[shards[0][0]] = g_flat
        else:
            g_big.update(_unflat_shards(g_flat, shard_shapes, shards))

    small_sum = _sum_small(_pack_small(small["ln_g"], small["ln_b"], small["q_norm"], small["kv_norm"], loss[:, :1]))
    loss_out = small_sum[6, 0]

    row = lambda a: a.reshape(1, -1)
    g_all = {**g_big, **_unpack_small(small_sum)}
    g_all["kv_norm"] = row(g_all["kv_norm"])
    state = {**{name: (w_in[name], m_in[name], v_in[name]) for name in BIG},
             "ln_g": (ln_g, m_ln_g, v_ln_g), "ln_b": (ln_b, m_ln_b, v_ln_b), "b_q_norm": (b_q_norm, m_b_q_norm, v_b_q_norm),
             "kv_norm": (row(kv_norm), row(m_kv_norm), row(v_kv_norm))}
    upd = {name: _adamw(state[name][0], g_all[name], state[name][1], state[name][2], rows=256, name="adamw_" + name)
           for name in W_NAMES}

    def shaped(name, a):
        if name == "kv_norm":
            return a.reshape(-1)
        return a[None] if name in ("a_w_in", "a_w_out", "b_w_in", "b_w_uq", "b_w_out") else a

    outputs = [[shaped(name, g_all[name]) for name in W_NAMES]]
    outputs += [[shaped(name, upd[name][k]) for name in W_NAMES] for k in range(3)]
    return (loss_out, grad_x[None], *outputs[0], *outputs[1], *outputs[2], *outputs[3])
```

```python
import functools
import math

import jax
import jax.numpy as jnp
from jax import lax
from jax.experimental import pallas as pl
from jax.experimental.pallas import tpu as pltpu

F32 = jnp.float32
BF16 = jnp.bfloat16
MESH = pl.DeviceIdType.MESH

D_MODEL = 1024
DEPTH = 2
H_A, DK_A, DV_A = 4, 256, 512
WIDTH_A = H_A * DV_A
CHUNK = 128
H_B, QK_NOPE, QK_ROPE, V_HEAD = 16, 128, 64, 128
QK_PAD = 256
Q_LORA, KV_LORA = 768, 512
KV_DOWN_PAD = 640
WIDTH_B = H_B * V_HEAD
IN_A = 2 * H_A * DK_A + 2 * WIDTH_A
IN_B = Q_LORA + WIDTH_B
H1_B = KV_DOWN_PAD + IN_B
ROPE_BASE = 10000.0
ALPHA = (2.0 * DEPTH) ** 0.25
ATT_SCALE = (QK_NOPE + QK_ROPE) ** -0.5
NEG_BIG = -1e30

ADAM_LR, ADAM_B1, ADAM_B2, ADAM_EPS, ADAM_WD, ADAM_STEP = 0.001, 0.9, 0.999, 1e-08, 0.01, 10

VMEM_LIMIT_BYTES = 56 * 1024 * 1024
LANES = 128
FLAT_COLS = 1024
SHARD_ROWS = (("a_w_in", 1536), ("a_w_out", 512), ("b_w_in", 704), ("b_w_uq", 576), ("b_w_out", 512),
              ("kv_w_down", 144), ("kv_w_up", 512))
FLAT_ROWS = 4608
N_CHIPS = 4
N_DEV = 8


def _params(sem, vmem=VMEM_LIMIT_BYTES):
    return pltpu.CompilerParams(dimension_semantics=sem, vmem_limit_bytes=vmem)


def _row_spec(ts, w, col_block=0):
    return pl.BlockSpec((ts, w), lambda i: (i, col_block))


def _bc_spec(shape):
    nd = len(shape)
    return pl.BlockSpec(shape, lambda i: (0,) * nd)


def _sigmoid(x):
    return 1.0 / (1.0 + jnp.exp(-x))


def _fold8(v):
    ts, w = v.shape
    return jnp.sum(v.reshape(ts // 8, 8, w), axis=0)


def _mm(a, b, *, ta=False, tb=False, out_dtype=F32, tm=1024, tn=512, tk=None, name, extras=(), epilogue=None, out_tn=None,
        side=None):
    if ta:
        K, M = a.shape
    else:
        M, K = a.shape
    if tb:
        N, Kb = b.shape
    else:
        Kb, N = b.shape
    assert K == Kb, (a.shape, b.shape)
    tm, tn = min(tm, M), min(tn, N)
    tk = K if tk is None else min(tk, K)
    assert M % tm == 0 and N % tn == 0 and K % tk == 0, (name, M, N, K, tm, tn, tk)
    grid = (M // tm, N // tn, K // tk)
    nk = grid[2]
    out_tn = tn if out_tn is None else out_tn
    n_extra = len(extras)
    side_inputs = [] if side is None else list(side["inputs"])
    n_side = len(side_inputs)
    n_acc = 0 if nk == 1 else 1
    dims = (((0,) if ta else (1,), (1,) if tb else (0,)), ((), ()))

    def body(a_ref, b_ref, *rest):
        extra_refs, rest = rest[:n_extra], rest[n_extra:]
        side_in, o_ref, rest = rest[:n_side], rest[n_side], rest[n_side + 1:]
        if side is not None:
            side_refs, rest = side_in + rest[:1] + rest[1 + n_acc:], rest[1:]
            ids = [pl.program_id(d) for d in range(3)]

            @pl.when((ids[0] == 0) & (ids[1] == 0) & (ids[2] == 0))
            def _():
                side["start"](*side_refs)

        prod = lax.dot_general(a_ref[...].astype(BF16), b_ref[...].astype(BF16), dims,
                               preferred_element_type=F32)

        def store(tile):
            if epilogue is None:
                o_ref[...] = tile.astype(o_ref.dtype)
            else:
                epilogue(tile, o_ref, *extra_refs)

        if nk == 1:
            store(prod)
        else:
            acc = rest[0]
            k = pl.program_id(2)

            @pl.when(k == 0)
            def _():
                acc[...] = prod

            @pl.when(k > 0)
            def _():
                acc[...] += prod

            @pl.when(k == nk - 1)
            def _():
                store(acc[...])

        if side is not None:
            @pl.when((ids[0] == grid[0] - 1) & (ids[1] == grid[1] - 1) & (ids[2] == grid[2] - 1))
            def _():
                side["finish"](*side_refs)

    a_spec = pl.BlockSpec((tk, tm), lambda i, j, k: (k, i)) if ta else pl.BlockSpec((tm, tk), lambda i, j, k: (i, k))
    b_spec = pl.BlockSpec((tn, tk), lambda i, j, k: (j, k)) if tb else pl.BlockSpec((tk, tn), lambda i, j, k: (k, j))
    extra_specs = [pl.BlockSpec((tm, e.shape[1]), lambda i, j, k: (i, 0)) for e in extras]
    out_specs = pl.BlockSpec((tm, out_tn), lambda i, j, k: (i, j))
    out_shape = jax.ShapeDtypeStruct((M, (N // tn) * out_tn), out_dtype)
    scratch = [] if nk == 1 else [pltpu.VMEM((tm, tn), F32)]
    if side is not None:
        out_specs, out_shape, scratch = [out_specs, HBM_SPEC], [out_shape, side["out_shape"]], scratch + list(side["scratch"])
    return pl.pallas_call(
        body, name=name, grid=grid,
        in_specs=[a_spec, b_spec] + extra_specs + [HBM_SPEC] * n_side,
        out_specs=out_specs, out_shape=out_shape, scratch_shapes=scratch,
        compiler_params=_params(("parallel", "parallel", "arbitrary") if side is None else ("arbitrary",) * 3),
    )(a, b, *extras, *side_inputs)


def _rope_tables_a(s):
    half = DK_A // 2
    inv = ROPE_BASE ** (-jnp.arange(half, dtype=F32) / half)
    ang = jnp.arange(s, dtype=F32)[:, None] * inv[None, :]
    return jnp.cos(ang), jnp.sin(ang)


def _rope_tables_b(s):
    half = QK_ROPE // 2
    inv = ROPE_BASE ** (-jnp.arange(half, dtype=F32) / half)
    ang = jnp.arange(s, dtype=F32)[:, None] * inv[None, :]
    c, sn = jnp.cos(ang), jnp.sin(ang)
    z = jnp.zeros_like(c)
    cos = jnp.concatenate([c, c, z, z], axis=1)
    sa = jnp.concatenate([-sn, z, z, z], axis=1)
    sb = jnp.concatenate([z, sn, z, z], axis=1)
    return cos, sa, sb


def _rope_b(r, cos, sa, sb, sign):
    return r * cos + sign * (pltpu.roll(r, 96, 1) * sa + pltpu.roll(r, 32, 1) * sb)


def _retention_tables():
    lg = jnp.log1p(-jnp.exp2(-5.0 - jnp.arange(H_A, dtype=F32)))
    idx = jnp.arange(CHUNK, dtype=F32)
    diff = idx[:, None] - idx[None, :]
    causal = diff >= 0
    dmat = jnp.where(causal, jnp.exp(jnp.where(causal, diff, 0.0)[None] * lg[:, None, None]), 0.0)
    qdec = jnp.exp((idx + 1.0)[None, :] * lg[:, None])[:, :, None]
    kdec = jnp.exp((CHUNK - 1.0 - idx)[None, :] * lg[:, None])[:, :, None]
    cdec = jnp.broadcast_to(jnp.exp(CHUNK * lg)[:, None, None], (H_A, 1, DV_A))
    return dmat, qdec, kdec, cdec


def _group_norm(o):
    mu = jnp.mean(o, axis=-1, keepdims=True)
    oc = o - mu
    var = jnp.mean(oc * oc, axis=-1, keepdims=True)
    rstd = lax.rsqrt(var + 1e-5)
    return oc * rstd, rstd


Q_COL, K_COL, V_COL, GATE_COL = 0, H_A * DK_A, 2 * H_A * DK_A, 2 * H_A * DK_A + WIDTH_A


def _ln_stats(z):
    mu = jnp.mean(z, axis=-1, keepdims=True)
    zc = z - mu
    var = jnp.mean(zc * zc, axis=-1, keepdims=True)
    rstd = lax.rsqrt(var + 1e-5)
    return zc * rstd, rstd


def _ln_bwd(dy, xhat, rstd, g):
    dxh = dy * g
    m1 = jnp.mean(dxh, axis=-1, keepdims=True)
    m2 = jnp.mean(dxh * xhat, axis=-1, keepdims=True)
    return rstd * (dxh - m1 - xhat * m2)


def _ln_fwd(x, y, g, b, *, ts):
    s = x.shape[0]

    def body(x_ref, y_ref, g_ref, b_ref, o_ref, o16_ref):
        xhat, _ = _ln_stats(ALPHA * x_ref[...] + y_ref[...])
        out = xhat * g_ref[...] + b_ref[...]
        o_ref[...] = out
        o16_ref[...] = out.astype(o16_ref.dtype)

    return pl.pallas_call(
        body, name="ln_fwd", grid=(s // ts,),
        in_specs=[_row_spec(ts, D_MODEL), _row_spec(ts, D_MODEL), _bc_spec((1, D_MODEL)), _bc_spec((1, D_MODEL))],
        out_specs=[_row_spec(ts, D_MODEL), _row_spec(ts, D_MODEL)],
        out_shape=[jax.ShapeDtypeStruct((s, D_MODEL), F32), jax.ShapeDtypeStruct((s, D_MODEL), BF16)],
        compiler_params=_params(("parallel",)),
    )(x, y, g, b)


def _ln_loss_bwd(x1, y, target, g, b, *, ts):
    s = x1.shape[0]
    n = s // ts

    def body(x_ref, y_ref, t_ref, g_ref, b_ref, dz_ref, dz16_ref, dg_ref, db_ref, loss_ref, ag, ab, al):
        i = pl.program_id(0)

        @pl.when(i == 0)
        def _():
            ag[...] = jnp.zeros_like(ag)
            ab[...] = jnp.zeros_like(ab)
            al[...] = jnp.zeros_like(al)

        xhat, rstd = _ln_stats(ALPHA * x_ref[...] + y_ref[...])
        err = xhat * g_ref[...] + b_ref[...] - t_ref[...]
        al[...] += _fold8(err * err)
        dy = err * (1.0 / D_MODEL)
        ag[...] += _fold8(dy * xhat)
        ab[...] += _fold8(dy)
        dz = _ln_bwd(dy, xhat, rstd, g_ref[...])
        dz_ref[...] = dz
        dz16_ref[...] = dz.astype(dz16_ref.dtype)

        @pl.when(i == n - 1)
        def _():
            dg_ref[...] = jnp.sum(ag[...], axis=0, keepdims=True)
            db_ref[...] = jnp.sum(ab[...], axis=0, keepdims=True)
            loss_ref[...] = jnp.full((1, LANES), (0.5 / D_MODEL) * jnp.sum(al[...]), F32)

    return pl.pallas_call(
        body, name="ln_loss_bwd", grid=(n,),
        in_specs=[_row_spec(ts, D_MODEL)] * 3 + [_bc_spec((1, D_MODEL))] * 2,
        out_specs=[_row_spec(ts, D_MODEL), _row_spec(ts, D_MODEL), _bc_spec((1, D_MODEL)), _bc_spec((1, D_MODEL)),
                   _bc_spec((1, LANES))],
        out_shape=[jax.ShapeDtypeStruct((s, D_MODEL), F32), jax.ShapeDtypeStruct((s, D_MODEL), BF16),
                   jax.ShapeDtypeStruct((1, D_MODEL), F32), jax.ShapeDtypeStruct((1, D_MODEL), F32),
                   jax.ShapeDtypeStruct((1, LANES), F32)],
        scratch_shapes=[pltpu.VMEM((8, D_MODEL), F32)] * 3,
        compiler_params=_params(("arbitrary",)),
    )(x1, y, target, g, b)


def _ln_bwd_call(dy, x, y, g, *, ts):
    s = x.shape[0]
    n = s // ts

    def body(dy_ref, x_ref, y_ref, g_ref, dz_ref, dz16_ref, dg_ref, db_ref, ag, ab):
        i = pl.program_id(0)

        @pl.when(i == 0)
        def _():
            ag[...] = jnp.zeros_like(ag)
            ab[...] = jnp.zeros_like(ab)

        xhat, rstd = _ln_stats(ALPHA * x_ref[...] + y_ref[...])
        dy = dy_ref[...]
        ag[...] += _fold8(dy * xhat)
        ab[...] += _fold8(dy)
        dz = _ln_bwd(dy, xhat, rstd, g_ref[...])
        dz_ref[...] = dz
        dz16_ref[...] = dz.astype(dz16_ref.dtype)

        @pl.when(i == n - 1)
        def _():
            dg_ref[...] = jnp.sum(ag[...], axis=0, keepdims=True)
            db_ref[...] = jnp.sum(ab[...], axis=0, keepdims=True)

    return pl.pallas_call(
        body, name="ln_bwd", grid=(n,),
        in_specs=[_row_spec(ts, D_MODEL)] * 3 + [_bc_spec((1, D_MODEL))],
        out_specs=[_row_spec(ts, D_MODEL), _row_spec(ts, D_MODEL), _bc_spec((1, D_MODEL)), _bc_spec((1, D_MODEL))],
        out_shape=[jax.ShapeDtypeStruct((s, D_MODEL), F32), jax.ShapeDtypeStruct((s, D_MODEL), BF16),
                   jax.ShapeDtypeStruct((1, D_MODEL), F32), jax.ShapeDtypeStruct((1, D_MODEL), F32)],
        scratch_shapes=[pltpu.VMEM((8, D_MODEL), F32)] * 2,
        compiler_params=_params(("arbitrary",)),
    )(dy, x, y, g)


def _residual_store(tile, o_ref, dz_ref):
    o_ref[...] = ALPHA * dz_ref[...] + tile


C_LAT = slice(0, KV_LORA)
C_ROPE = slice(KV_LORA, KV_DOWN_PAD)
C_QL = slice(KV_DOWN_PAD, KV_DOWN_PAD + Q_LORA)
C_GATE = slice(KV_DOWN_PAD + Q_LORA, H1_B)


def _rms(x, eps=1e-6):
    r = lax.rsqrt(jnp.mean(x * x, axis=-1, keepdims=True) + eps)
    return x * r, r


def _rms_bwd(dy, xhat, r, g):
    dxh = dy * g
    return r * (dxh - xhat * jnp.mean(dxh * xhat, axis=-1, keepdims=True))


def _kvq_prep(h1, kv_norm, q_norm, cos, sa, sb, *, ts):
    s = h1.shape[0]

    def body(h_ref, kn_ref, qn_ref, c_ref, sa_ref, sb_ref, lat_ref, kr_ref, ql_ref):
        lat, _ = _rms(h_ref[:, C_LAT])
        lat_ref[...] = (lat * kn_ref[...]).astype(lat_ref.dtype)
        kr_ref[...] = _rope_b(h_ref[:, C_ROPE], c_ref[...], sa_ref[...], sb_ref[...], 1.0).astype(kr_ref.dtype)
        ql, _ = _rms(h_ref[:, C_QL])
        ql_ref[...] = (ql * qn_ref[...]).astype(ql_ref.dtype)

    return pl.pallas_call(
        body, name="kvq_prep", grid=(s // ts,),
        in_specs=[_row_spec(ts, H1_B), _bc_spec((1, KV_LORA)), _bc_spec((1, Q_LORA))] + [_row_spec(ts, 128)] * 3,
        out_specs=[_row_spec(ts, KV_LORA), _row_spec(ts, 128), _row_spec(ts, Q_LORA)],
        out_shape=[jax.ShapeDtypeStruct((s, KV_LORA), BF16), jax.ShapeDtypeStruct((s, 128), BF16),
                   jax.ShapeDtypeStruct((s, Q_LORA), BF16)],
        compiler_params=_params(("parallel",)),
    )(h1, kv_norm, q_norm, cos, sa, sb)


LOG2E = 1.4426950408889634
LN2 = 0.6931471805599453
Q_SCALE = ATT_SCALE * LOG2E


def _rope_q_store(tile, o_ref, c_ref, sa_ref, sb_ref):
    c, a, b = c_ref[...], sa_ref[...], sb_ref[...]
    for hd in range(tile.shape[1] // QK_PAD):
        lo = hd * QK_PAD
        o_ref[:, lo:lo + 128] = (tile[:, lo:lo + 128] * Q_SCALE).astype(o_ref.dtype)
        o_ref[:, lo + 128:lo + 256] = (_rope_b(tile[:, lo + 128:lo + 256], c, a, b, 1.0) * Q_SCALE).astype(o_ref.dtype)


def _assemble_k_store(tile, o_ref, kr_ref):
    r = kr_ref[...]
    for hd in range(tile.shape[1] // QK_NOPE):
        o_ref[:, hd * QK_PAD:hd * QK_PAD + 128] = tile[:, hd * 128:(hd + 1) * 128].astype(o_ref.dtype)
        o_ref[:, hd * QK_PAD + 128:(hd + 1) * QK_PAD] = r


def _h1_bwd(h1, dlat_k, dlat_v, dkr_heads, dqn, dg16, kv_norm, q_norm, cos, sa, sb, *, ts):
    s = h1.shape[0]
    n = s // ts

    def body(h_ref, dk_ref, dv_ref, dkr_ref, dqn_ref, dg_ref, kn_ref, qn_ref, c_ref, sa_ref, sb_ref,
             o_ref, dkn_ref, dqn_out_ref, akn, aqn):
        i = pl.program_id(0)

        @pl.when(i == 0)
        def _():
            akn[...] = jnp.zeros_like(akn)
            aqn[...] = jnp.zeros_like(aqn)

        lat, r = _rms(h_ref[:, C_LAT])
        dlat = dk_ref[...] + dv_ref[...]
        akn[...] += _fold8(dlat * lat)
        o_ref[:, C_LAT] = _rms_bwd(dlat, lat, r, kn_ref[...]).astype(o_ref.dtype)

        dkr = dkr_ref[:, 0:128]
        for hd in range(1, H_B):
            dkr = dkr + dkr_ref[:, hd * 128:(hd + 1) * 128]
        o_ref[:, C_ROPE] = _rope_b(dkr, c_ref[...], sa_ref[...], sb_ref[...], -1.0).astype(o_ref.dtype)

        ql, rq = _rms(h_ref[:, C_QL])
        dq = dqn_ref[...]
        aqn[...] += _fold8(dq * ql)
        o_ref[:, C_QL] = _rms_bwd(dq, ql, rq, qn_ref[...]).astype(o_ref.dtype)
        o_ref[:, C_GATE] = dg_ref[...]

        @pl.when(i == n - 1)
        def _():
            dkn_ref[...] = jnp.sum(akn[...], axis=0, keepdims=True)
            dqn_out_ref[...] = jnp.sum(aqn[...], axis=0, keepdims=True)

    return pl.pallas_call(
        body, name="h1_bwd", grid=(n,),
        in_specs=[_row_spec(ts, H1_B), _row_spec(ts, KV_LORA), _row_spec(ts, KV_LORA), _row_spec(ts, H_B * 128),
                  _row_spec(ts, Q_LORA), _row_spec(ts, WIDTH_B), _bc_spec((1, KV_LORA)), _bc_spec((1, Q_LORA))]
        + [_row_spec(ts, 128)] * 3,
        out_specs=[_row_spec(ts, H1_B), _bc_spec((1, KV_LORA)), _bc_spec((1, Q_LORA))],
        out_shape=[jax.ShapeDtypeStruct((s, H1_B), BF16), jax.ShapeDtypeStruct((1, KV_LORA), F32),
                   jax.ShapeDtypeStruct((1, Q_LORA), F32)],
        scratch_shapes=[pltpu.VMEM((8, KV_LORA), F32), pltpu.VMEM((8, Q_LORA), F32)],
        compiler_params=_params(("arbitrary",)),
    )(h1, dlat_k, dlat_v, dkr_heads, dqn, dg16, kv_norm, q_norm, cos, sa, sb)


def _dot(a, b, ca, cb):
    return lax.dot_general(a, b, (((ca,), (cb,)), ((), ())), preferred_element_type=F32)


CHUNKS_PER_STEP = 2


def _table_specs():
    full = lambda shape: pl.BlockSpec(shape, lambda i: (0,) * len(shape))
    return [full((H_A, CHUNK, CHUNK)), full((H_A, CHUNK, 1)), full((H_A, CHUNK, 1)), full((H_A, 1, DV_A))]


def _retention_fwd(h_a, cos, sin, tables):
    s = h_a.shape[0]
    n = s // CHUNK

    def body(h_ref, c_ref, s_ref, dm_ref, qd_ref, kd_ref, cd_ref, q_ref, k_ref, v_ref, o_ref, u_ref, st_ref, state):
        @pl.when(pl.program_id(0) == 0)
        def _():
            state[...] = jnp.zeros_like(state)

        for cb in range(CHUNKS_PER_STEP):
            rows = slice(cb * CHUNK, (cb + 1) * CHUNK)
            c, sn = c_ref[rows, :], s_ref[rows, :]
            for hd in range(H_A):
                qs, vs = slice(hd * DK_A, (hd + 1) * DK_A), slice(hd * DV_A, (hd + 1) * DV_A)
                for r_ref, base, scale in ((q_ref, Q_COL, 1.0), (k_ref, K_COL, DK_A ** -0.5)):
                    lo = base + hd * DK_A
                    x1, x2 = h_ref[rows, lo:lo + 128], h_ref[rows, lo + 128:lo + 256]
                    r_ref[rows, hd * DK_A:hd * DK_A + 128] = ((x1 * c - x2 * sn) * scale).astype(r_ref.dtype)
                    r_ref[rows, hd * DK_A + 128:(hd + 1) * DK_A] = ((x2 * c + x1 * sn) * scale).astype(r_ref.dtype)
                v_ref[rows, vs] = h_ref[rows, V_COL + hd * DV_A:V_COL + (hd + 1) * DV_A].astype(v_ref.dtype)
                qv, kv, vv = q_ref[rows, qs], k_ref[rows, qs], v_ref[rows, vs]
                st = state[hd]
                st16 = st.astype(BF16)
                st_ref[cb, hd] = st16
                scores = _dot(qv, kv, 1, 1) * dm_ref[hd]
                qd = (qv.astype(F32) * qd_ref[hd]).astype(BF16)
                o = _dot(scores.astype(BF16), vv, 1, 0) + _dot(qd, st16, 1, 0)
                o_ref[rows, vs] = o
                kd = (kv.astype(F32) * kd_ref[hd]).astype(BF16)
                state[hd] = st * cd_ref[hd] + _dot(kd, vv, 0, 0)
                on, _ = _group_norm(o)
                g = h_ref[rows, GATE_COL + hd * DV_A:GATE_COL + (hd + 1) * DV_A]
                u_ref[rows, vs] = (on * (g * _sigmoid(g))).astype(u_ref.dtype)

    row = lambda w: pl.BlockSpec((CHUNKS_PER_STEP * CHUNK, w), lambda i: (i, 0))
    return pl.pallas_call(
        body, name="retention_fwd", grid=(n // CHUNKS_PER_STEP,),
        in_specs=[row(IN_A), row(128), row(128)] + _table_specs(),
        out_specs=[row(H_A * DK_A), row(H_A * DK_A), row(WIDTH_A), row(WIDTH_A), row(WIDTH_A),
                   pl.BlockSpec((CHUNKS_PER_STEP, H_A, DK_A, DV_A), lambda i: (i, 0, 0, 0))],
        out_shape=[jax.ShapeDtypeStruct((s, H_A * DK_A), BF16), jax.ShapeDtypeStruct((s, H_A * DK_A), BF16),
                   jax.ShapeDtypeStruct((s, WIDTH_A), BF16), jax.ShapeDtypeStruct((s, WIDTH_A), F32),
                   jax.ShapeDtypeStruct((s, WIDTH_A), BF16), jax.ShapeDtypeStruct((n, H_A, DK_A, DV_A), BF16)],
        scratch_shapes=[pltpu.VMEM((H_A, DK_A, DV_A), F32)],
        compiler_params=_params(("arbitrary",)),
    )(h_a, cos, sin, *tables)


def _retention_bwd(q, k, v, states, du, o, h_a, cos, sin, tables):
    s = q.shape[0]
    n = s // CHUNK

    def body(q_ref, k_ref, v_ref, st_ref, du_ref, o_ref, g_ref, c_ref, s_ref, dm_ref, qd_ref, kd_ref, cd_ref, dh_ref, grad_state):
        @pl.when(pl.program_id(0) == 0)
        def _():
            grad_state[...] = jnp.zeros_like(grad_state)

        for cb in reversed(range(CHUNKS_PER_STEP)):
            rows = slice(cb * CHUNK, (cb + 1) * CHUNK)
            c, sn = c_ref[rows, :], s_ref[rows, :]
            for hd in range(H_A):
                qs, vs = slice(hd * DK_A, (hd + 1) * DK_A), slice(hd * DV_A, (hd + 1) * DV_A)
                on, rstd = _group_norm(o_ref[rows, vs])
                g = g_ref[rows, vs]
                sg = _sigmoid(g)
                du_v = du_ref[rows, vs]
                don = du_v * (g * sg)
                dh_ref[rows, GATE_COL + hd * DV_A:GATE_COL + (hd + 1) * DV_A] = (du_v * on * (sg * (1.0 + g * (1.0 - sg)))).astype(dh_ref.dtype)
                m1 = jnp.mean(don, axis=-1, keepdims=True)
                m2 = jnp.mean(don * on, axis=-1, keepdims=True)
                dov = (rstd * (don - m1 - on * m2)).astype(BF16)

                qv, kv, vv = q_ref[rows, qs], k_ref[rows, qs], v_ref[rows, vs]
                dm = dm_ref[hd]
                gs = grad_state[hd]
                g16 = gs.astype(BF16)
                scores = (_dot(qv, kv, 1, 1) * dm).astype(BF16)
                dscores = (_dot(dov, vv, 1, 1) * dm).astype(BF16)
                qd = (qv.astype(F32) * qd_ref[hd]).astype(BF16)
                kd = (kv.astype(F32) * kd_ref[hd]).astype(BF16)
                dq = _dot(dscores, kv, 1, 0) + _dot(dov, st_ref[cb, hd], 1, 1) * qd_ref[hd]
                dk = (_dot(dscores, qv, 0, 0) + _dot(vv, g16, 1, 1) * kd_ref[hd]) * (DK_A ** -0.5)
                dh_ref[rows, V_COL + hd * DV_A:V_COL + (hd + 1) * DV_A] = (_dot(scores, dov, 0, 0) + _dot(kd, g16, 1, 0)).astype(dh_ref.dtype)
                grad_state[hd] = gs * cd_ref[hd] + _dot(qd, dov, 0, 0)
                for d, base in ((dq, Q_COL), (dk, K_COL)):
                    lo = base + hd * DK_A
                    d1, d2 = d[:, 0:128], d[:, 128:256]
                    dh_ref[rows, lo:lo + 128] = (d1 * c + d2 * sn).astype(dh_ref.dtype)
                    dh_ref[rows, lo + 128:lo + 256] = (d2 * c - d1 * sn).astype(dh_ref.dtype)

    steps = n // CHUNKS_PER_STEP
    rev = lambda i: steps - 1 - i
    row = lambda w, col=0: pl.BlockSpec((CHUNKS_PER_STEP * CHUNK, w), lambda i: (rev(i), col))
    return pl.pallas_call(
        body, name="retention_bwd", grid=(steps,),
        in_specs=[row(H_A * DK_A), row(H_A * DK_A), row(WIDTH_A),
                  pl.BlockSpec((CHUNKS_PER_STEP, H_A, DK_A, DV_A), lambda i: (rev(i), 0, 0, 0)),
                  row(WIDTH_A), row(WIDTH_A), row(WIDTH_A, GATE_COL // WIDTH_A), row(128), row(128)] + _table_specs(),
        out_specs=row(IN_A),
        out_shape=jax.ShapeDtypeStruct((s, IN_A), BF16),
        scratch_shapes=[pltpu.VMEM((H_A, DK_A, DV_A), F32)],
        compiler_params=_params(("arbitrary",)),
    )(q, k, v, states, du, o, h_a, cos, sin, *tables)


GATE_BLOCK0 = (KV_DOWN_PAD + Q_LORA) // LANES


def _causal_mask(sc, row0):
    row = lax.broadcasted_iota(jnp.int32, sc.shape, 0) + row0
    col = lax.broadcasted_iota(jnp.int32, sc.shape, 1)
    return jnp.where(col <= row, sc, NEG_BIG)


def _key_block_loop(step, n, per_trip, smallest=1):
    def trip_body(jj, carry):
        for t in range(per_trip):
            step(per_trip * jj + t)
        return carry

    lax.fori_loop(0, n // per_trip, trip_body, 0)
    group = per_trip // 2
    while group >= smallest:
        def tail(group=group):
            first = (n // (2 * group)) * (2 * group)
            for t in range(group):
                step(first + t)

        pl.when((n // group) % 2 == 1)(tail)
        group //= 2


def _attention_fwd(q, k, v, h1, *, blk, bk, sub, per_trip):
    s = q.shape[0]
    nb = s // blk

    def body(q_ref, k_ref, v_ref, g_ref, o_ref, u_ref, lse_ref, vext_s, m_s, acc_s):
        i = pl.program_id(1)

        @pl.when(i == 0)
        def _():
            vext_s[:, 0:V_HEAD] = v_ref[...]
            vext_s[:, V_HEAD:2 * V_HEAD] = jnp.ones((s, V_HEAD), vext_s.dtype)

        m_s[...] = jnp.full_like(m_s, NEG_BIG)
        acc_s[...] = jnp.zeros_like(acc_s)

        def update(rows, kb, vb, row0):
            sc = _dot(q_ref[rows, :], kb, 1, 1)
            if row0 is not None:
                sc = _causal_mask(sc, row0)
            m_prev = m_s[rows, :]
            m_new = jnp.maximum(m_prev, jnp.max(sc, axis=-1, keepdims=True))
            p = jnp.exp2(sc - jnp.tile(m_new, (1, kb.shape[0] // LANES)))
            a = jnp.exp2(m_prev - m_new)
            acc_s[rows, :] = jnp.tile(a, (1, 2)) * acc_s[rows, :] + _dot(p.astype(BF16), vb, 1, 0)
            m_s[rows, :] = m_new

        def step(j):
            kv_rows = pl.ds(pl.multiple_of(j * bk, bk), bk)
            kb, vb = k_ref[kv_rows, :], vext_s[kv_rows, :]
            for r in range(blk // sub):
                update(slice(r * sub, (r + 1) * sub), kb, vb, None)

        _key_block_loop(step, i * (blk // bk), per_trip, smallest=blk // bk)
        for r in range(blk // sub):
            ncols = (r + 1) * sub
            kv_rows = pl.ds(pl.multiple_of(i * blk, blk), ncols)
            update(slice(r * sub, (r + 1) * sub), k_ref[kv_rows, :], vext_s[kv_rows, :], r * sub)
        acc = acc_s[...]
        l = acc[:, V_HEAD:2 * V_HEAD]
        o = acc[:, 0:V_HEAD] / l
        g = g_ref[...]
        o_ref[...] = o
        u_ref[...] = (o * (g * _sigmoid(g))).astype(u_ref.dtype)
        lse_ref[0] = (m_s[...] + jnp.log2(l))[:, 0:1]

    blk_idx = lambda h, i: (i, h)
    return pl.pallas_call(
        body, name="attention_fwd", grid=(H_B, nb),
        in_specs=[pl.BlockSpec((blk, QK_PAD), blk_idx), pl.BlockSpec((s, QK_PAD), lambda h, i: (0, h)),
                  pl.BlockSpec((s, V_HEAD), lambda h, i: (0, h)), pl.BlockSpec((blk, LANES), lambda h, i: (i, GATE_BLOCK0 + h))],
        out_specs=[pl.BlockSpec((blk, V_HEAD), blk_idx), pl.BlockSpec((blk, V_HEAD), blk_idx),
                   pl.BlockSpec((1, blk, 1), lambda h, i: (h, i, 0))],
        out_shape=[jax.ShapeDtypeStruct((s, WIDTH_B), F32), jax.ShapeDtypeStruct((s, WIDTH_B), BF16),
                   jax.ShapeDtypeStruct((H_B, s, 1), F32)],
        scratch_shapes=[pltpu.VMEM((s, 2 * V_HEAD), BF16), pltpu.VMEM((blk, LANES), F32), pltpu.VMEM((blk, 2 * V_HEAD), F32)],
        compiler_params=_params(("parallel", "arbitrary")),
    )(q, k, v, h1)


def _attention_bwd(q, k, v, du, o, h1, lse, cos, sa, sb, *, blk, bk, per_trip):
    s = q.shape[0]
    nb = s // blk

    def body(q_ref, k_ref, v_ref, du_ref, o_ref, g_ref, lse_ref, c_ref, sa_ref, sb_ref,
             dq_ref, dkn_ref, dkr_ref, dv_ref, dg_ref, lse_s, dl_s, do_s, dq_s):
        i = pl.program_id(1)
        g = g_ref[...]
        sg = _sigmoid(g)
        du_v, ov = du_ref[...], o_ref[...]
        do = du_v * (g * sg)
        dg_ref[...] = (du_v * ov * (sg * (1.0 + g * (1.0 - sg)))).astype(dg_ref.dtype)
        do_s[...] = do.astype(do_s.dtype)
        dl_s[...] = jnp.broadcast_to(jnp.sum(do * ov, axis=-1, keepdims=True), (blk, LANES))
        lse_s[...] = jnp.broadcast_to(lse_ref[0], (blk, LANES))
        dq_s[...] = jnp.zeros_like(dq_s)

        def products(rows, kv_rows, row0):
            qv, dov, kb = q_ref[rows, :], do_s[rows, :], k_ref[kv_rows, :]
            tile = (1, kb.shape[0] // LANES)
            sc = _dot(qv, kb, 1, 1)
            if row0 is not None:
                sc = _causal_mask(sc, row0)
            p = jnp.exp2(sc - jnp.tile(lse_s[rows, :], tile))
            dp = _dot(dov, v_ref[kv_rows, :], 1, 1)
            ds = (p * (dp - jnp.tile(dl_s[rows, :], tile))).astype(BF16)
            dq_s[rows, :] += _dot(ds, kb, 1, 0)
            return _dot(ds, qv, 0, 0), _dot(p.astype(BF16), dov, 0, 0)

        def put(kv_rows, dk_c, dv_c, first):
            if first:
                dkn_ref[kv_rows, :] = dk_c[:, 0:128]
                dkr_ref[kv_rows, :] = dk_c[:, 128:256]
                dv_ref[kv_rows, :] = dv_c
            else:
                dkn_ref[kv_rows, :] += dk_c[:, 0:128]
                dkr_ref[kv_rows, :] += dk_c[:, 128:256]
                dv_ref[kv_rows, :] += dv_c

        n_sub = blk // bk
        sub_rows = [slice(r * bk, (r + 1) * bk) for r in range(n_sub)]

        def step(j):
            kv_rows = pl.ds(pl.multiple_of(j * bk, bk), bk)
            for rows in sub_rows:
                dk_c, dv_c = products(rows, kv_rows, None)
                put(kv_rows, dk_c, dv_c, False)

        _key_block_loop(step, i * n_sub, per_trip, smallest=n_sub)
        for c in range(n_sub):
            kv_rows = pl.ds(pl.multiple_of(i * blk + c * bk, bk), bk)
            for r in range(c, n_sub):
                dk_c, dv_c = products(sub_rows[r], kv_rows, 0 if r == c else None)
                put(kv_rows, dk_c, dv_c, r == c)
        dq = dq_s[...] * ATT_SCALE
        dq_ref[:, 0:128] = dq[:, 0:128].astype(dq_ref.dtype)
        dq_ref[:, 128:256] = _rope_b(dq[:, 128:256], c_ref[...], sa_ref[...], sb_ref[...], -1.0).astype(dq_ref.dtype)

        @pl.when(i == nb - 1)
        def _():
            dkn_ref[...] = dkn_ref[...] * LN2
            dkr_ref[...] = dkr_ref[...] * LN2

    head = lambda h, i: (0, h)
    blk_idx = lambda h, i: (i, h)
    row_idx = lambda h, i: (i, 0)
    return pl.pallas_call(
        body, name="attention_bwd", grid=(H_B, nb),
        in_specs=[pl.BlockSpec((blk, QK_PAD), blk_idx), pl.BlockSpec((s, QK_PAD), head), pl.BlockSpec((s, V_HEAD), head),
                  pl.BlockSpec((blk, V_HEAD), blk_idx), pl.BlockSpec((blk, V_HEAD), blk_idx),
                  pl.BlockSpec((blk, LANES), lambda h, i: (i, GATE_BLOCK0 + h)), pl.BlockSpec((1, blk, 1), lambda h, i: (h, i, 0)),
                  pl.BlockSpec((blk, LANES), row_idx), pl.BlockSpec((blk, LANES), row_idx), pl.BlockSpec((blk, LANES), row_idx)],
        out_specs=[pl.BlockSpec((blk, QK_PAD), blk_idx), pl.BlockSpec((s, 128), head), pl.BlockSpec((s, 128), head),
                   pl.BlockSpec((s, 128), head), pl.BlockSpec((blk, V_HEAD), blk_idx)],
        out_shape=[jax.ShapeDtypeStruct((s, H_B * QK_PAD), BF16), jax.ShapeDtypeStruct((s, H_B * 128), F32),
                   jax.ShapeDtypeStruct((s, H_B * 128), F32), jax.ShapeDtypeStruct((s, H_B * 128), F32),
                   jax.ShapeDtypeStruct((s, WIDTH_B), BF16)],
        scratch_shapes=[pltpu.VMEM((blk, LANES), F32), pltpu.VMEM((blk, LANES), F32), pltpu.VMEM((blk, V_HEAD), BF16),
                        pltpu.VMEM((blk, QK_PAD), F32)],
        compiler_params=_params(("parallel", "arbitrary")),
    )(q, k, v, du, o, h1, lse, cos, sa, sb)


def _local_step(x, target, w, kv_norm, q_norm, ln_g, ln_b, *, ts=256, blk=512, late=None, reduce=None):
    s = x.shape[0]
    cos_a, sin_a = _rope_tables_a(s)
    cos_b, sa_b, sb_b = _rope_tables_b(s)
    tables = _retention_tables()
    g0, g1, b0, b1 = ln_g[0:1], ln_g[1:2], ln_b[0:1], ln_b[1:2]

    x16 = x.astype(BF16)
    if late is None:
        h_a = _mm(x16, w["a_in"], tn=1536, name="a_in_fwd")
    else:
        flat_b, chip = late
        h_a, got = _mm(x16, w["a_in"], tn=1536, name="a_in_fwd", side=_gather_side(flat_b))
        got = lax.dynamic_update_slice(got, flat_b[None], (chip, 0, 0))
        w = {**w, **_kernel_layout_b(_full_from_gathered(got, B_SHARDS))}
    q_a, k_a, v_a, o_a, u_a, states = _retention_fwd(h_a, cos_a, sin_a, tables)
    y_a = _mm(u_a, w["a_out"], tn=1024, name="a_out_fwd")
    x1, x1_16 = _ln_fwd(x, y_a, g0, b0, ts=ts)

    h1 = _mm(x1_16, w["b_in1"], tn=1152, name="b_in_fwd")
    lat16, kr16, qn16 = _kvq_prep(h1, kv_norm, q_norm, cos_b, sa_b, sb_b, ts=ts)
    k16 = _mm(lat16, w["up_k"], out_dtype=BF16, tn=2048, out_tn=H_B * QK_PAD, extras=(kr16,), epilogue=_assemble_k_store, name="up_k_fwd")
    v16 = _mm(lat16, w["up_v"], out_dtype=BF16, tn=2048, name="up_v_fwd")
    q16 = _mm(qn16, w["uq"], out_dtype=BF16, tn=2048, extras=(cos_b, sa_b, sb_b), epilogue=_rope_q_store, name="uq_fwd")
    o_b, u_b, lse = _attention_fwd(q16, k16, v16, h1, blk=2 * blk, bk=blk, sub=blk // 2, per_trip=4)
    y_b = _mm(u_b, w["b_out"], tn=1024, name="b_out_fwd")

    dz_b, dz_b16, dg1, db1, loss = _ln_loss_bwd(x1, y_b, target, g1, b1, ts=ts)
    d_b_out = _mm(u_b, dz_b16, ta=True, tn=1024, tk=1024, name="b_out_dw")
    du_b = _mm(dz_b16, w["b_out"], tb=True, tn=1024, name="b_out_dx")
    dqf16, dkn, dkr_heads, dv, dgate16 = _attention_bwd(q16, k16, v16, du_b, o_b, h1, lse, cos_b, sa_b, sb_b, blk=2 * blk, bk=blk, per_trip=4)
    d_uq = _mm(qn16, dqf16, ta=True, tm=768, tn=2048, tk=1024, name="uq_dw")
    dqn = _mm(dqf16, w["uq"], tb=True, tn=768, tk=2048, name="uq_dx")
    d_up_k = _mm(lat16, dkn, ta=True, tn=2048, tk=1024, name="up_k_dw")
    d_up_v = _mm(lat16, dv, ta=True, tn=2048, tk=1024, name="up_v_dw")
    dlat_k = _mm(dkn, w["up_k"], tb=True, tn=512, name="up_k_dx")
    dlat_v = _mm(dv, w["up_v"], tb=True, tn=512, name="up_v_dx")
    dh1, dkvn, dqnorm = _h1_bwd(h1, dlat_k, dlat_v, dkr_heads, dqn, dgate16, kv_norm, q_norm, cos_b, sa_b, sb_b, ts=ts)
    d_b_in1 = _mm(x1_16, dh1, ta=True, tn=1152, tk=1024, name="b_in_dw")
    dx1 = _mm(dh1, w["b_in1"], tb=True, tn=1024, extras=(dz_b,), epilogue=_residual_store, name="b_in_dx")

    dz_a, dz_a16, dg0, db0 = _ln_bwd_call(dx1, x, y_a, g0, ts=ts)
    d_a_out = _mm(u_a, dz_a16, ta=True, tn=1024, tk=1024, name="a_out_dw")
    du_a = _mm(dz_a16, w["a_out"], tb=True, tn=1024, name="a_out_dx")
    dh_a = _retention_bwd(q_a, k_a, v_a, states, du_a, o_a, h_a, cos_a, sin_a, tables)
    grads = dict(a_out=d_a_out, b_in1=d_b_in1, uq=d_uq, b_out=d_b_out, up_k=d_up_k, up_v=d_up_v)
    small = dict(ln_g=jnp.concatenate([dg0, dg1], axis=0), ln_b=jnp.concatenate([db0, db1], axis=0),
                 q_norm=dqnorm, kv_norm=dkvn)
    if reduce is None:
        grads["a_in"] = _mm(x16, dh_a, ta=True, tn=1536, tk=1024, name="a_in_dw")
        grad_x = _mm(dh_a, w["a_in"], tb=True, tn=1024, tk=2048, extras=(dz_a,), epilogue=_residual_store, name="a_in_dx")
        return loss, grad_x, grads, small
    own_early, travel_early = reduce(_reference_layout_grads(grads), EARLY_SHARDS, EARLY_ROWS, "early")
    d_a_in, got_early = _mm(x16, dh_a, ta=True, tn=1536, tk=1024, name="a_in_dw", side=_chip_exchange_side(travel_early))
    own_late, travel_late = reduce(dict(a_w_in=d_a_in), LATE_SHARDS, LATE_ROWS, "late")
    grad_x, got_late = _mm(dh_a, w["a_in"], tb=True, tn=1024, tk=2048, extras=(dz_a,), epilogue=_residual_store, name="a_in_dx",
                           side=_chip_exchange_side(travel_late))
    return loss, grad_x, ((own_early, got_early), (own_late, got_late)), small


def _flat_shards(shards, dtype, which, total_rows):
    parts = [shards[name].reshape(rows, FLAT_COLS) for name, rows in which]
    used = sum(rows for _, rows in which)
    parts.append(jnp.zeros((total_rows - used, FLAT_COLS), parts[0].dtype))
    return jnp.concatenate(parts, axis=0).astype(dtype)


def _unflat_shards(flat, shapes, shards):
    out, off = {}, 0
    for name, rows in shards:
        out[name] = flat[off:off + rows].reshape(shapes[name])
        off += rows
    return out


COL_SHARDED = {"a_w_in": (D_MODEL, IN_A), "b_w_in": (D_MODEL, IN_B), "b_w_uq": (Q_LORA, H_B * (QK_NOPE + QK_ROPE)),
               "kv_w_up": (KV_LORA, H_B * (QK_NOPE + V_HEAD))}
ROW_SHARDED = {"a_w_out": (WIDTH_A, D_MODEL), "b_w_out": (WIDTH_B, D_MODEL), "kv_w_down": (D_MODEL, KV_LORA + QK_ROPE)}


def _full_from_gathered(gathered, shards=SHARD_ROWS):
    out, off = {}, 0
    for name, rows in shards:
        part = gathered[:, off:off + rows]
        off += rows
        if name in COL_SHARDED:
            r, c = COL_SHARDED[name]
            out[name] = part.reshape(N_CHIPS, r, c // N_CHIPS).transpose(1, 0, 2).reshape(r, c)
        else:
            r, c = ROW_SHARDED[name]
            out[name] = part.reshape(r, c)
    return out


def _chip_major(g):
    r, c = g.shape
    return g.reshape(r, N_CHIPS, c // N_CHIPS).transpose(1, 0, 2)


def _gathered_from_full(full, shards, total_rows):
    if len(shards) == 1 and shards[0][0] in COL_SHARDED:
        return _chip_major(full[shards[0][0]])
    parts = []
    for name, rows in shards:
        g = full[name]
        if name in COL_SHARDED:
            r, c = COL_SHARDED[name]
            g = g.reshape(r, N_CHIPS, c // N_CHIPS).transpose(1, 0, 2)
        parts.append(g.reshape(N_CHIPS, rows, FLAT_COLS))
    used = sum(rows for _, rows in shards)
    if total_rows > used:
        parts.append(jnp.zeros((N_CHIPS, total_rows - used, FLAT_COLS), F32))
    return jnp.concatenate(parts, axis=1)


A_SHARDS, B_SHARDS = SHARD_ROWS[:1], SHARD_ROWS[1:]
A_ROWS = sum(rows for _, rows in A_SHARDS)


def _kernel_layout_a(full):
    return dict(a_in=full["a_w_in"])


def _kernel_layout_b(full):
    uq = full["b_w_uq"].reshape(Q_LORA, H_B, QK_NOPE + QK_ROPE)
    uq = jnp.pad(uq, ((0, 0), (0, 0), (0, QK_PAD - QK_NOPE - QK_ROPE))).reshape(Q_LORA, H_B * QK_PAD)
    up = full["kv_w_up"].reshape(KV_LORA, H_B, QK_NOPE + V_HEAD)
    down = jnp.pad(full["kv_w_down"], ((0, 0), (0, KV_DOWN_PAD - KV_LORA - QK_ROPE)))
    return dict(a_out=full["a_w_out"], b_out=full["b_w_out"], uq=uq,
                up_k=up[:, :, :QK_NOPE].reshape(KV_LORA, H_B * QK_NOPE),
                up_v=up[:, :, QK_NOPE:].reshape(KV_LORA, H_B * V_HEAD),
                b_in1=jnp.concatenate([down, full["b_w_in"]], axis=1))


def _kernel_layout(full):
    return {**_kernel_layout_a(full), **_kernel_layout_b(full)}


EARLY_SHARDS = tuple(sh for sh in SHARD_ROWS if sh[0] != "a_w_in")
LATE_SHARDS = tuple(sh for sh in SHARD_ROWS if sh[0] == "a_w_in")
EARLY_ROWS, LATE_ROWS = 3072, 1536


def _reference_layout_grads(g):
    uq = g["uq"].reshape(Q_LORA, H_B, QK_PAD)[:, :, :QK_NOPE + QK_ROPE].reshape(Q_LORA, H_B * (QK_NOPE + QK_ROPE))
    up = jnp.concatenate([g["up_k"].reshape(KV_LORA, H_B, QK_NOPE), g["up_v"].reshape(KV_LORA, H_B, V_HEAD)], axis=2)
    return dict(a_w_out=g["a_out"], b_w_out=g["b_out"], b_w_uq=uq,
                kv_w_up=up.reshape(KV_LORA, H_B * (QK_NOPE + V_HEAD)),
                kv_w_down=g["b_in1"][:, :KV_LORA + QK_ROPE], b_w_in=g["b_in1"][:, KV_DOWN_PAD:])


HBM_SPEC = pl.BlockSpec(memory_space=pl.ANY)


def _me():
    return lax.axis_index("x"), lax.axis_index("y"), lax.axis_index("c")


def _chip_flips(x, y):
    return [(1 - x, y), (x, 1 - y), (1 - x, 1 - y)]


def _gather_copies(src_ref, out_ref, send_sems, recv_sems):
    x, y, c = _me()
    half = src_ref.shape[0] // 2
    my_rows = pl.ds(pl.multiple_of(c * half, 16), half)
    their_rows = pl.ds(pl.multiple_of((1 - c) * half, 16), half)
    chips = _chip_flips(x, y)
    sibling = (x, y, 1 - c)

    def copy(k, src, dst, to):
        return pltpu.make_async_remote_copy(src_ref=src, dst_ref=dst, send_sem=send_sems.at[k], recv_sem=recv_sems.at[k],
                                            device_id=to, device_id_type=MESH)

    sends = [copy(k, src_ref.at[my_rows, :], out_ref.at[2 * x + y, my_rows, :], (px, py, c)) for k, (px, py) in enumerate(chips)]
    landed = [out_ref.at[2 * px + py, my_rows, :] for px, py in chips]
    lands = [copy(k, landed[k], landed[k], (px, py, c)) for k, (px, py) in enumerate(chips)]
    forwards = [copy(3 + k, landed[k], landed[k], sibling) for k in range(3)]
    theirs = [out_ref.at[2 * px + py, their_rows, :] for px, py in chips]
    arrivals = [copy(3 + k, theirs[k], theirs[k], sibling) for k in range(3)]
    return sends, lands, forwards, arrivals


def _gather_start(src_ref, out_ref, send_sems, recv_sems):
    sends, _, _, _ = _gather_copies(src_ref, out_ref, send_sems, recv_sems)
    for cp in sends:
        cp.start()


def _gather_finish(src_ref, out_ref, send_sems, recv_sems):
    sends, lands, forwards, arrivals = _gather_copies(src_ref, out_ref, send_sems, recv_sems)
    for k in range(3):
        lands[k].wait_recv()
        forwards[k].start()
    for cp in arrivals:
        cp.wait_recv()
    for cp in sends + forwards:
        cp.wait_send()


def _gather_scratch():
    return [pltpu.SemaphoreType.DMA((6,)), pltpu.SemaphoreType.DMA((6,))]


def _gather_weights(flat16):
    def body(src_ref, out_ref, send_sems, recv_sems):
        _gather_start(src_ref, out_ref, send_sems, recv_sems)
        _gather_finish(src_ref, out_ref, send_sems, recv_sems)

    return pl.pallas_call(
        body, name="gather_weights",
        in_specs=[HBM_SPEC], out_specs=HBM_SPEC,
        out_shape=jax.ShapeDtypeStruct((N_CHIPS,) + flat16.shape, flat16.dtype),
        scratch_shapes=_gather_scratch(),
    )(flat16)


def _gather_side(flat16):
    return dict(inputs=[flat16], out_shape=jax.ShapeDtypeStruct((N_CHIPS,) + flat16.shape, flat16.dtype),
                scratch=_gather_scratch(), start=_gather_start, finish=_gather_finish)


def _chip_exchange_copies(p_ref, out_ref, send_sems, recv_sems):
    x, y, c = _me()
    return [pltpu.make_async_remote_copy(
        src_ref=p_ref.at[2 * px + py], dst_ref=out_ref.at[k], send_sem=send_sems.at[k], recv_sem=recv_sems.at[k],
        device_id=(px, py, c), device_id_type=MESH) for k, (px, py) in enumerate(_chip_flips(x, y))]


def _chip_exchange_start(p_ref, out_ref, send_sems, recv_sems):
    for cp in _chip_exchange_copies(p_ref, out_ref, send_sems, recv_sems):
        cp.start()


def _chip_exchange_finish(p_ref, out_ref, send_sems, recv_sems):
    copies = _chip_exchange_copies(p_ref, out_ref, send_sems, recv_sems)
    for cp in copies:
        cp.wait_send()
    for cp in copies:
        cp.wait_recv()


def _chip_exchange_side(p):
    return dict(inputs=[p], out_shape=jax.ShapeDtypeStruct((3,) + p.shape[1:], p.dtype),
                scratch=[pltpu.SemaphoreType.DMA((3,)), pltpu.SemaphoreType.DMA((3,))],
                start=_chip_exchange_start, finish=_chip_exchange_finish)


def _pair_swap(r, name):
    def body(r_ref, out_ref, send_sem, recv_sem):
        x, y, c = _me()
        cp = pltpu.make_async_remote_copy(src_ref=r_ref, dst_ref=out_ref, send_sem=send_sem, recv_sem=recv_sem,
                                          device_id=(x, y, 1 - c), device_id_type=MESH)
        cp.start()
        cp.wait_send()
        cp.wait_recv()

    return pl.pallas_call(
        body, name=name,
        in_specs=[HBM_SPEC], out_specs=HBM_SPEC,
        out_shape=jax.ShapeDtypeStruct(r.shape, r.dtype),
        scratch_shapes=[pltpu.SemaphoreType.DMA, pltpu.SemaphoreType.DMA],
    )(r)


def _sum_small(vec):
    def body(v_ref, out_ref, slots, send_sems, recv_sems):
        x, y, c = _me()
        me = 4 * x + 2 * y + c
        slots[me] = v_ref[...]
        flips = [(fx, fy, fc) for fx in (0, 1) for fy in (0, 1) for fc in (0, 1)][1:]
        copies = []
        for k, (fx, fy, fc) in enumerate(flips):
            copies.append(pltpu.make_async_remote_copy(
                src_ref=v_ref, dst_ref=slots.at[me], send_sem=send_sems.at[k], recv_sem=recv_sems.at[k],
                device_id=(x ^ fx, y ^ fy, c ^ fc), device_id_type=MESH))
        for cp in copies:
            cp.start()
        for cp in copies:
            cp.wait_send()
        for k, (fx, fy, fc) in enumerate(flips):
            src = 4 * (x ^ fx) + 2 * (y ^ fy) + (c ^ fc)
            pltpu.make_async_remote_copy(
                src_ref=v_ref, dst_ref=slots.at[src], send_sem=send_sems.at[k], recv_sem=recv_sems.at[k],
                device_id=(x ^ fx, y ^ fy, c ^ fc), device_id_type=MESH).wait_recv()
        total = slots[0]
        for d in range(1, N_DEV):
            total = total + slots[d]
        out_ref[...] = total

    return pl.pallas_call(
        body, name="sum_small",
        in_specs=[pl.BlockSpec(memory_space=pltpu.VMEM)], out_specs=pl.BlockSpec(memory_space=pltpu.VMEM),
        out_shape=jax.ShapeDtypeStruct(vec.shape, vec.dtype),
        scratch_shapes=[pltpu.VMEM((N_DEV,) + vec.shape, vec.dtype), pltpu.SemaphoreType.DMA((7,)),
                        pltpu.SemaphoreType.DMA((7,))],
    )(vec)


UPD_ROWS = 256


def _pair_sum(g, theirs, core, chip, name):
    half, cols = theirs.shape[1:]
    nb = half // UPD_ROWS

    def body(core_ref, chip_ref, g_ref, t_ref, own_ref, o16_ref):
        total = g_ref[0] + t_ref[0].astype(F32)
        o16_ref[0] = total.astype(o16_ref.dtype)

        @pl.when(pl.program_id(1) == chip_ref[0])
        def _():
            own_ref[...] = total

    return pl.pallas_call(
        body, name=name,
        grid_spec=pltpu.PrefetchScalarGridSpec(
            num_scalar_prefetch=2, grid=(nb, N_CHIPS),
            in_specs=[pl.BlockSpec((1, UPD_ROWS, cols), lambda i, d, core_ref, chip_ref: (d, core_ref[0] * nb + i, 0)),
                      pl.BlockSpec((1, UPD_ROWS, cols), lambda i, d, core_ref, chip_ref: (d, i, 0))],
            out_specs=[pl.BlockSpec((UPD_ROWS, cols), lambda i, d, core_ref, chip_ref: (i, 0)),
                       pl.BlockSpec((1, UPD_ROWS, cols), lambda i, d, core_ref, chip_ref: (d, i, 0))]),
        out_shape=[jax.ShapeDtypeStruct((half, cols), F32),
                   jax.ShapeDtypeStruct((N_CHIPS, half, cols), BF16)],
        compiler_params=_params(("parallel", "arbitrary")),
    )(core, chip, g, theirs)


def _chip_sum(own, received, name):
    half, cols = own.shape
    nb = half // UPD_ROWS

    def body(p_ref, r_ref, o_ref):
        o_ref[...] = ((p_ref[...] + r_ref[0].astype(F32)) + r_ref[1].astype(F32)) + r_ref[2].astype(F32)

    return pl.pallas_call(
        body, name=name, grid=(nb,),
        in_specs=[pl.BlockSpec((UPD_ROWS, cols), lambda i: (i, 0)),
                  pl.BlockSpec((3, UPD_ROWS, cols), lambda i: (0, i, 0))],
        out_specs=pl.BlockSpec((UPD_ROWS, cols), lambda i: (i, 0)),
        out_shape=jax.ShapeDtypeStruct((half, cols), F32),
        compiler_params=_params(("parallel",)),
    )(own, received)


def _adamw(w, g, m, v, *, rows, name):
    r, c = w.shape
    rows = min(rows, r)
    assert r % rows == 0

    def body(w_ref, g_ref, m_ref, v_ref, d_ref, nm_ref, nv_ref):
        gv = g_ref[...]
        nm = ADAM_B1 * m_ref[...] + (1.0 - ADAM_B1) * gv
        nv = ADAM_B2 * v_ref[...] + (1.0 - ADAM_B2) * (gv * gv)
        m_hat = nm / (1.0 - ADAM_B1 ** ADAM_STEP)
        v_hat = nv / (1.0 - ADAM_B2 ** ADAM_STEP)
        d_ref[...] = -ADAM_LR * (m_hat / (jnp.sqrt(v_hat) + ADAM_EPS) + ADAM_WD * w_ref[...])
        nm_ref[...] = nm
        nv_ref[...] = nv

    spec = pl.BlockSpec((rows, c), lambda i: (i, 0))
    return pl.pallas_call(
        body, name=name, grid=(r // rows,),
        in_specs=[spec] * 4, out_specs=[spec] * 3,
        out_shape=[jax.ShapeDtypeStruct((r, c), F32)] * 3,
        compiler_params=_params(("parallel",)),
    )(w, g, m, v)


W_NAMES = ("a_w_in", "a_w_out", "b_w_in", "b_q_norm", "b_w_uq", "b_w_out", "kv_w_down", "kv_norm", "kv_w_up", "ln_g", "ln_b")
BIG = tuple(name for name, _ in SHARD_ROWS)


def _pack_small(ln_g, ln_b, q_norm, kv_norm, extra=None):
    pad = lambda a: jnp.pad(a.reshape(1, -1), ((0, 0), (0, FLAT_COLS - a.size)))
    rows = [ln_g, ln_b, pad(q_norm), pad(kv_norm),
            jnp.zeros((1, FLAT_COLS), F32) if extra is None else pad(extra), jnp.zeros((1, FLAT_COLS), F32)]
    return jnp.concatenate(rows, axis=0)


def _unpack_small(p):
    return dict(ln_g=p[0:2], ln_b=p[2:4], b_q_norm=p[4:5, :Q_LORA], kv_norm=p[5, :KV_LORA])


def kernel(x, a_w_in, a_w_out, b_w_in, b_q_norm, b_w_uq, b_w_out, kv_w_down, kv_norm, kv_w_up, ln_g, ln_b, loss_target, m_a_w_in, m_a_w_out, m_b_w_in, m_b_q_norm, m_b_w_uq, m_b_w_out, m_kv_w_down, m_kv_norm, m_kv_w_up, m_ln_g, m_ln_b, v_a_w_in, v_a_w_out, v_b_w_in, v_b_q_norm, v_b_w_uq, v_b_w_out, v_kv_w_down, v_kv_norm, v_kv_w_up, v_ln_g, v_ln_b):
    w_in = dict(a_w_in=a_w_in[0], a_w_out=a_w_out[0], b_w_in=b_w_in[0], b_w_uq=b_w_uq[0], b_w_out=b_w_out[0],
                kv_w_down=kv_w_down, kv_w_up=kv_w_up)
    m_in = dict(a_w_in=m_a_w_in[0], a_w_out=m_a_w_out[0], b_w_in=m_b_w_in[0], b_w_uq=m_b_w_uq[0], b_w_out=m_b_w_out[0],
                kv_w_down=m_kv_w_down, kv_w_up=m_kv_w_up)
    v_in = dict(a_w_in=v_a_w_in[0], a_w_out=v_a_w_out[0], b_w_in=v_b_w_in[0], b_w_uq=v_b_w_uq[0], b_w_out=v_b_w_out[0],
                kv_w_down=v_kv_w_down, kv_w_up=v_kv_w_up)
    shard_shapes = {name: w_in[name].shape for name in BIG}

    cx, cy, cc = lax.axis_index("x"), lax.axis_index("y"), lax.axis_index("c")
    chip = 2 * cx + cy
    flat_b = _flat_shards(w_in, BF16, B_SHARDS, FLAT_ROWS - A_ROWS)
    a16 = w_in["a_w_in"].astype(BF16)
    got_a = lax.dynamic_update_slice(_gather_weights(a16), a16[None], (chip, 0, 0))
    weights_a = dict(a_in=got_a.transpose(1, 0, 2).reshape(D_MODEL, IN_A))

    core_arr, chip_arr = cc.astype(jnp.int32).reshape(1), chip.astype(jnp.int32).reshape(1)

    def reduce_pair(full, shards, rows, tag):
        g_all = _gathered_from_full(full, shards, rows)
        half, cols = g_all.shape[1] // 2, g_all.shape[2]
        other_half = lax.dynamic_slice(g_all, (0, (1 - cc) * half, 0), (N_CHIPS, half, cols)).astype(BF16)
        theirs = _pair_swap(other_half, "pair_exchange_" + tag)
        return _pair_sum(g_all, theirs, core_arr, chip_arr, "pair_sum_" + tag)

    loss, grad_x, reduced, small = _local_step(x[0], loss_target[0], weights_a, kv_norm.reshape(1, -1), b_q_norm, ln_g, ln_b,
                                               late=(flat_b, chip), reduce=reduce_pair)

    g_big = {}
    for (own, received), shards, tag in zip(reduced, (EARLY_SHARDS, LATE_SHARDS), ("early", "late")):
        mine = _chip_sum(own, received, "chip_sum_" + tag)
        sibling = _pair_swap(mine, "pair_share_" + tag)
        g_flat = jnp.concatenate([jnp.where(cc == 0, mine, sibling), jnp.where(cc == 0, sibling, mine)], axis=0)
        if g_flat.shape == shard_shapes[shards[0][0]]:
            g_big[shards[0][0]] = g_flat
        else:
            g_big.update(_unflat_shards(g_flat, shard_shapes, shards))

    small_sum = _sum_small(_pack_small(small["ln_g"], small["ln_b"], small["q_norm"], small["kv_norm"], loss[:, :1]))
    loss_out = small_sum[6, 0]

    row = lambda a: a.reshape(1, -1)
    g_all = {**g_big, **_unpack_small(small_sum)}
    g_all["kv_norm"] = row(g_all["kv_norm"])
    state = {**{name: (w_in[name], m_in[name], v_in[name]) for name in BIG},
             "ln_g": (ln_g, m_ln_g, v_ln_g), "ln_b": (ln_b, m_ln_b, v_ln_b), "b_q_norm": (b_q_norm, m_b_q_norm, v_b_q_norm),
             "kv_norm": (row(kv_norm), row(m_kv_norm), row(v_kv_norm))}
    upd = {name: _adamw(state[name][0], g_all[name], state[name][1], state[name][2], rows=256, name="adamw_" + name)
           for name in W_NAMES}

    def shaped(name, a):
        if name == "kv_norm":
            return a.reshape(-1)
        return a[None] if name in ("a_w_in", "a_w_out", "b_w_in", "b_w_uq", "b_w_out") else a

    outputs = [[shaped(name, g_all[name]) for name in W_NAMES]]
    outputs += [[shaped(name, upd[name][k]) for name in W_NAMES] for k in range(3)]
    return (loss_out, grad_x[None], *outputs[0], *outputs[1], *outputs[2], *outputs[3])
```

```python
import functools
import math

import jax
import jax.numpy as jnp
from jax import lax
from jax.experimental import pallas as pl
from jax.experimental.pallas import tpu as pltpu

F32 = jnp.float32
BF16 = jnp.bfloat16
MESH = pl.DeviceIdType.MESH

D_MODEL = 1024
DEPTH = 2
H_A, DK_A, DV_A = 4, 256, 512
WIDTH_A = H_A * DV_A
CHUNK = 128
H_B, QK_NOPE, QK_ROPE, V_HEAD = 16, 128, 64, 128
QK_PAD = 256
Q_LORA, KV_LORA = 768, 512
KV_DOWN_PAD = 640
WIDTH_B = H_B * V_HEAD
IN_A = 2 * H_A * DK_A + 2 * WIDTH_A
IN_B = Q_LORA + WIDTH_B
H1_B = KV_DOWN_PAD + IN_B
ROPE_BASE = 10000.0
ALPHA = (2.0 * DEPTH) ** 0.25
ATT_SCALE = (QK_NOPE + QK_ROPE) ** -0.5
NEG_BIG = -1e30

ADAM_LR, ADAM_B1, ADAM_B2, ADAM_EPS, ADAM_WD, ADAM_STEP = 0.001, 0.9, 0.999, 1e-08, 0.01, 10

VMEM_LIMIT_BYTES = 56 * 1024 * 1024
LANES = 128
FLAT_COLS = 1024
SHARD_ROWS = (("a_w_in", 1536), ("a_w_out", 512), ("b_w_in", 704), ("b_w_uq", 576), ("b_w_out", 512),
              ("kv_w_down", 144), ("kv_w_up", 512))
FLAT_ROWS = 4608
N_CHIPS = 4
N_DEV = 8


def _params(sem, vmem=VMEM_LIMIT_BYTES):
    return pltpu.CompilerParams(dimension_semantics=sem, vmem_limit_bytes=vmem)


def _row_spec(ts, w, col_block=0):
    return pl.BlockSpec((ts, w), lambda i: (i, col_block))


def _bc_spec(shape):
    nd = len(shape)
    return pl.BlockSpec(shape, lambda i: (0,) * nd)


def _sigmoid(x):
    return 1.0 / (1.0 + jnp.exp(-x))


def _fold8(v):
    ts, w = v.shape
    return jnp.sum(v.reshape(ts // 8, 8, w), axis=0)


def _mm(a, b, *, ta=False, tb=False, out_dtype=F32, tm=1024, tn=512, tk=None, name, extras=(), epilogue=None, out_tn=None,
        side=None):
    if ta:
        K, M = a.shape
    else:
        M, K = a.shape
    if tb:
        N, Kb = b.shape
    else:
        Kb, N = b.shape
    assert K == Kb, (a.shape, b.shape)
    tm, tn = min(tm, M), min(tn, N)
    tk = K if tk is None else min(tk, K)
    assert M % tm == 0 and N % tn == 0 and K % tk == 0, (name, M, N, K, tm, tn, tk)
    grid = (M // tm, N // tn, K // tk)
    nk = grid[2]
    out_tn = tn if out_tn is None else out_tn
    n_extra = len(extras)
    side_inputs = [] if side is None else list(side["inputs"])
    n_side = len(side_inputs)
    n_acc = 0 if nk == 1 else 1
    dims = (((0,) if ta else (1,), (1,) if tb else (0,)), ((), ()))

    def body(a_ref, b_ref, *rest):
        extra_refs, rest = rest[:n_extra], rest[n_extra:]
        side_in, o_ref, rest = rest[:n_side], rest[n_side], rest[n_side + 1:]
        if side is not None:
            side_refs, rest = side_in + rest[:1] + rest[1 + n_acc:], rest[1:]
            ids = [pl.program_id(d) for d in range(3)]

            @pl.when((ids[0] == 0) & (ids[1] == 0) & (ids[2] == 0))
            def _():
                side["start"](*side_refs)

        prod = lax.dot_general(a_ref[...].astype(BF16), b_ref[...].astype(BF16), dims,
                               preferred_element_type=F32)

        def store(tile):
            if epilogue is None:
                o_ref[...] = tile.astype(o_ref.dtype)
            else:
                epilogue(tile, o_ref, *extra_refs)

        if nk == 1:
            store(prod)
        else:
            acc = rest[0]
            k = pl.program_id(2)

            @pl.when(k == 0)
            def _():
                acc[...] = prod

            @pl.when(k > 0)
            def _():
                acc[...] += prod

            @pl.when(k == nk - 1)
            def _():
                store(acc[...])

        if side is not None:
            @pl.when((ids[0] == grid[0] - 1) & (ids[1] == grid[1] - 1) & (ids[2] == grid[2] - 1))
            def _():
                side["finish"](*side_refs)

    a_spec = pl.BlockSpec((tk, tm), lambda i, j, k: (k, i)) if ta else pl.BlockSpec((tm, tk), lambda i, j, k: (i, k))
    b_spec = pl.BlockSpec((tn, tk), lambda i, j, k: (j, k)) if tb else pl.BlockSpec((tk, tn), lambda i, j, k: (k, j))
    extra_specs = [pl.BlockSpec((tm, e.shape[1]), lambda i, j, k: (i, 0)) for e in extras]
    out_specs = pl.BlockSpec((tm, out_tn), lambda i, j, k: (i, j))
    out_shape = jax.ShapeDtypeStruct((M, (N // tn) * out_tn), out_dtype)
    scratch = [] if nk == 1 else [pltpu.VMEM((tm, tn), F32)]
    if side is not None:
        out_specs, out_shape, scratch = [out_specs, HBM_SPEC], [out_shape, side["out_shape"]], scratch + list(side["scratch"])
    return pl.pallas_call(
        body, name=name, grid=grid,
        in_specs=[a_spec, b_spec] + extra_specs + [HBM_SPEC] * n_side,
        out_specs=out_specs, out_shape=out_shape, scratch_shapes=scratch,
        compiler_params=_params(("parallel", "parallel", "arbitrary") if side is None else ("arbitrary",) * 3),
    )(a, b, *extras, *side_inputs)


def _rope_tables_a(s):
    half = DK_A // 2
    inv = ROPE_BASE ** (-jnp.arange(half, dtype=F32) / half)
    ang = jnp.arange(s, dtype=F32)[:, None] * inv[None, :]
    return jnp.cos(ang), jnp.sin(ang)


def _rope_tables_b(s):
    half = QK_ROPE // 2
    inv = ROPE_BASE ** (-jnp.arange(half, dtype=F32) / half)
    ang = jnp.arange(s, dtype=F32)[:, None] * inv[None, :]
    c, sn = jnp.cos(ang), jnp.sin(ang)
    z = jnp.zeros_like(c)
    cos = jnp.concatenate([c, c, z, z], axis=1)
    sa = jnp.concatenate([-sn, z, z, z], axis=1)
    sb = jnp.concatenate([z, sn, z, z], axis=1)
    return cos, sa, sb


def _rope_b(r, cos, sa, sb, sign):
    return r * cos + sign * (pltpu.roll(r, 96, 1) * sa + pltpu.roll(r, 32, 1) * sb)


def _retention_tables():
    lg = jnp.log1p(-jnp.exp2(-5.0 - jnp.arange(H_A, dtype=F32)))
    idx = jnp.arange(CHUNK, dtype=F32)
    diff = idx[:, None] - idx[None, :]
    causal = diff >= 0
    dmat = jnp.where(causal, jnp.exp(jnp.where(causal, diff, 0.0)[None] * lg[:, None, None]), 0.0)
    qdec = jnp.exp((idx + 1.0)[None, :] * lg[:, None])[:, :, None]
    kdec = jnp.exp((CHUNK - 1.0 - idx)[None, :] * lg[:, None])[:, :, None]
    cdec = jnp.broadcast_to(jnp.exp(CHUNK * lg)[:, None, None], (H_A, 1, DV_A))
    return dmat, qdec, kdec, cdec


def _group_norm(o):
    mu = jnp.mean(o, axis=-1, keepdims=True)
    oc = o - mu
    var = jnp.mean(oc * oc, axis=-1, keepdims=True)
    rstd = lax.rsqrt(var + 1e-5)
    return oc * rstd, rstd


Q_COL, K_COL, V_COL, GATE_COL = 0, H_A * DK_A, 2 * H_A * DK_A, 2 * H_A * DK_A + WIDTH_A


def _ln_stats(z):
    mu = jnp.mean(z, axis=-1, keepdims=True)
    zc = z - mu
    var = jnp.mean(zc * zc, axis=-1, keepdims=True)
    rstd = lax.rsqrt(var + 1e-5)
    return zc * rstd, rstd


def _ln_bwd(dy, xhat, rstd, g):
    dxh = dy * g
    m1 = jnp.mean(dxh, axis=-1, keepdims=True)
    m2 = jnp.mean(dxh * xhat, axis=-1, keepdims=True)
    return rstd * (dxh - m1 - xhat * m2)


def _ln_fwd(x, y, g, b, *, ts):
    s = x.shape[0]

    def body(x_ref, y_ref, g_ref, b_ref, o_ref, o16_ref):
        xhat, _ = _ln_stats(ALPHA * x_ref[...] + y_ref[...])
        out = xhat * g_ref[...] + b_ref[...]
        o_ref[...] = out
        o16_ref[...] = out.astype(o16_ref.dtype)

    return pl.pallas_call(
        body, name="ln_fwd", grid=(s // ts,),
        in_specs=[_row_spec(ts, D_MODEL), _row_spec(ts, D_MODEL), _bc_spec((1, D_MODEL)), _bc_spec((1, D_MODEL))],
        out_specs=[_row_spec(ts, D_MODEL), _row_spec(ts, D_MODEL)],
        out_shape=[jax.ShapeDtypeStruct((s, D_MODEL), F32), jax.ShapeDtypeStruct((s, D_MODEL), BF16)],
        compiler_params=_params(("parallel",)),
    )(x, y, g, b)


def _ln_loss_bwd(x1, y, target, g, b, *, ts):
    s = x1.shape[0]
    n = s // ts

    def body(x_ref, y_ref, t_ref, g_ref, b_ref, dz_ref, dz16_ref, dg_ref, db_ref, loss_ref, ag, ab, al):
        i = pl.program_id(0)

        @pl.when(i == 0)
        def _():
            ag[...] = jnp.zeros_like(ag)
            ab[...] = jnp.zeros_like(ab)
            al[...] = jnp.zeros_like(al)

        xhat, rstd = _ln_stats(ALPHA * x_ref[...] + y_ref[...])
        err = xhat * g_ref[...] + b_ref[...] - t_ref[...]
        al[...] += _fold8(err * err)
        dy = err * (1.0 / D_MODEL)
        ag[...] += _fold8(dy * xhat)
        ab[...] += _fold8(dy)
        dz = _ln_bwd(dy, xhat, rstd, g_ref[...])
        dz_ref[...] = dz
        dz16_ref[...] = dz.astype(dz16_ref.dtype)

        @pl.when(i == n - 1)
        def _():
            dg_ref[...] = jnp.sum(ag[...], axis=0, keepdims=True)
            db_ref[...] = jnp.sum(ab[...], axis=0, keepdims=True)
            loss_ref[...] = jnp.full((1, LANES), (0.5 / D_MODEL) * jnp.sum(al[...]), F32)

    return pl.pallas_call(
        body, name="ln_loss_bwd", grid=(n,),
        in_specs=[_row_spec(ts, D_MODEL)] * 3 + [_bc_spec((1, D_MODEL))] * 2,
        out_specs=[_row_spec(ts, D_MODEL), _row_spec(ts, D_MODEL), _bc_spec((1, D_MODEL)), _bc_spec((1, D_MODEL)),
                   _bc_spec((1, LANES))],
        out_shape=[jax.ShapeDtypeStruct((s, D_MODEL), F32), jax.ShapeDtypeStruct((s, D_MODEL), BF16),
                   jax.ShapeDtypeStruct((1, D_MODEL), F32), jax.ShapeDtypeStruct((1, D_MODEL), F32),
                   jax.ShapeDtypeStruct((1, LANES), F32)],
        scratch_shapes=[pltpu.VMEM((8, D_MODEL), F32)] * 3,
        compiler_params=_params(("arbitrary",)),
    )(x1, y, target, g, b)


def _ln_bwd_call(dy, x, y, g, *, ts):
    s = x.shape[0]
    n = s // ts

    def body(dy_ref, x_ref, y_ref, g_ref, dz_ref, dz16_ref, dg_ref, db_ref, ag, ab):
        i = pl.program_id(0)

        @pl.when(i == 0)
        def _():
            ag[...] = jnp.zeros_like(ag)
            ab[...] = jnp.zeros_like(ab)

        xhat, rstd = _ln_stats(ALPHA * x_ref[...] + y_ref[...])
        dy = dy_ref[...]
        ag[...] += _fold8(dy * xhat)
        ab[...] += _fold8(dy)
        dz = _ln_bwd(dy, xhat, rstd, g_ref[...])
        dz_ref[...] = dz
        dz16_ref[...] = dz.astype(dz16_ref.dtype)

        @pl.when(i == n - 1)
        def _():
            dg_ref[...] = jnp.sum(ag[...], axis=0, keepdims=True)
            db_ref[...] = jnp.sum(ab[...], axis=0, keepdims=True)

    return pl.pallas_call(
        body, name="ln_bwd", grid=(n,),
        in_specs=[_row_spec(ts, D_MODEL)] * 3 + [_bc_spec((1, D_MODEL))],
        out_specs=[_row_spec(ts, D_MODEL), _row_spec(ts, D_MODEL), _bc_spec((1, D_MODEL)), _bc_spec((1, D_MODEL))],
        out_shape=[jax.ShapeDtypeStruct((s, D_MODEL), F32), jax.ShapeDtypeStruct((s, D_MODEL), BF16),
                   jax.ShapeDtypeStruct((1, D_MODEL), F32), jax.ShapeDtypeStruct((1, D_MODEL), F32)],
        scratch_shapes=[pltpu.VMEM((8, D_MODEL), F32)] * 2,
        compiler_params=_params(("arbitrary",)),
    )(dy, x, y, g)


def _residual_store(tile, o_ref, dz_ref):
    o_ref[...] = ALPHA * dz_ref[...] + tile


C_LAT = slice(0, KV_LORA)
C_ROPE = slice(KV_LORA, KV_DOWN_PAD)
C_QL = slice(KV_DOWN_PAD, KV_DOWN_PAD + Q_LORA)
C_GATE = slice(KV_DOWN_PAD + Q_LORA, H1_B)


def _rms(x, eps=1e-6):
    r = lax.rsqrt(jnp.mean(x * x, axis=-1, keepdims=True) + eps)
    return x * r, r


def _rms_bwd(dy, xhat, r, g):
    dxh = dy * g
    return r * (dxh - xhat * jnp.mean(dxh * xhat, axis=-1, keepdims=True))


def _kvq_prep(h1, kv_norm, q_norm, cos, sa, sb, *, ts):
    s = h1.shape[0]

    def body(h_ref, kn_ref, qn_ref, c_ref, sa_ref, sb_ref, lat_ref, kr_ref, ql_ref):
        lat, _ = _rms(h_ref[:, C_LAT])
        lat_ref[...] = (lat * kn_ref[...]).astype(lat_ref.dtype)
        kr_ref[...] = _rope_b(h_ref[:, C_ROPE], c_ref[...], sa_ref[...], sb_ref[...], 1.0).astype(kr_ref.dtype)
        ql, _ = _rms(h_ref[:, C_QL])
        ql_ref[...] = (ql * qn_ref[...]).astype(ql_ref.dtype)

    return pl.pallas_call(
        body, name="kvq_prep", grid=(s // ts,),
        in_specs=[_row_spec(ts, H1_B), _bc_spec((1, KV_LORA)), _bc_spec((1, Q_LORA))] + [_row_spec(ts, 128)] * 3,
        out_specs=[_row_spec(ts, KV_LORA), _row_spec(ts, 128), _row_spec(ts, Q_LORA)],
        out_shape=[jax.ShapeDtypeStruct((s, KV_LORA), BF16), jax.ShapeDtypeStruct((s, 128), BF16),
                   jax.ShapeDtypeStruct((s, Q_LORA), BF16)],
        compiler_params=_params(("parallel",)),
    )(h1, kv_norm, q_norm, cos, sa, sb)


LOG2E = 1.4426950408889634
LN2 = 0.6931471805599453
Q_SCALE = ATT_SCALE * LOG2E


def _rope_q_store(tile, o_ref, c_ref, sa_ref, sb_ref):
    c, a, b = c_ref[...], sa_ref[...], sb_ref[...]
    for hd in range(tile.shape[1] // QK_PAD):
        lo = hd * QK_PAD
        o_ref[:, lo:lo + 128] = (tile[:, lo:lo + 128] * Q_SCALE).astype(o_ref.dtype)
        o_ref[:, lo + 128:lo + 256] = (_rope_b(tile[:, lo + 128:lo + 256], c, a, b, 1.0) * Q_SCALE).astype(o_ref.dtype)


def _assemble_k_store(tile, o_ref, kr_ref):
    r = kr_ref[...]
    for hd in range(tile.shape[1] // QK_NOPE):
        o_ref[:, hd * QK_PAD:hd * QK_PAD + 128] = tile[:, hd * 128:(hd + 1) * 128].astype(o_ref.dtype)
        o_ref[:, hd * QK_PAD + 128:(hd + 1) * QK_PAD] = r


def _h1_bwd(h1, dlat_k, dlat_v, dkr_heads, dqn, dg16, kv_norm, q_norm, cos, sa, sb, *, ts):
    s = h1.shape[0]
    n = s // ts

    def body(h_ref, dk_ref, dv_ref, dkr_ref, dqn_ref, dg_ref, kn_ref, qn_ref, c_ref, sa_ref, sb_ref,
             o_ref, dkn_ref, dqn_out_ref, akn, aqn):
        i = pl.program_id(0)

        @pl.when(i == 0)
        def _():
            akn[...] = jnp.zeros_like(akn)
            aqn[...] = jnp.zeros_like(aqn)

        lat, r = _rms(h_ref[:, C_LAT])
        dlat = dk_ref[...] + dv_ref[...]
        akn[...] += _fold8(dlat * lat)
        o_ref[:, C_LAT] = _rms_bwd(dlat, lat, r, kn_ref[...]).astype(o_ref.dtype)

        dkr = dkr_ref[:, 0:128]
        for hd in range(1, H_B):
            dkr = dkr + dkr_ref[:, hd * 128:(hd + 1) * 128]
        o_ref[:, C_ROPE] = _rope_b(dkr, c_ref[...], sa_ref[...], sb_ref[...], -1.0).astype(o_ref.dtype)

        ql, rq = _rms(h_ref[:, C_QL])
        dq = dqn_ref[...]
        aqn[...] += _fold8(dq * ql)
        o_ref[:, C_QL] = _rms_bwd(dq, ql, rq, qn_ref[...]).astype(o_ref.dtype)
        o_ref[:, C_GATE] = dg_ref[...]

        @pl.when(i == n - 1)
        def _():
            dkn_ref[...] = jnp.sum(akn[...], axis=0, keepdims=True)
            dqn_out_ref[...] = jnp.sum(aqn[...], axis=0, keepdims=True)

    return pl.pallas_call(
        body, name="h1_bwd", grid=(n,),
        in_specs=[_row_spec(ts, H1_B), _row_spec(ts, KV_LORA), _row_spec(ts, KV_LORA), _row_spec(ts, H_B * 128),
                  _row_spec(ts, Q_LORA), _row_spec(ts, WIDTH_B), _bc_spec((1, KV_LORA)), _bc_spec((1, Q_LORA))]
        + [_row_spec(ts, 128)] * 3,
        out_specs=[_row_spec(ts, H1_B), _bc_spec((1, KV_LORA)), _bc_spec((1, Q_LORA))],
        out_shape=[jax.ShapeDtypeStruct((s, H1_B), BF16), jax.ShapeDtypeStruct((1, KV_LORA), F32),
                   jax.ShapeDtypeStruct((1, Q_LORA), F32)],
        scratch_shapes=[pltpu.VMEM((8, KV_LORA), F32), pltpu.VMEM((8, Q_LORA), F32)],
        compiler_params=_params(("arbitrary",)),
    )(h1, dlat_k, dlat_v, dkr_heads, dqn, dg16, kv_norm, q_norm, cos, sa, sb)


def _dot(a, b, ca, cb):
    return lax.dot_general(a, b, (((ca,), (cb,)), ((), ())), preferred_element_type=F32)


CHUNKS_PER_STEP = 2


def _table_specs():
    full = lambda shape: pl.BlockSpec(shape, lambda i: (0,) * len(shape))
    return [full((H_A, CHUNK, CHUNK)), full((H_A, CHUNK, 1)), full((H_A, CHUNK, 1)), full((H_A, 1, DV_A))]


def _retention_fwd(h_a, cos, sin, tables):
    s = h_a.shape[0]
    n = s // CHUNK

    def body(h_ref, c_ref, s_ref, dm_ref, qd_ref, kd_ref, cd_ref, q_ref, k_ref, v_ref, o_ref, u_ref, st_ref, state):
        @pl.when(pl.program_id(0) == 0)
        def _():
            state[...] = jnp.zeros_like(state)

        for cb in range(CHUNKS_PER_STEP):
            rows = slice(cb * CHUNK, (cb + 1) * CHUNK)
            c, sn = c_ref[rows, :], s_ref[rows, :]
            for hd in range(H_A):
                qs, vs = slice(hd * DK_A, (hd + 1) * DK_A), slice(hd * DV_A, (hd + 1) * DV_A)
                for r_ref, base, scale in ((q_ref, Q_COL, 1.0), (k_ref, K_COL, DK_A ** -0.5)):
                    lo = base + hd * DK_A
                    x1, x2 = h_ref[rows, lo:lo + 128], h_ref[rows, lo + 128:lo + 256]
                    r_ref[rows, hd * DK_A:hd * DK_A + 128] = ((x1 * c - x2 * sn) * scale).astype(r_ref.dtype)
                    r_ref[rows, hd * DK_A + 128:(hd + 1) * DK_A] = ((x2 * c + x1 * sn) * scale).astype(r_ref.dtype)
                v_ref[rows, vs] = h_ref[rows, V_COL + hd * DV_A:V_COL + (hd + 1) * DV_A].astype(v_ref.dtype)
                qv, kv, vv = q_ref[rows, qs], k_ref[rows, qs], v_ref[rows, vs]
                st = state[hd]
                st16 = st.astype(BF16)
                st_ref[cb, hd] = st16
                scores = _dot(qv, kv, 1, 1) * dm_ref[hd]
                qd = (qv.astype(F32) * qd_ref[hd]).astype(BF16)
                o = _dot(scores.astype(BF16), vv, 1, 0) + _dot(qd, st16, 1, 0)
                o_ref[rows, vs] = o
                kd = (kv.astype(F32) * kd_ref[hd]).astype(BF16)
                state[hd] = st * cd_ref[hd] + _dot(kd, vv, 0, 0)
                on, _ = _group_norm(o)
                g = h_ref[rows, GATE_COL + hd * DV_A:GATE_COL + (hd + 1) * DV_A]
                u_ref[rows, vs] = (on * (g * _sigmoid(g))).astype(u_ref.dtype)

    row = lambda w: pl.BlockSpec((CHUNKS_PER_STEP * CHUNK, w), lambda i: (i, 0))
    return pl.pallas_call(
        body, name="retention_fwd", grid=(n // CHUNKS_PER_STEP,),
        in_specs=[row(IN_A), row(128), row(128)] + _table_specs(),
        out_specs=[row(H_A * DK_A), row(H_A * DK_A), row(WIDTH_A), row(WIDTH_A), row(WIDTH_A),
                   pl.BlockSpec((CHUNKS_PER_STEP, H_A, DK_A, DV_A), lambda i: (i, 0, 0, 0))],
        out_shape=[jax.ShapeDtypeStruct((s, H_A * DK_A), BF16), jax.ShapeDtypeStruct((s, H_A * DK_A), BF16),
                   jax.ShapeDtypeStruct((s, WIDTH_A), BF16), jax.ShapeDtypeStruct((s, WIDTH_A), F32),
                   jax.ShapeDtypeStruct((s, WIDTH_A), BF16), jax.ShapeDtypeStruct((n, H_A, DK_A, DV_A), BF16)],
        scratch_shapes=[pltpu.VMEM((H_A, DK_A, DV_A), F32)],
        compiler_params=_params(("arbitrary",)),
    )(h_a, cos, sin, *tables)


def _retention_bwd(q, k, v, states, du, o, h_a, cos, sin, tables):
    s = q.shape[0]
    n = s // CHUNK

    def body(q_ref, k_ref, v_ref, st_ref, du_ref, o_ref, g_ref, c_ref, s_ref, dm_ref, qd_ref, kd_ref, cd_ref, dh_ref, grad_state):
        @pl.when(pl.program_id(0) == 0)
        def _():
            grad_state[...] = jnp.zeros_like(grad_state)

        for cb in reversed(range(CHUNKS_PER_STEP)):
            rows = slice(cb * CHUNK, (cb + 1) * CHUNK)
            c, sn = c_ref[rows, :], s_ref[rows, :]
            for hd in range(H_A):
                qs, vs = slice(hd * DK_A, (hd + 1) * DK_A), slice(hd * DV_A, (hd + 1) * DV_A)
                on, rstd = _group_norm(o_ref[rows, vs])
                g = g_ref[rows, vs]
                sg = _sigmoid(g)
                du_v = du_ref[rows, vs]
                don = du_v * (g * sg)
                dh_ref[rows, GATE_COL + hd * DV_A:GATE_COL + (hd + 1) * DV_A] = (du_v * on * (sg * (1.0 + g * (1.0 - sg)))).astype(dh_ref.dtype)
                m1 = jnp.mean(don, axis=-1, keepdims=True)
                m2 = jnp.mean(don * on, axis=-1, keepdims=True)
                dov = (rstd * (don - m1 - on * m2)).astype(BF16)

                qv, kv, vv = q_ref[rows, qs], k_ref[rows, qs], v_ref[rows, vs]
                dm = dm_ref[hd]
                gs = grad_state[hd]
                g16 = gs.astype(BF16)
                scores = (_dot(qv, kv, 1, 1) * dm).astype(BF16)
                dscores = (_dot(dov, vv, 1, 1) * dm).astype(BF16)
                qd = (qv.astype(F32) * qd_ref[hd]).astype(BF16)
                kd = (kv.astype(F32) * kd_ref[hd]).astype(BF16)
                dq = _dot(dscores, kv, 1, 0) + _dot(dov, st_ref[cb, hd], 1, 1) * qd_ref[hd]
                dk = (_dot(dscores, qv, 0, 0) + _dot(vv, g16, 1, 1) * kd_ref[hd]) * (DK_A ** -0.5)
                dh_ref[rows, V_COL + hd * DV_A:V_COL + (hd + 1) * DV_A] = (_dot(scores, dov, 0, 0) + _dot(kd, g16, 1, 0)).astype(dh_ref.dtype)
                grad_state[hd] = gs * cd_ref[hd] + _dot(qd, dov, 0, 0)
                for d, base in ((dq, Q_COL), (dk, K_COL)):
                    lo = base + hd * DK_A
                    d1, d2 = d[:, 0:128], d[:, 128:256]
                    dh_ref[rows, lo:lo + 128] = (d1 * c + d2 * sn).astype(dh_ref.dtype)
                    dh_ref[rows, lo + 128:lo + 256] = (d2 * c - d1 * sn).astype(dh_ref.dtype)

    steps = n // CHUNKS_PER_STEP
    rev = lambda i: steps - 1 - i
    row = lambda w, col=0: pl.BlockSpec((CHUNKS_PER_STEP * CHUNK, w), lambda i: (rev(i), col))
    return pl.pallas_call(
        body, name="retention_bwd", grid=(steps,),
        in_specs=[row(H_A * DK_A), row(H_A * DK_A), row(WIDTH_A),
                  pl.BlockSpec((CHUNKS_PER_STEP, H_A, DK_A, DV_A), lambda i: (rev(i), 0, 0, 0)),
                  row(WIDTH_A), row(WIDTH_A), row(WIDTH_A, GATE_COL // WIDTH_A), row(128), row(128)] + _table_specs(),
        out_specs=row(IN_A),
        out_shape=jax.ShapeDtypeStruct((s, IN_A), BF16),
        scratch_shapes=[pltpu.VMEM((H_A, DK_A, DV_A), F32)],
        compiler_params=_params(("arbitrary",)),
    )(q, k, v, states, du, o, h_a, cos, sin, *tables)


GATE_BLOCK0 = (KV_DOWN_PAD + Q_LORA) // LANES


def _causal_mask(sc, row0):
    row = lax.broadcasted_iota(jnp.int32, sc.shape, 0) + row0
    col = lax.broadcasted_iota(jnp.int32, sc.shape, 1)
    return jnp.where(col <= row, sc, NEG_BIG)


def _key_block_loop(step, n, per_trip, smallest=1):
    def trip_body(jj, carry):
        for t in range(per_trip):
            step(per_trip * jj + t)
        return carry

    lax.fori_loop(0, n // per_trip, trip_body, 0)
    group = per_trip // 2
    while group >= smallest:
        def tail(group=group):
            first = (n // (2 * group)) * (2 * group)
            for t in range(group):
                step(first + t)

        pl.when((n // group) % 2 == 1)(tail)
        group //= 2


def _attention_fwd(q, k, v, h1, *, blk, bk, sub, per_trip):
    s = q.shape[0]
    nb = s // blk

    def body(q_ref, k_ref, v_ref, g_ref, o_ref, u_ref, lse_ref, vext_s, m_s, acc_s):
        i = pl.program_id(1)

        @pl.when(i == 0)
        def _():
            vext_s[:, 0:V_HEAD] = v_ref[...]
            vext_s[:, V_HEAD:2 * V_HEAD] = jnp.ones((s, V_HEAD), vext_s.dtype)

        m_s[...] = jnp.full_like(m_s, NEG_BIG)
        acc_s[...] = jnp.zeros_like(acc_s)

        def update(rows, kb, vb, row0):
            sc = _dot(q_ref[rows, :], kb, 1, 1)
            if row0 is not None:
                sc = _causal_mask(sc, row0)
            m_prev = m_s[rows, :]
            m_new = jnp.maximum(m_prev, jnp.max(sc, axis=-1, keepdims=True))
            p = jnp.exp2(sc - jnp.tile(m_new, (1, kb.shape[0] // LANES)))
            a = jnp.exp2(m_prev - m_new)
            acc_s[rows, :] = jnp.tile(a, (1, 2)) * acc_s[rows, :] + _dot(p.astype(BF16), vb, 1, 0)
            m_s[rows, :] = m_new

        def step(j):
            kv_rows = pl.ds(pl.multiple_of(j * bk, bk), bk)
            kb, vb = k_ref[kv_rows, :], vext_s[kv_rows, :]
            for r in range(blk // sub):
                update(slice(r * sub, (r + 1) * sub), kb, vb, None)

        _key_block_loop(step, i * (blk // bk), per_trip, smallest=blk // bk)
        for r in range(blk // sub):
            ncols = (r + 1) * sub
            kv_rows = pl.ds(pl.multiple_of(i * blk, blk), ncols)
            update(slice(r * sub, (r + 1) * sub), k_ref[kv_rows, :], vext_s[kv_rows, :], r * sub)
        acc = acc_s[...]
        l = acc[:, V_HEAD:2 * V_HEAD]
        o = acc[:, 0:V_HEAD] / l
        g = g_ref[...]
        o_ref[...] = o
        u_ref[...] = (o * (g * _sigmoid(g))).astype(u_ref.dtype)
        lse_ref[0] = (m_s[...] + jnp.log2(l))[:, 0:1]

    blk_idx = lambda h, i: (i, h)
    return pl.pallas_call(
        body, name="attention_fwd", grid=(H_B, nb),
        in_specs=[pl.BlockSpec((blk, QK_PAD), blk_idx), pl.BlockSpec((s, QK_PAD), lambda h, i: (0, h)),
                  pl.BlockSpec((s, V_HEAD), lambda h, i: (0, h)), pl.BlockSpec((blk, LANES), lambda h, i: (i, GATE_BLOCK0 + h))],
        out_specs=[pl.BlockSpec((blk, V_HEAD), blk_idx), pl.BlockSpec((blk, V_HEAD), blk_idx),
                   pl.BlockSpec((1, blk, 1), lambda h, i: (h, i, 0))],
        out_shape=[jax.ShapeDtypeStruct((s, WIDTH_B), F32), jax.ShapeDtypeStruct((s, WIDTH_B), BF16),
                   jax.ShapeDtypeStruct((H_B, s, 1), F32)],
        scratch_shapes=[pltpu.VMEM((s, 2 * V_HEAD), BF16), pltpu.VMEM((blk, LANES), F32), pltpu.VMEM((blk, 2 * V_HEAD), F32)],
        compiler_params=_params(("parallel", "arbitrary")),
    )(q, k, v, h1)


def _attention_bwd(q, k, v, du, o, h1, lse, cos, sa, sb, *, blk, bk, per_trip):
    s = q.shape[0]
    nb = s // blk

    def body(q_ref, k_ref, v_ref, du_ref, o_ref, g_ref, lse_ref, c_ref, sa_ref, sb_ref,
             dq_ref, dkn_ref, dkr_ref, dv_ref, dg_ref, lse_s, dl_s, do_s, dq_s):
        i = pl.program_id(1)
        g = g_ref[...]
        sg = _sigmoid(g)
        du_v, ov = du_ref[...], o_ref[...]
        do = du_v * (g * sg)
        dg_ref[...] = (du_v * ov * (sg * (1.0 + g * (1.0 - sg)))).astype(dg_ref.dtype)
        do_s[...] = do.astype(do_s.dtype)
        dl_s[...] = jnp.broadcast_to(jnp.sum(do * ov, axis=-1, keepdims=True), (blk, LANES))
        lse_s[...] = jnp.broadcast_to(lse_ref[0], (blk, LANES))
        dq_s[...] = jnp.zeros_like(dq_s)

        def products(rows, kv_rows, row0):
            qv, dov, kb = q_ref[rows, :], do_s[rows, :], k_ref[kv_rows, :]
            tile = (1, kb.shape[0] // LANES)
            sc = _dot(qv, kb, 1, 1)
            if row0 is not None:
                sc = _causal_mask(sc, row0)
            p = jnp.exp2(sc - jnp.tile(lse_s[rows, :], tile))
            dp = _dot(dov, v_ref[kv_rows, :], 1, 1)
            ds = (p * (dp - jnp.tile(dl_s[rows, :], tile))).astype(BF16)
            dq_s[rows, :] += _dot(ds, kb, 1, 0)
            return _dot(ds, qv, 0, 0), _dot(p.astype(BF16), dov, 0, 0)

        def put(kv_rows, dk_c, dv_c, first):
            if first:
                dkn_ref[kv_rows, :] = dk_c[:, 0:128]
                dkr_ref[kv_rows, :] = dk_c[:, 128:256]
                dv_ref[kv_rows, :] = dv_c
            else:
                dkn_ref[kv_rows, :] += dk_c[:, 0:128]
                dkr_ref[kv_rows, :] += dk_c[:, 128:256]
                dv_ref[kv_rows, :] += dv_c

        n_sub = blk // bk
        sub_rows = [slice(r * bk, (r + 1) * bk) for r in range(n_sub)]

        def step(j):
            kv_rows = pl.ds(pl.multiple_of(j * bk, bk), bk)
            for rows in sub_rows:
                dk_c, dv_c = products(rows, kv_rows, None)
                put(kv_rows, dk_c, dv_c, False)

        _key_block_loop(step, i * n_sub, per_trip, smallest=n_sub)
        for c in range(n_sub):
            kv_rows = pl.ds(pl.multiple_of(i * blk + c * bk, bk), bk)
            for r in range(c, n_sub):
                dk_c, dv_c = products(sub_rows[r], kv_rows, 0 if r == c else None)
                put(kv_rows, dk_c, dv_c, r == c)
        dq = dq_s[...] * ATT_SCALE
        dq_ref[:, 0:128] = dq[:, 0:128].astype(dq_ref.dtype)
        dq_ref[:, 128:256] = _rope_b(dq[:, 128:256], c_ref[...], sa_ref[...], sb_ref[...], -1.0).astype(dq_ref.dtype)

        @pl.when(i == nb - 1)
        def _():
            dkn_ref[...] = dkn_ref[...] * LN2
            dkr_ref[...] = dkr_ref[...] * LN2

    head = lambda h, i: (0, h)
    blk_idx = lambda h, i: (i, h)
    row_idx = lambda h, i: (i, 0)
    return pl.pallas_call(
        body, name="attention_bwd", grid=(H_B, nb),
        in_specs=[pl.BlockSpec((blk, QK_PAD), blk_idx), pl.BlockSpec((s, QK_PAD), head), pl.BlockSpec((s, V_HEAD), head),
                  pl.BlockSpec((blk, V_HEAD), blk_idx), pl.BlockSpec((blk, V_HEAD), blk_idx),
                  pl.BlockSpec((blk, LANES), lambda h, i: (i, GATE_BLOCK0 + h)), pl.BlockSpec((1, blk, 1), lambda h, i: (h, i, 0)),
                  pl.BlockSpec((blk, LANES), row_idx), pl.BlockSpec((blk, LANES), row_idx), pl.BlockSpec((blk, LANES), row_idx)],
        out_specs=[pl.BlockSpec((blk, QK_PAD), blk_idx), pl.BlockSpec((s, 128), head), pl.BlockSpec((s, 128), head),
                   pl.BlockSpec((s, 128), head), pl.BlockSpec((blk, V_HEAD), blk_idx)],
        out_shape=[jax.ShapeDtypeStruct((s, H_B * QK_PAD), BF16), jax.ShapeDtypeStruct((s, H_B * 128), F32),
                   jax.ShapeDtypeStruct((s, H_B * 128), F32), jax.ShapeDtypeStruct((s, H_B * 128), F32),
                   jax.ShapeDtypeStruct((s, WIDTH_B), BF16)],
        scratch_shapes=[pltpu.VMEM((blk, LANES), F32), pltpu.VMEM((blk, LANES), F32), pltpu.VMEM((blk, V_HEAD), BF16),
                        pltpu.VMEM((blk, QK_PAD), F32)],
        compiler_params=_params(("parallel", "arbitrary")),
    )(q, k, v, du, o, h1, lse, cos, sa, sb)


def _local_step(x, target, w, kv_norm, q_norm, ln_g, ln_b, *, ts=512, blk=512, late=None, reduce=None):
    s = x.shape[0]
    cos_a, sin_a = _rope_tables_a(s)
    cos_b, sa_b, sb_b = _rope_tables_b(s)
    tables = _retention_tables()
    g0, g1, b0, b1 = ln_g[0:1], ln_g[1:2], ln_b[0:1], ln_b[1:2]

    x16 = x.astype(BF16)
    if late is None:
        h_a = _mm(x16, w["a_in"], tn=1536, name="a_in_fwd")
    else:
        flat_b, chip = late
        h_a, got = _mm(x16, w["a_in"], tn=1536, name="a_in_fwd", side=_gather_side(flat_b))
        got = lax.dynamic_update_slice(got, flat_b[None], (chip, 0, 0))
        w = {**w, **_kernel_layout_b(_full_from_gathered(got, B_SHARDS))}
    q_a, k_a, v_a, o_a, u_a, states = _retention_fwd(h_a, cos_a, sin_a, tables)
    y_a = _mm(u_a, w["a_out"], tn=1024, name="a_out_fwd")
    x1, x1_16 = _ln_fwd(x, y_a, g0, b0, ts=ts)

    h1 = _mm(x1_16, w["b_in1"], tn=1152, name="b_in_fwd")
    lat16, kr16, qn16 = _kvq_prep(h1, kv_norm, q_norm, cos_b, sa_b, sb_b, ts=ts)
    k16 = _mm(lat16, w["up_k"], out_dtype=BF16, tn=2048, out_tn=H_B * QK_PAD, extras=(kr16,), epilogue=_assemble_k_store, name="up_k_fwd")
    v16 = _mm(lat16, w["up_v"], out_dtype=BF16, tn=2048, name="up_v_fwd")
    q16 = _mm(qn16, w["uq"], out_dtype=BF16, tn=2048, extras=(cos_b, sa_b, sb_b), epilogue=_rope_q_store, name="uq_fwd")
    o_b, u_b, lse = _attention_fwd(q16, k16, v16, h1, blk=2 * blk, bk=blk, sub=blk // 2, per_trip=4)
    y_b = _mm(u_b, w["b_out"], tn=1024, name="b_out_fwd")

    dz_b, dz_b16, dg1, db1, loss = _ln_loss_bwd(x1, y_b, target, g1, b1, ts=ts)
    d_b_out = _mm(u_b, dz_b16, ta=True, tn=1024, tk=1024, name="b_out_dw")
    du_b = _mm(dz_b16, w["b_out"], tb=True, tn=1024, name="b_out_dx")
    dqf16, dkn, dkr_heads, dv, dgate16 = _attention_bwd(q16, k16, v16, du_b, o_b, h1, lse, cos_b, sa_b, sb_b, blk=2 * blk, bk=blk, per_trip=4)
    d_uq = _mm(qn16, dqf16, ta=True, tm=768, tn=2048, tk=1024, name="uq_dw")
    dqn = _mm(dqf16, w["uq"], tb=True, tn=768, tk=2048, name="uq_dx")
    d_up_k = _mm(lat16, dkn, ta=True, tn=2048, tk=1024, name="up_k_dw")
    d_up_v = _mm(lat16, dv, ta=True, tn=2048, tk=1024, name="up_v_dw")
    dlat_k = _mm(dkn, w["up_k"], tb=True, tn=512, name="up_k_dx")
    dlat_v = _mm(dv, w["up_v"], tb=True, tn=512, name="up_v_dx")
    dh1, dkvn, dqnorm = _h1_bwd(h1, dlat_k, dlat_v, dkr_heads, dqn, dgate16, kv_norm, q_norm, cos_b, sa_b, sb_b, ts=ts)
    d_b_in1 = _mm(x1_16, dh1, ta=True, tn=1152, tk=1024, name="b_in_dw")
    dx1 = _mm(dh1, w["b_in1"], tb=True, tn=1024, extras=(dz_b,), epilogue=_residual_store, name="b_in_dx")

    dz_a, dz_a16, dg0, db0 = _ln_bwd_call(dx1, x, y_a, g0, ts=ts)
    d_a_out = _mm(u_a, dz_a16, ta=True, tn=1024, tk=1024, name="a_out_dw")
    du_a = _mm(dz_a16, w["a_out"], tb=True, tn=1024, name="a_out_dx")
    dh_a = _retention_bwd(q_a, k_a, v_a, states, du_a, o_a, h_a, cos_a, sin_a, tables)
    grads = dict(a_out=d_a_out, b_in1=d_b_in1, uq=d_uq, b_out=d_b_out, up_k=d_up_k, up_v=d_up_v)
    small = dict(ln_g=jnp.concatenate([dg0, dg1], axis=0), ln_b=jnp.concatenate([db0, db1], axis=0),
                 q_norm=dqnorm, kv_norm=dkvn)
    if reduce is None:
        grads["a_in"] = _mm(x16, dh_a, ta=True, tn=1536, tk=1024, name="a_in_dw")
        grad_x = _mm(dh_a, w["a_in"], tb=True, tn=1024, tk=2048, extras=(dz_a,), epilogue=_residual_store, name="a_in_dx")
        return loss, grad_x, grads, small
    own_early, travel_early = reduce(_reference_layout_grads(grads), EARLY_SHARDS, EARLY_ROWS, "early")
    d_a_in, got_early = _mm(x16, dh_a, ta=True, tn=1536, tk=1024, name="a_in_dw", side=_chip_exchange_side(travel_early))
    own_late, travel_late = reduce(dict(a_w_in=d_a_in), LATE_SHARDS, LATE_ROWS, "late")
    grad_x, got_late = _mm(dh_a, w["a_in"], tb=True, tn=1024, tk=2048, extras=(dz_a,), epilogue=_residual_store, name="a_in_dx",
                           side=_chip_exchange_side(travel_late))
    return loss, grad_x, ((own_early, got_early), (own_late, got_late)), small


def _flat_shards(shards, dtype, which, total_rows):
    parts = [shards[name].reshape(rows, FLAT_COLS) for name, rows in which]
    used = sum(rows for _, rows in which)
    parts.append(jnp.zeros((total_rows - used, FLAT_COLS), parts[0].dtype))
    return jnp.concatenate(parts, axis=0).astype(dtype)


def _unflat_shards(flat, shapes, shards):
    out, off = {}, 0
    for name, rows in shards:
        out[name] = flat[off:off + rows].reshape(shapes[name])
        off += rows
    return out


COL_SHARDED = {"a_w_in": (D_MODEL, IN_A), "b_w_in": (D_MODEL, IN_B), "b_w_uq": (Q_LORA, H_B * (QK_NOPE + QK_ROPE)),
               "kv_w_up": (KV_LORA, H_B * (QK_NOPE + V_HEAD))}
ROW_SHARDED = {"a_w_out": (WIDTH_A, D_MODEL), "b_w_out": (WIDTH_B, D_MODEL), "kv_w_down": (D_MODEL, KV_LORA + QK_ROPE)}


def _full_from_gathered(gathered, shards=SHARD_ROWS):
    out, off = {}, 0
    for name, rows in shards:
        part = gathered[:, off:off + rows]
        off += rows
        if name in COL_SHARDED:
            r, c = COL_SHARDED[name]
            out[name] = part.reshape(N_CHIPS, r, c // N_CHIPS).transpose(1, 0, 2).reshape(r, c)
        else:
            r, c = ROW_SHARDED[name]
            out[name] = part.reshape(r, c)
    return out


def _chip_major(g):
    r, c = g.shape
    return g.reshape(r, N_CHIPS, c // N_CHIPS).transpose(1, 0, 2)


def _gathered_from_full(full, shards, total_rows):
    if len(shards) == 1 and shards[0][0] in COL_SHARDED:
        return _chip_major(full[shards[0][0]])
    parts = []
    for name, rows in shards:
        g = full[name]
        if name in COL_SHARDED:
            r, c = COL_SHARDED[name]
            g = g.reshape(r, N_CHIPS, c // N_CHIPS).transpose(1, 0, 2)
        parts.append(g.reshape(N_CHIPS, rows, FLAT_COLS))
    used = sum(rows for _, rows in shards)
    if total_rows > used:
        parts.append(jnp.zeros((N_CHIPS, total_rows - used, FLAT_COLS), F32))
    return jnp.concatenate(parts, axis=1)


A_SHARDS, B_SHARDS = SHARD_ROWS[:1], SHARD_ROWS[1:]
A_ROWS = sum(rows for _, rows in A_SHARDS)


def _kernel_layout_a(full):
    return dict(a_in=full["a_w_in"])


def _kernel_layout_b(full):
    uq = full["b_w_uq"].reshape(Q_LORA, H_B, QK_NOPE + QK_ROPE)
    uq = jnp.pad(uq, ((0, 0), (0, 0), (0, QK_PAD - QK_NOPE - QK_ROPE))).reshape(Q_LORA, H_B * QK_PAD)
    up = full["kv_w_up"].reshape(KV_LORA, H_B, QK_NOPE + V_HEAD)
    down = jnp.pad(full["kv_w_down"], ((0, 0), (0, KV_DOWN_PAD - KV_LORA - QK_ROPE)))
    return dict(a_out=full["a_w_out"], b_out=full["b_w_out"], uq=uq,
                up_k=up[:, :, :QK_NOPE].reshape(KV_LORA, H_B * QK_NOPE),
                up_v=up[:, :, QK_NOPE:].reshape(KV_LORA, H_B * V_HEAD),
                b_in1=jnp.concatenate([down, full["b_w_in"]], axis=1))


def _kernel_layout(full):
    return {**_kernel_layout_a(full), **_kernel_layout_b(full)}


EARLY_SHARDS = tuple(sh for sh in SHARD_ROWS if sh[0] != "a_w_in")
LATE_SHARDS = tuple(sh for sh in SHARD_ROWS if sh[0] == "a_w_in")
EARLY_ROWS, LATE_ROWS = 3072, 1536


def _reference_layout_grads(g):
    uq = g["uq"].reshape(Q_LORA, H_B, QK_PAD)[:, :, :QK_NOPE + QK_ROPE].reshape(Q_LORA, H_B * (QK_NOPE + QK_ROPE))
    up = jnp.concatenate([g["up_k"].reshape(KV_LORA, H_B, QK_NOPE), g["up_v"].reshape(KV_LORA, H_B, V_HEAD)], axis=2)
    return dict(a_w_out=g["a_out"], b_w_out=g["b_out"], b_w_uq=uq,
                kv_w_up=up.reshape(KV_LORA, H_B * (QK_NOPE + V_HEAD)),
                kv_w_down=g["b_in1"][:, :KV_LORA + QK_ROPE], b_w_in=g["b_in1"][:, KV_DOWN_PAD:])


HBM_SPEC = pl.BlockSpec(memory_space=pl.ANY)


def _me():
    return lax.axis_index("x"), lax.axis_index("y"), lax.axis_index("c")


def _chip_flips(x, y):
    return [(1 - x, y), (x, 1 - y), (1 - x, 1 - y)]


def _gather_copies(src_ref, out_ref, send_sems, recv_sems):
    x, y, c = _me()
    half = src_ref.shape[0] // 2
    my_rows = pl.ds(pl.multiple_of(c * half, 16), half)
    their_rows = pl.ds(pl.multiple_of((1 - c) * half, 16), half)
    chips = _chip_flips(x, y)
    sibling = (x, y, 1 - c)

    def copy(k, src, dst, to):
        return pltpu.make_async_remote_copy(src_ref=src, dst_ref=dst, send_sem=send_sems.at[k], recv_sem=recv_sems.at[k],
                                            device_id=to, device_id_type=MESH)

    sends = [copy(k, src_ref.at[my_rows, :], out_ref.at[2 * x + y, my_rows, :], (px, py, c)) for k, (px, py) in enumerate(chips)]
    landed = [out_ref.at[2 * px + py, my_rows, :] for px, py in chips]
    lands = [copy(k, landed[k], landed[k], (px, py, c)) for k, (px, py) in enumerate(chips)]
    forwards = [copy(3 + k, landed[k], landed[k], sibling) for k in range(3)]
    theirs = [out_ref.at[2 * px + py, their_rows, :] for px, py in chips]
    arrivals = [copy(3 + k, theirs[k], theirs[k], sibling) for k in range(3)]
    return sends, lands, forwards, arrivals


def _gather_start(src_ref, out_ref, send_sems, recv_sems):
    sends, _, _, _ = _gather_copies(src_ref, out_ref, send_sems, recv_sems)
    for cp in sends:
        cp.start()


def _gather_finish(src_ref, out_ref, send_sems, recv_sems):
    sends, lands, forwards, arrivals = _gather_copies(src_ref, out_ref, send_sems, recv_sems)
    for k in range(3):
        lands[k].wait_recv()
        forwards[k].start()
    for cp in arrivals:
        cp.wait_recv()
    for cp in sends + forwards:
        cp.wait_send()


def _gather_scratch():
    return [pltpu.SemaphoreType.DMA((6,)), pltpu.SemaphoreType.DMA((6,))]


def _gather_weights(flat16):
    def body(src_ref, out_ref, send_sems, recv_sems):
        _gather_start(src_ref, out_ref, send_sems, recv_sems)
        _gather_finish(src_ref, out_ref, send_sems, recv_sems)

    return pl.pallas_call(
        body, name="gather_weights",
        in_specs=[HBM_SPEC], out_specs=HBM_SPEC,
        out_shape=jax.ShapeDtypeStruct((N_CHIPS,) + flat16.shape, flat16.dtype),
        scratch_shapes=_gather_scratch(),
    )(flat16)


def _gather_side(flat16):
    return dict(inputs=[flat16], out_shape=jax.ShapeDtypeStruct((N_CHIPS,) + flat16.shape, flat16.dtype),
                scratch=_gather_scratch(), start=_gather_start, finish=_gather_finish)


def _chip_exchange_copies(p_ref, out_ref, send_sems, recv_sems):
    x, y, c = _me()
    return [pltpu.make_async_remote_copy(
        src_ref=p_ref.at[2 * px + py], dst_ref=out_ref.at[k], send_sem=send_sems.at[k], recv_sem=recv_sems.at[k],
        device_id=(px, py, c), device_id_type=MESH) for k, (px, py) in enumerate(_chip_flips(x, y))]


def _chip_exchange_start(p_ref, out_ref, send_sems, recv_sems):
    for cp in _chip_exchange_copies(p_ref, out_ref, send_sems, recv_sems):
        cp.start()


def _chip_exchange_finish(p_ref, out_ref, send_sems, recv_sems):
    copies = _chip_exchange_copies(p_ref, out_ref, send_sems, recv_sems)
    for cp in copies:
        cp.wait_send()
    for cp in copies:
        cp.wait_recv()


def _chip_exchange_side(p):
    return dict(inputs=[p], out_shape=jax.ShapeDtypeStruct((3,) + p.shape[1:], p.dtype),
                scratch=[pltpu.SemaphoreType.DMA((3,)), pltpu.SemaphoreType.DMA((3,))],
                start=_chip_exchange_start, finish=_chip_exchange_finish)


def _pair_swap(r, name):
    def body(r_ref, out_ref, send_sem, recv_sem):
        x, y, c = _me()
        cp = pltpu.make_async_remote_copy(src_ref=r_ref, dst_ref=out_ref, send_sem=send_sem, recv_sem=recv_sem,
                                          device_id=(x, y, 1 - c), device_id_type=MESH)
        cp.start()
        cp.wait_send()
        cp.wait_recv()

    return pl.pallas_call(
        body, name=name,
        in_specs=[HBM_SPEC], out_specs=HBM_SPEC,
        out_shape=jax.ShapeDtypeStruct(r.shape, r.dtype),
        scratch_shapes=[pltpu.SemaphoreType.DMA, pltpu.SemaphoreType.DMA],
    )(r)


def _sum_small(vec):
    def body(v_ref, out_ref, slots, send_sems, recv_sems):
        x, y, c = _me()
        me = 4 * x + 2 * y + c
        slots[me] = v_ref[...]
        flips = [(fx, fy, fc) for fx in (0, 1) for fy in (0, 1) for fc in (0, 1)][1:]
        copies = []
        for k, (fx, fy, fc) in enumerate(flips):
            copies.append(pltpu.make_async_remote_copy(
                src_ref=v_ref, dst_ref=slots.at[me], send_sem=send_sems.at[k], recv_sem=recv_sems.at[k],
                device_id=(x ^ fx, y ^ fy, c ^ fc), device_id_type=MESH))
        for cp in copies:
            cp.start()
        for cp in copies:
            cp.wait_send()
        for k, (fx, fy, fc) in enumerate(flips):
            src = 4 * (x ^ fx) + 2 * (y ^ fy) + (c ^ fc)
            pltpu.make_async_remote_copy(
                src_ref=v_ref, dst_ref=slots.at[src], send_sem=send_sems.at[k], recv_sem=recv_sems.at[k],
                device_id=(x ^ fx, y ^ fy, c ^ fc), device_id_type=MESH).wait_recv()
        total = slots[0]
        for d in range(1, N_DEV):
            total = total + slots[d]
        out_ref[...] = total

    return pl.pallas_call(
        body, name="sum_small",
        in_specs=[pl.BlockSpec(memory_space=pltpu.VMEM)], out_specs=pl.BlockSpec(memory_space=pltpu.VMEM),
        out_shape=jax.ShapeDtypeStruct(vec.shape, vec.dtype),
        scratch_shapes=[pltpu.VMEM((N_DEV,) + vec.shape, vec.dtype), pltpu.SemaphoreType.DMA((7,)),
                        pltpu.SemaphoreType.DMA((7,))],
    )(vec)


UPD_ROWS = 256


def _pair_sum(g, theirs, core, chip, name):
    half, cols = theirs.shape[1:]
    nb = half // UPD_ROWS

    def body(core_ref, chip_ref, g_ref, t_ref, own_ref, o16_ref):
        total = g_ref[0] + t_ref[0].astype(F32)
        o16_ref[0] = total.astype(o16_ref.dtype)

        @pl.when(pl.program_id(1) == chip_ref[0])
        def _():
            own_ref[...] = total

    return pl.pallas_call(
        body, name=name,
        grid_spec=pltpu.PrefetchScalarGridSpec(
            num_scalar_prefetch=2, grid=(nb, N_CHIPS),
            in_specs=[pl.BlockSpec((1, UPD_ROWS, cols), lambda i, d, core_ref, chip_ref: (d, core_ref[0] * nb + i, 0)),
                      pl.BlockSpec((1, UPD_ROWS, cols), lambda i, d, core_ref, chip_ref: (d, i, 0))],
            out_specs=[pl.BlockSpec((UPD_ROWS, cols), lambda i, d, core_ref, chip_ref: (i, 0)),
                       pl.BlockSpec((1, UPD_ROWS, cols), lambda i, d, core_ref, chip_ref: (d, i, 0))]),
        out_shape=[jax.ShapeDtypeStruct((half, cols), F32),
                   jax.ShapeDtypeStruct((N_CHIPS, half, cols), BF16)],
        compiler_params=_params(("parallel", "arbitrary")),
    )(core, chip, g, theirs)


def _chip_sum(own, received, name):
    half, cols = own.shape
    nb = half // UPD_ROWS

    def body(p_ref, r_ref, o_ref):
        o_ref[...] = ((p_ref[...] + r_ref[0].astype(F32)) + r_ref[1].astype(F32)) + r_ref[2].astype(F32)

    return pl.pallas_call(
        body, name=name, grid=(nb,),
        in_specs=[pl.BlockSpec((UPD_ROWS, cols), lambda i: (i, 0)),
                  pl.BlockSpec((3, UPD_ROWS, cols), lambda i: (0, i, 0))],
        out_specs=pl.BlockSpec((UPD_ROWS, cols), lambda i: (i, 0)),
        out_shape=jax.ShapeDtypeStruct((half, cols), F32),
        compiler_params=_params(("parallel",)),
    )(own, received)


def _adamw(w, g, m, v, *, rows, name):
    r, c = w.shape
    rows = min(rows, r)
    assert r % rows == 0

    def body(w_ref, g_ref, m_ref, v_ref, d_ref, nm_ref, nv_ref):
        gv = g_ref[...]
        nm = ADAM_B1 * m_ref[...] + (1.0 - ADAM_B1) * gv
        nv = ADAM_B2 * v_ref[...] + (1.0 - ADAM_B2) * (gv * gv)
        m_hat = nm / (1.0 - ADAM_B1 ** ADAM_STEP)
        v_hat = nv / (1.0 - ADAM_B2 ** ADAM_STEP)
        d_ref[...] = -ADAM_LR * (m_hat / (jnp.sqrt(v_hat) + ADAM_EPS) + ADAM_WD * w_ref[...])
        nm_ref[...] = nm
        nv_ref[...] = nv

    spec = pl.BlockSpec((rows, c), lambda i: (i, 0))
    return pl.pallas_call(
        body, name=name, grid=(r // rows,),
        in_specs=[spec] * 4, out_specs=[spec] * 3,
        out_shape=[jax.ShapeDtypeStruct((r, c), F32)] * 3,
        compiler_params=_params(("parallel",)),
    )(w, g, m, v)


W_NAMES = ("a_w_in", "a_w_out", "b_w_in", "b_q_norm", "b_w_uq", "b_w_out", "kv_w_down", "kv_norm", "kv_w_up", "ln_g", "ln_b")
BIG = tuple(name for name, _ in SHARD_ROWS)


def _pack_small(ln_g, ln_b, q_norm, kv_norm, extra=None):
    pad = lambda a: jnp.pad(a.reshape(1, -1), ((0, 0), (0, FLAT_COLS - a.size)))
    rows = [ln_g, ln_b, pad(q_norm), pad(kv_norm),
            jnp.zeros((1, FLAT_COLS), F32) if extra is None else pad(extra), jnp.zeros((1, FLAT_COLS), F32)]
    return jnp.concatenate(rows, axis=0)


def _unpack_small(p):
    return dict(ln_g=p[0:2], ln_b=p[2:4], b_q_norm=p[4:5, :Q_LORA], kv_norm=p[5, :KV_LORA])


def kernel(x, a_w_in, a_w_out, b_w_in, b_q_norm, b_w_uq, b_w_out, kv_w_down, kv_norm, kv_w_up, ln_g, ln_b, loss_target, m_a_w_in, m_a_w_out, m_b_w_in, m_b_q_norm, m_b_w_uq, m_b_w_out, m_kv_w_down, m_kv_norm, m_kv_w_up, m_ln_g, m_ln_b, v_a_w_in, v_a_w_out, v_b_w_in, v_b_q_norm, v_b_w_uq, v_b_w_out, v_kv_w_down, v_kv_norm, v_kv_w_up, v_ln_g, v_ln_b):
    w_in = dict(a_w_in=a_w_in[0], a_w_out=a_w_out[0], b_w_in=b_w_in[0], b_w_uq=b_w_uq[0], b_w_out=b_w_out[0],
                kv_w_down=kv_w_down, kv_w_up=kv_w_up)
    m_in = dict(a_w_in=m_a_w_in[0], a_w_out=m_a_w_out[0], b_w_in=m_b_w_in[0], b_w_uq=m_b_w_uq[0], b_w_out=m_b_w_out[0],
                kv_w_down=m_kv_w_down, kv_w_up=m_kv_w_up)
    v_in = dict(a_w_in=v_a_w_in[0], a_w_out=v_a_w_out[0], b_w_in=v_b_w_in[0], b_w_uq=v_b_w_uq[0], b_w_out=v_b_w_out[0],
                kv_w_down=v_kv_w_down, kv_w_up=v_kv_w_up)
    shard_shapes = {name: w_in[name].shape for name in BIG}

    cx, cy, cc = lax.axis_index("x"), lax.axis_index("y"), lax.axis_index("c")
    chip = 2 * cx + cy
    flat_b = _flat_shards(w_in, BF16, B_SHARDS, FLAT_ROWS - A_ROWS)
    a16 = w_in["a_w_in"].astype(BF16)
    got_a = lax.dynamic_update_slice(_gather_weights(a16), a16[None], (chip, 0, 0))
    weights_a = dict(a_in=got_a.transpose(1, 0, 2).reshape(D_MODEL, IN_A))

    core_arr, chip_arr = cc.astype(jnp.int32).reshape(1), chip.astype(jnp.int32).reshape(1)

    def reduce_pair(full, shards, rows, tag):
        g_all = _gathered_from_full(full, shards, rows)
        half, cols = g_all.shape[1] // 2, g_all.shape[2]
        other_half = lax.dynamic_slice(g_all, (0, (1 - cc) * half, 0), (N_CHIPS, half, cols)).astype(BF16)
        theirs = _pair_swap(other_half, "pair_exchange_" + tag)
        return _pair_sum(g_all, theirs, core_arr, chip_arr, "pair_sum_" + tag)

    loss, grad_x, reduced, small = _local_step(x[0], loss_target[0], weights_a, kv_norm.reshape(1, -1), b_q_norm, ln_g, ln_b,
                                               late=(flat_b, chip), reduce=reduce_pair)

    g_big = {}
    for (own, received), shards, tag in zip(reduced, (EARLY_SHARDS, LATE_SHARDS), ("early", "late")):
        mine = _chip_sum(own, received, "chip_sum_" + tag)
        sibling = _pair_swap(mine, "pair_share_" + tag)
        g_flat = jnp.concatenate([jnp.where(cc == 0, mine, sibling), jnp.where(cc == 0, sibling, mine)], axis=0)
        if g_flat.shape == shard_shapes[shards[0][0]]:
            g_big[shards[0][0]] = g_flat
        else:
            g_big.update(_unflat_shards(g_flat, shard_shapes, shards))

    small_sum = _sum_small(_pack_small(small["ln_g"], small["ln_b"], small["q_norm"], small["kv_norm"], loss[:, :1]))
    loss_out = small_sum[6, 0]

    row = lambda a: a.reshape(1, -1)
    g_all = {**g_big, **_unpack_small(small_sum)}
    g_all["kv_norm"] = row(g_all["kv_norm"])
    state = {**{name: (w_in[name], m_in[name], v_in[name]) for name in BIG},
             "ln_g": (ln_g, m_ln_g, v_ln_g), "ln_b": (ln_b, m_ln_b, v_ln_b), "b_q_norm": (b_q_norm, m_b_q_norm, v_b_q_norm),
             "kv_norm": (row(kv_norm), row(m_kv_norm), row(v_kv_norm))}
    upd = {name: _adamw(state[name][0], g_all[name], state[name][1], state[name][2], rows=256, name="adamw_" + name)
           for name in W_NAMES}

    def shaped(name, a):
        if name == "kv_norm":
            return a.reshape(-1)
        return a[None] if name in ("a_w_in", "a_w_out", "b_w_in", "b_w_uq", "b_w_out") else a

    outputs = [[shaped(name, g_all[name]) for name in W_NAMES]]
    outputs += [[shaped(name, upd[name][k]) for name in W_NAMES] for k in range(3)]
    return (loss_out, grad_x[None], *outputs[0], *outputs[1], *outputs[2], *outputs[3])
```

```python
import functools
import math

import jax
import jax.numpy as jnp
from jax import lax
from jax.experimental import pallas as pl
from jax.experimental.pallas import tpu as pltpu

F32 = jnp.float32
BF16 = jnp.bfloat16
MESH = pl.DeviceIdType.MESH

D_MODEL = 1024
DEPTH = 2
H_A, DK_A, DV_A = 4, 256, 512
WIDTH_A = H_A * DV_A
CHUNK = 128
H_B, QK_NOPE, QK_ROPE, V_HEAD = 16, 128, 64, 128
QK_PAD = 256
Q_LORA, KV_LORA = 768, 512
KV_DOWN_PAD = 640
WIDTH_B = H_B * V_HEAD
IN_A = 2 * H_A * DK_A + 2 * WIDTH_A
IN_B = Q_LORA + WIDTH_B
H1_B = KV_DOWN_PAD + IN_B
ROPE_BASE = 10000.0
ALPHA = (2.0 * DEPTH) ** 0.25
ATT_SCALE = (QK_NOPE + QK_ROPE) ** -0.5
NEG_BIG = -1e30

ADAM_LR, ADAM_B1, ADAM_B2, ADAM_EPS, ADAM_WD, ADAM_STEP = 0.001, 0.9, 0.999, 1e-08, 0.01, 10

VMEM_LIMIT_BYTES = 56 * 1024 * 1024
LANES = 128
FLAT_COLS = 1024
SHARD_ROWS = (("a_w_in", 1536), ("a_w_out", 512), ("b_w_in", 704), ("b_w_uq", 576), ("b_w_out", 512),
              ("kv_w_down", 144), ("kv_w_up", 512))
FLAT_ROWS = 4608
N_CHIPS = 4
N_DEV = 8


def _params(sem, vmem=VMEM_LIMIT_BYTES):
    return pltpu.CompilerParams(dimension_semantics=sem, vmem_limit_bytes=vmem)


def _row_spec(ts, w, col_block=0):
    return pl.BlockSpec((ts, w), lambda i: (i, col_block))


def _bc_spec(shape):
    nd = len(shape)
    return pl.BlockSpec(shape, lambda i: (0,) * nd)


def _sigmoid(x):
    return 1.0 / (1.0 + jnp.exp(-x))


def _fold8(v):
    ts, w = v.shape
    return jnp.sum(v.reshape(ts // 8, 8, w), axis=0)


def _mm(a, b, *, ta=False, tb=False, out_dtype=F32, tm=1024, tn=512, tk=None, name, extras=(), epilogue=None, out_tn=None,
        side=None):
    if ta:
        K, M = a.shape
    else:
        M, K = a.shape
    if tb:
        N, Kb = b.shape
    else:
        Kb, N = b.shape
    assert K == Kb, (a.shape, b.shape)
    tm, tn = min(tm, M), min(tn, N)
    tk = K if tk is None else min(tk, K)
    assert M % tm == 0 and N % tn == 0 and K % tk == 0, (name, M, N, K, tm, tn, tk)
    grid = (M // tm, N // tn, K // tk)
    nk = grid[2]
    out_tn = tn if out_tn is None else out_tn
    n_extra = len(extras)
    side_inputs = [] if side is None else list(side["inputs"])
    n_side = len(side_inputs)
    n_acc = 0 if nk == 1 else 1
    dims = (((0,) if ta else (1,), (1,) if tb else (0,)), ((), ()))

    def body(a_ref, b_ref, *rest):
        extra_refs, rest = rest[:n_extra], rest[n_extra:]
        side_in, o_ref, rest = rest[:n_side], rest[n_side], rest[n_side + 1:]
        if side is not None:
            side_refs, rest = side_in + rest[:1] + rest[1 + n_acc:], rest[1:]
            ids = [pl.program_id(d) for d in range(3)]

            @pl.when((ids[0] == 0) & (ids[1] == 0) & (ids[2] == 0))
            def _():
                side["start"](*side_refs)

        prod = lax.dot_general(a_ref[...].astype(BF16), b_ref[...].astype(BF16), dims,
                               preferred_element_type=F32)

        def store(tile):
            if epilogue is None:
                o_ref[...] = tile.astype(o_ref.dtype)
            else:
                epilogue(tile, o_ref, *extra_refs)

        if nk == 1:
            store(prod)
        else:
            acc = rest[0]
            k = pl.program_id(2)

            @pl.when(k == 0)
            def _():
                acc[...] = prod

            @pl.when(k > 0)
            def _():
                acc[...] += prod

            @pl.when(k == nk - 1)
            def _():
                store(acc[...])

        if side is not None:
            @pl.when((ids[0] == grid[0] - 1) & (ids[1] == grid[1] - 1) & (ids[2] == grid[2] - 1))
            def _():
                side["finish"](*side_refs)

    a_spec = pl.BlockSpec((tk, tm), lambda i, j, k: (k, i)) if ta else pl.BlockSpec((tm, tk), lambda i, j, k: (i, k))
    b_spec = pl.BlockSpec((tn, tk), lambda i, j, k: (j, k)) if tb else pl.BlockSpec((tk, tn), lambda i, j, k: (k, j))
    extra_specs = [pl.BlockSpec((tm, e.shape[1]), lambda i, j, k: (i, 0)) for e in extras]
    out_specs = pl.BlockSpec((tm, out_tn), lambda i, j, k: (i, j))
    out_shape = jax.ShapeDtypeStruct((M, (N // tn) * out_tn), out_dtype)
    scratch = [] if nk == 1 else [pltpu.VMEM((tm, tn), F32)]
    if side is not None:
        out_specs, out_shape, scratch = [out_specs, HBM_SPEC], [out_shape, side["out_shape"]], scratch + list(side["scratch"])
    return pl.pallas_call(
        body, name=name, grid=grid,
        in_specs=[a_spec, b_spec] + extra_specs + [HBM_SPEC] * n_side,
        out_specs=out_specs, out_shape=out_shape, scratch_shapes=scratch,
        compiler_params=_params(("parallel", "parallel", "arbitrary") if side is None else ("arbitrary",) * 3),
    )(a, b, *extras, *side_inputs)


def _rope_tables_a(s):
    half = DK_A // 2
    inv = ROPE_BASE ** (-jnp.arange(half, dtype=F32) / half)
    ang = jnp.arange(s, dtype=F32)[:, None] * inv[None, :]
    return jnp.cos(ang), jnp.sin(ang)


def _rope_tables_b(s):
    half = QK_ROPE // 2
    inv = ROPE_BASE ** (-jnp.arange(half, dtype=F32) / half)
    ang = jnp.arange(s, dtype=F32)[:, None] * inv[None, :]
    c, sn = jnp.cos(ang), jnp.sin(ang)
    z = jnp.zeros_like(c)
    cos = jnp.concatenate([c, c, z, z], axis=1)
    sa = jnp.concatenate([-sn, z, z, z], axis=1)
    sb = jnp.concatenate([z, sn, z, z], axis=1)
    return cos, sa, sb


def _rope_b(r, cos, sa, sb, sign):
    return r * cos + sign * (pltpu.roll(r, 96, 1) * sa + pltpu.roll(r, 32, 1) * sb)


def _retention_tables():
    lg = jnp.log1p(-jnp.exp2(-5.0 - jnp.arange(H_A, dtype=F32)))
    idx = jnp.arange(CHUNK, dtype=F32)
    diff = idx[:, None] - idx[None, :]
    causal = diff >= 0
    dmat = jnp.where(causal, jnp.exp(jnp.where(causal, diff, 0.0)[None] * lg[:, None, None]), 0.0)
    qdec = jnp.exp((idx + 1.0)[None, :] * lg[:, None])[:, :, None]
    kdec = jnp.exp((CHUNK - 1.0 - idx)[None, :] * lg[:, None])[:, :, None]
    cdec = jnp.broadcast_to(jnp.exp(CHUNK * lg)[:, None, None], (H_A, 1, DV_A))
    return dmat, qdec, kdec, cdec


def _group_norm(o):
    mu = jnp.mean(o, axis=-1, keepdims=True)
    oc = o - mu
    var = jnp.mean(oc * oc, axis=-1, keepdims=True)
    rstd = lax.rsqrt(var + 1e-5)
    return oc * rstd, rstd


Q_COL, K_COL, V_COL, GATE_COL = 0, H_A * DK_A, 2 * H_A * DK_A, 2 * H_A * DK_A + WIDTH_A


def _ln_stats(z):
    mu = jnp.mean(z, axis=-1, keepdims=True)
    zc = z - mu
    var = jnp.mean(zc * zc, axis=-1, keepdims=True)
    rstd = lax.rsqrt(var + 1e-5)
    return zc * rstd, rstd


def _ln_bwd(dy, xhat, rstd, g):
    dxh = dy * g
    m1 = jnp.mean(dxh, axis=-1, keepdims=True)
    m2 = jnp.mean(dxh * xhat, axis=-1, keepdims=True)
    return rstd * (dxh - m1 - xhat * m2)


def _ln_fwd(x, y, g, b, *, ts):
    s = x.shape[0]

    def body(x_ref, y_ref, g_ref, b_ref, o_ref, o16_ref):
        xhat, _ = _ln_stats(ALPHA * x_ref[...] + y_ref[...])
        out = xhat * g_ref[...] + b_ref[...]
        o_ref[...] = out
        o16_ref[...] = out.astype(o16_ref.dtype)

    return pl.pallas_call(
        body, name="ln_fwd", grid=(s // ts,),
        in_specs=[_row_spec(ts, D_MODEL), _row_spec(ts, D_MODEL), _bc_spec((1, D_MODEL)), _bc_spec((1, D_MODEL))],
        out_specs=[_row_spec(ts, D_MODEL), _row_spec(ts, D_MODEL)],
        out_shape=[jax.ShapeDtypeStruct((s, D_MODEL), F32), jax.ShapeDtypeStruct((s, D_MODEL), BF16)],
        compiler_params=_params(("parallel",)),
    )(x, y, g, b)


def _ln_loss_bwd(x1, y, target, g, b, *, ts):
    s = x1.shape[0]
    n = s // ts

    def body(x_ref, y_ref, t_ref, g_ref, b_ref, dz_ref, dz16_ref, dg_ref, db_ref, loss_ref, ag, ab, al):
        i = pl.program_id(0)

        @pl.when(i == 0)
        def _():
            ag[...] = jnp.zeros_like(ag)
            ab[...] = jnp.zeros_like(ab)
            al[...] = jnp.zeros_like(al)

        xhat, rstd = _ln_stats(ALPHA * x_ref[...] + y_ref[...])
        err = xhat * g_ref[...] + b_ref[...] - t_ref[...]
        al[...] += _fold8(err * err)
        dy = err * (1.0 / D_MODEL)
        ag[...] += _fold8(dy * xhat)
        ab[...] += _fold8(dy)
        dz = _ln_bwd(dy, xhat, rstd, g_ref[...])
        dz_ref[...] = dz
        dz16_ref[...] = dz.astype(dz16_ref.dtype)

        @pl.when(i == n - 1)
        def _():
            dg_ref[...] = jnp.sum(ag[...], axis=0, keepdims=True)
            db_ref[...] = jnp.sum(ab[...], axis=0, keepdims=True)
            loss_ref[...] = jnp.full((1, LANES), (0.5 / D_MODEL) * jnp.sum(al[...]), F32)

    return pl.pallas_call(
        body, name="ln_loss_bwd", grid=(n,),
        in_specs=[_row_spec(ts, D_MODEL)] * 3 + [_bc_spec((1, D_MODEL))] * 2,
        out_specs=[_row_spec(ts, D_MODEL), _row_spec(ts, D_MODEL), _bc_spec((1, D_MODEL)), _bc_spec((1, D_MODEL)),
                   _bc_spec((1, LANES))],
        out_shape=[jax.ShapeDtypeStruct((s, D_MODEL), F32), jax.ShapeDtypeStruct((s, D_MODEL), BF16),
                   jax.ShapeDtypeStruct((1, D_MODEL), F32), jax.ShapeDtypeStruct((1, D_MODEL), F32),
                   jax.ShapeDtypeStruct((1, LANES), F32)],
        scratch_shapes=[pltpu.VMEM((8, D_MODEL), F32)] * 3,
        compiler_params=_params(("arbitrary",)),
    )(x1, y, target, g, b)


def _ln_bwd_call(dy, x, y, g, *, ts):
    s = x.shape[0]
    n = s // ts

    def body(dy_ref, x_ref, y_ref, g_ref, dz_ref, dz16_ref, dg_ref, db_ref, ag, ab):
        i = pl.program_id(0)

        @pl.when(i == 0)
        def _():
            ag[...] = jnp.zeros_like(ag)
            ab[...] = jnp.zeros_like(ab)

        xhat, rstd = _ln_stats(ALPHA * x_ref[...] + y_ref[...])
        dy = dy_ref[...]
        ag[...] += _fold8(dy * xhat)
        ab[...] += _fold8(dy)
        dz = _ln_bwd(dy, xhat, rstd, g_ref[...])
        dz_ref[...] = dz
        dz16_ref[...] = dz.astype(dz16_ref.dtype)

        @pl.when(i == n - 1)
        def _():
            dg_ref[...] = jnp.sum(ag[...], axis=0, keepdims=True)
            db_ref[...] = jnp.sum(ab[...], axis=0, keepdims=True)

    return pl.pallas_call(
        body, name="ln_bwd", grid=(n,),
        in_specs=[_row_spec(ts, D_MODEL)] * 3 + [_bc_spec((1, D_MODEL))],
        out_specs=[_row_spec(ts, D_MODEL), _row_spec(ts, D_MODEL), _bc_spec((1, D_MODEL)), _bc_spec((1, D_MODEL))],
        out_shape=[jax.ShapeDtypeStruct((s, D_MODEL), F32), jax.ShapeDtypeStruct((s, D_MODEL), BF16),
                   jax.ShapeDtypeStruct((1, D_MODEL), F32), jax.ShapeDtypeStruct((1, D_MODEL), F32)],
        scratch_shapes=[pltpu.VMEM((8, D_MODEL), F32)] * 2,
        compiler_params=_params(("arbitrary",)),
    )(dy, x, y, g)


def _residual_store(tile, o_ref, dz_ref):
    o_ref[...] = ALPHA * dz_ref[...] + tile


C_LAT = slice(0, KV_LORA)
C_ROPE = slice(KV_LORA, KV_DOWN_PAD)
C_QL = slice(KV_DOWN_PAD, KV_DOWN_PAD + Q_LORA)
C_GATE = slice(KV_DOWN_PAD + Q_LORA, H1_B)


def _rms(x, eps=1e-6):
    r = lax.rsqrt(jnp.mean(x * x, axis=-1, keepdims=True) + eps)
    return x * r, r


def _rms_bwd(dy, xhat, r, g):
    dxh = dy * g
    return r * (dxh - xhat * jnp.mean(dxh * xhat, axis=-1, keepdims=True))


def _kvq_prep(h1, kv_norm, q_norm, cos, sa, sb, *, ts):
    s = h1.shape[0]

    def body(h_ref, kn_ref, qn_ref, c_ref, sa_ref, sb_ref, lat_ref, kr_ref, ql_ref):
        lat, _ = _rms(h_ref[:, C_LAT])
        lat_ref[...] = (lat * kn_ref[...]).astype(lat_ref.dtype)
        kr_ref[...] = _rope_b(h_ref[:, C_ROPE], c_ref[...], sa_ref[...], sb_ref[...], 1.0).astype(kr_ref.dtype)
        ql, _ = _rms(h_ref[:, C_QL])
        ql_ref[...] = (ql * qn_ref[...]).astype(ql_ref.dtype)

    return pl.pallas_call(
        body, name="kvq_prep", grid=(s // ts,),
        in_specs=[_row_spec(ts, H1_B), _bc_spec((1, KV_LORA)), _bc_spec((1, Q_LORA))] + [_row_spec(ts, 128)] * 3,
        out_specs=[_row_spec(ts, KV_LORA), _row_spec(ts, 128), _row_spec(ts, Q_LORA)],
        out_shape=[jax.ShapeDtypeStruct((s, KV_LORA), BF16), jax.ShapeDtypeStruct((s, 128), BF16),
                   jax.ShapeDtypeStruct((s, Q_LORA), BF16)],
        compiler_params=_params(("parallel",)),
    )(h1, kv_norm, q_norm, cos, sa, sb)


LOG2E = 1.4426950408889634
LN2 = 0.6931471805599453
Q_SCALE = ATT_SCALE * LOG2E


def _rope_q_store(tile, o_ref, c_ref, sa_ref, sb_ref):
    c, a, b = c_ref[...], sa_ref[...], sb_ref[...]
    for hd in range(tile.shape[1] // QK_PAD):
        lo = hd * QK_PAD
        o_ref[:, lo:lo + 128] = (tile[:, lo:lo + 128] * Q_SCALE).astype(o_ref.dtype)
        o_ref[:, lo + 128:lo + 256] = (_rope_b(tile[:, lo + 128:lo + 256], c, a, b, 1.0) * Q_SCALE).astype(o_ref.dtype)


def _assemble_k_store(tile, o_ref, kr_ref):
    r = kr_ref[...]
    for hd in range(tile.shape[1] // QK_NOPE):
        o_ref[:, hd * QK_PAD:hd * QK_PAD + 128] = tile[:, hd * 128:(hd + 1) * 128].astype(o_ref.dtype)
        o_ref[:, hd * QK_PAD + 128:(hd + 1) * QK_PAD] = r


def _h1_bwd(h1, dlat_k, dlat_v, dkr_heads, dqn, dg16, kv_norm, q_norm, cos, sa, sb, *, ts):
    s = h1.shape[0]
    n = s // ts

    def body(h_ref, dk_ref, dv_ref, dkr_ref, dqn_ref, dg_ref, kn_ref, qn_ref, c_ref, sa_ref, sb_ref,
             o_ref, dkn_ref, dqn_out_ref, akn, aqn):
        i = pl.program_id(0)

        @pl.when(i == 0)
        def _():
            akn[...] = jnp.zeros_like(akn)
            aqn[...] = jnp.zeros_like(aqn)

        lat, r = _rms(h_ref[:, C_LAT])
        dlat = dk_ref[...] + dv_ref[...]
        akn[...] += _fold8(dlat * lat)
        o_ref[:, C_LAT] = _rms_bwd(dlat, lat, r, kn_ref[...]).astype(o_ref.dtype)

        dkr = dkr_ref[:, 0:128]
        for hd in range(1, H_B):
            dkr = dkr + dkr_ref[:, hd * 128:(hd + 1) * 128]
        o_ref[:, C_ROPE] = _rope_b(dkr, c_ref[...], sa_ref[...], sb_ref[...], -1.0).astype(o_ref.dtype)

        ql, rq = _rms(h_ref[:, C_QL])
        dq = dqn_ref[...]
        aqn[...] += _fold8(dq * ql)
        o_ref[:, C_QL] = _rms_bwd(dq, ql, rq, qn_ref[...]).astype(o_ref.dtype)
        o_ref[:, C_GATE] = dg_ref[...]

        @pl.when(i == n - 1)
        def _():
            dkn_ref[...] = jnp.sum(akn[...], axis=0, keepdims=True)
            dqn_out_ref[...] = jnp.sum(aqn[...], axis=0, keepdims=True)

    return pl.pallas_call(
        body, name="h1_bwd", grid=(n,),
        in_specs=[_row_spec(ts, H1_B), _row_spec(ts, KV_LORA), _row_spec(ts, KV_LORA), _row_spec(ts, H_B * 128),
                  _row_spec(ts, Q_LORA), _row_spec(ts, WIDTH_B), _bc_spec((1, KV_LORA)), _bc_spec((1, Q_LORA))]
        + [_row_spec(ts, 128)] * 3,
        out_specs=[_row_spec(ts, H1_B), _bc_spec((1, KV_LORA)), _bc_spec((1, Q_LORA))],
        out_shape=[jax.ShapeDtypeStruct((s, H1_B), BF16), jax.ShapeDtypeStruct((1, KV_LORA), F32),
                   jax.ShapeDtypeStruct((1, Q_LORA), F32)],
        scratch_shapes=[pltpu.VMEM((8, KV_LORA), F32), pltpu.VMEM((8, Q_LORA), F32)],
        compiler_params=_params(("arbitrary",)),
    )(h1, dlat_k, dlat_v, dkr_heads, dqn, dg16, kv_norm, q_norm, cos, sa, sb)


def _dot(a, b, ca, cb):
    return lax.dot_general(a, b, (((ca,), (cb,)), ((), ())), preferred_element_type=F32)


CHUNKS_PER_STEP = 2


def _table_specs():
    full = lambda shape: pl.BlockSpec(shape, lambda i: (0,) * len(shape))
    return [full((H_A, CHUNK, CHUNK)), full((H_A, CHUNK, 1)), full((H_A, CHUNK, 1)), full((H_A, 1, DV_A))]


def _retention_fwd(h_a, cos, sin, tables):
    s = h_a.shape[0]
    n = s // CHUNK

    def body(h_ref, c_ref, s_ref, dm_ref, qd_ref, kd_ref, cd_ref, q_ref, k_ref, v_ref, o_ref, u_ref, st_ref, state):
        @pl.when(pl.program_id(0) == 0)
        def _():
            state[...] = jnp.zeros_like(state)

        for cb in range(CHUNKS_PER_STEP):
            rows = slice(cb * CHUNK, (cb + 1) * CHUNK)
            c, sn = c_ref[rows, :], s_ref[rows, :]
            for hd in range(H_A):
                qs, vs = slice(hd * DK_A, (hd + 1) * DK_A), slice(hd * DV_A, (hd + 1) * DV_A)
                for r_ref, base, scale in ((q_ref, Q_COL, 1.0), (k_ref, K_COL, DK_A ** -0.5)):
                    lo = base + hd * DK_A
                    x1, x2 = h_ref[rows, lo:lo + 128], h_ref[rows, lo + 128:lo + 256]
                    r_ref[rows, hd * DK_A:hd * DK_A + 128] = ((x1 * c - x2 * sn) * scale).astype(r_ref.dtype)
                    r_ref[rows, hd * DK_A + 128:(hd + 1) * DK_A] = ((x2 * c + x1 * sn) * scale).astype(r_ref.dtype)
                v_ref[rows, vs] = h_ref[rows, V_COL + hd * DV_A:V_COL + (hd + 1) * DV_A].astype(v_ref.dtype)
                qv, kv, vv = q_ref[rows, qs], k_ref[rows, qs], v_ref[rows, vs]
                st = state[hd]
                st16 = st.astype(BF16)
                st_ref[cb, hd] = st16
                scores = _dot(qv, kv, 1, 1) * dm_ref[hd]
                qd = (qv.astype(F32) * qd_ref[hd]).astype(BF16)
                o = _dot(scores.astype(BF16), vv, 1, 0) + _dot(qd, st16, 1, 0)
                o_ref[rows, vs] = o
                kd = (kv.astype(F32) * kd_ref[hd]).astype(BF16)
                state[hd] = st * cd_ref[hd] + _dot(kd, vv, 0, 0)
                on, _ = _group_norm(o)
                g = h_ref[rows, GATE_COL + hd * DV_A:GATE_COL + (hd + 1) * DV_A]
                u_ref[rows, vs] = (on * (g * _sigmoid(g))).astype(u_ref.dtype)

    row = lambda w: pl.BlockSpec((CHUNKS_PER_STEP * CHUNK, w), lambda i: (i, 0))
    return pl.pallas_call(
        body, name="retention_fwd", grid=(n // CHUNKS_PER_STEP,),
        in_specs=[row(IN_A), row(128), row(128)] + _table_specs(),
        out_specs=[row(H_A * DK_A), row(H_A * DK_A), row(WIDTH_A), row(WIDTH_A), row(WIDTH_A),
                   pl.BlockSpec((CHUNKS_PER_STEP, H_A, DK_A, DV_A), lambda i: (i, 0, 0, 0))],
        out_shape=[jax.ShapeDtypeStruct((s, H_A * DK_A), BF16), jax.ShapeDtypeStruct((s, H_A * DK_A), BF16),
                   jax.ShapeDtypeStruct((s, WIDTH_A), BF16), jax.ShapeDtypeStruct((s, WIDTH_A), F32),
                   jax.ShapeDtypeStruct((s, WIDTH_A), BF16), jax.ShapeDtypeStruct((n, H_A, DK_A, DV_A), BF16)],
        scratch_shapes=[pltpu.VMEM((H_A, DK_A, DV_A), F32)],
        compiler_params=_params(("arbitrary",)),
    )(h_a, cos, sin, *tables)


def _retention_bwd(q, k, v, states, du, o, h_a, cos, sin, tables):
    s = q.shape[0]
    n = s // CHUNK

    def body(q_ref, k_ref, v_ref, st_ref, du_ref, o_ref, g_ref, c_ref, s_ref, dm_ref, qd_ref, kd_ref, cd_ref, dh_ref, grad_state):
        @pl.when(pl.program_id(0) == 0)
        def _():
            grad_state[...] = jnp.zeros_like(grad_state)

        for cb in reversed(range(CHUNKS_PER_STEP)):
            rows = slice(cb * CHUNK, (cb + 1) * CHUNK)
            c, sn = c_ref[rows, :], s_ref[rows, :]
            for hd in range(H_A):
                qs, vs = slice(hd * DK_A, (hd + 1) * DK_A), slice(hd * DV_A, (hd + 1) * DV_A)
                on, rstd = _group_norm(o_ref[rows, vs])
                g = g_ref[rows, vs]
                sg = _sigmoid(g)
                du_v = du_ref[rows, vs]
                don = du_v * (g * sg)
                dh_ref[rows, GATE_COL + hd * DV_A:GATE_COL + (hd + 1) * DV_A] = (du_v * on * (sg * (1.0 + g * (1.0 - sg)))).astype(dh_ref.dtype)
                m1 = jnp.mean(don, axis=-1, keepdims=True)
                m2 = jnp.mean(don * on, axis=-1, keepdims=True)
                dov = (rstd * (don - m1 - on * m2)).astype(BF16)

                qv, kv, vv = q_ref[rows, qs], k_ref[rows, qs], v_ref[rows, vs]
                dm = dm_ref[hd]
                gs = grad_state[hd]
                g16 = gs.astype(BF16)
                scores = (_dot(qv, kv, 1, 1) * dm).astype(BF16)
                dscores = (_dot(dov, vv, 1, 1) * dm).astype(BF16)
                qd = (qv.astype(F32) * qd_ref[hd]).astype(BF16)
                kd = (kv.astype(F32) * kd_ref[hd]).astype(BF16)
                dq = _dot(dscores, kv, 1, 0) + _dot(dov, st_ref[cb, hd], 1, 1) * qd_ref[hd]
                dk = (_dot(dscores, qv, 0, 0) + _dot(vv, g16, 1, 1) * kd_ref[hd]) * (DK_A ** -0.5)
                dh_ref[rows, V_COL + hd * DV_A:V_COL + (hd + 1) * DV_A] = (_dot(scores, dov, 0, 0) + _dot(kd, g16, 1, 0)).astype(dh_ref.dtype)
                grad_state[hd] = gs * cd_ref[hd] + _dot(qd, dov, 0, 0)
                for d, base in ((dq, Q_COL), (dk, K_COL)):
                    lo = base + hd * DK_A
                    d1, d2 = d[:, 0:128], d[:, 128:256]
                    dh_ref[rows, lo:lo + 128] = (d1 * c + d2 * sn).astype(dh_ref.dtype)
                    dh_ref[rows, lo + 128:lo + 256] = (d2 * c - d1 * sn).astype(dh_ref.dtype)

    steps = n // CHUNKS_PER_STEP
    rev = lambda i: steps - 1 - i
    row = lambda w, col=0: pl.BlockSpec((CHUNKS_PER_STEP * CHUNK, w), lambda i: (rev(i), col))
    return pl.pallas_call(
        body, name="retention_bwd", grid=(steps,),
        in_specs=[row(H_A * DK_A), row(H_A * DK_A), row(WIDTH_A),
                  pl.BlockSpec((CHUNKS_PER_STEP, H_A, DK_A, DV_A), lambda i: (rev(i), 0, 0, 0)),
                  row(WIDTH_A), row(WIDTH_A), row(WIDTH_A, GATE_COL // WIDTH_A), row(128), row(128)] + _table_specs(),
        out_specs=row(IN_A),
        out_shape=jax.ShapeDtypeStruct((s, IN_A), BF16),
        scratch_shapes=[pltpu.VMEM((H_A, DK_A, DV_A), F32)],
        compiler_params=_params(("arbitrary",)),
    )(q, k, v, states, du, o, h_a, cos, sin, *tables)


GATE_BLOCK0 = (KV_DOWN_PAD + Q_LORA) // LANES


def _causal_mask(sc, row0):
    row = lax.broadcasted_iota(jnp.int32, sc.shape, 0) + row0
    col = lax.broadcasted_iota(jnp.int32, sc.shape, 1)
    return jnp.where(col <= row, sc, NEG_BIG)


def _key_block_loop(step, n, per_trip, smallest=1):
    def trip_body(jj, carry):
        for t in range(per_trip):
            step(per_trip * jj + t)
        return carry

    lax.fori_loop(0, n // per_trip, trip_body, 0)
    group = per_trip // 2
    while group >= smallest:
        def tail(group=group):
            first = (n // (2 * group)) * (2 * group)
            for t in range(group):
                step(first + t)

        pl.when((n // group) % 2 == 1)(tail)
        group //= 2


def _attention_fwd(q, k, v, h1, *, blk, bk, sub, per_trip):
    s = q.shape[0]
    nb = s // blk

    def body(q_ref, k_ref, v_ref, g_ref, o_ref, u_ref, lse_ref, vext_s, m_s, acc_s):
        i = pl.program_id(1)

        @pl.when(i == 0)
        def _():
            vext_s[:, 0:V_HEAD] = v_ref[...]
            vext_s[:, V_HEAD:2 * V_HEAD] = jnp.ones((s, V_HEAD), vext_s.dtype)

        m_s[...] = jnp.full_like(m_s, NEG_BIG)
        acc_s[...] = jnp.zeros_like(acc_s)

        def update(rows, kb, vb, row0):
            sc = _dot(q_ref[rows, :], kb, 1, 1)
            if row0 is not None:
                sc = _causal_mask(sc, row0)
            m_prev = m_s[rows, :]
            m_new = jnp.maximum(m_prev, jnp.max(sc, axis=-1, keepdims=True))
            p = jnp.exp2(sc - jnp.tile(m_new, (1, kb.shape[0] // LANES)))
            a = jnp.exp2(m_prev - m_new)
            acc_s[rows, :] = jnp.tile(a, (1, 2)) * acc_s[rows, :] + _dot(p.astype(BF16), vb, 1, 0)
            m_s[rows, :] = m_new

        def step(j):
            kv_rows = pl.ds(pl.multiple_of(j * bk, bk), bk)
            kb, vb = k_ref[kv_rows, :], vext_s[kv_rows, :]
            for r in range(blk // sub):
                update(slice(r * sub, (r + 1) * sub), kb, vb, None)

        _key_block_loop(step, i * (blk // bk), per_trip, smallest=blk // bk)
        for r in range(blk // sub):
            ncols = (r + 1) * sub
            kv_rows = pl.ds(pl.multiple_of(i * blk, blk), ncols)
            update(slice(r * sub, (r + 1) * sub), k_ref[kv_rows, :], vext_s[kv_rows, :], r * sub)
        acc = acc_s[...]
        l = acc[:, V_HEAD:2 * V_HEAD]
        o = acc[:, 0:V_HEAD] / l
        g = g_ref[...]
        o_ref[...] = o
        u_ref[...] = (o * (g * _sigmoid(g))).astype(u_ref.dtype)
        lse_ref[0] = (m_s[...] + jnp.log2(l))[:, 0:1]

    blk_idx = lambda h, i: (i, h)
    return pl.pallas_call(
        body, name="attention_fwd", grid=(H_B, nb),
        in_specs=[pl.BlockSpec((blk, QK_PAD), blk_idx), pl.BlockSpec((s, QK_PAD), lambda h, i: (0, h)),
                  pl.BlockSpec((s, V_HEAD), lambda h, i: (0, h)), pl.BlockSpec((blk, LANES), lambda h, i: (i, GATE_BLOCK0 + h))],
        out_specs=[pl.BlockSpec((blk, V_HEAD), blk_idx), pl.BlockSpec((blk, V_HEAD), blk_idx),
                   pl.BlockSpec((1, blk, 1), lambda h, i: (h, i, 0))],
        out_shape=[jax.ShapeDtypeStruct((s, WIDTH_B), F32), jax.ShapeDtypeStruct((s, WIDTH_B), BF16),
                   jax.ShapeDtypeStruct((H_B, s, 1), F32)],
        scratch_shapes=[pltpu.VMEM((s, 2 * V_HEAD), BF16), pltpu.VMEM((blk, LANES), F32), pltpu.VMEM((blk, 2 * V_HEAD), F32)],
        compiler_params=_params(("parallel", "arbitrary")),
    )(q, k, v, h1)


def _attention_bwd(q, k, v, du, o, h1, lse, cos, sa, sb, *, blk, bk, per_trip):
    s = q.shape[0]
    nb = s // blk

    def body(q_ref, k_ref, v_ref, du_ref, o_ref, g_ref, lse_ref, c_ref, sa_ref, sb_ref,
             dq_ref, dkn_ref, dkr_ref, dv_ref, dg_ref, lse_s, dl_s, do_s, dq_s):
        i = pl.program_id(1)
        g = g_ref[...]
        sg = _sigmoid(g)
        du_v, ov = du_ref[...], o_ref[...]
        do = du_v * (g * sg)
        dg_ref[...] = (du_v * ov * (sg * (1.0 + g * (1.0 - sg)))).astype(dg_ref.dtype)
        do_s[...] = do.astype(do_s.dtype)
        dl_s[...] = jnp.broadcast_to(jnp.sum(do * ov, axis=-1, keepdims=True), (blk, LANES))
        lse_s[...] = jnp.broadcast_to(lse_ref[0], (blk, LANES))
        dq_s[...] = jnp.zeros_like(dq_s)

        def products(rows, kv_rows, row0):
            qv, dov, kb = q_ref[rows, :], do_s[rows, :], k_ref[kv_rows, :]
            tile = (1, kb.shape[0] // LANES)
            sc = _dot(qv, kb, 1, 1)
            if row0 is not None:
                sc = _causal_mask(sc, row0)
            p = jnp.exp2(sc - jnp.tile(lse_s[rows, :], tile))
            dp = _dot(dov, v_ref[kv_rows, :], 1, 1)
            ds = (p * (dp - jnp.tile(dl_s[rows, :], tile))).astype(BF16)
            dq_s[rows, :] += _dot(ds, kb, 1, 0)
            return _dot(ds, qv, 0, 0), _dot(p.astype(BF16), dov, 0, 0)

        def put(kv_rows, dk_c, dv_c, first):
            if first:
                dkn_ref[kv_rows, :] = dk_c[:, 0:128]
                dkr_ref[kv_rows, :] = dk_c[:, 128:256]
                dv_ref[kv_rows, :] = dv_c
            else:
                dkn_ref[kv_rows, :] += dk_c[:, 0:128]
                dkr_ref[kv_rows, :] += dk_c[:, 128:256]
                dv_ref[kv_rows, :] += dv_c

        n_sub = blk // bk
        sub_rows = [slice(r * bk, (r + 1) * bk) for r in range(n_sub)]

        def step(j):
            kv_rows = pl.ds(pl.multiple_of(j * bk, bk), bk)
            for rows in sub_rows:
                dk_c, dv_c = products(rows, kv_rows, None)
                put(kv_rows, dk_c, dv_c, False)

        _key_block_loop(step, i * n_sub, per_trip, smallest=n_sub)
        for c in range(n_sub):
            kv_rows = pl.ds(pl.multiple_of(i * blk + c * bk, bk), bk)
            for r in range(c, n_sub):
                dk_c, dv_c = products(sub_rows[r], kv_rows, 0 if r == c else None)
                put(kv_rows, dk_c, dv_c, r == c)
        dq = dq_s[...] * ATT_SCALE
        dq_ref[:, 0:128] = dq[:, 0:128].astype(dq_ref.dtype)
        dq_ref[:, 128:256] = _rope_b(dq[:, 128:256], c_ref[...], sa_ref[...], sb_ref[...], -1.0).astype(dq_ref.dtype)

        @pl.when(i == nb - 1)
        def _():
            dkn_ref[...] = dkn_ref[...] * LN2
            dkr_ref[...] = dkr_ref[...] * LN2

    head = lambda h, i: (0, h)
    blk_idx = lambda h, i: (i, h)
    row_idx = lambda h, i: (i, 0)
    return pl.pallas_call(
        body, name="attention_bwd", grid=(H_B, nb),
        in_specs=[pl.BlockSpec((blk, QK_PAD), blk_idx), pl.BlockSpec((s, QK_PAD), head), pl.BlockSpec((s, V_HEAD), head),
                  pl.BlockSpec((blk, V_HEAD), blk_idx), pl.BlockSpec((blk, V_HEAD), blk_idx),
                  pl.BlockSpec((blk, LANES), lambda h, i: (i, GATE_BLOCK0 + h)), pl.BlockSpec((1, blk, 1), lambda h, i: (h, i, 0)),
                  pl.BlockSpec((blk, LANES), row_idx), pl.BlockSpec((blk, LANES), row_idx), pl.BlockSpec((blk, LANES), row_idx)],
        out_specs=[pl.BlockSpec((blk, QK_PAD), blk_idx), pl.BlockSpec((s, 128), head), pl.BlockSpec((s, 128), head),
                   pl.BlockSpec((s, 128), head), pl.BlockSpec((blk, V_HEAD), blk_idx)],
        out_shape=[jax.ShapeDtypeStruct((s, H_B * QK_PAD), BF16), jax.ShapeDtypeStruct((s, H_B * 128), F32),
                   jax.ShapeDtypeStruct((s, H_B * 128), F32), jax.ShapeDtypeStruct((s, H_B * 128), F32),
                   jax.ShapeDtypeStruct((s, WIDTH_B), BF16)],
        scratch_shapes=[pltpu.VMEM((blk, LANES), F32), pltpu.VMEM((blk, LANES), F32), pltpu.VMEM((blk, V_HEAD), BF16),
                        pltpu.VMEM((blk, QK_PAD), F32)],
        compiler_params=_params(("parallel", "arbitrary")),
    )(q, k, v, du, o, h1, lse, cos, sa, sb)


def _local_step(x, target, w, kv_norm, q_norm, ln_g, ln_b, *, ts=512, blk=512, late=None, reduce=None):
    s = x.shape[0]
    cos_a, sin_a = _rope_tables_a(s)
    cos_b, sa_b, sb_b = _rope_tables_b(s)
    tables = _retention_tables()
    g0, g1, b0, b1 = ln_g[0:1], ln_g[1:2], ln_b[0:1], ln_b[1:2]

    x16 = x.astype(BF16)
    if late is None:
        h_a = _mm(x16, w["a_in"], tn=1536, name="a_in_fwd")
    else:
        flat_b, chip = late
        h_a, got = _mm(x16, w["a_in"], tn=1536, name="a_in_fwd", side=_gather_side(flat_b))
        got = lax.dynamic_update_slice(got, flat_b[None], (chip, 0, 0))
        w = {**w, **_kernel_layout_b(_full_from_gathered(got, B_SHARDS))}
    q_a, k_a, v_a, o_a, u_a, states = _retention_fwd(h_a, cos_a, sin_a, tables)
    y_a = _mm(u_a, w["a_out"], tn=1024, name="a_out_fwd")
    x1, x1_16 = _ln_fwd(x, y_a, g0, b0, ts=ts)

    h1 = _mm(x1_16, w["b_in1"], tn=1152, name="b_in_fwd")
    lat16, kr16, qn16 = _kvq_prep(h1, kv_norm, q_norm, cos_b, sa_b, sb_b, ts=ts)
    k16 = _mm(lat16, w["up_k"], out_dtype=BF16, tn=2048, out_tn=H_B * QK_PAD, extras=(kr16,), epilogue=_assemble_k_store, name="up_k_fwd")
    v16 = _mm(lat16, w["up_v"], out_dtype=BF16, tn=2048, name="up_v_fwd")
    q16 = _mm(qn16, w["uq"], out_dtype=BF16, tn=2048, extras=(cos_b, sa_b, sb_b), epilogue=_rope_q_store, name="uq_fwd")
    o_b, u_b, lse = _attention_fwd(q16, k16, v16, h1, blk=2 * blk, bk=blk, sub=blk // 2, per_trip=4)
    y_b = _mm(u_b, w["b_out"], tn=1024, name="b_out_fwd")

    dz_b, dz_b16, dg1, db1, loss = _ln_loss_bwd(x1, y_b, target, g1, b1, ts=ts)
    d_b_out = _mm(u_b, dz_b16, ta=True, tn=1024, tk=2048, name="b_out_dw")
    du_b = _mm(dz_b16, w["b_out"], tb=True, tn=1024, name="b_out_dx")
    dqf16, dkn, dkr_heads, dv, dgate16 = _attention_bwd(q16, k16, v16, du_b, o_b, h1, lse, cos_b, sa_b, sb_b, blk=2 * blk, bk=blk, per_trip=4)
    d_uq = _mm(qn16, dqf16, ta=True, tm=768, tn=2048, tk=2048, name="uq_dw")
    dqn = _mm(dqf16, w["uq"], tb=True, tn=768, name="uq_dx")
    d_up_k = _mm(lat16, dkn, ta=True, tn=2048, tk=1024, name="up_k_dw")
    d_up_v = _mm(lat16, dv, ta=True, tn=2048, tk=1024, name="up_v_dw")
    dlat_k = _mm(dkn, w["up_k"], tb=True, tn=512, name="up_k_dx")
    dlat_v = _mm(dv, w["up_v"], tb=True, tn=512, name="up_v_dx")
    dh1, dkvn, dqnorm = _h1_bwd(h1, dlat_k, dlat_v, dkr_heads, dqn, dgate16, kv_norm, q_norm, cos_b, sa_b, sb_b, ts=ts)
    d_b_in1 = _mm(x1_16, dh1, ta=True, tn=1152, tk=2048, name="b_in_dw")
    dx1 = _mm(dh1, w["b_in1"], tb=True, tn=1024, extras=(dz_b,), epilogue=_residual_store, name="b_in_dx")

    dz_a, dz_a16, dg0, db0 = _ln_bwd_call(dx1, x, y_a, g0, ts=ts)
    d_a_out = _mm(u_a, dz_a16, ta=True, tn=1024, tk=2048, name="a_out_dw")
    du_a = _mm(dz_a16, w["a_out"], tb=True, tn=1024, name="a_out_dx")
    dh_a = _retention_bwd(q_a, k_a, v_a, states, du_a, o_a, h_a, cos_a, sin_a, tables)
    grads = dict(a_out=d_a_out, b_in1=d_b_in1, uq=d_uq, b_out=d_b_out, up_k=d_up_k, up_v=d_up_v)
    small = dict(ln_g=jnp.concatenate([dg0, dg1], axis=0), ln_b=jnp.concatenate([db0, db1], axis=0),
                 q_norm=dqnorm, kv_norm=dkvn)
    if reduce is None:
        grads["a_in"] = _mm(x16, dh_a, ta=True, tn=1536, tk=1024, name="a_in_dw")
        grad_x = _mm(dh_a, w["a_in"], tb=True, tn=1024, tk=2048, extras=(dz_a,), epilogue=_residual_store, name="a_in_dx")
        return loss, grad_x, grads, small
    own_early, travel_early = reduce(_reference_layout_grads(grads), EARLY_SHARDS, EARLY_ROWS, "early")
    d_a_in, got_early = _mm(x16, dh_a, ta=True, tn=1536, tk=2048, name="a_in_dw", side=_chip_exchange_side(travel_early))
    own_late, travel_late = reduce(dict(a_w_in=d_a_in), LATE_SHARDS, LATE_ROWS, "late")
    grad_x, got_late = _mm(dh_a, w["a_in"], tb=True, tn=1024, tk=3072, extras=(dz_a,), epilogue=_residual_store, name="a_in_dx",
                           side=_chip_exchange_side(travel_late))
    return loss, grad_x, ((own_early, got_early), (own_late, got_late)), small


def _flat_shards(shards, dtype, which, total_rows):
    parts = [shards[name].reshape(rows, FLAT_COLS) for name, rows in which]
    used = sum(rows for _, rows in which)
    parts.append(jnp.zeros((total_rows - used, FLAT_COLS), parts[0].dtype))
    return jnp.concatenate(parts, axis=0).astype(dtype)


def _unflat_shards(flat, shapes, shards):
    out, off = {}, 0
    for name, rows in shards:
        out[name] = flat[off:off + rows].reshape(shapes[name])
        off += rows
    return out


COL_SHARDED = {"a_w_in": (D_MODEL, IN_A), "b_w_in": (D_MODEL, IN_B), "b_w_uq": (Q_LORA, H_B * (QK_NOPE + QK_ROPE)),
               "kv_w_up": (KV_LORA, H_B * (QK_NOPE + V_HEAD))}
ROW_SHARDED = {"a_w_out": (WIDTH_A, D_MODEL), "b_w_out": (WIDTH_B, D_MODEL), "kv_w_down": (D_MODEL, KV_LORA + QK_ROPE)}


def _full_from_gathered(gathered, shards=SHARD_ROWS):
    out, off = {}, 0
    for name, rows in shards:
        part = gathered[:, off:off + rows]
        off += rows
        if name in COL_SHARDED:
            r, c = COL_SHARDED[name]
            out[name] = part.reshape(N_CHIPS, r, c // N_CHIPS).transpose(1, 0, 2).reshape(r, c)
        else:
            r, c = ROW_SHARDED[name]
            out[name] = part.reshape(r, c)
    return out


def _chip_major(g):
    r, c = g.shape
    return g.reshape(r, N_CHIPS, c // N_CHIPS).transpose(1, 0, 2)


def _gathered_from_full(full, shards, total_rows):
    if len(shards) == 1 and shards[0][0] in COL_SHARDED:
        return _chip_major(full[shards[0][0]])
    parts = []
    for name, rows in shards:
        g = full[name]
        if name in COL_SHARDED:
            r, c = COL_SHARDED[name]
            g = g.reshape(r, N_CHIPS, c // N_CHIPS).transpose(1, 0, 2)
        parts.append(g.reshape(N_CHIPS, rows, FLAT_COLS))
    used = sum(rows for _, rows in shards)
    if total_rows > used:
        parts.append(jnp.zeros((N_CHIPS, total_rows - used, FLAT_COLS), F32))
    return jnp.concatenate(parts, axis=1)


A_SHARDS, B_SHARDS = SHARD_ROWS[:1], SHARD_ROWS[1:]
A_ROWS = sum(rows for _, rows in A_SHARDS)


def _kernel_layout_a(full):
    return dict(a_in=full["a_w_in"])


def _kernel_layout_b(full):
    uq = full["b_w_uq"].reshape(Q_LORA, H_B, QK_NOPE + QK_ROPE)
    uq = jnp.pad(uq, ((0, 0), (0, 0), (0, QK_PAD - QK_NOPE - QK_ROPE))).reshape(Q_LORA, H_B * QK_PAD)
    up = full["kv_w_up"].reshape(KV_LORA, H_B, QK_NOPE + V_HEAD)
    down = jnp.pad(full["kv_w_down"], ((0, 0), (0, KV_DOWN_PAD - KV_LORA - QK_ROPE)))
    return dict(a_out=full["a_w_out"], b_out=full["b_w_out"], uq=uq,
                up_k=up[:, :, :QK_NOPE].reshape(KV_LORA, H_B * QK_NOPE),
                up_v=up[:, :, QK_NOPE:].reshape(KV_LORA, H_B * V_HEAD),
                b_in1=jnp.concatenate([down, full["b_w_in"]], axis=1))


def _kernel_layout(full):
    return {**_kernel_layout_a(full), **_kernel_layout_b(full)}


EARLY_SHARDS = tuple(sh for sh in SHARD_ROWS if sh[0] != "a_w_in")
LATE_SHARDS = tuple(sh for sh in SHARD_ROWS if sh[0] == "a_w_in")
EARLY_ROWS, LATE_ROWS = 3072, 1536


def _reference_layout_grads(g):
    uq = g["uq"].reshape(Q_LORA, H_B, QK_PAD)[:, :, :QK_NOPE + QK_ROPE].reshape(Q_LORA, H_B * (QK_NOPE + QK_ROPE))
    up = jnp.concatenate([g["up_k"].reshape(KV_LORA, H_B, QK_NOPE), g["up_v"].reshape(KV_LORA, H_B, V_HEAD)], axis=2)
    return dict(a_w_out=g["a_out"], b_w_out=g["b_out"], b_w_uq=uq,
                kv_w_up=up.reshape(KV_LORA, H_B * (QK_NOPE + V_HEAD)),
                kv_w_down=g["b_in1"][:, :KV_LORA + QK_ROPE], b_w_in=g["b_in1"][:, KV_DOWN_PAD:])


HBM_SPEC = pl.BlockSpec(memory_space=pl.ANY)


def _me():
    return lax.axis_index("x"), lax.axis_index("y"), lax.axis_index("c")


def _chip_flips(x, y):
    return [(1 - x, y), (x, 1 - y), (1 - x, 1 - y)]


def _gather_copies(src_ref, out_ref, send_sems, recv_sems):
    x, y, c = _me()
    half = src_ref.shape[0] // 2
    my_rows = pl.ds(pl.multiple_of(c * half, 16), half)
    their_rows = pl.ds(pl.multiple_of((1 - c) * half, 16), half)
    chips = _chip_flips(x, y)
    sibling = (x, y, 1 - c)

    def copy(k, src, dst, to):
        return pltpu.make_async_remote_copy(src_ref=src, dst_ref=dst, send_sem=send_sems.at[k], recv_sem=recv_sems.at[k],
                                            device_id=to, device_id_type=MESH)

    sends = [copy(k, src_ref.at[my_rows, :], out_ref.at[2 * x + y, my_rows, :], (px, py, c)) for k, (px, py) in enumerate(chips)]
    landed = [out_ref.at[2 * px + py, my_rows, :] for px, py in chips]
    lands = [copy(k, landed[k], landed[k], (px, py, c)) for k, (px, py) in enumerate(chips)]
    forwards = [copy(3 + k, landed[k], landed[k], sibling) for k in range(3)]
    theirs = [out_ref.at[2 * px + py, their_rows, :] for px, py in chips]
    arrivals = [copy(3 + k, theirs[k], theirs[k], sibling) for k in range(3)]
    return sends, lands, forwards, arrivals


def _gather_start(src_ref, out_ref, send_sems, recv_sems):
    sends, _, _, _ = _gather_copies(src_ref, out_ref, send_sems, recv_sems)
    for cp in sends:
        cp.start()


def _gather_finish(src_ref, out_ref, send_sems, recv_sems):
    sends, lands, forwards, arrivals = _gather_copies(src_ref, out_ref, send_sems, recv_sems)
    for k in range(3):
        lands[k].wait_recv()
        forwards[k].start()
    for cp in arrivals:
        cp.wait_recv()
    for cp in sends + forwards:
        cp.wait_send()


def _gather_scratch():
    return [pltpu.SemaphoreType.DMA((6,)), pltpu.SemaphoreType.DMA((6,))]


def _gather_weights(flat16):
    def body(src_ref, out_ref, send_sems, recv_sems):
        _gather_start(src_ref, out_ref, send_sems, recv_sems)
        _gather_finish(src_ref, out_ref, send_sems, recv_sems)

    return pl.pallas_call(
        body, name="gather_weights",
        in_specs=[HBM_SPEC], out_specs=HBM_SPEC,
        out_shape=jax.ShapeDtypeStruct((N_CHIPS,) + flat16.shape, flat16.dtype),
        scratch_shapes=_gather_scratch(),
    )(flat16)


def _gather_side(flat16):
    return dict(inputs=[flat16], out_shape=jax.ShapeDtypeStruct((N_CHIPS,) + flat16.shape, flat16.dtype),
                scratch=_gather_scratch(), start=_gather_start, finish=_gather_finish)


def _chip_exchange_copies(p_ref, out_ref, send_sems, recv_sems):
    x, y, c = _me()
    return [pltpu.make_async_remote_copy(
        src_ref=p_ref.at[2 * px + py], dst_ref=out_ref.at[k], send_sem=send_sems.at[k], recv_sem=recv_sems.at[k],
        device_id=(px, py, c), device_id_type=MESH) for k, (px, py) in enumerate(_chip_flips(x, y))]


def _chip_exchange_start(p_ref, out_ref, send_sems, recv_sems):
    for cp in _chip_exchange_copies(p_ref, out_ref, send_sems, recv_sems):
        cp.start()


def _chip_exchange_finish(p_ref, out_ref, send_sems, recv_sems):
    copies = _chip_exchange_copies(p_ref, out_ref, send_sems, recv_sems)
    for cp in copies:
        cp.wait_send()
    for cp in copies:
        cp.wait_recv()


def _chip_exchange_side(p):
    return dict(inputs=[p], out_shape=jax.ShapeDtypeStruct((3,) + p.shape[1:], p.dtype),
                scratch=[pltpu.SemaphoreType.DMA((3,)), pltpu.SemaphoreType.DMA((3,))],
                start=_chip_exchange_start, finish=_chip_exchange_finish)


def _pair_swap(r, name):
    def body(r_ref, out_ref, send_sem, recv_sem):
        x, y, c = _me()
        cp = pltpu.make_async_remote_copy(src_ref=r_ref, dst_ref=out_ref, send_sem=send_sem, recv_sem=recv_sem,
                                          device_id=(x, y, 1 - c), device_id_type=MESH)
        cp.start()
        cp.wait_send()
        cp.wait_recv()

    return pl.pallas_call(
        body, name=name,
        in_specs=[HBM_SPEC], out_specs=HBM_SPEC,
        out_shape=jax.ShapeDtypeStruct(r.shape, r.dtype),
        scratch_shapes=[pltpu.SemaphoreType.DMA, pltpu.SemaphoreType.DMA],
    )(r)


def _sum_small(vec):
    def body(v_ref, out_ref, slots, send_sems, recv_sems):
        x, y, c = _me()
        me = 4 * x + 2 * y + c
        slots[me] = v_ref[...]
        flips = [(fx, fy, fc) for fx in (0, 1) for fy in (0, 1) for fc in (0, 1)][1:]
        copies = []
        for k, (fx, fy, fc) in enumerate(flips):
            copies.append(pltpu.make_async_remote_copy(
                src_ref=v_ref, dst_ref=slots.at[me], send_sem=send_sems.at[k], recv_sem=recv_sems.at[k],
                device_id=(x ^ fx, y ^ fy, c ^ fc), device_id_type=MESH))
        for cp in copies:
            cp.start()
        for cp in copies:
            cp.wait_send()
        for k, (fx, fy, fc) in enumerate(flips):
            src = 4 * (x ^ fx) + 2 * (y ^ fy) + (c ^ fc)
            pltpu.make_async_remote_copy(
                src_ref=v_ref, dst_ref=slots.at[src], send_sem=send_sems.at[k], recv_sem=recv_sems.at[k],
                device_id=(x ^ fx, y ^ fy, c ^ fc), device_id_type=MESH).wait_recv()
        total = slots[0]
        for d in range(1, N_DEV):
            total = total + slots[d]
        out_ref[...] = total

    return pl.pallas_call(
        body, name="sum_small",
        in_specs=[pl.BlockSpec(memory_space=pltpu.VMEM)], out_specs=pl.BlockSpec(memory_space=pltpu.VMEM),
        out_shape=jax.ShapeDtypeStruct(vec.shape, vec.dtype),
        scratch_shapes=[pltpu.VMEM((N_DEV,) + vec.shape, vec.dtype), pltpu.SemaphoreType.DMA((7,)),
                        pltpu.SemaphoreType.DMA((7,))],
    )(vec)


UPD_ROWS = 256


def _pair_sum(g, theirs, core, chip, name):
    half, cols = theirs.shape[1:]
    nb = half // UPD_ROWS

    def body(core_ref, chip_ref, g_ref, t_ref, own_ref, o16_ref):
        total = g_ref[0] + t_ref[0].astype(F32)
        o16_ref[0] = total.astype(o16_ref.dtype)

        @pl.when(pl.program_id(1) == chip_ref[0])
        def _():
            own_ref[...] = total

    return pl.pallas_call(
        body, name=name,
        grid_spec=pltpu.PrefetchScalarGridSpec(
            num_scalar_prefetch=2, grid=(nb, N_CHIPS),
            in_specs=[pl.BlockSpec((1, UPD_ROWS, cols), lambda i, d, core_ref, chip_ref: (d, core_ref[0] * nb + i, 0)),
                      pl.BlockSpec((1, UPD_ROWS, cols), lambda i, d, core_ref, chip_ref: (d, i, 0))],
            out_specs=[pl.BlockSpec((UPD_ROWS, cols), lambda i, d, core_ref, chip_ref: (i, 0)),
                       pl.BlockSpec((1, UPD_ROWS, cols), lambda i, d, core_ref, chip_ref: (d, i, 0))]),
        out_shape=[jax.ShapeDtypeStruct((half, cols), F32),
                   jax.ShapeDtypeStruct((N_CHIPS, half, cols), BF16)],
        compiler_params=_params(("parallel", "arbitrary")),
    )(core, chip, g, theirs)


def _chip_sum(own, received, name):
    half, cols = own.shape
    nb = half // UPD_ROWS

    def body(p_ref, r_ref, o_ref):
        o_ref[...] = ((p_ref[...] + r_ref[0].astype(F32)) + r_ref[1].astype(F32)) + r_ref[2].astype(F32)

    return pl.pallas_call(
        body, name=name, grid=(nb,),
        in_specs=[pl.BlockSpec((UPD_ROWS, cols), lambda i: (i, 0)),
                  pl.BlockSpec((3, UPD_ROWS, cols), lambda i: (0, i, 0))],
        out_specs=pl.BlockSpec((UPD_ROWS, cols), lambda i: (i, 0)),
        out_shape=jax.ShapeDtypeStruct((half, cols), F32),
        compiler_params=_params(("parallel",)),
    )(own, received)


def _adamw(w, g, m, v, *, rows, name):
    r, c = w.shape
    rows = min(rows, r)
    assert r % rows == 0

    def body(w_ref, g_ref, m_ref, v_ref, d_ref, nm_ref, nv_ref):
        gv = g_ref[...]
        nm = ADAM_B1 * m_ref[...] + (1.0 - ADAM_B1) * gv
        nv = ADAM_B2 * v_ref[...] + (1.0 - ADAM_B2) * (gv * gv)
        m_hat = nm / (1.0 - ADAM_B1 ** ADAM_STEP)
        v_hat = nv / (1.0 - ADAM_B2 ** ADAM_STEP)
        d_ref[...] = -ADAM_LR * (m_hat / (jnp.sqrt(v_hat) + ADAM_EPS) + ADAM_WD * w_ref[...])
        nm_ref[...] = nm
        nv_ref[...] = nv

    spec = pl.BlockSpec((rows, c), lambda i: (i, 0))
    return pl.pallas_call(
        body, name=name, grid=(r // rows,),
        in_specs=[spec] * 4, out_specs=[spec] * 3,
        out_shape=[jax.ShapeDtypeStruct((r, c), F32)] * 3,
        compiler_params=_params(("parallel",)),
    )(w, g, m, v)


W_NAMES = ("a_w_in", "a_w_out", "b_w_in", "b_q_norm", "b_w_uq", "b_w_out", "kv_w_down", "kv_norm", "kv_w_up", "ln_g", "ln_b")
BIG = tuple(name for name, _ in SHARD_ROWS)


def _pack_small(ln_g, ln_b, q_norm, kv_norm, extra=None):
    pad = lambda a: jnp.pad(a.reshape(1, -1), ((0, 0), (0, FLAT_COLS - a.size)))
    rows = [ln_g, ln_b, pad(q_norm), pad(kv_norm),
            jnp.zeros((1, FLAT_COLS), F32) if extra is None else pad(extra), jnp.zeros((1, FLAT_COLS), F32)]
    return jnp.concatenate(rows, axis=0)


def _unpack_small(p):
    return dict(ln_g=p[0:2], ln_b=p[2:4], b_q_norm=p[4:5, :Q_LORA], kv_norm=p[5, :KV_LORA])


def kernel(x, a_w_in, a_w_out, b_w_in, b_q_norm, b_w_uq, b_w_out, kv_w_down, kv_norm, kv_w_up, ln_g, ln_b, loss_target, m_a_w_in, m_a_w_out, m_b_w_in, m_b_q_norm, m_b_w_uq, m_b_w_out, m_kv_w_down, m_kv_norm, m_kv_w_up, m_ln_g, m_ln_b, v_a_w_in, v_a_w_out, v_b_w_in, v_b_q_norm, v_b_w_uq, v_b_w_out, v_kv_w_down, v_kv_norm, v_kv_w_up, v_ln_g, v_ln_b):
    w_in = dict(a_w_in=a_w_in[0], a_w_out=a_w_out[0], b_w_in=b_w_in[0], b_w_uq=b_w_uq[0], b_w_out=b_w_out[0],
                kv_w_down=kv_w_down, kv_w_up=kv_w_up)
    m_in = dict(a_w_in=m_a_w_in[0], a_w_out=m_a_w_out[0], b_w_in=m_b_w_in[0], b_w_uq=m_b_w_uq[0], b_w_out=m_b_w_out[0],
                kv_w_down=m_kv_w_down, kv_w_up=m_kv_w_up)
    v_in = dict(a_w_in=v_a_w_in[0], a_w_out=v_a_w_out[0], b_w_in=v_b_w_in[0], b_w_uq=v_b_w_uq[0], b_w_out=v_b_w_out[0],
                kv_w_down=v_kv_w_down, kv_w_up=v_kv_w_up)
    shard_shapes = {name: w_in[name].shape for name in BIG}

    cx, cy, cc = lax.axis_index("x"), lax.axis_index("y"), lax.axis_index("c")
    chip = 2 * cx + cy
    flat_b = _flat_shards(w_in, BF16, B_SHARDS, FLAT_ROWS - A_ROWS)
    a16 = w_in["a_w_in"].astype(BF16)
    got_a = lax.dynamic_update_slice(_gather_weights(a16), a16[None], (chip, 0, 0))
    weights_a = dict(a_in=got_a.transpose(1, 0, 2).reshape(D_MODEL, IN_A))

    core_arr, chip_arr = cc.astype(jnp.int32).reshape(1), chip.astype(jnp.int32).reshape(1)

    def reduce_pair(full, shards, rows, tag):
        g_all = _gathered_from_full(full, shards, rows)
        half, cols = g_all.shape[1] // 2, g_all.shape[2]
        other_half = lax.dynamic_slice(g_all, (0, (1 - cc) * half, 0), (N_CHIPS, half, cols)).astype(BF16)
        theirs = _pair_swap(other_half, "pair_exchange_" + tag)
        return _pair_sum(g_all, theirs, core_arr, chip_arr, "pair_sum_" + tag)

    loss, grad_x, reduced, small = _local_step(x[0], loss_target[0], weights_a, kv_norm.reshape(1, -1), b_q_norm, ln_g, ln_b,
                                               late=(flat_b, chip), reduce=reduce_pair)

    g_big = {}
    for (own, received), shards, tag in zip(reduced, (EARLY_SHARDS, LATE_SHARDS), ("early", "late")):
        mine = _chip_sum(own, received, "chip_sum_" + tag)
        sibling = _pair_swap(mine, "pair_share_" + tag)
        g_flat = jnp.concatenate([jnp.where(cc == 0, mine, sibling), jnp.where(cc == 0, sibling, mine)], axis=0)
        if g_flat.shape == shard_shapes[shards[0][0]]:
            g_big[shards[0][0]] = g_flat
        else:
            g_big.update(_unflat_shards(g_flat, shard_shapes, shards))

    small_sum = _sum_small(_pack_small(small["ln_g"], small["ln_b"], small["q_norm"], small["kv_norm"], loss[:, :1]))
    loss_out = small_sum[6, 0]

    row = lambda a: a.reshape(1, -1)
    g_all = {**g_big, **_unpack_small(small_sum)}
    g_all["kv_norm"] = row(g_all["kv_norm"])
    state = {**{name: (w_in[name], m_in[name], v_in[name]) for name in BIG},
             "ln_g": (ln_g, m_ln_g, v_ln_g), "ln_b": (ln_b, m_ln_b, v_ln_b), "b_q_norm": (b_q_norm, m_b_q_norm, v_b_q_norm),
             "kv_norm": (row(kv_norm), row(m_kv_norm), row(v_kv_norm))}
    upd = {name: _adamw(state[name][0], g_all[name], state[name][1], state[name][2], rows=256, name="adamw_" + name)
           for name in W_NAMES}

    def shaped(name, a):
        if name == "kv_norm":
            return a.reshape(-1)
        return a[None] if name in ("a_w_in", "a_w_out", "b_w_in", "b_w_uq", "b_w_out") else a

    outputs = [[shaped(name, g_all[name]) for name in W_NAMES]]
    outputs += [[shaped(name, upd[name][k]) for name in W_NAMES] for k in range(3)]
    return (loss_out, grad_x[None], *outputs[0], *outputs[1], *outputs[2], *outputs[3])
```

```python
import functools
import math

import jax
import jax.numpy as jnp
from jax import lax
from jax.experimental import pallas as pl
from jax.experimental.pallas import tpu as pltpu

F32 = jnp.float32
BF16 = jnp.bfloat16
MESH = pl.DeviceIdType.MESH

D_MODEL = 1024
DEPTH = 2
H_A, DK_A, DV_A = 4, 256, 512
WIDTH_A = H_A * DV_A
CHUNK = 128
H_B, QK_NOPE, QK_ROPE, V_HEAD = 16, 128, 64, 128
QK_PAD = 256
Q_LORA, KV_LORA = 768, 512
KV_DOWN_PAD = 640
WIDTH_B = H_B * V_HEAD
IN_A = 2 * H_A * DK_A + 2 * WIDTH_A
IN_B = Q_LORA + WIDTH_B
H1_B = KV_DOWN_PAD + IN_B
ROPE_BASE = 10000.0
ALPHA = (2.0 * DEPTH) ** 0.25
ATT_SCALE = (QK_NOPE + QK_ROPE) ** -0.5
NEG_BIG = -1e30

ADAM_LR, ADAM_B1, ADAM_B2, ADAM_EPS, ADAM_WD, ADAM_STEP = 0.001, 0.9, 0.999, 1e-08, 0.01, 10

VMEM_LIMIT_BYTES = 56 * 1024 * 1024
LANES = 128
FLAT_COLS = 1024
SHARD_ROWS = (("a_w_in", 1536), ("a_w_out", 512), ("b_w_in", 704), ("b_w_uq", 576), ("b_w_out", 512),
              ("kv_w_down", 144), ("kv_w_up", 512))
FLAT_ROWS = 4608
N_CHIPS = 4
N_DEV = 8


def _params(sem, vmem=VMEM_LIMIT_BYTES):
    return pltpu.CompilerParams(dimension_semantics=sem, vmem_limit_bytes=vmem)


def _row_spec(ts, w, col_block=0):
    return pl.BlockSpec((ts, w), lambda i: (i, col_block))


def _bc_spec(shape):
    nd = len(shape)
    return pl.BlockSpec(shape, lambda i: (0,) * nd)


def _sigmoid(x):
    return 1.0 / (1.0 + jnp.exp(-x))


def _fold8(v):
    ts, w = v.shape
    return jnp.sum(v.reshape(ts // 8, 8, w), axis=0)


def _mm(a, b, *, ta=False, tb=False, out_dtype=F32, tm=1024, tn=512, tk=None, name, extras=(), epilogue=None, out_tn=None,
        side=None):
    if ta:
        K, M = a.shape
    else:
        M, K = a.shape
    if tb:
        N, Kb = b.shape
    else:
        Kb, N = b.shape
    assert K == Kb, (a.shape, b.shape)
    tm, tn = min(tm, M), min(tn, N)
    tk = K if tk is None else min(tk, K)
    assert M % tm == 0 and N % tn == 0 and K % tk == 0, (name, M, N, K, tm, tn, tk)
    grid = (M // tm, N // tn, K // tk)
    nk = grid[2]
    out_tn = tn if out_tn is None else out_tn
    n_extra = len(extras)
    side_inputs = [] if side is None else list(side["inputs"])
    n_side = len(side_inputs)
    n_acc = 0 if nk == 1 else 1
    dims = (((0,) if ta else (1,), (1,) if tb else (0,)), ((), ()))

    def body(a_ref, b_ref, *rest):
        extra_refs, rest = rest[:n_extra], rest[n_extra:]
        side_in, o_ref, rest = rest[:n_side], rest[n_side], rest[n_side + 1:]
        if side is not None:
            side_refs, rest = side_in + rest[:1] + rest[1 + n_acc:], rest[1:]
            ids = [pl.program_id(d) for d in range(3)]

            @pl.when((ids[0] == 0) & (ids[1] == 0) & (ids[2] == 0))
            def _():
                side["start"](*side_refs)

        prod = lax.dot_general(a_ref[...].astype(BF16), b_ref[...].astype(BF16), dims,
                               preferred_element_type=F32)

        def store(tile):
            if epilogue is None:
                o_ref[...] = tile.astype(o_ref.dtype)
            else:
                epilogue(tile, o_ref, *extra_refs)

        if nk == 1:
            store(prod)
        else:
            acc = rest[0]
            k = pl.program_id(2)

            @pl.when(k == 0)
            def _():
                acc[...] = prod

            @pl.when(k > 0)
            def _():
                acc[...] += prod

            @pl.when(k == nk - 1)
            def _():
                store(acc[...])

        if side is not None:
            @pl.when((ids[0] == grid[0] - 1) & (ids[1] == grid[1] - 1) & (ids[2] == grid[2] - 1))
            def _():
                side["finish"](*side_refs)

    a_spec = pl.BlockSpec((tk, tm), lambda i, j, k: (k, i)) if ta else pl.BlockSpec((tm, tk), lambda i, j, k: (i, k))
    b_spec = pl.BlockSpec((tn, tk), lambda i, j, k: (j, k)) if tb else pl.BlockSpec((tk, tn), lambda i, j, k: (k, j))
    extra_specs = [pl.BlockSpec((tm, e.shape[1]), lambda i, j, k: (i, 0)) for e in extras]
    out_specs = pl.BlockSpec((tm, out_tn), lambda i, j, k: (i, j))
    out_shape = jax.ShapeDtypeStruct((M, (N // tn) * out_tn), out_dtype)
    scratch = [] if nk == 1 else [pltpu.VMEM((tm, tn), F32)]
    if side is not None:
        out_specs, out_shape, scratch = [out_specs, HBM_SPEC], [out_shape, side["out_shape"]], scratch + list(side["scratch"])
    return pl.pallas_call(
        body, name=name, grid=grid,
        in_specs=[a_spec, b_spec] + extra_specs + [HBM_SPEC] * n_side,
        out_specs=out_specs, out_shape=out_shape, scratch_shapes=scratch,
        compiler_params=_params(("parallel", "parallel", "arbitrary") if side is None else ("arbitrary",) * 3),
    )(a, b, *extras, *side_inputs)


def _rope_tables_a(s):
    half = DK_A // 2
    inv = ROPE_BASE ** (-jnp.arange(half, dtype=F32) / half)
    ang = jnp.arange(s, dtype=F32)[:, None] * inv[None, :]
    return jnp.cos(ang), jnp.sin(ang)


def _rope_tables_b(s):
    half = QK_ROPE // 2
    inv = ROPE_BASE ** (-jnp.arange(half, dtype=F32) / half)
    ang = jnp.arange(s, dtype=F32)[:, None] * inv[None, :]
    c, sn = jnp.cos(ang), jnp.sin(ang)
    z = jnp.zeros_like(c)
    cos = jnp.concatenate([c, c, z, z], axis=1)
    sa = jnp.concatenate([-sn, z, z, z], axis=1)
    sb = jnp.concatenate([z, sn, z, z], axis=1)
    return cos, sa, sb


def _rope_b(r, cos, sa, sb, sign):
    return r * cos + sign * (pltpu.roll(r, 96, 1) * sa + pltpu.roll(r, 32, 1) * sb)


def _retention_tables():
    lg = jnp.log1p(-jnp.exp2(-5.0 - jnp.arange(H_A, dtype=F32)))
    idx = jnp.arange(CHUNK, dtype=F32)
    diff = idx[:, None] - idx[None, :]
    causal = diff >= 0
    dmat = jnp.where(causal, jnp.exp(jnp.where(causal, diff, 0.0)[None] * lg[:, None, None]), 0.0)
    qdec = jnp.exp((idx + 1.0)[None, :] * lg[:, None])[:, :, None]
    kdec = jnp.exp((CHUNK - 1.0 - idx)[None, :] * lg[:, None])[:, :, None]
    cdec = jnp.broadcast_to(jnp.exp(CHUNK * lg)[:, None, None], (H_A, 1, DV_A))
    return dmat, qdec, kdec, cdec


def _group_norm(o):
    mu = jnp.mean(o, axis=-1, keepdims=True)
    oc = o - mu
    var = jnp.mean(oc * oc, axis=-1, keepdims=True)
    rstd = lax.rsqrt(var + 1e-5)
    return oc * rstd, rstd


Q_COL, K_COL, V_COL, GATE_COL = 0, H_A * DK_A, 2 * H_A * DK_A, 2 * H_A * DK_A + WIDTH_A


def _ln_stats(z):
    mu = jnp.mean(z, axis=-1, keepdims=True)
    zc = z - mu
    var = jnp.mean(zc * zc, axis=-1, keepdims=True)
    rstd = lax.rsqrt(var + 1e-5)
    return zc * rstd, rstd


def _ln_bwd(dy, xhat, rstd, g):
    dxh = dy * g
    m1 = jnp.mean(dxh, axis=-1, keepdims=True)
    m2 = jnp.mean(dxh * xhat, axis=-1, keepdims=True)
    return rstd * (dxh - m1 - xhat * m2)


def _ln_fwd(x, y, g, b, *, ts):
    s = x.shape[0]

    def body(x_ref, y_ref, g_ref, b_ref, o_ref, o16_ref):
        xhat, _ = _ln_stats(ALPHA * x_ref[...] + y_ref[...])
        out = xhat * g_ref[...] + b_ref[...]
        o_ref[...] = out
        o16_ref[...] = out.astype(o16_ref.dtype)

    return pl.pallas_call(
        body, name="ln_fwd", grid=(s // ts,),
        in_specs=[_row_spec(ts, D_MODEL), _row_spec(ts, D_MODEL), _bc_spec((1, D_MODEL)), _bc_spec((1, D_MODEL))],
        out_specs=[_row_spec(ts, D_MODEL), _row_spec(ts, D_MODEL)],
        out_shape=[jax.ShapeDtypeStruct((s, D_MODEL), F32), jax.ShapeDtypeStruct((s, D_MODEL), BF16)],
        compiler_params=_params(("parallel",)),
    )(x, y, g, b)


def _ln_loss_bwd(x1, y, target, g, b, *, ts):
    s = x1.shape[0]
    n = s // ts

    def body(x_ref, y_ref, t_ref, g_ref, b_ref, dz_ref, dz16_ref, dg_ref, db_ref, loss_ref, ag, ab, al):
        i = pl.program_id(0)

        @pl.when(i == 0)
        def _():
            ag[...] = jnp.zeros_like(ag)
            ab[...] = jnp.zeros_like(ab)
            al[...] = jnp.zeros_like(al)

        xhat, rstd = _ln_stats(ALPHA * x_ref[...] + y_ref[...])
        err = xhat * g_ref[...] + b_ref[...] - t_ref[...]
        al[...] += _fold8(err * err)
        dy = err * (1.0 / D_MODEL)
        ag[...] += _fold8(dy * xhat)
        ab[...] += _fold8(dy)
        dz = _ln_bwd(dy, xhat, rstd, g_ref[...])
        dz_ref[...] = dz
        dz16_ref[...] = dz.astype(dz16_ref.dtype)

        @pl.when(i == n - 1)
        def _():
            dg_ref[...] = jnp.sum(ag[...], axis=0, keepdims=True)
            db_ref[...] = jnp.sum(ab[...], axis=0, keepdims=True)
            loss_ref[...] = jnp.full((1, LANES), (0.5 / D_MODEL) * jnp.sum(al[...]), F32)

    return pl.pallas_call(
        body, name="ln_loss_bwd", grid=(n,),
        in_specs=[_row_spec(ts, D_MODEL)] * 3 + [_bc_spec((1, D_MODEL))] * 2,
        out_specs=[_row_spec(ts, D_MODEL), _row_spec(ts, D_MODEL), _bc_spec((1, D_MODEL)), _bc_spec((1, D_MODEL)),
                   _bc_spec((1, LANES))],
        out_shape=[jax.ShapeDtypeStruct((s, D_MODEL), F32), jax.ShapeDtypeStruct((s, D_MODEL), BF16),
                   jax.ShapeDtypeStruct((1, D_MODEL), F32), jax.ShapeDtypeStruct((1, D_MODEL), F32),
                   jax.ShapeDtypeStruct((1, LANES), F32)],
        scratch_shapes=[pltpu.VMEM((8, D_MODEL), F32)] * 3,
        compiler_params=_params(("arbitrary",)),
    )(x1, y, target, g, b)


def _ln_bwd_call(dy, x, y, g, *, ts):
    s = x.shape[0]
    n = s // ts

    def body(dy_ref, x_ref, y_ref, g_ref, dz_ref, dz16_ref, dg_ref, db_ref, ag, ab):
        i = pl.program_id(0)

        @pl.when(i == 0)
        def _():
            ag[...] = jnp.zeros_like(ag)
            ab[...] = jnp.zeros_like(ab)

        xhat, rstd = _ln_stats(ALPHA * x_ref[...] + y_ref[...])
        dy = dy_ref[...]
        ag[...] += _fold8(dy * xhat)
        ab[...] += _fold8(dy)
        dz = _ln_bwd(dy, xhat, rstd, g_ref[...])
        dz_ref[...] = dz
        dz16_ref[...] = dz.astype(dz16_ref.dtype)

        @pl.when(i == n - 1)
        def _():
            dg_ref[...] = jnp.sum(ag[...], axis=0, keepdims=True)
            db_ref[...] = jnp.sum(ab[...], axis=0, keepdims=True)

    return pl.pallas_call(
        body, name="ln_bwd", grid=(n,),
        in_specs=[_row_spec(ts, D_MODEL)] * 3 + [_bc_spec((1, D_MODEL))],
        out_specs=[_row_spec(ts, D_MODEL), _row_spec(ts, D_MODEL), _bc_spec((1, D_MODEL)), _bc_spec((1, D_MODEL))],
        out_shape=[jax.ShapeDtypeStruct((s, D_MODEL), F32), jax.ShapeDtypeStruct((s, D_MODEL), BF16),
                   jax.ShapeDtypeStruct((1, D_MODEL), F32), jax.ShapeDtypeStruct((1, D_MODEL), F32)],
        scratch_shapes=[pltpu.VMEM((8, D_MODEL), F32)] * 2,
        compiler_params=_params(("arbitrary",)),
    )(dy, x, y, g)


def _residual_store(tile, o_ref, dz_ref):
    o_ref[...] = ALPHA * dz_ref[...] + tile


C_LAT = slice(0, KV_LORA)
C_ROPE = slice(KV_LORA, KV_DOWN_PAD)
C_QL = slice(KV_DOWN_PAD, KV_DOWN_PAD + Q_LORA)
C_GATE = slice(KV_DOWN_PAD + Q_LORA, H1_B)


def _rms(x, eps=1e-6):
    r = lax.rsqrt(jnp.mean(x * x, axis=-1, keepdims=True) + eps)
    return x * r, r


def _rms_bwd(dy, xhat, r, g):
    dxh = dy * g
    return r * (dxh - xhat * jnp.mean(dxh * xhat, axis=-1, keepdims=True))


def _kvq_prep(h1, kv_norm, q_norm, cos, sa, sb, *, ts):
    s = h1.shape[0]

    def body(h_ref, kn_ref, qn_ref, c_ref, sa_ref, sb_ref, lat_ref, kr_ref, ql_ref):
        lat, _ = _rms(h_ref[:, C_LAT])
        lat_ref[...] = (lat * kn_ref[...]).astype(lat_ref.dtype)
        kr_ref[...] = _rope_b(h_ref[:, C_ROPE], c_ref[...], sa_ref[...], sb_ref[...], 1.0).astype(kr_ref.dtype)
        ql, _ = _rms(h_ref[:, C_QL])
        ql_ref[...] = (ql * qn_ref[...]).astype(ql_ref.dtype)

    return pl.pallas_call(
        body, name="kvq_prep", grid=(s // ts,),
        in_specs=[_row_spec(ts, H1_B), _bc_spec((1, KV_LORA)), _bc_spec((1, Q_LORA))] + [_row_spec(ts, 128)] * 3,
        out_specs=[_row_spec(ts, KV_LORA), _row_spec(ts, 128), _row_spec(ts, Q_LORA)],
        out_shape=[jax.ShapeDtypeStruct((s, KV_LORA), BF16), jax.ShapeDtypeStruct((s, 128), BF16),
                   jax.ShapeDtypeStruct((s, Q_LORA), BF16)],
        compiler_params=_params(("parallel",)),
    )(h1, kv_norm, q_norm, cos, sa, sb)


LOG2E = 1.4426950408889634
LN2 = 0.6931471805599453
Q_SCALE = ATT_SCALE * LOG2E


def _rope_q_store(tile, o_ref, c_ref, sa_ref, sb_ref):
    c, a, b = c_ref[...], sa_ref[...], sb_ref[...]
    for hd in range(tile.shape[1] // QK_PAD):
        lo = hd * QK_PAD
        o_ref[:, lo:lo + 128] = (tile[:, lo:lo + 128] * Q_SCALE).astype(o_ref.dtype)
        o_ref[:, lo + 128:lo + 256] = (_rope_b(tile[:, lo + 128:lo + 256], c, a, b, 1.0) * Q_SCALE).astype(o_ref.dtype)


def _assemble_k_store(tile, o_ref, kr_ref):
    r = kr_ref[...]
    for hd in range(tile.shape[1] // QK_NOPE):
        o_ref[:, hd * QK_PAD:hd * QK_PAD + 128] = tile[:, hd * 128:(hd + 1) * 128].astype(o_ref.dtype)
        o_ref[:, hd * QK_PAD + 128:(hd + 1) * QK_PAD] = r


def _h1_bwd(h1, dlat_k, dlat_v, dkr_heads, dqn, dg16, kv_norm, q_norm, cos, sa, sb, *, ts):
    s = h1.shape[0]
    n = s // ts

    def body(h_ref, dk_ref, dv_ref, dkr_ref, dqn_ref, dg_ref, kn_ref, qn_ref, c_ref, sa_ref, sb_ref,
             o_ref, dkn_ref, dqn_out_ref, akn, aqn):
        i = pl.program_id(0)

        @pl.when(i == 0)
        def _():
            akn[...] = jnp.zeros_like(akn)
            aqn[...] = jnp.zeros_like(aqn)

        lat, r = _rms(h_ref[:, C_LAT])
        dlat = dk_ref[...] + dv_ref[...]
        akn[...] += _fold8(dlat * lat)
        o_ref[:, C_LAT] = _rms_bwd(dlat, lat, r, kn_ref[...]).astype(o_ref.dtype)

        dkr = dkr_ref[:, 0:128]
        for hd in range(1, H_B):
            dkr = dkr + dkr_ref[:, hd * 128:(hd + 1) * 128]
        o_ref[:, C_ROPE] = _rope_b(dkr, c_ref[...], sa_ref[...], sb_ref[...], -1.0).astype(o_ref.dtype)

        ql, rq = _rms(h_ref[:, C_QL])
        dq = dqn_ref[...]
        aqn[...] += _fold8(dq * ql)
        o_ref[:, C_QL] = _rms_bwd(dq, ql, rq, qn_ref[...]).astype(o_ref.dtype)
        o_ref[:, C_GATE] = dg_ref[...]

        @pl.when(i == n - 1)
        def _():
            dkn_ref[...] = jnp.sum(akn[...], axis=0, keepdims=True)
            dqn_out_ref[...] = jnp.sum(aqn[...], axis=0, keepdims=True)

    return pl.pallas_call(
        body, name="h1_bwd", grid=(n,),
        in_specs=[_row_spec(ts, H1_B), _row_spec(ts, KV_LORA), _row_spec(ts, KV_LORA), _row_spec(ts, H_B * 128),
                  _row_spec(ts, Q_LORA), _row_spec(ts, WIDTH_B), _bc_spec((1, KV_LORA)), _bc_spec((1, Q_LORA))]
        + [_row_spec(ts, 128)] * 3,
        out_specs=[_row_spec(ts, H1_B), _bc_spec((1, KV_LORA)), _bc_spec((1, Q_LORA))],
        out_shape=[jax.ShapeDtypeStruct((s, H1_B), BF16), jax.ShapeDtypeStruct((1, KV_LORA), F32),
                   jax.ShapeDtypeStruct((1, Q_LORA), F32)],
        scratch_shapes=[pltpu.VMEM((8, KV_LORA), F32), pltpu.VMEM((8, Q_LORA), F32)],
        compiler_params=_params(("arbitrary",)),
    )(h1, dlat_k, dlat_v, dkr_heads, dqn, dg16, kv_norm, q_norm, cos, sa, sb)


def _dot(a, b, ca, cb):
    return lax.dot_general(a, b, (((ca,), (cb,)), ((), ())), preferred_element_type=F32)


CHUNKS_PER_STEP = 2


def _table_specs():
    full = lambda shape: pl.BlockSpec(shape, lambda i: (0,) * len(shape))
    return [full((H_A, CHUNK, CHUNK)), full((H_A, CHUNK, 1)), full((H_A, CHUNK, 1)), full((H_A, 1, DV_A))]


def _retention_fwd(h_a, cos, sin, tables):
    s = h_a.shape[0]
    n = s // CHUNK

    def body(h_ref, c_ref, s_ref, dm_ref, qd_ref, kd_ref, cd_ref, q_ref, k_ref, v_ref, o_ref, u_ref, st_ref, state):
        @pl.when(pl.program_id(0) == 0)
        def _():
            state[...] = jnp.zeros_like(state)

        for cb in range(CHUNKS_PER_STEP):
            rows = slice(cb * CHUNK, (cb + 1) * CHUNK)
            c, sn = c_ref[rows, :], s_ref[rows, :]
            for hd in range(H_A):
                qs, vs = slice(hd * DK_A, (hd + 1) * DK_A), slice(hd * DV_A, (hd + 1) * DV_A)
                for r_ref, base, scale in ((q_ref, Q_COL, 1.0), (k_ref, K_COL, DK_A ** -0.5)):
                    lo = base + hd * DK_A
                    x1, x2 = h_ref[rows, lo:lo + 128], h_ref[rows, lo + 128:lo + 256]
                    r_ref[rows, hd * DK_A:hd * DK_A + 128] = ((x1 * c - x2 * sn) * scale).astype(r_ref.dtype)
                    r_ref[rows, hd * DK_A + 128:(hd + 1) * DK_A] = ((x2 * c + x1 * sn) * scale).astype(r_ref.dtype)
                v_ref[rows, vs] = h_ref[rows, V_COL + hd * DV_A:V_COL + (hd + 1) * DV_A].astype(v_ref.dtype)
                qv, kv, vv = q_ref[rows, qs], k_ref[rows, qs], v_ref[rows, vs]
                st = state[hd]
                st16 = st.astype(BF16)
                st_ref[cb, hd] = st16
                scores = _dot(qv, kv, 1, 1) * dm_ref[hd]
                qd = (qv.astype(F32) * qd_ref[hd]).astype(BF16)
                o = _dot(scores.astype(BF16), vv, 1, 0) + _dot(qd, st16, 1, 0)
                o_ref[rows, vs] = o
                kd = (kv.astype(F32) * kd_ref[hd]).astype(BF16)
                state[hd] = st * cd_ref[hd] + _dot(kd, vv, 0, 0)
                on, _ = _group_norm(o)
                g = h_ref[rows, GATE_COL + hd * DV_A:GATE_COL + (hd + 1) * DV_A]
                u_ref[rows, vs] = (on * (g * _sigmoid(g))).astype(u_ref.dtype)

    row = lambda w: pl.BlockSpec((CHUNKS_PER_STEP * CHUNK, w), lambda i: (i, 0))
    return pl.pallas_call(
        body, name="retention_fwd", grid=(n // CHUNKS_PER_STEP,),
        in_specs=[row(IN_A), row(128), row(128)] + _table_specs(),
        out_specs=[row(H_A * DK_A), row(H_A * DK_A), row(WIDTH_A), row(WIDTH_A), row(WIDTH_A),
                   pl.BlockSpec((CHUNKS_PER_STEP, H_A, DK_A, DV_A), lambda i: (i, 0, 0, 0))],
        out_shape=[jax.ShapeDtypeStruct((s, H_A * DK_A), BF16), jax.ShapeDtypeStruct((s, H_A * DK_A), BF16),
                   jax.ShapeDtypeStruct((s, WIDTH_A), BF16), jax.ShapeDtypeStruct((s, WIDTH_A), F32),
                   jax.ShapeDtypeStruct((s, WIDTH_A), BF16), jax.ShapeDtypeStruct((n, H_A, DK_A, DV_A), BF16)],
        scratch_shapes=[pltpu.VMEM((H_A, DK_A, DV_A), F32)],
        compiler_params=_params(("arbitrary",)),
    )(h_a, cos, sin, *tables)


def _retention_bwd(q, k, v, states, du, o, h_a, cos, sin, tables):
    s = q.shape[0]
    n = s // CHUNK

    def body(q_ref, k_ref, v_ref, st_ref, du_ref, o_ref, g_ref, c_ref, s_ref, dm_ref, qd_ref, kd_ref, cd_ref, dh_ref, grad_state):
        @pl.when(pl.program_id(0) == 0)
        def _():
            grad_state[...] = jnp.zeros_like(grad_state)

        for cb in reversed(range(CHUNKS_PER_STEP)):
            rows = slice(cb * CHUNK, (cb + 1) * CHUNK)
            c, sn = c_ref[rows, :], s_ref[rows, :]
            for hd in range(H_A):
                qs, vs = slice(hd * DK_A, (hd + 1) * DK_A), slice(hd * DV_A, (hd + 1) * DV_A)
                on, rstd = _group_norm(o_ref[rows, vs])
                g = g_ref[rows, vs]
                sg = _sigmoid(g)
                du_v = du_ref[rows, vs]
                don = du_v * (g * sg)
                dh_ref[rows, GATE_COL + hd * DV_A:GATE_COL + (hd + 1) * DV_A] = (du_v * on * (sg * (1.0 + g * (1.0 - sg)))).astype(dh_ref.dtype)
                m1 = jnp.mean(don, axis=-1, keepdims=True)
                m2 = jnp.mean(don * on, axis=-1, keepdims=True)
                dov = (rstd * (don - m1 - on * m2)).astype(BF16)

                qv, kv, vv = q_ref[rows, qs], k_ref[rows, qs], v_ref[rows, vs]
                dm = dm_ref[hd]
                gs = grad_state[hd]
                g16 = gs.astype(BF16)
                scores = (_dot(qv, kv, 1, 1) * dm).astype(BF16)
                dscores = (_dot(dov, vv, 1, 1) * dm).astype(BF16)
                qd = (qv.astype(F32) * qd_ref[hd]).astype(BF16)
                kd = (kv.astype(F32) * kd_ref[hd]).astype(BF16)
                dq = _dot(dscores, kv, 1, 0) + _dot(dov, st_ref[cb, hd], 1, 1) * qd_ref[hd]
                dk = (_dot(dscores, qv, 0, 0) + _dot(vv, g16, 1, 1) * kd_ref[hd]) * (DK_A ** -0.5)
                dh_ref[rows, V_COL + hd * DV_A:V_COL + (hd + 1) * DV_A] = (_dot(scores, dov, 0, 0) + _dot(kd, g16, 1, 0)).astype(dh_ref.dtype)
                grad_state[hd] = gs * cd_ref[hd] + _dot(qd, dov, 0, 0)
                for d, base in ((dq, Q_COL), (dk, K_COL)):
                    lo = base + hd * DK_A
                    d1, d2 = d[:, 0:128], d[:, 128:256]
                    dh_ref[rows, lo:lo + 128] = (d1 * c + d2 * sn).astype(dh_ref.dtype)
                    dh_ref[rows, lo + 128:lo + 256] = (d2 * c - d1 * sn).astype(dh_ref.dtype)

    steps = n // CHUNKS_PER_STEP
    rev = lambda i: steps - 1 - i
    row = lambda w, col=0: pl.BlockSpec((CHUNKS_PER_STEP * CHUNK, w), lambda i: (rev(i), col))
    return pl.pallas_call(
        body, name="retention_bwd", grid=(steps,),
        in_specs=[row(H_A * DK_A), row(H_A * DK_A), row(WIDTH_A),
                  pl.BlockSpec((CHUNKS_PER_STEP, H_A, DK_A, DV_A), lambda i: (rev(i), 0, 0, 0)),
                  row(WIDTH_A), row(WIDTH_A), row(WIDTH_A, GATE_COL // WIDTH_A), row(128), row(128)] + _table_specs(),
        out_specs=row(IN_A),
        out_shape=jax.ShapeDtypeStruct((s, IN_A), BF16),
        scratch_shapes=[pltpu.VMEM((H_A, DK_A, DV_A), F32)],
        compiler_params=_params(("arbitrary",)),
    )(q, k, v, states, du, o, h_a, cos, sin, *tables)


GATE_BLOCK0 = (KV_DOWN_PAD + Q_LORA) // LANES


def _causal_mask(sc, row0):
    row = lax.broadcasted_iota(jnp.int32, sc.shape, 0) + row0
    col = lax.broadcasted_iota(jnp.int32, sc.shape, 1)
    return jnp.where(col <= row, sc, NEG_BIG)


def _key_block_loop(step, n, per_trip, smallest=1):
    def trip_body(jj, carry):
        for t in range(per_trip):
            step(per_trip * jj + t)
        return carry

    lax.fori_loop(0, n // per_trip, trip_body, 0)
    group = per_trip // 2
    while group >= smallest:
        def tail(group=group):
            first = (n // (2 * group)) * (2 * group)
            for t in range(group):
                step(first + t)

        pl.when((n // group) % 2 == 1)(tail)
        group //= 2


def _attention_fwd(q, k, v, h1, *, blk, bk, sub, per_trip):
    s = q.shape[0]
    nb = s // blk

    def body(q_ref, k_ref, v_ref, g_ref, o_ref, u_ref, lse_ref, vext_s, m_s, acc_s):
        i = pl.program_id(1)

        @pl.when(i == 0)
        def _():
            vext_s[:, 0:V_HEAD] = v_ref[...]
            vext_s[:, V_HEAD:2 * V_HEAD] = jnp.ones((s, V_HEAD), vext_s.dtype)

        m_s[...] = jnp.full_like(m_s, NEG_BIG)
        acc_s[...] = jnp.zeros_like(acc_s)

        def update(rows, kb, vb, row0):
            sc = _dot(q_ref[rows, :], kb, 1, 1)
            if row0 is not None:
                sc = _causal_mask(sc, row0)
            m_prev = m_s[rows, :]
            m_new = jnp.maximum(m_prev, jnp.max(sc, axis=-1, keepdims=True))
            p = jnp.exp2(sc - jnp.tile(m_new, (1, kb.shape[0] // LANES)))
            a = jnp.exp2(m_prev - m_new)
            acc_s[rows, :] = jnp.tile(a, (1, 2)) * acc_s[rows, :] + _dot(p.astype(BF16), vb, 1, 0)
            m_s[rows, :] = m_new

        def step(j):
            kv_rows = pl.ds(pl.multiple_of(j * bk, bk), bk)
            kb, vb = k_ref[kv_rows, :], vext_s[kv_rows, :]
            for r in range(blk // sub):
                update(slice(r * sub, (r + 1) * sub), kb, vb, None)

        _key_block_loop(step, i * (blk // bk), per_trip, smallest=blk // bk)
        for r in range(blk // sub):
            ncols = (r + 1) * sub
            kv_rows = pl.ds(pl.multiple_of(i * blk, blk), ncols)
            update(slice(r * sub, (r + 1) * sub), k_ref[kv_rows, :], vext_s[kv_rows, :], r * sub)
        acc = acc_s[...]
        l = acc[:, V_HEAD:2 * V_HEAD]
        o = acc[:, 0:V_HEAD] / l
        g = g_ref[...]
        o_ref[...] = o
        u_ref[...] = (o * (g * _sigmoid(g))).astype(u_ref.dtype)
        lse_ref[0] = (m_s[...] + jnp.log2(l))[:, 0:1]

    blk_idx = lambda h, i: (i, h)
    return pl.pallas_call(
        body, name="attention_fwd", grid=(H_B, nb),
        in_specs=[pl.BlockSpec((blk, QK_PAD), blk_idx), pl.BlockSpec((s, QK_PAD), lambda h, i: (0, h)),
                  pl.BlockSpec((s, V_HEAD), lambda h, i: (0, h)), pl.BlockSpec((blk, LANES), lambda h, i: (i, GATE_BLOCK0 + h))],
        out_specs=[pl.BlockSpec((blk, V_HEAD), blk_idx), pl.BlockSpec((blk, V_HEAD), blk_idx),
                   pl.BlockSpec((1, blk, 1), lambda h, i: (h, i, 0))],
        out_shape=[jax.ShapeDtypeStruct((s, WIDTH_B), F32), jax.ShapeDtypeStruct((s, WIDTH_B), BF16),
                   jax.ShapeDtypeStruct((H_B, s, 1), F32)],
        scratch_shapes=[pltpu.VMEM((s, 2 * V_HEAD), BF16), pltpu.VMEM((blk, LANES), F32), pltpu.VMEM((blk, 2 * V_HEAD), F32)],
        compiler_params=_params(("parallel", "arbitrary")),
    )(q, k, v, h1)


def _attention_bwd(q, k, v, du, o, h1, lse, cos, sa, sb, *, blk, bk, per_trip):
    s = q.shape[0]
    nb = s // blk

    def body(q_ref, k_ref, v_ref, du_ref, o_ref, g_ref, lse_ref, c_ref, sa_ref, sb_ref,
             dq_ref, dkn_ref, dkr_ref, dv_ref, dg_ref, lse_s, dl_s, do_s, dq_s, dkn_s, dv_s):
        i = pl.program_id(1)
        g = g_ref[...]
        sg = _sigmoid(g)
        du_v, ov = du_ref[...], o_ref[...]
        do = du_v * (g * sg)
        dg_ref[...] = (du_v * ov * (sg * (1.0 + g * (1.0 - sg)))).astype(dg_ref.dtype)
        do_s[...] = do.astype(do_s.dtype)
        dl_s[...] = jnp.broadcast_to(jnp.sum(do * ov, axis=-1, keepdims=True), (blk, LANES))
        lse_s[...] = jnp.broadcast_to(lse_ref[0], (blk, LANES))
        dq_s[...] = jnp.zeros_like(dq_s)

        def products(rows, kv_rows, row0):
            qv, dov, kb = q_ref[rows, :], do_s[rows, :], k_ref[kv_rows, :]
            tile = (1, kb.shape[0] // LANES)
            sc = _dot(qv, kb, 1, 1)
            if row0 is not None:
                sc = _causal_mask(sc, row0)
            p = jnp.exp2(sc - jnp.tile(lse_s[rows, :], tile))
            dp = _dot(dov, v_ref[kv_rows, :], 1, 1)
            ds = (p * (dp - jnp.tile(dl_s[rows, :], tile))).astype(BF16)
            dq_s[rows, :] += _dot(ds, kb, 1, 0)
            return _dot(ds, qv, 0, 0), _dot(p.astype(BF16), dov, 0, 0)

        def put(kv_rows, dk_c, dv_c, first):
            if first:
                dkn_s[kv_rows, :] = dk_c[:, 0:128]
                dkr_ref[kv_rows, :] = dk_c[:, 128:256]
                dv_s[kv_rows, :] = dv_c
            else:
                dkn_s[kv_rows, :] += dk_c[:, 0:128]
                dkr_ref[kv_rows, :] += dk_c[:, 128:256]
                dv_s[kv_rows, :] += dv_c

        n_sub = blk // bk
        sub_rows = [slice(r * bk, (r + 1) * bk) for r in range(n_sub)]

        def step(j):
            kv_rows = pl.ds(pl.multiple_of(j * bk, bk), bk)
            for rows in sub_rows:
                dk_c, dv_c = products(rows, kv_rows, None)
                put(kv_rows, dk_c, dv_c, False)

        _key_block_loop(step, i * n_sub, per_trip, smallest=n_sub)
        for c in range(n_sub):
            kv_rows = pl.ds(pl.multiple_of(i * blk + c * bk, bk), bk)
            for r in range(c, n_sub):
                dk_c, dv_c = products(sub_rows[r], kv_rows, 0 if r == c else None)
                put(kv_rows, dk_c, dv_c, r == c)
        dq = dq_s[...] * ATT_SCALE
        dq_ref[:, 0:128] = dq[:, 0:128].astype(dq_ref.dtype)
        dq_ref[:, 128:256] = _rope_b(dq[:, 128:256], c_ref[...], sa_ref[...], sb_ref[...], -1.0).astype(dq_ref.dtype)

        @pl.when(i == nb - 1)
        def _():
            dkn_ref[...] = (dkn_s[...] * LN2).astype(dkn_ref.dtype)
            dv_ref[...] = dv_s[...].astype(dv_ref.dtype)
            dkr_ref[...] = dkr_ref[...] * LN2

    head = lambda h, i: (0, h)
    blk_idx = lambda h, i: (i, h)
    row_idx = lambda h, i: (i, 0)
    return pl.pallas_call(
        body, name="attention_bwd", grid=(H_B, nb),
        in_specs=[pl.BlockSpec((blk, QK_PAD), blk_idx), pl.BlockSpec((s, QK_PAD), head), pl.BlockSpec((s, V_HEAD), head),
                  pl.BlockSpec((blk, V_HEAD), blk_idx), pl.BlockSpec((blk, V_HEAD), blk_idx),
                  pl.BlockSpec((blk, LANES), lambda h, i: (i, GATE_BLOCK0 + h)), pl.BlockSpec((1, blk, 1), lambda h, i: (h, i, 0)),
                  pl.BlockSpec((blk, LANES), row_idx), pl.BlockSpec((blk, LANES), row_idx), pl.BlockSpec((blk, LANES), row_idx)],
        out_specs=[pl.BlockSpec((blk, QK_PAD), blk_idx), pl.BlockSpec((s, 128), head), pl.BlockSpec((s, 128), head),
                   pl.BlockSpec((s, 128), head), pl.BlockSpec((blk, V_HEAD), blk_idx)],
        out_shape=[jax.ShapeDtypeStruct((s, H_B * QK_PAD), BF16), jax.ShapeDtypeStruct((s, H_B * 128), BF16),
                   jax.ShapeDtypeStruct((s, H_B * 128), F32), jax.ShapeDtypeStruct((s, H_B * 128), BF16),
                   jax.ShapeDtypeStruct((s, WIDTH_B), BF16)],
        scratch_shapes=[pltpu.VMEM((blk, LANES), F32), pltpu.VMEM((blk, LANES), F32), pltpu.VMEM((blk, V_HEAD), BF16),
                        pltpu.VMEM((blk, QK_PAD), F32), pltpu.VMEM((s, 128), F32), pltpu.VMEM((s, V_HEAD), F32)],
        compiler_params=_params(("parallel", "arbitrary")),
    )(q, k, v, du, o, h1, lse, cos, sa, sb)


def _local_step(x, target, w, kv_norm, q_norm, ln_g, ln_b, *, ts=512, blk=512, late=None, reduce=None):
    s = x.shape[0]
    cos_a, sin_a = _rope_tables_a(s)
    cos_b, sa_b, sb_b = _rope_tables_b(s)
    tables = _retention_tables()
    g0, g1, b0, b1 = ln_g[0:1], ln_g[1:2], ln_b[0:1], ln_b[1:2]

    x16 = x.astype(BF16)
    if late is None:
        h_a = _mm(x16, w["a_in"], tn=1536, name="a_in_fwd")
    else:
        flat_b, chip = late
        h_a, got = _mm(x16, w["a_in"], tn=1536, name="a_in_fwd", side=_gather_side(flat_b))
        got = lax.dynamic_update_slice(got, flat_b[None], (chip, 0, 0))
        w = {**w, **_kernel_layout_b(_full_from_gathered(got, B_SHARDS))}
    q_a, k_a, v_a, o_a, u_a, states = _retention_fwd(h_a, cos_a, sin_a, tables)
    y_a = _mm(u_a, w["a_out"], tn=1024, name="a_out_fwd")
    x1, x1_16 = _ln_fwd(x, y_a, g0, b0, ts=ts)

    h1 = _mm(x1_16, w["b_in1"], tn=1152, name="b_in_fwd")
    lat16, kr16, qn16 = _kvq_prep(h1, kv_norm, q_norm, cos_b, sa_b, sb_b, ts=ts)
    k16 = _mm(lat16, w["up_k"], out_dtype=BF16, tn=2048, out_tn=H_B * QK_PAD, extras=(kr16,), epilogue=_assemble_k_store, name="up_k_fwd")
    v16 = _mm(lat16, w["up_v"], out_dtype=BF16, tn=2048, name="up_v_fwd")
    q16 = _mm(qn16, w["uq"], out_dtype=BF16, tn=2048, extras=(cos_b, sa_b, sb_b), epilogue=_rope_q_store, name="uq_fwd")
    o_b, u_b, lse = _attention_fwd(q16, k16, v16, h1, blk=2 * blk, bk=blk, sub=blk // 2, per_trip=4)
    y_b = _mm(u_b, w["b_out"], tn=1024, name="b_out_fwd")

    dz_b, dz_b16, dg1, db1, loss = _ln_loss_bwd(x1, y_b, target, g1, b1, ts=ts)
    d_b_out = _mm(u_b, dz_b16, ta=True, tn=1024, tk=2048, name="b_out_dw")
    du_b = _mm(dz_b16, w["b_out"], tb=True, tn=1024, name="b_out_dx")
    dqf16, dkn, dkr_heads, dv, dgate16 = _attention_bwd(q16, k16, v16, du_b, o_b, h1, lse, cos_b, sa_b, sb_b, blk=2 * blk, bk=blk, per_trip=4)
    d_uq = _mm(qn16, dqf16, ta=True, tm=768, tn=2048, tk=2048, name="uq_dw")
    dqn = _mm(dqf16, w["uq"], tb=True, tn=768, name="uq_dx")
    d_up_k = _mm(lat16, dkn, ta=True, tn=2048, tk=2048, name="up_k_dw")
    d_up_v = _mm(lat16, dv, ta=True, tn=2048, tk=2048, name="up_v_dw")
    dlat_k = _mm(dkn, w["up_k"], tb=True, tn=512, name="up_k_dx")
    dlat_v = _mm(dv, w["up_v"], tb=True, tn=512, name="up_v_dx")
    dh1, dkvn, dqnorm = _h1_bwd(h1, dlat_k, dlat_v, dkr_heads, dqn, dgate16, kv_norm, q_norm, cos_b, sa_b, sb_b, ts=ts)
    d_b_in1 = _mm(x1_16, dh1, ta=True, tn=1152, tk=2048, name="b_in_dw")
    dx1 = _mm(dh1, w["b_in1"], tb=True, tn=1024, extras=(dz_b,), epilogue=_residual_store, name="b_in_dx")

    dz_a, dz_a16, dg0, db0 = _ln_bwd_call(dx1, x, y_a, g0, ts=ts)
    d_a_out = _mm(u_a, dz_a16, ta=True, tn=1024, tk=2048, name="a_out_dw")
    du_a = _mm(dz_a16, w["a_out"], tb=True, tn=1024, name="a_out_dx")
    dh_a = _retention_bwd(q_a, k_a, v_a, states, du_a, o_a, h_a, cos_a, sin_a, tables)
    grads = dict(a_out=d_a_out, b_in1=d_b_in1, uq=d_uq, b_out=d_b_out, up_k=d_up_k, up_v=d_up_v)
    small = dict(ln_g=jnp.concatenate([dg0, dg1], axis=0), ln_b=jnp.concatenate([db0, db1], axis=0),
                 q_norm=dqnorm, kv_norm=dkvn)
    if reduce is None:
        grads["a_in"] = _mm(x16, dh_a, ta=True, tn=1536, tk=1024, name="a_in_dw")
        grad_x = _mm(dh_a, w["a_in"], tb=True, tn=1024, tk=2048, extras=(dz_a,), epilogue=_residual_store, name="a_in_dx")
        return loss, grad_x, grads, small
    own_early, travel_early = reduce(_reference_layout_grads(grads), EARLY_SHARDS, EARLY_ROWS, "early")
    d_a_in, got_early = _mm(x16, dh_a, ta=True, tn=1536, tk=2048, name="a_in_dw", side=_chip_exchange_side(travel_early))
    own_late, travel_late = reduce(dict(a_w_in=d_a_in), LATE_SHARDS, LATE_ROWS, "late")
    grad_x, got_late = _mm(dh_a, w["a_in"], tb=True, tn=1024, tk=3072, extras=(dz_a,), epilogue=_residual_store, name="a_in_dx",
                           side=_chip_exchange_side(travel_late))
    return loss, grad_x, ((own_early, got_early), (own_late, got_late)), small


def _flat_shards(shards, dtype, which, total_rows):
    parts = [shards[name].reshape(rows, FLAT_COLS) for name, rows in which]
    used = sum(rows for _, rows in which)
    parts.append(jnp.zeros((total_rows - used, FLAT_COLS), parts[0].dtype))
    return jnp.concatenate(parts, axis=0).astype(dtype)


def _unflat_shards(flat, shapes, shards):
    out, off = {}, 0
    for name, rows in shards:
        out[name] = flat[off:off + rows].reshape(shapes[name])
        off += rows
    return out


COL_SHARDED = {"a_w_in": (D_MODEL, IN_A), "b_w_in": (D_MODEL, IN_B), "b_w_uq": (Q_LORA, H_B * (QK_NOPE + QK_ROPE)),
               "kv_w_up": (KV_LORA, H_B * (QK_NOPE + V_HEAD))}
ROW_SHARDED = {"a_w_out": (WIDTH_A, D_MODEL), "b_w_out": (WIDTH_B, D_MODEL), "kv_w_down": (D_MODEL, KV_LORA + QK_ROPE)}


def _full_from_gathered(gathered, shards=SHARD_ROWS):
    out, off = {}, 0
    for name, rows in shards:
        part = gathered[:, off:off + rows]
        off += rows
        if name in COL_SHARDED:
            r, c = COL_SHARDED[name]
            out[name] = part.reshape(N_CHIPS, r, c // N_CHIPS).transpose(1, 0, 2).reshape(r, c)
        else:
            r, c = ROW_SHARDED[name]
            out[name] = part.reshape(r, c)
    return out


def _chip_major(g):
    r, c = g.shape
    return g.reshape(r, N_CHIPS, c // N_CHIPS).transpose(1, 0, 2)


def _gathered_from_full(full, shards, total_rows):
    if len(shards) == 1 and shards[0][0] in COL_SHARDED:
        return _chip_major(full[shards[0][0]])
    parts = []
    for name, rows in shards:
        g = full[name]
        if name in COL_SHARDED:
            r, c = COL_SHARDED[name]
            g = g.reshape(r, N_CHIPS, c // N_CHIPS).transpose(1, 0, 2)
        parts.append(g.reshape(N_CHIPS, rows, FLAT_COLS))
    used = sum(rows for _, rows in shards)
    if total_rows > used:
        parts.append(jnp.zeros((N_CHIPS, total_rows - used, FLAT_COLS), F32))
    return jnp.concatenate(parts, axis=1)


A_SHARDS, B_SHARDS = SHARD_ROWS[:1], SHARD_ROWS[1:]
A_ROWS = sum(rows for _, rows in A_SHARDS)


def _kernel_layout_a(full):
    return dict(a_in=full["a_w_in"])


def _kernel_layout_b(full):
    uq = full["b_w_uq"].reshape(Q_LORA, H_B, QK_NOPE + QK_ROPE)
    uq = jnp.pad(uq, ((0, 0), (0, 0), (0, QK_PAD - QK_NOPE - QK_ROPE))).reshape(Q_LORA, H_B * QK_PAD)
    up = full["kv_w_up"].reshape(KV_LORA, H_B, QK_NOPE + V_HEAD)
    down = jnp.pad(full["kv_w_down"], ((0, 0), (0, KV_DOWN_PAD - KV_LORA - QK_ROPE)))
    return dict(a_out=full["a_w_out"], b_out=full["b_w_out"], uq=uq,
                up_k=up[:, :, :QK_NOPE].reshape(KV_LORA, H_B * QK_NOPE),
                up_v=up[:, :, QK_NOPE:].reshape(KV_LORA, H_B * V_HEAD),
                b_in1=jnp.concatenate([down, full["b_w_in"]], axis=1))


def _kernel_layout(full):
    return {**_kernel_layout_a(full), **_kernel_layout_b(full)}


EARLY_SHARDS = tuple(sh for sh in SHARD_ROWS if sh[0] != "a_w_in")
LATE_SHARDS = tuple(sh for sh in SHARD_ROWS if sh[0] == "a_w_in")
EARLY_ROWS, LATE_ROWS = 3072, 1536


def _reference_layout_grads(g):
    uq = g["uq"].reshape(Q_LORA, H_B, QK_PAD)[:, :, :QK_NOPE + QK_ROPE].reshape(Q_LORA, H_B * (QK_NOPE + QK_ROPE))
    up = jnp.concatenate([g["up_k"].reshape(KV_LORA, H_B, QK_NOPE), g["up_v"].reshape(KV_LORA, H_B, V_HEAD)], axis=2)
    return dict(a_w_out=g["a_out"], b_w_out=g["b_out"], b_w_uq=uq,
                kv_w_up=up.reshape(KV_LORA, H_B * (QK_NOPE + V_HEAD)),
                kv_w_down=g["b_in1"][:, :KV_LORA + QK_ROPE], b_w_in=g["b_in1"][:, KV_DOWN_PAD:])


HBM_SPEC = pl.BlockSpec(memory_space=pl.ANY)


def _me():
    return lax.axis_index("x"), lax.axis_index("y"), lax.axis_index("c")


def _chip_flips(x, y):
    return [(1 - x, y), (x, 1 - y), (1 - x, 1 - y)]


def _gather_copies(src_ref, out_ref, send_sems, recv_sems):
    x, y, c = _me()
    half = src_ref.shape[0] // 2
    my_rows = pl.ds(pl.multiple_of(c * half, 16), half)
    their_rows = pl.ds(pl.multiple_of((1 - c) * half, 16), half)
    chips = _chip_flips(x, y)
    sibling = (x, y, 1 - c)

    def copy(k, src, dst, to):
        return pltpu.make_async_remote_copy(src_ref=src, dst_ref=dst, send_sem=send_sems.at[k], recv_sem=recv_sems.at[k],
                                            device_id=to, device_id_type=MESH)

    sends = [copy(k, src_ref.at[my_rows, :], out_ref.at[2 * x + y, my_rows, :], (px, py, c)) for k, (px, py) in enumerate(chips)]
    landed = [out_ref.at[2 * px + py, my_rows, :] for px, py in chips]
    lands = [copy(k, landed[k], landed[k], (px, py, c)) for k, (px, py) in enumerate(chips)]
    forwards = [copy(3 + k, landed[k], landed[k], sibling) for k in range(3)]
    theirs = [out_ref.at[2 * px + py, their_rows, :] for px, py in chips]
    arrivals = [copy(3 + k, theirs[k], theirs[k], sibling) for k in range(3)]
    return sends, lands, forwards, arrivals


def _gather_start(src_ref, out_ref, send_sems, recv_sems):
    sends, _, _, _ = _gather_copies(src_ref, out_ref, send_sems, recv_sems)
    for cp in sends:
        cp.start()


def _gather_finish(src_ref, out_ref, send_sems, recv_sems):
    sends, lands, forwards, arrivals = _gather_copies(src_ref, out_ref, send_sems, recv_sems)
    for k in range(3):
        lands[k].wait_recv()
        forwards[k].start()
    for cp in arrivals:
        cp.wait_recv()
    for cp in sends + forwards:
        cp.wait_send()


def _gather_scratch():
    return [pltpu.SemaphoreType.DMA((6,)), pltpu.SemaphoreType.DMA((6,))]


def _gather_weights(flat16):
    def body(src_ref, out_ref, send_sems, recv_sems):
        _gather_start(src_ref, out_ref, send_sems, recv_sems)
        _gather_finish(src_ref, out_ref, send_sems, recv_sems)

    return pl.pallas_call(
        body, name="gather_weights",
        in_specs=[HBM_SPEC], out_specs=HBM_SPEC,
        out_shape=jax.ShapeDtypeStruct((N_CHIPS,) + flat16.shape, flat16.dtype),
        scratch_shapes=_gather_scratch(),
    )(flat16)


def _gather_side(flat16):
    return dict(inputs=[flat16], out_shape=jax.ShapeDtypeStruct((N_CHIPS,) + flat16.shape, flat16.dtype),
                scratch=_gather_scratch(), start=_gather_start, finish=_gather_finish)


def _chip_exchange_copies(p_ref, out_ref, send_sems, recv_sems):
    x, y, c = _me()
    return [pltpu.make_async_remote_copy(
        src_ref=p_ref.at[2 * px + py], dst_ref=out_ref.at[k], send_sem=send_sems.at[k], recv_sem=recv_sems.at[k],
        device_id=(px, py, c), device_id_type=MESH) for k, (px, py) in enumerate(_chip_flips(x, y))]


def _chip_exchange_start(p_ref, out_ref, send_sems, recv_sems):
    for cp in _chip_exchange_copies(p_ref, out_ref, send_sems, recv_sems):
        cp.start()


def _chip_exchange_finish(p_ref, out_ref, send_sems, recv_sems):
    copies = _chip_exchange_copies(p_ref, out_ref, send_sems, recv_sems)
    for cp in copies:
        cp.wait_send()
    for cp in copies:
        cp.wait_recv()


def _chip_exchange_side(p):
    return dict(inputs=[p], out_shape=jax.ShapeDtypeStruct((3,) + p.shape[1:], p.dtype),
                scratch=[pltpu.SemaphoreType.DMA((3,)), pltpu.SemaphoreType.DMA((3,))],
                start=_chip_exchange_start, finish=_chip_exchange_finish)


def _pair_swap(r, name):
    def body(r_ref, out_ref, send_sem, recv_sem):
        x, y, c = _me()
        cp = pltpu.make_async_remote_copy(src_ref=r_ref, dst_ref=out_ref, send_sem=send_sem, recv_sem=recv_sem,
                                          device_id=(x, y, 1 - c), device_id_type=MESH)
        cp.start()
        cp.wait_send()
        cp.wait_recv()

    return pl.pallas_call(
        body, name=name,
        in_specs=[HBM_SPEC], out_specs=HBM_SPEC,
        out_shape=jax.ShapeDtypeStruct(r.shape, r.dtype),
        scratch_shapes=[pltpu.SemaphoreType.DMA, pltpu.SemaphoreType.DMA],
    )(r)


def _sum_small(vec):
    def body(v_ref, out_ref, slots, send_sems, recv_sems):
        x, y, c = _me()
        me = 4 * x + 2 * y + c
        slots[me] = v_ref[...]
        flips = [(fx, fy, fc) for fx in (0, 1) for fy in (0, 1) for fc in (0, 1)][1:]
        copies = []
        for k, (fx, fy, fc) in enumerate(flips):
            copies.append(pltpu.make_async_remote_copy(
                src_ref=v_ref, dst_ref=slots.at[me], send_sem=send_sems.at[k], recv_sem=recv_sems.at[k],
                device_id=(x ^ fx, y ^ fy, c ^ fc), device_id_type=MESH))
        for cp in copies:
            cp.start()
        for cp in copies:
            cp.wait_send()
        for k, (fx, fy, fc) in enumerate(flips):
            src = 4 * (x ^ fx) + 2 * (y ^ fy) + (c ^ fc)
            pltpu.make_async_remote_copy(
                src_ref=v_ref, dst_ref=slots.at[src], send_sem=send_sems.at[k], recv_sem=recv_sems.at[k],
                device_id=(x ^ fx, y ^ fy, c ^ fc), device_id_type=MESH).wait_recv()
        total = slots[0]
        for d in range(1, N_DEV):
            total = total + slots[d]
        out_ref[...] = total

    return pl.pallas_call(
        body, name="sum_small",
        in_specs=[pl.BlockSpec(memory_space=pltpu.VMEM)], out_specs=pl.BlockSpec(memory_space=pltpu.VMEM),
        out_shape=jax.ShapeDtypeStruct(vec.shape, vec.dtype),
        scratch_shapes=[pltpu.VMEM((N_DEV,) + vec.shape, vec.dtype), pltpu.SemaphoreType.DMA((7,)),
                        pltpu.SemaphoreType.DMA((7,))],
    )(vec)


UPD_ROWS = 256


def _pair_sum(g, theirs, core, chip, name):
    half, cols = theirs.shape[1:]
    nb = half // UPD_ROWS

    def body(core_ref, chip_ref, g_ref, t_ref, own_ref, o16_ref):
        total = g_ref[0] + t_ref[0].astype(F32)
        o16_ref[0] = total.astype(o16_ref.dtype)

        @pl.when(pl.program_id(1) == chip_ref[0])
        def _():
            own_ref[...] = total

    return pl.pallas_call(
        body, name=name,
        grid_spec=pltpu.PrefetchScalarGridSpec(
            num_scalar_prefetch=2, grid=(nb, N_CHIPS),
            in_specs=[pl.BlockSpec((1, UPD_ROWS, cols), lambda i, d, core_ref, chip_ref: (d, core_ref[0] * nb + i, 0)),
                      pl.BlockSpec((1, UPD_ROWS, cols), lambda i, d, core_ref, chip_ref: (d, i, 0))],
            out_specs=[pl.BlockSpec((UPD_ROWS, cols), lambda i, d, core_ref, chip_ref: (i, 0)),
                       pl.BlockSpec((1, UPD_ROWS, cols), lambda i, d, core_ref, chip_ref: (d, i, 0))]),
        out_shape=[jax.ShapeDtypeStruct((half, cols), F32),
                   jax.ShapeDtypeStruct((N_CHIPS, half, cols), BF16)],
        compiler_params=_params(("parallel", "arbitrary")),
    )(core, chip, g, theirs)


def _chip_sum(own, received, name):
    half, cols = own.shape
    nb = half // UPD_ROWS

    def body(p_ref, r_ref, o_ref):
        o_ref[...] = ((p_ref[...] + r_ref[0].astype(F32)) + r_ref[1].astype(F32)) + r_ref[2].astype(F32)

    return pl.pallas_call(
        body, name=name, grid=(nb,),
        in_specs=[pl.BlockSpec((UPD_ROWS, cols), lambda i: (i, 0)),
                  pl.BlockSpec((3, UPD_ROWS, cols), lambda i: (0, i, 0))],
        out_specs=pl.BlockSpec((UPD_ROWS, cols), lambda i: (i, 0)),
        out_shape=jax.ShapeDtypeStruct((half, cols), F32),
        compiler_params=_params(("parallel",)),
    )(own, received)


def _adamw(w, g, m, v, *, rows, name):
    r, c = w.shape
    rows = min(rows, r)
    assert r % rows == 0

    def body(w_ref, g_ref, m_ref, v_ref, d_ref, nm_ref, nv_ref):
        gv = g_ref[...]
        nm = ADAM_B1 * m_ref[...] + (1.0 - ADAM_B1) * gv
        nv = ADAM_B2 * v_ref[...] + (1.0 - ADAM_B2) * (gv * gv)
        m_hat = nm / (1.0 - ADAM_B1 ** ADAM_STEP)
        v_hat = nv / (1.0 - ADAM_B2 ** ADAM_STEP)
        d_ref[...] = -ADAM_LR * (m_hat / (jnp.sqrt(v_hat) + ADAM_EPS) + ADAM_WD * w_ref[...])
        nm_ref[...] = nm
        nv_ref[...] = nv

    spec = pl.BlockSpec((rows, c), lambda i: (i, 0))
    return pl.pallas_call(
        body, name=name, grid=(r // rows,),
        in_specs=[spec] * 4, out_specs=[spec] * 3,
        out_shape=[jax.ShapeDtypeStruct((r, c), F32)] * 3,
        compiler_params=_params(("parallel",)),
    )(w, g, m, v)


W_NAMES = ("a_w_in", "a_w_out", "b_w_in", "b_q_norm", "b_w_uq", "b_w_out", "kv_w_down", "kv_norm", "kv_w_up", "ln_g", "ln_b")
BIG = tuple(name for name, _ in SHARD_ROWS)


def _pack_small(ln_g, ln_b, q_norm, kv_norm, extra=None):
    pad = lambda a: jnp.pad(a.reshape(1, -1), ((0, 0), (0, FLAT_COLS - a.size)))
    rows = [ln_g, ln_b, pad(q_norm), pad(kv_norm),
            jnp.zeros((1, FLAT_COLS), F32) if extra is None else pad(extra), jnp.zeros((1, FLAT_COLS), F32)]
    return jnp.concatenate(rows, axis=0)


def _unpack_small(p):
    return dict(ln_g=p[0:2], ln_b=p[2:4], b_q_norm=p[4:5, :Q_LORA], kv_norm=p[5, :KV_LORA])


def kernel(x, a_w_in, a_w_out, b_w_in, b_q_norm, b_w_uq, b_w_out, kv_w_down, kv_norm, kv_w_up, ln_g, ln_b, loss_target, m_a_w_in, m_a_w_out, m_b_w_in, m_b_q_norm, m_b_w_uq, m_b_w_out, m_kv_w_down, m_kv_norm, m_kv_w_up, m_ln_g, m_ln_b, v_a_w_in, v_a_w_out, v_b_w_in, v_b_q_norm, v_b_w_uq, v_b_w_out, v_kv_w_down, v_kv_norm, v_kv_w_up, v_ln_g, v_ln_b):
    w_in = dict(a_w_in=a_w_in[0], a_w_out=a_w_out[0], b_w_in=b_w_in[0], b_w_uq=b_w_uq[0], b_w_out=b_w_out[0],
                kv_w_down=kv_w_down, kv_w_up=kv_w_up)
    m_in = dict(a_w_in=m_a_w_in[0], a_w_out=m_a_w_out[0], b_w_in=m_b_w_in[0], b_w_uq=m_b_w_uq[0], b_w_out=m_b_w_out[0],
                kv_w_down=m_kv_w_down, kv_w_up=m_kv_w_up)
    v_in = dict(a_w_in=v_a_w_in[0], a_w_out=v_a_w_out[0], b_w_in=v_b_w_in[0], b_w_uq=v_b_w_uq[0], b_w_out=v_b_w_out[0],
                kv_w_down=v_kv_w_down, kv_w_up=v_kv_w_up)
    shard_shapes = {name: w_in[name].shape for name in BIG}

    cx, cy, cc = lax.axis_index("x"), lax.axis_index("y"), lax.axis_index("c")
    chip = 2 * cx + cy
    flat_b = _flat_shards(w_in, BF16, B_SHARDS, FLAT_ROWS - A_ROWS)
    a16 = w_in["a_w_in"].astype(BF16)
    got_a = lax.dynamic_update_slice(_gather_weights(a16), a16[None], (chip, 0, 0))
    weights_a = dict(a_in=got_a.transpose(1, 0, 2).reshape(D_MODEL, IN_A))

    core_arr, chip_arr = cc.astype(jnp.int32).reshape(1), chip.astype(jnp.int32).reshape(1)

    def reduce_pair(full, shards, rows, tag):
        g_all = _gathered_from_full(full, shards, rows)
        half, cols = g_all.shape[1] // 2, g_all.shape[2]
        other_half = lax.dynamic_slice(g_all, (0, (1 - cc) * half, 0), (N_CHIPS, half, cols)).astype(BF16)
        theirs = _pair_swap(other_half, "pair_exchange_" + tag)
        return _pair_sum(g_all, theirs, core_arr, chip_arr, "pair_sum_" + tag)

    loss, grad_x, reduced, small = _local_step(x[0], loss_target[0], weights_a, kv_norm.reshape(1, -1), b_q_norm, ln_g, ln_b,
                                               late=(flat_b, chip), reduce=reduce_pair)

    g_big = {}
    for (own, received), shards, tag in zip(reduced, (EARLY_SHARDS, LATE_SHARDS), ("early", "late")):
        mine = _chip_sum(own, received, "chip_sum_" + tag)
        sibling = _pair_swap(mine, "pair_share_" + tag)
        g_flat = jnp.concatenate([jnp.where(cc == 0, mine, sibling), jnp.where(cc == 0, sibling, mine)], axis=0)
        if g_flat.shape == shard_shapes[shards[0][0]]:
            g_big[shards[0][0]] = g_flat
        else:
            g_big.update(_unflat_shards(g_flat, shard_shapes, shards))

    small_sum = _sum_small(_pack_small(small["ln_g"], small["ln_b"], small["q_norm"], small["kv_norm"], loss[:, :1]))
    loss_out = small_sum[6, 0]

    row = lambda a: a.reshape(1, -1)
    g_all = {**g_big, **_unpack_small(small_sum)}
    g_all["kv_norm"] = row(g_all["kv_norm"])
    state = {**{name: (w_in[name], m_in[name], v_in[name]) for name in BIG},
             "ln_g": (ln_g, m_ln_g, v_ln_g), "ln_b": (ln_b, m_ln_b, v_ln_b), "b_q_norm": (b_q_norm, m_b_q_norm, v_b_q_norm),
             "kv_norm": (row(kv_norm), row(m_kv_norm), row(v_kv_norm))}
    upd = {name: _adamw(state[name][0], g_all[name], state[name][1], state[name][2], rows=256, name="adamw_" + name)
           for name in W_NAMES}

    def shaped(name, a):
        if name == "kv_norm":
            return a.reshape(-1)
        return a[None] if name in ("a_w_in", "a_w_out", "b_w_in", "b_w_uq", "b_w_out") else a

    outputs = [[shaped(name, g_all[name]) for name in W_NAMES]]
    outputs += [[shaped(name, upd[name][k]) for name in W_NAMES] for k in range(3)]
    return (loss_out, grad_x[None], *outputs[0], *outputs[1], *outputs[2], *outputs[3])
```

```python
import functools
import math

import jax
import jax.numpy as jnp
from jax import lax
from jax.experimental import pallas as pl
from jax.experimental.pallas import tpu as pltpu

F32 = jnp.float32
BF16 = jnp.bfloat16
MESH = pl.DeviceIdType.MESH

D_MODEL = 1024
DEPTH = 2
H_A, DK_A, DV_A = 4, 256, 512
WIDTH_A = H_A * DV_A
CHUNK = 128
H_B, QK_NOPE, QK_ROPE, V_HEAD = 16, 128, 64, 128
QK_PAD = 256
Q_LORA, KV_LORA = 768, 512
KV_DOWN_PAD = 640
WIDTH_B = H_B * V_HEAD
IN_A = 2 * H_A * DK_A + 2 * WIDTH_A
IN_B = Q_LORA + WIDTH_B
H1_B = KV_DOWN_PAD + IN_B
ROPE_BASE = 10000.0
ALPHA = (2.0 * DEPTH) ** 0.25
ATT_SCALE = (QK_NOPE + QK_ROPE) ** -0.5
NEG_BIG = -1e30

ADAM_LR, ADAM_B1, ADAM_B2, ADAM_EPS, ADAM_WD, ADAM_STEP = 0.001, 0.9, 0.999, 1e-08, 0.01, 10

VMEM_LIMIT_BYTES = 56 * 1024 * 1024
LANES = 128
FLAT_COLS = 1024
SHARD_ROWS = (("a_w_in", 1536), ("a_w_out", 512), ("b_w_in", 704), ("b_w_uq", 576), ("b_w_out", 512),
              ("kv_w_down", 144), ("kv_w_up", 512))
FLAT_ROWS = 4608
N_CHIPS = 4
N_DEV = 8


def _params(sem, vmem=VMEM_LIMIT_BYTES):
    return pltpu.CompilerParams(dimension_semantics=sem, vmem_limit_bytes=vmem)


def _row_spec(ts, w, col_block=0):
    return pl.BlockSpec((ts, w), lambda i: (i, col_block))


def _bc_spec(shape):
    nd = len(shape)
    return pl.BlockSpec(shape, lambda i: (0,) * nd)


def _sigmoid(x):
    return 1.0 / (1.0 + jnp.exp(-x))


def _fold8(v):
    ts, w = v.shape
    return jnp.sum(v.reshape(ts // 8, 8, w), axis=0)


def _mm(a, b, *, ta=False, tb=False, out_dtype=F32, tm=1024, tn=512, tk=None, name, extras=(), epilogue=None, out_tn=None,
        side=None):
    if ta:
        K, M = a.shape
    else:
        M, K = a.shape
    if tb:
        N, Kb = b.shape
    else:
        Kb, N = b.shape
    assert K == Kb, (a.shape, b.shape)
    tm, tn = min(tm, M), min(tn, N)
    tk = K if tk is None else min(tk, K)
    assert M % tm == 0 and N % tn == 0 and K % tk == 0, (name, M, N, K, tm, tn, tk)
    grid = (M // tm, N // tn, K // tk)
    nk = grid[2]
    out_tn = tn if out_tn is None else out_tn
    n_extra = len(extras)
    side_inputs = [] if side is None else list(side["inputs"])
    n_side = len(side_inputs)
    n_acc = 0 if nk == 1 else 1
    dims = (((0,) if ta else (1,), (1,) if tb else (0,)), ((), ()))

    def body(a_ref, b_ref, *rest):
        extra_refs, rest = rest[:n_extra], rest[n_extra:]
        side_in, o_ref, rest = rest[:n_side], rest[n_side], rest[n_side + 1:]
        if side is not None:
            side_refs, rest = side_in + rest[:1] + rest[1 + n_acc:], rest[1:]
            ids = [pl.program_id(d) for d in range(3)]

            @pl.when((ids[0] == 0) & (ids[1] == 0) & (ids[2] == 0))
            def _():
                side["start"](*side_refs)

        prod = lax.dot_general(a_ref[...].astype(BF16), b_ref[...].astype(BF16), dims,
                               preferred_element_type=F32)

        def store(tile):
            if epilogue is None:
                o_ref[...] = tile.astype(o_ref.dtype)
            else:
                epilogue(tile, o_ref, *extra_refs)

        if nk == 1:
            store(prod)
        else:
            acc = rest[0]
            k = pl.program_id(2)

            @pl.when(k == 0)
            def _():
                acc[...] = prod

            @pl.when(k > 0)
            def _():
                acc[...] += prod

            @pl.when(k == nk - 1)
            def _():
                store(acc[...])

        if side is not None:
            @pl.when((ids[0] == grid[0] - 1) & (ids[1] == grid[1] - 1) & (ids[2] == grid[2] - 1))
            def _():
                side["finish"](*side_refs)

    a_spec = pl.BlockSpec((tk, tm), lambda i, j, k: (k, i)) if ta else pl.BlockSpec((tm, tk), lambda i, j, k: (i, k))
    b_spec = pl.BlockSpec((tn, tk), lambda i, j, k: (j, k)) if tb else pl.BlockSpec((tk, tn), lambda i, j, k: (k, j))
    extra_specs = [pl.BlockSpec((tm, e.shape[1]), lambda i, j, k: (i, 0)) for e in extras]
    out_specs = pl.BlockSpec((tm, out_tn), lambda i, j, k: (i, j))
    out_shape = jax.ShapeDtypeStruct((M, (N // tn) * out_tn), out_dtype)
    scratch = [] if nk == 1 else [pltpu.VMEM((tm, tn), F32)]
    if side is not None:
        out_specs, out_shape, scratch = [out_specs, HBM_SPEC], [out_shape, side["out_shape"]], scratch + list(side["scratch"])
    return pl.pallas_call(
        body, name=name, grid=grid,
        in_specs=[a_spec, b_spec] + extra_specs + [HBM_SPEC] * n_side,
        out_specs=out_specs, out_shape=out_shape, scratch_shapes=scratch,
        compiler_params=_params(("parallel", "parallel", "arbitrary") if side is None else ("arbitrary",) * 3),
    )(a, b, *extras, *side_inputs)


def _rope_tables_a(s):
    half = DK_A // 2
    inv = ROPE_BASE ** (-jnp.arange(half, dtype=F32) / half)
    ang = jnp.arange(s, dtype=F32)[:, None] * inv[None, :]
    return jnp.cos(ang), jnp.sin(ang)


def _rope_tables_b(s):
    half = QK_ROPE // 2
    inv = ROPE_BASE ** (-jnp.arange(half, dtype=F32) / half)
    ang = jnp.arange(s, dtype=F32)[:, None] * inv[None, :]
    c, sn = jnp.cos(ang), jnp.sin(ang)
    z = jnp.zeros_like(c)
    cos = jnp.concatenate([c, c, z, z], axis=1)
    sa = jnp.concatenate([-sn, z, z, z], axis=1)
    sb = jnp.concatenate([z, sn, z, z], axis=1)
    return cos, sa, sb


def _rope_b(r, cos, sa, sb, sign):
    return r * cos + sign * (pltpu.roll(r, 96, 1) * sa + pltpu.roll(r, 32, 1) * sb)


def _retention_tables():
    lg = jnp.log1p(-jnp.exp2(-5.0 - jnp.arange(H_A, dtype=F32)))
    idx = jnp.arange(CHUNK, dtype=F32)
    diff = idx[:, None] - idx[None, :]
    causal = diff >= 0
    dmat = jnp.where(causal, jnp.exp(jnp.where(causal, diff, 0.0)[None] * lg[:, None, None]), 0.0)
    qdec = jnp.exp((idx + 1.0)[None, :] * lg[:, None])[:, :, None]
    kdec = jnp.exp((CHUNK - 1.0 - idx)[None, :] * lg[:, None])[:, :, None]
    cdec = jnp.broadcast_to(jnp.exp(CHUNK * lg)[:, None, None], (H_A, 1, DV_A))
    return dmat, qdec, kdec, cdec


def _group_norm(o):
    mu = jnp.mean(o, axis=-1, keepdims=True)
    oc = o - mu
    var = jnp.mean(oc * oc, axis=-1, keepdims=True)
    rstd = lax.rsqrt(var + 1e-5)
    return oc * rstd, rstd


Q_COL, K_COL, V_COL, GATE_COL = 0, H_A * DK_A, 2 * H_A * DK_A, 2 * H_A * DK_A + WIDTH_A


def _ln_stats(z):
    mu = jnp.mean(z, axis=-1, keepdims=True)
    zc = z - mu
    var = jnp.mean(zc * zc, axis=-1, keepdims=True)
    rstd = lax.rsqrt(var + 1e-5)
    return zc * rstd, rstd


def _ln_bwd(dy, xhat, rstd, g):
    dxh = dy * g
    m1 = jnp.mean(dxh, axis=-1, keepdims=True)
    m2 = jnp.mean(dxh * xhat, axis=-1, keepdims=True)
    return rstd * (dxh - m1 - xhat * m2)


def _ln_fwd(x, y, g, b, *, ts):
    s = x.shape[0]

    def body(x_ref, y_ref, g_ref, b_ref, o_ref, o16_ref):
        xhat, _ = _ln_stats(ALPHA * x_ref[...] + y_ref[...])
        out = xhat * g_ref[...] + b_ref[...]
        o_ref[...] = out
        o16_ref[...] = out.astype(o16_ref.dtype)

    return pl.pallas_call(
        body, name="ln_fwd", grid=(s // ts,),
        in_specs=[_row_spec(ts, D_MODEL), _row_spec(ts, D_MODEL), _bc_spec((1, D_MODEL)), _bc_spec((1, D_MODEL))],
        out_specs=[_row_spec(ts, D_MODEL), _row_spec(ts, D_MODEL)],
        out_shape=[jax.ShapeDtypeStruct((s, D_MODEL), F32), jax.ShapeDtypeStruct((s, D_MODEL), BF16)],
        compiler_params=_params(("parallel",)),
    )(x, y, g, b)


def _ln_loss_bwd(x1, y, target, g, b, *, ts):
    s = x1.shape[0]
    n = s // ts

    def body(x_ref, y_ref, t_ref, g_ref, b_ref, dz_ref, dz16_ref, dg_ref, db_ref, loss_ref, ag, ab, al):
        i = pl.program_id(0)

        @pl.when(i == 0)
        def _():
            ag[...] = jnp.zeros_like(ag)
            ab[...] = jnp.zeros_like(ab)
            al[...] = jnp.zeros_like(al)

        xhat, rstd = _ln_stats(ALPHA * x_ref[...] + y_ref[...])
        err = xhat * g_ref[...] + b_ref[...] - t_ref[...]
        al[...] += _fold8(err * err)
        dy = err * (1.0 / D_MODEL)
        ag[...] += _fold8(dy * xhat)
        ab[...] += _fold8(dy)
        dz = _ln_bwd(dy, xhat, rstd, g_ref[...])
        dz_ref[...] = dz
        dz16_ref[...] = dz.astype(dz16_ref.dtype)

        @pl.when(i == n - 1)
        def _():
            dg_ref[...] = jnp.sum(ag[...], axis=0, keepdims=True)
            db_ref[...] = jnp.sum(ab[...], axis=0, keepdims=True)
            loss_ref[...] = jnp.full((1, LANES), (0.5 / D_MODEL) * jnp.sum(al[...]), F32)

    return pl.pallas_call(
        body, name="ln_loss_bwd", grid=(n,),
        in_specs=[_row_spec(ts, D_MODEL)] * 3 + [_bc_spec((1, D_MODEL))] * 2,
        out_specs=[_row_spec(ts, D_MODEL), _row_spec(ts, D_MODEL), _bc_spec((1, D_MODEL)), _bc_spec((1, D_MODEL)),
                   _bc_spec((1, LANES))],
        out_shape=[jax.ShapeDtypeStruct((s, D_MODEL), F32), jax.ShapeDtypeStruct((s, D_MODEL), BF16),
                   jax.ShapeDtypeStruct((1, D_MODEL), F32), jax.ShapeDtypeStruct((1, D_MODEL), F32),
                   jax.ShapeDtypeStruct((1, LANES), F32)],
        scratch_shapes=[pltpu.VMEM((8, D_MODEL), F32)] * 3,
        compiler_params=_params(("arbitrary",)),
    )(x1, y, target, g, b)


def _ln_bwd_call(dy, x, y, g, *, ts):
    s = x.shape[0]
    n = s // ts

    def body(dy_ref, x_ref, y_ref, g_ref, dz_ref, dz16_ref, dg_ref, db_ref, ag, ab):
        i = pl.program_id(0)

        @pl.when(i == 0)
        def _():
            ag[...] = jnp.zeros_like(ag)
            ab[...] = jnp.zeros_like(ab)

        xhat, rstd = _ln_stats(ALPHA * x_ref[...] + y_ref[...])
        dy = dy_ref[...]
        ag[...] += _fold8(dy * xhat)
        ab[...] += _fold8(dy)
        dz = _ln_bwd(dy, xhat, rstd, g_ref[...])
        dz_ref[...] = dz
        dz16_ref[...] = dz.astype(dz16_ref.dtype)

        @pl.when(i == n - 1)
        def _():
            dg_ref[...] = jnp.sum(ag[...], axis=0, keepdims=True)
            db_ref[...] = jnp.sum(ab[...], axis=0, keepdims=True)

    return pl.pallas_call(
        body, name="ln_bwd", grid=(n,),
        in_specs=[_row_spec(ts, D_MODEL)] * 3 + [_bc_spec((1, D_MODEL))],
        out_specs=[_row_spec(ts, D_MODEL), _row_spec(ts, D_MODEL), _bc_spec((1, D_MODEL)), _bc_spec((1, D_MODEL))],
        out_shape=[jax.ShapeDtypeStruct((s, D_MODEL), F32), jax.ShapeDtypeStruct((s, D_MODEL), BF16),
                   jax.ShapeDtypeStruct((1, D_MODEL), F32), jax.ShapeDtypeStruct((1, D_MODEL), F32)],
        scratch_shapes=[pltpu.VMEM((8, D_MODEL), F32)] * 2,
        compiler_params=_params(("arbitrary",)),
    )(dy, x, y, g)


def _residual_store(tile, o_ref, dz_ref):
    o_ref[...] = ALPHA * dz_ref[...] + tile


C_LAT = slice(0, KV_LORA)
C_ROPE = slice(KV_LORA, KV_DOWN_PAD)
C_QL = slice(KV_DOWN_PAD, KV_DOWN_PAD + Q_LORA)
C_GATE = slice(KV_DOWN_PAD + Q_LORA, H1_B)


def _rms(x, eps=1e-6):
    r = lax.rsqrt(jnp.mean(x * x, axis=-1, keepdims=True) + eps)
    return x * r, r


def _rms_bwd(dy, xhat, r, g):
    dxh = dy * g
    return r * (dxh - xhat * jnp.mean(dxh * xhat, axis=-1, keepdims=True))


def _kvq_prep(h1, kv_norm, q_norm, cos, sa, sb, *, ts):
    s = h1.shape[0]

    def body(h_ref, kn_ref, qn_ref, c_ref, sa_ref, sb_ref, lat_ref, kr_ref, ql_ref):
        lat, _ = _rms(h_ref[:, C_LAT])
        lat_ref[...] = (lat * kn_ref[...]).astype(lat_ref.dtype)
        kr_ref[...] = _rope_b(h_ref[:, C_ROPE], c_ref[...], sa_ref[...], sb_ref[...], 1.0).astype(kr_ref.dtype)
        ql, _ = _rms(h_ref[:, C_QL])
        ql_ref[...] = (ql * qn_ref[...]).astype(ql_ref.dtype)

    return pl.pallas_call(
        body, name="kvq_prep", grid=(s // ts,),
        in_specs=[_row_spec(ts, H1_B), _bc_spec((1, KV_LORA)), _bc_spec((1, Q_LORA))] + [_row_spec(ts, 128)] * 3,
        out_specs=[_row_spec(ts, KV_LORA), _row_spec(ts, 128), _row_spec(ts, Q_LORA)],
        out_shape=[jax.ShapeDtypeStruct((s, KV_LORA), BF16), jax.ShapeDtypeStruct((s, 128), BF16),
                   jax.ShapeDtypeStruct((s, Q_LORA), BF16)],
        compiler_params=_params(("parallel",)),
    )(h1, kv_norm, q_norm, cos, sa, sb)


LOG2E = 1.4426950408889634
LN2 = 0.6931471805599453
Q_SCALE = ATT_SCALE * LOG2E


def _rope_q_store(tile, o_ref, c_ref, sa_ref, sb_ref):
    c, a, b = c_ref[...], sa_ref[...], sb_ref[...]
    for hd in range(tile.shape[1] // QK_PAD):
        lo = hd * QK_PAD
        o_ref[:, lo:lo + 128] = (tile[:, lo:lo + 128] * Q_SCALE).astype(o_ref.dtype)
        o_ref[:, lo + 128:lo + 256] = (_rope_b(tile[:, lo + 128:lo + 256], c, a, b, 1.0) * Q_SCALE).astype(o_ref.dtype)


def _assemble_k_store(tile, o_ref, kr_ref):
    r = kr_ref[...]
    for hd in range(tile.shape[1] // QK_NOPE):
        o_ref[:, hd * QK_PAD:hd * QK_PAD + 128] = tile[:, hd * 128:(hd + 1) * 128].astype(o_ref.dtype)
        o_ref[:, hd * QK_PAD + 128:(hd + 1) * QK_PAD] = r


def _h1_bwd(h1, dlat_k, dlat_v, dkr_heads, dqn, dg16, kv_norm, q_norm, cos, sa, sb, *, ts):
    s = h1.shape[0]
    n = s // ts

    def body(h_ref, dk_ref, dv_ref, dkr_ref, dqn_ref, dg_ref, kn_ref, qn_ref, c_ref, sa_ref, sb_ref,
             o_ref, dkn_ref, dqn_out_ref, akn, aqn):
        i = pl.program_id(0)

        @pl.when(i == 0)
        def _():
            akn[...] = jnp.zeros_like(akn)
            aqn[...] = jnp.zeros_like(aqn)

        lat, r = _rms(h_ref[:, C_LAT])
        dlat = dk_ref[...] + dv_ref[...]
        akn[...] += _fold8(dlat * lat)
        o_ref[:, C_LAT] = _rms_bwd(dlat, lat, r, kn_ref[...]).astype(o_ref.dtype)

        dkr = dkr_ref[:, 0:128]
        for hd in range(1, H_B):
            dkr = dkr + dkr_ref[:, hd * 128:(hd + 1) * 128]
        o_ref[:, C_ROPE] = _rope_b(dkr, c_ref[...], sa_ref[...], sb_ref[...], -1.0).astype(o_ref.dtype)

        ql, rq = _rms(h_ref[:, C_QL])
        dq = dqn_ref[...]
        aqn[...] += _fold8(dq * ql)
        o_ref[:, C_QL] = _rms_bwd(dq, ql, rq, qn_ref[...]).astype(o_ref.dtype)
        o_ref[:, C_GATE] = dg_ref[...]

        @pl.when(i == n - 1)
        def _():
            dkn_ref[...] = jnp.sum(akn[...], axis=0, keepdims=True)
            dqn_out_ref[...] = jnp.sum(aqn[...], axis=0, keepdims=True)

    return pl.pallas_call(
        body, name="h1_bwd", grid=(n,),
        in_specs=[_row_spec(ts, H1_B), _row_spec(ts, KV_LORA), _row_spec(ts, KV_LORA), _row_spec(ts, H_B * 128),
                  _row_spec(ts, Q_LORA), _row_spec(ts, WIDTH_B), _bc_spec((1, KV_LORA)), _bc_spec((1, Q_LORA))]
        + [_row_spec(ts, 128)] * 3,
        out_specs=[_row_spec(ts, H1_B), _bc_spec((1, KV_LORA)), _bc_spec((1, Q_LORA))],
        out_shape=[jax.ShapeDtypeStruct((s, H1_B), BF16), jax.ShapeDtypeStruct((1, KV_LORA), F32),
                   jax.ShapeDtypeStruct((1, Q_LORA), F32)],
        scratch_shapes=[pltpu.VMEM((8, KV_LORA), F32), pltpu.VMEM((8, Q_LORA), F32)],
        compiler_params=_params(("arbitrary",)),
    )(h1, dlat_k, dlat_v, dkr_heads, dqn, dg16, kv_norm, q_norm, cos, sa, sb)


def _dot(a, b, ca, cb):
    return lax.dot_general(a, b, (((ca,), (cb,)), ((), ())), preferred_element_type=F32)


CHUNKS_PER_STEP = 2


def _table_specs():
    full = lambda shape: pl.BlockSpec(shape, lambda i: (0,) * len(shape))
    return [full((H_A, CHUNK, CHUNK)), full((H_A, CHUNK, 1)), full((H_A, CHUNK, 1)), full((H_A, 1, DV_A))]


def _retention_fwd(h_a, cos, sin, tables):
    s = h_a.shape[0]
    n = s // CHUNK

    def body(h_ref, c_ref, s_ref, dm_ref, qd_ref, kd_ref, cd_ref, q_ref, k_ref, v_ref, o_ref, u_ref, st_ref, state):
        @pl.when(pl.program_id(0) == 0)
        def _():
            state[...] = jnp.zeros_like(state)

        for cb in range(CHUNKS_PER_STEP):
            rows = slice(cb * CHUNK, (cb + 1) * CHUNK)
            c, sn = c_ref[rows, :], s_ref[rows, :]
            for hd in range(H_A):
                qs, vs = slice(hd * DK_A, (hd + 1) * DK_A), slice(hd * DV_A, (hd + 1) * DV_A)
                for r_ref, base, scale in ((q_ref, Q_COL, 1.0), (k_ref, K_COL, DK_A ** -0.5)):
                    lo = base + hd * DK_A
                    x1, x2 = h_ref[rows, lo:lo + 128], h_ref[rows, lo + 128:lo + 256]
                    r_ref[rows, hd * DK_A:hd * DK_A + 128] = ((x1 * c - x2 * sn) * scale).astype(r_ref.dtype)
                    r_ref[rows, hd * DK_A + 128:(hd + 1) * DK_A] = ((x2 * c + x1 * sn) * scale).astype(r_ref.dtype)
                v_ref[rows, vs] = h_ref[rows, V_COL + hd * DV_A:V_COL + (hd + 1) * DV_A].astype(v_ref.dtype)
                qv, kv, vv = q_ref[rows, qs], k_ref[rows, qs], v_ref[rows, vs]
                st = state[hd]
                st16 = st.astype(BF16)
                st_ref[cb, hd] = st16
                scores = _dot(qv, kv, 1, 1) * dm_ref[hd]
                qd = (qv.astype(F32) * qd_ref[hd]).astype(BF16)
                o = _dot(scores.astype(BF16), vv, 1, 0) + _dot(qd, st16, 1, 0)
                o_ref[rows, vs] = o
                kd = (kv.astype(F32) * kd_ref[hd]).astype(BF16)
                state[hd] = st * cd_ref[hd] + _dot(kd, vv, 0, 0)
                on, _ = _group_norm(o)
                g = h_ref[rows, GATE_COL + hd * DV_A:GATE_COL + (hd + 1) * DV_A]
                u_ref[rows, vs] = (on * (g * _sigmoid(g))).astype(u_ref.dtype)

    row = lambda w: pl.BlockSpec((CHUNKS_PER_STEP * CHUNK, w), lambda i: (i, 0))
    return pl.pallas_call(
        body, name="retention_fwd", grid=(n // CHUNKS_PER_STEP,),
        in_specs=[row(IN_A), row(128), row(128)] + _table_specs(),
        out_specs=[row(H_A * DK_A), row(H_A * DK_A), row(WIDTH_A), row(WIDTH_A), row(WIDTH_A),
                   pl.BlockSpec((CHUNKS_PER_STEP, H_A, DK_A, DV_A), lambda i: (i, 0, 0, 0))],
        out_shape=[jax.ShapeDtypeStruct((s, H_A * DK_A), BF16), jax.ShapeDtypeStruct((s, H_A * DK_A), BF16),
                   jax.ShapeDtypeStruct((s, WIDTH_A), BF16), jax.ShapeDtypeStruct((s, WIDTH_A), F32),
                   jax.ShapeDtypeStruct((s, WIDTH_A), BF16), jax.ShapeDtypeStruct((n, H_A, DK_A, DV_A), BF16)],
        scratch_shapes=[pltpu.VMEM((H_A, DK_A, DV_A), F32)],
        compiler_params=_params(("arbitrary",)),
    )(h_a, cos, sin, *tables)


def _retention_bwd(q, k, v, states, du, o, h_a, cos, sin, tables):
    s = q.shape[0]
    n = s // CHUNK

    def body(q_ref, k_ref, v_ref, st_ref, du_ref, o_ref, g_ref, c_ref, s_ref, dm_ref, qd_ref, kd_ref, cd_ref, dh_ref, grad_state):
        @pl.when(pl.program_id(0) == 0)
        def _():
            grad_state[...] = jnp.zeros_like(grad_state)

        for cb in reversed(range(CHUNKS_PER_STEP)):
            rows = slice(cb * CHUNK, (cb + 1) * CHUNK)
            c, sn = c_ref[rows, :], s_ref[rows, :]
            for hd in range(H_A):
                qs, vs = slice(hd * DK_A, (hd + 1) * DK_A), slice(hd * DV_A, (hd + 1) * DV_A)
                on, rstd = _group_norm(o_ref[rows, vs])
                g = g_ref[rows, vs]
                sg = _sigmoid(g)
                du_v = du_ref[rows, vs]
                don = du_v * (g * sg)
                dh_ref[rows, GATE_COL + hd * DV_A:GATE_COL + (hd + 1) * DV_A] = (du_v * on * (sg * (1.0 + g * (1.0 - sg)))).astype(dh_ref.dtype)
                m1 = jnp.mean(don, axis=-1, keepdims=True)
                m2 = jnp.mean(don * on, axis=-1, keepdims=True)
                dov = (rstd * (don - m1 - on * m2)).astype(BF16)

                qv, kv, vv = q_ref[rows, qs], k_ref[rows, qs], v_ref[rows, vs]
                dm = dm_ref[hd]
                gs = grad_state[hd]
                g16 = gs.astype(BF16)
                scores = (_dot(qv, kv, 1, 1) * dm).astype(BF16)
                dscores = (_dot(dov, vv, 1, 1) * dm).astype(BF16)
                qd = (qv.astype(F32) * qd_ref[hd]).astype(BF16)
                kd = (kv.astype(F32) * kd_ref[hd]).astype(BF16)
                dq = _dot(dscores, kv, 1, 0) + _dot(dov, st_ref[cb, hd], 1, 1) * qd_ref[hd]
                dk = (_dot(dscores, qv, 0, 0) + _dot(vv, g16, 1, 1) * kd_ref[hd]) * (DK_A ** -0.5)
                dh_ref[rows, V_COL + hd * DV_A:V_COL + (hd + 1) * DV_A] = (_dot(scores, dov, 0, 0) + _dot(kd, g16, 1, 0)).astype(dh_ref.dtype)
                grad_state[hd] = gs * cd_ref[hd] + _dot(qd, dov, 0, 0)
                for d, base in ((dq, Q_COL), (dk, K_COL)):
                    lo = base + hd * DK_A
                    d1, d2 = d[:, 0:128], d[:, 128:256]
                    dh_ref[rows, lo:lo + 128] = (d1 * c + d2 * sn).astype(dh_ref.dtype)
                    dh_ref[rows, lo + 128:lo + 256] = (d2 * c - d1 * sn).astype(dh_ref.dtype)

    steps = n // CHUNKS_PER_STEP
    rev = lambda i: steps - 1 - i
    row = lambda w, col=0: pl.BlockSpec((CHUNKS_PER_STEP * CHUNK, w), lambda i: (rev(i), col))
    return pl.pallas_call(
        body, name="retention_bwd", grid=(steps,),
        in_specs=[row(H_A * DK_A), row(H_A * DK_A), row(WIDTH_A),
                  pl.BlockSpec((CHUNKS_PER_STEP, H_A, DK_A, DV_A), lambda i: (rev(i), 0, 0, 0)),
                  row(WIDTH_A), row(WIDTH_A), row(WIDTH_A, GATE_COL // WIDTH_A), row(128), row(128)] + _table_specs(),
        out_specs=row(IN_A),
        out_shape=jax.ShapeDtypeStruct((s, IN_A), BF16),
        scratch_shapes=[pltpu.VMEM((H_A, DK_A, DV_A), F32)],
        compiler_params=_params(("arbitrary",)),
    )(q, k, v, states, du, o, h_a, cos, sin, *tables)


GATE_BLOCK0 = (KV_DOWN_PAD + Q_LORA) // LANES


def _causal_mask(sc, row0):
    row = lax.broadcasted_iota(jnp.int32, sc.shape, 0) + row0
    col = lax.broadcasted_iota(jnp.int32, sc.shape, 1)
    return jnp.where(col <= row, sc, NEG_BIG)


def _key_block_loop(step, n, per_trip, smallest=1):
    def trip_body(jj, carry):
        for t in range(per_trip):
            step(per_trip * jj + t)
        return carry

    lax.fori_loop(0, n // per_trip, trip_body, 0)
    group = per_trip // 2
    while group >= smallest:
        def tail(group=group):
            first = (n // (2 * group)) * (2 * group)
            for t in range(group):
                step(first + t)

        pl.when((n // group) % 2 == 1)(tail)
        group //= 2


def _attention_fwd(q, k, v, h1, *, blk, bk, sub, per_trip):
    s = q.shape[0]
    nb = s // blk

    def body(q_ref, k_ref, v_ref, g_ref, o_ref, u_ref, lse_ref, vext_s, m_s, acc_s):
        i = pl.program_id(1)

        @pl.when(i == 0)
        def _():
            vext_s[:, 0:V_HEAD] = v_ref[...]
            vext_s[:, V_HEAD:2 * V_HEAD] = jnp.ones((s, V_HEAD), vext_s.dtype)

        m_s[...] = jnp.full_like(m_s, NEG_BIG)
        acc_s[...] = jnp.zeros_like(acc_s)

        def update(rows, kb, vb, row0):
            sc = _dot(q_ref[rows, :], kb, 1, 1)
            if row0 is not None:
                sc = _causal_mask(sc, row0)
            m_prev = m_s[rows, :]
            m_new = jnp.maximum(m_prev, jnp.max(sc, axis=-1, keepdims=True))
            p = jnp.exp2(sc - jnp.tile(m_new, (1, kb.shape[0] // LANES)))
            a = jnp.exp2(m_prev - m_new)
            acc_s[rows, :] = jnp.tile(a, (1, 2)) * acc_s[rows, :] + _dot(p.astype(BF16), vb, 1, 0)
            m_s[rows, :] = m_new

        def step(j):
            kv_rows = pl.ds(pl.multiple_of(j * bk, bk), bk)
            kb, vb = k_ref[kv_rows, :], vext_s[kv_rows, :]
            for r in range(blk // sub):
                update(slice(r * sub, (r + 1) * sub), kb, vb, None)

        _key_block_loop(step, i * (blk // bk), per_trip, smallest=blk // bk)
        for r in range(blk // sub):
            ncols = (r + 1) * sub
            kv_rows = pl.ds(pl.multiple_of(i * blk, blk), ncols)
            update(slice(r * sub, (r + 1) * sub), k_ref[kv_rows, :], vext_s[kv_rows, :], r * sub)
        acc = acc_s[...]
        l = acc[:, V_HEAD:2 * V_HEAD]
        o = acc[:, 0:V_HEAD] / l
        g = g_ref[...]
        o_ref[...] = o
        u_ref[...] = (o * (g * _sigmoid(g))).astype(u_ref.dtype)
        lse_ref[0] = (m_s[...] + jnp.log2(l))[:, 0:1]

    blk_idx = lambda h, i: (i, h)
    return pl.pallas_call(
        body, name="attention_fwd", grid=(H_B, nb),
        in_specs=[pl.BlockSpec((blk, QK_PAD), blk_idx), pl.BlockSpec((s, QK_PAD), lambda h, i: (0, h)),
                  pl.BlockSpec((s, V_HEAD), lambda h, i: (0, h)), pl.BlockSpec((blk, LANES), lambda h, i: (i, GATE_BLOCK0 + h))],
        out_specs=[pl.BlockSpec((blk, V_HEAD), blk_idx), pl.BlockSpec((blk, V_HEAD), blk_idx),
                   pl.BlockSpec((1, blk, 1), lambda h, i: (h, i, 0))],
        out_shape=[jax.ShapeDtypeStruct((s, WIDTH_B), F32), jax.ShapeDtypeStruct((s, WIDTH_B), BF16),
                   jax.ShapeDtypeStruct((H_B, s, 1), F32)],
        scratch_shapes=[pltpu.VMEM((s, 2 * V_HEAD), BF16), pltpu.VMEM((blk, LANES), F32), pltpu.VMEM((blk, 2 * V_HEAD), F32)],
        compiler_params=_params(("parallel", "arbitrary")),
    )(q, k, v, h1)


def _attention_bwd(q, k, v, du, o, h1, lse, cos, sa, sb, *, blk, bk, per_trip):
    s = q.shape[0]
    nb = s // blk

    def body(q_ref, k_ref, v_ref, du_ref, o_ref, g_ref, lse_ref, c_ref, sa_ref, sb_ref,
             dq_ref, dkn_ref, dkr_ref, dv_ref, dg_ref, lse_s, dl_s, do_s, dq_s, dkn_s, dv_s):
        i = pl.program_id(1)
        g = g_ref[...]
        sg = _sigmoid(g)
        du_v, ov = du_ref[...], o_ref[...]
        do = du_v * (g * sg)
        dg_ref[...] = (du_v * ov * (sg * (1.0 + g * (1.0 - sg)))).astype(dg_ref.dtype)
        do_s[...] = do.astype(do_s.dtype)
        dl_s[...] = jnp.broadcast_to(jnp.sum(do * ov, axis=-1, keepdims=True), (blk, LANES))
        lse_s[...] = jnp.broadcast_to(lse_ref[0], (blk, LANES))
        dq_s[...] = jnp.zeros_like(dq_s)

        def products(rows, kv_rows, row0):
            qv, dov, kb = q_ref[rows, :], do_s[rows, :], k_ref[kv_rows, :]
            tile = (1, kb.shape[0] // LANES)
            sc = _dot(qv, kb, 1, 1)
            if row0 is not None:
                sc = _causal_mask(sc, row0)
            p = jnp.exp2(sc - jnp.tile(lse_s[rows, :], tile))
            dp = _dot(dov, v_ref[kv_rows, :], 1, 1)
            ds = (p * (dp - jnp.tile(dl_s[rows, :], tile))).astype(BF16)
            dq_s[rows, :] += _dot(ds, kb, 1, 0)
            return _dot(ds, qv, 0, 0), _dot(p.astype(BF16), dov, 0, 0)

        def put(kv_rows, dk_c, dv_c, first):
            if first:
                dkn_s[kv_rows, :] = dk_c[:, 0:128]
                dkr_ref[kv_rows, :] = dk_c[:, 128:256]
                dv_s[kv_rows, :] = dv_c
            else:
                dkn_s[kv_rows, :] += dk_c[:, 0:128]
                dkr_ref[kv_rows, :] += dk_c[:, 128:256]
                dv_s[kv_rows, :] += dv_c

        n_sub = blk // bk
        sub_rows = [slice(r * bk, (r + 1) * bk) for r in range(n_sub)]

        def step(j):
            kv_rows = pl.ds(pl.multiple_of(j * bk, bk), bk)
            for rows in sub_rows:
                dk_c, dv_c = products(rows, kv_rows, None)
                put(kv_rows, dk_c, dv_c, False)

        _key_block_loop(step, i * n_sub, per_trip, smallest=n_sub)
        for c in range(n_sub):
            kv_rows = pl.ds(pl.multiple_of(i * blk + c * bk, bk), bk)
            for r in range(c, n_sub):
                dk_c, dv_c = products(sub_rows[r], kv_rows, 0 if r == c else None)
                put(kv_rows, dk_c, dv_c, r == c)
        dq = dq_s[...] * ATT_SCALE
        dq_ref[:, 0:128] = dq[:, 0:128].astype(dq_ref.dtype)
        dq_ref[:, 128:256] = _rope_b(dq[:, 128:256], c_ref[...], sa_ref[...], sb_ref[...], -1.0).astype(dq_ref.dtype)

        @pl.when(i == nb - 1)
        def _():
            dkn_ref[...] = (dkn_s[...] * LN2).astype(dkn_ref.dtype)
            dv_ref[...] = dv_s[...].astype(dv_ref.dtype)
            dkr_ref[...] = dkr_ref[...] * LN2

    head = lambda h, i: (0, h)
    blk_idx = lambda h, i: (i, h)
    row_idx = lambda h, i: (i, 0)
    return pl.pallas_call(
        body, name="attention_bwd", grid=(H_B, nb),
        in_specs=[pl.BlockSpec((blk, QK_PAD), blk_idx), pl.BlockSpec((s, QK_PAD), head), pl.BlockSpec((s, V_HEAD), head),
                  pl.BlockSpec((blk, V_HEAD), blk_idx), pl.BlockSpec((blk, V_HEAD), blk_idx),
                  pl.BlockSpec((blk, LANES), lambda h, i: (i, GATE_BLOCK0 + h)), pl.BlockSpec((1, blk, 1), lambda h, i: (h, i, 0)),
                  pl.BlockSpec((blk, LANES), row_idx), pl.BlockSpec((blk, LANES), row_idx), pl.BlockSpec((blk, LANES), row_idx)],
        out_specs=[pl.BlockSpec((blk, QK_PAD), blk_idx), pl.BlockSpec((s, 128), head), pl.BlockSpec((s, 128), head),
                   pl.BlockSpec((s, 128), head), pl.BlockSpec((blk, V_HEAD), blk_idx)],
        out_shape=[jax.ShapeDtypeStruct((s, H_B * QK_PAD), BF16), jax.ShapeDtypeStruct((s, H_B * 128), BF16),
                   jax.ShapeDtypeStruct((s, H_B * 128), F32), jax.ShapeDtypeStruct((s, H_B * 128), BF16),
                   jax.ShapeDtypeStruct((s, WIDTH_B), BF16)],
        scratch_shapes=[pltpu.VMEM((blk, LANES), F32), pltpu.VMEM((blk, LANES), F32), pltpu.VMEM((blk, V_HEAD), BF16),
                        pltpu.VMEM((blk, QK_PAD), F32), pltpu.VMEM((s, 128), F32), pltpu.VMEM((s, V_HEAD), F32)],
        compiler_params=_params(("parallel", "arbitrary")),
    )(q, k, v, du, o, h1, lse, cos, sa, sb)


def _local_step(x, target, w, kv_norm, q_norm, ln_g, ln_b, *, ts=512, blk=512, late=None, reduce=None):
    s = x.shape[0]
    cos_a, sin_a = _rope_tables_a(s)
    cos_b, sa_b, sb_b = _rope_tables_b(s)
    tables = _retention_tables()
    g0, g1, b0, b1 = ln_g[0:1], ln_g[1:2], ln_b[0:1], ln_b[1:2]

    x16 = x.astype(BF16)
    if late is None:
        h_a = _mm(x16, w["a_in"], tn=1536, name="a_in_fwd")
    else:
        flat_b, chip = late
        h_a, got = _mm(x16, w["a_in"], tn=1536, name="a_in_fwd", side=_gather_side(flat_b))
        got = lax.dynamic_update_slice(got, flat_b[None], (chip, 0, 0))
        w = {**w, **_kernel_layout_b(_full_from_gathered(got, B_SHARDS))}
    q_a, k_a, v_a, o_a, u_a, states = _retention_fwd(h_a, cos_a, sin_a, tables)
    y_a = _mm(u_a, w["a_out"], tn=1024, name="a_out_fwd")
    x1, x1_16 = _ln_fwd(x, y_a, g0, b0, ts=ts)

    h1 = _mm(x1_16, w["b_in1"], tn=1152, name="b_in_fwd")
    lat16, kr16, qn16 = _kvq_prep(h1, kv_norm, q_norm, cos_b, sa_b, sb_b, ts=ts)
    k16 = _mm(lat16, w["up_k"], out_dtype=BF16, tn=2048, out_tn=H_B * QK_PAD, extras=(kr16,), epilogue=_assemble_k_store, name="up_k_fwd")
    v16 = _mm(lat16, w["up_v"], out_dtype=BF16, tn=2048, name="up_v_fwd")
    q16 = _mm(qn16, w["uq"], out_dtype=BF16, tn=2048, extras=(cos_b, sa_b, sb_b), epilogue=_rope_q_store, name="uq_fwd")
    o_b, u_b, lse = _attention_fwd(q16, k16, v16, h1, blk=2 * blk, bk=blk, sub=blk // 2, per_trip=4)
    y_b = _mm(u_b, w["b_out"], tn=1024, name="b_out_fwd")

    dz_b, dz_b16, dg1, db1, loss = _ln_loss_bwd(x1, y_b, target, g1, b1, ts=ts)
    d_b_out = _mm(u_b, dz_b16, ta=True, tn=1024, tk=2048, name="b_out_dw")
    du_b = _mm(dz_b16, w["b_out"], tb=True, tn=2048, name="b_out_dx")
    dqf16, dkn, dkr_heads, dv, dgate16 = _attention_bwd(q16, k16, v16, du_b, o_b, h1, lse, cos_b, sa_b, sb_b, blk=2 * blk, bk=blk, per_trip=4)
    d_uq = _mm(qn16, dqf16, ta=True, tm=768, tn=2048, tk=2048, name="uq_dw")
    dqn = _mm(dqf16, w["uq"], tb=True, tn=768, name="uq_dx")
    d_up_k = _mm(lat16, dkn, ta=True, tn=2048, tk=2048, name="up_k_dw")
    d_up_v = _mm(lat16, dv, ta=True, tn=2048, tk=2048, name="up_v_dw")
    dlat_k = _mm(dkn, w["up_k"], tb=True, tn=512, name="up_k_dx")
    dlat_v = _mm(dv, w["up_v"], tb=True, tn=512, name="up_v_dx")
    dh1, dkvn, dqnorm = _h1_bwd(h1, dlat_k, dlat_v, dkr_heads, dqn, dgate16, kv_norm, q_norm, cos_b, sa_b, sb_b, ts=ts)
    d_b_in1 = _mm(x1_16, dh1, ta=True, tn=1152, tk=2048, name="b_in_dw")
    dx1 = _mm(dh1, w["b_in1"], tb=True, tn=1024, extras=(dz_b,), epilogue=_residual_store, name="b_in_dx")

    dz_a, dz_a16, dg0, db0 = _ln_bwd_call(dx1, x, y_a, g0, ts=ts)
    d_a_out = _mm(u_a, dz_a16, ta=True, tn=1024, tk=2048, name="a_out_dw")
    du_a = _mm(dz_a16, w["a_out"], tb=True, tn=2048, name="a_out_dx")
    dh_a = _retention_bwd(q_a, k_a, v_a, states, du_a, o_a, h_a, cos_a, sin_a, tables)
    grads = dict(a_out=d_a_out, b_in1=d_b_in1, uq=d_uq, b_out=d_b_out, up_k=d_up_k, up_v=d_up_v)
    small = dict(ln_g=jnp.concatenate([dg0, dg1], axis=0), ln_b=jnp.concatenate([db0, db1], axis=0),
                 q_norm=dqnorm, kv_norm=dkvn)
    if reduce is None:
        grads["a_in"] = _mm(x16, dh_a, ta=True, tn=1536, tk=1024, name="a_in_dw")
        grad_x = _mm(dh_a, w["a_in"], tb=True, tn=1024, tk=2048, extras=(dz_a,), epilogue=_residual_store, name="a_in_dx")
        return loss, grad_x, grads, small
    own_early, travel_early = reduce(_reference_layout_grads(grads), EARLY_SHARDS, EARLY_ROWS, "early")
    d_a_in, got_early = _mm(x16, dh_a, ta=True, tn=1536, tk=2048, name="a_in_dw", side=_chip_exchange_side(travel_early))
    own_late, travel_late = reduce(dict(a_w_in=d_a_in), LATE_SHARDS, LATE_ROWS, "late")
    grad_x, got_late = _mm(dh_a, w["a_in"], tb=True, tn=1024, tk=3072, extras=(dz_a,), epilogue=_residual_store, name="a_in_dx",
                           side=_chip_exchange_side(travel_late))
    return loss, grad_x, ((own_early, got_early), (own_late, got_late)), small


def _flat_shards(shards, dtype, which, total_rows):
    parts = [shards[name].reshape(rows, FLAT_COLS) for name, rows in which]
    used = sum(rows for _, rows in which)
    parts.append(jnp.zeros((total_rows - used, FLAT_COLS), parts[0].dtype))
    return jnp.concatenate(parts, axis=0).astype(dtype)


def _unflat_shards(flat, shapes, shards):
    out, off = {}, 0
    for name, rows in shards:
        out[name] = flat[off:off + rows].reshape(shapes[name])
        off += rows
    return out


COL_SHARDED = {"a_w_in": (D_MODEL, IN_A), "b_w_in": (D_MODEL, IN_B), "b_w_uq": (Q_LORA, H_B * (QK_NOPE + QK_ROPE)),
               "kv_w_up": (KV_LORA, H_B * (QK_NOPE + V_HEAD))}
ROW_SHARDED = {"a_w_out": (WIDTH_A, D_MODEL), "b_w_out": (WIDTH_B, D_MODEL), "kv_w_down": (D_MODEL, KV_LORA + QK_ROPE)}


def _full_from_gathered(gathered, shards=SHARD_ROWS):
    out, off = {}, 0
    for name, rows in shards:
        part = gathered[:, off:off + rows]
        off += rows
        if name in COL_SHARDED:
            r, c = COL_SHARDED[name]
            out[name] = part.reshape(N_CHIPS, r, c // N_CHIPS).transpose(1, 0, 2).reshape(r, c)
        else:
            r, c = ROW_SHARDED[name]
            out[name] = part.reshape(r, c)
    return out


def _chip_major(g):
    r, c = g.shape
    return g.reshape(r, N_CHIPS, c // N_CHIPS).transpose(1, 0, 2)


def _gathered_from_full(full, shards, total_rows):
    if len(shards) == 1 and shards[0][0] in COL_SHARDED:
        return _chip_major(full[shards[0][0]])
    parts = []
    for name, rows in shards:
        g = full[name]
        if name in COL_SHARDED:
            r, c = COL_SHARDED[name]
            g = g.reshape(r, N_CHIPS, c // N_CHIPS).transpose(1, 0, 2)
        parts.append(g.reshape(N_CHIPS, rows, FLAT_COLS))
    used = sum(rows for _, rows in shards)
    if total_rows > used:
        parts.append(jnp.zeros((N_CHIPS, total_rows - used, FLAT_COLS), F32))
    return jnp.concatenate(parts, axis=1)


A_SHARDS, B_SHARDS = SHARD_ROWS[:1], SHARD_ROWS[1:]
A_ROWS = sum(rows for _, rows in A_SHARDS)


def _kernel_layout_a(full):
    return dict(a_in=full["a_w_in"])


def _kernel_layout_b(full):
    uq = full["b_w_uq"].reshape(Q_LORA, H_B, QK_NOPE + QK_ROPE)
    uq = jnp.pad(uq, ((0, 0), (0, 0), (0, QK_PAD - QK_NOPE - QK_ROPE))).reshape(Q_LORA, H_B * QK_PAD)
    up = full["kv_w_up"].reshape(KV_LORA, H_B, QK_NOPE + V_HEAD)
    down = jnp.pad(full["kv_w_down"], ((0, 0), (0, KV_DOWN_PAD - KV_LORA - QK_ROPE)))
    return dict(a_out=full["a_w_out"], b_out=full["b_w_out"], uq=uq,
                up_k=up[:, :, :QK_NOPE].reshape(KV_LORA, H_B * QK_NOPE),
                up_v=up[:, :, QK_NOPE:].reshape(KV_LORA, H_B * V_HEAD),
                b_in1=jnp.concatenate([down, full["b_w_in"]], axis=1))


def _kernel_layout(full):
    return {**_kernel_layout_a(full), **_kernel_layout_b(full)}


EARLY_SHARDS = tuple(sh for sh in SHARD_ROWS if sh[0] != "a_w_in")
LATE_SHARDS = tuple(sh for sh in SHARD_ROWS if sh[0] == "a_w_in")
EARLY_ROWS, LATE_ROWS = 3072, 1536


def _reference_layout_grads(g):
    uq = g["uq"].reshape(Q_LORA, H_B, QK_PAD)[:, :, :QK_NOPE + QK_ROPE].reshape(Q_LORA, H_B * (QK_NOPE + QK_ROPE))
    up = jnp.concatenate([g["up_k"].reshape(KV_LORA, H_B, QK_NOPE), g["up_v"].reshape(KV_LORA, H_B, V_HEAD)], axis=2)
    return dict(a_w_out=g["a_out"], b_w_out=g["b_out"], b_w_uq=uq,
                kv_w_up=up.reshape(KV_LORA, H_B * (QK_NOPE + V_HEAD)),
                kv_w_down=g["b_in1"][:, :KV_LORA + QK_ROPE], b_w_in=g["b_in1"][:, KV_DOWN_PAD:])


HBM_SPEC = pl.BlockSpec(memory_space=pl.ANY)


def _me():
    return lax.axis_index("x"), lax.axis_index("y"), lax.axis_index("c")


def _chip_flips(x, y):
    return [(1 - x, y), (x, 1 - y), (1 - x, 1 - y)]


def _gather_copies(src_ref, out_ref, send_sems, recv_sems):
    x, y, c = _me()
    half = src_ref.shape[0] // 2
    my_rows = pl.ds(pl.multiple_of(c * half, 16), half)
    their_rows = pl.ds(pl.multiple_of((1 - c) * half, 16), half)
    chips = _chip_flips(x, y)
    sibling = (x, y, 1 - c)

    def copy(k, src, dst, to):
        return pltpu.make_async_remote_copy(src_ref=src, dst_ref=dst, send_sem=send_sems.at[k], recv_sem=recv_sems.at[k],
                                            device_id=to, device_id_type=MESH)

    sends = [copy(k, src_ref.at[my_rows, :], out_ref.at[2 * x + y, my_rows, :], (px, py, c)) for k, (px, py) in enumerate(chips)]
    landed = [out_ref.at[2 * px + py, my_rows, :] for px, py in chips]
    lands = [copy(k, landed[k], landed[k], (px, py, c)) for k, (px, py) in enumerate(chips)]
    forwards = [copy(3 + k, landed[k], landed[k], sibling) for k in range(3)]
    theirs = [out_ref.at[2 * px + py, their_rows, :] for px, py in chips]
    arrivals = [copy(3 + k, theirs[k], theirs[k], sibling) for k in range(3)]
    return sends, lands, forwards, arrivals


def _gather_start(src_ref, out_ref, send_sems, recv_sems):
    sends, _, _, _ = _gather_copies(src_ref, out_ref, send_sems, recv_sems)
    for cp in sends:
        cp.start()


def _gather_finish(src_ref, out_ref, send_sems, recv_sems):
    sends, lands, forwards, arrivals = _gather_copies(src_ref, out_ref, send_sems, recv_sems)
    for k in range(3):
        lands[k].wait_recv()
        forwards[k].start()
    for cp in arrivals:
        cp.wait_recv()
    for cp in sends + forwards:
        cp.wait_send()


def _gather_scratch():
    return [pltpu.SemaphoreType.DMA((6,)), pltpu.SemaphoreType.DMA((6,))]


def _gather_weights(flat16):
    def body(src_ref, out_ref, send_sems, recv_sems):
        _gather_start(src_ref, out_ref, send_sems, recv_sems)
        _gather_finish(src_ref, out_ref, send_sems, recv_sems)

    return pl.pallas_call(
        body, name="gather_weights",
        in_specs=[HBM_SPEC], out_specs=HBM_SPEC,
        out_shape=jax.ShapeDtypeStruct((N_CHIPS,) + flat16.shape, flat16.dtype),
        scratch_shapes=_gather_scratch(),
    )(flat16)


def _gather_side(flat16):
    return dict(inputs=[flat16], out_shape=jax.ShapeDtypeStruct((N_CHIPS,) + flat16.shape, flat16.dtype),
                scratch=_gather_scratch(), start=_gather_start, finish=_gather_finish)


def _chip_exchange_copies(p_ref, out_ref, send_sems, recv_sems):
    x, y, c = _me()
    return [pltpu.make_async_remote_copy(
        src_ref=p_ref.at[2 * px + py], dst_ref=out_ref.at[k], send_sem=send_sems.at[k], recv_sem=recv_sems.at[k],
        device_id=(px, py, c), device_id_type=MESH) for k, (px, py) in enumerate(_chip_flips(x, y))]


def _chip_exchange_start(p_ref, out_ref, send_sems, recv_sems):
    for cp in _chip_exchange_copies(p_ref, out_ref, send_sems, recv_sems):
        cp.start()


def _chip_exchange_finish(p_ref, out_ref, send_sems, recv_sems):
    copies = _chip_exchange_copies(p_ref, out_ref, send_sems, recv_sems)
    for cp in copies:
        cp.wait_send()
    for cp in copies:
        cp.wait_recv()


def _chip_exchange_side(p):
    return dict(inputs=[p], out_shape=jax.ShapeDtypeStruct((3,) + p.shape[1:], p.dtype),
                scratch=[pltpu.SemaphoreType.DMA((3,)), pltpu.SemaphoreType.DMA((3,))],
                start=_chip_exchange_start, finish=_chip_exchange_finish)


def _pair_swap(r, name):
    def body(r_ref, out_ref, send_sem, recv_sem):
        x, y, c = _me()
        cp = pltpu.make_async_remote_copy(src_ref=r_ref, dst_ref=out_ref, send_sem=send_sem, recv_sem=recv_sem,
                                          device_id=(x, y, 1 - c), device_id_type=MESH)
        cp.start()
        cp.wait_send()
        cp.wait_recv()

    return pl.pallas_call(
        body, name=name,
        in_specs=[HBM_SPEC], out_specs=HBM_SPEC,
        out_shape=jax.ShapeDtypeStruct(r.shape, r.dtype),
        scratch_shapes=[pltpu.SemaphoreType.DMA, pltpu.SemaphoreType.DMA],
    )(r)


def _sum_small(vec):
    def body(v_ref, out_ref, slots, send_sems, recv_sems):
        x, y, c = _me()
        me = 4 * x + 2 * y + c
        slots[me] = v_ref[...]
        flips = [(fx, fy, fc) for fx in (0, 1) for fy in (0, 1) for fc in (0, 1)][1:]
        copies = []
        for k, (fx, fy, fc) in enumerate(flips):
            copies.append(pltpu.make_async_remote_copy(
                src_ref=v_ref, dst_ref=slots.at[me], send_sem=send_sems.at[k], recv_sem=recv_sems.at[k],
                device_id=(x ^ fx, y ^ fy, c ^ fc), device_id_type=MESH))
        for cp in copies:
            cp.start()
        for cp in copies:
            cp.wait_send()
        for k, (fx, fy, fc) in enumerate(flips):
            src = 4 * (x ^ fx) + 2 * (y ^ fy) + (c ^ fc)
            pltpu.make_async_remote_copy(
                src_ref=v_ref, dst_ref=slots.at[src], send_sem=send_sems.at[k], recv_sem=recv_sems.at[k],
                device_id=(x ^ fx, y ^ fy, c ^ fc), device_id_type=MESH).wait_recv()
        total = slots[0]
        for d in range(1, N_DEV):
            total = total + slots[d]
        out_ref[...] = total

    return pl.pallas_call(
        body, name="sum_small",
        in_specs=[pl.BlockSpec(memory_space=pltpu.VMEM)], out_specs=pl.BlockSpec(memory_space=pltpu.VMEM),
        out_shape=jax.ShapeDtypeStruct(vec.shape, vec.dtype),
        scratch_shapes=[pltpu.VMEM((N_DEV,) + vec.shape, vec.dtype), pltpu.SemaphoreType.DMA((7,)),
                        pltpu.SemaphoreType.DMA((7,))],
    )(vec)


UPD_ROWS = 512


def _pair_sum(g, theirs, core, chip, name):
    half, cols = theirs.shape[1:]
    nb = half // UPD_ROWS

    def body(core_ref, chip_ref, g_ref, t_ref, own_ref, o16_ref):
        total = g_ref[0] + t_ref[0].astype(F32)
        o16_ref[0] = total.astype(o16_ref.dtype)

        @pl.when(pl.program_id(1) == chip_ref[0])
        def _():
            own_ref[...] = total

    return pl.pallas_call(
        body, name=name,
        grid_spec=pltpu.PrefetchScalarGridSpec(
            num_scalar_prefetch=2, grid=(nb, N_CHIPS),
            in_specs=[pl.BlockSpec((1, UPD_ROWS, cols), lambda i, d, core_ref, chip_ref: (d, core_ref[0] * nb + i, 0)),
                      pl.BlockSpec((1, UPD_ROWS, cols), lambda i, d, core_ref, chip_ref: (d, i, 0))],
            out_specs=[pl.BlockSpec((UPD_ROWS, cols), lambda i, d, core_ref, chip_ref: (i, 0)),
                       pl.BlockSpec((1, UPD_ROWS, cols), lambda i, d, core_ref, chip_ref: (d, i, 0))]),
        out_shape=[jax.ShapeDtypeStruct((half, cols), F32),
                   jax.ShapeDtypeStruct((N_CHIPS, half, cols), BF16)],
        compiler_params=_params(("parallel", "arbitrary")),
    )(core, chip, g, theirs)


def _chip_sum(own, received, name):
    half, cols = own.shape
    nb = half // UPD_ROWS

    def body(p_ref, r_ref, o_ref):
        o_ref[...] = ((p_ref[...] + r_ref[0].astype(F32)) + r_ref[1].astype(F32)) + r_ref[2].astype(F32)

    return pl.pallas_call(
        body, name=name, grid=(nb,),
        in_specs=[pl.BlockSpec((UPD_ROWS, cols), lambda i: (i, 0)),
                  pl.BlockSpec((3, UPD_ROWS, cols), lambda i: (0, i, 0))],
        out_specs=pl.BlockSpec((UPD_ROWS, cols), lambda i: (i, 0)),
        out_shape=jax.ShapeDtypeStruct((half, cols), F32),
        compiler_params=_params(("parallel",)),
    )(own, received)


def _adamw(w, g, m, v, *, rows, name):
    r, c = w.shape
    rows = min(rows, r)
    assert r % rows == 0

    def body(w_ref, g_ref, m_ref, v_ref, d_ref, nm_ref, nv_ref):
        gv = g_ref[...]
        nm = ADAM_B1 * m_ref[...] + (1.0 - ADAM_B1) * gv
        nv = ADAM_B2 * v_ref[...] + (1.0 - ADAM_B2) * (gv * gv)
        m_hat = nm / (1.0 - ADAM_B1 ** ADAM_STEP)
        v_hat = nv / (1.0 - ADAM_B2 ** ADAM_STEP)
        d_ref[...] = -ADAM_LR * (m_hat / (jnp.sqrt(v_hat) + ADAM_EPS) + ADAM_WD * w_ref[...])
        nm_ref[...] = nm
        nv_ref[...] = nv

    spec = pl.BlockSpec((rows, c), lambda i: (i, 0))
    return pl.pallas_call(
        body, name=name, grid=(r // rows,),
        in_specs=[spec] * 4, out_specs=[spec] * 3,
        out_shape=[jax.ShapeDtypeStruct((r, c), F32)] * 3,
        compiler_params=_params(("parallel",)),
    )(w, g, m, v)


W_NAMES = ("a_w_in", "a_w_out", "b_w_in", "b_q_norm", "b_w_uq", "b_w_out", "kv_w_down", "kv_norm", "kv_w_up", "ln_g", "ln_b")
BIG = tuple(name for name, _ in SHARD_ROWS)


def _pack_small(ln_g, ln_b, q_norm, kv_norm, extra=None):
    pad = lambda a: jnp.pad(a.reshape(1, -1), ((0, 0), (0, FLAT_COLS - a.size)))
    rows = [ln_g, ln_b, pad(q_norm), pad(kv_norm),
            jnp.zeros((1, FLAT_COLS), F32) if extra is None else pad(extra), jnp.zeros((1, FLAT_COLS), F32)]
    return jnp.concatenate(rows, axis=0)


def _unpack_small(p):
    return dict(ln_g=p[0:2], ln_b=p[2:4], b_q_norm=p[4:5, :Q_LORA], kv_norm=p[5, :KV_LORA])


def kernel(x, a_w_in, a_w_out, b_w_in, b_q_norm, b_w_uq, b_w_out, kv_w_down, kv_norm, kv_w_up, ln_g, ln_b, loss_target, m_a_w_in, m_a_w_out, m_b_w_in, m_b_q_norm, m_b_w_uq, m_b_w_out, m_kv_w_down, m_kv_norm, m_kv_w_up, m_ln_g, m_ln_b, v_a_w_in, v_a_w_out, v_b_w_in, v_b_q_norm, v_b_w_uq, v_b_w_out, v_kv_w_down, v_kv_norm, v_kv_w_up, v_ln_g, v_ln_b):
    w_in = dict(a_w_in=a_w_in[0], a_w_out=a_w_out[0], b_w_in=b_w_in[0], b_w_uq=b_w_uq[0], b_w_out=b_w_out[0],
                kv_w_down=kv_w_down, kv_w_up=kv_w_up)
    m_in = dict(a_w_in=m_a_w_in[0], a_w_out=m_a_w_out[0], b_w_in=m_b_w_in[0], b_w_uq=m_b_w_uq[0], b_w_out=m_b_w_out[0],
                kv_w_down=m_kv_w_down, kv_w_up=m_kv_w_up)
    v_in = dict(a_w_in=v_a_w_in[0], a_w_out=v_a_w_out[0], b_w_in=v_b_w_in[0], b_w_uq=v_b_w_uq[0], b_w_out=v_b_w_out[0],
                kv_w_down=v_kv_w_down, kv_w_up=v_kv_w_up)
    shard_shapes = {name: w_in[name].shape for name in BIG}

    cx, cy, cc = lax.axis_index("x"), lax.axis_index("y"), lax.axis_index("c")
    chip = 2 * cx + cy
    flat_b = _flat_shards(w_in, BF16, B_SHARDS, FLAT_ROWS - A_ROWS)
    a16 = w_in["a_w_in"].astype(BF16)
    got_a = lax.dynamic_update_slice(_gather_weights(a16), a16[None], (chip, 0, 0))
    weights_a = dict(a_in=got_a.transpose(1, 0, 2).reshape(D_MODEL, IN_A))

    core_arr, chip_arr = cc.astype(jnp.int32).reshape(1), chip.astype(jnp.int32).reshape(1)

    def reduce_pair(full, shards, rows, tag):
        g_all = _gathered_from_full(full, shards, rows)
        half, cols = g_all.shape[1] // 2, g_all.shape[2]
        other_half = lax.dynamic_slice(g_all, (0, (1 - cc) * half, 0), (N_CHIPS, half, cols)).astype(BF16)
        theirs = _pair_swap(other_half, "pair_exchange_" + tag)
        return _pair_sum(g_all, theirs, core_arr, chip_arr, "pair_sum_" + tag)

    loss, grad_x, reduced, small = _local_step(x[0], loss_target[0], weights_a, kv_norm.reshape(1, -1), b_q_norm, ln_g, ln_b,
                                               late=(flat_b, chip), reduce=reduce_pair)

    g_big = {}
    for (own, received), shards, tag in zip(reduced, (EARLY_SHARDS, LATE_SHARDS), ("early", "late")):
        mine = _chip_sum(own, received, "chip_sum_" + tag)
        sibling = _pair_swap(mine, "pair_share_" + tag)
        g_flat = jnp.concatenate([jnp.where(cc == 0, mine, sibling), jnp.where(cc == 0, sibling, mine)], axis=0)
        if g_flat.shape == shard_shapes[shards[0][0]]:
            g_big[shards[0][0]] = g_flat
        else:
            g_big.update(_unflat_shards(g_flat, shard_shapes, shards))

    small_sum = _sum_small(_pack_small(small["ln_g"], small["ln_b"], small["q_norm"], small["kv_norm"], loss[:, :1]))
    loss_out = small_sum[6, 0]

    row = lambda a: a.reshape(1, -1)
    g_all = {**g_big, **_unpack_small(small_sum)}
    g_all["kv_norm"] = row(g_all["kv_norm"])
    state = {**{name: (w_in[name], m_in[name], v_in[name]) for name in BIG},
             "ln_g": (ln_g, m_ln_g, v_ln_g), "ln_b": (ln_b, m_ln_b, v_ln_b), "b_q_norm": (b_q_norm, m_b_q_norm, v_b_q_norm),
             "kv_norm": (row(kv_norm), row(m_kv_norm), row(v_kv_norm))}
    upd = {name: _adamw(state[name][0], g_all[name], state[name][1], state[name][2], rows=256, name="adamw_" + name)
           for name in W_NAMES}

    def shaped(name, a):
        if name == "kv_norm":
            return a.reshape(-1)
        return a[None] if name in ("a_w_in", "a_w_out", "b_w_in", "b_w_uq", "b_w_out") else a

    outputs = [[shaped(name, g_all[name]) for name in W_NAMES]]
    outputs += [[shaped(name, upd[name][k]) for name in W_NAMES] for k in range(3)]
    return (loss_out, grad_x[None], *outputs[0], *outputs[1], *outputs[2], *outputs[3])
```

```python
import functools
import math

import jax
import jax.numpy as jnp
from jax import lax
from jax.experimental import pallas as pl
from jax.experimental.pallas import tpu as pltpu

F32 = jnp.float32
BF16 = jnp.bfloat16
MESH = pl.DeviceIdType.MESH

D_MODEL = 1024
DEPTH = 2
H_A, DK_A, DV_A = 4, 256, 512
WIDTH_A = H_A * DV_A
CHUNK = 128
H_B, QK_NOPE, QK_ROPE, V_HEAD = 16, 128, 64, 128
QK_PAD = 256
Q_LORA, KV_LORA = 768, 512
KV_DOWN_PAD = 640
WIDTH_B = H_B * V_HEAD
IN_A = 2 * H_A * DK_A + 2 * WIDTH_A
IN_B = Q_LORA + WIDTH_B
H1_B = KV_DOWN_PAD + IN_B
ROPE_BASE = 10000.0
ALPHA = (2.0 * DEPTH) ** 0.25
ATT_SCALE = (QK_NOPE + QK_ROPE) ** -0.5
NEG_BIG = -1e30

ADAM_LR, ADAM_B1, ADAM_B2, ADAM_EPS, ADAM_WD, ADAM_STEP = 0.001, 0.9, 0.999, 1e-08, 0.01, 10

VMEM_LIMIT_BYTES = 56 * 1024 * 1024
LANES = 128
FLAT_COLS = 1024
SHARD_ROWS = (("a_w_in", 1536), ("a_w_out", 512), ("b_w_in", 704), ("b_w_uq", 576), ("b_w_out", 512),
              ("kv_w_down", 144), ("kv_w_up", 512))
FLAT_ROWS = 4608
N_CHIPS = 4
N_DEV = 8


def _params(sem, vmem=VMEM_LIMIT_BYTES):
    return pltpu.CompilerParams(dimension_semantics=sem, vmem_limit_bytes=vmem)


def _row_spec(ts, w, col_block=0):
    return pl.BlockSpec((ts, w), lambda i: (i, col_block))


def _bc_spec(shape):
    nd = len(shape)
    return pl.BlockSpec(shape, lambda i: (0,) * nd)


def _sigmoid(x):
    return 1.0 / (1.0 + jnp.exp(-x))


def _fold8(v):
    ts, w = v.shape
    return jnp.sum(v.reshape(ts // 8, 8, w), axis=0)


def _mm(a, b, *, ta=False, tb=False, out_dtype=F32, tm=1024, tn=512, tk=None, name, extras=(), epilogue=None, out_tn=None,
        side=None):
    if ta:
        K, M = a.shape
    else:
        M, K = a.shape
    if tb:
        N, Kb = b.shape
    else:
        Kb, N = b.shape
    assert K == Kb, (a.shape, b.shape)
    tm, tn = min(tm, M), min(tn, N)
    tk = K if tk is None else min(tk, K)
    assert M % tm == 0 and N % tn == 0 and K % tk == 0, (name, M, N, K, tm, tn, tk)
    grid = (M // tm, N // tn, K // tk)
    nk = grid[2]
    out_tn = tn if out_tn is None else out_tn
    n_extra = len(extras)
    side_inputs = [] if side is None else list(side["inputs"])
    n_side = len(side_inputs)
    n_acc = 0 if nk == 1 else 1
    dims = (((0,) if ta else (1,), (1,) if tb else (0,)), ((), ()))

    def body(a_ref, b_ref, *rest):
        extra_refs, rest = rest[:n_extra], rest[n_extra:]
        side_in, o_ref, rest = rest[:n_side], rest[n_side], rest[n_side + 1:]
        if side is not None:
            side_refs, rest = side_in + rest[:1] + rest[1 + n_acc:], rest[1:]
            ids = [pl.program_id(d) for d in range(3)]

            @pl.when((ids[0] == 0) & (ids[1] == 0) & (ids[2] == 0))
            def _():
                side["start"](*side_refs)

        prod = lax.dot_general(a_ref[...].astype(BF16), b_ref[...].astype(BF16), dims,
                               preferred_element_type=F32)

        def store(tile):
            if epilogue is None:
                o_ref[...] = tile.astype(o_ref.dtype)
            else:
                epilogue(tile, o_ref, *extra_refs)

        if nk == 1:
            store(prod)
        else:
            acc = rest[0]
            k = pl.program_id(2)

            @pl.when(k == 0)
            def _():
                acc[...] = prod

            @pl.when(k > 0)
            def _():
                acc[...] += prod

            @pl.when(k == nk - 1)
            def _():
                store(acc[...])

        if side is not None:
            @pl.when((ids[0] == grid[0] - 1) & (ids[1] == grid[1] - 1) & (ids[2] == grid[2] - 1))
            def _():
                side["finish"](*side_refs)

    a_spec = pl.BlockSpec((tk, tm), lambda i, j, k: (k, i)) if ta else pl.BlockSpec((tm, tk), lambda i, j, k: (i, k))
    b_spec = pl.BlockSpec((tn, tk), lambda i, j, k: (j, k)) if tb else pl.BlockSpec((tk, tn), lambda i, j, k: (k, j))
    extra_specs = [pl.BlockSpec((tm, e.shape[1]), lambda i, j, k: (i, 0)) for e in extras]
    out_specs = pl.BlockSpec((tm, out_tn), lambda i, j, k: (i, j))
    out_shape = jax.ShapeDtypeStruct((M, (N // tn) * out_tn), out_dtype)
    scratch = [] if nk == 1 else [pltpu.VMEM((tm, tn), F32)]
    if side is not None:
        out_specs, out_shape, scratch = [out_specs, HBM_SPEC], [out_shape, side["out_shape"]], scratch + list(side["scratch"])
    return pl.pallas_call(
        body, name=name, grid=grid,
        in_specs=[a_spec, b_spec] + extra_specs + [HBM_SPEC] * n_side,
        out_specs=out_specs, out_shape=out_shape, scratch_shapes=scratch,
        compiler_params=_params(("parallel", "parallel", "arbitrary") if side is None else ("arbitrary",) * 3),
    )(a, b, *extras, *side_inputs)


def _rope_tables_a(s):
    half = DK_A // 2
    inv = ROPE_BASE ** (-jnp.arange(half, dtype=F32) / half)
    ang = jnp.arange(s, dtype=F32)[:, None] * inv[None, :]
    return jnp.cos(ang), jnp.sin(ang)


def _rope_tables_b(s):
    half = QK_ROPE // 2
    inv = ROPE_BASE ** (-jnp.arange(half, dtype=F32) / half)
    ang = jnp.arange(s, dtype=F32)[:, None] * inv[None, :]
    c, sn = jnp.cos(ang), jnp.sin(ang)
    z = jnp.zeros_like(c)
    cos = jnp.concatenate([c, c, z, z], axis=1)
    sa = jnp.concatenate([-sn, z, z, z], axis=1)
    sb = jnp.concatenate([z, sn, z, z], axis=1)
    return cos, sa, sb


def _rope_b(r, cos, sa, sb, sign):
    return r * cos + sign * (pltpu.roll(r, 96, 1) * sa + pltpu.roll(r, 32, 1) * sb)


def _retention_tables():
    lg = jnp.log1p(-jnp.exp2(-5.0 - jnp.arange(H_A, dtype=F32)))
    idx = jnp.arange(CHUNK, dtype=F32)
    diff = idx[:, None] - idx[None, :]
    causal = diff >= 0
    dmat = jnp.where(causal, jnp.exp(jnp.where(causal, diff, 0.0)[None] * lg[:, None, None]), 0.0)
    qdec = jnp.exp((idx + 1.0)[None, :] * lg[:, None])[:, :, None]
    kdec = jnp.exp((CHUNK - 1.0 - idx)[None, :] * lg[:, None])[:, :, None]
    cdec = jnp.broadcast_to(jnp.exp(CHUNK * lg)[:, None, None], (H_A, 1, DV_A))
    return dmat, qdec, kdec, cdec


def _group_norm(o):
    mu = jnp.mean(o, axis=-1, keepdims=True)
    oc = o - mu
    var = jnp.mean(oc * oc, axis=-1, keepdims=True)
    rstd = lax.rsqrt(var + 1e-5)
    return oc * rstd, rstd


Q_COL, K_COL, V_COL, GATE_COL = 0, H_A * DK_A, 2 * H_A * DK_A, 2 * H_A * DK_A + WIDTH_A


def _ln_stats(z):
    mu = jnp.mean(z, axis=-1, keepdims=True)
    zc = z - mu
    var = jnp.mean(zc * zc, axis=-1, keepdims=True)
    rstd = lax.rsqrt(var + 1e-5)
    return zc * rstd, rstd


def _ln_bwd(dy, xhat, rstd, g):
    dxh = dy * g
    m1 = jnp.mean(dxh, axis=-1, keepdims=True)
    m2 = jnp.mean(dxh * xhat, axis=-1, keepdims=True)
    return rstd * (dxh - m1 - xhat * m2)


def _ln_fwd(x, y, g, b, *, ts):
    s = x.shape[0]

    def body(x_ref, y_ref, g_ref, b_ref, o_ref, o16_ref):
        xhat, _ = _ln_stats(ALPHA * x_ref[...] + y_ref[...])
        out = xhat * g_ref[...] + b_ref[...]
        o_ref[...] = out
        o16_ref[...] = out.astype(o16_ref.dtype)

    return pl.pallas_call(
        body, name="ln_fwd", grid=(s // ts,),
        in_specs=[_row_spec(ts, D_MODEL), _row_spec(ts, D_MODEL), _bc_spec((1, D_MODEL)), _bc_spec((1, D_MODEL))],
        out_specs=[_row_spec(ts, D_MODEL), _row_spec(ts, D_MODEL)],
        out_shape=[jax.ShapeDtypeStruct((s, D_MODEL), F32), jax.ShapeDtypeStruct((s, D_MODEL), BF16)],
        compiler_params=_params(("parallel",)),
    )(x, y, g, b)


def _ln_loss_bwd(x1, y, target, g, b, *, ts):
    s = x1.shape[0]
    n = s // ts

    def body(x_ref, y_ref, t_ref, g_ref, b_ref, dz_ref, dz16_ref, dg_ref, db_ref, loss_ref, ag, ab, al):
        i = pl.program_id(0)

        @pl.when(i == 0)
        def _():
            ag[...] = jnp.zeros_like(ag)
            ab[...] = jnp.zeros_like(ab)
            al[...] = jnp.zeros_like(al)

        xhat, rstd = _ln_stats(ALPHA * x_ref[...] + y_ref[...])
        err = xhat * g_ref[...] + b_ref[...] - t_ref[...]
        al[...] += _fold8(err * err)
        dy = err * (1.0 / D_MODEL)
        ag[...] += _fold8(dy * xhat)
        ab[...] += _fold8(dy)
        dz = _ln_bwd(dy, xhat, rstd, g_ref[...])
        dz_ref[...] = dz
        dz16_ref[...] = dz.astype(dz16_ref.dtype)

        @pl.when(i == n - 1)
        def _():
            dg_ref[...] = jnp.sum(ag[...], axis=0, keepdims=True)
            db_ref[...] = jnp.sum(ab[...], axis=0, keepdims=True)
            loss_ref[...] = jnp.full((1, LANES), (0.5 / D_MODEL) * jnp.sum(al[...]), F32)

    return pl.pallas_call(
        body, name="ln_loss_bwd", grid=(n,),
        in_specs=[_row_spec(ts, D_MODEL)] * 3 + [_bc_spec((1, D_MODEL))] * 2,
        out_specs=[_row_spec(ts, D_MODEL), _row_spec(ts, D_MODEL), _bc_spec((1, D_MODEL)), _bc_spec((1, D_MODEL)),
                   _bc_spec((1, LANES))],
        out_shape=[jax.ShapeDtypeStruct((s, D_MODEL), F32), jax.ShapeDtypeStruct((s, D_MODEL), BF16),
                   jax.ShapeDtypeStruct((1, D_MODEL), F32), jax.ShapeDtypeStruct((1, D_MODEL), F32),
                   jax.ShapeDtypeStruct((1, LANES), F32)],
        scratch_shapes=[pltpu.VMEM((8, D_MODEL), F32)] * 3,
        compiler_params=_params(("arbitrary",)),
    )(x1, y, target, g, b)


def _ln_bwd_call(dy, x, y, g, *, ts):
    s = x.shape[0]
    n = s // ts

    def body(dy_ref, x_ref, y_ref, g_ref, dz_ref, dz16_ref, dg_ref, db_ref, ag, ab):
        i = pl.program_id(0)

        @pl.when(i == 0)
        def _():
            ag[...] = jnp.zeros_like(ag)
            ab[...] = jnp.zeros_like(ab)

        xhat, rstd = _ln_stats(ALPHA * x_ref[...] + y_ref[...])
        dy = dy_ref[...]
        ag[...] += _fold8(dy * xhat)
        ab[...] += _fold8(dy)
        dz = _ln_bwd(dy, xhat, rstd, g_ref[...])
        dz_ref[...] = dz
        dz16_ref[...] = dz.astype(dz16_ref.dtype)

        @pl.when(i == n - 1)
        def _():
            dg_ref[...] = jnp.sum(ag[...], axis=0, keepdims=True)
            db_ref[...] = jnp.sum(ab[...], axis=0, keepdims=True)

    return pl.pallas_call(
        body, name="ln_bwd", grid=(n,),
        in_specs=[_row_spec(ts, D_MODEL)] * 3 + [_bc_spec((1, D_MODEL))],
        out_specs=[_row_spec(ts, D_MODEL), _row_spec(ts, D_MODEL), _bc_spec((1, D_MODEL)), _bc_spec((1, D_MODEL))],
        out_shape=[jax.ShapeDtypeStruct((s, D_MODEL), F32), jax.ShapeDtypeStruct((s, D_MODEL), BF16),
                   jax.ShapeDtypeStruct((1, D_MODEL), F32), jax.ShapeDtypeStruct((1, D_MODEL), F32)],
        scratch_shapes=[pltpu.VMEM((8, D_MODEL), F32)] * 2,
        compiler_params=_params(("arbitrary",)),
    )(dy, x, y, g)


def _residual_store(tile, o_ref, dz_ref):
    o_ref[...] = ALPHA * dz_ref[...] + tile


C_LAT = slice(0, KV_LORA)
C_ROPE = slice(KV_LORA, KV_DOWN_PAD)
C_QL = slice(KV_DOWN_PAD, KV_DOWN_PAD + Q_LORA)
C_GATE = slice(KV_DOWN_PAD + Q_LORA, H1_B)


def _rms(x, eps=1e-6):
    r = lax.rsqrt(jnp.mean(x * x, axis=-1, keepdims=True) + eps)
    return x * r, r


def _rms_bwd(dy, xhat, r, g):
    dxh = dy * g
    return r * (dxh - xhat * jnp.mean(dxh * xhat, axis=-1, keepdims=True))


def _kvq_prep(h1, kv_norm, q_norm, cos, sa, sb, *, ts):
    s = h1.shape[0]

    def body(h_ref, kn_ref, qn_ref, c_ref, sa_ref, sb_ref, lat_ref, kr_ref, ql_ref):
        lat, _ = _rms(h_ref[:, C_LAT])
        lat_ref[...] = (lat * kn_ref[...]).astype(lat_ref.dtype)
        kr_ref[...] = _rope_b(h_ref[:, C_ROPE], c_ref[...], sa_ref[...], sb_ref[...], 1.0).astype(kr_ref.dtype)
        ql, _ = _rms(h_ref[:, C_QL])
        ql_ref[...] = (ql * qn_ref[...]).astype(ql_ref.dtype)

    return pl.pallas_call(
        body, name="kvq_prep", grid=(s // ts,),
        in_specs=[_row_spec(ts, H1_B), _bc_spec((1, KV_LORA)), _bc_spec((1, Q_LORA))] + [_row_spec(ts, 128)] * 3,
        out_specs=[_row_spec(ts, KV_LORA), _row_spec(ts, 128), _row_spec(ts, Q_LORA)],
        out_shape=[jax.ShapeDtypeStruct((s, KV_LORA), BF16), jax.ShapeDtypeStruct((s, 128), BF16),
                   jax.ShapeDtypeStruct((s, Q_LORA), BF16)],
        compiler_params=_params(("parallel",)),
    )(h1, kv_norm, q_norm, cos, sa, sb)


LOG2E = 1.4426950408889634
LN2 = 0.6931471805599453
Q_SCALE = ATT_SCALE * LOG2E


def _rope_q_store(tile, o_ref, c_ref, sa_ref, sb_ref):
    c, a, b = c_ref[...], sa_ref[...], sb_ref[...]
    for hd in range(tile.shape[1] // QK_PAD):
        lo = hd * QK_PAD
        o_ref[:, lo:lo + 128] = (tile[:, lo:lo + 128] * Q_SCALE).astype(o_ref.dtype)
        o_ref[:, lo + 128:lo + 256] = (_rope_b(tile[:, lo + 128:lo + 256], c, a, b, 1.0) * Q_SCALE).astype(o_ref.dtype)


def _assemble_k_store(tile, o_ref, kr_ref):
    r = kr_ref[...]
    for hd in range(tile.shape[1] // QK_NOPE):
        o_ref[:, hd * QK_PAD:hd * QK_PAD + 128] = tile[:, hd * 128:(hd + 1) * 128].astype(o_ref.dtype)
        o_ref[:, hd * QK_PAD + 128:(hd + 1) * QK_PAD] = r


def _h1_bwd(h1, dlat_k, dlat_v, dkr_heads, dqn, dg16, kv_norm, q_norm, cos, sa, sb, *, ts):
    s = h1.shape[0]
    n = s // ts

    def body(h_ref, dk_ref, dv_ref, dkr_ref, dqn_ref, dg_ref, kn_ref, qn_ref, c_ref, sa_ref, sb_ref,
             o_ref, dkn_ref, dqn_out_ref, akn, aqn):
        i = pl.program_id(0)

        @pl.when(i == 0)
        def _():
            akn[...] = jnp.zeros_like(akn)
            aqn[...] = jnp.zeros_like(aqn)

        lat, r = _rms(h_ref[:, C_LAT])
        dlat = dk_ref[...] + dv_ref[...]
        akn[...] += _fold8(dlat * lat)
        o_ref[:, C_LAT] = _rms_bwd(dlat, lat, r, kn_ref[...]).astype(o_ref.dtype)

        dkr = dkr_ref[:, 0:128]
        for hd in range(1, H_B):
            dkr = dkr + dkr_ref[:, hd * 128:(hd + 1) * 128]
        o_ref[:, C_ROPE] = _rope_b(dkr, c_ref[...], sa_ref[...], sb_ref[...], -1.0).astype(o_ref.dtype)

        ql, rq = _rms(h_ref[:, C_QL])
        dq = dqn_ref[...]
        aqn[...] += _fold8(dq * ql)
        o_ref[:, C_QL] = _rms_bwd(dq, ql, rq, qn_ref[...]).astype(o_ref.dtype)
        o_ref[:, C_GATE] = dg_ref[...]

        @pl.when(i == n - 1)
        def _():
            dkn_ref[...] = jnp.sum(akn[...], axis=0, keepdims=True)
            dqn_out_ref[...] = jnp.sum(aqn[...], axis=0, keepdims=True)

    return pl.pallas_call(
        body, name="h1_bwd", grid=(n,),
        in_specs=[_row_spec(ts, H1_B), _row_spec(ts, KV_LORA), _row_spec(ts, KV_LORA), _row_spec(ts, H_B * 128),
                  _row_spec(ts, Q_LORA), _row_spec(ts, WIDTH_B), _bc_spec((1, KV_LORA)), _bc_spec((1, Q_LORA))]
        + [_row_spec(ts, 128)] * 3,
        out_specs=[_row_spec(ts, H1_B), _bc_spec((1, KV_LORA)), _bc_spec((1, Q_LORA))],
        out_shape=[jax.ShapeDtypeStruct((s, H1_B), BF16), jax.ShapeDtypeStruct((1, KV_LORA), F32),
                   jax.ShapeDtypeStruct((1, Q_LORA), F32)],
        scratch_shapes=[pltpu.VMEM((8, KV_LORA), F32), pltpu.VMEM((8, Q_LORA), F32)],
        compiler_params=_params(("arbitrary",)),
    )(h1, dlat_k, dlat_v, dkr_heads, dqn, dg16, kv_norm, q_norm, cos, sa, sb)


def _dot(a, b, ca, cb):
    return lax.dot_general(a, b, (((ca,), (cb,)), ((), ())), preferred_element_type=F32)


CHUNKS_PER_STEP = 2


def _table_specs():
    full = lambda shape: pl.BlockSpec(shape, lambda i: (0,) * len(shape))
    return [full((H_A, CHUNK, CHUNK)), full((H_A, CHUNK, 1)), full((H_A, CHUNK, 1)), full((H_A, 1, DV_A))]


def _retention_fwd(h_a, cos, sin, tables):
    s = h_a.shape[0]
    n = s // CHUNK

    def body(h_ref, c_ref, s_ref, dm_ref, qd_ref, kd_ref, cd_ref, q_ref, k_ref, v_ref, o_ref, u_ref, st_ref, state):
        @pl.when(pl.program_id(0) == 0)
        def _():
            state[...] = jnp.zeros_like(state)

        for cb in range(CHUNKS_PER_STEP):
            rows = slice(cb * CHUNK, (cb + 1) * CHUNK)
            c, sn = c_ref[rows, :], s_ref[rows, :]
            for hd in range(H_A):
                qs, vs = slice(hd * DK_A, (hd + 1) * DK_A), slice(hd * DV_A, (hd + 1) * DV_A)
                for r_ref, base, scale in ((q_ref, Q_COL, 1.0), (k_ref, K_COL, DK_A ** -0.5)):
                    lo = base + hd * DK_A
                    x1, x2 = h_ref[rows, lo:lo + 128], h_ref[rows, lo + 128:lo + 256]
                    r_ref[rows, hd * DK_A:hd * DK_A + 128] = ((x1 * c - x2 * sn) * scale).astype(r_ref.dtype)
                    r_ref[rows, hd * DK_A + 128:(hd + 1) * DK_A] = ((x2 * c + x1 * sn) * scale).astype(r_ref.dtype)
                v_ref[rows, vs] = h_ref[rows, V_COL + hd * DV_A:V_COL + (hd + 1) * DV_A].astype(v_ref.dtype)
                qv, kv, vv = q_ref[rows, qs], k_ref[rows, qs], v_ref[rows, vs]
                st = state[hd]
                st16 = st.astype(BF16)
                st_ref[cb, hd] = st16
                scores = _dot(qv, kv, 1, 1) * dm_ref[hd]
                qd = (qv.astype(F32) * qd_ref[hd]).astype(BF16)
                o = _dot(scores.astype(BF16), vv, 1, 0) + _dot(qd, st16, 1, 0)
                o_ref[rows, vs] = o
                kd = (kv.astype(F32) * kd_ref[hd]).astype(BF16)
                state[hd] = st * cd_ref[hd] + _dot(kd, vv, 0, 0)
                on, _ = _group_norm(o)
                g = h_ref[rows, GATE_COL + hd * DV_A:GATE_COL + (hd + 1) * DV_A]
                u_ref[rows, vs] = (on * (g * _sigmoid(g))).astype(u_ref.dtype)

    row = lambda w: pl.BlockSpec((CHUNKS_PER_STEP * CHUNK, w), lambda i: (i, 0))
    return pl.pallas_call(
        body, name="retention_fwd", grid=(n // CHUNKS_PER_STEP,),
        in_specs=[row(IN_A), row(128), row(128)] + _table_specs(),
        out_specs=[row(H_A * DK_A), row(H_A * DK_A), row(WIDTH_A), row(WIDTH_A), row(WIDTH_A),
                   pl.BlockSpec((CHUNKS_PER_STEP, H_A, DK_A, DV_A), lambda i: (i, 0, 0, 0))],
        out_shape=[jax.ShapeDtypeStruct((s, H_A * DK_A), BF16), jax.ShapeDtypeStruct((s, H_A * DK_A), BF16),
                   jax.ShapeDtypeStruct((s, WIDTH_A), BF16), jax.ShapeDtypeStruct((s, WIDTH_A), F32),
                   jax.ShapeDtypeStruct((s, WIDTH_A), BF16), jax.ShapeDtypeStruct((n, H_A, DK_A, DV_A), BF16)],
        scratch_shapes=[pltpu.VMEM((H_A, DK_A, DV_A), F32)],
        compiler_params=_params(("arbitrary",)),
    )(h_a, cos, sin, *tables)


def _retention_bwd(q, k, v, states, du, o, h_a, cos, sin, tables):
    s = q.shape[0]
    n = s // CHUNK

    def body(q_ref, k_ref, v_ref, st_ref, du_ref, o_ref, g_ref, c_ref, s_ref, dm_ref, qd_ref, kd_ref, cd_ref, dh_ref, grad_state):
        @pl.when(pl.program_id(0) == 0)
        def _():
            grad_state[...] = jnp.zeros_like(grad_state)

        for cb in reversed(range(CHUNKS_PER_STEP)):
            rows = slice(cb * CHUNK, (cb + 1) * CHUNK)
            c, sn = c_ref[rows, :], s_ref[rows, :]
            for hd in range(H_A):
                qs, vs = slice(hd * DK_A, (hd + 1) * DK_A), slice(hd * DV_A, (hd + 1) * DV_A)
                on, rstd = _group_norm(o_ref[rows, vs])
                g = g_ref[rows, vs]
                sg = _sigmoid(g)
                du_v = du_ref[rows, vs]
                don = du_v * (g * sg)
                dh_ref[rows, GATE_COL + hd * DV_A:GATE_COL + (hd + 1) * DV_A] = (du_v * on * (sg * (1.0 + g * (1.0 - sg)))).astype(dh_ref.dtype)
                m1 = jnp.mean(don, axis=-1, keepdims=True)
                m2 = jnp.mean(don * on, axis=-1, keepdims=True)
                dov = (rstd * (don - m1 - on * m2)).astype(BF16)

                qv, kv, vv = q_ref[rows, qs], k_ref[rows, qs], v_ref[rows, vs]
                dm = dm_ref[hd]
                gs = grad_state[hd]
                g16 = gs.astype(BF16)
                scores = (_dot(qv, kv, 1, 1) * dm).astype(BF16)
                dscores = (_dot(dov, vv, 1, 1) * dm).astype(BF16)
                qd = (qv.astype(F32) * qd_ref[hd]).astype(BF16)
                kd = (kv.astype(F32) * kd_ref[hd]).astype(BF16)
                dq = _dot(dscores, kv, 1, 0) + _dot(dov, st_ref[cb, hd], 1, 1) * qd_ref[hd]
                dk = (_dot(dscores, qv, 0, 0) + _dot(vv, g16, 1, 1) * kd_ref[hd]) * (DK_A ** -0.5)
                dh_ref[rows, V_COL + hd * DV_A:V_COL + (hd + 1) * DV_A] = (_dot(scores, dov, 0, 0) + _dot(kd, g16, 1, 0)).astype(dh_ref.dtype)
                grad_state[hd] = gs * cd_ref[hd] + _dot(qd, dov, 0, 0)
                for d, base in ((dq, Q_COL), (dk, K_COL)):
                    lo = base + hd * DK_A
                    d1, d2 = d[:, 0:128], d[:, 128:256]
                    dh_ref[rows, lo:lo + 128] = (d1 * c + d2 * sn).astype(dh_ref.dtype)
                    dh_ref[rows, lo + 128:lo + 256] = (d2 * c - d1 * sn).astype(dh_ref.dtype)

    steps = n // CHUNKS_PER_STEP
    rev = lambda i: steps - 1 - i
    row = lambda w, col=0: pl.BlockSpec((CHUNKS_PER_STEP * CHUNK, w), lambda i: (rev(i), col))
    return pl.pallas_call(
        body, name="retention_bwd", grid=(steps,),
        in_specs=[row(H_A * DK_A), row(H_A * DK_A), row(WIDTH_A),
                  pl.BlockSpec((CHUNKS_PER_STEP, H_A, DK_A, DV_A), lambda i: (rev(i), 0, 0, 0)),
                  row(WIDTH_A), row(WIDTH_A), row(WIDTH_A, GATE_COL // WIDTH_A), row(128), row(128)] + _table_specs(),
        out_specs=row(IN_A),
        out_shape=jax.ShapeDtypeStruct((s, IN_A), BF16),
        scratch_shapes=[pltpu.VMEM((H_A, DK_A, DV_A), F32)],
        compiler_params=_params(("arbitrary",)),
    )(q, k, v, states, du, o, h_a, cos, sin, *tables)


GATE_BLOCK0 = (KV_DOWN_PAD + Q_LORA) // LANES


def _causal_mask(sc, row0):
    row = lax.broadcasted_iota(jnp.int32, sc.shape, 0) + row0
    col = lax.broadcasted_iota(jnp.int32, sc.shape, 1)
    return jnp.where(col <= row, sc, NEG_BIG)


def _key_block_loop(step, n, per_trip, smallest=1):
    def trip_body(jj, carry):
        for t in range(per_trip):
            step(per_trip * jj + t)
        return carry

    lax.fori_loop(0, n // per_trip, trip_body, 0)
    group = per_trip // 2
    while group >= smallest:
        def tail(group=group):
            first = (n // (2 * group)) * (2 * group)
            for t in range(group):
                step(first + t)

        pl.when((n // group) % 2 == 1)(tail)
        group //= 2


def _attention_fwd(q, k, v, h1, *, blk, bk, sub, per_trip):
    s = q.shape[0]
    nb = s // blk

    def body(q_ref, k_ref, v_ref, g_ref, o_ref, u_ref, lse_ref, vext_s, m_s, acc_s):
        i = pl.program_id(1)

        @pl.when(i == 0)
        def _():
            vext_s[:, 0:V_HEAD] = v_ref[...]
            vext_s[:, V_HEAD:2 * V_HEAD] = jnp.ones((s, V_HEAD), vext_s.dtype)

        m_s[...] = jnp.full_like(m_s, NEG_BIG)
        acc_s[...] = jnp.zeros_like(acc_s)

        def update(rows, kb, vb, row0):
            sc = _dot(q_ref[rows, :], kb, 1, 1)
            if row0 is not None:
                sc = _causal_mask(sc, row0)
            m_prev = m_s[rows, :]
            m_new = jnp.maximum(m_prev, jnp.max(sc, axis=-1, keepdims=True))
            p = jnp.exp2(sc - jnp.tile(m_new, (1, kb.shape[0] // LANES)))
            a = jnp.exp2(m_prev - m_new)
            acc_s[rows, :] = jnp.tile(a, (1, 2)) * acc_s[rows, :] + _dot(p.astype(BF16), vb, 1, 0)
            m_s[rows, :] = m_new

        def step(j):
            kv_rows = pl.ds(pl.multiple_of(j * bk, bk), bk)
            kb, vb = k_ref[kv_rows, :], vext_s[kv_rows, :]
            for r in range(blk // sub):
                update(slice(r * sub, (r + 1) * sub), kb, vb, None)

        _key_block_loop(step, i * (blk // bk), per_trip, smallest=blk // bk)
        for r in range(blk // sub):
            ncols = (r + 1) * sub
            kv_rows = pl.ds(pl.multiple_of(i * blk, blk), ncols)
            update(slice(r * sub, (r + 1) * sub), k_ref[kv_rows, :], vext_s[kv_rows, :], r * sub)
        acc = acc_s[...]
        l = acc[:, V_HEAD:2 * V_HEAD]
        o = acc[:, 0:V_HEAD] / l
        g = g_ref[...]
        o_ref[...] = o
        u_ref[...] = (o * (g * _sigmoid(g))).astype(u_ref.dtype)
        lse_ref[0] = (m_s[...] + jnp.log2(l))[:, 0:1]

    blk_idx = lambda h, i: (i, h)
    return pl.pallas_call(
        body, name="attention_fwd", grid=(H_B, nb),
        in_specs=[pl.BlockSpec((blk, QK_PAD), blk_idx), pl.BlockSpec((s, QK_PAD), lambda h, i: (0, h)),
                  pl.BlockSpec((s, V_HEAD), lambda h, i: (0, h)), pl.BlockSpec((blk, LANES), lambda h, i: (i, GATE_BLOCK0 + h))],
        out_specs=[pl.BlockSpec((blk, V_HEAD), blk_idx), pl.BlockSpec((blk, V_HEAD), blk_idx),
                   pl.BlockSpec((1, blk, 1), lambda h, i: (h, i, 0))],
        out_shape=[jax.ShapeDtypeStruct((s, WIDTH_B), F32), jax.ShapeDtypeStruct((s, WIDTH_B), BF16),
                   jax.ShapeDtypeStruct((H_B, s, 1), F32)],
        scratch_shapes=[pltpu.VMEM((s, 2 * V_HEAD), BF16), pltpu.VMEM((blk, LANES), F32), pltpu.VMEM((blk, 2 * V_HEAD), F32)],
        compiler_params=_params(("parallel", "arbitrary")),
    )(q, k, v, h1)


def _attention_bwd(q, k, v, du, o, h1, lse, cos, sa, sb, *, blk, bk, per_trip):
    s = q.shape[0]
    nb = s // blk

    def body(q_ref, k_ref, v_ref, du_ref, o_ref, g_ref, lse_ref, c_ref, sa_ref, sb_ref,
             dq_ref, dkn_ref, dkr_ref, dv_ref, dg_ref, lse_s, dl_s, do_s, dq_s, dkn_s, dv_s):
        i = pl.program_id(1)
        g = g_ref[...]
        sg = _sigmoid(g)
        du_v, ov = du_ref[...], o_ref[...]
        do = du_v * (g * sg)
        dg_ref[...] = (du_v * ov * (sg * (1.0 + g * (1.0 - sg)))).astype(dg_ref.dtype)
        do_s[...] = do.astype(do_s.dtype)
        dl_s[...] = jnp.broadcast_to(jnp.sum(do * ov, axis=-1, keepdims=True), (blk, LANES))
        lse_s[...] = jnp.broadcast_to(lse_ref[0], (blk, LANES))
        dq_s[...] = jnp.zeros_like(dq_s)

        def products(rows, kv_rows, row0):
            qv, dov, kb = q_ref[rows, :], do_s[rows, :], k_ref[kv_rows, :]
            tile = (1, kb.shape[0] // LANES)
            sc = _dot(qv, kb, 1, 1)
            if row0 is not None:
                sc = _causal_mask(sc, row0)
            p = jnp.exp2(sc - jnp.tile(lse_s[rows, :], tile))
            dp = _dot(dov, v_ref[kv_rows, :], 1, 1)
            ds = (p * (dp - jnp.tile(dl_s[rows, :], tile))).astype(BF16)
            dq_s[rows, :] += _dot(ds, kb, 1, 0)
            return _dot(ds, qv, 0, 0), _dot(p.astype(BF16), dov, 0, 0)

        def put(kv_rows, dk_c, dv_c, first):
            if first:
                dkn_s[kv_rows, :] = dk_c[:, 0:128]
                dkr_ref[kv_rows, :] = dk_c[:, 128:256]
                dv_s[kv_rows, :] = dv_c
            else:
                dkn_s[kv_rows, :] += dk_c[:, 0:128]
                dkr_ref[kv_rows, :] += dk_c[:, 128:256]
                dv_s[kv_rows, :] += dv_c

        n_sub = blk // bk
        sub_rows = [slice(r * bk, (r + 1) * bk) for r in range(n_sub)]

        def step(j):
            kv_rows = pl.ds(pl.multiple_of(j * bk, bk), bk)
            for rows in sub_rows:
                dk_c, dv_c = products(rows, kv_rows, None)
                put(kv_rows, dk_c, dv_c, False)

        _key_block_loop(step, i * n_sub, per_trip, smallest=n_sub)
        for c in range(n_sub):
            kv_rows = pl.ds(pl.multiple_of(i * blk + c * bk, bk), bk)
            for r in range(c, n_sub):
                dk_c, dv_c = products(sub_rows[r], kv_rows, 0 if r == c else None)
                put(kv_rows, dk_c, dv_c, r == c)
        dq = dq_s[...] * ATT_SCALE
        dq_ref[:, 0:128] = dq[:, 0:128].astype(dq_ref.dtype)
        dq_ref[:, 128:256] = _rope_b(dq[:, 128:256], c_ref[...], sa_ref[...], sb_ref[...], -1.0).astype(dq_ref.dtype)

        @pl.when(i == nb - 1)
        def _():
            dkn_ref[...] = (dkn_s[...] * LN2).astype(dkn_ref.dtype)
            dv_ref[...] = dv_s[...].astype(dv_ref.dtype)
            dkr_ref[...] = dkr_ref[...] * LN2

    head = lambda h, i: (0, h)
    blk_idx = lambda h, i: (i, h)
    row_idx = lambda h, i: (i, 0)
    return pl.pallas_call(
        body, name="attention_bwd", grid=(H_B, nb),
        in_specs=[pl.BlockSpec((blk, QK_PAD), blk_idx), pl.BlockSpec((s, QK_PAD), head), pl.BlockSpec((s, V_HEAD), head),
                  pl.BlockSpec((blk, V_HEAD), blk_idx), pl.BlockSpec((blk, V_HEAD), blk_idx),
                  pl.BlockSpec((blk, LANES), lambda h, i: (i, GATE_BLOCK0 + h)), pl.BlockSpec((1, blk, 1), lambda h, i: (h, i, 0)),
                  pl.BlockSpec((blk, LANES), row_idx), pl.BlockSpec((blk, LANES), row_idx), pl.BlockSpec((blk, LANES), row_idx)],
        out_specs=[pl.BlockSpec((blk, QK_PAD), blk_idx), pl.BlockSpec((s, 128), head), pl.BlockSpec((s, 128), head),
                   pl.BlockSpec((s, 128), head), pl.BlockSpec((blk, V_HEAD), blk_idx)],
        out_shape=[jax.ShapeDtypeStruct((s, H_B * QK_PAD), BF16), jax.ShapeDtypeStruct((s, H_B * 128), BF16),
                   jax.ShapeDtypeStruct((s, H_B * 128), F32), jax.ShapeDtypeStruct((s, H_B * 128), BF16),
                   jax.ShapeDtypeStruct((s, WIDTH_B), BF16)],
        scratch_shapes=[pltpu.VMEM((blk, LANES), F32), pltpu.VMEM((blk, LANES), F32), pltpu.VMEM((blk, V_HEAD), BF16),
                        pltpu.VMEM((blk, QK_PAD), F32), pltpu.VMEM((s, 128), F32), pltpu.VMEM((s, V_HEAD), F32)],
        compiler_params=_params(("parallel", "arbitrary")),
    )(q, k, v, du, o, h1, lse, cos, sa, sb)


def _local_step(x, target, w, kv_norm, q_norm, ln_g, ln_b, *, ts=512, blk=512, late=None, reduce=None):
    s = x.shape[0]
    cos_a, sin_a = _rope_tables_a(s)
    cos_b, sa_b, sb_b = _rope_tables_b(s)
    tables = _retention_tables()
    g0, g1, b0, b1 = ln_g[0:1], ln_g[1:2], ln_b[0:1], ln_b[1:2]

    x16 = x.astype(BF16)
    if late is None:
        h_a = _mm(x16, w["a_in"], tn=1536, name="a_in_fwd")
    else:
        flat_b, chip = late
        h_a, got = _mm(x16, w["a_in"], tn=1536, name="a_in_fwd", side=_gather_side(flat_b))
        got = lax.dynamic_update_slice(got, flat_b[None], (chip, 0, 0))
        w = {**w, **_kernel_layout_b(_full_from_gathered(got, B_SHARDS))}
    q_a, k_a, v_a, o_a, u_a, states = _retention_fwd(h_a, cos_a, sin_a, tables)
    y_a = _mm(u_a, w["a_out"], tn=1024, name="a_out_fwd")
    x1, x1_16 = _ln_fwd(x, y_a, g0, b0, ts=2 * ts)

    h1 = _mm(x1_16, w["b_in1"], tn=1152, name="b_in_fwd")
    lat16, kr16, qn16 = _kvq_prep(h1, kv_norm, q_norm, cos_b, sa_b, sb_b, ts=ts)
    k16 = _mm(lat16, w["up_k"], out_dtype=BF16, tn=2048, out_tn=H_B * QK_PAD, extras=(kr16,), epilogue=_assemble_k_store, name="up_k_fwd")
    v16 = _mm(lat16, w["up_v"], out_dtype=BF16, tn=2048, name="up_v_fwd")
    q16 = _mm(qn16, w["uq"], out_dtype=BF16, tn=2048, extras=(cos_b, sa_b, sb_b), epilogue=_rope_q_store, name="uq_fwd")
    o_b, u_b, lse = _attention_fwd(q16, k16, v16, h1, blk=2 * blk, bk=blk, sub=blk // 2, per_trip=4)
    y_b = _mm(u_b, w["b_out"], tn=1024, name="b_out_fwd")

    dz_b, dz_b16, dg1, db1, loss = _ln_loss_bwd(x1, y_b, target, g1, b1, ts=2 * ts)
    d_b_out = _mm(u_b, dz_b16, ta=True, tn=1024, tk=2048, name="b_out_dw")
    du_b = _mm(dz_b16, w["b_out"], tb=True, tn=2048, name="b_out_dx")
    dqf16, dkn, dkr_heads, dv, dgate16 = _attention_bwd(q16, k16, v16, du_b, o_b, h1, lse, cos_b, sa_b, sb_b, blk=2 * blk, bk=blk, per_trip=4)
    d_uq = _mm(qn16, dqf16, ta=True, tm=768, tn=2048, tk=2048, name="uq_dw")
    dqn = _mm(dqf16, w["uq"], tb=True, tn=768, name="uq_dx")
    d_up_k = _mm(lat16, dkn, ta=True, tn=2048, tk=2048, name="up_k_dw")
    d_up_v = _mm(lat16, dv, ta=True, tn=2048, tk=2048, name="up_v_dw")
    dlat_k = _mm(dkn, w["up_k"], tb=True, tn=512, name="up_k_dx")
    dlat_v = _mm(dv, w["up_v"], tb=True, tn=512, name="up_v_dx")
    dh1, dkvn, dqnorm = _h1_bwd(h1, dlat_k, dlat_v, dkr_heads, dqn, dgate16, kv_norm, q_norm, cos_b, sa_b, sb_b, ts=ts)
    d_b_in1 = _mm(x1_16, dh1, ta=True, tn=1152, tk=2048, name="b_in_dw")
    dx1 = _mm(dh1, w["b_in1"], tb=True, tn=1024, extras=(dz_b,), epilogue=_residual_store, name="b_in_dx")

    dz_a, dz_a16, dg0, db0 = _ln_bwd_call(dx1, x, y_a, g0, ts=2 * ts)
    d_a_out = _mm(u_a, dz_a16, ta=True, tn=1024, tk=2048, name="a_out_dw")
    du_a = _mm(dz_a16, w["a_out"], tb=True, tn=2048, name="a_out_dx")
    dh_a = _retention_bwd(q_a, k_a, v_a, states, du_a, o_a, h_a, cos_a, sin_a, tables)
    grads = dict(a_out=d_a_out, b_in1=d_b_in1, uq=d_uq, b_out=d_b_out, up_k=d_up_k, up_v=d_up_v)
    small = dict(ln_g=jnp.concatenate([dg0, dg1], axis=0), ln_b=jnp.concatenate([db0, db1], axis=0),
                 q_norm=dqnorm, kv_norm=dkvn)
    if reduce is None:
        grads["a_in"] = _mm(x16, dh_a, ta=True, tn=1536, tk=1024, name="a_in_dw")
        grad_x = _mm(dh_a, w["a_in"], tb=True, tn=1024, tk=2048, extras=(dz_a,), epilogue=_residual_store, name="a_in_dx")
        return loss, grad_x, grads, small
    own_early, travel_early = reduce(_reference_layout_grads(grads), EARLY_SHARDS, EARLY_ROWS, "early")
    d_a_in, got_early = _mm(x16, dh_a, ta=True, tn=1536, tk=2048, name="a_in_dw", side=_chip_exchange_side(travel_early))
    own_late, travel_late = reduce(dict(a_w_in=d_a_in), LATE_SHARDS, LATE_ROWS, "late")
    grad_x, got_late = _mm(dh_a, w["a_in"], tb=True, tn=1024, tk=3072, extras=(dz_a,), epilogue=_residual_store, name="a_in_dx",
                           side=_chip_exchange_side(travel_late))
    return loss, grad_x, ((own_early, got_early), (own_late, got_late)), small


def _flat_shards(shards, dtype, which, total_rows):
    parts = [shards[name].reshape(rows, FLAT_COLS) for name, rows in which]
    used = sum(rows for _, rows in which)
    parts.append(jnp.zeros((total_rows - used, FLAT_COLS), parts[0].dtype))
    return jnp.concatenate(parts, axis=0).astype(dtype)


def _unflat_shards(flat, shapes, shards):
    out, off = {}, 0
    for name, rows in shards:
        out[name] = flat[off:off + rows].reshape(shapes[name])
        off += rows
    return out


COL_SHARDED = {"a_w_in": (D_MODEL, IN_A), "b_w_in": (D_MODEL, IN_B), "b_w_uq": (Q_LORA, H_B * (QK_NOPE + QK_ROPE)),
               "kv_w_up": (KV_LORA, H_B * (QK_NOPE + V_HEAD))}
ROW_SHARDED = {"a_w_out": (WIDTH_A, D_MODEL), "b_w_out": (WIDTH_B, D_MODEL), "kv_w_down": (D_MODEL, KV_LORA + QK_ROPE)}


def _full_from_gathered(gathered, shards=SHARD_ROWS):
    out, off = {}, 0
    for name, rows in shards:
        part = gathered[:, off:off + rows]
        off += rows
        if name in COL_SHARDED:
            r, c = COL_SHARDED[name]
            out[name] = part.reshape(N_CHIPS, r, c // N_CHIPS).transpose(1, 0, 2).reshape(r, c)
        else:
            r, c = ROW_SHARDED[name]
            out[name] = part.reshape(r, c)
    return out


def _chip_major(g):
    r, c = g.shape
    return g.reshape(r, N_CHIPS, c // N_CHIPS).transpose(1, 0, 2)


def _gathered_from_full(full, shards, total_rows):
    if len(shards) == 1 and shards[0][0] in COL_SHARDED:
        return _chip_major(full[shards[0][0]])
    parts = []
    for name, rows in shards:
        g = full[name]
        if name in COL_SHARDED:
            r, c = COL_SHARDED[name]
            g = g.reshape(r, N_CHIPS, c // N_CHIPS).transpose(1, 0, 2)
        parts.append(g.reshape(N_CHIPS, rows, FLAT_COLS))
    used = sum(rows for _, rows in shards)
    if total_rows > used:
        parts.append(jnp.zeros((N_CHIPS, total_rows - used, FLAT_COLS), F32))
    return jnp.concatenate(parts, axis=1)


A_SHARDS, B_SHARDS = SHARD_ROWS[:1], SHARD_ROWS[1:]
A_ROWS = sum(rows for _, rows in A_SHARDS)


def _kernel_layout_a(full):
    return dict(a_in=full["a_w_in"])


def _kernel_layout_b(full):
    uq = full["b_w_uq"].reshape(Q_LORA, H_B, QK_NOPE + QK_ROPE)
    uq = jnp.pad(uq, ((0, 0), (0, 0), (0, QK_PAD - QK_NOPE - QK_ROPE))).reshape(Q_LORA, H_B * QK_PAD)
    up = full["kv_w_up"].reshape(KV_LORA, H_B, QK_NOPE + V_HEAD)
    down = jnp.pad(full["kv_w_down"], ((0, 0), (0, KV_DOWN_PAD - KV_LORA - QK_ROPE)))
    return dict(a_out=full["a_w_out"], b_out=full["b_w_out"], uq=uq,
                up_k=up[:, :, :QK_NOPE].reshape(KV_LORA, H_B * QK_NOPE),
                up_v=up[:, :, QK_NOPE:].reshape(KV_LORA, H_B * V_HEAD),
                b_in1=jnp.concatenate([down, full["b_w_in"]], axis=1))


def _kernel_layout(full):
    return {**_kernel_layout_a(full), **_kernel_layout_b(full)}


EARLY_SHARDS = tuple(sh for sh in SHARD_ROWS if sh[0] != "a_w_in")
LATE_SHARDS = tuple(sh for sh in SHARD_ROWS if sh[0] == "a_w_in")
EARLY_ROWS, LATE_ROWS = 3072, 1536


def _reference_layout_grads(g):
    uq = g["uq"].reshape(Q_LORA, H_B, QK_PAD)[:, :, :QK_NOPE + QK_ROPE].reshape(Q_LORA, H_B * (QK_NOPE + QK_ROPE))
    up = jnp.concatenate([g["up_k"].reshape(KV_LORA, H_B, QK_NOPE), g["up_v"].reshape(KV_LORA, H_B, V_HEAD)], axis=2)
    return dict(a_w_out=g["a_out"], b_w_out=g["b_out"], b_w_uq=uq,
                kv_w_up=up.reshape(KV_LORA, H_B * (QK_NOPE + V_HEAD)),
                kv_w_down=g["b_in1"][:, :KV_LORA + QK_ROPE], b_w_in=g["b_in1"][:, KV_DOWN_PAD:])


HBM_SPEC = pl.BlockSpec(memory_space=pl.ANY)


def _me():
    return lax.axis_index("x"), lax.axis_index("y"), lax.axis_index("c")


def _chip_flips(x, y):
    return [(1 - x, y), (x, 1 - y), (1 - x, 1 - y)]


def _gather_copies(src_ref, out_ref, send_sems, recv_sems):
    x, y, c = _me()
    half = src_ref.shape[0] // 2
    my_rows = pl.ds(pl.multiple_of(c * half, 16), half)
    their_rows = pl.ds(pl.multiple_of((1 - c) * half, 16), half)
    chips = _chip_flips(x, y)
    sibling = (x, y, 1 - c)

    def copy(k, src, dst, to):
        return pltpu.make_async_remote_copy(src_ref=src, dst_ref=dst, send_sem=send_sems.at[k], recv_sem=recv_sems.at[k],
                                            device_id=to, device_id_type=MESH)

    sends = [copy(k, src_ref.at[my_rows, :], out_ref.at[2 * x + y, my_rows, :], (px, py, c)) for k, (px, py) in enumerate(chips)]
    landed = [out_ref.at[2 * px + py, my_rows, :] for px, py in chips]
    lands = [copy(k, landed[k], landed[k], (px, py, c)) for k, (px, py) in enumerate(chips)]
    forwards = [copy(3 + k, landed[k], landed[k], sibling) for k in range(3)]
    theirs = [out_ref.at[2 * px + py, their_rows, :] for px, py in chips]
    arrivals = [copy(3 + k, theirs[k], theirs[k], sibling) for k in range(3)]
    return sends, lands, forwards, arrivals


def _gather_start(src_ref, out_ref, send_sems, recv_sems):
    sends, _, _, _ = _gather_copies(src_ref, out_ref, send_sems, recv_sems)
    for cp in sends:
        cp.start()


def _gather_finish(src_ref, out_ref, send_sems, recv_sems):
    sends, lands, forwards, arrivals = _gather_copies(src_ref, out_ref, send_sems, recv_sems)
    for k in range(3):
        lands[k].wait_recv()
        forwards[k].start()
    for cp in arrivals:
        cp.wait_recv()
    for cp in sends + forwards:
        cp.wait_send()


def _gather_scratch():
    return [pltpu.SemaphoreType.DMA((6,)), pltpu.SemaphoreType.DMA((6,))]


def _gather_weights(flat16):
    def body(src_ref, out_ref, send_sems, recv_sems):
        _gather_start(src_ref, out_ref, send_sems, recv_sems)
        _gather_finish(src_ref, out_ref, send_sems, recv_sems)

    return pl.pallas_call(
        body, name="gather_weights",
        in_specs=[HBM_SPEC], out_specs=HBM_SPEC,
        out_shape=jax.ShapeDtypeStruct((N_CHIPS,) + flat16.shape, flat16.dtype),
        scratch_shapes=_gather_scratch(),
    )(flat16)


def _gather_side(flat16):
    return dict(inputs=[flat16], out_shape=jax.ShapeDtypeStruct((N_CHIPS,) + flat16.shape, flat16.dtype),
                scratch=_gather_scratch(), start=_gather_start, finish=_gather_finish)


def _chip_exchange_copies(p_ref, out_ref, send_sems, recv_sems):
    x, y, c = _me()
    return [pltpu.make_async_remote_copy(
        src_ref=p_ref.at[2 * px + py], dst_ref=out_ref.at[k], send_sem=send_sems.at[k], recv_sem=recv_sems.at[k],
        device_id=(px, py, c), device_id_type=MESH) for k, (px, py) in enumerate(_chip_flips(x, y))]


def _chip_exchange_start(p_ref, out_ref, send_sems, recv_sems):
    for cp in _chip_exchange_copies(p_ref, out_ref, send_sems, recv_sems):
        cp.start()


def _chip_exchange_finish(p_ref, out_ref, send_sems, recv_sems):
    copies = _chip_exchange_copies(p_ref, out_ref, send_sems, recv_sems)
    for cp in copies:
        cp.wait_send()
    for cp in copies:
        cp.wait_recv()


def _chip_exchange_side(p):
    return dict(inputs=[p], out_shape=jax.ShapeDtypeStruct((3,) + p.shape[1:], p.dtype),
                scratch=[pltpu.SemaphoreType.DMA((3,)), pltpu.SemaphoreType.DMA((3,))],
                start=_chip_exchange_start, finish=_chip_exchange_finish)


def _pair_swap(r, name):
    def body(r_ref, out_ref, send_sem, recv_sem):
        x, y, c = _me()
        cp = pltpu.make_async_remote_copy(src_ref=r_ref, dst_ref=out_ref, send_sem=send_sem, recv_sem=recv_sem,
                                          device_id=(x, y, 1 - c), device_id_type=MESH)
        cp.start()
        cp.wait_send()
        cp.wait_recv()

    return pl.pallas_call(
        body, name=name,
        in_specs=[HBM_SPEC], out_specs=HBM_SPEC,
        out_shape=jax.ShapeDtypeStruct(r.shape, r.dtype),
        scratch_shapes=[pltpu.SemaphoreType.DMA, pltpu.SemaphoreType.DMA],
    )(r)


def _sum_small(vec):
    def body(v_ref, out_ref, slots, send_sems, recv_sems):
        x, y, c = _me()
        me = 4 * x + 2 * y + c
        slots[me] = v_ref[...]
        flips = [(fx, fy, fc) for fx in (0, 1) for fy in (0, 1) for fc in (0, 1)][1:]
        copies = []
        for k, (fx, fy, fc) in enumerate(flips):
            copies.append(pltpu.make_async_remote_copy(
                src_ref=v_ref, dst_ref=slots.at[me], send_sem=send_sems.at[k], recv_sem=recv_sems.at[k],
                device_id=(x ^ fx, y ^ fy, c ^ fc), device_id_type=MESH))
        for cp in copies:
            cp.start()
        for cp in copies:
            cp.wait_send()
        for k, (fx, fy, fc) in enumerate(flips):
            src = 4 * (x ^ fx) + 2 * (y ^ fy) + (c ^ fc)
            pltpu.make_async_remote_copy(
                src_ref=v_ref, dst_ref=slots.at[src], send_sem=send_sems.at[k], recv_sem=recv_sems.at[k],
                device_id=(x ^ fx, y ^ fy, c ^ fc), device_id_type=MESH).wait_recv()
        total = slots[0]
        for d in range(1, N_DEV):
            total = total + slots[d]
        out_ref[...] = total

    return pl.pallas_call(
        body, name="sum_small",
        in_specs=[pl.BlockSpec(memory_space=pltpu.VMEM)], out_specs=pl.BlockSpec(memory_space=pltpu.VMEM),
        out_shape=jax.ShapeDtypeStruct(vec.shape, vec.dtype),
        scratch_shapes=[pltpu.VMEM((N_DEV,) + vec.shape, vec.dtype), pltpu.SemaphoreType.DMA((7,)),
                        pltpu.SemaphoreType.DMA((7,))],
    )(vec)


UPD_ROWS = 512


def _pair_sum(g, theirs, core, chip, name):
    half, cols = theirs.shape[1:]
    nb = half // UPD_ROWS

    def body(core_ref, chip_ref, g_ref, t_ref, own_ref, o16_ref):
        total = g_ref[0] + t_ref[0].astype(F32)
        o16_ref[0] = total.astype(o16_ref.dtype)

        @pl.when(pl.program_id(1) == chip_ref[0])
        def _():
            own_ref[...] = total

    return pl.pallas_call(
        body, name=name,
        grid_spec=pltpu.PrefetchScalarGridSpec(
            num_scalar_prefetch=2, grid=(nb, N_CHIPS),
            in_specs=[pl.BlockSpec((1, UPD_ROWS, cols), lambda i, d, core_ref, chip_ref: (d, core_ref[0] * nb + i, 0)),
                      pl.BlockSpec((1, UPD_ROWS, cols), lambda i, d, core_ref, chip_ref: (d, i, 0))],
            out_specs=[pl.BlockSpec((UPD_ROWS, cols), lambda i, d, core_ref, chip_ref: (i, 0)),
                       pl.BlockSpec((1, UPD_ROWS, cols), lambda i, d, core_ref, chip_ref: (d, i, 0))]),
        out_shape=[jax.ShapeDtypeStruct((half, cols), F32),
                   jax.ShapeDtypeStruct((N_CHIPS, half, cols), BF16)],
        compiler_params=_params(("parallel", "arbitrary")),
    )(core, chip, g, theirs)


def _chip_sum(own, received, name):
    half, cols = own.shape
    nb = half // UPD_ROWS

    def body(p_ref, r_ref, o_ref):
        o_ref[...] = ((p_ref[...] + r_ref[0].astype(F32)) + r_ref[1].astype(F32)) + r_ref[2].astype(F32)

    return pl.pallas_call(
        body, name=name, grid=(nb,),
        in_specs=[pl.BlockSpec((UPD_ROWS, cols), lambda i: (i, 0)),
                  pl.BlockSpec((3, UPD_ROWS, cols), lambda i: (0, i, 0))],
        out_specs=pl.BlockSpec((UPD_ROWS, cols), lambda i: (i, 0)),
        out_shape=jax.ShapeDtypeStruct((half, cols), F32),
        compiler_params=_params(("parallel",)),
    )(own, received)


def _adamw(w, g, m, v, *, rows, name):
    r, c = w.shape
    rows = min(rows, r)
    assert r % rows == 0

    def body(w_ref, g_ref, m_ref, v_ref, d_ref, nm_ref, nv_ref):
        gv = g_ref[...]
        nm = ADAM_B1 * m_ref[...] + (1.0 - ADAM_B1) * gv
        nv = ADAM_B2 * v_ref[...] + (1.0 - ADAM_B2) * (gv * gv)
        m_hat = nm / (1.0 - ADAM_B1 ** ADAM_STEP)
        v_hat = nv / (1.0 - ADAM_B2 ** ADAM_STEP)
        d_ref[...] = -ADAM_LR * (m_hat / (jnp.sqrt(v_hat) + ADAM_EPS) + ADAM_WD * w_ref[...])
        nm_ref[...] = nm
        nv_ref[...] = nv

    spec = pl.BlockSpec((rows, c), lambda i: (i, 0))
    return pl.pallas_call(
        body, name=name, grid=(r // rows,),
        in_specs=[spec] * 4, out_specs=[spec] * 3,
        out_shape=[jax.ShapeDtypeStruct((r, c), F32)] * 3,
        compiler_params=_params(("parallel",)),
    )(w, g, m, v)


W_NAMES = ("a_w_in", "a_w_out", "b_w_in", "b_q_norm", "b_w_uq", "b_w_out", "kv_w_down", "kv_norm", "kv_w_up", "ln_g", "ln_b")
BIG = tuple(name for name, _ in SHARD_ROWS)


def _pack_small(ln_g, ln_b, q_norm, kv_norm, extra=None):
    pad = lambda a: jnp.pad(a.reshape(1, -1), ((0, 0), (0, FLAT_COLS - a.size)))
    rows = [ln_g, ln_b, pad(q_norm), pad(kv_norm),
            jnp.zeros((1, FLAT_COLS), F32) if extra is None else pad(extra), jnp.zeros((1, FLAT_COLS), F32)]
    return jnp.concatenate(rows, axis=0)


def _unpack_small(p):
    return dict(ln_g=p[0:2], ln_b=p[2:4], b_q_norm=p[4:5, :Q_LORA], kv_norm=p[5, :KV_LORA])


def kernel(x, a_w_in, a_w_out, b_w_in, b_q_norm, b_w_uq, b_w_out, kv_w_down, kv_norm, kv_w_up, ln_g, ln_b, loss_target, m_a_w_in, m_a_w_out, m_b_w_in, m_b_q_norm, m_b_w_uq, m_b_w_out, m_kv_w_down, m_kv_norm, m_kv_w_up, m_ln_g, m_ln_b, v_a_w_in, v_a_w_out, v_b_w_in, v_b_q_norm, v_b_w_uq, v_b_w_out, v_kv_w_down, v_kv_norm, v_kv_w_up, v_ln_g, v_ln_b):
    w_in = dict(a_w_in=a_w_in[0], a_w_out=a_w_out[0], b_w_in=b_w_in[0], b_w_uq=b_w_uq[0], b_w_out=b_w_out[0],
                kv_w_down=kv_w_down, kv_w_up=kv_w_up)
    m_in = dict(a_w_in=m_a_w_in[0], a_w_out=m_a_w_out[0], b_w_in=m_b_w_in[0], b_w_uq=m_b_w_uq[0], b_w_out=m_b_w_out[0],
                kv_w_down=m_kv_w_down, kv_w_up=m_kv_w_up)
    v_in = dict(a_w_in=v_a_w_in[0], a_w_out=v_a_w_out[0], b_w_in=v_b_w_in[0], b_w_uq=v_b_w_uq[0], b_w_out=v_b_w_out[0],
                kv_w_down=v_kv_w_down, kv_w_up=v_kv_w_up)
    shard_shapes = {name: w_in[name].shape for name in BIG}

    cx, cy, cc = lax.axis_index("x"), lax.axis_index("y"), lax.axis_index("c")
    chip = 2 * cx + cy
    flat_b = _flat_shards(w_in, BF16, B_SHARDS, FLAT_ROWS - A_ROWS)
    a16 = w_in["a_w_in"].astype(BF16)
    got_a = lax.dynamic_update_slice(_gather_weights(a16), a16[None], (chip, 0, 0))
    weights_a = dict(a_in=got_a.transpose(1, 0, 2).reshape(D_MODEL, IN_A))

    core_arr, chip_arr = cc.astype(jnp.int32).reshape(1), chip.astype(jnp.int32).reshape(1)

    def reduce_pair(full, shards, rows, tag):
        g_all = _gathered_from_full(full, shards, rows)
        half, cols = g_all.shape[1] // 2, g_all.shape[2]
        other_half = lax.dynamic_slice(g_all, (0, (1 - cc) * half, 0), (N_CHIPS, half, cols)).astype(BF16)
        theirs = _pair_swap(other_half, "pair_exchange_" + tag)
        return _pair_sum(g_all, theirs, core_arr, chip_arr, "pair_sum_" + tag)

    loss, grad_x, reduced, small = _local_step(x[0], loss_target[0], weights_a, kv_norm.reshape(1, -1), b_q_norm, ln_g, ln_b,
                                               late=(flat_b, chip), reduce=reduce_pair)

    g_big = {}
    for (own, received), shards, tag in zip(reduced, (EARLY_SHARDS, LATE_SHARDS), ("early", "late")):
        mine = _chip_sum(own, received, "chip_sum_" + tag)
        sibling = _pair_swap(mine, "pair_share_" + tag)
        g_flat = jnp.concatenate([jnp.where(cc == 0, mine, sibling), jnp.where(cc == 0, sibling, mine)], axis=0)
        if g_flat.shape == shard_shapes[shards[0][0]]:
            g_big[shards[0][0]] = g_flat
        else:
            g_big.update(_unflat_shards(g_flat, shard_shapes, shards))

    small_sum = _sum_small(_pack_small(small["ln_g"], small["ln_b"], small["q_norm"], small["kv_norm"], loss[:, :1]))
    loss_out = small_sum[6, 0]

    row = lambda a: a.reshape(1, -1)
    g_all = {**g_big, **_unpack_small(small_sum)}
    g_all["kv_norm"] = row(g_all["kv_norm"])
    state = {**{name: (w_in[name], m_in[name], v_in[name]) for name in BIG},
             "ln_g": (ln_g, m_ln_g, v_ln_g), "ln_b": (ln_b, m_ln_b, v_ln_b), "b_q_norm": (b_q_norm, m_b_q_norm, v_b_q_norm),
             "kv_norm": (row(kv_norm), row(m_kv_norm), row(v_kv_norm))}
    upd = {name: _adamw(state[name][0], g_all[name], state[name][1], state[name][2], rows=256, name="adamw_" + name)
           for name in W_NAMES}

    def shaped(name, a):
        if name == "kv_norm":
            return a.reshape(-1)
        return a[None] if name in ("a_w_in", "a_w_out", "b_w_in", "b_w_uq", "b_w_out") else a

    outputs = [[shaped(name, g_all[name]) for name in W_NAMES]]
    outputs += [[shaped(name, upd[name][k]) for name in W_NAMES] for k in range(3)]
    return (loss_out, grad_x[None], *outputs[0], *outputs[1], *outputs[2], *outputs[3])
```

```python
import functools
import math

import jax
import jax.numpy as jnp
from jax import lax
from jax.experimental import pallas as pl
from jax.experimental.pallas import tpu as pltpu

F32 = jnp.float32
BF16 = jnp.bfloat16
MESH = pl.DeviceIdType.MESH

D_MODEL = 1024
DEPTH = 2
H_A, DK_A, DV_A = 4, 256, 512
WIDTH_A = H_A * DV_A
CHUNK = 128
H_B, QK_NOPE, QK_ROPE, V_HEAD = 16, 128, 64, 128
QK_PAD = 256
Q_LORA, KV_LORA = 768, 512
KV_DOWN_PAD = 640
WIDTH_B = H_B * V_HEAD
IN_A = 2 * H_A * DK_A + 2 * WIDTH_A
IN_B = Q_LORA + WIDTH_B
H1_B = KV_DOWN_PAD + IN_B
ROPE_BASE = 10000.0
ALPHA = (2.0 * DEPTH) ** 0.25
ATT_SCALE = (QK_NOPE + QK_ROPE) ** -0.5
NEG_BIG = -1e30

ADAM_LR, ADAM_B1, ADAM_B2, ADAM_EPS, ADAM_WD, ADAM_STEP = 0.001, 0.9, 0.999, 1e-08, 0.01, 10

VMEM_LIMIT_BYTES = 56 * 1024 * 1024
LANES = 128
FLAT_COLS = 1024
SHARD_ROWS = (("a_w_in", 1536), ("a_w_out", 512), ("b_w_in", 704), ("b_w_uq", 576), ("b_w_out", 512),
              ("kv_w_down", 144), ("kv_w_up", 512))
FLAT_ROWS = 4608
N_CHIPS = 4
N_DEV = 8


def _params(sem, vmem=VMEM_LIMIT_BYTES):
    return pltpu.CompilerParams(dimension_semantics=sem, vmem_limit_bytes=vmem)


def _row_spec(ts, w, col_block=0):
    return pl.BlockSpec((ts, w), lambda i: (i, col_block))


def _bc_spec(shape):
    nd = len(shape)
    return pl.BlockSpec(shape, lambda i: (0,) * nd)


def _sigmoid(x):
    return 1.0 / (1.0 + jnp.exp(-x))


def _fold8(v):
    ts, w = v.shape
    return jnp.sum(v.reshape(ts // 8, 8, w), axis=0)


def _mm(a, b, *, ta=False, tb=False, out_dtype=F32, tm=1024, tn=512, tk=None, name, extras=(), epilogue=None, out_tn=None,
        side=None):
    if ta:
        K, M = a.shape
    else:
        M, K = a.shape
    if tb:
        N, Kb = b.shape
    else:
        Kb, N = b.shape
    assert K == Kb, (a.shape, b.shape)
    tm, tn = min(tm, M), min(tn, N)
    tk = K if tk is None else min(tk, K)
    assert M % tm == 0 and N % tn == 0 and K % tk == 0, (name, M, N, K, tm, tn, tk)
    grid = (M // tm, N // tn, K // tk)
    nk = grid[2]
    out_tn = tn if out_tn is None else out_tn
    n_extra = len(extras)
    side_inputs = [] if side is None else list(side["inputs"])
    n_side = len(side_inputs)
    n_acc = 0 if nk == 1 else 1
    dims = (((0,) if ta else (1,), (1,) if tb else (0,)), ((), ()))

    def body(a_ref, b_ref, *rest):
        extra_refs, rest = rest[:n_extra], rest[n_extra:]
        side_in, o_ref, rest = rest[:n_side], rest[n_side], rest[n_side + 1:]
        if side is not None:
            side_refs, rest = side_in + rest[:1] + rest[1 + n_acc:], rest[1:]
            ids = [pl.program_id(d) for d in range(3)]

            @pl.when((ids[0] == 0) & (ids[1] == 0) & (ids[2] == 0))
            def _():
                side["start"](*side_refs)

        prod = lax.dot_general(a_ref[...].astype(BF16), b_ref[...].astype(BF16), dims,
                               preferred_element_type=F32)

        def store(tile):
            if epilogue is None:
                o_ref[...] = tile.astype(o_ref.dtype)
            else:
                epilogue(tile, o_ref, *extra_refs)

        if nk == 1:
            store(prod)
        else:
            acc = rest[0]
            k = pl.program_id(2)

            @pl.when(k == 0)
            def _():
                acc[...] = prod

            @pl.when(k > 0)
            def _():
                acc[...] += prod

            @pl.when(k == nk - 1)
            def _():
                store(acc[...])

        if side is not None:
            @pl.when((ids[0] == grid[0] - 1) & (ids[1] == grid[1] - 1) & (ids[2] == grid[2] - 1))
            def _():
                side["finish"](*side_refs)

    a_spec = pl.BlockSpec((tk, tm), lambda i, j, k: (k, i)) if ta else pl.BlockSpec((tm, tk), lambda i, j, k: (i, k))
    b_spec = pl.BlockSpec((tn, tk), lambda i, j, k: (j, k)) if tb else pl.BlockSpec((tk, tn), lambda i, j, k: (k, j))
    extra_specs = [pl.BlockSpec((tm, e.shape[1]), lambda i, j, k: (i, 0)) for e in extras]
    out_specs = pl.BlockSpec((tm, out_tn), lambda i, j, k: (i, j))
    out_shape = jax.ShapeDtypeStruct((M, (N // tn) * out_tn), out_dtype)
    scratch = [] if nk == 1 else [pltpu.VMEM((tm, tn), F32)]
    if side is not None:
        out_specs, out_shape, scratch = [out_specs, HBM_SPEC], [out_shape, side["out_shape"]], scratch + list(side["scratch"])
    return pl.pallas_call(
        body, name=name, grid=grid,
        in_specs=[a_spec, b_spec] + extra_specs + [HBM_SPEC] * n_side,
        out_specs=out_specs, out_shape=out_shape, scratch_shapes=scratch,
        compiler_params=_params(("parallel", "parallel", "arbitrary") if side is None else ("arbitrary",) * 3),
    )(a, b, *extras, *side_inputs)


def _rope_tables_a(s):
    half = DK_A // 2
    inv = ROPE_BASE ** (-jnp.arange(half, dtype=F32) / half)
    ang = jnp.arange(s, dtype=F32)[:, None] * inv[None, :]
    return jnp.cos(ang), jnp.sin(ang)


def _rope_tables_b(s):
    half = QK_ROPE // 2
    inv = ROPE_BASE ** (-jnp.arange(half, dtype=F32) / half)
    ang = jnp.arange(s, dtype=F32)[:, None] * inv[None, :]
    c, sn = jnp.cos(ang), jnp.sin(ang)
    z = jnp.zeros_like(c)
    cos = jnp.concatenate([c, c, z, z], axis=1)
    sa = jnp.concatenate([-sn, z, z, z], axis=1)
    sb = jnp.concatenate([z, sn, z, z], axis=1)
    return cos, sa, sb


def _rope_b(r, cos, sa, sb, sign):
    return r * cos + sign * (pltpu.roll(r, 96, 1) * sa + pltpu.roll(r, 32, 1) * sb)


def _retention_tables():
    lg = jnp.log1p(-jnp.exp2(-5.0 - jnp.arange(H_A, dtype=F32)))
    idx = jnp.arange(CHUNK, dtype=F32)
    diff = idx[:, None] - idx[None, :]
    causal = diff >= 0
    dmat = jnp.where(causal, jnp.exp(jnp.where(causal, diff, 0.0)[None] * lg[:, None, None]), 0.0)
    qdec = jnp.exp((idx + 1.0)[None, :] * lg[:, None])[:, :, None]
    kdec = jnp.exp((CHUNK - 1.0 - idx)[None, :] * lg[:, None])[:, :, None]
    cdec = jnp.broadcast_to(jnp.exp(CHUNK * lg)[:, None, None], (H_A, 1, DV_A))
    return dmat, qdec, kdec, cdec


def _group_norm(o):
    mu = jnp.mean(o, axis=-1, keepdims=True)
    oc = o - mu
    var = jnp.mean(oc * oc, axis=-1, keepdims=True)
    rstd = lax.rsqrt(var + 1e-5)
    return oc * rstd, rstd


Q_COL, K_COL, V_COL, GATE_COL = 0, H_A * DK_A, 2 * H_A * DK_A, 2 * H_A * DK_A + WIDTH_A


def _ln_stats(z):
    mu = jnp.mean(z, axis=-1, keepdims=True)
    zc = z - mu
    var = jnp.mean(zc * zc, axis=-1, keepdims=True)
    rstd = lax.rsqrt(var + 1e-5)
    return zc * rstd, rstd


def _ln_bwd(dy, xhat, rstd, g):
    dxh = dy * g
    m1 = jnp.mean(dxh, axis=-1, keepdims=True)
    m2 = jnp.mean(dxh * xhat, axis=-1, keepdims=True)
    return rstd * (dxh - m1 - xhat * m2)


def _ln_fwd(x, y, g, b, *, ts):
    s = x.shape[0]

    def body(x_ref, y_ref, g_ref, b_ref, o_ref, o16_ref):
        xhat, _ = _ln_stats(ALPHA * x_ref[...] + y_ref[...])
        out = xhat * g_ref[...] + b_ref[...]
        o_ref[...] = out
        o16_ref[...] = out.astype(o16_ref.dtype)

    return pl.pallas_call(
        body, name="ln_fwd", grid=(s // ts,),
        in_specs=[_row_spec(ts, D_MODEL), _row_spec(ts, D_MODEL), _bc_spec((1, D_MODEL)), _bc_spec((1, D_MODEL))],
        out_specs=[_row_spec(ts, D_MODEL), _row_spec(ts, D_MODEL)],
        out_shape=[jax.ShapeDtypeStruct((s, D_MODEL), F32), jax.ShapeDtypeStruct((s, D_MODEL), BF16)],
        compiler_params=_params(("parallel",)),
    )(x, y, g, b)


def _ln_loss_bwd(x1, y, target, g, b, *, ts):
    s = x1.shape[0]
    n = s // ts

    def body(x_ref, y_ref, t_ref, g_ref, b_ref, dz_ref, dz16_ref, dg_ref, db_ref, loss_ref, ag, ab, al):
        i = pl.program_id(0)

        @pl.when(i == 0)
        def _():
            ag[...] = jnp.zeros_like(ag)
            ab[...] = jnp.zeros_like(ab)
            al[...] = jnp.zeros_like(al)

        xhat, rstd = _ln_stats(ALPHA * x_ref[...] + y_ref[...])
        err = xhat * g_ref[...] + b_ref[...] - t_ref[...]
        al[...] += _fold8(err * err)
        dy = err * (1.0 / D_MODEL)
        ag[...] += _fold8(dy * xhat)
        ab[...] += _fold8(dy)
        dz = _ln_bwd(dy, xhat, rstd, g_ref[...])
        dz_ref[...] = dz
        dz16_ref[...] = dz.astype(dz16_ref.dtype)

        @pl.when(i == n - 1)
        def _():
            dg_ref[...] = jnp.sum(ag[...], axis=0, keepdims=True)
            db_ref[...] = jnp.sum(ab[...], axis=0, keepdims=True)
            loss_ref[...] = jnp.full((1, LANES), (0.5 / D_MODEL) * jnp.sum(al[...]), F32)

    return pl.pallas_call(
        body, name="ln_loss_bwd", grid=(n,),
        in_specs=[_row_spec(ts, D_MODEL)] * 3 + [_bc_spec((1, D_MODEL))] * 2,
        out_specs=[_row_spec(ts, D_MODEL), _row_spec(ts, D_MODEL), _bc_spec((1, D_MODEL)), _bc_spec((1, D_MODEL)),
                   _bc_spec((1, LANES))],
        out_shape=[jax.ShapeDtypeStruct((s, D_MODEL), F32), jax.ShapeDtypeStruct((s, D_MODEL), BF16),
                   jax.ShapeDtypeStruct((1, D_MODEL), F32), jax.ShapeDtypeStruct((1, D_MODEL), F32),
                   jax.ShapeDtypeStruct((1, LANES), F32)],
        scratch_shapes=[pltpu.VMEM((8, D_MODEL), F32)] * 3,
        compiler_params=_params(("arbitrary",)),
    )(x1, y, target, g, b)


def _ln_bwd_call(dy, x, y, g, *, ts):
    s = x.shape[0]
    n = s // ts

    def body(dy_ref, x_ref, y_ref, g_ref, dz_ref, dz16_ref, dg_ref, db_ref, ag, ab):
        i = pl.program_id(0)

        @pl.when(i == 0)
        def _():
            ag[...] = jnp.zeros_like(ag)
            ab[...] = jnp.zeros_like(ab)

        xhat, rstd = _ln_stats(ALPHA * x_ref[...] + y_ref[...])
        dy = dy_ref[...]
        ag[...] += _fold8(dy * xhat)
        ab[...] += _fold8(dy)
        dz = _ln_bwd(dy, xhat, rstd, g_ref[...])
        dz_ref[...] = dz
        dz16_ref[...] = dz.astype(dz16_ref.dtype)

        @pl.when(i == n - 1)
        def _():
            dg_ref[...] = jnp.sum(ag[...], axis=0, keepdims=True)
            db_ref[...] = jnp.sum(ab[...], axis=0, keepdims=True)

    return pl.pallas_call(
        body, name="ln_bwd", grid=(n,),
        in_specs=[_row_spec(ts, D_MODEL)] * 3 + [_bc_spec((1, D_MODEL))],
        out_specs=[_row_spec(ts, D_MODEL), _row_spec(ts, D_MODEL), _bc_spec((1, D_MODEL)), _bc_spec((1, D_MODEL))],
        out_shape=[jax.ShapeDtypeStruct((s, D_MODEL), F32), jax.ShapeDtypeStruct((s, D_MODEL), BF16),
                   jax.ShapeDtypeStruct((1, D_MODEL), F32), jax.ShapeDtypeStruct((1, D_MODEL), F32)],
        scratch_shapes=[pltpu.VMEM((8, D_MODEL), F32)] * 2,
        compiler_params=_params(("arbitrary",)),
    )(dy, x, y, g)


def _residual_store(tile, o_ref, dz_ref):
    o_ref[...] = ALPHA * dz_ref[...] + tile


C_LAT = slice(0, KV_LORA)
C_ROPE = slice(KV_LORA, KV_DOWN_PAD)
C_QL = slice(KV_DOWN_PAD, KV_DOWN_PAD + Q_LORA)
C_GATE = slice(KV_DOWN_PAD + Q_LORA, H1_B)


def _rms(x, eps=1e-6):
    r = lax.rsqrt(jnp.mean(x * x, axis=-1, keepdims=True) + eps)
    return x * r, r


def _rms_bwd(dy, xhat, r, g):
    dxh = dy * g
    return r * (dxh - xhat * jnp.mean(dxh * xhat, axis=-1, keepdims=True))


def _kvq_prep(h1, kv_norm, q_norm, cos, sa, sb, *, ts):
    s = h1.shape[0]

    def body(h_ref, kn_ref, qn_ref, c_ref, sa_ref, sb_ref, lat_ref, kr_ref, ql_ref):
        lat, _ = _rms(h_ref[:, C_LAT])
        lat_ref[...] = (lat * kn_ref[...]).astype(lat_ref.dtype)
        kr_ref[...] = _rope_b(h_ref[:, C_ROPE], c_ref[...], sa_ref[...], sb_ref[...], 1.0).astype(kr_ref.dtype)
        ql, _ = _rms(h_ref[:, C_QL])
        ql_ref[...] = (ql * qn_ref[...]).astype(ql_ref.dtype)

    return pl.pallas_call(
        body, name="kvq_prep", grid=(s // ts,),
        in_specs=[_row_spec(ts, H1_B), _bc_spec((1, KV_LORA)), _bc_spec((1, Q_LORA))] + [_row_spec(ts, 128)] * 3,
        out_specs=[_row_spec(ts, KV_LORA), _row_spec(ts, 128), _row_spec(ts, Q_LORA)],
        out_shape=[jax.ShapeDtypeStruct((s, KV_LORA), BF16), jax.ShapeDtypeStruct((s, 128), BF16),
                   jax.ShapeDtypeStruct((s, Q_LORA), BF16)],
        compiler_params=_params(("parallel",)),
    )(h1, kv_norm, q_norm, cos, sa, sb)


LOG2E = 1.4426950408889634
LN2 = 0.6931471805599453
Q_SCALE = ATT_SCALE * LOG2E


def _rope_q_store(tile, o_ref, c_ref, sa_ref, sb_ref):
    c, a, b = c_ref[...], sa_ref[...], sb_ref[...]
    for hd in range(tile.shape[1] // QK_PAD):
        lo = hd * QK_PAD
        o_ref[:, lo:lo + 128] = (tile[:, lo:lo + 128] * Q_SCALE).astype(o_ref.dtype)
        o_ref[:, lo + 128:lo + 256] = (_rope_b(tile[:, lo + 128:lo + 256], c, a, b, 1.0) * Q_SCALE).astype(o_ref.dtype)


def _assemble_k_store(tile, o_ref, kr_ref):
    r = kr_ref[...]
    for hd in range(tile.shape[1] // QK_NOPE):
        o_ref[:, hd * QK_PAD:hd * QK_PAD + 128] = tile[:, hd * 128:(hd + 1) * 128].astype(o_ref.dtype)
        o_ref[:, hd * QK_PAD + 128:(hd + 1) * QK_PAD] = r


def _h1_bwd(h1, dlat_k, dlat_v, dkr_heads, dqn, dg16, kv_norm, q_norm, cos, sa, sb, *, ts):
    s = h1.shape[0]
    n = s // ts

    def body(h_ref, dk_ref, dv_ref, dkr_ref, dqn_ref, dg_ref, kn_ref, qn_ref, c_ref, sa_ref, sb_ref,
             o_ref, dkn_ref, dqn_out_ref, akn, aqn):
        i = pl.program_id(0)

        @pl.when(i == 0)
        def _():
            akn[...] = jnp.zeros_like(akn)
            aqn[...] = jnp.zeros_like(aqn)

        lat, r = _rms(h_ref[:, C_LAT])
        dlat = dk_ref[...] + dv_ref[...]
        akn[...] += _fold8(dlat * lat)
        o_ref[:, C_LAT] = _rms_bwd(dlat, lat, r, kn_ref[...]).astype(o_ref.dtype)

        dkr = dkr_ref[:, 0:128]
        for hd in range(1, H_B):
            dkr = dkr + dkr_ref[:, hd * 128:(hd + 1) * 128]
        o_ref[:, C_ROPE] = _rope_b(dkr, c_ref[...], sa_ref[...], sb_ref[...], -1.0).astype(o_ref.dtype)

        ql, rq = _rms(h_ref[:, C_QL])
        dq = dqn_ref[...]
        aqn[...] += _fold8(dq * ql)
        o_ref[:, C_QL] = _rms_bwd(dq, ql, rq, qn_ref[...]).astype(o_ref.dtype)
        o_ref[:, C_GATE] = dg_ref[...]

        @pl.when(i == n - 1)
        def _():
            dkn_ref[...] = jnp.sum(akn[...], axis=0, keepdims=True)
            dqn_out_ref[...] = jnp.sum(aqn[...], axis=0, keepdims=True)

    return pl.pallas_call(
        body, name="h1_bwd", grid=(n,),
        in_specs=[_row_spec(ts, H1_B), _row_spec(ts, KV_LORA), _row_spec(ts, KV_LORA), _row_spec(ts, H_B * 128),
                  _row_spec(ts, Q_LORA), _row_spec(ts, WIDTH_B), _bc_spec((1, KV_LORA)), _bc_spec((1, Q_LORA))]
        + [_row_spec(ts, 128)] * 3,
        out_specs=[_row_spec(ts, H1_B), _bc_spec((1, KV_LORA)), _bc_spec((1, Q_LORA))],
        out_shape=[jax.ShapeDtypeStruct((s, H1_B), BF16), jax.ShapeDtypeStruct((1, KV_LORA), F32),
                   jax.ShapeDtypeStruct((1, Q_LORA), F32)],
        scratch_shapes=[pltpu.VMEM((8, KV_LORA), F32), pltpu.VMEM((8, Q_LORA), F32)],
        compiler_params=_params(("arbitrary",)),
    )(h1, dlat_k, dlat_v, dkr_heads, dqn, dg16, kv_norm, q_norm, cos, sa, sb)


def _dot(a, b, ca, cb):
    return lax.dot_general(a, b, (((ca,), (cb,)), ((), ())), preferred_element_type=F32)


CHUNKS_PER_STEP = 2


def _table_specs():
    full = lambda shape: pl.BlockSpec(shape, lambda i: (0,) * len(shape))
    return [full((H_A, CHUNK, CHUNK)), full((H_A, CHUNK, 1)), full((H_A, CHUNK, 1)), full((H_A, 1, DV_A))]


def _retention_fwd(h_a, cos, sin, tables):
    s = h_a.shape[0]
    n = s // CHUNK

    def body(h_ref, c_ref, s_ref, dm_ref, qd_ref, kd_ref, cd_ref, q_ref, k_ref, v_ref, o_ref, u_ref, st_ref, state):
        @pl.when(pl.program_id(0) == 0)
        def _():
            state[...] = jnp.zeros_like(state)

        for cb in range(CHUNKS_PER_STEP):
            rows = slice(cb * CHUNK, (cb + 1) * CHUNK)
            c, sn = c_ref[rows, :], s_ref[rows, :]
            for hd in range(H_A):
                qs, vs = slice(hd * DK_A, (hd + 1) * DK_A), slice(hd * DV_A, (hd + 1) * DV_A)
                for r_ref, base, scale in ((q_ref, Q_COL, 1.0), (k_ref, K_COL, DK_A ** -0.5)):
                    lo = base + hd * DK_A
                    x1, x2 = h_ref[rows, lo:lo + 128], h_ref[rows, lo + 128:lo + 256]
                    r_ref[rows, hd * DK_A:hd * DK_A + 128] = ((x1 * c - x2 * sn) * scale).astype(r_ref.dtype)
                    r_ref[rows, hd * DK_A + 128:(hd + 1) * DK_A] = ((x2 * c + x1 * sn) * scale).astype(r_ref.dtype)
                v_ref[rows, vs] = h_ref[rows, V_COL + hd * DV_A:V_COL + (hd + 1) * DV_A].astype(v_ref.dtype)
                qv, kv, vv = q_ref[rows, qs], k_ref[rows, qs], v_ref[rows, vs]
                st = state[hd]
                st16 = st.astype(BF16)
                st_ref[cb, hd] = st16
                scores = _dot(qv, kv, 1, 1) * dm_ref[hd]
                qd = (qv.astype(F32) * qd_ref[hd]).astype(BF16)
                o = _dot(scores.astype(BF16), vv, 1, 0) + _dot(qd, st16, 1, 0)
                o_ref[rows, vs] = o
                kd = (kv.astype(F32) * kd_ref[hd]).astype(BF16)
                state[hd] = st * cd_ref[hd] + _dot(kd, vv, 0, 0)
                on, _ = _group_norm(o)
                g = h_ref[rows, GATE_COL + hd * DV_A:GATE_COL + (hd + 1) * DV_A]
                u_ref[rows, vs] = (on * (g * _sigmoid(g))).astype(u_ref.dtype)

    row = lambda w: pl.BlockSpec((CHUNKS_PER_STEP * CHUNK, w), lambda i: (i, 0))
    return pl.pallas_call(
        body, name="retention_fwd", grid=(n // CHUNKS_PER_STEP,),
        in_specs=[row(IN_A), row(128), row(128)] + _table_specs(),
        out_specs=[row(H_A * DK_A), row(H_A * DK_A), row(WIDTH_A), row(WIDTH_A), row(WIDTH_A),
                   pl.BlockSpec((CHUNKS_PER_STEP, H_A, DK_A, DV_A), lambda i: (i, 0, 0, 0))],
        out_shape=[jax.ShapeDtypeStruct((s, H_A * DK_A), BF16), jax.ShapeDtypeStruct((s, H_A * DK_A), BF16),
                   jax.ShapeDtypeStruct((s, WIDTH_A), BF16), jax.ShapeDtypeStruct((s, WIDTH_A), F32),
                   jax.ShapeDtypeStruct((s, WIDTH_A), BF16), jax.ShapeDtypeStruct((n, H_A, DK_A, DV_A), BF16)],
        scratch_shapes=[pltpu.VMEM((H_A, DK_A, DV_A), F32)],
        compiler_params=_params(("arbitrary",)),
    )(h_a, cos, sin, *tables)


def _retention_bwd(q, k, v, states, du, o, h_a, cos, sin, tables):
    s = q.shape[0]
    n = s // CHUNK

    def body(q_ref, k_ref, v_ref, st_ref, du_ref, o_ref, g_ref, c_ref, s_ref, dm_ref, qd_ref, kd_ref, cd_ref, dh_ref, grad_state):
        @pl.when(pl.program_id(0) == 0)
        def _():
            grad_state[...] = jnp.zeros_like(grad_state)

        for cb in reversed(range(CHUNKS_PER_STEP)):
            rows = slice(cb * CHUNK, (cb + 1) * CHUNK)
            c, sn = c_ref[rows, :], s_ref[rows, :]
            for hd in range(H_A):
                qs, vs = slice(hd * DK_A, (hd + 1) * DK_A), slice(hd * DV_A, (hd + 1) * DV_A)
                on, rstd = _group_norm(o_ref[rows, vs])
                g = g_ref[rows, vs]
                sg = _sigmoid(g)
                du_v = du_ref[rows, vs]
                don = du_v * (g * sg)
                dh_ref[rows, GATE_COL + hd * DV_A:GATE_COL + (hd + 1) * DV_A] = (du_v * on * (sg * (1.0 + g * (1.0 - sg)))).astype(dh_ref.dtype)
                m1 = jnp.mean(don, axis=-1, keepdims=True)
                m2 = jnp.mean(don * on, axis=-1, keepdims=True)
                dov = (rstd * (don - m1 - on * m2)).astype(BF16)

                qv, kv, vv = q_ref[rows, qs], k_ref[rows, qs], v_ref[rows, vs]
                dm = dm_ref[hd]
                gs = grad_state[hd]
                g16 = gs.astype(BF16)
                scores = (_dot(qv, kv, 1, 1) * dm).astype(BF16)
                dscores = (_dot(dov, vv, 1, 1) * dm).astype(BF16)
                qd = (qv.astype(F32) * qd_ref[hd]).astype(BF16)
                kd = (kv.astype(F32) * kd_ref[hd]).astype(BF16)
                dq = _dot(dscores, kv, 1, 0) + _dot(dov, st_ref[cb, hd], 1, 1) * qd_ref[hd]
                dk = (_dot(dscores, qv, 0, 0) + _dot(vv, g16, 1, 1) * kd_ref[hd]) * (DK_A ** -0.5)
                dh_ref[rows, V_COL + hd * DV_A:V_COL + (hd + 1) * DV_A] = (_dot(scores, dov, 0, 0) + _dot(kd, g16, 1, 0)).astype(dh_ref.dtype)
                grad_state[hd] = gs * cd_ref[hd] + _dot(qd, dov, 0, 0)
                for d, base in ((dq, Q_COL), (dk, K_COL)):
                    lo = base + hd * DK_A
                    d1, d2 = d[:, 0:128], d[:, 128:256]
                    dh_ref[rows, lo:lo + 128] = (d1 * c + d2 * sn).astype(dh_ref.dtype)
                    dh_ref[rows, lo + 128:lo + 256] = (d2 * c - d1 * sn).astype(dh_ref.dtype)

    steps = n // CHUNKS_PER_STEP
    rev = lambda i: steps - 1 - i
    row = lambda w, col=0: pl.BlockSpec((CHUNKS_PER_STEP * CHUNK, w), lambda i: (rev(i), col))
    return pl.pallas_call(
        body, name="retention_bwd", grid=(steps,),
        in_specs=[row(H_A * DK_A), row(H_A * DK_A), row(WIDTH_A),
                  pl.BlockSpec((CHUNKS_PER_STEP, H_A, DK_A, DV_A), lambda i: (rev(i), 0, 0, 0)),
                  row(WIDTH_A), row(WIDTH_A), row(WIDTH_A, GATE_COL // WIDTH_A), row(128), row(128)] + _table_specs(),
        out_specs=row(IN_A),
        out_shape=jax.ShapeDtypeStruct((s, IN_A), BF16),
        scratch_shapes=[pltpu.VMEM((H_A, DK_A, DV_A), F32)],
        compiler_params=_params(("arbitrary",)),
    )(q, k, v, states, du, o, h_a, cos, sin, *tables)


GATE_BLOCK0 = (KV_DOWN_PAD + Q_LORA) // LANES


def _causal_mask(sc, row0):
    row = lax.broadcasted_iota(jnp.int32, sc.shape, 0) + row0
    col = lax.broadcasted_iota(jnp.int32, sc.shape, 1)
    return jnp.where(col <= row, sc, NEG_BIG)


def _key_block_loop(step, n, per_trip, smallest=1):
    def trip_body(jj, carry):
        for t in range(per_trip):
            step(per_trip * jj + t)
        return carry

    lax.fori_loop(0, n // per_trip, trip_body, 0)
    group = per_trip // 2
    while group >= smallest:
        def tail(group=group):
            first = (n // (2 * group)) * (2 * group)
            for t in range(group):
                step(first + t)

        pl.when((n // group) % 2 == 1)(tail)
        group //= 2


def _attention_fwd(q, k, v, h1, *, blk, bk, sub, per_trip):
    s = q.shape[0]
    nb = s // blk

    def body(q_ref, k_ref, v_ref, g_ref, o_ref, u_ref, lse_ref, vext_s, m_s, acc_s):
        i = pl.program_id(1)

        @pl.when(i == 0)
        def _():
            vext_s[:, 0:V_HEAD] = v_ref[...]
            vext_s[:, V_HEAD:2 * V_HEAD] = jnp.ones((s, V_HEAD), vext_s.dtype)

        m_s[...] = jnp.full_like(m_s, NEG_BIG)
        acc_s[...] = jnp.zeros_like(acc_s)

        def update(rows, kb, vb, row0):
            sc = _dot(q_ref[rows, :], kb, 1, 1)
            if row0 is not None:
                sc = _causal_mask(sc, row0)
            m_prev = m_s[rows, :]
            m_new = jnp.maximum(m_prev, jnp.max(sc, axis=-1, keepdims=True))
            p = jnp.exp2(sc - jnp.tile(m_new, (1, kb.shape[0] // LANES)))
            a = jnp.exp2(m_prev - m_new)
            acc_s[rows, :] = jnp.tile(a, (1, 2)) * acc_s[rows, :] + _dot(p.astype(BF16), vb, 1, 0)
            m_s[rows, :] = m_new

        def step(j):
            kv_rows = pl.ds(pl.multiple_of(j * bk, bk), bk)
            kb, vb = k_ref[kv_rows, :], vext_s[kv_rows, :]
            for r in range(blk // sub):
                update(slice(r * sub, (r + 1) * sub), kb, vb, None)

        _key_block_loop(step, i * (blk // bk), per_trip, smallest=blk // bk)
        for r in range(blk // sub):
            ncols = (r + 1) * sub
            kv_rows = pl.ds(pl.multiple_of(i * blk, blk), ncols)
            update(slice(r * sub, (r + 1) * sub), k_ref[kv_rows, :], vext_s[kv_rows, :], r * sub)
        acc = acc_s[...]
        l = acc[:, V_HEAD:2 * V_HEAD]
        o = acc[:, 0:V_HEAD] / l
        g = g_ref[...]
        o_ref[...] = o
        u_ref[...] = (o * (g * _sigmoid(g))).astype(u_ref.dtype)
        lse_ref[0] = (m_s[...] + jnp.log2(l))[:, 0:1]

    blk_idx = lambda h, i: (i, h)
    return pl.pallas_call(
        body, name="attention_fwd", grid=(H_B, nb),
        in_specs=[pl.BlockSpec((blk, QK_PAD), blk_idx), pl.BlockSpec((s, QK_PAD), lambda h, i: (0, h)),
                  pl.BlockSpec((s, V_HEAD), lambda h, i: (0, h)), pl.BlockSpec((blk, LANES), lambda h, i: (i, GATE_BLOCK0 + h))],
        out_specs=[pl.BlockSpec((blk, V_HEAD), blk_idx), pl.BlockSpec((blk, V_HEAD), blk_idx),
                   pl.BlockSpec((1, blk, 1), lambda h, i: (h, i, 0))],
        out_shape=[jax.ShapeDtypeStruct((s, WIDTH_B), F32), jax.ShapeDtypeStruct((s, WIDTH_B), BF16),
                   jax.ShapeDtypeStruct((H_B, s, 1), F32)],
        scratch_shapes=[pltpu.VMEM((s, 2 * V_HEAD), BF16), pltpu.VMEM((blk, LANES), F32), pltpu.VMEM((blk, 2 * V_HEAD), F32)],
        compiler_params=_params(("parallel", "arbitrary")),
    )(q, k, v, h1)


def _attention_bwd(q, k, v, du, o, h1, lse, cos, sa, sb, *, blk, bk, per_trip):
    s = q.shape[0]
    nb = s // blk

    def body(q_ref, k_ref, v_ref, du_ref, o_ref, g_ref, lse_ref, c_ref, sa_ref, sb_ref,
             dq_ref, dkn_ref, dkr_ref, dv_ref, dg_ref, lse_s, dl_s, do_s, dq_s, dkn_s, dv_s):
        i = pl.program_id(1)
        g = g_ref[...]
        sg = _sigmoid(g)
        du_v, ov = du_ref[...], o_ref[...]
        do = du_v * (g * sg)
        dg_ref[...] = (du_v * ov * (sg * (1.0 + g * (1.0 - sg)))).astype(dg_ref.dtype)
        do_s[...] = do.astype(do_s.dtype)
        dl_s[...] = jnp.broadcast_to(jnp.sum(do * ov, axis=-1, keepdims=True), (blk, LANES))
        lse_s[...] = jnp.broadcast_to(lse_ref[0], (blk, LANES))
        dq_s[...] = jnp.zeros_like(dq_s)

        def products(rows, kv_rows, row0):
            qv, dov, kb = q_ref[rows, :], do_s[rows, :], k_ref[kv_rows, :]
            tile = (1, kb.shape[0] // LANES)
            sc = _dot(qv, kb, 1, 1)
            if row0 is not None:
                sc = _causal_mask(sc, row0)
            p = jnp.exp2(sc - jnp.tile(lse_s[rows, :], tile))
            dp = _dot(dov, v_ref[kv_rows, :], 1, 1)
            ds = (p * (dp - jnp.tile(dl_s[rows, :], tile))).astype(BF16)
            dq_s[rows, :] += _dot(ds, kb, 1, 0)
            return _dot(ds, qv, 0, 0), _dot(p.astype(BF16), dov, 0, 0)

        def put(kv_rows, dk_c, dv_c, first):
            if first:
                dkn_s[kv_rows, :] = dk_c[:, 0:128]
                dkr_ref[kv_rows, :] = dk_c[:, 128:256]
                dv_s[kv_rows, :] = dv_c
            else:
                dkn_s[kv_rows, :] += dk_c[:, 0:128]
                dkr_ref[kv_rows, :] += dk_c[:, 128:256]
                dv_s[kv_rows, :] += dv_c

        n_sub = blk // bk
        sub_rows = [slice(r * bk, (r + 1) * bk) for r in range(n_sub)]

        def step(j):
            kv_rows = pl.ds(pl.multiple_of(j * bk, bk), bk)
            for rows in sub_rows:
                dk_c, dv_c = products(rows, kv_rows, None)
                put(kv_rows, dk_c, dv_c, False)

        _key_block_loop(step, i * n_sub, per_trip, smallest=n_sub)
        for c in range(n_sub):
            kv_rows = pl.ds(pl.multiple_of(i * blk + c * bk, bk), bk)
            for r in range(c, n_sub):
                dk_c, dv_c = products(sub_rows[r], kv_rows, 0 if r == c else None)
                put(kv_rows, dk_c, dv_c, r == c)
        dq = dq_s[...] * ATT_SCALE
        dq_ref[:, 0:128] = dq[:, 0:128].astype(dq_ref.dtype)
        dq_ref[:, 128:256] = _rope_b(dq[:, 128:256], c_ref[...], sa_ref[...], sb_ref[...], -1.0).astype(dq_ref.dtype)

        @pl.when(i == nb - 1)
        def _():
            dkn_ref[...] = (dkn_s[...] * LN2).astype(dkn_ref.dtype)
            dv_ref[...] = dv_s[...].astype(dv_ref.dtype)
            dkr_ref[...] = dkr_ref[...] * LN2

    head = lambda h, i: (0, h)
    blk_idx = lambda h, i: (i, h)
    row_idx = lambda h, i: (i, 0)
    return pl.pallas_call(
        body, name="attention_bwd", grid=(H_B, nb),
        in_specs=[pl.BlockSpec((blk, QK_PAD), blk_idx), pl.BlockSpec((s, QK_PAD), head), pl.BlockSpec((s, V_HEAD), head),
                  pl.BlockSpec((blk, V_HEAD), blk_idx), pl.BlockSpec((blk, V_HEAD), blk_idx),
                  pl.BlockSpec((blk, LANES), lambda h, i: (i, GATE_BLOCK0 + h)), pl.BlockSpec((1, blk, 1), lambda h, i: (h, i, 0)),
                  pl.BlockSpec((blk, LANES), row_idx), pl.BlockSpec((blk, LANES), row_idx), pl.BlockSpec((blk, LANES), row_idx)],
        out_specs=[pl.BlockSpec((blk, QK_PAD), blk_idx), pl.BlockSpec((s, 128), head), pl.BlockSpec((s, 128), head),
                   pl.BlockSpec((s, 128), head), pl.BlockSpec((blk, V_HEAD), blk_idx)],
        out_shape=[jax.ShapeDtypeStruct((s, H_B * QK_PAD), BF16), jax.ShapeDtypeStruct((s, H_B * 128), BF16),
                   jax.ShapeDtypeStruct((s, H_B * 128), F32), jax.ShapeDtypeStruct((s, H_B * 128), BF16),
                   jax.ShapeDtypeStruct((s, WIDTH_B), BF16)],
        scratch_shapes=[pltpu.VMEM((blk, LANES), F32), pltpu.VMEM((blk, LANES), F32), pltpu.VMEM((blk, V_HEAD), BF16),
                        pltpu.VMEM((blk, QK_PAD), F32), pltpu.VMEM((s, 128), F32), pltpu.VMEM((s, V_HEAD), F32)],
        compiler_params=_params(("parallel", "arbitrary")),
    )(q, k, v, du, o, h1, lse, cos, sa, sb)


def _local_step(x, target, w, kv_norm, q_norm, ln_g, ln_b, *, ts=512, blk=512, late=None, reduce=None):
    s = x.shape[0]
    cos_a, sin_a = _rope_tables_a(s)
    cos_b, sa_b, sb_b = _rope_tables_b(s)
    tables = _retention_tables()
    g0, g1, b0, b1 = ln_g[0:1], ln_g[1:2], ln_b[0:1], ln_b[1:2]

    x16 = x.astype(BF16)
    if late is None:
        h_a = _mm(x16, w["a_in"], tn=1536, name="a_in_fwd")
    else:
        flat_b, chip = late
        h_a, got = _mm(x16, w["a_in"], tn=1536, name="a_in_fwd", side=_gather_side(flat_b))
        got = lax.dynamic_update_slice(got, flat_b[None], (chip, 0, 0))
        w = {**w, **_kernel_layout_b(_full_from_gathered(got, B_SHARDS))}
    q_a, k_a, v_a, o_a, u_a, states = _retention_fwd(h_a, cos_a, sin_a, tables)
    y_a = _mm(u_a, w["a_out"], tn=1024, name="a_out_fwd")
    x1, x1_16 = _ln_fwd(x, y_a, g0, b0, ts=ts)

    h1 = _mm(x1_16, w["b_in1"], tn=1152, name="b_in_fwd")
    lat16, kr16, qn16 = _kvq_prep(h1, kv_norm, q_norm, cos_b, sa_b, sb_b, ts=ts)
    k16 = _mm(lat16, w["up_k"], out_dtype=BF16, tn=2048, out_tn=H_B * QK_PAD, extras=(kr16,), epilogue=_assemble_k_store, name="up_k_fwd")
    v16 = _mm(lat16, w["up_v"], out_dtype=BF16, tn=2048, name="up_v_fwd")
    q16 = _mm(qn16, w["uq"], out_dtype=BF16, tn=2048, extras=(cos_b, sa_b, sb_b), epilogue=_rope_q_store, name="uq_fwd")
    o_b, u_b, lse = _attention_fwd(q16, k16, v16, h1, blk=2 * blk, bk=blk, sub=blk // 2, per_trip=8)
    y_b = _mm(u_b, w["b_out"], tn=1024, name="b_out_fwd")

    dz_b, dz_b16, dg1, db1, loss = _ln_loss_bwd(x1, y_b, target, g1, b1, ts=ts)
    d_b_out = _mm(u_b, dz_b16, ta=True, tn=1024, tk=2048, name="b_out_dw")
    du_b = _mm(dz_b16, w["b_out"], tb=True, tn=2048, name="b_out_dx")
    dqf16, dkn, dkr_heads, dv, dgate16 = _attention_bwd(q16, k16, v16, du_b, o_b, h1, lse, cos_b, sa_b, sb_b, blk=2 * blk, bk=blk, per_trip=4)
    d_uq = _mm(qn16, dqf16, ta=True, tm=768, tn=2048, tk=2048, name="uq_dw")
    dqn = _mm(dqf16, w["uq"], tb=True, tn=768, name="uq_dx")
    d_up_k = _mm(lat16, dkn, ta=True, tn=2048, tk=2048, name="up_k_dw")
    d_up_v = _mm(lat16, dv, ta=True, tn=2048, tk=2048, name="up_v_dw")
    dlat_k = _mm(dkn, w["up_k"], tb=True, tn=512, name="up_k_dx")
    dlat_v = _mm(dv, w["up_v"], tb=True, tn=512, name="up_v_dx")
    dh1, dkvn, dqnorm = _h1_bwd(h1, dlat_k, dlat_v, dkr_heads, dqn, dgate16, kv_norm, q_norm, cos_b, sa_b, sb_b, ts=ts)
    d_b_in1 = _mm(x1_16, dh1, ta=True, tn=1152, tk=2048, name="b_in_dw")
    dx1 = _mm(dh1, w["b_in1"], tb=True, tn=1024, extras=(dz_b,), epilogue=_residual_store, name="b_in_dx")

    dz_a, dz_a16, dg0, db0 = _ln_bwd_call(dx1, x, y_a, g0, ts=ts)
    d_a_out = _mm(u_a, dz_a16, ta=True, tn=1024, tk=2048, name="a_out_dw")
    du_a = _mm(dz_a16, w["a_out"], tb=True, tn=2048, name="a_out_dx")
    dh_a = _retention_bwd(q_a, k_a, v_a, states, du_a, o_a, h_a, cos_a, sin_a, tables)
    grads = dict(a_out=d_a_out, b_in1=d_b_in1, uq=d_uq, b_out=d_b_out, up_k=d_up_k, up_v=d_up_v)
    small = dict(ln_g=jnp.concatenate([dg0, dg1], axis=0), ln_b=jnp.concatenate([db0, db1], axis=0),
                 q_norm=dqnorm, kv_norm=dkvn)
    if reduce is None:
        grads["a_in"] = _mm(x16, dh_a, ta=True, tn=1536, tk=1024, name="a_in_dw")
        grad_x = _mm(dh_a, w["a_in"], tb=True, tn=1024, tk=2048, extras=(dz_a,), epilogue=_residual_store, name="a_in_dx")
        return loss, grad_x, grads, small
    own_early, travel_early = reduce(_reference_layout_grads(grads), EARLY_SHARDS, EARLY_ROWS, "early")
    d_a_in, got_early = _mm(x16, dh_a, ta=True, tn=1536, tk=2048, name="a_in_dw", side=_chip_exchange_side(travel_early))
    own_late, travel_late = reduce(dict(a_w_in=d_a_in), LATE_SHARDS, LATE_ROWS, "late")
    grad_x, got_late = _mm(dh_a, w["a_in"], tb=True, tn=1024, tk=3072, extras=(dz_a,), epilogue=_residual_store, name="a_in_dx",
                           side=_chip_exchange_side(travel_late))
    return loss, grad_x, ((own_early, got_early), (own_late, got_late)), small


def _flat_shards(shards, dtype, which, total_rows):
    parts = [shards[name].reshape(rows, FLAT_COLS) for name, rows in which]
    used = sum(rows for _, rows in which)
    parts.append(jnp.zeros((total_rows - used, FLAT_COLS), parts[0].dtype))
    return jnp.concatenate(parts, axis=0).astype(dtype)


def _unflat_shards(flat, shapes, shards):
    out, off = {}, 0
    for name, rows in shards:
        out[name] = flat[off:off + rows].reshape(shapes[name])
        off += rows
    return out


COL_SHARDED = {"a_w_in": (D_MODEL, IN_A), "b_w_in": (D_MODEL, IN_B), "b_w_uq": (Q_LORA, H_B * (QK_NOPE + QK_ROPE)),
               "kv_w_up": (KV_LORA, H_B * (QK_NOPE + V_HEAD))}
ROW_SHARDED = {"a_w_out": (WIDTH_A, D_MODEL), "b_w_out": (WIDTH_B, D_MODEL), "kv_w_down": (D_MODEL, KV_LORA + QK_ROPE)}


def _full_from_gathered(gathered, shards=SHARD_ROWS):
    out, off = {}, 0
    for name, rows in shards:
        part = gathered[:, off:off + rows]
        off += rows
        if name in COL_SHARDED:
            r, c = COL_SHARDED[name]
            out[name] = part.reshape(N_CHIPS, r, c // N_CHIPS).transpose(1, 0, 2).reshape(r, c)
        else:
            r, c = ROW_SHARDED[name]
            out[name] = part.reshape(r, c)
    return out


def _chip_major(g):
    r, c = g.shape
    return g.reshape(r, N_CHIPS, c // N_CHIPS).transpose(1, 0, 2)


def _gathered_from_full(full, shards, total_rows):
    if len(shards) == 1 and shards[0][0] in COL_SHARDED:
        return _chip_major(full[shards[0][0]])
    parts = []
    for name, rows in shards:
        g = full[name]
        if name in COL_SHARDED:
            r, c = COL_SHARDED[name]
            g = g.reshape(r, N_CHIPS, c // N_CHIPS).transpose(1, 0, 2)
        parts.append(g.reshape(N_CHIPS, rows, FLAT_COLS))
    used = sum(rows for _, rows in shards)
    if total_rows > used:
        parts.append(jnp.zeros((N_CHIPS, total_rows - used, FLAT_COLS), F32))
    return jnp.concatenate(parts, axis=1)


A_SHARDS, B_SHARDS = SHARD_ROWS[:1], SHARD_ROWS[1:]
A_ROWS = sum(rows for _, rows in A_SHARDS)


def _kernel_layout_a(full):
    return dict(a_in=full["a_w_in"])


def _kernel_layout_b(full):
    uq = full["b_w_uq"].reshape(Q_LORA, H_B, QK_NOPE + QK_ROPE)
    uq = jnp.pad(uq, ((0, 0), (0, 0), (0, QK_PAD - QK_NOPE - QK_ROPE))).reshape(Q_LORA, H_B * QK_PAD)
    up = full["kv_w_up"].reshape(KV_LORA, H_B, QK_NOPE + V_HEAD)
    down = jnp.pad(full["kv_w_down"], ((0, 0), (0, KV_DOWN_PAD - KV_LORA - QK_ROPE)))
    return dict(a_out=full["a_w_out"], b_out=full["b_w_out"], uq=uq,
                up_k=up[:, :, :QK_NOPE].reshape(KV_LORA, H_B * QK_NOPE),
                up_v=up[:, :, QK_NOPE:].reshape(KV_LORA, H_B * V_HEAD),
                b_in1=jnp.concatenate([down, full["b_w_in"]], axis=1))


def _kernel_layout(full):
    return {**_kernel_layout_a(full), **_kernel_layout_b(full)}


EARLY_SHARDS = tuple(sh for sh in SHARD_ROWS if sh[0] != "a_w_in")
LATE_SHARDS = tuple(sh for sh in SHARD_ROWS if sh[0] == "a_w_in")
EARLY_ROWS, LATE_ROWS = 3072, 1536


def _reference_layout_grads(g):
    uq = g["uq"].reshape(Q_LORA, H_B, QK_PAD)[:, :, :QK_NOPE + QK_ROPE].reshape(Q_LORA, H_B * (QK_NOPE + QK_ROPE))
    up = jnp.concatenate([g["up_k"].reshape(KV_LORA, H_B, QK_NOPE), g["up_v"].reshape(KV_LORA, H_B, V_HEAD)], axis=2)
    return dict(a_w_out=g["a_out"], b_w_out=g["b_out"], b_w_uq=uq,
                kv_w_up=up.reshape(KV_LORA, H_B * (QK_NOPE + V_HEAD)),
                kv_w_down=g["b_in1"][:, :KV_LORA + QK_ROPE], b_w_in=g["b_in1"][:, KV_DOWN_PAD:])


HBM_SPEC = pl.BlockSpec(memory_space=pl.ANY)


def _me():
    return lax.axis_index("x"), lax.axis_index("y"), lax.axis_index("c")


def _chip_flips(x, y):
    return [(1 - x, y), (x, 1 - y), (1 - x, 1 - y)]


def _gather_copies(src_ref, out_ref, send_sems, recv_sems):
    x, y, c = _me()
    half = src_ref.shape[0] // 2
    my_rows = pl.ds(pl.multiple_of(c * half, 16), half)
    their_rows = pl.ds(pl.multiple_of((1 - c) * half, 16), half)
    chips = _chip_flips(x, y)
    sibling = (x, y, 1 - c)

    def copy(k, src, dst, to):
        return pltpu.make_async_remote_copy(src_ref=src, dst_ref=dst, send_sem=send_sems.at[k], recv_sem=recv_sems.at[k],
                                            device_id=to, device_id_type=MESH)

    sends = [copy(k, src_ref.at[my_rows, :], out_ref.at[2 * x + y, my_rows, :], (px, py, c)) for k, (px, py) in enumerate(chips)]
    landed = [out_ref.at[2 * px + py, my_rows, :] for px, py in chips]
    lands = [copy(k, landed[k], landed[k], (px, py, c)) for k, (px, py) in enumerate(chips)]
    forwards = [copy(3 + k, landed[k], landed[k], sibling) for k in range(3)]
    theirs = [out_ref.at[2 * px + py, their_rows, :] for px, py in chips]
    arrivals = [copy(3 + k, theirs[k], theirs[k], sibling) for k in range(3)]
    return sends, lands, forwards, arrivals


def _gather_start(src_ref, out_ref, send_sems, recv_sems):
    sends, _, _, _ = _gather_copies(src_ref, out_ref, send_sems, recv_sems)
    for cp in sends:
        cp.start()


def _gather_finish(src_ref, out_ref, send_sems, recv_sems):
    sends, lands, forwards, arrivals = _gather_copies(src_ref, out_ref, send_sems, recv_sems)
    for k in range(3):
        lands[k].wait_recv()
        forwards[k].start()
    for cp in arrivals:
        cp.wait_recv()
    for cp in sends + forwards:
        cp.wait_send()


def _gather_scratch():
    return [pltpu.SemaphoreType.DMA((6,)), pltpu.SemaphoreType.DMA((6,))]


def _gather_weights(flat16):
    def body(src_ref, out_ref, send_sems, recv_sems):
        _gather_start(src_ref, out_ref, send_sems, recv_sems)
        _gather_finish(src_ref, out_ref, send_sems, recv_sems)

    return pl.pallas_call(
        body, name="gather_weights",
        in_specs=[HBM_SPEC], out_specs=HBM_SPEC,
        out_shape=jax.ShapeDtypeStruct((N_CHIPS,) + flat16.shape, flat16.dtype),
        scratch_shapes=_gather_scratch(),
    )(flat16)


def _gather_side(flat16):
    return dict(inputs=[flat16], out_shape=jax.ShapeDtypeStruct((N_CHIPS,) + flat16.shape, flat16.dtype),
                scratch=_gather_scratch(), start=_gather_start, finish=_gather_finish)


def _chip_exchange_copies(p_ref, out_ref, send_sems, recv_sems):
    x, y, c = _me()
    return [pltpu.make_async_remote_copy(
        src_ref=p_ref.at[2 * px + py], dst_ref=out_ref.at[k], send_sem=send_sems.at[k], recv_sem=recv_sems.at[k],
        device_id=(px, py, c), device_id_type=MESH) for k, (px, py) in enumerate(_chip_flips(x, y))]


def _chip_exchange_start(p_ref, out_ref, send_sems, recv_sems):
    for cp in _chip_exchange_copies(p_ref, out_ref, send_sems, recv_sems):
        cp.start()


def _chip_exchange_finish(p_ref, out_ref, send_sems, recv_sems):
    copies = _chip_exchange_copies(p_ref, out_ref, send_sems, recv_sems)
    for cp in copies:
        cp.wait_send()
    for cp in copies:
        cp.wait_recv()


def _chip_exchange_side(p):
    return dict(inputs=[p], out_shape=jax.ShapeDtypeStruct((3,) + p.shape[1:], p.dtype),
                scratch=[pltpu.SemaphoreType.DMA((3,)), pltpu.SemaphoreType.DMA((3,))],
                start=_chip_exchange_start, finish=_chip_exchange_finish)


def _pair_swap(r, name):
    def body(r_ref, out_ref, send_sem, recv_sem):
        x, y, c = _me()
        cp = pltpu.make_async_remote_copy(src_ref=r_ref, dst_ref=out_ref, send_sem=send_sem, recv_sem=recv_sem,
                                          device_id=(x, y, 1 - c), device_id_type=MESH)
        cp.start()
        cp.wait_send()
        cp.wait_recv()

    return pl.pallas_call(
        body, name=name,
        in_specs=[HBM_SPEC], out_specs=HBM_SPEC,
        out_shape=jax.ShapeDtypeStruct(r.shape, r.dtype),
        scratch_shapes=[pltpu.SemaphoreType.DMA, pltpu.SemaphoreType.DMA],
    )(r)


def _sum_small(vec):
    def body(v_ref, out_ref, slots, send_sems, recv_sems):
        x, y, c = _me()
        me = 4 * x + 2 * y + c
        slots[me] = v_ref[...]
        flips = [(fx, fy, fc) for fx in (0, 1) for fy in (0, 1) for fc in (0, 1)][1:]
        copies = []
        for k, (fx, fy, fc) in enumerate(flips):
            copies.append(pltpu.make_async_remote_copy(
                src_ref=v_ref, dst_ref=slots.at[me], send_sem=send_sems.at[k], recv_sem=recv_sems.at[k],
                device_id=(x ^ fx, y ^ fy, c ^ fc), device_id_type=MESH))
        for cp in copies:
            cp.start()
        for cp in copies:
            cp.wait_send()
        for k, (fx, fy, fc) in enumerate(flips):
            src = 4 * (x ^ fx) + 2 * (y ^ fy) + (c ^ fc)
            pltpu.make_async_remote_copy(
                src_ref=v_ref, dst_ref=slots.at[src], send_sem=send_sems.at[k], recv_sem=recv_sems.at[k],
                device_id=(x ^ fx, y ^ fy, c ^ fc), device_id_type=MESH).wait_recv()
        total = slots[0]
        for d in range(1, N_DEV):
            total = total + slots[d]
        out_ref[...] = total

    return pl.pallas_call(
        body, name="sum_small",
        in_specs=[pl.BlockSpec(memory_space=pltpu.VMEM)], out_specs=pl.BlockSpec(memory_space=pltpu.VMEM),
        out_shape=jax.ShapeDtypeStruct(vec.shape, vec.dtype),
        scratch_shapes=[pltpu.VMEM((N_DEV,) + vec.shape, vec.dtype), pltpu.SemaphoreType.DMA((7,)),
                        pltpu.SemaphoreType.DMA((7,))],
    )(vec)


UPD_ROWS = 512


def _pair_sum(g, theirs, core, chip, name):
    half, cols = theirs.shape[1:]
    nb = half // UPD_ROWS

    def body(core_ref, chip_ref, g_ref, t_ref, own_ref, o16_ref):
        total = g_ref[0] + t_ref[0].astype(F32)
        o16_ref[0] = total.astype(o16_ref.dtype)

        @pl.when(pl.program_id(1) == chip_ref[0])
        def _():
            own_ref[...] = total

    return pl.pallas_call(
        body, name=name,
        grid_spec=pltpu.PrefetchScalarGridSpec(
            num_scalar_prefetch=2, grid=(nb, N_CHIPS),
            in_specs=[pl.BlockSpec((1, UPD_ROWS, cols), lambda i, d, core_ref, chip_ref: (d, core_ref[0] * nb + i, 0)),
                      pl.BlockSpec((1, UPD_ROWS, cols), lambda i, d, core_ref, chip_ref: (d, i, 0))],
            out_specs=[pl.BlockSpec((UPD_ROWS, cols), lambda i, d, core_ref, chip_ref: (i, 0)),
                       pl.BlockSpec((1, UPD_ROWS, cols), lambda i, d, core_ref, chip_ref: (d, i, 0))]),
        out_shape=[jax.ShapeDtypeStruct((half, cols), F32),
                   jax.ShapeDtypeStruct((N_CHIPS, half, cols), BF16)],
        compiler_params=_params(("parallel", "arbitrary")),
    )(core, chip, g, theirs)


def _chip_sum(own, received, name):
    half, cols = own.shape
    nb = half // UPD_ROWS

    def body(p_ref, r_ref, o_ref):
        o_ref[...] = ((p_ref[...] + r_ref[0].astype(F32)) + r_ref[1].astype(F32)) + r_ref[2].astype(F32)

    return pl.pallas_call(
        body, name=name, grid=(nb,),
        in_specs=[pl.BlockSpec((UPD_ROWS, cols), lambda i: (i, 0)),
                  pl.BlockSpec((3, UPD_ROWS, cols), lambda i: (0, i, 0))],
        out_specs=pl.BlockSpec((UPD_ROWS, cols), lambda i: (i, 0)),
        out_shape=jax.ShapeDtypeStruct((half, cols), F32),
        compiler_params=_params(("parallel",)),
    )(own, received)


def _adamw(w, g, m, v, *, rows, name):
    r, c = w.shape
    rows = min(rows, r)
    assert r % rows == 0

    def body(w_ref, g_ref, m_ref, v_ref, d_ref, nm_ref, nv_ref):
        gv = g_ref[...]
        nm = ADAM_B1 * m_ref[...] + (1.0 - ADAM_B1) * gv
        nv = ADAM_B2 * v_ref[...] + (1.0 - ADAM_B2) * (gv * gv)
        m_hat = nm / (1.0 - ADAM_B1 ** ADAM_STEP)
        v_hat = nv / (1.0 - ADAM_B2 ** ADAM_STEP)
        d_ref[...] = -ADAM_LR * (m_hat / (jnp.sqrt(v_hat) + ADAM_EPS) + ADAM_WD * w_ref[...])
        nm_ref[...] = nm
        nv_ref[...] = nv

    spec = pl.BlockSpec((rows, c), lambda i: (i, 0))
    return pl.pallas_call(
        body, name=name, grid=(r // rows,),
        in_specs=[spec] * 4, out_specs=[spec] * 3,
        out_shape=[jax.ShapeDtypeStruct((r, c), F32)] * 3,
        compiler_params=_params(("parallel",)),
    )(w, g, m, v)


W_NAMES = ("a_w_in", "a_w_out", "b_w_in", "b_q_norm", "b_w_uq", "b_w_out", "kv_w_down", "kv_norm", "kv_w_up", "ln_g", "ln_b")
BIG = tuple(name for name, _ in SHARD_ROWS)


def _pack_small(ln_g, ln_b, q_norm, kv_norm, extra=None):
    pad = lambda a: jnp.pad(a.reshape(1, -1), ((0, 0), (0, FLAT_COLS - a.size)))
    rows = [ln_g, ln_b, pad(q_norm), pad(kv_norm),
            jnp.zeros((1, FLAT_COLS), F32) if extra is None else pad(extra), jnp.zeros((1, FLAT_COLS), F32)]
    return jnp.concatenate(rows, axis=0)


def _unpack_small(p):
    return dict(ln_g=p[0:2], ln_b=p[2:4], b_q_norm=p[4:5, :Q_LORA], kv_norm=p[5, :KV_LORA])


def kernel(x, a_w_in, a_w_out, b_w_in, b_q_norm, b_w_uq, b_w_out, kv_w_down, kv_norm, kv_w_up, ln_g, ln_b, loss_target, m_a_w_in, m_a_w_out, m_b_w_in, m_b_q_norm, m_b_w_uq, m_b_w_out, m_kv_w_down, m_kv_norm, m_kv_w_up, m_ln_g, m_ln_b, v_a_w_in, v_a_w_out, v_b_w_in, v_b_q_norm, v_b_w_uq, v_b_w_out, v_kv_w_down, v_kv_norm, v_kv_w_up, v_ln_g, v_ln_b):
    w_in = dict(a_w_in=a_w_in[0], a_w_out=a_w_out[0], b_w_in=b_w_in[0], b_w_uq=b_w_uq[0], b_w_out=b_w_out[0],
                kv_w_down=kv_w_down, kv_w_up=kv_w_up)
    m_in = dict(a_w_in=m_a_w_in[0], a_w_out=m_a_w_out[0], b_w_in=m_b_w_in[0], b_w_uq=m_b_w_uq[0], b_w_out=m_b_w_out[0],
                kv_w_down=m_kv_w_down, kv_w_up=m_kv_w_up)
    v_in = dict(a_w_in=v_a_w_in[0], a_w_out=v_a_w_out[0], b_w_in=v_b_w_in[0], b_w_uq=v_b_w_uq[0], b_w_out=v_b_w_out[0],
                kv_w_down=v_kv_w_down, kv_w_up=v_kv_w_up)
    shard_shapes = {name: w_in[name].shape for name in BIG}

    cx, cy, cc = lax.axis_index("x"), lax.axis_index("y"), lax.axis_index("c")
    chip = 2 * cx + cy
    flat_b = _flat_shards(w_in, BF16, B_SHARDS, FLAT_ROWS - A_ROWS)
    a16 = w_in["a_w_in"].astype(BF16)
    got_a = lax.dynamic_update_slice(_gather_weights(a16), a16[None], (chip, 0, 0))
    weights_a = dict(a_in=got_a.transpose(1, 0, 2).reshape(D_MODEL, IN_A))

    core_arr, chip_arr = cc.astype(jnp.int32).reshape(1), chip.astype(jnp.int32).reshape(1)

    def reduce_pair(full, shards, rows, tag):
        g_all = _gathered_from_full(full, shards, rows)
        half, cols = g_all.shape[1] // 2, g_all.shape[2]
        other_half = lax.dynamic_slice(g_all, (0, (1 - cc) * half, 0), (N_CHIPS, half, cols)).astype(BF16)
        theirs = _pair_swap(other_half, "pair_exchange_" + tag)
        return _pair_sum(g_all, theirs, core_arr, chip_arr, "pair_sum_" + tag)

    loss, grad_x, reduced, small = _local_step(x[0], loss_target[0], weights_a, kv_norm.reshape(1, -1), b_q_norm, ln_g, ln_b,
                                               late=(flat_b, chip), reduce=reduce_pair)

    g_big = {}
    for (own, received), shards, tag in zip(reduced, (EARLY_SHARDS, LATE_SHARDS), ("early", "late")):
        mine = _chip_sum(own, received, "chip_sum_" + tag)
        sibling = _pair_swap(mine, "pair_share_" + tag)
        g_flat = jnp.concatenate([jnp.where(cc == 0, mine, sibling), jnp.where(cc == 0, sibling, mine)], axis=0)
        if g_flat.shape == shard_shapes[shards[0][0]]:
            g_big[shards[0][0]] = g_flat
        else:
            g_big.update(_unflat_shards(g_flat, shard_shapes, shards))

    small_sum = _sum_small(_pack_small(small["ln_g"], small["ln_b"], small["q_norm"], small["kv_norm"], loss[:, :1]))
    loss_out = small_sum[6, 0]

    row = lambda a: a.reshape(1, -1)
    g_all = {**g_big, **_unpack_small(small_sum)}
    g_all["kv_norm"] = row(g_all["kv_norm"])
    state = {**{name: (w_in[name], m_in[name], v_in[name]) for name in BIG},
             "ln_g": (ln_g, m_ln_g, v_ln_g), "ln_b": (ln_b, m_ln_b, v_ln_b), "b_q_norm": (b_q_norm, m_b_q_norm, v_b_q_norm),
             "kv_norm": (row(kv_norm), row(m_kv_norm), row(v_kv_norm))}
    upd = {name: _adamw(state[name][0], g_all[name], state[name][1], state[name][2], rows=256, name="adamw_" + name)
           for name in W_NAMES}

    def shaped(name, a):
        if name == "kv_norm":
            return a.reshape(-1)
        return a[None] if name in ("a_w_in", "a_w_out", "b_w_in", "b_w_uq", "b_w_out") else a

    outputs = [[shaped(name, g_all[name]) for name in W_NAMES]]
    outputs += [[shaped(name, upd[name][k]) for name in W_NAMES] for k in range(3)]
    return (loss_out, grad_x[None], *outputs[0], *outputs[1], *outputs[2], *outputs[3])
```
